```python
import math
import jax
import jax.numpy as jnp
from jax import lax
import numpy as np

D_MODEL = 1024
BATCH = 32
SEQ = 256
DEPTH = 2
DEC_BATCH = 8
DEC_SEQ = 2048
PAST_LEN = 256

GRID_W = 64
N_BRANCH = 4
BRANCH_W = 256
Q_BLOCK = 128
ROPE_BASE = 10000.0
EPS = 1e-6
NEG_INF = -1e30
A_HEADS = 4
A_KV_HEADS = 2
A_HEAD_DIM = 64
WINDOW = 128
S5_GROUP = 16
S5_GROUPS = BRANCH_W // S5_GROUP
S5_STATE = 64
S5_STEP_MIN = 1e-3
S5_STEP_MAX = 1e-1
LRU_WIDTH = BRANCH_W
LRU_BLOCKS = 4
LRU_BLOCK_W = LRU_WIDTH // LRU_BLOCKS
LRU_CONV = 4
LRU_C = 8.0
MLA_HEADS = 4
MLA_Q_LORA = 256
MLA_KV_LORA = 128
MLA_NOPE = 64
MLA_ROPE = 32
MLA_V = 64
MLA_QK = MLA_NOPE + MLA_ROPE
N_GROUPS = 4
EXPERTS_PER_GROUP = 4
N_EXPERTS = N_GROUPS * EXPERTS_PER_GROUP
TOP_K = 2
EXPERT_FF = 256
IN_SIZES = (A_HEADS * A_HEAD_DIM, A_KV_HEADS * A_HEAD_DIM, A_KV_HEADS * A_HEAD_DIM, BRANCH_W, LRU_WIDTH, LRU_WIDTH,
            MLA_Q_LORA, MLA_KV_LORA, MLA_ROPE, N_BRANCH * D_MODEL)
IN_COLS = sum(IN_SIZES)

kernel_name = 'hybrid_diffusion_prefix_trunk_step'

F32 = jnp.float32


def rmsnorm(x, g):
    xf = x.astype(F32)
    y = xf * lax.rsqrt(jnp.mean(xf * xf, axis=-1, keepdims=True) + EPS)
    return (y * g.astype(F32)).astype(x.dtype)


def rope_tables(seq_len, dim, dtype):
    rows = seq_len // GRID_W
    r, col = jnp.meshgrid(jnp.arange(rows), jnp.arange(GRID_W), indexing='ij')
    r = r.reshape(-1).astype(F32)
    col = col.reshape(-1).astype(F32)
    quarter = dim // 4
    freqs = ROPE_BASE ** (-jnp.arange(quarter, dtype=F32) / quarter)
    ang_r = r[:, None] * freqs
    ang_c = col[:, None] * freqs
    ang = jnp.concatenate([ang_r, ang_r, ang_c, ang_c], axis=-1)
    return jnp.cos(ang).astype(dtype), jnp.sin(ang).astype(dtype)


def apply_rope(x, cos, sin):
    x1, x2, x3, x4 = jnp.split(x, 4, axis=-1)
    rot = jnp.concatenate([-x2, x1, -x4, x3], axis=-1)
    return x * cos[:, None, :] + rot * sin[:, None, :]


def rope_tail(x, cos, sin):
    return jnp.concatenate([x[..., :MLA_NOPE], apply_rope(x[..., MLA_NOPE:], cos, sin)], axis=-1)


def attn_probs(s, sink):
    if sink is None:
        return jax.nn.softmax(s, axis=-1)
    s = jnp.concatenate([s, jnp.broadcast_to(sink, s.shape[:-1] + (1,))], axis=-1)
    return jax.nn.softmax(s, axis=-1)[..., :-1]


def dense_attention(q, k, v, sink, scale):
    B, Lq, H, D = q.shape
    KVH = k.shape[2]
    G = H // KVH
    nb = Lq // Q_BLOCK
    sink_b = None if sink is None else sink.astype(F32).reshape(1, KVH, G, 1, 1)
    qb = q.reshape(B, nb, Q_BLOCK, KVH, G, D).swapaxes(0, 1)

    def one(qblk):
        s = jnp.einsum('bqkgd,bskd->bkgqs', qblk, k).astype(F32) * scale
        pr = attn_probs(s, sink_b).astype(v.dtype)
        return jnp.einsum('bkgqs,bskd->bqkgd', pr, v)

    o = lax.map(one, qb)
    return o.swapaxes(0, 1).reshape(B, Lq, H * v.shape[-1])


def windowed_attention(q, k, v, kc, vc, sink, scale):
    B, L, H, D = q.shape
    KVH = k.shape[2]
    G = H // KVH
    nb = L // Q_BLOCK
    n_ctx = kc.shape[1]
    sink_b = sink.astype(F32).reshape(1, KVH, G, 1, 1)

    def windows(t):
        tp = jnp.pad(t, ((0, 0), (Q_BLOCK, Q_BLOCK), (0, 0), (0, 0))).reshape(B, nb + 2, Q_BLOCK, KVH, t.shape[-1])
        return jnp.concatenate([tp[:, :-2], tp[:, 1:-1], tp[:, 2:]], axis=2).swapaxes(0, 1)

    qpos = jnp.arange(nb)[:, None] * Q_BLOCK + jnp.arange(Q_BLOCK)[None, :]
    kpos = (jnp.arange(nb)[:, None] - 1) * Q_BLOCK + jnp.arange(3 * Q_BLOCK)[None, :]
    rel = kpos[:, None, :] - qpos[:, :, None]
    valid = (jnp.abs(rel) <= WINDOW) & (kpos[:, None, :] >= 0) & (kpos[:, None, :] < L)
    qb = q.reshape(B, nb, Q_BLOCK, KVH, G, D).swapaxes(0, 1)

    def one(args):
        qblk, kblk, vblk, vmask = args
        s_ctx = jnp.einsum('bqkgd,bskd->bkgqs', qblk, kc).astype(F32) * scale
        s_win = jnp.einsum('bqkgd,bskd->bkgqs', qblk, kblk).astype(F32) * scale
        s_win = jnp.where(vmask, s_win, NEG_INF)
        pr = attn_probs(jnp.concatenate([s_ctx, s_win], axis=-1), sink_b).astype(v.dtype)
        return (jnp.einsum('bkgqs,bskd->bqkgd', pr[..., :n_ctx], vc)
                + jnp.einsum('bkgqs,bskd->bqkgd', pr[..., n_ctx:], vblk))

    o = lax.map(one, (qb, windows(k), windows(v), valid))
    return o.swapaxes(0, 1).reshape(B, L, H * v.shape[-1])


def complex_scan(a_re, a_im, b_re, b_im, h0_re, h0_im, reverse):
    if reverse:
        b_re, b_im = jnp.flip(b_re, 1), jnp.flip(b_im, 1)
    b_re = b_re.at[:, 0].add(a_re * h0_re - a_im * h0_im)
    b_im = b_im.at[:, 0].add(a_re * h0_im + a_im * h0_re)
    ar = jnp.broadcast_to(a_re, b_re.shape)
    ai = jnp.broadcast_to(a_im, b_re.shape)

    def comb(e1, e2):
        a1r, a1i, b1r, b1i = e1
        a2r, a2i, b2r, b2i = e2
        return (a2r * a1r - a2i * a1i, a2r * a1i + a2i * a1r,
                a2r * b1r - a2i * b1i + b2r, a2r * b1i + a2i * b1r + b2i)

    _, _, hr, hi = lax.associative_scan(comb, (ar, ai, b_re, b_im), axis=1)
    fr, fi = hr[:, -1], hi[:, -1]
    if reverse:
        hr, hi = jnp.flip(hr, 1), jnp.flip(hi, 1)
    return hr, hi, fr, fi


def real_scan(a, b, h0, reverse):
    if reverse:
        a, b = jnp.flip(a, 1), jnp.flip(b, 1)
    b = b.at[:, 0].add(a[:, 0] * h0)

    def comb(e1, e2):
        return (e1[0] * e2[0], e2[0] * e1[1] + e2[1])

    _, h = lax.associative_scan(comb, (a, b), axis=1)
    final = h[:, -1]
    if reverse:
        h = jnp.flip(h, 1)
    return h, final


def s5_mixer(u, p, h0_re, h0_im):
    B, L, _ = u.shape
    uf = u.astype(F32)
    ug = uf.reshape(B, L, S5_GROUPS, S5_GROUP)
    y = p['s5_d'].astype(F32) * uf
    fin_r, fin_i = [], []
    for d in range(2):
        lre = p['s5_lambda_re'][d].astype(F32)
        lim = p['s5_lambda_im'][d].astype(F32)
        dt = jnp.exp(p['s5_log_step'][d].astype(F32))[:, None]
        mag = jnp.exp(lre * dt)
        ar, ai = mag * jnp.cos(lim * dt), mag * jnp.sin(lim * dt)
        den = lre * lre + lim * lim
        fr = ((ar - 1.0) * lre + ai * lim) / den
        fi = (ai * lre - (ar - 1.0) * lim) / den
        br, bi = p['s5_b_re'][d].astype(F32), p['s5_b_im'][d].astype(F32)
        bbr = fr[..., None] * br - fi[..., None] * bi
        bbi = fr[..., None] * bi + fi[..., None] * br
        bur = jnp.einsum('blgc,gpc->blgp', ug, bbr)
        bui = jnp.einsum('blgc,gpc->blgp', ug, bbi)
        hr, hi, sr, si = complex_scan(ar, ai, bur, bui, h0_re[:, d].astype(F32), h0_im[:, d].astype(F32), d == 1)
        yd = (jnp.einsum('blgp,gcp->blgc', hr, p['s5_c_re'][d].astype(F32))
              - jnp.einsum('blgp,gcp->blgc', hi, p['s5_c_im'][d].astype(F32)))
        y = y + yd.reshape(B, L, BRANCH_W)
        fin_r.append(sr)
        fin_i.append(si)
    y = jax.nn.gelu(y).astype(u.dtype)
    gv = y @ p['s5_w_glu']
    out = gv[..., :BRANCH_W] * jax.nn.sigmoid(gv[..., BRANCH_W:])
    return out, jnp.stack(fin_r, axis=1).astype(u.dtype), jnp.stack(fin_i, axis=1).astype(u.dtype)


def centred_dwconv(x, w, b):
    L = x.shape[1]
    left = LRU_CONV // 2
    xp = jnp.pad(x, ((0, 0), (left, LRU_CONV - 1 - left), (0, 0)))
    y = b
    for t in range(LRU_CONV):
        y = y + xp[:, t:t + L] * w[t]
    return y


def lru_mixer(xp, gp, p, h0):
    B, L, _ = xp.shape
    xc = centred_dwconv(xp, p['lru_conv_w'], p['lru_conv_b'])
    xb = xc.reshape(B, L, LRU_BLOCKS, LRU_BLOCK_W)
    xcf = xc.astype(F32)
    hs, finals = [], []
    for d in range(2):
        r = jax.nn.sigmoid((jnp.einsum('blnc,ncd->blnd', xb, p['lru_w_a'][d]).reshape(B, L, LRU_WIDTH)
                            + p['lru_b_a'][d]).astype(F32))
        i = jax.nn.sigmoid((jnp.einsum('blnc,ncd->blnd', xb, p['lru_w_x'][d]).reshape(B, L, LRU_WIDTH)
                            + p['lru_b_x'][d]).astype(F32))
        log_a = -LRU_C * r * jax.nn.softplus(-p['lru_lambda'][d].astype(F32))
        a = jnp.exp(log_a)
        bx = jnp.sqrt(-jnp.expm1(2.0 * log_a)) * (i * xcf)
        h, hT = real_scan(a, bx, h0[:, d].astype(F32), d == 1)
        hs.append(h)
        finals.append(hT)
    y = (hs[0] + hs[1]).astype(xp.dtype) * jax.nn.gelu(gp)
    return y, jnp.stack(finals, axis=1).astype(xp.dtype)


def attn_heads(aq, ak, av, p):
    B, L, _ = aq.shape
    q = rmsnorm(aq.reshape(B, L, A_HEADS, A_HEAD_DIM), p['a_qnorm_g'])
    k = rmsnorm(ak.reshape(B, L, A_KV_HEADS, A_HEAD_DIM), p['a_knorm_g'])
    v = av.reshape(B, L, A_KV_HEADS, A_HEAD_DIM)
    return q, k, v


def mla_queries(q_lat, p):
    B, L, _ = q_lat.shape
    q = rmsnorm(q_lat, p['mla_q_lat_norm']) @ p['mla_w_uq']
    return rmsnorm(q.reshape(B, L, MLA_HEADS, MLA_QK), p['mla_qnorm_g'])


def mla_keys_values(ckv_n, k_rope, p):
    B, L, _ = ckv_n.shape
    kv = (ckv_n @ p['mla_w_ukv']).reshape(B, L, MLA_HEADS, MLA_NOPE + MLA_V)
    k = jnp.concatenate([kv[..., :MLA_NOPE],
                         jnp.broadcast_to(k_rope[:, :, None, :], (B, L, MLA_HEADS, MLA_ROPE))], axis=-1)
    return rmsnorm(k, p['mla_knorm_g']), kv[..., MLA_NOPE:]


def split_in(z):
    offsets = [int(o) for o in np.cumsum(IN_SIZES)[:-1]]
    return jnp.split(z, offsets, axis=-1)


def merge_branches(branches, gates, p):
    B, L, _ = gates.shape
    o = jnp.stack(branches, axis=2)
    proj = jnp.einsum('blkc,kcd->blkd', o, p['w_branch'])
    g = jax.nn.sigmoid(gates.reshape(B, L, N_BRANCH, D_MODEL))
    return jnp.sum(g * proj, axis=2) @ p['w_out']


def mix_context(h, p):
    B, L, _ = h.shape
    aq, ak, av, s5u, lx, lg, mq, mkv, mkr, gates = split_in(h @ p['w_in'])
    q, k, v = attn_heads(aq, ak, av, p)
    o_a = dense_attention(q, k, v, p['a_sink'], A_HEAD_DIM ** -0.5)
    zs = jnp.zeros((B, 2, S5_GROUPS, S5_STATE), h.dtype)
    o_b, s5_re, s5_im = s5_mixer(s5u, p, zs, zs)
    o_c, lru_fin = lru_mixer(lx, lg, p, jnp.zeros((B, 2, LRU_WIDTH), h.dtype))
    ckv = rmsnorm(mkv, p['mla_kv_norm'])
    km, vm = mla_keys_values(ckv, mkr, p)
    o_d = dense_attention(mla_queries(mq, p), km, vm, None, MLA_QK ** -0.5)
    out = merge_branches((o_a, o_b, o_c, o_d), gates, p)
    return out, (k, v, ckv, mkr, s5_re, s5_im, lru_fin)


def mix_latent(h, ck, cv, cckv, ckr, s5_re0, s5_im0, lru0, p):
    B, L, _ = h.shape
    aq, ak, av, s5u, lx, lg, mq, mkv, mkr, gates = split_in(h @ p['w_in'])
    q, k, v = attn_heads(aq, ak, av, p)
    cos_a, sin_a = rope_tables(L, A_HEAD_DIM, h.dtype)
    o_a = windowed_attention(apply_rope(q, cos_a, sin_a), apply_rope(k, cos_a, sin_a), v, ck, cv,
                             p['a_sink'], A_HEAD_DIM ** -0.5)
    o_b, _, _ = s5_mixer(s5u, p, s5_re0, s5_im0)
    o_c, _ = lru_mixer(lx, lg, p, lru0)
    cos_m, sin_m = rope_tables(L, MLA_ROPE, h.dtype)
    km, vm = mla_keys_values(rmsnorm(mkv, p['mla_kv_norm']), mkr, p)
    kc, vc = mla_keys_values(cckv, ckr, p)
    qm = rope_tail(mla_queries(mq, p), cos_m, sin_m)
    km = rope_tail(km, cos_m, sin_m)
    o_d = dense_attention(qm, jnp.concatenate([kc, km], axis=1), jnp.concatenate([vc, vm], axis=1),
                          None, MLA_QK ** -0.5)
    return merge_branches((o_a, o_b, o_c, o_d), gates, p)


def hier_moe(h, p):
    gl = (h @ p['moe_w_group'] + p['moe_b_group']).astype(F32)
    pg = jax.nn.softmax(gl, axis=-1)
    g_idx = jnp.argmax(gl, axis=-1)
    g_sel = jax.nn.one_hot(g_idx, N_GROUPS, dtype=F32)
    pg_sel = jnp.sum(pg * g_sel, axis=-1)
    el = (h @ p['moe_w_expert'] + p['moe_b_expert']).astype(F32)
    el = el.reshape(h.shape[0], h.shape[1], N_GROUPS, EXPERTS_PER_GROUP)
    el_g = jnp.einsum('blge,blg->ble', el, g_sel)
    top_v, top_i = lax.top_k(jax.nn.softmax(el_g, axis=-1), TOP_K)
    w = pg_sel[..., None] * top_v / jnp.sum(top_v, axis=-1, keepdims=True)
    eid = g_idx[..., None] * EXPERTS_PER_GROUP + top_i
    gate = jnp.einsum('blk,blke->ble', w, jax.nn.one_hot(eid, N_EXPERTS, dtype=F32)).astype(h.dtype)
    a = jnp.einsum('bld,edf->blef', h, p['moe_w_gate'])
    u = jnp.einsum('bld,edf->blef', h, p['moe_w_up'])
    act = jax.nn.silu(a) * u * gate[..., None]
    return jnp.einsum('blef,efd->bld', act, p['moe_w_down'])


def modulation(cond, p):
    m = jax.nn.silu(cond) @ p['w_mod'] + p['b_mod']
    return jnp.split(m, 6, axis=-1)


def context_layer(x, c_ctx, p):
    sh1, sc1, g1, sh2, sc2, g2 = modulation(c_ctx[None, None, :], p)
    h = rmsnorm(x, p['norm1_g']) * (1.0 + sc1) + sh1
    mo, ctx_t = mix_context(h, p)
    x = x + g1 * mo
    h = rmsnorm(x, p['norm2_g']) * (1.0 + sc2) + sh2
    x = x + g2 * hier_moe(h, p)
    return x, ctx_t


def latent_layer(x, c, ck, cv, cckv, ckr, s5r, s5i, lru0, p):
    sh1, sc1, g1, sh2, sc2, g2 = modulation(c[:, None, :], p)
    h = rmsnorm(x, p['norm1_g']) * (1.0 + sc1) + sh1
    x = x + g1 * mix_latent(h, ck, cv, cckv, ckr, s5r, s5i, lru0, p)
    h = rmsnorm(x, p['norm2_g']) * (1.0 + sc2) + sh2
    return x + g2 * hier_moe(h, p)


def setup_inputs(seed: int = 0) -> dict:
    key = jax.random.key(seed)
    ks = iter(jax.random.split(key, 64))
    D = D_MODEL

    def nrm(shape, scale):
        return scale * jax.random.normal(next(ks), shape, F32)

    inp = {}
    inp['x_prompt'] = nrm((BATCH, SEQ, D), 1.0)
    inp['x_sample'] = nrm((DEC_BATCH, DEC_SEQ, D), 1.0)
    inp['cache_attn_k'] = nrm((DEC_BATCH, DEPTH, PAST_LEN, A_KV_HEADS, A_HEAD_DIM), 1.0)
    inp['cache_attn_v'] = nrm((DEC_BATCH, DEPTH, PAST_LEN, A_KV_HEADS, A_HEAD_DIM), 1.0)
    inp['cache_mla_ckv'] = nrm((DEC_BATCH, DEPTH, PAST_LEN, MLA_KV_LORA), 1.0)
    inp['cache_mla_krope'] = nrm((DEC_BATCH, DEPTH, PAST_LEN, MLA_ROPE), 1.0)
    inp['state_ssm_re'] = nrm((DEC_BATCH, DEPTH, 2, S5_GROUPS, S5_STATE), 0.5)
    inp['state_ssm_im'] = nrm((DEC_BATCH, DEPTH, 2, S5_GROUPS, S5_STATE), 0.5)
    inp['state_lru'] = nrm((DEC_BATCH, DEPTH, 2, LRU_WIDTH), 0.5)
    inp['c'] = nrm((DEC_BATCH, D), 1.0)
    inp['c_ctx'] = nrm((D,), 1.0)
    inp['w_mod'] = nrm((DEPTH, D, 6 * D), 0.5 * D ** -0.5)
    inp['b_mod'] = nrm((DEPTH, 6 * D), 0.01)
    inp['norm1_g'] = 1.0 + nrm((DEPTH, D), 0.02)
    inp['norm2_g'] = 1.0 + nrm((DEPTH, D), 0.02)
    inp['w_in'] = nrm((DEPTH, D, IN_COLS), D ** -0.5)
    inp['a_qnorm_g'] = 1.0 + nrm((DEPTH, A_HEAD_DIM), 0.02)
    inp['a_knorm_g'] = 1.0 + nrm((DEPTH, A_HEAD_DIM), 0.02)
    inp['a_sink'] = nrm((DEPTH, A_HEADS), 0.5)
    inp['s5_lambda_re'] = -0.5 + nrm((DEPTH, 2, S5_GROUPS, S5_STATE), 0.01)
    inp['s5_lambda_im'] = math.pi * jnp.arange(S5_STATE, dtype=F32) + nrm((DEPTH, 2, S5_GROUPS, S5_STATE), 0.01)
    inp['s5_log_step'] = jax.random.uniform(next(ks), (DEPTH, 2, S5_GROUPS), F32,
                                            math.log(S5_STEP_MIN), math.log(S5_STEP_MAX))
    inp['s5_b_re'] = nrm((DEPTH, 2, S5_GROUPS, S5_STATE, S5_GROUP), (2 * S5_GROUP) ** -0.5)
    inp['s5_b_im'] = nrm((DEPTH, 2, S5_GROUPS, S5_STATE, S5_GROUP), (2 * S5_GROUP) ** -0.5)
    inp['s5_c_re'] = nrm((DEPTH, 2, S5_GROUPS, S5_GROUP, S5_STATE), (2 * S5_STATE) ** -0.5)
    inp['s5_c_im'] = nrm((DEPTH, 2, S5_GROUPS, S5_GROUP, S5_STATE), (2 * S5_STATE) ** -0.5)
    inp['s5_d'] = nrm((DEPTH, BRANCH_W), 1.0)
    inp['s5_w_glu'] = nrm((DEPTH, BRANCH_W, 2 * BRANCH_W), BRANCH_W ** -0.5)
    inp['lru_conv_w'] = nrm((DEPTH, LRU_CONV, LRU_WIDTH), LRU_CONV ** -0.5)
    inp['lru_conv_b'] = nrm((DEPTH, LRU_WIDTH), 0.01)
    inp['lru_w_a'] = nrm((DEPTH, 2, LRU_BLOCKS, LRU_BLOCK_W, LRU_BLOCK_W), LRU_BLOCK_W ** -0.5)
    inp['lru_b_a'] = nrm((DEPTH, 2, LRU_WIDTH), 0.01)
    inp['lru_w_x'] = nrm((DEPTH, 2, LRU_BLOCKS, LRU_BLOCK_W, LRU_BLOCK_W), LRU_BLOCK_W ** -0.5)
    inp['lru_b_x'] = nrm((DEPTH, 2, LRU_WIDTH), 0.01)
    ua = jax.random.uniform(next(ks), (DEPTH, 2, LRU_WIDTH), F32, 0.9, 0.999)
    sa = ua ** (1.0 / LRU_C)
    inp['lru_lambda'] = jnp.log(sa) - jnp.log1p(-sa)
    inp['mla_q_lat_norm'] = 1.0 + nrm((DEPTH, MLA_Q_LORA), 0.02)
    inp['mla_w_uq'] = nrm((DEPTH, MLA_Q_LORA, MLA_HEADS * MLA_QK), MLA_Q_LORA ** -0.5)
    inp['mla_kv_norm'] = 1.0 + nrm((DEPTH, MLA_KV_LORA), 0.02)
    inp['mla_w_ukv'] = nrm((DEPTH, MLA_KV_LORA, MLA_HEADS * (MLA_NOPE + MLA_V)), MLA_KV_LORA ** -0.5)
    inp['mla_qnorm_g'] = 1.0 + nrm((DEPTH, MLA_QK), 0.02)
    inp['mla_knorm_g'] = 1.0 + nrm((DEPTH, MLA_QK), 0.02)
    inp['w_branch'] = nrm((DEPTH, N_BRANCH, BRANCH_W, D), BRANCH_W ** -0.5)
    inp['w_out'] = nrm((DEPTH, D, D), D ** -0.5)
    inp['moe_w_group'] = nrm((DEPTH, D, N_GROUPS), D ** -0.5)
    inp['moe_b_group'] = nrm((DEPTH, N_GROUPS), 0.01)
    inp['moe_w_expert'] = nrm((DEPTH, D, N_EXPERTS), D ** -0.5)
    inp['moe_b_expert'] = nrm((DEPTH, N_EXPERTS), 0.01)
    inp['moe_w_gate'] = nrm((DEPTH, N_EXPERTS, D, EXPERT_FF), D ** -0.5)
    inp['moe_w_up'] = nrm((DEPTH, N_EXPERTS, D, EXPERT_FF), D ** -0.5)
    inp['moe_w_down'] = nrm((DEPTH, N_EXPERTS, EXPERT_FF, D), EXPERT_FF ** -0.5)
    return inp


def reference(x_prompt, x_sample, cache_attn_k, cache_attn_v, cache_mla_ckv, cache_mla_krope,
              state_ssm_re, state_ssm_im, state_lru, c, c_ctx,
              w_mod, b_mod, norm1_g, norm2_g, w_in, a_qnorm_g, a_knorm_g, a_sink,
              s5_lambda_re, s5_lambda_im, s5_log_step, s5_b_re, s5_b_im, s5_c_re, s5_c_im, s5_d, s5_w_glu,
              lru_conv_w, lru_conv_b, lru_w_a, lru_b_a, lru_w_x, lru_b_x, lru_lambda,
              mla_q_lat_norm, mla_w_uq, mla_kv_norm, mla_w_ukv, mla_qnorm_g, mla_knorm_g,
              w_branch, w_out, moe_w_group, moe_b_group, moe_w_expert, moe_b_expert,
              moe_w_gate, moe_w_up, moe_w_down):
    xp = x_prompt
    xs = x_sample
    ak_l, av_l, ckv_l, kr_l, sr_l, si_l, lru_l = [], [], [], [], [], [], []
    for l in range(DEPTH):
        p = {
            'w_mod': w_mod[l], 'b_mod': b_mod[l], 'norm1_g': norm1_g[l], 'norm2_g': norm2_g[l],
            'w_in': w_in[l], 'a_qnorm_g': a_qnorm_g[l], 'a_knorm_g': a_knorm_g[l], 'a_sink': a_sink[l],
            's5_lambda_re': s5_lambda_re[l], 's5_lambda_im': s5_lambda_im[l], 's5_log_step': s5_log_step[l],
            's5_b_re': s5_b_re[l], 's5_b_im': s5_b_im[l], 's5_c_re': s5_c_re[l], 's5_c_im': s5_c_im[l],
            's5_d': s5_d[l], 's5_w_glu': s5_w_glu[l],
            'lru_conv_w': lru_conv_w[l], 'lru_conv_b': lru_conv_b[l], 'lru_w_a': lru_w_a[l], 'lru_b_a': lru_b_a[l],
            'lru_w_x': lru_w_x[l], 'lru_b_x': lru_b_x[l], 'lru_lambda': lru_lambda[l],
            'mla_q_lat_norm': mla_q_lat_norm[l], 'mla_w_uq': mla_w_uq[l], 'mla_kv_norm': mla_kv_norm[l],
            'mla_w_ukv': mla_w_ukv[l], 'mla_qnorm_g': mla_qnorm_g[l], 'mla_knorm_g': mla_knorm_g[l],
            'w_branch': w_branch[l], 'w_out': w_out[l],
            'moe_w_group': moe_w_group[l], 'moe_b_group': moe_b_group[l], 'moe_w_expert': moe_w_expert[l],
            'moe_b_expert': moe_b_expert[l], 'moe_w_gate': moe_w_gate[l], 'moe_w_up': moe_w_up[l],
            'moe_w_down': moe_w_down[l],
        }
        xp, (ak, av, ckv, kr, sr, si, lr) = context_layer(xp, c_ctx, p)
        ak_l.append(ak)
        av_l.append(av)
        ckv_l.append(ckv)
        kr_l.append(kr)
        sr_l.append(sr)
        si_l.append(si)
        lru_l.append(lr)
        xs = latent_layer(xs, c, cache_attn_k[:, l], cache_attn_v[:, l], cache_mla_ckv[:, l], cache_mla_krope[:, l],
                          state_ssm_re[:, l], state_ssm_im[:, l], state_lru[:, l], p)
    new_attn_k = jnp.stack(ak_l, axis=1)
    new_attn_v = jnp.stack(av_l, axis=1)
    new_mla_ckv = jnp.stack(ckv_l, axis=1)
    new_mla_krope = jnp.stack(kr_l, axis=1)
    new_ssm_re = jnp.stack(sr_l, axis=1)
    new_ssm_im = jnp.stack(si_l, axis=1)
    new_lru = jnp.stack(lru_l, axis=1)
    return (xp, xs, new_attn_k, new_attn_v, new_mla_ckv, new_mla_krope, new_ssm_re, new_ssm_im, new_lru)
```

```python
import functools
import math

import jax
import jax.numpy as jnp
import numpy as np
from jax import lax
from jax.experimental import pallas as pl
from jax.experimental.pallas import tpu as pltpu

F32 = jnp.float32
BF16 = jnp.bfloat16

D_MODEL = 1024
DEPTH = 2
GRID_W = 64
N_BRANCH = 4
BRANCH_W = 256
ROPE_BASE = 10000.0
EPS = 1e-6
NEG_INF = -1e30
A_HEADS = 4
A_KV_HEADS = 2
A_HEAD_DIM = 64
WINDOW = 128
Q_BLOCK = 128
S5_GROUP = 16
S5_GROUPS = 16
S5_STATE = 64
S5_WIDTH = S5_GROUPS * S5_STATE
LRU_WIDTH = 256
LRU_BLOCKS = 4
LRU_CONV = 4
LRU_C = 8.0
MLA_HEADS = 4
MLA_Q_LORA = 256
MLA_KV_LORA = 128
MLA_NOPE = 64
MLA_ROPE = 32
MLA_V = 64
MLA_QK = MLA_NOPE + MLA_ROPE
N_GROUPS = 4
EXPERTS_PER_GROUP = 4
N_EXPERTS = 16
EXPERT_FF = 256

LANES = 128
SUBLANES = 8
VMEM_LIMIT = 56 * 1024 * 1024

TOK_TILE = 256
SCAN_STEPS = 64
MOE_TILE = 1024
MLA_Q_TILE = 256
Z_COLS = 1792
HEAD_ORDER = np.array((0, 2, 1, 3))


def _params(sem):
    return pltpu.CompilerParams(dimension_semantics=sem, vmem_limit_bytes=VMEM_LIMIT)


def _full(shape):
    n = len(shape)
    return pl.BlockSpec(shape, lambda *_: (0,) * n)


def _modnorm(x, g, sc, sh):
    ms = jnp.mean(x * x, axis=-1, keepdims=True)
    return (x * lax.rsqrt(ms + EPS)) * g * (1.0 + sc) + sh


def _rmsnorm(x, g):
    ms = jnp.mean(x * x, axis=-1, keepdims=True)
    return (x * lax.rsqrt(ms + EPS)) * g


def _half_rmsnorm(x):
    lane = lax.broadcasted_iota(jnp.int32, (1, LANES), 1)
    lo = lane < A_HEAD_DIM
    outs = []
    for t in range(x.shape[1] // LANES):
        xt = x[:, LANES * t:LANES * (t + 1)]
        sq = xt * xt
        s_lo = jnp.sum(jnp.where(lo, sq, 0.0), axis=-1, keepdims=True)
        s_hi = jnp.sum(jnp.where(lo, 0.0, sq), axis=-1, keepdims=True)
        inv = jnp.where(lo, lax.rsqrt(s_lo * (1.0 / A_HEAD_DIM) + EPS),
                        lax.rsqrt(s_hi * (1.0 / A_HEAD_DIM) + EPS))
        outs.append(xt * inv)
    return jnp.concatenate(outs, axis=1)


def _tile_rmsnorm(x, n_real):
    outs = []
    for t in range(x.shape[1] // LANES):
        xt = x[:, LANES * t:LANES * (t + 1)]
        ss = jnp.sum(xt * xt, axis=-1, keepdims=True)
        outs.append(xt * lax.rsqrt(ss * (1.0 / n_real) + EPS))
    return jnp.concatenate(outs, axis=1)


def _rope(x, cos, sin_next, sin_prev, quarter):
    width = x.shape[1]
    return (x * cos + pltpu.roll(x, width - quarter, 1) * sin_next
            + pltpu.roll(x, quarter, 1) * sin_prev)


def _mla_kv(ckv_n, kr_placed, wukv, gk, rope_tabs):
    kv = jnp.dot(ckv_n.astype(BF16), wukv, preferred_element_type=F32)
    kfull = kv[:, :MLA_HEADS * LANES] + jnp.concatenate([kr_placed] * MLA_HEADS, axis=1)
    kn = _tile_rmsnorm(kfull, MLA_QK) * gk
    if rope_tabs is not None:
        kn = _rope(kn, *rope_tabs, MLA_ROPE // 4)
    return kn, kv[:, MLA_HEADS * LANES:]


def _mod_kernel(c_ref, w_ref, b_ref, o_ref):
    c = c_ref[...]
    s = c * jax.nn.sigmoid(c)
    o_ref[...] = jnp.dot(s.astype(BF16), w_ref[...], preferred_element_type=F32) + b_ref[...]


def _modulation(cond, w_mod, b_mod):
    n_rows = cond.shape[0]
    n_out = w_mod.shape[-1]
    tn = 1024
    return pl.pallas_call(
        _mod_kernel,
        grid=(DEPTH, n_out // tn),
        in_specs=[pl.BlockSpec((n_rows, D_MODEL), lambda l, j: (0, 0)),
                  pl.BlockSpec((None, D_MODEL, tn), lambda l, j: (l, 0, j)),
                  pl.BlockSpec((None, 1, tn), lambda l, j: (l, 0, j))],
        out_specs=pl.BlockSpec((None, n_rows, tn), lambda l, j: (l, 0, j)),
        out_shape=jax.ShapeDtypeStruct((DEPTH, n_rows, n_out), F32),
        compiler_params=_params(("arbitrary", "arbitrary")),
        name="modulation",
    )(cond, w_mod, b_mod)


def _in_proj_kernel(rope, ctx, *refs):
    it = iter(refs)
    x_ref, sc_ref, sh_ref, g1_ref, wz_ref = (next(it) for _ in range(5))
    gq_ref, gk_ref, qlg_ref, wuq_ref, gmq_ref, kvg_ref, wukv_ref, gmk_ref = (next(it) for _ in range(8))
    if rope:
        ca_ref, sna_ref, spa_ref, cm_ref, snm_ref, spm_ref = (next(it) for _ in range(6))
    q_ref, k_ref, v_ref, s5u_ref, lx_ref, lg_ref, qm_ref, km_ref, vm_ref = (next(it) for _ in range(9))
    if ctx:
        k32_ref, v32_ref, ckv_ref, krp_ref = (next(it) for _ in range(4))

    h = _modnorm(x_ref[...], g1_ref[...], sc_ref[...], sh_ref[...])
    z = jnp.dot(h.astype(BF16), wz_ref[...], preferred_element_type=F32)

    q = _half_rmsnorm(z[:, 0:256]) * gq_ref[...]
    k = _half_rmsnorm(z[:, 256:384]) * gk_ref[...]
    v = z[:, 384:512]
    if ctx:
        k32_ref[...] = k
        v32_ref[...] = v
    if rope:
        quarter = A_HEAD_DIM // 4
        q = _rope(q, ca_ref[...], sna_ref[...], spa_ref[...], quarter)
        k = _rope(k, ca_ref[:, :LANES], sna_ref[:, :LANES], spa_ref[:, :LANES], quarter)
    q_ref[...] = q.astype(BF16)
    k_ref[...] = k.astype(BF16)
    v_ref[...] = v.astype(BF16)

    s5u_ref[...] = z[:, 512:768]
    lx_ref[...] = z[:, 768:1024]
    lg_ref[...] = z[:, 1024:1280]

    mtabs = None
    mtabs4 = None
    if rope:
        mtabs = (cm_ref[...], snm_ref[...], spm_ref[...])
        mtabs4 = tuple(jnp.concatenate([t] * MLA_HEADS, axis=1) for t in mtabs)
    ql = _rmsnorm(z[:, 1280:1536], qlg_ref[...])
    qm = jnp.dot(ql.astype(BF16), wuq_ref[...], preferred_element_type=F32)
    qm = _tile_rmsnorm(qm, MLA_QK) * gmq_ref[...]
    if rope:
        qm = _rope(qm, *mtabs4, MLA_ROPE // 4)
    qm_ref[...] = qm.astype(BF16)

    ckv_n = _rmsnorm(z[:, 1536:1664], kvg_ref[...])
    krp = z[:, 1664:1792]
    km, vm = _mla_kv(ckv_n, krp, wukv_ref[...], gmk_ref[...], mtabs4)
    km_ref[...] = km.astype(BF16)
    vm_ref[...] = vm.astype(BF16)
    if ctx:
        ckv_ref[...] = ckv_n
        krp_ref[...] = krp


def _in_proj(x, sc, sh, lw, rope_tabs, ctx):
    B, L, _ = x.shape
    Bg = B // SUBLANES
    TL = TOK_TILE
    rope = rope_tabs is not None
    per_b = sc.shape[0] > 1

    def mod_map(b, i):
        return (b if per_b else 0, 0, 0)

    tok = lambda c: pl.BlockSpec((None, TL, c), lambda b, i: (b, i, 0))
    tview = pl.BlockSpec((None, TL, 256), lambda b, i: (b // SUBLANES, i, b % SUBLANES))
    in_specs = [tok(D_MODEL), pl.BlockSpec((None, 1, D_MODEL), mod_map),
                pl.BlockSpec((None, 1, D_MODEL), mod_map), _full((1, D_MODEL)), _full((D_MODEL, Z_COLS)),
                _full((1, 256)), _full((1, 128)), _full((1, 256)), _full((256, 512)), _full((1, 512)),
                _full((1, 128)), _full((128, 768)), _full((1, 512))]
    args = [x, sc, sh, lw["g1"], lw["wz"], lw["gq"], lw["gk"], lw["qlg"], lw["wuq"], lw["gmq"],
            lw["kvg"], lw["wukv"], lw["gmk"]]
    if rope:
        in_specs += [pl.BlockSpec((TL, 256), lambda b, i: (i, 0))] * 3
        in_specs += [pl.BlockSpec((TL, 128), lambda b, i: (i, 0))] * 3
        args += list(rope_tabs)
    out_specs = [tok(256), tok(128), tok(128), tview, tview, tok(256), tok(512), tok(512), tok(256)]
    sds = jax.ShapeDtypeStruct
    out_shape = [sds((B, L, 256), BF16), sds((B, L, 128), BF16), sds((B, L, 128), BF16),
                 sds((Bg, L, SUBLANES * 256), F32), sds((Bg, L, SUBLANES * 256), F32),
                 sds((B, L, 256), F32), sds((B, L, 512), BF16), sds((B, L, 512), BF16),
                 sds((B, L, 256), BF16)]
    if ctx:
        out_specs += [tok(128), tok(128), tok(128), tok(128)]
        out_shape += [sds((B, L, 128), F32)] * 4
    return pl.pallas_call(
        functools.partial(_in_proj_kernel, rope, ctx),
        grid=(B, L // TL),
        in_specs=in_specs, out_specs=out_specs, out_shape=out_shape,
        compiler_params=_params(("parallel", "parallel")),
        name="in_proj_ctx" if ctx else "in_proj_lat",
    )(*args)


def _mla_cache_kernel(ckv_ref, krp_ref, wukv_ref, gmk_ref, km_ref, vm_ref):
    km, vm = _mla_kv(ckv_ref[...], krp_ref[...], wukv_ref[...], gmk_ref[...], None)
    km_ref[...] = km.astype(BF16)
    vm_ref[...] = vm.astype(BF16)


def _mla_cache_kv(cckv, ckr_placed, lw):
    B, P, _ = cckv.shape
    tok = lambda c: pl.BlockSpec((None, P, c), lambda b: (b, 0, 0))
    return pl.pallas_call(
        _mla_cache_kernel,
        grid=(B,),
        in_specs=[tok(128), tok(128), _full((128, 768)), _full((1, 512))],
        out_specs=[tok(512), tok(256)],
        out_shape=[jax.ShapeDtypeStruct((B, P, 512), BF16), jax.ShapeDtypeStruct((B, P, 256), BF16)],
        compiler_params=_params(("parallel",)),
        name="mla_cache_kv",
    )(cckv, ckr_placed, lw["wukv"], lw["gmk"])


def _sink_softmax(s, sink):
    m = jnp.maximum(jnp.max(s, axis=-1, keepdims=True), sink)
    e = jnp.exp(s - m)
    den = jnp.sum(e, axis=-1, keepdims=True) + jnp.exp(sink - m)
    return (e / den).astype(BF16)


def _gqa_tile(qt, keys, vst, sink_lo, sink_hi, mask):
    lane = lax.broadcasted_iota(jnp.int32, (1, LANES), 1)
    lo = lane < A_HEAD_DIM
    zero = jnp.zeros_like(qt)
    ps = []
    for qh, sink in ((jnp.where(lo, qt, zero), sink_lo), (jnp.where(lo, zero, qt), sink_hi)):
        s = lax.dot_general(qh, keys, (((1,), (1,)), ((), ())), preferred_element_type=F32)
        if mask is not None:
            s = jnp.where(mask, s, NEG_INF)
        ps.append(_sink_softmax(s, sink))
    return jnp.dot(jnp.concatenate(ps, axis=1), vst, preferred_element_type=F32)


def _stack_kv_halves(v):
    lane = lax.broadcasted_iota(jnp.int32, (1, LANES), 1)
    lo = lane < A_HEAD_DIM
    zero = jnp.zeros_like(v)
    return jnp.concatenate([jnp.where(lo, v, zero), jnp.where(lo, zero, v)], axis=0)


def _attn_a_ctx_kernel(sink_ref, q_ref, k_ref, v_ref, o_ref):
    keys = k_ref[...]
    vst = _stack_kv_halves(v_ref[...])
    for t in range(2):
        o = _gqa_tile(q_ref[:, LANES * t:LANES * (t + 1)], keys, vst, sink_ref[t], sink_ref[2 + t], None)
        o_ref[:, LANES * t:LANES * (t + 1)] = o.astype(o_ref.dtype)


def _attn_a_ctx(q, k, v, sink):
    B, L, _ = q.shape
    tok = lambda c: pl.BlockSpec((None, L, c), lambda b: (b, 0, 0))
    return pl.pallas_call(
        _attn_a_ctx_kernel,
        grid=(B,),
        in_specs=[pl.BlockSpec(memory_space=pltpu.SMEM), tok(256), tok(128), tok(128)],
        out_specs=tok(256),
        out_shape=jax.ShapeDtypeStruct((B, L, 256), BF16),
        compiler_params=_params(("parallel",)),
        name="attn_a_ctx",
    )(sink, q, k, v)


def _attn_a_lat_kernel(sink_ref, q_ref, kc_ref, vc_ref, k_ref, v_ref, o_ref):
    i = pl.program_id(1)
    L = k_ref.shape[0]
    n_ctx = kc_ref.shape[0]
    span = 3 * Q_BLOCK
    start = pl.multiple_of(jnp.clip((i - 1) * Q_BLOCK, 0, L - span), Q_BLOCK)
    keys = jnp.concatenate([kc_ref[...].astype(BF16), k_ref[pl.ds(start, span), :]], axis=0)
    vals = jnp.concatenate([vc_ref[...].astype(BF16), v_ref[pl.ds(start, span), :]], axis=0)
    vst = _stack_kv_halves(vals)
    col = lax.broadcasted_iota(jnp.int32, (Q_BLOCK, n_ctx + span), 1)
    row = lax.broadcasted_iota(jnp.int32, (Q_BLOCK, n_ctx + span), 0)
    rel = (col - n_ctx + start) - (row + i * Q_BLOCK)
    mask = (col < n_ctx) | (jnp.abs(rel) <= WINDOW)
    for t in range(2):
        o = _gqa_tile(q_ref[:, LANES * t:LANES * (t + 1)], keys, vst, sink_ref[t], sink_ref[2 + t], mask)
        o_ref[:, LANES * t:LANES * (t + 1)] = o.astype(o_ref.dtype)


def _attn_a_lat(q, k, v, kc, vc, sink):
    B, L, _ = q.shape
    P = kc.shape[1]
    blk = lambda c: pl.BlockSpec((None, Q_BLOCK, c), lambda b, i: (b, i, 0))
    whole = lambda n, c: pl.BlockSpec((None, n, c), lambda b, i: (b, 0, 0))
    return pl.pallas_call(
        _attn_a_lat_kernel,
        grid=(B, L // Q_BLOCK),
        in_specs=[pl.BlockSpec(memory_space=pltpu.SMEM), blk(256), whole(P, 128), whole(P, 128),
                  whole(L, 128), whole(L, 128)],
        out_specs=blk(256),
        out_shape=jax.ShapeDtypeStruct((B, L, 256), BF16),
        compiler_params=_params(("parallel", "parallel")),
        name="attn_a_lat",
    )(sink, q, kc, vc, k, v)


def _mla_attend(q, key_parts, val_parts):
    vals = val_parts[0] if len(val_parts) == 1 else jnp.concatenate(val_parts, axis=0)
    lane = lax.broadcasted_iota(jnp.int32, (1, MLA_HEADS * MLA_V), 1)
    zero = jnp.zeros_like(vals)
    vst = jnp.concatenate([jnp.where((lane >= MLA_V * h) & (lane < MLA_V * (h + 1)), vals, zero)
                           for h in range(MLA_HEADS)], axis=0)
    ps = []
    for h in range(MLA_HEADS):
        qh = q[:, LANES * h:LANES * (h + 1)]
        ss = [lax.dot_general(qh, kp[:, LANES * h:LANES * (h + 1)], (((1,), (1,)), ((), ())),
                              preferred_element_type=F32) for kp in key_parts]
        m = functools.reduce(jnp.maximum, [jnp.max(s, axis=-1, keepdims=True) for s in ss])
        es = [jnp.exp(s - m) for s in ss]
        den = functools.reduce(lambda a, b: a + b, [jnp.sum(e, axis=-1, keepdims=True) for e in es])
        ps += [(e / den).astype(BF16) for e in es]
    return jnp.dot(jnp.concatenate(ps, axis=1), vst, preferred_element_type=F32)


def _mla_ctx_kernel(q_ref, k_ref, v_ref, o_ref):
    o_ref[...] = _mla_attend(q_ref[...], [k_ref[...]], [v_ref[...]]).astype(o_ref.dtype)


def _mla_ctx(q, k, v):
    B, L, _ = q.shape
    tok = lambda c: pl.BlockSpec((None, L, c), lambda b: (b, 0, 0))
    return pl.pallas_call(
        _mla_ctx_kernel,
        grid=(B,),
        in_specs=[tok(512), tok(512), tok(256)],
        out_specs=tok(256),
        out_shape=jax.ShapeDtypeStruct((B, L, 256), BF16),
        compiler_params=_params(("parallel",)),
        name="mla_ctx",
    )(q, k, v)


def _mla_lat_kernel(q_ref, kc_ref, vc_ref, k_ref, v_ref, o_ref):
    o = _mla_attend(q_ref[...], [kc_ref[...], k_ref[...]], [vc_ref[...], v_ref[...]])
    o_ref[...] = o.astype(o_ref.dtype)


def _mla_lat(q, k, v, kc, vc):
    B, L, _ = q.shape
    P = kc.shape[1]
    TQ = MLA_Q_TILE
    blk = lambda c: pl.BlockSpec((None, TQ, c), lambda b, i: (b, i, 0))
    whole = lambda n, c: pl.BlockSpec((None, n, c), lambda b, i: (b, 0, 0))
    return pl.pallas_call(
        _mla_lat_kernel,
        grid=(B, L // TQ),
        in_specs=[blk(512), whole(P, 512), whole(P, 256), whole(L, 512), whole(L, 256)],
        out_specs=blk(256),
        out_shape=jax.ShapeDtypeStruct((B, L, 256), BF16),
        compiler_params=_params(("parallel", "parallel")),
        name="mla_lat",
    )(q, kc, vc, k, v)


def _s5_kernel(uf_ref, ub_ref, bf_ref, bb_ref, a_ref, h0_ref, cf_ref, cb_ref,
               yf_ref, yb_ref, fin_ref, s_ref, st_ref):
    i = pl.program_id(1)
    steps = SCAN_STEPS

    @pl.when(i == 0)
    def _():
        st_ref[...] = h0_ref[...]

    uf = uf_ref[...].astype(BF16)
    ub = ub_ref[...].astype(BF16)
    s_ref[0] = jnp.dot(uf, bf_ref[:, :S5_WIDTH], preferred_element_type=F32)
    s_ref[1] = jnp.dot(uf, bf_ref[:, S5_WIDTH:], preferred_element_type=F32)
    s_ref[2] = jnp.dot(ub, bb_ref[:, :S5_WIDTH], preferred_element_type=F32)
    s_ref[3] = jnp.dot(ub, bb_ref[:, S5_WIDTH:], preferred_element_type=F32)

    lane_chunk = 2 * LANES
    for c in range(S5_WIDTH // lane_chunk):
        sl = slice(lane_chunk * c, lane_chunk * (c + 1))
        arf, aif, arb, aib = (a_ref[n, :, sl] for n in range(4))

        def body(j, carry):
            hrf, hif, hrb, hib = carry
            rf = pl.multiple_of(j * SUBLANES, SUBLANES)
            rb = pl.multiple_of((steps - 1 - j) * SUBLANES, SUBLANES)
            nrf = arf * hrf - aif * hif + s_ref[0, pl.ds(rf, SUBLANES), sl]
            nif = arf * hif + aif * hrf + s_ref[1, pl.ds(rf, SUBLANES), sl]
            nrb = arb * hrb - aib * hib + s_ref[2, pl.ds(rb, SUBLANES), sl]
            nib = arb * hib + aib * hrb + s_ref[3, pl.ds(rb, SUBLANES), sl]
            s_ref[0, pl.ds(rf, SUBLANES), sl] = nrf
            s_ref[1, pl.ds(rf, SUBLANES), sl] = nif
            s_ref[2, pl.ds(rb, SUBLANES), sl] = nrb
            s_ref[3, pl.ds(rb, SUBLANES), sl] = nib
            return nrf, nif, nrb, nib

        fin = lax.fori_loop(0, steps, body, tuple(st_ref[n, :, sl] for n in range(4)), unroll=4)
        for n in range(4):
            st_ref[n, :, sl] = fin[n]

    hf = jnp.concatenate([s_ref[0].astype(BF16), s_ref[1].astype(BF16)], axis=1)
    yf_ref[...] = jnp.dot(hf, cf_ref[...], preferred_element_type=F32)
    hb = jnp.concatenate([s_ref[2].astype(BF16), s_ref[3].astype(BF16)], axis=1)
    yb_ref[...] = jnp.dot(hb, cb_ref[...], preferred_element_type=F32)
    fin_ref[...] = st_ref[...]


def _s5_scan(u_rows, lw, h0):
    Bg, rows, _ = u_rows.shape
    R = SCAN_STEPS * SUBLANES
    n = rows // R
    fwd = pl.BlockSpec((None, R, 256), lambda g, i: (g, i, 0))
    bwd = pl.BlockSpec((None, R, 256), lambda g, i: (g, n - 1 - i, 0))
    st = pl.BlockSpec((None, 4, SUBLANES, S5_WIDTH), lambda g, i: (g, 0, 0, 0))
    return pl.pallas_call(
        _s5_kernel,
        grid=(Bg, n),
        in_specs=[fwd, bwd, _full((256, 2 * S5_WIDTH)), _full((256, 2 * S5_WIDTH)),
                  _full((4, SUBLANES, S5_WIDTH)), st, _full((2 * S5_WIDTH, 256)), _full((2 * S5_WIDTH, 256))],
        out_specs=[fwd, bwd, st],
        out_shape=[jax.ShapeDtypeStruct((Bg, rows, 256), F32), jax.ShapeDtypeStruct((Bg, rows, 256), F32),
                   jax.ShapeDtypeStruct((Bg, 4, SUBLANES, S5_WIDTH), F32)],
        scratch_shapes=[pltpu.VMEM((4, R, S5_WIDTH), F32), pltpu.VMEM((4, SUBLANES, S5_WIDTH), F32)],
        compiler_params=_params(("parallel", "arbitrary")),
        name="s5_scan",
    )(u_rows, u_rows, lw["s5_bf"], lw["s5_bb"], lw["s5_a"], h0, lw["s5_cf"], lw["s5_cb"])


def _lru_gates(x_ref, pre_ref, post_ref, has_pre, has_post, cw_ref, cb_ref, w_ref, b_ref, sp_ref, a_ref, h_ref):
    R = x_ref.shape[0]
    pre = jnp.where(has_pre, pre_ref[...], 0.0)
    post = jnp.where(has_post, post_ref[...], 0.0)
    xp = jnp.concatenate([pre, x_ref[...], post], axis=0)
    xc = cb_ref[...]
    for t in range(LRU_CONV):
        xc = xc + xp[SUBLANES * t:SUBLANES * t + R] * cw_ref[t:t + 1, :]
    g = jnp.dot(xc.astype(BF16), w_ref[...], preferred_element_type=F32) + b_ref[...]
    r = jax.nn.sigmoid(g[:, :LRU_WIDTH])
    ig = jax.nn.sigmoid(g[:, LRU_WIDTH:])
    log_a = (-LRU_C) * r * sp_ref[...]
    a = jnp.exp(log_a)
    a_ref[...] = a
    h_ref[...] = jnp.sqrt(1.0 - a * a) * (ig * xc)


def _lru_kernel(xf_ref, xfp_ref, xfn_ref, xb_ref, xbp_ref, xbn_ref, cw_ref, cb_ref, wf_ref, wb_ref,
                bf_ref, bb_ref, spf_ref, spb_ref, h0_ref, hf_ref, hb_ref, fin_ref, af_ref, ab_ref, st_ref):
    i = pl.program_id(1)
    n = pl.num_programs(1)
    steps = SCAN_STEPS

    @pl.when(i == 0)
    def _():
        st_ref[...] = h0_ref[...]

    _lru_gates(xf_ref, xfp_ref, xfn_ref, i > 0, i < n - 1, cw_ref, cb_ref, wf_ref, bf_ref, spf_ref, af_ref, hf_ref)
    _lru_gates(xb_ref, xbp_ref, xbn_ref, i < n - 1, i > 0, cw_ref, cb_ref, wb_ref, bb_ref, spb_ref, ab_ref, hb_ref)

    def body(j, carry):
        hf, hb = carry
        rf = pl.multiple_of(j * SUBLANES, SUBLANES)
        rb = pl.multiple_of((steps - 1 - j) * SUBLANES, SUBLANES)
        nf = af_ref[pl.ds(rf, SUBLANES), :] * hf + hf_ref[pl.ds(rf, SUBLANES), :]
        nb = ab_ref[pl.ds(rb, SUBLANES), :] * hb + hb_ref[pl.ds(rb, SUBLANES), :]
        hf_ref[pl.ds(rf, SUBLANES), :] = nf
        hb_ref[pl.ds(rb, SUBLANES), :] = nb
        return nf, nb

    ff, fb = lax.fori_loop(0, steps, body, (st_ref[0], st_ref[1]), unroll=8)
    st_ref[0] = ff
    st_ref[1] = fb
    fin_ref[...] = st_ref[...]


def _lru_scan(x_rows, lw, h0):
    Bg, rows, _ = x_rows.shape
    R = SCAN_STEPS * SUBLANES
    n = rows // R
    pre_rows = 2 * SUBLANES
    fwd = lambda g, i: (g, i, 0)
    bwd = lambda g, i: (g, n - 1 - i, 0)
    blk = lambda m: pl.BlockSpec((None, R, 256), m)
    pre = lambda m: pl.BlockSpec(
        (None, pre_rows, 256), lambda g, i: (g, jnp.maximum(m(g, i)[1] * (R // pre_rows) - 1, 0), 0))
    post = lambda m: pl.BlockSpec(
        (None, SUBLANES, 256),
        lambda g, i: (g, jnp.minimum((m(g, i)[1] + 1) * (R // SUBLANES), rows // SUBLANES - 1), 0))
    st = pl.BlockSpec((None, 2, SUBLANES, 256), lambda g, i: (g, 0, 0, 0))
    return pl.pallas_call(
        _lru_kernel,
        grid=(Bg, n),
        in_specs=[blk(fwd), pre(fwd), post(fwd), blk(bwd), pre(bwd), post(bwd),
                  _full((LRU_CONV, 256)), _full((1, 256)), _full((256, 512)), _full((256, 512)),
                  _full((1, 512)), _full((1, 512)), _full((1, 256)), _full((1, 256)), st],
        out_specs=[blk(fwd), blk(bwd), st],
        out_shape=[jax.ShapeDtypeStruct((Bg, rows, 256), F32), jax.ShapeDtypeStruct((Bg, rows, 256), F32),
                   jax.ShapeDtypeStruct((Bg, 2, SUBLANES, 256), F32)],
        scratch_shapes=[pltpu.VMEM((R, 256), F32), pltpu.VMEM((R, 256), F32),
                        pltpu.VMEM((2, SUBLANES, 256), F32)],
        compiler_params=_params(("parallel", "arbitrary")),
        name="lru_scan",
    )(x_rows, x_rows, x_rows, x_rows, x_rows, x_rows, lw["lru_cw"], lw["lru_cb"], lw["lru_wf"], lw["lru_wb"],
      lw["lru_bf"], lw["lru_bb"], lw["lru_spf"], lw["lru_spb"], h0)


def _merge_kernel(x_ref, sc_ref, sh_ref, gt_ref, g1_ref, wg_ref, oa_ref, u_ref, yf_ref, yb_ref, d_ref,
                  wglu_ref, hf_ref, hb_ref, lg_ref, od_ref, wb_ref, wo_ref, o_ref):
    x = x_ref[...]
    h = _modnorm(x, g1_ref[...], sc_ref[...], sh_ref[...]).astype(BF16)

    yb5 = jax.nn.gelu(d_ref[...] * u_ref[...] + yf_ref[...] + yb_ref[...])
    gv = jnp.dot(yb5.astype(BF16), wglu_ref[...], preferred_element_type=F32)
    o_b = gv[:, :BRANCH_W] * jax.nn.sigmoid(gv[:, BRANCH_W:])
    o_c = (hf_ref[...] + hb_ref[...]) * jax.nn.gelu(lg_ref[...])
    branches = (oa_ref[...], o_b.astype(BF16), o_c.astype(BF16), od_ref[...])

    acc = jnp.zeros(x.shape, F32)
    for n in range(N_BRANCH):
        gate = jnp.dot(h, wg_ref[:, D_MODEL * n:D_MODEL * (n + 1)], preferred_element_type=F32)
        proj = jnp.dot(branches[n], wb_ref[n], preferred_element_type=F32)
        acc = acc + jax.nn.sigmoid(gate) * proj
    out = jnp.dot(acc.astype(BF16), wo_ref[...], preferred_element_type=F32)
    o_ref[...] = x + gt_ref[...] * out


def _merge(x, sc, sh, gt, lw, o_a, s5u_t, yf_t, yb_t, hf_t, hb_t, lg, o_d):
    B, L, _ = x.shape
    TL = TOK_TILE
    per_b = sc.shape[0] > 1

    def mod_map(b, i):
        return (b if per_b else 0, 0, 0)

    tok = lambda c: pl.BlockSpec((None, TL, c), lambda b, i: (b, i, 0))
    tview = pl.BlockSpec((None, TL, 256), lambda b, i: (b // SUBLANES, i, b % SUBLANES))
    mod = pl.BlockSpec((None, 1, D_MODEL), mod_map)
    return pl.pallas_call(
        _merge_kernel,
        grid=(B, L // TL),
        in_specs=[tok(D_MODEL), mod, mod, mod, _full((1, D_MODEL)), _full((D_MODEL, N_BRANCH * D_MODEL)),
                  tok(256), tview, tview, tview, _full((1, 256)), _full((256, 512)),
                  tview, tview, tok(256), tok(256), _full((N_BRANCH, BRANCH_W, D_MODEL)),
                  _full((D_MODEL, D_MODEL))],
        out_specs=tok(D_MODEL),
        out_shape=jax.ShapeDtypeStruct((B, L, D_MODEL), F32),
        compiler_params=_params(("parallel", "parallel")),
        name="merge",
    )(x, sc, sh, gt, lw["g1"], lw["wg"], o_a, s5u_t, yf_t, yb_t, lw["s5_d"], lw["wglu"],
      hf_t, hb_t, lg, o_d, lw["wb"], lw["wo"])


def _route(logits):
    lane = lax.broadcasted_iota(jnp.int32, logits.shape, 1).astype(F32)
    is_g = lane < N_GROUPS
    big = 1e9
    gmax = jnp.max(jnp.where(is_g, logits, NEG_INF), axis=-1, keepdims=True)
    g_idx = jnp.min(jnp.where(is_g & (logits == gmax), lane, big), axis=-1, keepdims=True)
    gsum = jnp.sum(jnp.where(is_g, jnp.exp(logits - gmax), 0.0), axis=-1, keepdims=True)
    pg_sel = 1.0 / gsum
    first = N_GROUPS + g_idx * EXPERTS_PER_GROUP
    sel = (lane >= first) & (lane < first + EXPERTS_PER_GROUP)
    emax = jnp.max(jnp.where(sel, logits, NEG_INF), axis=-1, keepdims=True)
    ee = jnp.where(sel, jnp.exp(logits - emax), 0.0)
    pe = ee / jnp.sum(ee, axis=-1, keepdims=True)
    v1 = jnp.max(jnp.where(sel, pe, -1.0), axis=-1, keepdims=True)
    i1 = jnp.min(jnp.where(sel & (pe == v1), lane, big), axis=-1, keepdims=True)
    rest = sel & (lane != i1)
    v2 = jnp.max(jnp.where(rest, pe, -1.0), axis=-1, keepdims=True)
    i2 = jnp.min(jnp.where(rest & (pe == v2), lane, big), axis=-1, keepdims=True)
    tot = v1 + v2
    return jnp.where(lane == i1, pg_sel * v1 / tot, jnp.where(lane == i2, pg_sel * v2 / tot, 0.0))


def _moe_kernel(x_ref, sc_ref, sh_ref, gt_ref, g2_ref, wrh_ref, wrl_ref, br_ref, wga_ref, wup_ref, wdn_ref,
                o_ref, h_ref, gate_ref, acc_ref):
    e = pl.program_id(1)

    @pl.when(e == 0)
    def _():
        h = _modnorm(x_ref[...], g2_ref[...], sc_ref[...], sh_ref[...])
        hh = h.astype(BF16)
        hl = (h - hh.astype(F32)).astype(BF16)
        logits = (jnp.dot(hh, wrh_ref[...], preferred_element_type=F32)
                  + jnp.dot(hl, wrh_ref[...], preferred_element_type=F32)
                  + jnp.dot(hh, wrl_ref[...], preferred_element_type=F32)) + br_ref[...]
        h_ref[...] = hh
        gate_ref[...] = _route(logits)
        acc_ref[...] = jnp.zeros_like(acc_ref)

    h = h_ref[...]
    lane = lax.broadcasted_iota(jnp.int32, gate_ref.shape, 1)
    g_e = jnp.sum(jnp.where(lane == e + N_GROUPS, gate_ref[...], 0.0), axis=-1, keepdims=True)
    a = jnp.dot(h, wga_ref[...], preferred_element_type=F32)
    u = jnp.dot(h, wup_ref[...], preferred_element_type=F32)
    act = (a * jax.nn.sigmoid(a)) * u * g_e
    acc_ref[...] += jnp.dot(act.astype(BF16), wdn_ref[...], preferred_element_type=F32)

    @pl.when(e == N_EXPERTS - 1)
    def _():
        o_ref[...] = x_ref[...] + gt_ref[...] * acc_ref[...]


def _moe(x, sc, sh, gt, lw):
    B, L, _ = x.shape
    per_b = sc.shape[0] > 1
    xf = x.reshape(B * L, D_MODEL)
    TM = MOE_TILE
    tiles_per_seq = L // TM if per_b else 1

    def mod_map(t, e):
        return (t // tiles_per_seq if per_b else 0, 0, 0)

    tok = pl.BlockSpec((TM, D_MODEL), lambda t, e: (t, 0))
    mod = pl.BlockSpec((None, 1, D_MODEL), mod_map)
    out = pl.pallas_call(
        _moe_kernel,
        grid=(B * L // TM, N_EXPERTS),
        in_specs=[tok, mod, mod, mod, _full((1, D_MODEL)), _full((D_MODEL, LANES)), _full((D_MODEL, LANES)),
                  _full((1, LANES)),
                  pl.BlockSpec((None, D_MODEL, EXPERT_FF), lambda t, e: (e, 0, 0)),
                  pl.BlockSpec((None, D_MODEL, EXPERT_FF), lambda t, e: (e, 0, 0)),
                  pl.BlockSpec((None, EXPERT_FF, D_MODEL), lambda t, e: (e, 0, 0))],
        out_specs=tok,
        out_shape=jax.ShapeDtypeStruct((B * L, D_MODEL), F32),
        scratch_shapes=[pltpu.VMEM((TM, D_MODEL), BF16), pltpu.VMEM((TM, LANES), F32),
                        pltpu.VMEM((TM, D_MODEL), F32)],
        compiler_params=_params(("parallel", "arbitrary")),
        name="moe",
    )(xf, sc, sh, gt, lw["g2"], lw["wr_hi"], lw["wr_lo"], lw["br"], lw["wga"], lw["wup"], lw["wdn"])
    return out.reshape(B, L, D_MODEL)


def _block_diag(blocks):
    n, r, c = blocks.shape
    eye = jnp.eye(n, dtype=blocks.dtype)
    return jnp.einsum("nrc,nm->nrmc", blocks, eye).reshape(n * r, n * c)


def _rope_tables(seq_len, dim, width, offset):
    rows = seq_len // GRID_W
    r, col = jnp.meshgrid(jnp.arange(rows), jnp.arange(GRID_W), indexing="ij")
    r = r.reshape(-1).astype(F32)
    col = col.reshape(-1).astype(F32)
    quarter = dim // 4
    freqs = ROPE_BASE ** (-jnp.arange(quarter, dtype=F32) / quarter)
    ang_r = r[:, None] * freqs
    ang_c = col[:, None] * freqs
    zero = jnp.zeros_like(ang_r)
    cos = jnp.cos(jnp.concatenate([ang_r, ang_r, ang_c, ang_c], axis=-1))
    sin_next = jnp.concatenate([-jnp.sin(ang_r), zero, -jnp.sin(ang_c), zero], axis=-1)
    sin_prev = jnp.concatenate([zero, jnp.sin(ang_r), zero, jnp.sin(ang_c)], axis=-1)

    def place(t, fill):
        return jnp.pad(t, ((0, 0), (offset, width - offset - dim)), constant_values=fill)

    return place(cos, 1.0), place(sin_next, 0.0), place(sin_prev, 0.0)


def _layer_weights(l, w):
    lw = {}
    w_in = w["w_in"][l]
    q_cols = w_in[:, 0:256].reshape(D_MODEL, A_HEADS, A_HEAD_DIM)[:, HEAD_ORDER].reshape(D_MODEL, 256)
    kr_cols = jnp.pad(w_in[:, 1664:1696], ((0, 0), (MLA_NOPE, LANES - MLA_QK)))
    lw["wz"] = jnp.concatenate([q_cols, w_in[:, 256:1664], kr_cols], axis=1).astype(BF16)
    lw["wg"] = w_in[:, 1696:].astype(BF16)
    lw["g1"] = w["norm1_g"][l][None]
    lw["g2"] = w["norm2_g"][l][None]
    lw["gq"] = jnp.tile(w["a_qnorm_g"][l], A_HEADS)[None] * (A_HEAD_DIM ** -0.5)
    lw["gk"] = jnp.tile(w["a_knorm_g"][l], A_KV_HEADS)[None]
    lw["sink"] = w["a_sink"][l]
    lw["qlg"] = w["mla_q_lat_norm"][l][None]
    wuq = w["mla_w_uq"][l].reshape(MLA_Q_LORA, MLA_HEADS, MLA_QK)
    lw["wuq"] = jnp.pad(wuq, ((0, 0), (0, 0), (0, LANES - MLA_QK))).reshape(MLA_Q_LORA, MLA_HEADS * LANES).astype(BF16)
    pad_g = lambda g: jnp.tile(jnp.pad(g, (0, LANES - MLA_QK)), MLA_HEADS)[None]
    lw["gmq"] = pad_g(w["mla_qnorm_g"][l]) * (MLA_QK ** -0.5)
    lw["gmk"] = pad_g(w["mla_knorm_g"][l])
    lw["kvg"] = w["mla_kv_norm"][l][None]
    wukv = w["mla_w_ukv"][l].reshape(MLA_KV_LORA, MLA_HEADS, MLA_NOPE + MLA_V)
    wk = jnp.pad(wukv[:, :, :MLA_NOPE], ((0, 0), (0, 0), (0, LANES - MLA_NOPE))).reshape(MLA_KV_LORA, MLA_HEADS * LANES)
    wv = wukv[:, :, MLA_NOPE:].reshape(MLA_KV_LORA, MLA_HEADS * MLA_V)
    lw["wukv"] = jnp.concatenate([wk, wv], axis=1).astype(BF16)

    lre = w["s5_lambda_re"][l]
    lim = w["s5_lambda_im"][l]
    dt = jnp.exp(w["s5_log_step"][l])[:, :, None]
    mag = jnp.exp(lre * dt)
    ar, ai = mag * jnp.cos(lim * dt), mag * jnp.sin(lim * dt)
    den = lre * lre + lim * lim
    fr = ((ar - 1.0) * lre + ai * lim) / den
    fi = (ai * lre - (ar - 1.0) * lim) / den
    br, bi = w["s5_b_re"][l], w["s5_b_im"][l]
    bbr = fr[..., None] * br - fi[..., None] * bi
    bbi = fr[..., None] * bi + fi[..., None] * br
    in_map = lambda d: jnp.concatenate(
        [_block_diag(jnp.swapaxes(bbr[d], 1, 2)), _block_diag(jnp.swapaxes(bbi[d], 1, 2))], axis=1).astype(BF16)
    out_map = lambda d: jnp.concatenate(
        [_block_diag(jnp.swapaxes(w["s5_c_re"][l][d], 1, 2)),
         -_block_diag(jnp.swapaxes(w["s5_c_im"][l][d], 1, 2))], axis=0).astype(BF16)
    lw["s5_bf"], lw["s5_bb"] = in_map(0), in_map(1)
    lw["s5_cf"], lw["s5_cb"] = out_map(0), out_map(1)
    coef = jnp.stack([ar[0].reshape(-1), ai[0].reshape(-1), ar[1].reshape(-1), ai[1].reshape(-1)])
    lw["s5_a"] = jnp.broadcast_to(coef[:, None, :], (4, SUBLANES, S5_WIDTH))
    lw["s5_d"] = w["s5_d"][l][None]
    lw["wglu"] = w["s5_w_glu"][l].astype(BF16)

    lw["lru_cw"] = w["lru_conv_w"][l]
    lw["lru_cb"] = w["lru_conv_b"][l][None]
    gate_w = lambda d: jnp.concatenate(
        [_block_diag(w["lru_w_a"][l][d]), _block_diag(w["lru_w_x"][l][d])], axis=1).astype(BF16)
    gate_b = lambda d: jnp.concatenate([w["lru_b_a"][l][d], w["lru_b_x"][l][d]])[None]
    lw["lru_wf"], lw["lru_wb"] = gate_w(0), gate_w(1)
    lw["lru_bf"], lw["lru_bb"] = gate_b(0), gate_b(1)
    sp = jax.nn.softplus(-w["lru_lambda"][l])
    lw["lru_spf"], lw["lru_spb"] = sp[0][None], sp[1][None]

    wb = w["w_branch"][l]
    wb0 = wb[0].reshape(A_HEADS, A_HEAD_DIM, D_MODEL)[HEAD_ORDER].reshape(BRANCH_W, D_MODEL)
    lw["wb"] = jnp.concatenate([wb0[None], wb[1:]], axis=0).astype(BF16)
    lw["wo"] = w["w_out"][l].astype(BF16)

    wr = jnp.pad(jnp.concatenate([w["moe_w_group"][l], w["moe_w_expert"][l]], axis=1),
                 ((0, 0), (0, LANES - N_GROUPS - N_EXPERTS)))
    lw["wr_hi"] = wr.astype(BF16)
    lw["wr_lo"] = (wr - lw["wr_hi"].astype(F32)).astype(BF16)
    lw["br"] = jnp.pad(jnp.concatenate([w["moe_b_group"][l], w["moe_b_expert"][l]]),
                       (0, LANES - N_GROUPS - N_EXPERTS))[None]
    lw["wga"] = w["moe_w_gate"][l].astype(BF16)
    lw["wup"] = w["moe_w_up"][l].astype(BF16)
    lw["wdn"] = w["moe_w_down"][l].astype(BF16)
    return lw


def _rows_to_state(fin, width):
    Bg, n = fin.shape[0], fin.shape[1]
    return jnp.swapaxes(fin, 1, 2).reshape(Bg * SUBLANES, n, width)


def _state_to_rows(state):
    B, n, width = state.shape
    return jnp.swapaxes(state.reshape(B // SUBLANES, SUBLANES, n, width), 1, 2)


def _mix(x, mods, lw, rope_tabs, cache, s5_h0, lru_h0, ctx):
    sh1, sc1, gt1 = mods
    B, L, _ = x.shape
    Bg = B // SUBLANES
    outs = _in_proj(x, sc1, sh1, lw, rope_tabs, ctx)
    q, k, v, s5u_t, lx_t, lg, qm, km, vm = outs[:9]
    if ctx:
        o_a = _attn_a_ctx(q, k, v, lw["sink"])
        o_d = _mla_ctx(qm, km, vm)
    else:
        ck, cv, cckv, ckr = cache
        o_a = _attn_a_lat(q, k, v, ck, cv, lw["sink"])
        kc, vc = _mla_cache_kv(cckv, ckr, lw)
        o_d = _mla_lat(qm, km, vm, kc, vc)
    rows = lambda t: t.reshape(Bg, L * SUBLANES, 256)
    view = lambda t: t.reshape(Bg, L, SUBLANES * 256)
    yf, yb, s5_fin = _s5_scan(rows(s5u_t), lw, s5_h0)
    hf, hb, lru_fin = _lru_scan(rows(lx_t), lw, lru_h0)
    x = _merge(x, sc1, sh1, gt1, lw, o_a, s5u_t, view(yf), view(yb), view(hf), view(hb), lg, o_d)
    return x, outs[9:], s5_fin, lru_fin


def kernel(x_prompt, x_sample, cache_attn_k, cache_attn_v, cache_mla_ckv, cache_mla_krope, state_ssm_re, state_ssm_im, state_lru, c, c_ctx, w_mod, b_mod, norm1_g, norm2_g, w_in, a_qnorm_g, a_knorm_g, a_sink, s5_lambda_re, s5_lambda_im, s5_log_step, s5_b_re, s5_b_im, s5_c_re, s5_c_im, s5_d, s5_w_glu, lru_conv_w, lru_conv_b, lru_w_a, lru_b_a, lru_w_x, lru_b_x, lru_lambda, mla_q_lat_norm, mla_w_uq, mla_kv_norm, mla_w_ukv, mla_qnorm_g, mla_knorm_g, w_branch, w_out, moe_w_group, moe_b_group, moe_w_expert, moe_b_expert, moe_w_gate, moe_w_up, moe_w_down):
    w = dict(norm1_g=norm1_g, norm2_g=norm2_g, w_in=w_in, a_qnorm_g=a_qnorm_g, a_knorm_g=a_knorm_g,
             a_sink=a_sink, s5_lambda_re=s5_lambda_re, s5_lambda_im=s5_lambda_im, s5_log_step=s5_log_step,
             s5_b_re=s5_b_re, s5_b_im=s5_b_im, s5_c_re=s5_c_re, s5_c_im=s5_c_im, s5_d=s5_d, s5_w_glu=s5_w_glu,
             lru_conv_w=lru_conv_w, lru_conv_b=lru_conv_b, lru_w_a=lru_w_a, lru_b_a=lru_b_a, lru_w_x=lru_w_x,
             lru_b_x=lru_b_x, lru_lambda=lru_lambda, mla_q_lat_norm=mla_q_lat_norm, mla_w_uq=mla_w_uq,
             mla_kv_norm=mla_kv_norm, mla_w_ukv=mla_w_ukv, mla_qnorm_g=mla_qnorm_g, mla_knorm_g=mla_knorm_g,
             w_branch=w_branch, w_out=w_out, moe_w_group=moe_w_group, moe_b_group=moe_b_group,
             moe_w_expert=moe_w_expert, moe_b_expert=moe_b_expert, moe_w_gate=moe_w_gate, moe_w_up=moe_w_up,
             moe_w_down=moe_w_down)
    B, L, _ = x_prompt.shape
    Bd, Ld, _ = x_sample.shape
    P = cache_attn_k.shape[2]

    cond = jnp.zeros((2 * SUBLANES, D_MODEL), F32).at[:Bd].set(c).at[Bd].set(c_ctx)
    mod = _modulation(cond, w_mod.astype(BF16), b_mod[:, None, :])
    mod = mod.reshape(DEPTH, 2 * SUBLANES, 6, D_MODEL)

    rope_a = _rope_tables(Ld, A_HEAD_DIM, A_HEAD_DIM, 0)
    rope_a = tuple(jnp.tile(t, (1, A_HEADS)) for t in rope_a)
    rope_m = _rope_tables(Ld, MLA_ROPE, LANES, MLA_NOPE)

    xp, xs = x_prompt, x_sample
    ak_l, av_l, ckv_l, kr_l, sr_l, si_l, lru_l = [], [], [], [], [], [], []
    for l in range(DEPTH):
        lw = _layer_weights(l, w)
        lat_mod = [mod[l, :Bd, n][:, None, :] for n in range(6)]
        ctx_mod = [mod[l, Bd:Bd + 1, n][:, None, :] for n in range(6)]

        zs5 = jnp.zeros((B // SUBLANES, 4, SUBLANES, S5_WIDTH), F32)
        zlru = jnp.zeros((B // SUBLANES, 2, SUBLANES, LRU_WIDTH), F32)
        xp, (k32, v32, ckv_n, krp), s5_fin, lru_fin = _mix(xp, ctx_mod[0:3], lw, None, None, zs5, zlru, True)
        xp = _moe(xp, ctx_mod[4], ctx_mod[3], ctx_mod[5], lw)
        ak_l.append(k32.reshape(B, L, A_KV_HEADS, A_HEAD_DIM))
        av_l.append(v32.reshape(B, L, A_KV_HEADS, A_HEAD_DIM))
        ckv_l.append(ckv_n)
        kr_l.append(krp[:, :, MLA_NOPE:MLA_QK])
        fin = _rows_to_state(s5_fin, S5_WIDTH)
        sr_l.append(fin[:, 0::2].reshape(B, 2, S5_GROUPS, S5_STATE))
        si_l.append(fin[:, 1::2].reshape(B, 2, S5_GROUPS, S5_STATE))
        lru_l.append(_rows_to_state(lru_fin, LRU_WIDTH))

        sre = state_ssm_re[:, l].reshape(Bd, 2, S5_WIDTH)
        sim = state_ssm_im[:, l].reshape(Bd, 2, S5_WIDTH)
        s5_h0 = _state_to_rows(jnp.stack([sre[:, 0], sim[:, 0], sre[:, 1], sim[:, 1]], axis=1))
        lru_h0 = _state_to_rows(state_lru[:, l])
        cache = (cache_attn_k[:, l].reshape(Bd, P, 128), cache_attn_v[:, l].reshape(Bd, P, 128),
                 cache_mla_ckv[:, l],
                 jnp.pad(cache_mla_krope[:, l], ((0, 0), (0, 0), (MLA_NOPE, LANES - MLA_QK))))
        xs, _, _, _ = _mix(xs, lat_mod[0:3], lw, rope_a + rope_m, cache, s5_h0, lru_h0, False)
        xs = _moe(xs, lat_mod[4], lat_mod[3], lat_mod[5], lw)

    stack = lambda ts: jnp.stack(ts, axis=1)
    return (xp, xs, stack(ak_l), stack(av_l), stack(ckv_l), stack(kr_l), stack(sr_l), stack(si_l), stack(lru_l))
```

```python
import functools
import math

import jax
import jax.numpy as jnp
import numpy as np
from jax import lax
from jax.experimental import pallas as pl
from jax.experimental.pallas import tpu as pltpu

F32 = jnp.float32
BF16 = jnp.bfloat16

D_MODEL = 1024
DEPTH = 2
GRID_W = 64
N_BRANCH = 4
BRANCH_W = 256
ROPE_BASE = 10000.0
EPS = 1e-6
NEG_INF = -1e30
A_HEADS = 4
A_KV_HEADS = 2
A_HEAD_DIM = 64
WINDOW = 128
Q_BLOCK = 128
S5_GROUP = 16
S5_GROUPS = 16
S5_STATE = 64
S5_WIDTH = S5_GROUPS * S5_STATE
LRU_WIDTH = 256
LRU_BLOCKS = 4
LRU_CONV = 4
LRU_C = 8.0
MLA_HEADS = 4
MLA_Q_LORA = 256
MLA_KV_LORA = 128
MLA_NOPE = 64
MLA_ROPE = 32
MLA_V = 64
MLA_QK = MLA_NOPE + MLA_ROPE
N_GROUPS = 4
EXPERTS_PER_GROUP = 4
N_EXPERTS = 16
EXPERT_FF = 256

LANES = 128
SUBLANES = 8
VMEM_LIMIT = 56 * 1024 * 1024

TOK_TILE = 256
SCAN_STEPS = 64
MOE_SORT_TILE = 256
MOE_ROW_ALIGN = 16
MOE_SORTED_ROWS = MOE_SORT_TILE + LANES
MOE_ROW_TILE = 256
MOE_RUN_SIZES = (256, 128, 64, 32, 16)
MLA_Q_TILE = 256
Z_COLS = 1792
HEAD_ORDER = np.array((0, 2, 1, 3))


def _params(sem):
    return pltpu.CompilerParams(dimension_semantics=sem, vmem_limit_bytes=VMEM_LIMIT)


def _full(shape):
    n = len(shape)
    return pl.BlockSpec(shape, lambda *_: (0,) * n)


def _modnorm(x, g, sc, sh):
    ms = jnp.mean(x * x, axis=-1, keepdims=True)
    return (x * lax.rsqrt(ms + EPS)) * g * (1.0 + sc) + sh


def _rmsnorm(x, g):
    ms = jnp.mean(x * x, axis=-1, keepdims=True)
    return (x * lax.rsqrt(ms + EPS)) * g


def _half_rmsnorm(x):
    lane = lax.broadcasted_iota(jnp.int32, (1, LANES), 1)
    lo = lane < A_HEAD_DIM
    outs = []
    for t in range(x.shape[1] // LANES):
        xt = x[:, LANES * t:LANES * (t + 1)]
        sq = xt * xt
        s_lo = jnp.sum(jnp.where(lo, sq, 0.0), axis=-1, keepdims=True)
        s_hi = jnp.sum(jnp.where(lo, 0.0, sq), axis=-1, keepdims=True)
        inv = jnp.where(lo, lax.rsqrt(s_lo * (1.0 / A_HEAD_DIM) + EPS),
                        lax.rsqrt(s_hi * (1.0 / A_HEAD_DIM) + EPS))
        outs.append(xt * inv)
    return jnp.concatenate(outs, axis=1)


def _tile_rmsnorm(x, n_real):
    outs = []
    for t in range(x.shape[1] // LANES):
        xt = x[:, LANES * t:LANES * (t + 1)]
        ss = jnp.sum(xt * xt, axis=-1, keepdims=True)
        outs.append(xt * lax.rsqrt(ss * (1.0 / n_real) + EPS))
    return jnp.concatenate(outs, axis=1)


def _rope(x, cos, sin_next, sin_prev, quarter):
    width = x.shape[1]
    return (x * cos + pltpu.roll(x, width - quarter, 1) * sin_next
            + pltpu.roll(x, quarter, 1) * sin_prev)


def _mla_kv(ckv_n, kr_placed, wukv, gk, rope_tabs):
    kv = jnp.dot(ckv_n.astype(BF16), wukv, preferred_element_type=F32)
    kfull = kv[:, :MLA_HEADS * LANES] + jnp.concatenate([kr_placed] * MLA_HEADS, axis=1)
    kn = _tile_rmsnorm(kfull, MLA_QK) * gk
    if rope_tabs is not None:
        kn = _rope(kn, *rope_tabs, MLA_ROPE // 4)
    return kn, kv[:, MLA_HEADS * LANES:]


def _mod_kernel(c_ref, w_ref, b_ref, o_ref):
    c = c_ref[...]
    s = c * jax.nn.sigmoid(c)
    o_ref[...] = jnp.dot(s.astype(BF16), w_ref[...], preferred_element_type=F32) + b_ref[...]


def _modulation(cond, w_mod, b_mod):
    n_rows = cond.shape[0]
    n_out = w_mod.shape[-1]
    tn = 1024
    return pl.pallas_call(
        _mod_kernel,
        grid=(DEPTH, n_out // tn),
        in_specs=[pl.BlockSpec((n_rows, D_MODEL), lambda l, j: (0, 0)),
                  pl.BlockSpec((None, D_MODEL, tn), lambda l, j: (l, 0, j)),
                  pl.BlockSpec((None, 1, tn), lambda l, j: (l, 0, j))],
        out_specs=pl.BlockSpec((None, n_rows, tn), lambda l, j: (l, 0, j)),
        out_shape=jax.ShapeDtypeStruct((DEPTH, n_rows, n_out), F32),
        compiler_params=_params(("arbitrary", "arbitrary")),
        name="modulation",
    )(cond, w_mod, b_mod)


def _in_proj_kernel(rope, ctx, *refs):
    it = iter(refs)
    x_ref, sc_ref, sh_ref, g1_ref, wz_ref = (next(it) for _ in range(5))
    gq_ref, gk_ref, qlg_ref, wuq_ref, gmq_ref, kvg_ref, wukv_ref, gmk_ref = (next(it) for _ in range(8))
    if rope:
        ca_ref, sna_ref, spa_ref, cm_ref, snm_ref, spm_ref = (next(it) for _ in range(6))
    q_ref, k_ref, v_ref, s5u_ref, lx_ref, lg_ref, qm_ref, km_ref, vm_ref = (next(it) for _ in range(9))
    if ctx:
        k32_ref, v32_ref, ckv_ref, krp_ref = (next(it) for _ in range(4))

    h = _modnorm(x_ref[...], g1_ref[...], sc_ref[...], sh_ref[...])
    z = jnp.dot(h.astype(BF16), wz_ref[...], preferred_element_type=F32)

    q = _half_rmsnorm(z[:, 0:256]) * gq_ref[...]
    k = _half_rmsnorm(z[:, 256:384]) * gk_ref[...]
    v = z[:, 384:512]
    if ctx:
        k32_ref[...] = k
        v32_ref[...] = v
    if rope:
        quarter = A_HEAD_DIM // 4
        q = _rope(q, ca_ref[...], sna_ref[...], spa_ref[...], quarter)
        k = _rope(k, ca_ref[:, :LANES], sna_ref[:, :LANES], spa_ref[:, :LANES], quarter)
    q_ref[...] = q.astype(BF16)
    k_ref[...] = k.astype(BF16)
    v_ref[...] = v.astype(BF16)

    s5u_ref[...] = z[:, 512:768]
    lx_ref[...] = z[:, 768:1024]
    lg_ref[...] = z[:, 1024:1280]

    mtabs = None
    mtabs4 = None
    if rope:
        mtabs = (cm_ref[...], snm_ref[...], spm_ref[...])
        mtabs4 = tuple(jnp.concatenate([t] * MLA_HEADS, axis=1) for t in mtabs)
    ql = _rmsnorm(z[:, 1280:1536], qlg_ref[...])
    qm = jnp.dot(ql.astype(BF16), wuq_ref[...], preferred_element_type=F32)
    qm = _tile_rmsnorm(qm, MLA_QK) * gmq_ref[...]
    if rope:
        qm = _rope(qm, *mtabs4, MLA_ROPE // 4)
    qm_ref[...] = qm.astype(BF16)

    ckv_n = _rmsnorm(z[:, 1536:1664], kvg_ref[...])
    krp = z[:, 1664:1792]
    km, vm = _mla_kv(ckv_n, krp, wukv_ref[...], gmk_ref[...], mtabs4)
    km_ref[...] = km.astype(BF16)
    vm_ref[...] = vm.astype(BF16)
    if ctx:
        ckv_ref[...] = ckv_n
        krp_ref[...] = krp


def _in_proj(x, sc, sh, lw, rope_tabs, ctx):
    B, L, _ = x.shape
    Bg = B // SUBLANES
    TL = TOK_TILE
    rope = rope_tabs is not None
    per_b = sc.shape[0] > 1

    def mod_map(b, i):
        return (b if per_b else 0, 0, 0)

    tok = lambda c: pl.BlockSpec((None, TL, c), lambda b, i: (b, i, 0))
    tview = pl.BlockSpec((None, TL, 256), lambda b, i: (b // SUBLANES, i, b % SUBLANES))
    in_specs = [tok(D_MODEL), pl.BlockSpec((None, 1, D_MODEL), mod_map),
                pl.BlockSpec((None, 1, D_MODEL), mod_map), _full((1, D_MODEL)), _full((D_MODEL, Z_COLS)),
                _full((1, 256)), _full((1, 128)), _full((1, 256)), _full((256, 512)), _full((1, 512)),
                _full((1, 128)), _full((128, 768)), _full((1, 512))]
    args = [x, sc, sh, lw["g1"], lw["wz"], lw["gq"], lw["gk"], lw["qlg"], lw["wuq"], lw["gmq"],
            lw["kvg"], lw["wukv"], lw["gmk"]]
    if rope:
        in_specs += [pl.BlockSpec((TL, 256), lambda b, i: (i, 0))] * 3
        in_specs += [pl.BlockSpec((TL, 128), lambda b, i: (i, 0))] * 3
        args += list(rope_tabs)
    out_specs = [tok(256), tok(128), tok(128), tview, tview, tok(256), tok(512), tok(512), tok(256)]
    sds = jax.ShapeDtypeStruct
    out_shape = [sds((B, L, 256), BF16), sds((B, L, 128), BF16), sds((B, L, 128), BF16),
                 sds((Bg, L, SUBLANES * 256), F32), sds((Bg, L, SUBLANES * 256), F32),
                 sds((B, L, 256), F32), sds((B, L, 512), BF16), sds((B, L, 512), BF16),
                 sds((B, L, 256), BF16)]
    if ctx:
        out_specs += [tok(128), tok(128), tok(128), tok(128)]
        out_shape += [sds((B, L, 128), F32)] * 4
    return pl.pallas_call(
        functools.partial(_in_proj_kernel, rope, ctx),
        grid=(B, L // TL),
        in_specs=in_specs, out_specs=out_specs, out_shape=out_shape,
        compiler_params=_params(("parallel", "parallel")),
        name="in_proj_ctx" if ctx else "in_proj_lat",
    )(*args)


def _mla_cache_kernel(ckv_ref, krp_ref, wukv_ref, gmk_ref, km_ref, vm_ref):
    km, vm = _mla_kv(ckv_ref[...], krp_ref[...], wukv_ref[...], gmk_ref[...], None)
    km_ref[...] = km.astype(BF16)
    vm_ref[...] = vm.astype(BF16)


def _mla_cache_kv(cckv, ckr_placed, lw):
    B, P, _ = cckv.shape
    tok = lambda c: pl.BlockSpec((None, P, c), lambda b: (b, 0, 0))
    return pl.pallas_call(
        _mla_cache_kernel,
        grid=(B,),
        in_specs=[tok(128), tok(128), _full((128, 768)), _full((1, 512))],
        out_specs=[tok(512), tok(256)],
        out_shape=[jax.ShapeDtypeStruct((B, P, 512), BF16), jax.ShapeDtypeStruct((B, P, 256), BF16)],
        compiler_params=_params(("parallel",)),
        name="mla_cache_kv",
    )(cckv, ckr_placed, lw["wukv"], lw["gmk"])


def _sink_softmax(s, sink):
    m = jnp.maximum(jnp.max(s, axis=-1, keepdims=True), sink)
    e = jnp.exp(s - m)
    den = jnp.sum(e, axis=-1, keepdims=True) + jnp.exp(sink - m)
    return (e / den).astype(BF16)


def _gqa_tile(qt, keys, vst, sink_lo, sink_hi, mask):
    lane = lax.broadcasted_iota(jnp.int32, (1, LANES), 1)
    lo = lane < A_HEAD_DIM
    zero = jnp.zeros_like(qt)
    ps = []
    for qh, sink in ((jnp.where(lo, qt, zero), sink_lo), (jnp.where(lo, zero, qt), sink_hi)):
        s = lax.dot_general(qh, keys, (((1,), (1,)), ((), ())), preferred_element_type=F32)
        if mask is not None:
            s = jnp.where(mask, s, NEG_INF)
        ps.append(_sink_softmax(s, sink))
    return jnp.dot(jnp.concatenate(ps, axis=1), vst, preferred_element_type=F32)


def _stack_kv_halves(v):
    lane = lax.broadcasted_iota(jnp.int32, (1, LANES), 1)
    lo = lane < A_HEAD_DIM
    zero = jnp.zeros_like(v)
    return jnp.concatenate([jnp.where(lo, v, zero), jnp.where(lo, zero, v)], axis=0)


def _attn_a_ctx_kernel(sink_ref, q_ref, k_ref, v_ref, o_ref):
    keys = k_ref[...]
    vst = _stack_kv_halves(v_ref[...])
    for t in range(2):
        o = _gqa_tile(q_ref[:, LANES * t:LANES * (t + 1)], keys, vst, sink_ref[t], sink_ref[2 + t], None)
        o_ref[:, LANES * t:LANES * (t + 1)] = o.astype(o_ref.dtype)


def _attn_a_ctx(q, k, v, sink):
    B, L, _ = q.shape
    tok = lambda c: pl.BlockSpec((None, L, c), lambda b: (b, 0, 0))
    return pl.pallas_call(
        _attn_a_ctx_kernel,
        grid=(B,),
        in_specs=[pl.BlockSpec(memory_space=pltpu.SMEM), tok(256), tok(128), tok(128)],
        out_specs=tok(256),
        out_shape=jax.ShapeDtypeStruct((B, L, 256), BF16),
        compiler_params=_params(("parallel",)),
        name="attn_a_ctx",
    )(sink, q, k, v)


def _attn_a_lat_kernel(sink_ref, q_ref, kc_ref, vc_ref, k_ref, v_ref, o_ref):
    i = pl.program_id(1)
    L = k_ref.shape[0]
    n_ctx = kc_ref.shape[0]
    span = 3 * Q_BLOCK
    start = pl.multiple_of(jnp.clip((i - 1) * Q_BLOCK, 0, L - span), Q_BLOCK)
    keys = jnp.concatenate([kc_ref[...].astype(BF16), k_ref[pl.ds(start, span), :]], axis=0)
    vals = jnp.concatenate([vc_ref[...].astype(BF16), v_ref[pl.ds(start, span), :]], axis=0)
    vst = _stack_kv_halves(vals)
    col = lax.broadcasted_iota(jnp.int32, (Q_BLOCK, n_ctx + span), 1)
    row = lax.broadcasted_iota(jnp.int32, (Q_BLOCK, n_ctx + span), 0)
    rel = (col - n_ctx + start) - (row + i * Q_BLOCK)
    mask = (col < n_ctx) | (jnp.abs(rel) <= WINDOW)
    for t in range(2):
        o = _gqa_tile(q_ref[:, LANES * t:LANES * (t + 1)], keys, vst, sink_ref[t], sink_ref[2 + t], mask)
        o_ref[:, LANES * t:LANES * (t + 1)] = o.astype(o_ref.dtype)


def _attn_a_lat(q, k, v, kc, vc, sink):
    B, L, _ = q.shape
    P = kc.shape[1]
    blk = lambda c: pl.BlockSpec((None, Q_BLOCK, c), lambda b, i: (b, i, 0))
    whole = lambda n, c: pl.BlockSpec((None, n, c), lambda b, i: (b, 0, 0))
    return pl.pallas_call(
        _attn_a_lat_kernel,
        grid=(B, L // Q_BLOCK),
        in_specs=[pl.BlockSpec(memory_space=pltpu.SMEM), blk(256), whole(P, 128), whole(P, 128),
                  whole(L, 128), whole(L, 128)],
        out_specs=blk(256),
        out_shape=jax.ShapeDtypeStruct((B, L, 256), BF16),
        compiler_params=_params(("parallel", "parallel")),
        name="attn_a_lat",
    )(sink, q, kc, vc, k, v)


def _mla_attend(q, key_parts, val_parts):
    vals = val_parts[0] if len(val_parts) == 1 else jnp.concatenate(val_parts, axis=0)
    lane = lax.broadcasted_iota(jnp.int32, (1, MLA_HEADS * MLA_V), 1)
    zero = jnp.zeros_like(vals)
    vst = jnp.concatenate([jnp.where((lane >= MLA_V * h) & (lane < MLA_V * (h + 1)), vals, zero)
                           for h in range(MLA_HEADS)], axis=0)
    ps = []
    for h in range(MLA_HEADS):
        qh = q[:, LANES * h:LANES * (h + 1)]
        ss = [lax.dot_general(qh, kp[:, LANES * h:LANES * (h + 1)], (((1,), (1,)), ((), ())),
                              preferred_element_type=F32) for kp in key_parts]
        m = functools.reduce(jnp.maximum, [jnp.max(s, axis=-1, keepdims=True) for s in ss])
        es = [jnp.exp(s - m) for s in ss]
        den = functools.reduce(lambda a, b: a + b, [jnp.sum(e, axis=-1, keepdims=True) for e in es])
        ps += [(e / den).astype(BF16) for e in es]
    return jnp.dot(jnp.concatenate(ps, axis=1), vst, preferred_element_type=F32)


def _mla_ctx_kernel(q_ref, k_ref, v_ref, o_ref):
    o_ref[...] = _mla_attend(q_ref[...], [k_ref[...]], [v_ref[...]]).astype(o_ref.dtype)


def _mla_ctx(q, k, v):
    B, L, _ = q.shape
    tok = lambda c: pl.BlockSpec((None, L, c), lambda b: (b, 0, 0))
    return pl.pallas_call(
        _mla_ctx_kernel,
        grid=(B,),
        in_specs=[tok(512), tok(512), tok(256)],
        out_specs=tok(256),
        out_shape=jax.ShapeDtypeStruct((B, L, 256), BF16),
        compiler_params=_params(("parallel",)),
        name="mla_ctx",
    )(q, k, v)


def _mla_lat_kernel(q_ref, kc_ref, vc_ref, k_ref, v_ref, o_ref):
    o = _mla_attend(q_ref[...], [kc_ref[...], k_ref[...]], [vc_ref[...], v_ref[...]])
    o_ref[...] = o.astype(o_ref.dtype)


def _mla_lat(q, k, v, kc, vc):
    B, L, _ = q.shape
    P = kc.shape[1]
    TQ = MLA_Q_TILE
    blk = lambda c: pl.BlockSpec((None, TQ, c), lambda b, i: (b, i, 0))
    whole = lambda n, c: pl.BlockSpec((None, n, c), lambda b, i: (b, 0, 0))
    return pl.pallas_call(
        _mla_lat_kernel,
        grid=(B, L // TQ),
        in_specs=[blk(512), whole(P, 512), whole(P, 256), whole(L, 512), whole(L, 256)],
        out_specs=blk(256),
        out_shape=jax.ShapeDtypeStruct((B, L, 256), BF16),
        compiler_params=_params(("parallel", "parallel")),
        name="mla_lat",
    )(q, kc, vc, k, v)


def _s5_kernel(uf_ref, ub_ref, bf_ref, bb_ref, a_ref, h0_ref, cf_ref, cb_ref,
               yf_ref, yb_ref, fin_ref, s_ref, st_ref):
    i = pl.program_id(1)
    steps = SCAN_STEPS

    @pl.when(i == 0)
    def _():
        st_ref[...] = h0_ref[...]

    uf = uf_ref[...].astype(BF16)
    ub = ub_ref[...].astype(BF16)
    s_ref[0] = jnp.dot(uf, bf_ref[:, :S5_WIDTH], preferred_element_type=F32)
    s_ref[1] = jnp.dot(uf, bf_ref[:, S5_WIDTH:], preferred_element_type=F32)
    s_ref[2] = jnp.dot(ub, bb_ref[:, :S5_WIDTH], preferred_element_type=F32)
    s_ref[3] = jnp.dot(ub, bb_ref[:, S5_WIDTH:], preferred_element_type=F32)

    lane_chunk = 2 * LANES
    for c in range(S5_WIDTH // lane_chunk):
        sl = slice(lane_chunk * c, lane_chunk * (c + 1))
        arf, aif, arb, aib = (a_ref[n, :, sl] for n in range(4))

        def body(j, carry):
            hrf, hif, hrb, hib = carry
            rf = pl.multiple_of(j * SUBLANES, SUBLANES)
            rb = pl.multiple_of((steps - 1 - j) * SUBLANES, SUBLANES)
            nrf = arf * hrf - aif * hif + s_ref[0, pl.ds(rf, SUBLANES), sl]
            nif = arf * hif + aif * hrf + s_ref[1, pl.ds(rf, SUBLANES), sl]
            nrb = arb * hrb - aib * hib + s_ref[2, pl.ds(rb, SUBLANES), sl]
            nib = arb * hib + aib * hrb + s_ref[3, pl.ds(rb, SUBLANES), sl]
            s_ref[0, pl.ds(rf, SUBLANES), sl] = nrf
            s_ref[1, pl.ds(rf, SUBLANES), sl] = nif
            s_ref[2, pl.ds(rb, SUBLANES), sl] = nrb
            s_ref[3, pl.ds(rb, SUBLANES), sl] = nib
            return nrf, nif, nrb, nib

        fin = lax.fori_loop(0, steps, body, tuple(st_ref[n, :, sl] for n in range(4)), unroll=4)
        for n in range(4):
            st_ref[n, :, sl] = fin[n]

    hf = jnp.concatenate([s_ref[0].astype(BF16), s_ref[1].astype(BF16)], axis=1)
    yf_ref[...] = jnp.dot(hf, cf_ref[...], preferred_element_type=F32)
    hb = jnp.concatenate([s_ref[2].astype(BF16), s_ref[3].astype(BF16)], axis=1)
    yb_ref[...] = jnp.dot(hb, cb_ref[...], preferred_element_type=F32)
    fin_ref[...] = st_ref[...]


def _s5_scan(u_rows, lw, h0):
    Bg, rows, _ = u_rows.shape
    R = SCAN_STEPS * SUBLANES
    n = rows // R
    fwd = pl.BlockSpec((None, R, 256), lambda g, i: (g, i, 0))
    bwd = pl.BlockSpec((None, R, 256), lambda g, i: (g, n - 1 - i, 0))
    st = pl.BlockSpec((None, 4, SUBLANES, S5_WIDTH), lambda g, i: (g, 0, 0, 0))
    return pl.pallas_call(
        _s5_kernel,
        grid=(Bg, n),
        in_specs=[fwd, bwd, _full((256, 2 * S5_WIDTH)), _full((256, 2 * S5_WIDTH)),
                  _full((4, SUBLANES, S5_WIDTH)), st, _full((2 * S5_WIDTH, 256)), _full((2 * S5_WIDTH, 256))],
        out_specs=[fwd, bwd, st],
        out_shape=[jax.ShapeDtypeStruct((Bg, rows, 256), F32), jax.ShapeDtypeStruct((Bg, rows, 256), F32),
                   jax.ShapeDtypeStruct((Bg, 4, SUBLANES, S5_WIDTH), F32)],
        scratch_shapes=[pltpu.VMEM((4, R, S5_WIDTH), F32), pltpu.VMEM((4, SUBLANES, S5_WIDTH), F32)],
        compiler_params=_params(("parallel", "arbitrary")),
        name="s5_scan",
    )(u_rows, u_rows, lw["s5_bf"], lw["s5_bb"], lw["s5_a"], h0, lw["s5_cf"], lw["s5_cb"])


def _lru_gates(x_ref, pre_ref, post_ref, has_pre, has_post, cw_ref, cb_ref, w_ref, b_ref, sp_ref, a_ref, h_ref):
    R = x_ref.shape[0]
    pre = jnp.where(has_pre, pre_ref[...], 0.0)
    post = jnp.where(has_post, post_ref[...], 0.0)
    xp = jnp.concatenate([pre, x_ref[...], post], axis=0)
    xc = cb_ref[...]
    for t in range(LRU_CONV):
        xc = xc + xp[SUBLANES * t:SUBLANES * t + R] * cw_ref[t:t + 1, :]
    g = jnp.dot(xc.astype(BF16), w_ref[...], preferred_element_type=F32) + b_ref[...]
    r = jax.nn.sigmoid(g[:, :LRU_WIDTH])
    ig = jax.nn.sigmoid(g[:, LRU_WIDTH:])
    log_a = (-LRU_C) * r * sp_ref[...]
    a = jnp.exp(log_a)
    a_ref[...] = a
    h_ref[...] = jnp.sqrt(1.0 - a * a) * (ig * xc)


def _lru_kernel(xf_ref, xfp_ref, xfn_ref, xb_ref, xbp_ref, xbn_ref, cw_ref, cb_ref, wf_ref, wb_ref,
                bf_ref, bb_ref, spf_ref, spb_ref, h0_ref, hf_ref, hb_ref, fin_ref, af_ref, ab_ref, st_ref):
    i = pl.program_id(1)
    n = pl.num_programs(1)
    steps = SCAN_STEPS

    @pl.when(i == 0)
    def _():
        st_ref[...] = h0_ref[...]

    _lru_gates(xf_ref, xfp_ref, xfn_ref, i > 0, i < n - 1, cw_ref, cb_ref, wf_ref, bf_ref, spf_ref, af_ref, hf_ref)
    _lru_gates(xb_ref, xbp_ref, xbn_ref, i < n - 1, i > 0, cw_ref, cb_ref, wb_ref, bb_ref, spb_ref, ab_ref, hb_ref)

    def body(j, carry):
        hf, hb = carry
        rf = pl.multiple_of(j * SUBLANES, SUBLANES)
        rb = pl.multiple_of((steps - 1 - j) * SUBLANES, SUBLANES)
        nf = af_ref[pl.ds(rf, SUBLANES), :] * hf + hf_ref[pl.ds(rf, SUBLANES), :]
        nb = ab_ref[pl.ds(rb, SUBLANES), :] * hb + hb_ref[pl.ds(rb, SUBLANES), :]
        hf_ref[pl.ds(rf, SUBLANES), :] = nf
        hb_ref[pl.ds(rb, SUBLANES), :] = nb
        return nf, nb

    ff, fb = lax.fori_loop(0, steps, body, (st_ref[0], st_ref[1]), unroll=8)
    st_ref[0] = ff
    st_ref[1] = fb
    fin_ref[...] = st_ref[...]


def _lru_scan(x_rows, lw, h0):
    Bg, rows, _ = x_rows.shape
    R = SCAN_STEPS * SUBLANES
    n = rows // R
    pre_rows = 2 * SUBLANES
    fwd = lambda g, i: (g, i, 0)
    bwd = lambda g, i: (g, n - 1 - i, 0)
    blk = lambda m: pl.BlockSpec((None, R, 256), m)
    pre = lambda m: pl.BlockSpec(
        (None, pre_rows, 256), lambda g, i: (g, jnp.maximum(m(g, i)[1] * (R // pre_rows) - 1, 0), 0))
    post = lambda m: pl.BlockSpec(
        (None, SUBLANES, 256),
        lambda g, i: (g, jnp.minimum((m(g, i)[1] + 1) * (R // SUBLANES), rows // SUBLANES - 1), 0))
    st = pl.BlockSpec((None, 2, SUBLANES, 256), lambda g, i: (g, 0, 0, 0))
    return pl.pallas_call(
        _lru_kernel,
        grid=(Bg, n),
        in_specs=[blk(fwd), pre(fwd), post(fwd), blk(bwd), pre(bwd), post(bwd),
                  _full((LRU_CONV, 256)), _full((1, 256)), _full((256, 512)), _full((256, 512)),
                  _full((1, 512)), _full((1, 512)), _full((1, 256)), _full((1, 256)), st],
        out_specs=[blk(fwd), blk(bwd), st],
        out_shape=[jax.ShapeDtypeStruct((Bg, rows, 256), F32), jax.ShapeDtypeStruct((Bg, rows, 256), F32),
                   jax.ShapeDtypeStruct((Bg, 2, SUBLANES, 256), F32)],
        scratch_shapes=[pltpu.VMEM((R, 256), F32), pltpu.VMEM((R, 256), F32),
                        pltpu.VMEM((2, SUBLANES, 256), F32)],
        compiler_params=_params(("parallel", "arbitrary")),
        name="lru_scan",
    )(x_rows, x_rows, x_rows, x_rows, x_rows, x_rows, lw["lru_cw"], lw["lru_cb"], lw["lru_wf"], lw["lru_wb"],
      lw["lru_bf"], lw["lru_bb"], lw["lru_spf"], lw["lru_spb"], h0)


def _merge_kernel(x_ref, sc_ref, sh_ref, gt_ref, g1_ref, wg_ref, oa_ref, u_ref, yf_ref, yb_ref, d_ref,
                  wglu_ref, hf_ref, hb_ref, lg_ref, od_ref, wb_ref, wo_ref, o_ref):
    x = x_ref[...]
    h = _modnorm(x, g1_ref[...], sc_ref[...], sh_ref[...]).astype(BF16)

    yb5 = jax.nn.gelu(d_ref[...] * u_ref[...] + yf_ref[...] + yb_ref[...])
    gv = jnp.dot(yb5.astype(BF16), wglu_ref[...], preferred_element_type=F32)
    o_b = gv[:, :BRANCH_W] * jax.nn.sigmoid(gv[:, BRANCH_W:])
    o_c = (hf_ref[...] + hb_ref[...]) * jax.nn.gelu(lg_ref[...])
    branches = (oa_ref[...], o_b.astype(BF16), o_c.astype(BF16), od_ref[...])

    acc = jnp.zeros(x.shape, F32)
    for n in range(N_BRANCH):
        gate = jnp.dot(h, wg_ref[:, D_MODEL * n:D_MODEL * (n + 1)], preferred_element_type=F32)
        proj = jnp.dot(branches[n], wb_ref[n], preferred_element_type=F32)
        acc = acc + jax.nn.sigmoid(gate) * proj
    out = jnp.dot(acc.astype(BF16), wo_ref[...], preferred_element_type=F32)
    o_ref[...] = x + gt_ref[...] * out


def _merge(x, sc, sh, gt, lw, o_a, s5u_t, yf_t, yb_t, hf_t, hb_t, lg, o_d):
    B, L, _ = x.shape
    TL = TOK_TILE
    per_b = sc.shape[0] > 1

    def mod_map(b, i):
        return (b if per_b else 0, 0, 0)

    tok = lambda c: pl.BlockSpec((None, TL, c), lambda b, i: (b, i, 0))
    tview = pl.BlockSpec((None, TL, 256), lambda b, i: (b // SUBLANES, i, b % SUBLANES))
    mod = pl.BlockSpec((None, 1, D_MODEL), mod_map)
    return pl.pallas_call(
        _merge_kernel,
        grid=(B, L // TL),
        in_specs=[tok(D_MODEL), mod, mod, mod, _full((1, D_MODEL)), _full((D_MODEL, N_BRANCH * D_MODEL)),
                  tok(256), tview, tview, tview, _full((1, 256)), _full((256, 512)),
                  tview, tview, tok(256), tok(256), _full((N_BRANCH, BRANCH_W, D_MODEL)),
                  _full((D_MODEL, D_MODEL))],
        out_specs=tok(D_MODEL),
        out_shape=jax.ShapeDtypeStruct((B, L, D_MODEL), F32),
        compiler_params=_params(("parallel", "parallel")),
        name="merge",
    )(x, sc, sh, gt, lw["g1"], lw["wg"], o_a, s5u_t, yf_t, yb_t, lw["s5_d"], lw["wglu"],
      hf_t, hb_t, lg, o_d, lw["wb"], lw["wo"])


def _first_index(values, target):
    idx = jnp.full_like(target, float(len(values) - 1))
    for n in range(len(values) - 2, -1, -1):
        idx = jnp.where(values[n] == target, float(n), idx)
    return idx


def _list_max(values):
    return functools.reduce(jnp.maximum, values)


def _moe_sort_kernel(x_ref, sc_ref, sh_ref, g2_ref, wrh_ref, wrl_ref, br_ref, tri_ref,
                     hs_ref, gs_ref, dcol_ref, meta_ref):
    T = MOE_SORT_TILE
    h = _modnorm(x_ref[...], g2_ref[...], sc_ref[...], sh_ref[...])
    hh = h.astype(BF16)
    hl = (h - hh.astype(F32)).astype(BF16)
    nt = (((1,), (1,)), ((), ()))
    lt = (lax.dot_general(wrh_ref[...], hh, nt, preferred_element_type=F32)
          + lax.dot_general(wrh_ref[...], hl, nt, preferred_element_type=F32)
          + lax.dot_general(wrl_ref[...], hh, nt, preferred_element_type=F32)) + br_ref[...]
    gl = [lt[g:g + 1, :] for g in range(N_GROUPS)]
    gmax = _list_max(gl)
    g_idx = _first_index(gl, gmax)
    pg = 1.0 / sum(jnp.exp(v - gmax) for v in gl)
    hot = [g_idx == float(g) for g in range(N_GROUPS)]
    el = []
    for e in range(EXPERTS_PER_GROUP):
        v = jnp.zeros_like(gmax)
        for g in range(N_GROUPS):
            r = N_GROUPS + EXPERTS_PER_GROUP * g + e
            v = jnp.where(hot[g], lt[r:r + 1, :], v)
        el.append(v)
    emax = _list_max(el)
    ee = [jnp.exp(v - emax) for v in el]
    esum = sum(ee)
    pe = [v / esum for v in ee]
    v1 = _list_max(pe)
    i1 = _first_index(pe, v1)
    pe2 = [jnp.where(i1 == float(e), -1.0, pe[e]) for e in range(EXPERTS_PER_GROUP)]
    v2 = _list_max(pe2)
    i2 = _first_index(pe2, v2)
    tot = v1 + v2
    w = [jnp.where(i1 == float(e), pg * v1 / tot, jnp.where(i2 == float(e), pg * v2 / tot, 0.0))
         for e in range(EXPERTS_PER_GROUP)]

    zero_row = jnp.zeros_like(gmax)
    g8 = jnp.concatenate([jnp.where(hot[g], 1.0, 0.0) for g in range(N_GROUPS)] + [zero_row] * 4, axis=0)
    cum = jnp.dot(g8.astype(BF16), tri_ref[...], preferred_element_type=F32)
    off = jnp.zeros((1, 1), F32)
    dest = zero_row
    counts = []
    for g in range(N_GROUPS):
        cnt = cum[g:g + 1, T - 1:T]
        padded = jnp.floor((cnt + (MOE_ROW_ALIGN - 1.0)) * (1.0 / MOE_ROW_ALIGN)) * MOE_ROW_ALIGN
        dest = jnp.where(hot[g], off + cum[g:g + 1, :] - 1.0, dest)
        off = off + padded
        counts.append(padded)
    rows = lax.broadcasted_iota(jnp.int32, (MOE_SORTED_ROWS, T), 0).astype(F32)
    perm = jnp.where(rows == dest, 1.0, 0.0).astype(BF16)
    hs_ref[...] = jnp.dot(perm, hh, preferred_element_type=F32).astype(BF16)
    gates = jnp.concatenate(w + [jnp.zeros((LANES - EXPERTS_PER_GROUP, T), F32)], axis=0)
    ghi = gates.astype(BF16)
    glo = (gates - ghi.astype(F32)).astype(BF16)
    gs_ref[...] = (lax.dot_general(perm, ghi, nt, preferred_element_type=F32)
                   + lax.dot_general(perm, glo, nt, preferred_element_type=F32))
    r_i = lax.broadcasted_iota(jnp.int32, (T, T), 0)
    c_i = lax.broadcasted_iota(jnp.int32, (T, T), 1)
    dcol = jnp.sum(jnp.where(r_i == c_i, jnp.broadcast_to(dest, (T, T)), 0.0), axis=1, keepdims=True)
    dcol_ref[...] = jnp.broadcast_to(dcol, (T, LANES))
    meta_ref[...] = jnp.concatenate([jnp.broadcast_to(c, (1, LANES)) for c in counts]
                                    + [jnp.zeros((SUBLANES - N_GROUPS, LANES), F32)], axis=0)


def _moe_sort(xf, sc, sh, lw, tiles_per_mod):
    N = xf.shape[0]
    T = MOE_SORT_TILE
    n_tiles = N // T
    per_b = sc.shape[0] > 1
    mod = pl.BlockSpec((None, 1, D_MODEL), lambda t: (t // tiles_per_mod if per_b else 0, 0, 0))
    sds = jax.ShapeDtypeStruct
    return pl.pallas_call(
        _moe_sort_kernel,
        grid=(n_tiles,),
        in_specs=[pl.BlockSpec((T, D_MODEL), lambda t: (t, 0)), mod, mod, _full((1, D_MODEL)),
                  _full((LANES, D_MODEL)), _full((LANES, D_MODEL)), _full((LANES, 1)), _full((T, T))],
        out_specs=[pl.BlockSpec((MOE_SORTED_ROWS, D_MODEL), lambda t: (t, 0)),
                   pl.BlockSpec((MOE_SORTED_ROWS, LANES), lambda t: (t, 0)),
                   pl.BlockSpec((T, LANES), lambda t: (t, 0)),
                   pl.BlockSpec((None, SUBLANES, LANES), lambda t: (t, 0, 0))],
        out_shape=[sds((n_tiles * MOE_SORTED_ROWS, D_MODEL), BF16), sds((n_tiles * MOE_SORTED_ROWS, LANES), F32),
                   sds((N, LANES), F32), sds((n_tiles, SUBLANES, LANES), F32)],
        compiler_params=_params(("parallel",)),
        name="moe_sort",
    )(xf, sc, sh, lw["g2"], lw["wrt_hi"], lw["wrt_lo"], lw["brt"], lw["tri"])


def _run_copy_kernel(n_arrays, cnt_ref, src_ref, dst_ref, *refs):
    srcs = refs[:n_arrays]
    dsts = refs[2 * n_arrays:3 * n_arrays]
    sem = refs[3 * n_arrays]
    n_runs = cnt_ref.shape[0]

    def for_each_piece(r, fn):
        cnt = cnt_ref[r]
        s = src_ref[r]
        d = dst_ref[r]
        for size in MOE_RUN_SIZES:
            take = (cnt & size) != 0

            @pl.when(take)
            def _(s=s, d=d, size=size):
                for k in range(n_arrays):
                    fn(pltpu.make_async_copy(
                        srcs[k].at[pl.ds(pl.multiple_of(s, MOE_ROW_ALIGN), size)],
                        dsts[k].at[pl.ds(pl.multiple_of(d, MOE_ROW_ALIGN), size)], sem.at[k]))

            step = jnp.where(take, size, 0)
            s = s + step
            d = d + step

    def body(r, carry):
        for_each_piece(r, lambda cp: cp.start())

        @pl.when(r > 0)
        def _():
            for_each_piece(r - 1, lambda cp: cp.wait())

        return carry

    lax.fori_loop(0, n_runs, body, 0)
    for_each_piece(n_runs - 1, lambda cp: cp.wait())


def _run_copy(cnt, src, dst, arrays, out_rows, name):
    n = len(arrays)
    zeros = [jnp.zeros((out_rows,) + a.shape[1:], a.dtype) for a in arrays]
    smem = pl.BlockSpec(memory_space=pltpu.SMEM)
    hbm = pl.BlockSpec(memory_space=pl.ANY)
    return pl.pallas_call(
        functools.partial(_run_copy_kernel, n),
        in_specs=[smem] * 3 + [hbm] * (2 * n),
        out_specs=[hbm] * n,
        out_shape=[jax.ShapeDtypeStruct(z.shape, z.dtype) for z in zeros],
        scratch_shapes=[pltpu.SemaphoreType.DMA((n,))],
        input_output_aliases={3 + n + k: k for k in range(n)},
        name=name,
    )(cnt, src, dst, *arrays, *zeros)


def _moe_expert_kernel(grp_ref, valid_ref, last_ref, h_ref, g_ref, wga_ref, wup_ref, wdn_ref, y_ref):
    m = pl.program_id(0)

    @pl.when(valid_ref[m] != 0)
    def _():
        h = h_ref[...]
        acc = jnp.zeros((h.shape[0], D_MODEL), F32)
        for e in range(EXPERTS_PER_GROUP):
            a = jnp.dot(h, wga_ref[e], preferred_element_type=F32)
            u = jnp.dot(h, wup_ref[e], preferred_element_type=F32)
            act = (a * jax.nn.sigmoid(a)) * u * g_ref[:, e:e + 1]
            acc = acc + jnp.dot(act.astype(BF16), wdn_ref[e], preferred_element_type=F32)
        y_ref[...] = acc.astype(BF16)

    @pl.when(valid_ref[m] == 0)
    def _():
        y_ref[...] = jnp.zeros_like(y_ref)


def _moe_experts(grp, valid, last, hg, gg, lw):
    rows = hg.shape[0]
    TM = MOE_ROW_TILE
    tok = lambda c: pl.BlockSpec((TM, c), lambda m, grp, valid, last: (jnp.minimum(m, last[0]), 0))
    wspec = lambda r, c: pl.BlockSpec((EXPERTS_PER_GROUP, r, c), lambda m, grp, valid, last: (grp[m], 0, 0))
    return pl.pallas_call(
        _moe_expert_kernel,
        grid_spec=pltpu.PrefetchScalarGridSpec(
            num_scalar_prefetch=3, grid=(rows // TM,),
            in_specs=[tok(D_MODEL), tok(LANES), wspec(D_MODEL, EXPERT_FF), wspec(D_MODEL, EXPERT_FF),
                      wspec(EXPERT_FF, D_MODEL)],
            out_specs=pl.BlockSpec((TM, D_MODEL), lambda m, grp, valid, last: (m, 0))),
        out_shape=jax.ShapeDtypeStruct((rows, D_MODEL), BF16),
        compiler_params=_params(("arbitrary",)),
        name="moe_experts",
    )(grp, valid, last, hg, gg, lw["wga"], lw["wup"], lw["wdn"])


def _moe_unsort_kernel(x_ref, gt_ref, ys_ref, dcol_ref, o_ref):
    cols = lax.broadcasted_iota(jnp.int32, (MOE_SORT_TILE, MOE_SORTED_ROWS), 1).astype(F32)
    perm_t = jnp.where(cols == dcol_ref[:, 0:1], 1.0, 0.0).astype(BF16)
    y = jnp.dot(perm_t, ys_ref[...], preferred_element_type=F32)
    o_ref[...] = x_ref[...] + gt_ref[...] * y


def _moe_unsort(xf, gt, ys, dcol, tiles_per_mod):
    N = xf.shape[0]
    T = MOE_SORT_TILE
    per_b = gt.shape[0] > 1
    mod = pl.BlockSpec((None, 1, D_MODEL), lambda t: (t // tiles_per_mod if per_b else 0, 0, 0))
    tok = pl.BlockSpec((T, D_MODEL), lambda t: (t, 0))
    return pl.pallas_call(
        _moe_unsort_kernel,
        grid=(N // T,),
        in_specs=[tok, mod, pl.BlockSpec((MOE_SORTED_ROWS, D_MODEL), lambda t: (t, 0)),
                  pl.BlockSpec((T, LANES), lambda t: (t, 0))],
        out_specs=tok,
        out_shape=jax.ShapeDtypeStruct((N, D_MODEL), F32),
        compiler_params=_params(("parallel",)),
        name="moe_unsort",
    )(xf, gt, ys, dcol)


def _moe(x, sc, sh, gt, lw):
    B, L, _ = x.shape
    N = B * L
    T, TM = MOE_SORT_TILE, MOE_ROW_TILE
    n_tiles = N // T
    xf = x.reshape(N, D_MODEL)
    hs, gs, dcol, meta = _moe_sort(xf, sc, sh, lw, L // T)

    cnt = meta[:, :N_GROUPS, 0].astype(jnp.int32)
    tile_off = jnp.arange(n_tiles, dtype=jnp.int32)[:, None] * MOE_SORTED_ROWS + jnp.cumsum(cnt, axis=1) - cnt
    total = jnp.sum(cnt, axis=0)
    padded_total = (total + (TM - 1)) // TM * TM
    ends = jnp.cumsum(padded_total)
    grp_off = (ends - padded_total)[None, :] + jnp.cumsum(cnt, axis=0) - cnt
    cap = (-(-(N + n_tiles * N_GROUPS * (MOE_ROW_ALIGN - 1)) // TM) + N_GROUPS) * TM
    first_row = jnp.arange(cap // TM, dtype=jnp.int32) * TM
    grp = jnp.minimum(jnp.sum(first_row[:, None] >= ends[None, :], axis=1), N_GROUPS - 1).astype(jnp.int32)
    valid = (first_row < ends[-1]).astype(jnp.int32)
    last = (ends[-1:] // TM - 1).astype(jnp.int32)
    flat = lambda a: a.reshape(-1).astype(jnp.int32)

    hg, gg = _run_copy(flat(cnt), flat(tile_off), flat(grp_off), [hs, gs], cap, "moe_gather")
    yg = _moe_experts(grp, valid, last, hg, gg, lw)
    (ys,) = _run_copy(flat(cnt), flat(grp_off), flat(tile_off), [yg], n_tiles * MOE_SORTED_ROWS, "moe_scatter")
    out = _moe_unsort(xf, gt, ys, dcol, L // T)
    return out.reshape(B, L, D_MODEL)


def _block_diag(blocks):
    n, r, c = blocks.shape
    eye = jnp.eye(n, dtype=blocks.dtype)
    return jnp.einsum("nrc,nm->nrmc", blocks, eye).reshape(n * r, n * c)


def _rope_tables(seq_len, dim, width, offset):
    rows = seq_len // GRID_W
    r, col = jnp.meshgrid(jnp.arange(rows), jnp.arange(GRID_W), indexing="ij")
    r = r.reshape(-1).astype(F32)
    col = col.reshape(-1).astype(F32)
    quarter = dim // 4
    freqs = ROPE_BASE ** (-jnp.arange(quarter, dtype=F32) / quarter)
    ang_r = r[:, None] * freqs
    ang_c = col[:, None] * freqs
    zero = jnp.zeros_like(ang_r)
    cos = jnp.cos(jnp.concatenate([ang_r, ang_r, ang_c, ang_c], axis=-1))
    sin_next = jnp.concatenate([-jnp.sin(ang_r), zero, -jnp.sin(ang_c), zero], axis=-1)
    sin_prev = jnp.concatenate([zero, jnp.sin(ang_r), zero, jnp.sin(ang_c)], axis=-1)

    def place(t, fill):
        return jnp.pad(t, ((0, 0), (offset, width - offset - dim)), constant_values=fill)

    return place(cos, 1.0), place(sin_next, 0.0), place(sin_prev, 0.0)


def _layer_weights(l, w):
    lw = {}
    w_in = w["w_in"][l]
    q_cols = w_in[:, 0:256].reshape(D_MODEL, A_HEADS, A_HEAD_DIM)[:, HEAD_ORDER].reshape(D_MODEL, 256)
    kr_cols = jnp.pad(w_in[:, 1664:1696], ((0, 0), (MLA_NOPE, LANES - MLA_QK)))
    lw["wz"] = jnp.concatenate([q_cols, w_in[:, 256:1664], kr_cols], axis=1).astype(BF16)
    lw["wg"] = w_in[:, 1696:].astype(BF16)
    lw["g1"] = w["norm1_g"][l][None]
    lw["g2"] = w["norm2_g"][l][None]
    lw["gq"] = jnp.tile(w["a_qnorm_g"][l], A_HEADS)[None] * (A_HEAD_DIM ** -0.5)
    lw["gk"] = jnp.tile(w["a_knorm_g"][l], A_KV_HEADS)[None]
    lw["sink"] = w["a_sink"][l]
    lw["qlg"] = w["mla_q_lat_norm"][l][None]
    wuq = w["mla_w_uq"][l].reshape(MLA_Q_LORA, MLA_HEADS, MLA_QK)
    lw["wuq"] = jnp.pad(wuq, ((0, 0), (0, 0), (0, LANES - MLA_QK))).reshape(MLA_Q_LORA, MLA_HEADS * LANES).astype(BF16)
    pad_g = lambda g: jnp.tile(jnp.pad(g, (0, LANES - MLA_QK)), MLA_HEADS)[None]
    lw["gmq"] = pad_g(w["mla_qnorm_g"][l]) * (MLA_QK ** -0.5)
    lw["gmk"] = pad_g(w["mla_knorm_g"][l])
    lw["kvg"] = w["mla_kv_norm"][l][None]
    wukv = w["mla_w_ukv"][l].reshape(MLA_KV_LORA, MLA_HEADS, MLA_NOPE + MLA_V)
    wk = jnp.pad(wukv[:, :, :MLA_NOPE], ((0, 0), (0, 0), (0, LANES - MLA_NOPE))).reshape(MLA_KV_LORA, MLA_HEADS * LANES)
    wv = wukv[:, :, MLA_NOPE:].reshape(MLA_KV_LORA, MLA_HEADS * MLA_V)
    lw["wukv"] = jnp.concatenate([wk, wv], axis=1).astype(BF16)

    lre = w["s5_lambda_re"][l]
    lim = w["s5_lambda_im"][l]
    dt = jnp.exp(w["s5_log_step"][l])[:, :, None]
    mag = jnp.exp(lre * dt)
    ar, ai = mag * jnp.cos(lim * dt), mag * jnp.sin(lim * dt)
    den = lre * lre + lim * lim
    fr = ((ar - 1.0) * lre + ai * lim) / den
    fi = (ai * lre - (ar - 1.0) * lim) / den
    br, bi = w["s5_b_re"][l], w["s5_b_im"][l]
    bbr = fr[..., None] * br - fi[..., None] * bi
    bbi = fr[..., None] * bi + fi[..., None] * br
    in_map = lambda d: jnp.concatenate(
        [_block_diag(jnp.swapaxes(bbr[d], 1, 2)), _block_diag(jnp.swapaxes(bbi[d], 1, 2))], axis=1).astype(BF16)
    out_map = lambda d: jnp.concatenate(
        [_block_diag(jnp.swapaxes(w["s5_c_re"][l][d], 1, 2)),
         -_block_diag(jnp.swapaxes(w["s5_c_im"][l][d], 1, 2))], axis=0).astype(BF16)
    lw["s5_bf"], lw["s5_bb"] = in_map(0), in_map(1)
    lw["s5_cf"], lw["s5_cb"] = out_map(0), out_map(1)
    coef = jnp.stack([ar[0].reshape(-1), ai[0].reshape(-1), ar[1].reshape(-1), ai[1].reshape(-1)])
    lw["s5_a"] = jnp.broadcast_to(coef[:, None, :], (4, SUBLANES, S5_WIDTH))
    lw["s5_d"] = w["s5_d"][l][None]
    lw["wglu"] = w["s5_w_glu"][l].astype(BF16)

    lw["lru_cw"] = w["lru_conv_w"][l]
    lw["lru_cb"] = w["lru_conv_b"][l][None]
    gate_w = lambda d: jnp.concatenate(
        [_block_diag(w["lru_w_a"][l][d]), _block_diag(w["lru_w_x"][l][d])], axis=1).astype(BF16)
    gate_b = lambda d: jnp.concatenate([w["lru_b_a"][l][d], w["lru_b_x"][l][d]])[None]
    lw["lru_wf"], lw["lru_wb"] = gate_w(0), gate_w(1)
    lw["lru_bf"], lw["lru_bb"] = gate_b(0), gate_b(1)
    sp = jax.nn.softplus(-w["lru_lambda"][l])
    lw["lru_spf"], lw["lru_spb"] = sp[0][None], sp[1][None]

    wb = w["w_branch"][l]
    wb0 = wb[0].reshape(A_HEADS, A_HEAD_DIM, D_MODEL)[HEAD_ORDER].reshape(BRANCH_W, D_MODEL)
    lw["wb"] = jnp.concatenate([wb0[None], wb[1:]], axis=0).astype(BF16)
    lw["wo"] = w["w_out"][l].astype(BF16)

    wr = jnp.pad(jnp.concatenate([w["moe_w_group"][l], w["moe_w_expert"][l]], axis=1),
                 ((0, 0), (0, LANES - N_GROUPS - N_EXPERTS)))
    lw["wrt_hi"] = wr.T.astype(BF16)
    lw["wrt_lo"] = (wr.T - lw["wrt_hi"].astype(F32)).astype(BF16)
    lw["brt"] = jnp.pad(jnp.concatenate([w["moe_b_group"][l], w["moe_b_expert"][l]]),
                        (0, LANES - N_GROUPS - N_EXPERTS))[:, None]
    lw["tri"] = jnp.triu(jnp.ones((MOE_SORT_TILE, MOE_SORT_TILE), BF16))
    lw["wga"] = w["moe_w_gate"][l].astype(BF16)
    lw["wup"] = w["moe_w_up"][l].astype(BF16)
    lw["wdn"] = w["moe_w_down"][l].astype(BF16)
    return lw


def _rows_to_state(fin, width):
    Bg, n = fin.shape[0], fin.shape[1]
    return jnp.swapaxes(fin, 1, 2).reshape(Bg * SUBLANES, n, width)


def _state_to_rows(state):
    B, n, width = state.shape
    return jnp.swapaxes(state.reshape(B // SUBLANES, SUBLANES, n, width), 1, 2)


def _mix(x, mods, lw, rope_tabs, cache, s5_h0, lru_h0, ctx):
    sh1, sc1, gt1 = mods
    B, L, _ = x.shape
    Bg = B // SUBLANES
    outs = _in_proj(x, sc1, sh1, lw, rope_tabs, ctx)
    q, k, v, s5u_t, lx_t, lg, qm, km, vm = outs[:9]
    if ctx:
        o_a = _attn_a_ctx(q, k, v, lw["sink"])
        o_d = _mla_ctx(qm, km, vm)
    else:
        ck, cv, cckv, ckr = cache
        o_a = _attn_a_lat(q, k, v, ck, cv, lw["sink"])
        kc, vc = _mla_cache_kv(cckv, ckr, lw)
        o_d = _mla_lat(qm, km, vm, kc, vc)
    rows = lambda t: t.reshape(Bg, L * SUBLANES, 256)
    view = lambda t: t.reshape(Bg, L, SUBLANES * 256)
    yf, yb, s5_fin = _s5_scan(rows(s5u_t), lw, s5_h0)
    hf, hb, lru_fin = _lru_scan(rows(lx_t), lw, lru_h0)
    x = _merge(x, sc1, sh1, gt1, lw, o_a, s5u_t, view(yf), view(yb), view(hf), view(hb), lg, o_d)
    return x, outs[9:], s5_fin, lru_fin


def kernel(x_prompt, x_sample, cache_attn_k, cache_attn_v, cache_mla_ckv, cache_mla_krope, state_ssm_re, state_ssm_im, state_lru, c, c_ctx, w_mod, b_mod, norm1_g, norm2_g, w_in, a_qnorm_g, a_knorm_g, a_sink, s5_lambda_re, s5_lambda_im, s5_log_step, s5_b_re, s5_b_im, s5_c_re, s5_c_im, s5_d, s5_w_glu, lru_conv_w, lru_conv_b, lru_w_a, lru_b_a, lru_w_x, lru_b_x, lru_lambda, mla_q_lat_norm, mla_w_uq, mla_kv_norm, mla_w_ukv, mla_qnorm_g, mla_knorm_g, w_branch, w_out, moe_w_group, moe_b_group, moe_w_expert, moe_b_expert, moe_w_gate, moe_w_up, moe_w_down):
    w = dict(norm1_g=norm1_g, norm2_g=norm2_g, w_in=w_in, a_qnorm_g=a_qnorm_g, a_knorm_g=a_knorm_g,
             a_sink=a_sink, s5_lambda_re=s5_lambda_re, s5_lambda_im=s5_lambda_im, s5_log_step=s5_log_step,
             s5_b_re=s5_b_re, s5_b_im=s5_b_im, s5_c_re=s5_c_re, s5_c_im=s5_c_im, s5_d=s5_d, s5_w_glu=s5_w_glu,
             lru_conv_w=lru_conv_w, lru_conv_b=lru_conv_b, lru_w_a=lru_w_a, lru_b_a=lru_b_a, lru_w_x=lru_w_x,
             lru_b_x=lru_b_x, lru_lambda=lru_lambda, mla_q_lat_norm=mla_q_lat_norm, mla_w_uq=mla_w_uq,
             mla_kv_norm=mla_kv_norm, mla_w_ukv=mla_w_ukv, mla_qnorm_g=mla_qnorm_g, mla_knorm_g=mla_knorm_g,
             w_branch=w_branch, w_out=w_out, moe_w_group=moe_w_group, moe_b_group=moe_b_group,
             moe_w_expert=moe_w_expert, moe_b_expert=moe_b_expert, moe_w_gate=moe_w_gate, moe_w_up=moe_w_up,
             moe_w_down=moe_w_down)
    B, L, _ = x_prompt.shape
    Bd, Ld, _ = x_sample.shape
    P = cache_attn_k.shape[2]

    cond = jnp.zeros((2 * SUBLANES, D_MODEL), F32).at[:Bd].set(c).at[Bd].set(c_ctx)
    mod = _modulation(cond, w_mod.astype(BF16), b_mod[:, None, :])
    mod = mod.reshape(DEPTH, 2 * SUBLANES, 6, D_MODEL)

    rope_a = _rope_tables(Ld, A_HEAD_DIM, A_HEAD_DIM, 0)
    rope_a = tuple(jnp.tile(t, (1, A_HEADS)) for t in rope_a)
    rope_m = _rope_tables(Ld, MLA_ROPE, LANES, MLA_NOPE)

    xp, xs = x_prompt, x_sample
    ak_l, av_l, ckv_l, kr_l, sr_l, si_l, lru_l = [], [], [], [], [], [], []
    for l in range(DEPTH):
        lw = _layer_weights(l, w)
        lat_mod = [mod[l, :Bd, n][:, None, :] for n in range(6)]
        ctx_mod = [mod[l, Bd:Bd + 1, n][:, None, :] for n in range(6)]

        zs5 = jnp.zeros((B // SUBLANES, 4, SUBLANES, S5_WIDTH), F32)
        zlru = jnp.zeros((B // SUBLANES, 2, SUBLANES, LRU_WIDTH), F32)
        xp, (k32, v32, ckv_n, krp), s5_fin, lru_fin = _mix(xp, ctx_mod[0:3], lw, None, None, zs5, zlru, True)
        xp = _moe(xp, ctx_mod[4], ctx_mod[3], ctx_mod[5], lw)
        ak_l.append(k32.reshape(B, L, A_KV_HEADS, A_HEAD_DIM))
        av_l.append(v32.reshape(B, L, A_KV_HEADS, A_HEAD_DIM))
        ckv_l.append(ckv_n)
        kr_l.append(krp[:, :, MLA_NOPE:MLA_QK])
        fin = _rows_to_state(s5_fin, S5_WIDTH)
        sr_l.append(fin[:, 0::2].reshape(B, 2, S5_GROUPS, S5_STATE))
        si_l.append(fin[:, 1::2].reshape(B, 2, S5_GROUPS, S5_STATE))
        lru_l.append(_rows_to_state(lru_fin, LRU_WIDTH))

        sre = state_ssm_re[:, l].reshape(Bd, 2, S5_WIDTH)
        sim = state_ssm_im[:, l].reshape(Bd, 2, S5_WIDTH)
        s5_h0 = _state_to_rows(jnp.stack([sre[:, 0], sim[:, 0], sre[:, 1], sim[:, 1]], axis=1))
        lru_h0 = _state_to_rows(state_lru[:, l])
        cache = (cache_attn_k[:, l].reshape(Bd, P, 128), cache_attn_v[:, l].reshape(Bd, P, 128),
                 cache_mla_ckv[:, l],
                 jnp.pad(cache_mla_krope[:, l], ((0, 0), (0, 0), (MLA_NOPE, LANES - MLA_QK))))
        xs, _, _, _ = _mix(xs, lat_mod[0:3], lw, rope_a + rope_m, cache, s5_h0, lru_h0, False)
        xs = _moe(xs, lat_mod[4], lat_mod[3], lat_mod[5], lw)

    stack = lambda ts: jnp.stack(ts, axis=1)
    return (xp, xs, stack(ak_l), stack(av_l), stack(ckv_l), stack(kr_l), stack(sr_l), stack(si_l), stack(lru_l))
```

```python
import functools
import math

import jax
import jax.numpy as jnp
import numpy as np
from jax import lax
from jax.experimental import pallas as pl
from jax.experimental.pallas import tpu as pltpu

F32 = jnp.float32
BF16 = jnp.bfloat16

D_MODEL = 1024
DEPTH = 2
GRID_W = 64
N_BRANCH = 4
BRANCH_W = 256
ROPE_BASE = 10000.0
EPS = 1e-6
NEG_INF = -1e30
A_HEADS = 4
A_KV_HEADS = 2
A_HEAD_DIM = 64
WINDOW = 128
Q_BLOCK = 128
S5_GROUP = 16
S5_GROUPS = 16
S5_STATE = 64
S5_WIDTH = S5_GROUPS * S5_STATE
LRU_WIDTH = 256
LRU_BLOCKS = 4
LRU_CONV = 4
LRU_C = 8.0
MLA_HEADS = 4
MLA_Q_LORA = 256
MLA_KV_LORA = 128
MLA_NOPE = 64
MLA_ROPE = 32
MLA_V = 64
MLA_QK = MLA_NOPE + MLA_ROPE
N_GROUPS = 4
EXPERTS_PER_GROUP = 4
N_EXPERTS = 16
EXPERT_FF = 256

LANES = 128
SUBLANES = 8
VMEM_LIMIT = 56 * 1024 * 1024

TOK_TILE = 256
SCAN_STEPS = 64
MOE_SORT_TILE = 256
MOE_ROW_ALIGN = 16
MOE_SORTED_ROWS = MOE_SORT_TILE + LANES
MOE_ROW_TILE = 512
MOE_SORT_SUBTILES = 4
MOE_RUNS_IN_FLIGHT = 64
MOE_RUN_SIZES = (256, 128, 64, 32, 16)
MLA_Q_TILE = 256
Z_COLS = 1792
HEAD_ORDER = np.array((0, 2, 1, 3))


def _params(sem):
    return pltpu.CompilerParams(dimension_semantics=sem, vmem_limit_bytes=VMEM_LIMIT)


def _full(shape):
    n = len(shape)
    return pl.BlockSpec(shape, lambda *_: (0,) * n)


def _modnorm(x, g, sc, sh):
    ms = jnp.mean(x * x, axis=-1, keepdims=True)
    return (x * lax.rsqrt(ms + EPS)) * g * (1.0 + sc) + sh


def _rmsnorm(x, g):
    ms = jnp.mean(x * x, axis=-1, keepdims=True)
    return (x * lax.rsqrt(ms + EPS)) * g


def _half_rmsnorm(x):
    lane = lax.broadcasted_iota(jnp.int32, (1, LANES), 1)
    lo = lane < A_HEAD_DIM
    outs = []
    for t in range(x.shape[1] // LANES):
        xt = x[:, LANES * t:LANES * (t + 1)]
        sq = xt * xt
        s_lo = jnp.sum(jnp.where(lo, sq, 0.0), axis=-1, keepdims=True)
        s_hi = jnp.sum(jnp.where(lo, 0.0, sq), axis=-1, keepdims=True)
        inv = jnp.where(lo, lax.rsqrt(s_lo * (1.0 / A_HEAD_DIM) + EPS),
                        lax.rsqrt(s_hi * (1.0 / A_HEAD_DIM) + EPS))
        outs.append(xt * inv)
    return jnp.concatenate(outs, axis=1)


def _tile_rmsnorm(x, n_real):
    outs = []
    for t in range(x.shape[1] // LANES):
        xt = x[:, LANES * t:LANES * (t + 1)]
        ss = jnp.sum(xt * xt, axis=-1, keepdims=True)
        outs.append(xt * lax.rsqrt(ss * (1.0 / n_real) + EPS))
    return jnp.concatenate(outs, axis=1)


def _rope(x, cos, sin_next, sin_prev, quarter):
    width = x.shape[1]
    return (x * cos + pltpu.roll(x, width - quarter, 1) * sin_next
            + pltpu.roll(x, quarter, 1) * sin_prev)


def _mla_kv(ckv_n, kr_placed, wukv, gk, rope_tabs):
    kv = jnp.dot(ckv_n.astype(BF16), wukv, preferred_element_type=F32)
    kfull = kv[:, :MLA_HEADS * LANES] + jnp.concatenate([kr_placed] * MLA_HEADS, axis=1)
    kn = _tile_rmsnorm(kfull, MLA_QK) * gk
    if rope_tabs is not None:
        kn = _rope(kn, *rope_tabs, MLA_ROPE // 4)
    return kn, kv[:, MLA_HEADS * LANES:]


def _mod_kernel(c_ref, w_ref, b_ref, o_ref):
    c = c_ref[...]
    s = c * jax.nn.sigmoid(c)
    o_ref[...] = jnp.dot(s.astype(BF16), w_ref[...], preferred_element_type=F32) + b_ref[...]


def _modulation(cond, w_mod, b_mod):
    n_rows = cond.shape[0]
    n_out = w_mod.shape[-1]
    tn = 1024
    return pl.pallas_call(
        _mod_kernel,
        grid=(DEPTH, n_out // tn),
        in_specs=[pl.BlockSpec((n_rows, D_MODEL), lambda l, j: (0, 0)),
                  pl.BlockSpec((None, D_MODEL, tn), lambda l, j: (l, 0, j)),
                  pl.BlockSpec((None, 1, tn), lambda l, j: (l, 0, j))],
        out_specs=pl.BlockSpec((None, n_rows, tn), lambda l, j: (l, 0, j)),
        out_shape=jax.ShapeDtypeStruct((DEPTH, n_rows, n_out), F32),
        compiler_params=_params(("arbitrary", "arbitrary")),
        name="modulation",
    )(cond, w_mod, b_mod)


def _in_proj_kernel(rope, ctx, *refs):
    it = iter(refs)
    x_ref, sc_ref, sh_ref, g1_ref, wz_ref = (next(it) for _ in range(5))
    gq_ref, gk_ref, qlg_ref, wuq_ref, gmq_ref, kvg_ref, wukv_ref, gmk_ref = (next(it) for _ in range(8))
    if rope:
        ca_ref, sna_ref, spa_ref, cm_ref, snm_ref, spm_ref = (next(it) for _ in range(6))
    q_ref, k_ref, v_ref, s5u_ref, lx_ref, lg_ref, qm_ref, km_ref, vm_ref = (next(it) for _ in range(9))
    if ctx:
        k32_ref, v32_ref, ckv_ref, krp_ref = (next(it) for _ in range(4))

    h = _modnorm(x_ref[...], g1_ref[...], sc_ref[...], sh_ref[...])
    z = jnp.dot(h.astype(BF16), wz_ref[...], preferred_element_type=F32)

    q = _half_rmsnorm(z[:, 0:256]) * gq_ref[...]
    k = _half_rmsnorm(z[:, 256:384]) * gk_ref[...]
    v = z[:, 384:512]
    if ctx:
        k32_ref[...] = k
        v32_ref[...] = v
    if rope:
        quarter = A_HEAD_DIM // 4
        q = _rope(q, ca_ref[...], sna_ref[...], spa_ref[...], quarter)
        k = _rope(k, ca_ref[:, :LANES], sna_ref[:, :LANES], spa_ref[:, :LANES], quarter)
    q_ref[...] = q.astype(BF16)
    k_ref[...] = k.astype(BF16)
    v_ref[...] = v.astype(BF16)

    s5u_ref[...] = z[:, 512:768]
    lx_ref[...] = z[:, 768:1024]
    lg_ref[...] = z[:, 1024:1280]

    mtabs = None
    mtabs4 = None
    if rope:
        mtabs = (cm_ref[...], snm_ref[...], spm_ref[...])
        mtabs4 = tuple(jnp.concatenate([t] * MLA_HEADS, axis=1) for t in mtabs)
    ql = _rmsnorm(z[:, 1280:1536], qlg_ref[...])
    qm = jnp.dot(ql.astype(BF16), wuq_ref[...], preferred_element_type=F32)
    qm = _tile_rmsnorm(qm, MLA_QK) * gmq_ref[...]
    if rope:
        qm = _rope(qm, *mtabs4, MLA_ROPE // 4)
    qm_ref[...] = qm.astype(BF16)

    ckv_n = _rmsnorm(z[:, 1536:1664], kvg_ref[...])
    krp = z[:, 1664:1792]
    km, vm = _mla_kv(ckv_n, krp, wukv_ref[...], gmk_ref[...], mtabs4)
    km_ref[...] = km.astype(BF16)
    vm_ref[...] = vm.astype(BF16)
    if ctx:
        ckv_ref[...] = ckv_n
        krp_ref[...] = krp


def _in_proj(x, sc, sh, lw, rope_tabs, ctx):
    B, L, _ = x.shape
    Bg = B // SUBLANES
    TL = TOK_TILE
    rope = rope_tabs is not None
    per_b = sc.shape[0] > 1

    def mod_map(b, i):
        return (b if per_b else 0, 0, 0)

    tok = lambda c: pl.BlockSpec((None, TL, c), lambda b, i: (b, i, 0))
    tview = pl.BlockSpec((None, TL, 256), lambda b, i: (b // SUBLANES, i, b % SUBLANES))
    in_specs = [tok(D_MODEL), pl.BlockSpec((None, 1, D_MODEL), mod_map),
                pl.BlockSpec((None, 1, D_MODEL), mod_map), _full((1, D_MODEL)), _full((D_MODEL, Z_COLS)),
                _full((1, 256)), _full((1, 128)), _full((1, 256)), _full((256, 512)), _full((1, 512)),
                _full((1, 128)), _full((128, 768)), _full((1, 512))]
    args = [x, sc, sh, lw["g1"], lw["wz"], lw["gq"], lw["gk"], lw["qlg"], lw["wuq"], lw["gmq"],
            lw["kvg"], lw["wukv"], lw["gmk"]]
    if rope:
        in_specs += [pl.BlockSpec((TL, 256), lambda b, i: (i, 0))] * 3
        in_specs += [pl.BlockSpec((TL, 128), lambda b, i: (i, 0))] * 3
        args += list(rope_tabs)
    out_specs = [tok(256), tok(128), tok(128), tview, tview, tok(256), tok(512), tok(512), tok(256)]
    sds = jax.ShapeDtypeStruct
    out_shape = [sds((B, L, 256), BF16), sds((B, L, 128), BF16), sds((B, L, 128), BF16),
                 sds((Bg, L, SUBLANES * 256), F32), sds((Bg, L, SUBLANES * 256), F32),
                 sds((B, L, 256), F32), sds((B, L, 512), BF16), sds((B, L, 512), BF16),
                 sds((B, L, 256), BF16)]
    if ctx:
        out_specs += [tok(128), tok(128), tok(128), tok(128)]
        out_shape += [sds((B, L, 128), F32)] * 4
    return pl.pallas_call(
        functools.partial(_in_proj_kernel, rope, ctx),
        grid=(B, L // TL),
        in_specs=in_specs, out_specs=out_specs, out_shape=out_shape,
        compiler_params=_params(("parallel", "parallel")),
        name="in_proj_ctx" if ctx else "in_proj_lat",
    )(*args)


def _mla_cache_kernel(ckv_ref, krp_ref, wukv_ref, gmk_ref, km_ref, vm_ref):
    km, vm = _mla_kv(ckv_ref[...], krp_ref[...], wukv_ref[...], gmk_ref[...], None)
    km_ref[...] = km.astype(BF16)
    vm_ref[...] = vm.astype(BF16)


def _mla_cache_kv(cckv, ckr_placed, lw):
    B, P, _ = cckv.shape
    tok = lambda c: pl.BlockSpec((None, P, c), lambda b: (b, 0, 0))
    return pl.pallas_call(
        _mla_cache_kernel,
        grid=(B,),
        in_specs=[tok(128), tok(128), _full((128, 768)), _full((1, 512))],
        out_specs=[tok(512), tok(256)],
        out_shape=[jax.ShapeDtypeStruct((B, P, 512), BF16), jax.ShapeDtypeStruct((B, P, 256), BF16)],
        compiler_params=_params(("parallel",)),
        name="mla_cache_kv",
    )(cckv, ckr_placed, lw["wukv"], lw["gmk"])


def _sink_softmax(s, sink):
    m = jnp.maximum(jnp.max(s, axis=-1, keepdims=True), sink)
    e = jnp.exp(s - m)
    den = jnp.sum(e, axis=-1, keepdims=True) + jnp.exp(sink - m)
    return (e / den).astype(BF16)


def _gqa_tile(qt, keys, vst, sink_lo, sink_hi, mask):
    lane = lax.broadcasted_iota(jnp.int32, (1, LANES), 1)
    lo = lane < A_HEAD_DIM
    zero = jnp.zeros_like(qt)
    ps = []
    for qh, sink in ((jnp.where(lo, qt, zero), sink_lo), (jnp.where(lo, zero, qt), sink_hi)):
        s = lax.dot_general(qh, keys, (((1,), (1,)), ((), ())), preferred_element_type=F32)
        if mask is not None:
            s = jnp.where(mask, s, NEG_INF)
        ps.append(_sink_softmax(s, sink))
    return jnp.dot(jnp.concatenate(ps, axis=1), vst, preferred_element_type=F32)


def _stack_kv_halves(v):
    lane = lax.broadcasted_iota(jnp.int32, (1, LANES), 1)
    lo = lane < A_HEAD_DIM
    zero = jnp.zeros_like(v)
    return jnp.concatenate([jnp.where(lo, v, zero), jnp.where(lo, zero, v)], axis=0)


def _attn_a_ctx_kernel(sink_ref, q_ref, k_ref, v_ref, o_ref):
    keys = k_ref[...]
    vst = _stack_kv_halves(v_ref[...])
    for t in range(2):
        o = _gqa_tile(q_ref[:, LANES * t:LANES * (t + 1)], keys, vst, sink_ref[t], sink_ref[2 + t], None)
        o_ref[:, LANES * t:LANES * (t + 1)] = o.astype(o_ref.dtype)


def _attn_a_ctx(q, k, v, sink):
    B, L, _ = q.shape
    tok = lambda c: pl.BlockSpec((None, L, c), lambda b: (b, 0, 0))
    return pl.pallas_call(
        _attn_a_ctx_kernel,
        grid=(B,),
        in_specs=[pl.BlockSpec(memory_space=pltpu.SMEM), tok(256), tok(128), tok(128)],
        out_specs=tok(256),
        out_shape=jax.ShapeDtypeStruct((B, L, 256), BF16),
        compiler_params=_params(("parallel",)),
        name="attn_a_ctx",
    )(sink, q, k, v)


def _attn_a_lat_kernel(sink_ref, q_ref, kc_ref, vc_ref, k_ref, v_ref, o_ref):
    i = pl.program_id(1)
    L = k_ref.shape[0]
    n_ctx = kc_ref.shape[0]
    span = 3 * Q_BLOCK
    start = pl.multiple_of(jnp.clip((i - 1) * Q_BLOCK, 0, L - span), Q_BLOCK)
    keys = jnp.concatenate([kc_ref[...].astype(BF16), k_ref[pl.ds(start, span), :]], axis=0)
    vals = jnp.concatenate([vc_ref[...].astype(BF16), v_ref[pl.ds(start, span), :]], axis=0)
    vst = _stack_kv_halves(vals)
    col = lax.broadcasted_iota(jnp.int32, (Q_BLOCK, n_ctx + span), 1)
    row = lax.broadcasted_iota(jnp.int32, (Q_BLOCK, n_ctx + span), 0)
    rel = (col - n_ctx + start) - (row + i * Q_BLOCK)
    mask = (col < n_ctx) | (jnp.abs(rel) <= WINDOW)
    for t in range(2):
        o = _gqa_tile(q_ref[:, LANES * t:LANES * (t + 1)], keys, vst, sink_ref[t], sink_ref[2 + t], mask)
        o_ref[:, LANES * t:LANES * (t + 1)] = o.astype(o_ref.dtype)


def _attn_a_lat(q, k, v, kc, vc, sink):
    B, L, _ = q.shape
    P = kc.shape[1]
    blk = lambda c: pl.BlockSpec((None, Q_BLOCK, c), lambda b, i: (b, i, 0))
    whole = lambda n, c: pl.BlockSpec((None, n, c), lambda b, i: (b, 0, 0))
    return pl.pallas_call(
        _attn_a_lat_kernel,
        grid=(B, L // Q_BLOCK),
        in_specs=[pl.BlockSpec(memory_space=pltpu.SMEM), blk(256), whole(P, 128), whole(P, 128),
                  whole(L, 128), whole(L, 128)],
        out_specs=blk(256),
        out_shape=jax.ShapeDtypeStruct((B, L, 256), BF16),
        compiler_params=_params(("parallel", "parallel")),
        name="attn_a_lat",
    )(sink, q, kc, vc, k, v)


def _mla_attend(q, key_parts, val_parts):
    vals = val_parts[0] if len(val_parts) == 1 else jnp.concatenate(val_parts, axis=0)
    lane = lax.broadcasted_iota(jnp.int32, (1, MLA_HEADS * MLA_V), 1)
    zero = jnp.zeros_like(vals)
    vst = jnp.concatenate([jnp.where((lane >= MLA_V * h) & (lane < MLA_V * (h + 1)), vals, zero)
                           for h in range(MLA_HEADS)], axis=0)
    ps = []
    inv = jnp.zeros((q.shape[0], MLA_HEADS * MLA_V), F32)
    for h in range(MLA_HEADS):
        qh = q[:, LANES * h:LANES * (h + 1)]
        ss = [lax.dot_general(qh, kp[:, LANES * h:LANES * (h + 1)], (((1,), (1,)), ((), ())),
                              preferred_element_type=F32) for kp in key_parts]
        m = functools.reduce(jnp.maximum, [jnp.max(s, axis=-1, keepdims=True) for s in ss])
        es = [jnp.exp2(s - m) for s in ss]
        den = functools.reduce(lambda a, b: a + b, [jnp.sum(e, axis=-1, keepdims=True) for e in es])
        ps += [e.astype(BF16) for e in es]
        inv = jnp.where((lane >= MLA_V * h) & (lane < MLA_V * (h + 1)), 1.0 / den, inv)
    return jnp.dot(jnp.concatenate(ps, axis=1), vst, preferred_element_type=F32) * inv


def _mla_ctx_kernel(q_ref, k_ref, v_ref, o_ref):
    o_ref[...] = _mla_attend(q_ref[...], [k_ref[...]], [v_ref[...]]).astype(o_ref.dtype)


def _mla_ctx(q, k, v):
    B, L, _ = q.shape
    tok = lambda c: pl.BlockSpec((None, L, c), lambda b: (b, 0, 0))
    return pl.pallas_call(
        _mla_ctx_kernel,
        grid=(B,),
        in_specs=[tok(512), tok(512), tok(256)],
        out_specs=tok(256),
        out_shape=jax.ShapeDtypeStruct((B, L, 256), BF16),
        compiler_params=_params(("parallel",)),
        name="mla_ctx",
    )(q, k, v)


def _mla_lat_kernel(q_ref, kc_ref, vc_ref, k_ref, v_ref, o_ref):
    o = _mla_attend(q_ref[...], [kc_ref[...], k_ref[...]], [vc_ref[...], v_ref[...]])
    o_ref[...] = o.astype(o_ref.dtype)


def _mla_lat(q, k, v, kc, vc):
    B, L, _ = q.shape
    P = kc.shape[1]
    TQ = MLA_Q_TILE
    blk = lambda c: pl.BlockSpec((None, TQ, c), lambda b, i: (b, i, 0))
    whole = lambda n, c: pl.BlockSpec((None, n, c), lambda b, i: (b, 0, 0))
    return pl.pallas_call(
        _mla_lat_kernel,
        grid=(B, L // TQ),
        in_specs=[blk(512), whole(P, 512), whole(P, 256), whole(L, 512), whole(L, 256)],
        out_specs=blk(256),
        out_shape=jax.ShapeDtypeStruct((B, L, 256), BF16),
        compiler_params=_params(("parallel", "parallel")),
        name="mla_lat",
    )(q, kc, vc, k, v)


def _s5_kernel(uf_ref, ub_ref, bf_ref, bb_ref, a_ref, h0_ref, cf_ref, cb_ref,
               yf_ref, yb_ref, fin_ref, s_ref, st_ref):
    i = pl.program_id(1)
    steps = SCAN_STEPS

    @pl.when(i == 0)
    def _():
        st_ref[...] = h0_ref[...]

    uf = uf_ref[...].astype(BF16)
    ub = ub_ref[...].astype(BF16)
    s_ref[0] = jnp.dot(uf, bf_ref[:, :S5_WIDTH], preferred_element_type=F32)
    s_ref[1] = jnp.dot(uf, bf_ref[:, S5_WIDTH:], preferred_element_type=F32)
    s_ref[2] = jnp.dot(ub, bb_ref[:, :S5_WIDTH], preferred_element_type=F32)
    s_ref[3] = jnp.dot(ub, bb_ref[:, S5_WIDTH:], preferred_element_type=F32)

    lane_chunk = 2 * LANES
    for c in range(S5_WIDTH // lane_chunk):
        sl = slice(lane_chunk * c, lane_chunk * (c + 1))
        arf, aif, arb, aib = (a_ref[n, :, sl] for n in range(4))

        def body(j, carry):
            hrf, hif, hrb, hib = carry
            rf = pl.multiple_of(j * SUBLANES, SUBLANES)
            rb = pl.multiple_of((steps - 1 - j) * SUBLANES, SUBLANES)
            nrf = arf * hrf - aif * hif + s_ref[0, pl.ds(rf, SUBLANES), sl]
            nif = arf * hif + aif * hrf + s_ref[1, pl.ds(rf, SUBLANES), sl]
            nrb = arb * hrb - aib * hib + s_ref[2, pl.ds(rb, SUBLANES), sl]
            nib = arb * hib + aib * hrb + s_ref[3, pl.ds(rb, SUBLANES), sl]
            s_ref[0, pl.ds(rf, SUBLANES), sl] = nrf
            s_ref[1, pl.ds(rf, SUBLANES), sl] = nif
            s_ref[2, pl.ds(rb, SUBLANES), sl] = nrb
            s_ref[3, pl.ds(rb, SUBLANES), sl] = nib
            return nrf, nif, nrb, nib

        fin = lax.fori_loop(0, steps, body, tuple(st_ref[n, :, sl] for n in range(4)), unroll=4)
        for n in range(4):
            st_ref[n, :, sl] = fin[n]

    hf = jnp.concatenate([s_ref[0].astype(BF16), s_ref[1].astype(BF16)], axis=1)
    yf_ref[...] = jnp.dot(hf, cf_ref[...], preferred_element_type=F32)
    hb = jnp.concatenate([s_ref[2].astype(BF16), s_ref[3].astype(BF16)], axis=1)
    yb_ref[...] = jnp.dot(hb, cb_ref[...], preferred_element_type=F32)
    fin_ref[...] = st_ref[...]


def _s5_scan(u_rows, lw, h0):
    Bg, rows, _ = u_rows.shape
    R = SCAN_STEPS * SUBLANES
    n = rows // R
    fwd = pl.BlockSpec((None, R, 256), lambda g, i: (g, i, 0))
    bwd = pl.BlockSpec((None, R, 256), lambda g, i: (g, n - 1 - i, 0))
    st = pl.BlockSpec((None, 4, SUBLANES, S5_WIDTH), lambda g, i: (g, 0, 0, 0))
    return pl.pallas_call(
        _s5_kernel,
        grid=(Bg, n),
        in_specs=[fwd, bwd, _full((256, 2 * S5_WIDTH)), _full((256, 2 * S5_WIDTH)),
                  _full((4, SUBLANES, S5_WIDTH)), st, _full((2 * S5_WIDTH, 256)), _full((2 * S5_WIDTH, 256))],
        out_specs=[fwd, bwd, st],
        out_shape=[jax.ShapeDtypeStruct((Bg, rows, 256), F32), jax.ShapeDtypeStruct((Bg, rows, 256), F32),
                   jax.ShapeDtypeStruct((Bg, 4, SUBLANES, S5_WIDTH), F32)],
        scratch_shapes=[pltpu.VMEM((4, R, S5_WIDTH), F32), pltpu.VMEM((4, SUBLANES, S5_WIDTH), F32)],
        compiler_params=_params(("parallel", "arbitrary")),
        name="s5_scan",
    )(u_rows, u_rows, lw["s5_bf"], lw["s5_bb"], lw["s5_a"], h0, lw["s5_cf"], lw["s5_cb"])


def _lru_gates(x_ref, pre_ref, post_ref, has_pre, has_post, cw_ref, cb_ref, w_ref, b_ref, sp_ref, a_ref, h_ref):
    R = x_ref.shape[0]
    pre = jnp.where(has_pre, pre_ref[...], 0.0)
    post = jnp.where(has_post, post_ref[...], 0.0)
    xp = jnp.concatenate([pre, x_ref[...], post], axis=0)
    xc = cb_ref[...]
    for t in range(LRU_CONV):
        xc = xc + xp[SUBLANES * t:SUBLANES * t + R] * cw_ref[t:t + 1, :]
    g = jnp.dot(xc.astype(BF16), w_ref[...], preferred_element_type=F32) + b_ref[...]
    r = jax.nn.sigmoid(g[:, :LRU_WIDTH])
    ig = jax.nn.sigmoid(g[:, LRU_WIDTH:])
    log_a = (-LRU_C) * r * sp_ref[...]
    a = jnp.exp(log_a)
    a_ref[...] = a
    h_ref[...] = jnp.sqrt(1.0 - a * a) * (ig * xc)


def _lru_kernel(xf_ref, xfp_ref, xfn_ref, xb_ref, xbp_ref, xbn_ref, cw_ref, cb_ref, wf_ref, wb_ref,
                bf_ref, bb_ref, spf_ref, spb_ref, h0_ref, hf_ref, hb_ref, fin_ref, af_ref, ab_ref, st_ref):
    i = pl.program_id(1)
    n = pl.num_programs(1)
    steps = SCAN_STEPS

    @pl.when(i == 0)
    def _():
        st_ref[...] = h0_ref[...]

    _lru_gates(xf_ref, xfp_ref, xfn_ref, i > 0, i < n - 1, cw_ref, cb_ref, wf_ref, bf_ref, spf_ref, af_ref, hf_ref)
    _lru_gates(xb_ref, xbp_ref, xbn_ref, i < n - 1, i > 0, cw_ref, cb_ref, wb_ref, bb_ref, spb_ref, ab_ref, hb_ref)

    def body(j, carry):
        hf, hb = carry
        rf = pl.multiple_of(j * SUBLANES, SUBLANES)
        rb = pl.multiple_of((steps - 1 - j) * SUBLANES, SUBLANES)
        nf = af_ref[pl.ds(rf, SUBLANES), :] * hf + hf_ref[pl.ds(rf, SUBLANES), :]
        nb = ab_ref[pl.ds(rb, SUBLANES), :] * hb + hb_ref[pl.ds(rb, SUBLANES), :]
        hf_ref[pl.ds(rf, SUBLANES), :] = nf
        hb_ref[pl.ds(rb, SUBLANES), :] = nb
        return nf, nb

    ff, fb = lax.fori_loop(0, steps, body, (st_ref[0], st_ref[1]), unroll=8)
    st_ref[0] = ff
    st_ref[1] = fb
    fin_ref[...] = st_ref[...]


def _lru_scan(x_rows, lw, h0):
    Bg, rows, _ = x_rows.shape
    R = SCAN_STEPS * SUBLANES
    n = rows // R
    pre_rows = 2 * SUBLANES
    fwd = lambda g, i: (g, i, 0)
    bwd = lambda g, i: (g, n - 1 - i, 0)
    blk = lambda m: pl.BlockSpec((None, R, 256), m)
    pre = lambda m: pl.BlockSpec(
        (None, pre_rows, 256), lambda g, i: (g, jnp.maximum(m(g, i)[1] * (R // pre_rows) - 1, 0), 0))
    post = lambda m: pl.BlockSpec(
        (None, SUBLANES, 256),
        lambda g, i: (g, jnp.minimum((m(g, i)[1] + 1) * (R // SUBLANES), rows // SUBLANES - 1), 0))
    st = pl.BlockSpec((None, 2, SUBLANES, 256), lambda g, i: (g, 0, 0, 0))
    return pl.pallas_call(
        _lru_kernel,
        grid=(Bg, n),
        in_specs=[blk(fwd), pre(fwd), post(fwd), blk(bwd), pre(bwd), post(bwd),
                  _full((LRU_CONV, 256)), _full((1, 256)), _full((256, 512)), _full((256, 512)),
                  _full((1, 512)), _full((1, 512)), _full((1, 256)), _full((1, 256)), st],
        out_specs=[blk(fwd), blk(bwd), st],
        out_shape=[jax.ShapeDtypeStruct((Bg, rows, 256), F32), jax.ShapeDtypeStruct((Bg, rows, 256), F32),
                   jax.ShapeDtypeStruct((Bg, 2, SUBLANES, 256), F32)],
        scratch_shapes=[pltpu.VMEM((R, 256), F32), pltpu.VMEM((R, 256), F32),
                        pltpu.VMEM((2, SUBLANES, 256), F32)],
        compiler_params=_params(("parallel", "arbitrary")),
        name="lru_scan",
    )(x_rows, x_rows, x_rows, x_rows, x_rows, x_rows, lw["lru_cw"], lw["lru_cb"], lw["lru_wf"], lw["lru_wb"],
      lw["lru_bf"], lw["lru_bb"], lw["lru_spf"], lw["lru_spb"], h0)


def _merge_kernel(x_ref, sc_ref, sh_ref, gt_ref, g1_ref, wg_ref, oa_ref, u_ref, yf_ref, yb_ref, d_ref,
                  wglu_ref, hf_ref, hb_ref, lg_ref, od_ref, wb_ref, wo_ref, o_ref):
    x = x_ref[...]
    h = _modnorm(x, g1_ref[...], sc_ref[...], sh_ref[...]).astype(BF16)

    yb5 = jax.nn.gelu(d_ref[...] * u_ref[...] + yf_ref[...] + yb_ref[...])
    gv = jnp.dot(yb5.astype(BF16), wglu_ref[...], preferred_element_type=F32)
    o_b = gv[:, :BRANCH_W] * jax.nn.sigmoid(gv[:, BRANCH_W:])
    o_c = (hf_ref[...] + hb_ref[...]) * jax.nn.gelu(lg_ref[...])
    branches = (oa_ref[...], o_b.astype(BF16), o_c.astype(BF16), od_ref[...])

    acc = jnp.zeros(x.shape, F32)
    for n in range(N_BRANCH):
        gate = jnp.dot(h, wg_ref[:, D_MODEL * n:D_MODEL * (n + 1)], preferred_element_type=F32)
        proj = jnp.dot(branches[n], wb_ref[n], preferred_element_type=F32)
        acc = acc + jax.nn.sigmoid(gate) * proj
    out = jnp.dot(acc.astype(BF16), wo_ref[...], preferred_element_type=F32)
    o_ref[...] = x + gt_ref[...] * out


def _merge(x, sc, sh, gt, lw, o_a, s5u_t, yf_t, yb_t, hf_t, hb_t, lg, o_d):
    B, L, _ = x.shape
    TL = TOK_TILE
    per_b = sc.shape[0] > 1

    def mod_map(b, i):
        return (b if per_b else 0, 0, 0)

    tok = lambda c: pl.BlockSpec((None, TL, c), lambda b, i: (b, i, 0))
    tview = pl.BlockSpec((None, TL, 256), lambda b, i: (b // SUBLANES, i, b % SUBLANES))
    mod = pl.BlockSpec((None, 1, D_MODEL), mod_map)
    return pl.pallas_call(
        _merge_kernel,
        grid=(B, L // TL),
        in_specs=[tok(D_MODEL), mod, mod, mod, _full((1, D_MODEL)), _full((D_MODEL, N_BRANCH * D_MODEL)),
                  tok(256), tview, tview, tview, _full((1, 256)), _full((256, 512)),
                  tview, tview, tok(256), tok(256), _full((N_BRANCH, BRANCH_W, D_MODEL)),
                  _full((D_MODEL, D_MODEL))],
        out_specs=tok(D_MODEL),
        out_shape=jax.ShapeDtypeStruct((B, L, D_MODEL), F32),
        compiler_params=_params(("parallel", "parallel")),
        name="merge",
    )(x, sc, sh, gt, lw["g1"], lw["wg"], o_a, s5u_t, yf_t, yb_t, lw["s5_d"], lw["wglu"],
      hf_t, hb_t, lg, o_d, lw["wb"], lw["wo"])


def _first_index(values, target):
    idx = jnp.full_like(target, float(len(values) - 1))
    for n in range(len(values) - 2, -1, -1):
        idx = jnp.where(values[n] == target, float(n), idx)
    return idx


def _list_max(values):
    return functools.reduce(jnp.maximum, values)


def _moe_sort_kernel(x_ref, sc_ref, sh_ref, g2_ref, wrh_ref, wrl_ref, br_ref, tri_ref,
                     hs_ref, gs_ref, dcol_ref, meta_ref):
    for s in range(x_ref.shape[0] // MOE_SORT_TILE):
        tok = pl.ds(s * MOE_SORT_TILE, MOE_SORT_TILE)
        srt = pl.ds(s * MOE_SORTED_ROWS, MOE_SORTED_ROWS)
        _moe_sort_tile(x_ref.at[tok], sc_ref, sh_ref, g2_ref, wrh_ref, wrl_ref, br_ref, tri_ref,
                       hs_ref.at[srt], gs_ref.at[srt], dcol_ref.at[tok], meta_ref.at[s])


def _moe_sort_tile(x_ref, sc_ref, sh_ref, g2_ref, wrh_ref, wrl_ref, br_ref, tri_ref,
                   hs_ref, gs_ref, dcol_ref, meta_ref):
    T = MOE_SORT_TILE
    h = _modnorm(x_ref[...], g2_ref[...], sc_ref[...], sh_ref[...])
    hh = h.astype(BF16)
    hl = (h - hh.astype(F32)).astype(BF16)
    nt = (((1,), (1,)), ((), ()))
    lt = (lax.dot_general(wrh_ref[...], hh, nt, preferred_element_type=F32)
          + lax.dot_general(wrh_ref[...], hl, nt, preferred_element_type=F32)
          + lax.dot_general(wrl_ref[...], hh, nt, preferred_element_type=F32)) + br_ref[...]
    gl = [lt[g:g + 1, :] for g in range(N_GROUPS)]
    gmax = _list_max(gl)
    g_idx = _first_index(gl, gmax)
    pg = 1.0 / sum(jnp.exp(v - gmax) for v in gl)
    hot = [g_idx == float(g) for g in range(N_GROUPS)]
    el = []
    for e in range(EXPERTS_PER_GROUP):
        v = jnp.zeros_like(gmax)
        for g in range(N_GROUPS):
            r = N_GROUPS + EXPERTS_PER_GROUP * g + e
            v = jnp.where(hot[g], lt[r:r + 1, :], v)
        el.append(v)
    emax = _list_max(el)
    ee = [jnp.exp(v - emax) for v in el]
    esum = sum(ee)
    pe = [v / esum for v in ee]
    v1 = _list_max(pe)
    i1 = _first_index(pe, v1)
    pe2 = [jnp.where(i1 == float(e), -1.0, pe[e]) for e in range(EXPERTS_PER_GROUP)]
    v2 = _list_max(pe2)
    i2 = _first_index(pe2, v2)
    tot = v1 + v2
    w = [jnp.where(i1 == float(e), pg * v1 / tot, jnp.where(i2 == float(e), pg * v2 / tot, 0.0))
         for e in range(EXPERTS_PER_GROUP)]

    zero_row = jnp.zeros_like(gmax)
    g8 = jnp.concatenate([jnp.where(hot[g], 1.0, 0.0) for g in range(N_GROUPS)] + [zero_row] * 4, axis=0)
    cum = jnp.dot(g8.astype(BF16), tri_ref[...], preferred_element_type=F32)
    off = jnp.zeros((1, 1), F32)
    dest = zero_row
    counts = []
    for g in range(N_GROUPS):
        cnt = cum[g:g + 1, T - 1:T]
        padded = jnp.floor((cnt + (MOE_ROW_ALIGN - 1.0)) * (1.0 / MOE_ROW_ALIGN)) * MOE_ROW_ALIGN
        dest = jnp.where(hot[g], off + cum[g:g + 1, :] - 1.0, dest)
        off = off + padded
        counts.append(padded)
    rows = lax.broadcasted_iota(jnp.int32, (MOE_SORTED_ROWS, T), 0).astype(F32)
    perm = jnp.where(rows == dest, 1.0, 0.0).astype(BF16)
    hs_ref[...] = jnp.dot(perm, hh, preferred_element_type=F32).astype(BF16)
    gates = jnp.concatenate(w + [jnp.zeros((LANES - EXPERTS_PER_GROUP, T), F32)], axis=0)
    ghi = gates.astype(BF16)
    glo = (gates - ghi.astype(F32)).astype(BF16)
    gs_ref[...] = (lax.dot_general(perm, ghi, nt, preferred_element_type=F32)
                   + lax.dot_general(perm, glo, nt, preferred_element_type=F32))
    r_i = lax.broadcasted_iota(jnp.int32, (T, T), 0)
    c_i = lax.broadcasted_iota(jnp.int32, (T, T), 1)
    dcol = jnp.sum(jnp.where(r_i == c_i, jnp.broadcast_to(dest, (T, T)), 0.0), axis=1, keepdims=True)
    dcol_ref[...] = jnp.broadcast_to(dcol, (T, LANES))
    meta_ref[...] = jnp.concatenate([jnp.broadcast_to(c, (1, LANES)) for c in counts]
                                    + [jnp.zeros((SUBLANES - N_GROUPS, LANES), F32)], axis=0)


def _moe_sort(xf, sc, sh, lw, tiles_per_mod):
    N = xf.shape[0]
    T = MOE_SORT_TILE
    n_tiles = N // T
    sub = MOE_SORT_SUBTILES
    per_b = sc.shape[0] > 1
    mod = pl.BlockSpec((None, 1, D_MODEL), lambda t: (t * sub // tiles_per_mod if per_b else 0, 0, 0))
    sds = jax.ShapeDtypeStruct
    return pl.pallas_call(
        _moe_sort_kernel,
        grid=(n_tiles // sub,),
        in_specs=[pl.BlockSpec((sub * T, D_MODEL), lambda t: (t, 0)), mod, mod, _full((1, D_MODEL)),
                  _full((LANES, D_MODEL)), _full((LANES, D_MODEL)), _full((LANES, 1)), _full((T, T))],
        out_specs=[pl.BlockSpec((sub * MOE_SORTED_ROWS, D_MODEL), lambda t: (t, 0)),
                   pl.BlockSpec((sub * MOE_SORTED_ROWS, LANES), lambda t: (t, 0)),
                   pl.BlockSpec((sub * T, LANES), lambda t: (t, 0)),
                   pl.BlockSpec((sub, SUBLANES, LANES), lambda t: (t, 0, 0))],
        out_shape=[sds((n_tiles * MOE_SORTED_ROWS, D_MODEL), BF16), sds((n_tiles * MOE_SORTED_ROWS, LANES), F32),
                   sds((N, LANES), F32), sds((n_tiles, SUBLANES, LANES), F32)],
        compiler_params=_params(("parallel",)),
        name="moe_sort",
    )(xf, sc, sh, lw["g2"], lw["wrt_hi"], lw["wrt_lo"], lw["brt"], lw["tri"])


def _run_copy_kernel(n_arrays, cnt_ref, src_ref, dst_ref, *refs):
    srcs = refs[:n_arrays]
    dsts = refs[2 * n_arrays:3 * n_arrays]
    sem = refs[3 * n_arrays]
    n_runs = cnt_ref.shape[0]

    def for_each_piece(r, fn):
        cnt = cnt_ref[r]
        s = src_ref[r]
        d = dst_ref[r]
        for size in MOE_RUN_SIZES:
            take = (cnt & size) != 0

            @pl.when(take)
            def _(s=s, d=d, size=size):
                for k in range(n_arrays):
                    fn(pltpu.make_async_copy(
                        srcs[k].at[pl.ds(pl.multiple_of(s, MOE_ROW_ALIGN), size)],
                        dsts[k].at[pl.ds(pl.multiple_of(d, MOE_ROW_ALIGN), size)], sem.at[k]))

            step = jnp.where(take, size, 0)
            s = s + step
            d = d + step

    depth = min(MOE_RUNS_IN_FLIGHT, n_runs)

    def body(r, carry):
        @pl.when(r < n_runs)
        def _():
            for_each_piece(r, lambda cp: cp.start())

        @pl.when(r >= depth)
        def _():
            for_each_piece(r - depth, lambda cp: cp.wait())

        return carry

    lax.fori_loop(0, n_runs + depth, body, 0)


def _run_copy(cnt, src, dst, arrays, out_rows, name):
    n = len(arrays)
    zeros = [jnp.zeros((out_rows,) + a.shape[1:], a.dtype) for a in arrays]
    smem = pl.BlockSpec(memory_space=pltpu.SMEM)
    hbm = pl.BlockSpec(memory_space=pl.ANY)
    return pl.pallas_call(
        functools.partial(_run_copy_kernel, n),
        in_specs=[smem] * 3 + [hbm] * (2 * n),
        out_specs=[hbm] * n,
        out_shape=[jax.ShapeDtypeStruct(z.shape, z.dtype) for z in zeros],
        scratch_shapes=[pltpu.SemaphoreType.DMA((n,))],
        input_output_aliases={3 + n + k: k for k in range(n)},
        name=name,
    )(cnt, src, dst, *arrays, *zeros)


def _moe_expert_kernel(grp_ref, valid_ref, last_ref, h_ref, g_ref, wga_ref, wup_ref, wdn_ref, y_ref):
    m = pl.program_id(0)

    @pl.when(valid_ref[m] != 0)
    def _():
        h = h_ref[...]
        acc = jnp.zeros((h.shape[0], D_MODEL), F32)
        for e in range(EXPERTS_PER_GROUP):
            a = jnp.dot(h, wga_ref[e], preferred_element_type=F32)
            u = jnp.dot(h, wup_ref[e], preferred_element_type=F32)
            act = (a * jax.nn.sigmoid(a)) * u * g_ref[:, e:e + 1]
            acc = acc + jnp.dot(act.astype(BF16), wdn_ref[e], preferred_element_type=F32)
        y_ref[...] = acc.astype(BF16)

    @pl.when(valid_ref[m] == 0)
    def _():
        y_ref[...] = jnp.zeros_like(y_ref)


def _moe_experts(grp, valid, last, hg, gg, lw):
    rows = hg.shape[0]
    TM = MOE_ROW_TILE
    tok = lambda c: pl.BlockSpec((TM, c), lambda m, grp, valid, last: (jnp.minimum(m, last[0]), 0))
    wspec = lambda r, c: pl.BlockSpec((EXPERTS_PER_GROUP, r, c), lambda m, grp, valid, last: (grp[m], 0, 0))
    return pl.pallas_call(
        _moe_expert_kernel,
        grid_spec=pltpu.PrefetchScalarGridSpec(
            num_scalar_prefetch=3, grid=(rows // TM,),
            in_specs=[tok(D_MODEL), tok(LANES), wspec(D_MODEL, EXPERT_FF), wspec(D_MODEL, EXPERT_FF),
                      wspec(EXPERT_FF, D_MODEL)],
            out_specs=pl.BlockSpec((TM, D_MODEL), lambda m, grp, valid, last: (m, 0))),
        out_shape=jax.ShapeDtypeStruct((rows, D_MODEL), BF16),
        compiler_params=_params(("arbitrary",)),
        name="moe_experts",
    )(grp, valid, last, hg, gg, lw["wga"], lw["wup"], lw["wdn"])


def _moe_unsort_kernel(x_ref, gt_ref, ys_ref, dcol_ref, o_ref):
    cols = lax.broadcasted_iota(jnp.int32, (MOE_SORT_TILE, MOE_SORTED_ROWS), 1).astype(F32)
    perm_t = jnp.where(cols == dcol_ref[:, 0:1], 1.0, 0.0).astype(BF16)
    y = jnp.dot(perm_t, ys_ref[...], preferred_element_type=F32)
    o_ref[...] = x_ref[...] + gt_ref[...] * y


def _moe_unsort(xf, gt, ys, dcol, tiles_per_mod):
    N = xf.shape[0]
    T = MOE_SORT_TILE
    per_b = gt.shape[0] > 1
    mod = pl.BlockSpec((None, 1, D_MODEL), lambda t: (t // tiles_per_mod if per_b else 0, 0, 0))
    tok = pl.BlockSpec((T, D_MODEL), lambda t: (t, 0))
    return pl.pallas_call(
        _moe_unsort_kernel,
        grid=(N // T,),
        in_specs=[tok, mod, pl.BlockSpec((MOE_SORTED_ROWS, D_MODEL), lambda t: (t, 0)),
                  pl.BlockSpec((T, LANES), lambda t: (t, 0))],
        out_specs=tok,
        out_shape=jax.ShapeDtypeStruct((N, D_MODEL), F32),
        compiler_params=_params(("parallel",)),
        name="moe_unsort",
    )(xf, gt, ys, dcol)


def _moe(x, sc, sh, gt, lw):
    B, L, _ = x.shape
    N = B * L
    T, TM = MOE_SORT_TILE, MOE_ROW_TILE
    n_tiles = N // T
    xf = x.reshape(N, D_MODEL)
    hs, gs, dcol, meta = _moe_sort(xf, sc, sh, lw, L // T)

    cnt = meta[:, :N_GROUPS, 0].astype(jnp.int32)
    tile_off = jnp.arange(n_tiles, dtype=jnp.int32)[:, None] * MOE_SORTED_ROWS + jnp.cumsum(cnt, axis=1) - cnt
    total = jnp.sum(cnt, axis=0)
    padded_total = (total + (TM - 1)) // TM * TM
    ends = jnp.cumsum(padded_total)
    grp_off = (ends - padded_total)[None, :] + jnp.cumsum(cnt, axis=0) - cnt
    cap = (-(-(N + n_tiles * N_GROUPS * (MOE_ROW_ALIGN - 1)) // TM) + N_GROUPS) * TM
    first_row = jnp.arange(cap // TM, dtype=jnp.int32) * TM
    grp = jnp.minimum(jnp.sum(first_row[:, None] >= ends[None, :], axis=1), N_GROUPS - 1).astype(jnp.int32)
    valid = (first_row < ends[-1]).astype(jnp.int32)
    last = (ends[-1:] // TM - 1).astype(jnp.int32)
    flat = lambda a: a.reshape(-1).astype(jnp.int32)

    hg, gg = _run_copy(flat(cnt), flat(tile_off), flat(grp_off), [hs, gs], cap, "moe_gather")
    yg = _moe_experts(grp, valid, last, hg, gg, lw)
    (ys,) = _run_copy(flat(cnt), flat(grp_off), flat(tile_off), [yg], n_tiles * MOE_SORTED_ROWS, "moe_scatter")
    out = _moe_unsort(xf, gt, ys, dcol, L // T)
    return out.reshape(B, L, D_MODEL)


def _block_diag(blocks):
    n, r, c = blocks.shape
    eye = jnp.eye(n, dtype=blocks.dtype)
    return jnp.einsum("nrc,nm->nrmc", blocks, eye).reshape(n * r, n * c)


def _rope_tables(seq_len, dim, width, offset):
    rows = seq_len // GRID_W
    r, col = jnp.meshgrid(jnp.arange(rows), jnp.arange(GRID_W), indexing="ij")
    r = r.reshape(-1).astype(F32)
    col = col.reshape(-1).astype(F32)
    quarter = dim // 4
    freqs = ROPE_BASE ** (-jnp.arange(quarter, dtype=F32) / quarter)
    ang_r = r[:, None] * freqs
    ang_c = col[:, None] * freqs
    zero = jnp.zeros_like(ang_r)
    cos = jnp.cos(jnp.concatenate([ang_r, ang_r, ang_c, ang_c], axis=-1))
    sin_next = jnp.concatenate([-jnp.sin(ang_r), zero, -jnp.sin(ang_c), zero], axis=-1)
    sin_prev = jnp.concatenate([zero, jnp.sin(ang_r), zero, jnp.sin(ang_c)], axis=-1)

    def place(t, fill):
        return jnp.pad(t, ((0, 0), (offset, width - offset - dim)), constant_values=fill)

    return place(cos, 1.0), place(sin_next, 0.0), place(sin_prev, 0.0)


def _layer_weights(l, w):
    lw = {}
    w_in = w["w_in"][l]
    q_cols = w_in[:, 0:256].reshape(D_MODEL, A_HEADS, A_HEAD_DIM)[:, HEAD_ORDER].reshape(D_MODEL, 256)
    kr_cols = jnp.pad(w_in[:, 1664:1696], ((0, 0), (MLA_NOPE, LANES - MLA_QK)))
    lw["wz"] = jnp.concatenate([q_cols, w_in[:, 256:1664], kr_cols], axis=1).astype(BF16)
    lw["wg"] = w_in[:, 1696:].astype(BF16)
    lw["g1"] = w["norm1_g"][l][None]
    lw["g2"] = w["norm2_g"][l][None]
    lw["gq"] = jnp.tile(w["a_qnorm_g"][l], A_HEADS)[None] * (A_HEAD_DIM ** -0.5)
    lw["gk"] = jnp.tile(w["a_knorm_g"][l], A_KV_HEADS)[None]
    lw["sink"] = w["a_sink"][l]
    lw["qlg"] = w["mla_q_lat_norm"][l][None]
    wuq = w["mla_w_uq"][l].reshape(MLA_Q_LORA, MLA_HEADS, MLA_QK)
    lw["wuq"] = jnp.pad(wuq, ((0, 0), (0, 0), (0, LANES - MLA_QK))).reshape(MLA_Q_LORA, MLA_HEADS * LANES).astype(BF16)
    pad_g = lambda g: jnp.tile(jnp.pad(g, (0, LANES - MLA_QK)), MLA_HEADS)[None]
    lw["gmq"] = pad_g(w["mla_qnorm_g"][l]) * (MLA_QK ** -0.5 * math.log2(math.e))
    lw["gmk"] = pad_g(w["mla_knorm_g"][l])
    lw["kvg"] = w["mla_kv_norm"][l][None]
    wukv = w["mla_w_ukv"][l].reshape(MLA_KV_LORA, MLA_HEADS, MLA_NOPE + MLA_V)
    wk = jnp.pad(wukv[:, :, :MLA_NOPE], ((0, 0), (0, 0), (0, LANES - MLA_NOPE))).reshape(MLA_KV_LORA, MLA_HEADS * LANES)
    wv = wukv[:, :, MLA_NOPE:].reshape(MLA_KV_LORA, MLA_HEADS * MLA_V)
    lw["wukv"] = jnp.concatenate([wk, wv], axis=1).astype(BF16)

    lre = w["s5_lambda_re"][l]
    lim = w["s5_lambda_im"][l]
    dt = jnp.exp(w["s5_log_step"][l])[:, :, None]
    mag = jnp.exp(lre * dt)
    ar, ai = mag * jnp.cos(lim * dt), mag * jnp.sin(lim * dt)
    den = lre * lre + lim * lim
    fr = ((ar - 1.0) * lre + ai * lim) / den
    fi = (ai * lre - (ar - 1.0) * lim) / den
    br, bi = w["s5_b_re"][l], w["s5_b_im"][l]
    bbr = fr[..., None] * br - fi[..., None] * bi
    bbi = fr[..., None] * bi + fi[..., None] * br
    in_map = lambda d: jnp.concatenate(
        [_block_diag(jnp.swapaxes(bbr[d], 1, 2)), _block_diag(jnp.swapaxes(bbi[d], 1, 2))], axis=1).astype(BF16)
    out_map = lambda d: jnp.concatenate(
        [_block_diag(jnp.swapaxes(w["s5_c_re"][l][d], 1, 2)),
         -_block_diag(jnp.swapaxes(w["s5_c_im"][l][d], 1, 2))], axis=0).astype(BF16)
    lw["s5_bf"], lw["s5_bb"] = in_map(0), in_map(1)
    lw["s5_cf"], lw["s5_cb"] = out_map(0), out_map(1)
    coef = jnp.stack([ar[0].reshape(-1), ai[0].reshape(-1), ar[1].reshape(-1), ai[1].reshape(-1)])
    lw["s5_a"] = jnp.broadcast_to(coef[:, None, :], (4, SUBLANES, S5_WIDTH))
    lw["s5_d"] = w["s5_d"][l][None]
    lw["wglu"] = w["s5_w_glu"][l].astype(BF16)

    lw["lru_cw"] = w["lru_conv_w"][l]
    lw["lru_cb"] = w["lru_conv_b"][l][None]
    gate_w = lambda d: jnp.concatenate(
        [_block_diag(w["lru_w_a"][l][d]), _block_diag(w["lru_w_x"][l][d])], axis=1).astype(BF16)
    gate_b = lambda d: jnp.concatenate([w["lru_b_a"][l][d], w["lru_b_x"][l][d]])[None]
    lw["lru_wf"], lw["lru_wb"] = gate_w(0), gate_w(1)
    lw["lru_bf"], lw["lru_bb"] = gate_b(0), gate_b(1)
    sp = jax.nn.softplus(-w["lru_lambda"][l])
    lw["lru_spf"], lw["lru_spb"] = sp[0][None], sp[1][None]

    wb = w["w_branch"][l]
    wb0 = wb[0].reshape(A_HEADS, A_HEAD_DIM, D_MODEL)[HEAD_ORDER].reshape(BRANCH_W, D_MODEL)
    lw["wb"] = jnp.concatenate([wb0[None], wb[1:]], axis=0).astype(BF16)
    lw["wo"] = w["w_out"][l].astype(BF16)

    wr = jnp.pad(jnp.concatenate([w["moe_w_group"][l], w["moe_w_expert"][l]], axis=1),
                 ((0, 0), (0, LANES - N_GROUPS - N_EXPERTS)))
    lw["wrt_hi"] = wr.T.astype(BF16)
    lw["wrt_lo"] = (wr.T - lw["wrt_hi"].astype(F32)).astype(BF16)
    lw["brt"] = jnp.pad(jnp.concatenate([w["moe_b_group"][l], w["moe_b_expert"][l]]),
                        (0, LANES - N_GROUPS - N_EXPERTS))[:, None]
    lw["tri"] = jnp.triu(jnp.ones((MOE_SORT_TILE, MOE_SORT_TILE), BF16))
    lw["wga"] = w["moe_w_gate"][l].astype(BF16)
    lw["wup"] = w["moe_w_up"][l].astype(BF16)
    lw["wdn"] = w["moe_w_down"][l].astype(BF16)
    return lw


def _rows_to_state(fin, width):
    Bg, n = fin.shape[0], fin.shape[1]
    return jnp.swapaxes(fin, 1, 2).reshape(Bg * SUBLANES, n, width)


def _state_to_rows(state):
    B, n, width = state.shape
    return jnp.swapaxes(state.reshape(B // SUBLANES, SUBLANES, n, width), 1, 2)


def _mix(x, mods, lw, rope_tabs, cache, s5_h0, lru_h0, ctx):
    sh1, sc1, gt1 = mods
    B, L, _ = x.shape
    Bg = B // SUBLANES
    outs = _in_proj(x, sc1, sh1, lw, rope_tabs, ctx)
    q, k, v, s5u_t, lx_t, lg, qm, km, vm = outs[:9]
    if ctx:
        o_a = _attn_a_ctx(q, k, v, lw["sink"])
        o_d = _mla_ctx(qm, km, vm)
    else:
        ck, cv, cckv, ckr = cache
        o_a = _attn_a_lat(q, k, v, ck, cv, lw["sink"])
        kc, vc = _mla_cache_kv(cckv, ckr, lw)
        o_d = _mla_lat(qm, km, vm, kc, vc)
    rows = lambda t: t.reshape(Bg, L * SUBLANES, 256)
    view = lambda t: t.reshape(Bg, L, SUBLANES * 256)
    yf, yb, s5_fin = _s5_scan(rows(s5u_t), lw, s5_h0)
    hf, hb, lru_fin = _lru_scan(rows(lx_t), lw, lru_h0)
    x = _merge(x, sc1, sh1, gt1, lw, o_a, s5u_t, view(yf), view(yb), view(hf), view(hb), lg, o_d)
    return x, outs[9:], s5_fin, lru_fin


def kernel(x_prompt, x_sample, cache_attn_k, cache_attn_v, cache_mla_ckv, cache_mla_krope, state_ssm_re, state_ssm_im, state_lru, c, c_ctx, w_mod, b_mod, norm1_g, norm2_g, w_in, a_qnorm_g, a_knorm_g, a_sink, s5_lambda_re, s5_lambda_im, s5_log_step, s5_b_re, s5_b_im, s5_c_re, s5_c_im, s5_d, s5_w_glu, lru_conv_w, lru_conv_b, lru_w_a, lru_b_a, lru_w_x, lru_b_x, lru_lambda, mla_q_lat_norm, mla_w_uq, mla_kv_norm, mla_w_ukv, mla_qnorm_g, mla_knorm_g, w_branch, w_out, moe_w_group, moe_b_group, moe_w_expert, moe_b_expert, moe_w_gate, moe_w_up, moe_w_down):
    w = dict(norm1_g=norm1_g, norm2_g=norm2_g, w_in=w_in, a_qnorm_g=a_qnorm_g, a_knorm_g=a_knorm_g,
             a_sink=a_sink, s5_lambda_re=s5_lambda_re, s5_lambda_im=s5_lambda_im, s5_log_step=s5_log_step,
             s5_b_re=s5_b_re, s5_b_im=s5_b_im, s5_c_re=s5_c_re, s5_c_im=s5_c_im, s5_d=s5_d, s5_w_glu=s5_w_glu,
             lru_conv_w=lru_conv_w, lru_conv_b=lru_conv_b, lru_w_a=lru_w_a, lru_b_a=lru_b_a, lru_w_x=lru_w_x,
             lru_b_x=lru_b_x, lru_lambda=lru_lambda, mla_q_lat_norm=mla_q_lat_norm, mla_w_uq=mla_w_uq,
             mla_kv_norm=mla_kv_norm, mla_w_ukv=mla_w_ukv, mla_qnorm_g=mla_qnorm_g, mla_knorm_g=mla_knorm_g,
             w_branch=w_branch, w_out=w_out, moe_w_group=moe_w_group, moe_b_group=moe_b_group,
             moe_w_expert=moe_w_expert, moe_b_expert=moe_b_expert, moe_w_gate=moe_w_gate, moe_w_up=moe_w_up,
             moe_w_down=moe_w_down)
    B, L, _ = x_prompt.shape
    Bd, Ld, _ = x_sample.shape
    P = cache_attn_k.shape[2]

    cond = jnp.zeros((2 * SUBLANES, D_MODEL), F32).at[:Bd].set(c).at[Bd].set(c_ctx)
    mod = _modulation(cond, w_mod.astype(BF16), b_mod[:, None, :])
    mod = mod.reshape(DEPTH, 2 * SUBLANES, 6, D_MODEL)

    rope_a = _rope_tables(Ld, A_HEAD_DIM, A_HEAD_DIM, 0)
    rope_a = tuple(jnp.tile(t, (1, A_HEADS)) for t in rope_a)
    rope_m = _rope_tables(Ld, MLA_ROPE, LANES, MLA_NOPE)

    xp, xs = x_prompt, x_sample
    ak_l, av_l, ckv_l, kr_l, sr_l, si_l, lru_l = [], [], [], [], [], [], []
    for l in range(DEPTH):
        lw = _layer_weights(l, w)
        lat_mod = [mod[l, :Bd, n][:, None, :] for n in range(6)]
        ctx_mod = [mod[l, Bd:Bd + 1, n][:, None, :] for n in range(6)]

        zs5 = jnp.zeros((B // SUBLANES, 4, SUBLANES, S5_WIDTH), F32)
        zlru = jnp.zeros((B // SUBLANES, 2, SUBLANES, LRU_WIDTH), F32)
        xp, (k32, v32, ckv_n, krp), s5_fin, lru_fin = _mix(xp, ctx_mod[0:3], lw, None, None, zs5, zlru, True)
        xp = _moe(xp, ctx_mod[4], ctx_mod[3], ctx_mod[5], lw)
        ak_l.append(k32.reshape(B, L, A_KV_HEADS, A_HEAD_DIM))
        av_l.append(v32.reshape(B, L, A_KV_HEADS, A_HEAD_DIM))
        ckv_l.append(ckv_n)
        kr_l.append(krp[:, :, MLA_NOPE:MLA_QK])
        fin = _rows_to_state(s5_fin, S5_WIDTH)
        sr_l.append(fin[:, 0::2].reshape(B, 2, S5_GROUPS, S5_STATE))
        si_l.append(fin[:, 1::2].reshape(B, 2, S5_GROUPS, S5_STATE))
        lru_l.append(_rows_to_state(lru_fin, LRU_WIDTH))

        sre = state_ssm_re[:, l].reshape(Bd, 2, S5_WIDTH)
        sim = state_ssm_im[:, l].reshape(Bd, 2, S5_WIDTH)
        s5_h0 = _state_to_rows(jnp.stack([sre[:, 0], sim[:, 0], sre[:, 1], sim[:, 1]], axis=1))
        lru_h0 = _state_to_rows(state_lru[:, l])
        cache = (cache_attn_k[:, l].reshape(Bd, P, 128), cache_attn_v[:, l].reshape(Bd, P, 128),
                 cache_mla_ckv[:, l],
                 jnp.pad(cache_mla_krope[:, l], ((0, 0), (0, 0), (MLA_NOPE, LANES - MLA_QK))))
        xs, _, _, _ = _mix(xs, lat_mod[0:3], lw, rope_a + rope_m, cache, s5_h0, lru_h0, False)
        xs = _moe(xs, lat_mod[4], lat_mod[3], lat_mod[5], lw)

    stack = lambda ts: jnp.stack(ts, axis=1)
    return (xp, xs, stack(ak_l), stack(av_l), stack(ckv_l), stack(kr_l), stack(sr_l), stack(si_l), stack(lru_l))
```

```python
import functools
import math

import jax
import jax.numpy as jnp
import numpy as np
from jax import lax
from jax.experimental import pallas as pl
from jax.experimental.pallas import tpu as pltpu

F32 = jnp.float32
BF16 = jnp.bfloat16

D_MODEL = 1024
DEPTH = 2
GRID_W = 64
N_BRANCH = 4
BRANCH_W = 256
ROPE_BASE = 10000.0
EPS = 1e-6
NEG_INF = -1e30
A_HEADS = 4
A_KV_HEADS = 2
A_HEAD_DIM = 64
WINDOW = 128
Q_BLOCK = 128
S5_GROUP = 16
S5_GROUPS = 16
S5_STATE = 64
S5_WIDTH = S5_GROUPS * S5_STATE
LRU_WIDTH = 256
LRU_BLOCKS = 4
LRU_CONV = 4
LRU_C = 8.0
MLA_HEADS = 4
MLA_Q_LORA = 256
MLA_KV_LORA = 128
MLA_NOPE = 64
MLA_ROPE = 32
MLA_V = 64
MLA_QK = MLA_NOPE + MLA_ROPE
N_GROUPS = 4
EXPERTS_PER_GROUP = 4
N_EXPERTS = 16
EXPERT_FF = 256

LANES = 128
SUBLANES = 8
VMEM_LIMIT = 56 * 1024 * 1024

TOK_TILE = 256
SCAN_STEPS = 64
MOE_SORT_TILE = 256
MOE_ROW_ALIGN = 16
MOE_SORTED_ROWS = MOE_SORT_TILE + LANES
MOE_SORT_SUBTILES = 4
MOE_BLOCK_TILES = 8
MOE_CHUNK = 256
MLA_Q_TILE = 256
Z_COLS = 1792
HEAD_ORDER = np.array((0, 2, 1, 3))


def _params(sem):
    return pltpu.CompilerParams(dimension_semantics=sem, vmem_limit_bytes=VMEM_LIMIT)


def _full(shape):
    n = len(shape)
    return pl.BlockSpec(shape, lambda *_: (0,) * n)


def _modnorm(x, g, sc, sh):
    ms = jnp.mean(x * x, axis=-1, keepdims=True)
    return (x * lax.rsqrt(ms + EPS)) * g * (1.0 + sc) + sh


def _rmsnorm(x, g):
    ms = jnp.mean(x * x, axis=-1, keepdims=True)
    return (x * lax.rsqrt(ms + EPS)) * g


def _half_rmsnorm(x):
    lane = lax.broadcasted_iota(jnp.int32, (1, LANES), 1)
    lo = lane < A_HEAD_DIM
    outs = []
    for t in range(x.shape[1] // LANES):
        xt = x[:, LANES * t:LANES * (t + 1)]
        sq = xt * xt
        s_lo = jnp.sum(jnp.where(lo, sq, 0.0), axis=-1, keepdims=True)
        s_hi = jnp.sum(jnp.where(lo, 0.0, sq), axis=-1, keepdims=True)
        inv = jnp.where(lo, lax.rsqrt(s_lo * (1.0 / A_HEAD_DIM) + EPS),
                        lax.rsqrt(s_hi * (1.0 / A_HEAD_DIM) + EPS))
        outs.append(xt * inv)
    return jnp.concatenate(outs, axis=1)


def _tile_rmsnorm(x, n_real):
    outs = []
    for t in range(x.shape[1] // LANES):
        xt = x[:, LANES * t:LANES * (t + 1)]
        ss = jnp.sum(xt * xt, axis=-1, keepdims=True)
        outs.append(xt * lax.rsqrt(ss * (1.0 / n_real) + EPS))
    return jnp.concatenate(outs, axis=1)


def _rope(x, cos, sin_next, sin_prev, quarter):
    width = x.shape[1]
    return (x * cos + pltpu.roll(x, width - quarter, 1) * sin_next
            + pltpu.roll(x, quarter, 1) * sin_prev)


def _mla_kv(ckv_n, kr_placed, wukv, gk, rope_tabs):
    kv = jnp.dot(ckv_n.astype(BF16), wukv, preferred_element_type=F32)
    kfull = kv[:, :MLA_HEADS * LANES] + jnp.concatenate([kr_placed] * MLA_HEADS, axis=1)
    kn = _tile_rmsnorm(kfull, MLA_QK) * gk
    if rope_tabs is not None:
        kn = _rope(kn, *rope_tabs, MLA_ROPE // 4)
    return kn, kv[:, MLA_HEADS * LANES:]


def _mod_kernel(c_ref, w_ref, b_ref, o_ref):
    c = c_ref[...]
    s = c * jax.nn.sigmoid(c)
    o_ref[...] = jnp.dot(s.astype(BF16), w_ref[...], preferred_element_type=F32) + b_ref[...]


def _modulation(cond, w_mod, b_mod):
    n_rows = cond.shape[0]
    n_out = w_mod.shape[-1]
    tn = 1024
    return pl.pallas_call(
        _mod_kernel,
        grid=(DEPTH, n_out // tn),
        in_specs=[pl.BlockSpec((n_rows, D_MODEL), lambda l, j: (0, 0)),
                  pl.BlockSpec((None, D_MODEL, tn), lambda l, j: (l, 0, j)),
                  pl.BlockSpec((None, 1, tn), lambda l, j: (l, 0, j))],
        out_specs=pl.BlockSpec((None, n_rows, tn), lambda l, j: (l, 0, j)),
        out_shape=jax.ShapeDtypeStruct((DEPTH, n_rows, n_out), F32),
        compiler_params=_params(("arbitrary", "arbitrary")),
        name="modulation",
    )(cond, w_mod, b_mod)


def _in_proj_kernel(rope, ctx, *refs):
    it = iter(refs)
    x_ref, sc_ref, sh_ref, g1_ref, wz_ref = (next(it) for _ in range(5))
    gq_ref, gk_ref, qlg_ref, wuq_ref, gmq_ref, kvg_ref, wukv_ref, gmk_ref = (next(it) for _ in range(8))
    if rope:
        ca_ref, sna_ref, spa_ref, cm_ref, snm_ref, spm_ref = (next(it) for _ in range(6))
    q_ref, k_ref, v_ref, s5u_ref, lx_ref, lg_ref, qm_ref, km_ref, vm_ref = (next(it) for _ in range(9))
    if ctx:
        k32_ref, v32_ref, ckv_ref, krp_ref = (next(it) for _ in range(4))

    h = _modnorm(x_ref[...], g1_ref[...], sc_ref[...], sh_ref[...])
    z = jnp.dot(h.astype(BF16), wz_ref[...], preferred_element_type=F32)

    q = _half_rmsnorm(z[:, 0:256]) * gq_ref[...]
    k = _half_rmsnorm(z[:, 256:384]) * gk_ref[...]
    v = z[:, 384:512]
    if ctx:
        k32_ref[...] = k
        v32_ref[...] = v
    if rope:
        quarter = A_HEAD_DIM // 4
        q = _rope(q, ca_ref[...], sna_ref[...], spa_ref[...], quarter)
        k = _rope(k, ca_ref[:, :LANES], sna_ref[:, :LANES], spa_ref[:, :LANES], quarter)
    q_ref[...] = q.astype(BF16)
    k_ref[...] = k.astype(BF16)
    v_ref[...] = v.astype(BF16)

    s5u_ref[...] = z[:, 512:768]
    lx_ref[...] = z[:, 768:1024]
    lg_ref[...] = z[:, 1024:1280]

    mtabs = None
    mtabs4 = None
    if rope:
        mtabs = (cm_ref[...], snm_ref[...], spm_ref[...])
        mtabs4 = tuple(jnp.concatenate([t] * MLA_HEADS, axis=1) for t in mtabs)
    ql = _rmsnorm(z[:, 1280:1536], qlg_ref[...])
    qm = jnp.dot(ql.astype(BF16), wuq_ref[...], preferred_element_type=F32)
    qm = _tile_rmsnorm(qm, MLA_QK) * gmq_ref[...]
    if rope:
        qm = _rope(qm, *mtabs4, MLA_ROPE // 4)
    qm_ref[...] = qm.astype(BF16)

    ckv_n = _rmsnorm(z[:, 1536:1664], kvg_ref[...])
    krp = z[:, 1664:1792]
    km, vm = _mla_kv(ckv_n, krp, wukv_ref[...], gmk_ref[...], mtabs4)
    km_ref[...] = km.astype(BF16)
    vm_ref[...] = vm.astype(BF16)
    if ctx:
        ckv_ref[...] = ckv_n
        krp_ref[...] = krp


def _in_proj(x, sc, sh, lw, rope_tabs, ctx):
    B, L, _ = x.shape
    Bg = B // SUBLANES
    TL = TOK_TILE
    rope = rope_tabs is not None
    per_b = sc.shape[0] > 1

    def mod_map(b, i):
        return (b if per_b else 0, 0, 0)

    tok = lambda c: pl.BlockSpec((None, TL, c), lambda b, i: (b, i, 0))
    tview = pl.BlockSpec((None, TL, 256), lambda b, i: (b // SUBLANES, i, b % SUBLANES))
    in_specs = [tok(D_MODEL), pl.BlockSpec((None, 1, D_MODEL), mod_map),
                pl.BlockSpec((None, 1, D_MODEL), mod_map), _full((1, D_MODEL)), _full((D_MODEL, Z_COLS)),
                _full((1, 256)), _full((1, 128)), _full((1, 256)), _full((256, 512)), _full((1, 512)),
                _full((1, 128)), _full((128, 768)), _full((1, 512))]
    args = [x, sc, sh, lw["g1"], lw["wz"], lw["gq"], lw["gk"], lw["qlg"], lw["wuq"], lw["gmq"],
            lw["kvg"], lw["wukv"], lw["gmk"]]
    if rope:
        in_specs += [pl.BlockSpec((TL, 256), lambda b, i: (i, 0))] * 3
        in_specs += [pl.BlockSpec((TL, 128), lambda b, i: (i, 0))] * 3
        args += list(rope_tabs)
    out_specs = [tok(256), tok(128), tok(128), tview, tview, tok(256), tok(512), tok(512), tok(256)]
    sds = jax.ShapeDtypeStruct
    out_shape = [sds((B, L, 256), BF16), sds((B, L, 128), BF16), sds((B, L, 128), BF16),
                 sds((Bg, L, SUBLANES * 256), F32), sds((Bg, L, SUBLANES * 256), F32),
                 sds((B, L, 256), F32), sds((B, L, 512), BF16), sds((B, L, 512), BF16),
                 sds((B, L, 256), BF16)]
    if ctx:
        out_specs += [tok(128), tok(128), tok(128), tok(128)]
        out_shape += [sds((B, L, 128), F32)] * 4
    return pl.pallas_call(
        functools.partial(_in_proj_kernel, rope, ctx),
        grid=(B, L // TL),
        in_specs=in_specs, out_specs=out_specs, out_shape=out_shape,
        compiler_params=_params(("parallel", "parallel")),
        name="in_proj_ctx" if ctx else "in_proj_lat",
    )(*args)


def _mla_cache_kernel(ckv_ref, krp_ref, wukv_ref, gmk_ref, km_ref, vm_ref):
    km, vm = _mla_kv(ckv_ref[...], krp_ref[...], wukv_ref[...], gmk_ref[...], None)
    km_ref[...] = km.astype(BF16)
    vm_ref[...] = vm.astype(BF16)


def _mla_cache_kv(cckv, ckr_placed, lw):
    B, P, _ = cckv.shape
    tok = lambda c: pl.BlockSpec((None, P, c), lambda b: (b, 0, 0))
    return pl.pallas_call(
        _mla_cache_kernel,
        grid=(B,),
        in_specs=[tok(128), tok(128), _full((128, 768)), _full((1, 512))],
        out_specs=[tok(512), tok(256)],
        out_shape=[jax.ShapeDtypeStruct((B, P, 512), BF16), jax.ShapeDtypeStruct((B, P, 256), BF16)],
        compiler_params=_params(("parallel",)),
        name="mla_cache_kv",
    )(cckv, ckr_placed, lw["wukv"], lw["gmk"])


def _sink_softmax(s, sink):
    m = jnp.maximum(jnp.max(s, axis=-1, keepdims=True), sink)
    e = jnp.exp(s - m)
    den = jnp.sum(e, axis=-1, keepdims=True) + jnp.exp(sink - m)
    return (e / den).astype(BF16)


def _gqa_tile(qt, keys, vst, sink_lo, sink_hi, mask):
    lane = lax.broadcasted_iota(jnp.int32, (1, LANES), 1)
    lo = lane < A_HEAD_DIM
    zero = jnp.zeros_like(qt)
    ps = []
    for qh, sink in ((jnp.where(lo, qt, zero), sink_lo), (jnp.where(lo, zero, qt), sink_hi)):
        s = lax.dot_general(qh, keys, (((1,), (1,)), ((), ())), preferred_element_type=F32)
        if mask is not None:
            s = jnp.where(mask, s, NEG_INF)
        ps.append(_sink_softmax(s, sink))
    return jnp.dot(jnp.concatenate(ps, axis=1), vst, preferred_element_type=F32)


def _stack_kv_halves(v):
    lane = lax.broadcasted_iota(jnp.int32, (1, LANES), 1)
    lo = lane < A_HEAD_DIM
    zero = jnp.zeros_like(v)
    return jnp.concatenate([jnp.where(lo, v, zero), jnp.where(lo, zero, v)], axis=0)


def _attn_a_ctx_kernel(sink_ref, q_ref, k_ref, v_ref, o_ref):
    keys = k_ref[...]
    vst = _stack_kv_halves(v_ref[...])
    for t in range(2):
        o = _gqa_tile(q_ref[:, LANES * t:LANES * (t + 1)], keys, vst, sink_ref[t], sink_ref[2 + t], None)
        o_ref[:, LANES * t:LANES * (t + 1)] = o.astype(o_ref.dtype)


def _attn_a_ctx(q, k, v, sink):
    B, L, _ = q.shape
    tok = lambda c: pl.BlockSpec((None, L, c), lambda b: (b, 0, 0))
    return pl.pallas_call(
        _attn_a_ctx_kernel,
        grid=(B,),
        in_specs=[pl.BlockSpec(memory_space=pltpu.SMEM), tok(256), tok(128), tok(128)],
        out_specs=tok(256),
        out_shape=jax.ShapeDtypeStruct((B, L, 256), BF16),
        compiler_params=_params(("parallel",)),
        name="attn_a_ctx",
    )(sink, q, k, v)


def _attn_a_lat_kernel(sink_ref, q_ref, kc_ref, vc_ref, k_ref, v_ref, o_ref):
    i = pl.program_id(1)
    L = k_ref.shape[0]
    n_ctx = kc_ref.shape[0]
    span = 3 * Q_BLOCK
    start = pl.multiple_of(jnp.clip((i - 1) * Q_BLOCK, 0, L - span), Q_BLOCK)
    keys = jnp.concatenate([kc_ref[...].astype(BF16), k_ref[pl.ds(start, span), :]], axis=0)
    vals = jnp.concatenate([vc_ref[...].astype(BF16), v_ref[pl.ds(start, span), :]], axis=0)
    vst = _stack_kv_halves(vals)
    col = lax.broadcasted_iota(jnp.int32, (Q_BLOCK, n_ctx + span), 1)
    row = lax.broadcasted_iota(jnp.int32, (Q_BLOCK, n_ctx + span), 0)
    rel = (col - n_ctx + start) - (row + i * Q_BLOCK)
    mask = (col < n_ctx) | (jnp.abs(rel) <= WINDOW)
    for t in range(2):
        o = _gqa_tile(q_ref[:, LANES * t:LANES * (t + 1)], keys, vst, sink_ref[t], sink_ref[2 + t], mask)
        o_ref[:, LANES * t:LANES * (t + 1)] = o.astype(o_ref.dtype)


def _attn_a_lat(q, k, v, kc, vc, sink):
    B, L, _ = q.shape
    P = kc.shape[1]
    blk = lambda c: pl.BlockSpec((None, Q_BLOCK, c), lambda b, i: (b, i, 0))
    whole = lambda n, c: pl.BlockSpec((None, n, c), lambda b, i: (b, 0, 0))
    return pl.pallas_call(
        _attn_a_lat_kernel,
        grid=(B, L // Q_BLOCK),
        in_specs=[pl.BlockSpec(memory_space=pltpu.SMEM), blk(256), whole(P, 128), whole(P, 128),
                  whole(L, 128), whole(L, 128)],
        out_specs=blk(256),
        out_shape=jax.ShapeDtypeStruct((B, L, 256), BF16),
        compiler_params=_params(("parallel", "parallel")),
        name="attn_a_lat",
    )(sink, q, kc, vc, k, v)


def _mla_attend(q, key_parts, val_parts):
    vals = val_parts[0] if len(val_parts) == 1 else jnp.concatenate(val_parts, axis=0)
    lane = lax.broadcasted_iota(jnp.int32, (1, MLA_HEADS * MLA_V), 1)
    zero = jnp.zeros_like(vals)
    vst = jnp.concatenate([jnp.where((lane >= MLA_V * h) & (lane < MLA_V * (h + 1)), vals, zero)
                           for h in range(MLA_HEADS)], axis=0)
    ps = []
    inv = jnp.zeros((q.shape[0], MLA_HEADS * MLA_V), F32)
    for h in range(MLA_HEADS):
        qh = q[:, LANES * h:LANES * (h + 1)]
        ss = [lax.dot_general(qh, kp[:, LANES * h:LANES * (h + 1)], (((1,), (1,)), ((), ())),
                              preferred_element_type=F32) for kp in key_parts]
        m = functools.reduce(jnp.maximum, [jnp.max(s, axis=-1, keepdims=True) for s in ss])
        es = [jnp.exp2(s - m) for s in ss]
        den = functools.reduce(lambda a, b: a + b, [jnp.sum(e, axis=-1, keepdims=True) for e in es])
        ps += [e.astype(BF16) for e in es]
        inv = jnp.where((lane >= MLA_V * h) & (lane < MLA_V * (h + 1)), 1.0 / den, inv)
    return jnp.dot(jnp.concatenate(ps, axis=1), vst, preferred_element_type=F32) * inv


def _mla_ctx_kernel(q_ref, k_ref, v_ref, o_ref):
    o_ref[...] = _mla_attend(q_ref[...], [k_ref[...]], [v_ref[...]]).astype(o_ref.dtype)


def _mla_ctx(q, k, v):
    B, L, _ = q.shape
    tok = lambda c: pl.BlockSpec((None, L, c), lambda b: (b, 0, 0))
    return pl.pallas_call(
        _mla_ctx_kernel,
        grid=(B,),
        in_specs=[tok(512), tok(512), tok(256)],
        out_specs=tok(256),
        out_shape=jax.ShapeDtypeStruct((B, L, 256), BF16),
        compiler_params=_params(("parallel",)),
        name="mla_ctx",
    )(q, k, v)


def _mla_lat_kernel(q_ref, kc_ref, vc_ref, k_ref, v_ref, o_ref):
    o = _mla_attend(q_ref[...], [kc_ref[...], k_ref[...]], [vc_ref[...], v_ref[...]])
    o_ref[...] = o.astype(o_ref.dtype)


def _mla_lat(q, k, v, kc, vc):
    B, L, _ = q.shape
    P = kc.shape[1]
    TQ = MLA_Q_TILE
    blk = lambda c: pl.BlockSpec((None, TQ, c), lambda b, i: (b, i, 0))
    whole = lambda n, c: pl.BlockSpec((None, n, c), lambda b, i: (b, 0, 0))
    return pl.pallas_call(
        _mla_lat_kernel,
        grid=(B, L // TQ),
        in_specs=[blk(512), whole(P, 512), whole(P, 256), whole(L, 512), whole(L, 256)],
        out_specs=blk(256),
        out_shape=jax.ShapeDtypeStruct((B, L, 256), BF16),
        compiler_params=_params(("parallel", "parallel")),
        name="mla_lat",
    )(q, kc, vc, k, v)


def _s5_kernel(uf_ref, ub_ref, bf_ref, bb_ref, a_ref, h0_ref, cf_ref, cb_ref,
               yf_ref, yb_ref, fin_ref, s_ref, st_ref):
    i = pl.program_id(1)
    steps = SCAN_STEPS

    @pl.when(i == 0)
    def _():
        st_ref[...] = h0_ref[...]

    uf = uf_ref[...].astype(BF16)
    ub = ub_ref[...].astype(BF16)
    s_ref[0] = jnp.dot(uf, bf_ref[:, :S5_WIDTH], preferred_element_type=F32)
    s_ref[1] = jnp.dot(uf, bf_ref[:, S5_WIDTH:], preferred_element_type=F32)
    s_ref[2] = jnp.dot(ub, bb_ref[:, :S5_WIDTH], preferred_element_type=F32)
    s_ref[3] = jnp.dot(ub, bb_ref[:, S5_WIDTH:], preferred_element_type=F32)

    lane_chunk = 2 * LANES
    for c in range(S5_WIDTH // lane_chunk):
        sl = slice(lane_chunk * c, lane_chunk * (c + 1))
        arf, aif, arb, aib = (a_ref[n, :, sl] for n in range(4))

        def body(j, carry):
            hrf, hif, hrb, hib = carry
            rf = pl.multiple_of(j * SUBLANES, SUBLANES)
            rb = pl.multiple_of((steps - 1 - j) * SUBLANES, SUBLANES)
            nrf = arf * hrf - aif * hif + s_ref[0, pl.ds(rf, SUBLANES), sl]
            nif = arf * hif + aif * hrf + s_ref[1, pl.ds(rf, SUBLANES), sl]
            nrb = arb * hrb - aib * hib + s_ref[2, pl.ds(rb, SUBLANES), sl]
            nib = arb * hib + aib * hrb + s_ref[3, pl.ds(rb, SUBLANES), sl]
            s_ref[0, pl.ds(rf, SUBLANES), sl] = nrf
            s_ref[1, pl.ds(rf, SUBLANES), sl] = nif
            s_ref[2, pl.ds(rb, SUBLANES), sl] = nrb
            s_ref[3, pl.ds(rb, SUBLANES), sl] = nib
            return nrf, nif, nrb, nib

        fin = lax.fori_loop(0, steps, body, tuple(st_ref[n, :, sl] for n in range(4)), unroll=4)
        for n in range(4):
            st_ref[n, :, sl] = fin[n]

    hf = jnp.concatenate([s_ref[0].astype(BF16), s_ref[1].astype(BF16)], axis=1)
    yf_ref[...] = jnp.dot(hf, cf_ref[...], preferred_element_type=F32)
    hb = jnp.concatenate([s_ref[2].astype(BF16), s_ref[3].astype(BF16)], axis=1)
    yb_ref[...] = jnp.dot(hb, cb_ref[...], preferred_element_type=F32)
    fin_ref[...] = st_ref[...]


def _s5_scan(u_rows, lw, h0):
    Bg, rows, _ = u_rows.shape
    R = SCAN_STEPS * SUBLANES
    n = rows // R
    fwd = pl.BlockSpec((None, R, 256), lambda g, i: (g, i, 0))
    bwd = pl.BlockSpec((None, R, 256), lambda g, i: (g, n - 1 - i, 0))
    st = pl.BlockSpec((None, 4, SUBLANES, S5_WIDTH), lambda g, i: (g, 0, 0, 0))
    return pl.pallas_call(
        _s5_kernel,
        grid=(Bg, n),
        in_specs=[fwd, bwd, _full((256, 2 * S5_WIDTH)), _full((256, 2 * S5_WIDTH)),
                  _full((4, SUBLANES, S5_WIDTH)), st, _full((2 * S5_WIDTH, 256)), _full((2 * S5_WIDTH, 256))],
        out_specs=[fwd, bwd, st],
        out_shape=[jax.ShapeDtypeStruct((Bg, rows, 256), F32), jax.ShapeDtypeStruct((Bg, rows, 256), F32),
                   jax.ShapeDtypeStruct((Bg, 4, SUBLANES, S5_WIDTH), F32)],
        scratch_shapes=[pltpu.VMEM((4, R, S5_WIDTH), F32), pltpu.VMEM((4, SUBLANES, S5_WIDTH), F32)],
        compiler_params=_params(("parallel", "arbitrary")),
        name="s5_scan",
    )(u_rows, u_rows, lw["s5_bf"], lw["s5_bb"], lw["s5_a"], h0, lw["s5_cf"], lw["s5_cb"])


def _lru_gates(x_ref, pre_ref, post_ref, has_pre, has_post, cw_ref, cb_ref, w_ref, b_ref, sp_ref, a_ref, h_ref):
    R = x_ref.shape[0]
    pre = jnp.where(has_pre, pre_ref[...], 0.0)
    post = jnp.where(has_post, post_ref[...], 0.0)
    xp = jnp.concatenate([pre, x_ref[...], post], axis=0)
    xc = cb_ref[...]
    for t in range(LRU_CONV):
        xc = xc + xp[SUBLANES * t:SUBLANES * t + R] * cw_ref[t:t + 1, :]
    g = jnp.dot(xc.astype(BF16), w_ref[...], preferred_element_type=F32) + b_ref[...]
    r = jax.nn.sigmoid(g[:, :LRU_WIDTH])
    ig = jax.nn.sigmoid(g[:, LRU_WIDTH:])
    log_a = (-LRU_C) * r * sp_ref[...]
    a = jnp.exp(log_a)
    a_ref[...] = a
    h_ref[...] = jnp.sqrt(1.0 - a * a) * (ig * xc)


def _lru_kernel(xf_ref, xfp_ref, xfn_ref, xb_ref, xbp_ref, xbn_ref, cw_ref, cb_ref, wf_ref, wb_ref,
                bf_ref, bb_ref, spf_ref, spb_ref, h0_ref, hf_ref, hb_ref, fin_ref, af_ref, ab_ref, st_ref):
    i = pl.program_id(1)
    n = pl.num_programs(1)
    steps = SCAN_STEPS

    @pl.when(i == 0)
    def _():
        st_ref[...] = h0_ref[...]

    _lru_gates(xf_ref, xfp_ref, xfn_ref, i > 0, i < n - 1, cw_ref, cb_ref, wf_ref, bf_ref, spf_ref, af_ref, hf_ref)
    _lru_gates(xb_ref, xbp_ref, xbn_ref, i < n - 1, i > 0, cw_ref, cb_ref, wb_ref, bb_ref, spb_ref, ab_ref, hb_ref)

    def body(j, carry):
        hf, hb = carry
        rf = pl.multiple_of(j * SUBLANES, SUBLANES)
        rb = pl.multiple_of((steps - 1 - j) * SUBLANES, SUBLANES)
        nf = af_ref[pl.ds(rf, SUBLANES), :] * hf + hf_ref[pl.ds(rf, SUBLANES), :]
        nb = ab_ref[pl.ds(rb, SUBLANES), :] * hb + hb_ref[pl.ds(rb, SUBLANES), :]
        hf_ref[pl.ds(rf, SUBLANES), :] = nf
        hb_ref[pl.ds(rb, SUBLANES), :] = nb
        return nf, nb

    ff, fb = lax.fori_loop(0, steps, body, (st_ref[0], st_ref[1]), unroll=8)
    st_ref[0] = ff
    st_ref[1] = fb
    fin_ref[...] = st_ref[...]


def _lru_scan(x_rows, lw, h0):
    Bg, rows, _ = x_rows.shape
    R = SCAN_STEPS * SUBLANES
    n = rows // R
    pre_rows = 2 * SUBLANES
    fwd = lambda g, i: (g, i, 0)
    bwd = lambda g, i: (g, n - 1 - i, 0)
    blk = lambda m: pl.BlockSpec((None, R, 256), m)
    pre = lambda m: pl.BlockSpec(
        (None, pre_rows, 256), lambda g, i: (g, jnp.maximum(m(g, i)[1] * (R // pre_rows) - 1, 0), 0))
    post = lambda m: pl.BlockSpec(
        (None, SUBLANES, 256),
        lambda g, i: (g, jnp.minimum((m(g, i)[1] + 1) * (R // SUBLANES), rows // SUBLANES - 1), 0))
    st = pl.BlockSpec((None, 2, SUBLANES, 256), lambda g, i: (g, 0, 0, 0))
    return pl.pallas_call(
        _lru_kernel,
        grid=(Bg, n),
        in_specs=[blk(fwd), pre(fwd), post(fwd), blk(bwd), pre(bwd), post(bwd),
                  _full((LRU_CONV, 256)), _full((1, 256)), _full((256, 512)), _full((256, 512)),
                  _full((1, 512)), _full((1, 512)), _full((1, 256)), _full((1, 256)), st],
        out_specs=[blk(fwd), blk(bwd), st],
        out_shape=[jax.ShapeDtypeStruct((Bg, rows, 256), F32), jax.ShapeDtypeStruct((Bg, rows, 256), F32),
                   jax.ShapeDtypeStruct((Bg, 2, SUBLANES, 256), F32)],
        scratch_shapes=[pltpu.VMEM((R, 256), F32), pltpu.VMEM((R, 256), F32),
                        pltpu.VMEM((2, SUBLANES, 256), F32)],
        compiler_params=_params(("parallel", "arbitrary")),
        name="lru_scan",
    )(x_rows, x_rows, x_rows, x_rows, x_rows, x_rows, lw["lru_cw"], lw["lru_cb"], lw["lru_wf"], lw["lru_wb"],
      lw["lru_bf"], lw["lru_bb"], lw["lru_spf"], lw["lru_spb"], h0)


def _merge_kernel(x_ref, sc_ref, sh_ref, gt_ref, g1_ref, wg_ref, oa_ref, u_ref, yf_ref, yb_ref, d_ref,
                  wglu_ref, hf_ref, hb_ref, lg_ref, od_ref, wb_ref, wo_ref, o_ref):
    x = x_ref[...]
    h = _modnorm(x, g1_ref[...], sc_ref[...], sh_ref[...]).astype(BF16)

    yb5 = jax.nn.gelu(d_ref[...] * u_ref[...] + yf_ref[...] + yb_ref[...])
    gv = jnp.dot(yb5.astype(BF16), wglu_ref[...], preferred_element_type=F32)
    o_b = gv[:, :BRANCH_W] * jax.nn.sigmoid(gv[:, BRANCH_W:])
    o_c = (hf_ref[...] + hb_ref[...]) * jax.nn.gelu(lg_ref[...])
    branches = (oa_ref[...], o_b.astype(BF16), o_c.astype(BF16), od_ref[...])

    acc = jnp.zeros(x.shape, F32)
    for n in range(N_BRANCH):
        gate = jnp.dot(h, wg_ref[:, D_MODEL * n:D_MODEL * (n + 1)], preferred_element_type=F32)
        proj = jnp.dot(branches[n], wb_ref[n], preferred_element_type=F32)
        acc = acc + jax.nn.sigmoid(gate) * proj
    out = jnp.dot(acc.astype(BF16), wo_ref[...], preferred_element_type=F32)
    o_ref[...] = x + gt_ref[...] * out


def _merge(x, sc, sh, gt, lw, o_a, s5u_t, yf_t, yb_t, hf_t, hb_t, lg, o_d):
    B, L, _ = x.shape
    TL = TOK_TILE
    per_b = sc.shape[0] > 1

    def mod_map(b, i):
        return (b if per_b else 0, 0, 0)

    tok = lambda c: pl.BlockSpec((None, TL, c), lambda b, i: (b, i, 0))
    tview = pl.BlockSpec((None, TL, 256), lambda b, i: (b // SUBLANES, i, b % SUBLANES))
    mod = pl.BlockSpec((None, 1, D_MODEL), mod_map)
    return pl.pallas_call(
        _merge_kernel,
        grid=(B, L // TL),
        in_specs=[tok(D_MODEL), mod, mod, mod, _full((1, D_MODEL)), _full((D_MODEL, N_BRANCH * D_MODEL)),
                  tok(256), tview, tview, tview, _full((1, 256)), _full((256, 512)),
                  tview, tview, tok(256), tok(256), _full((N_BRANCH, BRANCH_W, D_MODEL)),
                  _full((D_MODEL, D_MODEL))],
        out_specs=tok(D_MODEL),
        out_shape=jax.ShapeDtypeStruct((B, L, D_MODEL), F32),
        compiler_params=_params(("parallel", "parallel")),
        name="merge",
    )(x, sc, sh, gt, lw["g1"], lw["wg"], o_a, s5u_t, yf_t, yb_t, lw["s5_d"], lw["wglu"],
      hf_t, hb_t, lg, o_d, lw["wb"], lw["wo"])


def _first_index(values, target):
    idx = jnp.full_like(target, float(len(values) - 1))
    for n in range(len(values) - 2, -1, -1):
        idx = jnp.where(values[n] == target, float(n), idx)
    return idx


def _list_max(values):
    return functools.reduce(jnp.maximum, values)


def _moe_sort_kernel(x_ref, sc_ref, sh_ref, g2_ref, wrh_ref, wrl_ref, br_ref, tri_ref,
                     hs_ref, gs_ref, dcol_ref, meta_ref):
    for s in range(x_ref.shape[0] // MOE_SORT_TILE):
        tok = pl.ds(s * MOE_SORT_TILE, MOE_SORT_TILE)
        srt = pl.ds(s * MOE_SORTED_ROWS, MOE_SORTED_ROWS)
        _moe_sort_tile(x_ref.at[tok], sc_ref, sh_ref, g2_ref, wrh_ref, wrl_ref, br_ref, tri_ref,
                       hs_ref.at[srt], gs_ref.at[srt], dcol_ref.at[tok], meta_ref.at[s])


def _moe_sort_tile(x_ref, sc_ref, sh_ref, g2_ref, wrh_ref, wrl_ref, br_ref, tri_ref,
                   hs_ref, gs_ref, dcol_ref, meta_ref):
    T = MOE_SORT_TILE
    h = _modnorm(x_ref[...], g2_ref[...], sc_ref[...], sh_ref[...])
    hh = h.astype(BF16)
    hl = (h - hh.astype(F32)).astype(BF16)
    nt = (((1,), (1,)), ((), ()))
    lt = (lax.dot_general(wrh_ref[...], hh, nt, preferred_element_type=F32)
          + lax.dot_general(wrh_ref[...], hl, nt, preferred_element_type=F32)
          + lax.dot_general(wrl_ref[...], hh, nt, preferred_element_type=F32)) + br_ref[...]
    gl = [lt[g:g + 1, :] for g in range(N_GROUPS)]
    gmax = _list_max(gl)
    g_idx = _first_index(gl, gmax)
    pg = 1.0 / sum(jnp.exp(v - gmax) for v in gl)
    hot = [g_idx == float(g) for g in range(N_GROUPS)]
    el = []
    for e in range(EXPERTS_PER_GROUP):
        v = jnp.zeros_like(gmax)
        for g in range(N_GROUPS):
            r = N_GROUPS + EXPERTS_PER_GROUP * g + e
            v = jnp.where(hot[g], lt[r:r + 1, :], v)
        el.append(v)
    emax = _list_max(el)
    ee = [jnp.exp(v - emax) for v in el]
    esum = sum(ee)
    pe = [v / esum for v in ee]
    v1 = _list_max(pe)
    i1 = _first_index(pe, v1)
    pe2 = [jnp.where(i1 == float(e), -1.0, pe[e]) for e in range(EXPERTS_PER_GROUP)]
    v2 = _list_max(pe2)
    i2 = _first_index(pe2, v2)
    tot = v1 + v2
    w = [jnp.where(i1 == float(e), pg * v1 / tot, jnp.where(i2 == float(e), pg * v2 / tot, 0.0))
         for e in range(EXPERTS_PER_GROUP)]

    zero_row = jnp.zeros_like(gmax)
    g8 = jnp.concatenate([jnp.where(hot[g], 1.0, 0.0) for g in range(N_GROUPS)] + [zero_row] * 4, axis=0)
    cum = jnp.dot(g8.astype(BF16), tri_ref[...], preferred_element_type=F32)
    off = jnp.zeros((1, 1), F32)
    dest = zero_row
    counts = []
    for g in range(N_GROUPS):
        cnt = cum[g:g + 1, T - 1:T]
        padded = jnp.floor((cnt + (MOE_ROW_ALIGN - 1.0)) * (1.0 / MOE_ROW_ALIGN)) * MOE_ROW_ALIGN
        dest = jnp.where(hot[g], off + cum[g:g + 1, :] - 1.0, dest)
        off = off + padded
        counts.append(padded)
    rows = lax.broadcasted_iota(jnp.int32, (MOE_SORTED_ROWS, T), 0).astype(F32)
    perm = jnp.where(rows == dest, 1.0, 0.0).astype(BF16)
    hs_ref[...] = jnp.dot(perm, hh, preferred_element_type=F32).astype(BF16)
    gates = jnp.concatenate(w + [jnp.zeros((LANES - EXPERTS_PER_GROUP, T), F32)], axis=0)
    ghi = gates.astype(BF16)
    glo = (gates - ghi.astype(F32)).astype(BF16)
    gs_ref[...] = (lax.dot_general(perm, ghi, nt, preferred_element_type=F32)
                   + lax.dot_general(perm, glo, nt, preferred_element_type=F32))
    r_i = lax.broadcasted_iota(jnp.int32, (T, T), 0)
    c_i = lax.broadcasted_iota(jnp.int32, (T, T), 1)
    dcol = jnp.sum(jnp.where(r_i == c_i, jnp.broadcast_to(dest, (T, T)), 0.0), axis=1, keepdims=True)
    dcol_ref[...] = jnp.broadcast_to(dcol, (T, LANES))
    meta_ref[...] = jnp.concatenate([jnp.broadcast_to(c, (1, LANES)) for c in counts]
                                    + [jnp.zeros((SUBLANES - N_GROUPS, LANES), F32)], axis=0)


def _moe_sort(xf, sc, sh, lw, tiles_per_mod):
    N = xf.shape[0]
    T = MOE_SORT_TILE
    n_tiles = N // T
    sub = MOE_SORT_SUBTILES
    per_b = sc.shape[0] > 1
    mod = pl.BlockSpec((None, 1, D_MODEL), lambda t: (t * sub // tiles_per_mod if per_b else 0, 0, 0))
    sds = jax.ShapeDtypeStruct
    return pl.pallas_call(
        _moe_sort_kernel,
        grid=(n_tiles // sub,),
        in_specs=[pl.BlockSpec((sub * T, D_MODEL), lambda t: (t, 0)), mod, mod, _full((1, D_MODEL)),
                  _full((LANES, D_MODEL)), _full((LANES, D_MODEL)), _full((LANES, 1)), _full((T, T))],
        out_specs=[pl.BlockSpec((sub * MOE_SORTED_ROWS, D_MODEL), lambda t: (t, 0)),
                   pl.BlockSpec((sub * MOE_SORTED_ROWS, LANES), lambda t: (t, 0)),
                   pl.BlockSpec((sub * T, LANES), lambda t: (t, 0)),
                   pl.BlockSpec((sub, SUBLANES, LANES), lambda t: (t, 0, 0))],
        out_shape=[sds((n_tiles * MOE_SORTED_ROWS, D_MODEL), BF16), sds((n_tiles * MOE_SORTED_ROWS, LANES), F32),
                   sds((N, LANES), F32), sds((n_tiles, SUBLANES, LANES), F32)],
        compiler_params=_params(("parallel",)),
        name="moe_sort",
    )(xf, sc, sh, lw["g2"], lw["wrt_hi"], lw["wrt_lo"], lw["brt"], lw["tri"])


def _moe_expert_kernel(cnt_ref, hs_ref, gs_ref, wga_ref, wup_ref, wdn_ref, ys_ref, ch_ref, cg_ref, cy_ref):
    j = pl.program_id(0)
    g = pl.program_id(1)
    piece = MOE_ROW_ALIGN

    @pl.when((j == 0) & (g == 0))
    def _():
        ch_ref[...] = jnp.zeros_like(ch_ref)
        cg_ref[...] = jnp.zeros_like(cg_ref)

    @pl.when(g == 0)
    def _():
        ys_ref[...] = jnp.zeros_like(ys_ref)

    def for_each_piece(move):
        packed = jnp.int32(0)
        for t in range(MOE_BLOCK_TILES):
            base = (j * MOE_BLOCK_TILES + t) * N_GROUPS
            start = jnp.int32(t * MOE_SORTED_ROWS)
            for g2 in range(N_GROUPS - 1):
                start = start + jnp.where(g2 < g, cnt_ref[base + g2], 0)
            n = cnt_ref[base + g]

            def body(k, carry, start=start, packed=packed):
                move(pl.multiple_of(start + k * piece, piece), pl.multiple_of(packed + k * piece, piece))
                return carry

            lax.fori_loop(0, lax.shift_right_logical(n, 4), body, 0)
            packed = packed + n
        return packed

    def pack(src, dst):
        ch_ref[pl.ds(dst, piece), :] = hs_ref[pl.ds(src, piece), :]
        cg_ref[pl.ds(dst, piece), :] = gs_ref[pl.ds(src, piece), :]

    rows = for_each_piece(pack)

    def chunk(c, carry):
        r = pl.multiple_of(c * MOE_CHUNK, MOE_CHUNK)
        h = ch_ref[pl.ds(r, MOE_CHUNK), :]
        acc = jnp.zeros((MOE_CHUNK, D_MODEL), F32)
        for e in range(EXPERTS_PER_GROUP):
            a = jnp.dot(h, wga_ref[e], preferred_element_type=F32)
            u = jnp.dot(h, wup_ref[e], preferred_element_type=F32)
            act = (a * jax.nn.sigmoid(a)) * u * cg_ref[pl.ds(r, MOE_CHUNK), e:e + 1]
            acc = acc + jnp.dot(act.astype(BF16), wdn_ref[e], preferred_element_type=F32)
        cy_ref[pl.ds(r, MOE_CHUNK), :] = acc.astype(BF16)
        return carry

    lax.fori_loop(0, lax.shift_right_logical(rows + (MOE_CHUNK - 1), MOE_CHUNK.bit_length() - 1), chunk, 0)

    def unpack(src, dst):
        ys_ref[pl.ds(src, piece), :] = cy_ref[pl.ds(dst, piece), :]

    for_each_piece(unpack)


def _moe_experts(cnt, hs, gs, lw):
    rows = MOE_BLOCK_TILES * MOE_SORTED_ROWS
    cap = MOE_BLOCK_TILES * MOE_SORT_TILE
    blk = lambda c: pl.BlockSpec((rows, c), lambda j, g, cnt: (j, 0))
    wspec = lambda r, c: pl.BlockSpec((EXPERTS_PER_GROUP, r, c), lambda j, g, cnt: (g, 0, 0))
    return pl.pallas_call(
        _moe_expert_kernel,
        grid_spec=pltpu.PrefetchScalarGridSpec(
            num_scalar_prefetch=1, grid=(hs.shape[0] // rows, N_GROUPS),
            in_specs=[blk(D_MODEL), blk(LANES), wspec(D_MODEL, EXPERT_FF), wspec(D_MODEL, EXPERT_FF),
                      wspec(EXPERT_FF, D_MODEL)],
            out_specs=blk(D_MODEL),
            scratch_shapes=[pltpu.VMEM((cap, D_MODEL), BF16), pltpu.VMEM((cap, LANES), F32),
                            pltpu.VMEM((cap, D_MODEL), BF16)]),
        out_shape=jax.ShapeDtypeStruct(hs.shape, BF16),
        compiler_params=_params(("arbitrary", "arbitrary")),
        name="moe_experts",
    )(cnt, hs, gs, lw["wga"], lw["wup"], lw["wdn"])


def _moe_unsort_kernel(x_ref, gt_ref, ys_ref, dcol_ref, o_ref):
    cols = lax.broadcasted_iota(jnp.int32, (MOE_SORT_TILE, MOE_SORTED_ROWS), 1).astype(F32)
    perm_t = jnp.where(cols == dcol_ref[:, 0:1], 1.0, 0.0).astype(BF16)
    y = jnp.dot(perm_t, ys_ref[...], preferred_element_type=F32)
    o_ref[...] = x_ref[...] + gt_ref[...] * y


def _moe_unsort(xf, gt, ys, dcol, tiles_per_mod):
    N = xf.shape[0]
    T = MOE_SORT_TILE
    per_b = gt.shape[0] > 1
    mod = pl.BlockSpec((None, 1, D_MODEL), lambda t: (t // tiles_per_mod if per_b else 0, 0, 0))
    tok = pl.BlockSpec((T, D_MODEL), lambda t: (t, 0))
    return pl.pallas_call(
        _moe_unsort_kernel,
        grid=(N // T,),
        in_specs=[tok, mod, pl.BlockSpec((MOE_SORTED_ROWS, D_MODEL), lambda t: (t, 0)),
                  pl.BlockSpec((T, LANES), lambda t: (t, 0))],
        out_specs=tok,
        out_shape=jax.ShapeDtypeStruct((N, D_MODEL), F32),
        compiler_params=_params(("parallel",)),
        name="moe_unsort",
    )(xf, gt, ys, dcol)


def _moe(x, sc, sh, gt, lw):
    B, L, _ = x.shape
    N = B * L
    T = MOE_SORT_TILE
    xf = x.reshape(N, D_MODEL)
    hs, gs, dcol, meta = _moe_sort(xf, sc, sh, lw, L // T)

    cnt = meta[:, :N_GROUPS, 0].astype(jnp.int32).reshape(-1)
    ys = _moe_experts(cnt, hs, gs, lw)
    out = _moe_unsort(xf, gt, ys, dcol, L // T)
    return out.reshape(B, L, D_MODEL)


def _block_diag(blocks):
    n, r, c = blocks.shape
    eye = jnp.eye(n, dtype=blocks.dtype)
    return jnp.einsum("nrc,nm->nrmc", blocks, eye).reshape(n * r, n * c)


def _rope_tables(seq_len, dim, width, offset):
    rows = seq_len // GRID_W
    r, col = jnp.meshgrid(jnp.arange(rows), jnp.arange(GRID_W), indexing="ij")
    r = r.reshape(-1).astype(F32)
    col = col.reshape(-1).astype(F32)
    quarter = dim // 4
    freqs = ROPE_BASE ** (-jnp.arange(quarter, dtype=F32) / quarter)
    ang_r = r[:, None] * freqs
    ang_c = col[:, None] * freqs
    zero = jnp.zeros_like(ang_r)
    cos = jnp.cos(jnp.concatenate([ang_r, ang_r, ang_c, ang_c], axis=-1))
    sin_next = jnp.concatenate([-jnp.sin(ang_r), zero, -jnp.sin(ang_c), zero], axis=-1)
    sin_prev = jnp.concatenate([zero, jnp.sin(ang_r), zero, jnp.sin(ang_c)], axis=-1)

    def place(t, fill):
        return jnp.pad(t, ((0, 0), (offset, width - offset - dim)), constant_values=fill)

    return place(cos, 1.0), place(sin_next, 0.0), place(sin_prev, 0.0)


def _layer_weights(l, w):
    lw = {}
    w_in = w["w_in"][l]
    q_cols = w_in[:, 0:256].reshape(D_MODEL, A_HEADS, A_HEAD_DIM)[:, HEAD_ORDER].reshape(D_MODEL, 256)
    kr_cols = jnp.pad(w_in[:, 1664:1696], ((0, 0), (MLA_NOPE, LANES - MLA_QK)))
    lw["wz"] = jnp.concatenate([q_cols, w_in[:, 256:1664], kr_cols], axis=1).astype(BF16)
    lw["wg"] = w_in[:, 1696:].astype(BF16)
    lw["g1"] = w["norm1_g"][l][None]
    lw["g2"] = w["norm2_g"][l][None]
    lw["gq"] = jnp.tile(w["a_qnorm_g"][l], A_HEADS)[None] * (A_HEAD_DIM ** -0.5)
    lw["gk"] = jnp.tile(w["a_knorm_g"][l], A_KV_HEADS)[None]
    lw["sink"] = w["a_sink"][l]
    lw["qlg"] = w["mla_q_lat_norm"][l][None]
    wuq = w["mla_w_uq"][l].reshape(MLA_Q_LORA, MLA_HEADS, MLA_QK)
    lw["wuq"] = jnp.pad(wuq, ((0, 0), (0, 0), (0, LANES - MLA_QK))).reshape(MLA_Q_LORA, MLA_HEADS * LANES).astype(BF16)
    pad_g = lambda g: jnp.tile(jnp.pad(g, (0, LANES - MLA_QK)), MLA_HEADS)[None]
    lw["gmq"] = pad_g(w["mla_qnorm_g"][l]) * (MLA_QK ** -0.5 * math.log2(math.e))
    lw["gmk"] = pad_g(w["mla_knorm_g"][l])
    lw["kvg"] = w["mla_kv_norm"][l][None]
    wukv = w["mla_w_ukv"][l].reshape(MLA_KV_LORA, MLA_HEADS, MLA_NOPE + MLA_V)
    wk = jnp.pad(wukv[:, :, :MLA_NOPE], ((0, 0), (0, 0), (0, LANES - MLA_NOPE))).reshape(MLA_KV_LORA, MLA_HEADS * LANES)
    wv = wukv[:, :, MLA_NOPE:].reshape(MLA_KV_LORA, MLA_HEADS * MLA_V)
    lw["wukv"] = jnp.concatenate([wk, wv], axis=1).astype(BF16)

    lre = w["s5_lambda_re"][l]
    lim = w["s5_lambda_im"][l]
    dt = jnp.exp(w["s5_log_step"][l])[:, :, None]
    mag = jnp.exp(lre * dt)
    ar, ai = mag * jnp.cos(lim * dt), mag * jnp.sin(lim * dt)
    den = lre * lre + lim * lim
    fr = ((ar - 1.0) * lre + ai * lim) / den
    fi = (ai * lre - (ar - 1.0) * lim) / den
    br, bi = w["s5_b_re"][l], w["s5_b_im"][l]
    bbr = fr[..., None] * br - fi[..., None] * bi
    bbi = fr[..., None] * bi + fi[..., None] * br
    in_map = lambda d: jnp.concatenate(
        [_block_diag(jnp.swapaxes(bbr[d], 1, 2)), _block_diag(jnp.swapaxes(bbi[d], 1, 2))], axis=1).astype(BF16)
    out_map = lambda d: jnp.concatenate(
        [_block_diag(jnp.swapaxes(w["s5_c_re"][l][d], 1, 2)),
         -_block_diag(jnp.swapaxes(w["s5_c_im"][l][d], 1, 2))], axis=0).astype(BF16)
    lw["s5_bf"], lw["s5_bb"] = in_map(0), in_map(1)
    lw["s5_cf"], lw["s5_cb"] = out_map(0), out_map(1)
    coef = jnp.stack([ar[0].reshape(-1), ai[0].reshape(-1), ar[1].reshape(-1), ai[1].reshape(-1)])
    lw["s5_a"] = jnp.broadcast_to(coef[:, None, :], (4, SUBLANES, S5_WIDTH))
    lw["s5_d"] = w["s5_d"][l][None]
    lw["wglu"] = w["s5_w_glu"][l].astype(BF16)

    lw["lru_cw"] = w["lru_conv_w"][l]
    lw["lru_cb"] = w["lru_conv_b"][l][None]
    gate_w = lambda d: jnp.concatenate(
        [_block_diag(w["lru_w_a"][l][d]), _block_diag(w["lru_w_x"][l][d])], axis=1).astype(BF16)
    gate_b = lambda d: jnp.concatenate([w["lru_b_a"][l][d], w["lru_b_x"][l][d]])[None]
    lw["lru_wf"], lw["lru_wb"] = gate_w(0), gate_w(1)
    lw["lru_bf"], lw["lru_bb"] = gate_b(0), gate_b(1)
    sp = jax.nn.softplus(-w["lru_lambda"][l])
    lw["lru_spf"], lw["lru_spb"] = sp[0][None], sp[1][None]

    wb = w["w_branch"][l]
    wb0 = wb[0].reshape(A_HEADS, A_HEAD_DIM, D_MODEL)[HEAD_ORDER].reshape(BRANCH_W, D_MODEL)
    lw["wb"] = jnp.concatenate([wb0[None], wb[1:]], axis=0).astype(BF16)
    lw["wo"] = w["w_out"][l].astype(BF16)

    wr = jnp.pad(jnp.concatenate([w["moe_w_group"][l], w["moe_w_expert"][l]], axis=1),
                 ((0, 0), (0, LANES - N_GROUPS - N_EXPERTS)))
    lw["wrt_hi"] = wr.T.astype(BF16)
    lw["wrt_lo"] = (wr.T - lw["wrt_hi"].astype(F32)).astype(BF16)
    lw["brt"] = jnp.pad(jnp.concatenate([w["moe_b_group"][l], w["moe_b_expert"][l]]),
                        (0, LANES - N_GROUPS - N_EXPERTS))[:, None]
    lw["tri"] = jnp.triu(jnp.ones((MOE_SORT_TILE, MOE_SORT_TILE), BF16))
    lw["wga"] = w["moe_w_gate"][l].astype(BF16)
    lw["wup"] = w["moe_w_up"][l].astype(BF16)
    lw["wdn"] = w["moe_w_down"][l].astype(BF16)
    return lw


def _rows_to_state(fin, width):
    Bg, n = fin.shape[0], fin.shape[1]
    return jnp.swapaxes(fin, 1, 2).reshape(Bg * SUBLANES, n, width)


def _state_to_rows(state):
    B, n, width = state.shape
    return jnp.swapaxes(state.reshape(B // SUBLANES, SUBLANES, n, width), 1, 2)


def _mix(x, mods, lw, rope_tabs, cache, s5_h0, lru_h0, ctx):
    sh1, sc1, gt1 = mods
    B, L, _ = x.shape
    Bg = B // SUBLANES
    outs = _in_proj(x, sc1, sh1, lw, rope_tabs, ctx)
    q, k, v, s5u_t, lx_t, lg, qm, km, vm = outs[:9]
    if ctx:
        o_a = _attn_a_ctx(q, k, v, lw["sink"])
        o_d = _mla_ctx(qm, km, vm)
    else:
        ck, cv, cckv, ckr = cache
        o_a = _attn_a_lat(q, k, v, ck, cv, lw["sink"])
        kc, vc = _mla_cache_kv(cckv, ckr, lw)
        o_d = _mla_lat(qm, km, vm, kc, vc)
    rows = lambda t: t.reshape(Bg, L * SUBLANES, 256)
    view = lambda t: t.reshape(Bg, L, SUBLANES * 256)
    yf, yb, s5_fin = _s5_scan(rows(s5u_t), lw, s5_h0)
    hf, hb, lru_fin = _lru_scan(rows(lx_t), lw, lru_h0)
    x = _merge(x, sc1, sh1, gt1, lw, o_a, s5u_t, view(yf), view(yb), view(hf), view(hb), lg, o_d)
    return x, outs[9:], s5_fin, lru_fin


def kernel(x_prompt, x_sample, cache_attn_k, cache_attn_v, cache_mla_ckv, cache_mla_krope, state_ssm_re, state_ssm_im, state_lru, c, c_ctx, w_mod, b_mod, norm1_g, norm2_g, w_in, a_qnorm_g, a_knorm_g, a_sink, s5_lambda_re, s5_lambda_im, s5_log_step, s5_b_re, s5_b_im, s5_c_re, s5_c_im, s5_d, s5_w_glu, lru_conv_w, lru_conv_b, lru_w_a, lru_b_a, lru_w_x, lru_b_x, lru_lambda, mla_q_lat_norm, mla_w_uq, mla_kv_norm, mla_w_ukv, mla_qnorm_g, mla_knorm_g, w_branch, w_out, moe_w_group, moe_b_group, moe_w_expert, moe_b_expert, moe_w_gate, moe_w_up, moe_w_down):
    w = dict(norm1_g=norm1_g, norm2_g=norm2_g, w_in=w_in, a_qnorm_g=a_qnorm_g, a_knorm_g=a_knorm_g,
             a_sink=a_sink, s5_lambda_re=s5_lambda_re, s5_lambda_im=s5_lambda_im, s5_log_step=s5_log_step,
             s5_b_re=s5_b_re, s5_b_im=s5_b_im, s5_c_re=s5_c_re, s5_c_im=s5_c_im, s5_d=s5_d, s5_w_glu=s5_w_glu,
             lru_conv_w=lru_conv_w, lru_conv_b=lru_conv_b, lru_w_a=lru_w_a, lru_b_a=lru_b_a, lru_w_x=lru_w_x,
             lru_b_x=lru_b_x, lru_lambda=lru_lambda, mla_q_lat_norm=mla_q_lat_norm, mla_w_uq=mla_w_uq,
             mla_kv_norm=mla_kv_norm, mla_w_ukv=mla_w_ukv, mla_qnorm_g=mla_qnorm_g, mla_knorm_g=mla_knorm_g,
             w_branch=w_branch, w_out=w_out, moe_w_group=moe_w_group, moe_b_group=moe_b_group,
             moe_w_expert=moe_w_expert, moe_b_expert=moe_b_expert, moe_w_gate=moe_w_gate, moe_w_up=moe_w_up,
             moe_w_down=moe_w_down)
    B, L, _ = x_prompt.shape
    Bd, Ld, _ = x_sample.shape
    P = cache_attn_k.shape[2]

    cond = jnp.zeros((2 * SUBLANES, D_MODEL), F32).at[:Bd].set(c).at[Bd].set(c_ctx)
    mod = _modulation(cond, w_mod.astype(BF16), b_mod[:, None, :])
    mod = mod.reshape(DEPTH, 2 * SUBLANES, 6, D_MODEL)

    rope_a = _rope_tables(Ld, A_HEAD_DIM, A_HEAD_DIM, 0)
    rope_a = tuple(jnp.tile(t, (1, A_HEADS)) for t in rope_a)
    rope_m = _rope_tables(Ld, MLA_ROPE, LANES, MLA_NOPE)

    xp, xs = x_prompt, x_sample
    ak_l, av_l, ckv_l, kr_l, sr_l, si_l, lru_l = [], [], [], [], [], [], []
    for l in range(DEPTH):
        lw = _layer_weights(l, w)
        lat_mod = [mod[l, :Bd, n][:, None, :] for n in range(6)]
        ctx_mod = [mod[l, Bd:Bd + 1, n][:, None, :] for n in range(6)]

        zs5 = jnp.zeros((B // SUBLANES, 4, SUBLANES, S5_WIDTH), F32)
        zlru = jnp.zeros((B // SUBLANES, 2, SUBLANES, LRU_WIDTH), F32)
        xp, (k32, v32, ckv_n, krp), s5_fin, lru_fin = _mix(xp, ctx_mod[0:3], lw, None, None, zs5, zlru, True)
        xp = _moe(xp, ctx_mod[4], ctx_mod[3], ctx_mod[5], lw)
        ak_l.append(k32.reshape(B, L, A_KV_HEADS, A_HEAD_DIM))
        av_l.append(v32.reshape(B, L, A_KV_HEADS, A_HEAD_DIM))
        ckv_l.append(ckv_n)
        kr_l.append(krp[:, :, MLA_NOPE:MLA_QK])
        fin = _rows_to_state(s5_fin, S5_WIDTH)
        sr_l.append(fin[:, 0::2].reshape(B, 2, S5_GROUPS, S5_STATE))
        si_l.append(fin[:, 1::2].reshape(B, 2, S5_GROUPS, S5_STATE))
        lru_l.append(_rows_to_state(lru_fin, LRU_WIDTH))

        sre = state_ssm_re[:, l].reshape(Bd, 2, S5_WIDTH)
        sim = state_ssm_im[:, l].reshape(Bd, 2, S5_WIDTH)
        s5_h0 = _state_to_rows(jnp.stack([sre[:, 0], sim[:, 0], sre[:, 1], sim[:, 1]], axis=1))
        lru_h0 = _state_to_rows(state_lru[:, l])
        cache = (cache_attn_k[:, l].reshape(Bd, P, 128), cache_attn_v[:, l].reshape(Bd, P, 128),
                 cache_mla_ckv[:, l],
                 jnp.pad(cache_mla_krope[:, l], ((0, 0), (0, 0), (MLA_NOPE, LANES - MLA_QK))))
        xs, _, _, _ = _mix(xs, lat_mod[0:3], lw, rope_a + rope_m, cache, s5_h0, lru_h0, False)
        xs = _moe(xs, lat_mod[4], lat_mod[3], lat_mod[5], lw)

    stack = lambda ts: jnp.stack(ts, axis=1)
    return (xp, xs, stack(ak_l), stack(av_l), stack(ckv_l), stack(kr_l), stack(sr_l), stack(si_l), stack(lru_l))
```

```python
import functools
import math

import jax
import jax.numpy as jnp
import numpy as np
from jax import lax
from jax.experimental import pallas as pl
from jax.experimental.pallas import tpu as pltpu

F32 = jnp.float32
BF16 = jnp.bfloat16

D_MODEL = 1024
DEPTH = 2
GRID_W = 64
N_BRANCH = 4
BRANCH_W = 256
ROPE_BASE = 10000.0
EPS = 1e-6
NEG_INF = -1e30
A_HEADS = 4
A_KV_HEADS = 2
A_HEAD_DIM = 64
WINDOW = 128
Q_BLOCK = 128
S5_GROUP = 16
S5_GROUPS = 16
S5_STATE = 64
S5_WIDTH = S5_GROUPS * S5_STATE
LRU_WIDTH = 256
LRU_BLOCKS = 4
LRU_CONV = 4
LRU_C = 8.0
MLA_HEADS = 4
MLA_Q_LORA = 256
MLA_KV_LORA = 128
MLA_NOPE = 64
MLA_ROPE = 32
MLA_V = 64
MLA_QK = MLA_NOPE + MLA_ROPE
N_GROUPS = 4
EXPERTS_PER_GROUP = 4
N_EXPERTS = 16
EXPERT_FF = 256

LANES = 128
SUBLANES = 8
VMEM_LIMIT = 56 * 1024 * 1024

TOK_TILE = 512
SCAN_STEPS = 64
MOE_SORT_TILE = 256
MOE_ROW_ALIGN = 16
MOE_SORTED_ROWS = MOE_SORT_TILE + LANES
MOE_SORT_SUBTILES = 4
MOE_BLOCK_TILES = 8
MOE_CHUNK = 256
MLA_Q_TILE = 256
Z_COLS = 1792
HEAD_ORDER = np.array((0, 2, 1, 3))


def _params(sem):
    return pltpu.CompilerParams(dimension_semantics=sem, vmem_limit_bytes=VMEM_LIMIT)


def _full(shape):
    n = len(shape)
    return pl.BlockSpec(shape, lambda *_: (0,) * n)


def _modnorm(x, g, sc, sh):
    ms = jnp.mean(x * x, axis=-1, keepdims=True)
    return (x * lax.rsqrt(ms + EPS)) * g * (1.0 + sc) + sh


def _rmsnorm(x, g):
    ms = jnp.mean(x * x, axis=-1, keepdims=True)
    return (x * lax.rsqrt(ms + EPS)) * g


def _half_rmsnorm(x):
    lane = lax.broadcasted_iota(jnp.int32, (1, LANES), 1)
    lo = lane < A_HEAD_DIM
    outs = []
    for t in range(x.shape[1] // LANES):
        xt = x[:, LANES * t:LANES * (t + 1)]
        sq = xt * xt
        s_lo = jnp.sum(jnp.where(lo, sq, 0.0), axis=-1, keepdims=True)
        s_hi = jnp.sum(jnp.where(lo, 0.0, sq), axis=-1, keepdims=True)
        inv = jnp.where(lo, lax.rsqrt(s_lo * (1.0 / A_HEAD_DIM) + EPS),
                        lax.rsqrt(s_hi * (1.0 / A_HEAD_DIM) + EPS))
        outs.append(xt * inv)
    return jnp.concatenate(outs, axis=1)


def _tile_rmsnorm(x, n_real):
    outs = []
    for t in range(x.shape[1] // LANES):
        xt = x[:, LANES * t:LANES * (t + 1)]
        ss = jnp.sum(xt * xt, axis=-1, keepdims=True)
        outs.append(xt * lax.rsqrt(ss * (1.0 / n_real) + EPS))
    return jnp.concatenate(outs, axis=1)


def _rope(x, cos, sin_next, sin_prev, quarter):
    width = x.shape[1]
    return (x * cos + pltpu.roll(x, width - quarter, 1) * sin_next
            + pltpu.roll(x, quarter, 1) * sin_prev)


def _mla_kv(ckv_n, kr_placed, wukv, gk, rope_tabs):
    kv = jnp.dot(ckv_n.astype(BF16), wukv, preferred_element_type=F32)
    kfull = kv[:, :MLA_HEADS * LANES] + jnp.concatenate([kr_placed] * MLA_HEADS, axis=1)
    kn = _tile_rmsnorm(kfull, MLA_QK) * gk
    if rope_tabs is not None:
        kn = _rope(kn, *rope_tabs, MLA_ROPE // 4)
    return kn, kv[:, MLA_HEADS * LANES:]


def _mod_kernel(c_ref, w_ref, b_ref, o_ref):
    c = c_ref[...]
    s = c * jax.nn.sigmoid(c)
    o_ref[...] = jnp.dot(s.astype(BF16), w_ref[...], preferred_element_type=F32) + b_ref[...]


def _modulation(cond, w_mod, b_mod):
    n_rows = cond.shape[0]
    n_out = w_mod.shape[-1]
    tn = 1024
    return pl.pallas_call(
        _mod_kernel,
        grid=(DEPTH, n_out // tn),
        in_specs=[pl.BlockSpec((n_rows, D_MODEL), lambda l, j: (0, 0)),
                  pl.BlockSpec((None, D_MODEL, tn), lambda l, j: (l, 0, j)),
                  pl.BlockSpec((None, 1, tn), lambda l, j: (l, 0, j))],
        out_specs=pl.BlockSpec((None, n_rows, tn), lambda l, j: (l, 0, j)),
        out_shape=jax.ShapeDtypeStruct((DEPTH, n_rows, n_out), F32),
        compiler_params=_params(("arbitrary", "arbitrary")),
        name="modulation",
    )(cond, w_mod, b_mod)


def _in_proj_kernel(rope, ctx, *refs):
    it = iter(refs)
    x_ref, sc_ref, sh_ref, g1_ref, wz_ref = (next(it) for _ in range(5))
    gq_ref, gk_ref, qlg_ref, wuq_ref, gmq_ref, kvg_ref, wukv_ref, gmk_ref = (next(it) for _ in range(8))
    if rope:
        ca_ref, sna_ref, spa_ref, cm_ref, snm_ref, spm_ref = (next(it) for _ in range(6))
    q_ref, k_ref, v_ref, s5u_ref, lx_ref, lg_ref, qm_ref, km_ref, vm_ref = (next(it) for _ in range(9))
    if ctx:
        k32_ref, v32_ref, ckv_ref, krp_ref = (next(it) for _ in range(4))

    h = _modnorm(x_ref[...], g1_ref[...], sc_ref[...], sh_ref[...])
    z = jnp.dot(h.astype(BF16), wz_ref[...], preferred_element_type=F32)

    q = _half_rmsnorm(z[:, 0:256]) * gq_ref[...]
    k = _half_rmsnorm(z[:, 256:384]) * gk_ref[...]
    v = z[:, 384:512]
    if ctx:
        k32_ref[...] = k
        v32_ref[...] = v
    if rope:
        quarter = A_HEAD_DIM // 4
        q = _rope(q, ca_ref[...], sna_ref[...], spa_ref[...], quarter)
        k = _rope(k, ca_ref[:, :LANES], sna_ref[:, :LANES], spa_ref[:, :LANES], quarter)
    q_ref[...] = q.astype(BF16)
    k_ref[...] = k.astype(BF16)
    v_ref[...] = v.astype(BF16)

    s5u_ref[...] = z[:, 512:768]
    lx_ref[...] = z[:, 768:1024]
    lg_ref[...] = z[:, 1024:1280]

    mtabs = None
    mtabs4 = None
    if rope:
        mtabs = (cm_ref[...], snm_ref[...], spm_ref[...])
        mtabs4 = tuple(jnp.concatenate([t] * MLA_HEADS, axis=1) for t in mtabs)
    ql = _rmsnorm(z[:, 1280:1536], qlg_ref[...])
    qm = jnp.dot(ql.astype(BF16), wuq_ref[...], preferred_element_type=F32)
    qm = _tile_rmsnorm(qm, MLA_QK) * gmq_ref[...]
    if rope:
        qm = _rope(qm, *mtabs4, MLA_ROPE // 4)
    qm_ref[...] = qm.astype(BF16)

    ckv_n = _rmsnorm(z[:, 1536:1664], kvg_ref[...])
    krp = z[:, 1664:1792]
    km, vm = _mla_kv(ckv_n, krp, wukv_ref[...], gmk_ref[...], mtabs4)
    km_ref[...] = km.astype(BF16)
    vm_ref[...] = vm.astype(BF16)
    if ctx:
        ckv_ref[...] = ckv_n
        krp_ref[...] = krp


def _in_proj(x, sc, sh, lw, rope_tabs, ctx):
    B, L, _ = x.shape
    Bg = B // SUBLANES
    TL = TOK_TILE
    rope = rope_tabs is not None
    per_b = sc.shape[0] > 1

    def mod_map(b, i):
        return (b if per_b else 0, 0, 0)

    tok = lambda c: pl.BlockSpec((None, TL, c), lambda b, i: (b, i, 0))
    in_specs = [tok(D_MODEL), pl.BlockSpec((None, 1, D_MODEL), mod_map),
                pl.BlockSpec((None, 1, D_MODEL), mod_map), _full((1, D_MODEL)), _full((D_MODEL, Z_COLS)),
                _full((1, 256)), _full((1, 128)), _full((1, 256)), _full((256, 512)), _full((1, 512)),
                _full((1, 128)), _full((128, 768)), _full((1, 512))]
    args = [x, sc, sh, lw["g1"], lw["wz"], lw["gq"], lw["gk"], lw["qlg"], lw["wuq"], lw["gmq"],
            lw["kvg"], lw["wukv"], lw["gmk"]]
    if rope:
        in_specs += [pl.BlockSpec((TL, 256), lambda b, i: (i, 0))] * 3
        in_specs += [pl.BlockSpec((TL, 128), lambda b, i: (i, 0))] * 3
        args += list(rope_tabs)
    out_specs = [tok(256), tok(128), tok(128), tok(256), tok(256), tok(256), tok(512), tok(512), tok(256)]
    sds = jax.ShapeDtypeStruct
    out_shape = [sds((B, L, 256), BF16), sds((B, L, 128), BF16), sds((B, L, 128), BF16),
                 sds((B, L, 256), F32), sds((B, L, 256), F32),
                 sds((B, L, 256), F32), sds((B, L, 512), BF16), sds((B, L, 512), BF16),
                 sds((B, L, 256), BF16)]
    if ctx:
        out_specs += [tok(128), tok(128), tok(128), tok(128)]
        out_shape += [sds((B, L, 128), F32)] * 4
    return pl.pallas_call(
        functools.partial(_in_proj_kernel, rope, ctx),
        grid=(B, L // TL),
        in_specs=in_specs, out_specs=out_specs, out_shape=out_shape,
        compiler_params=_params(("parallel", "parallel")),
        name="in_proj_ctx" if ctx else "in_proj_lat",
    )(*args)


def _mla_cache_kernel(ckv_ref, krp_ref, wukv_ref, gmk_ref, km_ref, vm_ref):
    km, vm = _mla_kv(ckv_ref[...], krp_ref[...], wukv_ref[...], gmk_ref[...], None)
    km_ref[...] = km.astype(BF16)
    vm_ref[...] = vm.astype(BF16)


def _mla_cache_kv(cckv, ckr_placed, lw):
    B, P, _ = cckv.shape
    tok = lambda c: pl.BlockSpec((None, P, c), lambda b: (b, 0, 0))
    return pl.pallas_call(
        _mla_cache_kernel,
        grid=(B,),
        in_specs=[tok(128), tok(128), _full((128, 768)), _full((1, 512))],
        out_specs=[tok(512), tok(256)],
        out_shape=[jax.ShapeDtypeStruct((B, P, 512), BF16), jax.ShapeDtypeStruct((B, P, 256), BF16)],
        compiler_params=_params(("parallel",)),
        name="mla_cache_kv",
    )(cckv, ckr_placed, lw["wukv"], lw["gmk"])


def _sink_softmax(s, sink):
    m = jnp.maximum(jnp.max(s, axis=-1, keepdims=True), sink)
    e = jnp.exp(s - m)
    den = jnp.sum(e, axis=-1, keepdims=True) + jnp.exp(sink - m)
    return (e / den).astype(BF16)


def _gqa_tile(qt, keys, vst, sink_lo, sink_hi, mask):
    lane = lax.broadcasted_iota(jnp.int32, (1, LANES), 1)
    lo = lane < A_HEAD_DIM
    zero = jnp.zeros_like(qt)
    ps = []
    for qh, sink in ((jnp.where(lo, qt, zero), sink_lo), (jnp.where(lo, zero, qt), sink_hi)):
        s = lax.dot_general(qh, keys, (((1,), (1,)), ((), ())), preferred_element_type=F32)
        if mask is not None:
            s = jnp.where(mask, s, NEG_INF)
        ps.append(_sink_softmax(s, sink))
    return jnp.dot(jnp.concatenate(ps, axis=1), vst, preferred_element_type=F32)


def _stack_kv_halves(v):
    lane = lax.broadcasted_iota(jnp.int32, (1, LANES), 1)
    lo = lane < A_HEAD_DIM
    zero = jnp.zeros_like(v)
    return jnp.concatenate([jnp.where(lo, v, zero), jnp.where(lo, zero, v)], axis=0)


def _attn_a_ctx_kernel(sink_ref, q_ref, k_ref, v_ref, o_ref):
    keys = k_ref[...]
    vst = _stack_kv_halves(v_ref[...])
    for t in range(2):
        o = _gqa_tile(q_ref[:, LANES * t:LANES * (t + 1)], keys, vst, sink_ref[t], sink_ref[2 + t], None)
        o_ref[:, LANES * t:LANES * (t + 1)] = o.astype(o_ref.dtype)


def _attn_a_ctx(q, k, v, sink):
    B, L, _ = q.shape
    tok = lambda c: pl.BlockSpec((None, L, c), lambda b: (b, 0, 0))
    return pl.pallas_call(
        _attn_a_ctx_kernel,
        grid=(B,),
        in_specs=[pl.BlockSpec(memory_space=pltpu.SMEM), tok(256), tok(128), tok(128)],
        out_specs=tok(256),
        out_shape=jax.ShapeDtypeStruct((B, L, 256), BF16),
        compiler_params=_params(("parallel",)),
        name="attn_a_ctx",
    )(sink, q, k, v)


def _attn_a_lat_kernel(sink_ref, q_ref, kc_ref, vc_ref, k_ref, v_ref, o_ref):
    i = pl.program_id(1)
    L = k_ref.shape[0]
    n_ctx = kc_ref.shape[0]
    span = 3 * Q_BLOCK
    start = pl.multiple_of(jnp.clip((i - 1) * Q_BLOCK, 0, L - span), Q_BLOCK)
    keys = jnp.concatenate([kc_ref[...].astype(BF16), k_ref[pl.ds(start, span), :]], axis=0)
    vals = jnp.concatenate([vc_ref[...].astype(BF16), v_ref[pl.ds(start, span), :]], axis=0)
    vst = _stack_kv_halves(vals)
    col = lax.broadcasted_iota(jnp.int32, (Q_BLOCK, n_ctx + span), 1)
    row = lax.broadcasted_iota(jnp.int32, (Q_BLOCK, n_ctx + span), 0)
    rel = (col - n_ctx + start) - (row + i * Q_BLOCK)
    mask = (col < n_ctx) | (jnp.abs(rel) <= WINDOW)
    for t in range(2):
        o = _gqa_tile(q_ref[:, LANES * t:LANES * (t + 1)], keys, vst, sink_ref[t], sink_ref[2 + t], mask)
        o_ref[:, LANES * t:LANES * (t + 1)] = o.astype(o_ref.dtype)


def _attn_a_lat(q, k, v, kc, vc, sink):
    B, L, _ = q.shape
    P = kc.shape[1]
    blk = lambda c: pl.BlockSpec((None, Q_BLOCK, c), lambda b, i: (b, i, 0))
    whole = lambda n, c: pl.BlockSpec((None, n, c), lambda b, i: (b, 0, 0))
    return pl.pallas_call(
        _attn_a_lat_kernel,
        grid=(B, L // Q_BLOCK),
        in_specs=[pl.BlockSpec(memory_space=pltpu.SMEM), blk(256), whole(P, 128), whole(P, 128),
                  whole(L, 128), whole(L, 128)],
        out_specs=blk(256),
        out_shape=jax.ShapeDtypeStruct((B, L, 256), BF16),
        compiler_params=_params(("parallel", "parallel")),
        name="attn_a_lat",
    )(sink, q, kc, vc, k, v)


def _mla_attend(q, key_parts, val_parts):
    vals = val_parts[0] if len(val_parts) == 1 else jnp.concatenate(val_parts, axis=0)
    lane = lax.broadcasted_iota(jnp.int32, (1, MLA_HEADS * MLA_V), 1)
    zero = jnp.zeros_like(vals)
    vst = jnp.concatenate([jnp.where((lane >= MLA_V * h) & (lane < MLA_V * (h + 1)), vals, zero)
                           for h in range(MLA_HEADS)], axis=0)
    ps = []
    inv = jnp.zeros((q.shape[0], MLA_HEADS * MLA_V), F32)
    for h in range(MLA_HEADS):
        qh = q[:, LANES * h:LANES * (h + 1)]
        ss = [lax.dot_general(qh, kp[:, LANES * h:LANES * (h + 1)], (((1,), (1,)), ((), ())),
                              preferred_element_type=F32) for kp in key_parts]
        m = functools.reduce(jnp.maximum, [jnp.max(s, axis=-1, keepdims=True) for s in ss])
        es = [jnp.exp2(s - m) for s in ss]
        den = functools.reduce(lambda a, b: a + b, [jnp.sum(e, axis=-1, keepdims=True) for e in es])
        ps += [e.astype(BF16) for e in es]
        inv = jnp.where((lane >= MLA_V * h) & (lane < MLA_V * (h + 1)), 1.0 / den, inv)
    return jnp.dot(jnp.concatenate(ps, axis=1), vst, preferred_element_type=F32) * inv


def _mla_ctx_kernel(q_ref, k_ref, v_ref, o_ref):
    o_ref[...] = _mla_attend(q_ref[...], [k_ref[...]], [v_ref[...]]).astype(o_ref.dtype)


def _mla_ctx(q, k, v):
    B, L, _ = q.shape
    tok = lambda c: pl.BlockSpec((None, L, c), lambda b: (b, 0, 0))
    return pl.pallas_call(
        _mla_ctx_kernel,
        grid=(B,),
        in_specs=[tok(512), tok(512), tok(256)],
        out_specs=tok(256),
        out_shape=jax.ShapeDtypeStruct((B, L, 256), BF16),
        compiler_params=_params(("parallel",)),
        name="mla_ctx",
    )(q, k, v)


def _mla_lat_kernel(q_ref, kc_ref, vc_ref, k_ref, v_ref, o_ref):
    o = _mla_attend(q_ref[...], [kc_ref[...], k_ref[...]], [vc_ref[...], v_ref[...]])
    o_ref[...] = o.astype(o_ref.dtype)


def _mla_lat(q, k, v, kc, vc):
    B, L, _ = q.shape
    P = kc.shape[1]
    TQ = MLA_Q_TILE
    blk = lambda c: pl.BlockSpec((None, TQ, c), lambda b, i: (b, i, 0))
    whole = lambda n, c: pl.BlockSpec((None, n, c), lambda b, i: (b, 0, 0))
    return pl.pallas_call(
        _mla_lat_kernel,
        grid=(B, L // TQ),
        in_specs=[blk(512), whole(P, 512), whole(P, 256), whole(L, 512), whole(L, 256)],
        out_specs=blk(256),
        out_shape=jax.ShapeDtypeStruct((B, L, 256), BF16),
        compiler_params=_params(("parallel", "parallel")),
        name="mla_lat",
    )(q, kc, vc, k, v)


def _reorder(perm, x):
    hi = x.astype(BF16)
    lo = (x - hi.astype(F32)).astype(BF16)
    return jnp.dot(perm, hi, preferred_element_type=F32) + jnp.dot(perm, lo, preferred_element_type=F32)


def _scan_perms():
    rows = SCAN_STEPS * SUBLANES
    r = np.arange(rows)
    to_tb = np.zeros((rows, rows), np.float32)
    to_tb[r, (r % SUBLANES) * SCAN_STEPS + r // SUBLANES] = 1.0
    halo = np.zeros((4 * SUBLANES, 2 * SUBLANES * SUBLANES), np.float32)
    b = np.arange(SUBLANES)
    for k, step in enumerate((SUBLANES - 2, SUBLANES - 1)):
        halo[k * SUBLANES + b, b * SUBLANES + step] = 1.0
    halo[2 * SUBLANES + b, SUBLANES * SUBLANES + b * SUBLANES] = 1.0
    return jnp.asarray(to_tb, BF16), jnp.asarray(to_tb.T, BF16), jnp.asarray(halo, BF16)


def _s5_kernel(uf_ref, ub_ref, tb_ref, nat_ref, bf_ref, bb_ref, a_ref, h0_ref, cf_ref, cb_ref,
               yf_ref, yb_ref, fin_ref, s_ref, st_ref):
    i = pl.program_id(1)
    steps = SCAN_STEPS
    rows = steps * SUBLANES

    @pl.when(i == 0)
    def _():
        st_ref[...] = h0_ref[...]

    to_tb = lambda ref: jnp.dot(tb_ref[...], ref[...].reshape(rows, BRANCH_W).astype(BF16),
                                preferred_element_type=F32).astype(BF16)
    uf = to_tb(uf_ref)
    ub = to_tb(ub_ref)
    s_ref[0] = jnp.dot(uf, bf_ref[:, :S5_WIDTH], preferred_element_type=F32)
    s_ref[1] = jnp.dot(uf, bf_ref[:, S5_WIDTH:], preferred_element_type=F32)
    s_ref[2] = jnp.dot(ub, bb_ref[:, :S5_WIDTH], preferred_element_type=F32)
    s_ref[3] = jnp.dot(ub, bb_ref[:, S5_WIDTH:], preferred_element_type=F32)

    lane_chunk = 2 * LANES
    for c in range(S5_WIDTH // lane_chunk):
        sl = slice(lane_chunk * c, lane_chunk * (c + 1))
        arf, aif, arb, aib = (a_ref[n, :, sl] for n in range(4))

        def body(j, carry):
            hrf, hif, hrb, hib = carry
            rf = pl.multiple_of(j * SUBLANES, SUBLANES)
            rb = pl.multiple_of((steps - 1 - j) * SUBLANES, SUBLANES)
            nrf = arf * hrf - aif * hif + s_ref[0, pl.ds(rf, SUBLANES), sl]
            nif = arf * hif + aif * hrf + s_ref[1, pl.ds(rf, SUBLANES), sl]
            nrb = arb * hrb - aib * hib + s_ref[2, pl.ds(rb, SUBLANES), sl]
            nib = arb * hib + aib * hrb + s_ref[3, pl.ds(rb, SUBLANES), sl]
            s_ref[0, pl.ds(rf, SUBLANES), sl] = nrf
            s_ref[1, pl.ds(rf, SUBLANES), sl] = nif
            s_ref[2, pl.ds(rb, SUBLANES), sl] = nrb
            s_ref[3, pl.ds(rb, SUBLANES), sl] = nib
            return nrf, nif, nrb, nib

        fin = lax.fori_loop(0, steps, body, tuple(st_ref[n, :, sl] for n in range(4)), unroll=4)
        for n in range(4):
            st_ref[n, :, sl] = fin[n]

    to_nat = lambda y: _reorder(nat_ref[...], y).reshape(SUBLANES, steps, BRANCH_W)
    hf = jnp.concatenate([s_ref[0].astype(BF16), s_ref[1].astype(BF16)], axis=1)
    yf_ref[...] = to_nat(jnp.dot(hf, cf_ref[...], preferred_element_type=F32))
    hb = jnp.concatenate([s_ref[2].astype(BF16), s_ref[3].astype(BF16)], axis=1)
    yb_ref[...] = to_nat(jnp.dot(hb, cb_ref[...], preferred_element_type=F32))
    fin_ref[...] = st_ref[...]


def _s5_scan(u, lw, h0):
    B, L, _ = u.shape
    Bg = B // SUBLANES
    R = SCAN_STEPS * SUBLANES
    n = L // SCAN_STEPS
    to_tb, to_nat, _ = _scan_perms()
    fwd = pl.BlockSpec((SUBLANES, SCAN_STEPS, 256), lambda g, i: (g, i, 0))
    bwd = pl.BlockSpec((SUBLANES, SCAN_STEPS, 256), lambda g, i: (g, n - 1 - i, 0))
    st = pl.BlockSpec((None, 4, SUBLANES, S5_WIDTH), lambda g, i: (g, 0, 0, 0))
    return pl.pallas_call(
        _s5_kernel,
        grid=(Bg, n),
        in_specs=[fwd, bwd, _full((R, R)), _full((R, R)), _full((256, 2 * S5_WIDTH)), _full((256, 2 * S5_WIDTH)),
                  _full((4, SUBLANES, S5_WIDTH)), st, _full((2 * S5_WIDTH, 256)), _full((2 * S5_WIDTH, 256))],
        out_specs=[fwd, bwd, st],
        out_shape=[jax.ShapeDtypeStruct((B, L, 256), F32), jax.ShapeDtypeStruct((B, L, 256), F32),
                   jax.ShapeDtypeStruct((Bg, 4, SUBLANES, S5_WIDTH), F32)],
        scratch_shapes=[pltpu.VMEM((4, R, S5_WIDTH), F32), pltpu.VMEM((4, SUBLANES, S5_WIDTH), F32)],
        compiler_params=_params(("parallel", "arbitrary")),
        name="s5_scan",
    )(u, u, to_tb, to_nat, lw["s5_bf"], lw["s5_bb"], lw["s5_a"], h0, lw["s5_cf"], lw["s5_cb"])


def _lru_gates(x_ref, pre_ref, post_ref, has_pre, has_post, tb_ref, halo_ref, cw_ref, cb_ref, w_ref, b_ref,
               sp_ref, a_ref, h_ref):
    R = SCAN_STEPS * SUBLANES
    x = _reorder(tb_ref[...], x_ref[...].reshape(R, LRU_WIDTH))
    edge = jnp.concatenate([pre_ref[...].reshape(SUBLANES * SUBLANES, LRU_WIDTH),
                            post_ref[...].reshape(SUBLANES * SUBLANES, LRU_WIDTH)], axis=0)
    edge = _reorder(halo_ref[...], edge)
    pre = jnp.where(has_pre, edge[:2 * SUBLANES], 0.0)
    post = jnp.where(has_post, edge[2 * SUBLANES:3 * SUBLANES], 0.0)
    xp = jnp.concatenate([pre, x, post], axis=0)
    xc = cb_ref[...]
    for t in range(LRU_CONV):
        xc = xc + xp[SUBLANES * t:SUBLANES * t + R] * cw_ref[t:t + 1, :]
    g = jnp.dot(xc.astype(BF16), w_ref[...], preferred_element_type=F32) + b_ref[...]
    r = jax.nn.sigmoid(g[:, :LRU_WIDTH])
    ig = jax.nn.sigmoid(g[:, LRU_WIDTH:])
    log_a = (-LRU_C) * r * sp_ref[...]
    a = jnp.exp(log_a)
    a_ref[...] = a
    h_ref[...] = jnp.sqrt(1.0 - a * a) * (ig * xc)


def _lru_kernel(xf_ref, xfp_ref, xfn_ref, xb_ref, xbp_ref, xbn_ref, tb_ref, nat_ref, halo_ref, cw_ref, cb_ref,
                wf_ref, wb_ref, bf_ref, bb_ref, spf_ref, spb_ref, h0_ref, of_ref, ob_ref, fin_ref,
                af_ref, ab_ref, hf_ref, hb_ref, st_ref):
    i = pl.program_id(1)
    n = pl.num_programs(1)
    steps = SCAN_STEPS

    @pl.when(i == 0)
    def _():
        st_ref[...] = h0_ref[...]

    _lru_gates(xf_ref, xfp_ref, xfn_ref, i > 0, i < n - 1, tb_ref, halo_ref, cw_ref, cb_ref, wf_ref, bf_ref,
               spf_ref, af_ref, hf_ref)
    _lru_gates(xb_ref, xbp_ref, xbn_ref, i < n - 1, i > 0, tb_ref, halo_ref, cw_ref, cb_ref, wb_ref, bb_ref,
               spb_ref, ab_ref, hb_ref)

    def body(j, carry):
        hf, hb = carry
        rf = pl.multiple_of(j * SUBLANES, SUBLANES)
        rb = pl.multiple_of((steps - 1 - j) * SUBLANES, SUBLANES)
        nf = af_ref[pl.ds(rf, SUBLANES), :] * hf + hf_ref[pl.ds(rf, SUBLANES), :]
        nb = ab_ref[pl.ds(rb, SUBLANES), :] * hb + hb_ref[pl.ds(rb, SUBLANES), :]
        hf_ref[pl.ds(rf, SUBLANES), :] = nf
        hb_ref[pl.ds(rb, SUBLANES), :] = nb
        return nf, nb

    ff, fb = lax.fori_loop(0, steps, body, (st_ref[0], st_ref[1]), unroll=8)
    st_ref[0] = ff
    st_ref[1] = fb
    fin_ref[...] = st_ref[...]
    of_ref[...] = _reorder(nat_ref[...], hf_ref[...]).reshape(SUBLANES, steps, LRU_WIDTH)
    ob_ref[...] = _reorder(nat_ref[...], hb_ref[...]).reshape(SUBLANES, steps, LRU_WIDTH)


def _lru_scan(x, lw, h0):
    B, L, _ = x.shape
    Bg = B // SUBLANES
    R = SCAN_STEPS * SUBLANES
    n = L // SCAN_STEPS
    per_chunk = SCAN_STEPS // SUBLANES
    to_tb, to_nat, halo = _scan_perms()
    fwd = lambda g, i: (g, i, 0)
    bwd = lambda g, i: (g, n - 1 - i, 0)
    blk = lambda m: pl.BlockSpec((SUBLANES, SCAN_STEPS, 256), m)
    edge = (SUBLANES, SUBLANES, 256)
    pre = lambda m: pl.BlockSpec(edge, lambda g, i: (g, jnp.maximum(m(g, i)[1] * per_chunk - 1, 0), 0))
    post = lambda m: pl.BlockSpec(
        edge, lambda g, i: (g, jnp.minimum((m(g, i)[1] + 1) * per_chunk, L // SUBLANES - 1), 0))
    st = pl.BlockSpec((None, 2, SUBLANES, 256), lambda g, i: (g, 0, 0, 0))
    return pl.pallas_call(
        _lru_kernel,
        grid=(Bg, n),
        in_specs=[blk(fwd), pre(fwd), post(fwd), blk(bwd), pre(bwd), post(bwd),
                  _full((R, R)), _full((R, R)), _full(halo.shape),
                  _full((LRU_CONV, 256)), _full((1, 256)), _full((256, 512)), _full((256, 512)),
                  _full((1, 512)), _full((1, 512)), _full((1, 256)), _full((1, 256)), st],
        out_specs=[blk(fwd), blk(bwd), st],
        out_shape=[jax.ShapeDtypeStruct((B, L, 256), F32), jax.ShapeDtypeStruct((B, L, 256), F32),
                   jax.ShapeDtypeStruct((Bg, 2, SUBLANES, 256), F32)],
        scratch_shapes=[pltpu.VMEM((R, 256), F32), pltpu.VMEM((R, 256), F32), pltpu.VMEM((R, 256), F32),
                        pltpu.VMEM((R, 256), F32), pltpu.VMEM((2, SUBLANES, 256), F32)],
        compiler_params=_params(("parallel", "arbitrary")),
        name="lru_scan",
    )(x, x, x, x, x, x, to_tb, to_nat, halo, lw["lru_cw"], lw["lru_cb"], lw["lru_wf"], lw["lru_wb"],
      lw["lru_bf"], lw["lru_bb"], lw["lru_spf"], lw["lru_spb"], h0)


def _merge_kernel(x_ref, sc_ref, sh_ref, gt_ref, g1_ref, wg_ref, oa_ref, u_ref, yf_ref, yb_ref, d_ref,
                  wglu_ref, hf_ref, hb_ref, lg_ref, od_ref, wb_ref, wo_ref, o_ref):
    x = x_ref[...]
    h = _modnorm(x, g1_ref[...], sc_ref[...], sh_ref[...]).astype(BF16)

    yb5 = jax.nn.gelu(d_ref[...] * u_ref[...] + yf_ref[...] + yb_ref[...])
    gv = jnp.dot(yb5.astype(BF16), wglu_ref[...], preferred_element_type=F32)
    o_b = gv[:, :BRANCH_W] * jax.nn.sigmoid(gv[:, BRANCH_W:])
    o_c = (hf_ref[...] + hb_ref[...]) * jax.nn.gelu(lg_ref[...])
    branches = (oa_ref[...], o_b.astype(BF16), o_c.astype(BF16), od_ref[...])

    acc = jnp.zeros(x.shape, F32)
    for n in range(N_BRANCH):
        gate = jnp.dot(h, wg_ref[:, D_MODEL * n:D_MODEL * (n + 1)], preferred_element_type=F32)
        proj = jnp.dot(branches[n], wb_ref[n], preferred_element_type=F32)
        acc = acc + jax.nn.sigmoid(gate) * proj
    out = jnp.dot(acc.astype(BF16), wo_ref[...], preferred_element_type=F32)
    o_ref[...] = x + gt_ref[...] * out


def _merge(x, sc, sh, gt, lw, o_a, s5u_t, yf_t, yb_t, hf_t, hb_t, lg, o_d):
    B, L, _ = x.shape
    TL = TOK_TILE
    per_b = sc.shape[0] > 1

    def mod_map(b, i):
        return (b if per_b else 0, 0, 0)

    tok = lambda c: pl.BlockSpec((None, TL, c), lambda b, i: (b, i, 0))
    mod = pl.BlockSpec((None, 1, D_MODEL), mod_map)
    return pl.pallas_call(
        _merge_kernel,
        grid=(B, L // TL),
        in_specs=[tok(D_MODEL), mod, mod, mod, _full((1, D_MODEL)), _full((D_MODEL, N_BRANCH * D_MODEL)),
                  tok(256), tok(256), tok(256), tok(256), _full((1, 256)), _full((256, 512)),
                  tok(256), tok(256), tok(256), tok(256), _full((N_BRANCH, BRANCH_W, D_MODEL)),
                  _full((D_MODEL, D_MODEL))],
        out_specs=tok(D_MODEL),
        out_shape=jax.ShapeDtypeStruct((B, L, D_MODEL), F32),
        compiler_params=_params(("parallel", "parallel")),
        name="merge",
    )(x, sc, sh, gt, lw["g1"], lw["wg"], o_a, s5u_t, yf_t, yb_t, lw["s5_d"], lw["wglu"],
      hf_t, hb_t, lg, o_d, lw["wb"], lw["wo"])


def _first_index(values, target):
    idx = jnp.full_like(target, float(len(values) - 1))
    for n in range(len(values) - 2, -1, -1):
        idx = jnp.where(values[n] == target, float(n), idx)
    return idx


def _list_max(values):
    return functools.reduce(jnp.maximum, values)


def _moe_sort_kernel(x_ref, sc_ref, sh_ref, g2_ref, wrh_ref, wrl_ref, br_ref, tri_ref,
                     hs_ref, gs_ref, dcol_ref, meta_ref):
    for s in range(x_ref.shape[0] // MOE_SORT_TILE):
        tok = pl.ds(s * MOE_SORT_TILE, MOE_SORT_TILE)
        srt = pl.ds(s * MOE_SORTED_ROWS, MOE_SORTED_ROWS)
        _moe_sort_tile(x_ref.at[tok], sc_ref, sh_ref, g2_ref, wrh_ref, wrl_ref, br_ref, tri_ref,
                       hs_ref.at[srt], gs_ref.at[srt], dcol_ref.at[tok], meta_ref.at[s])


def _moe_sort_tile(x_ref, sc_ref, sh_ref, g2_ref, wrh_ref, wrl_ref, br_ref, tri_ref,
                   hs_ref, gs_ref, dcol_ref, meta_ref):
    T = MOE_SORT_TILE
    h = _modnorm(x_ref[...], g2_ref[...], sc_ref[...], sh_ref[...])
    hh = h.astype(BF16)
    hl = (h - hh.astype(F32)).astype(BF16)
    nt = (((1,), (1,)), ((), ()))
    lt = (lax.dot_general(wrh_ref[...], hh, nt, preferred_element_type=F32)
          + lax.dot_general(wrh_ref[...], hl, nt, preferred_element_type=F32)
          + lax.dot_general(wrl_ref[...], hh, nt, preferred_element_type=F32)) + br_ref[...]
    gl = [lt[g:g + 1, :] for g in range(N_GROUPS)]
    gmax = _list_max(gl)
    g_idx = _first_index(gl, gmax)
    pg = 1.0 / sum(jnp.exp(v - gmax) for v in gl)
    hot = [g_idx == float(g) for g in range(N_GROUPS)]
    el = []
    for e in range(EXPERTS_PER_GROUP):
        v = jnp.zeros_like(gmax)
        for g in range(N_GROUPS):
            r = N_GROUPS + EXPERTS_PER_GROUP * g + e
            v = jnp.where(hot[g], lt[r:r + 1, :], v)
        el.append(v)
    emax = _list_max(el)
    ee = [jnp.exp(v - emax) for v in el]
    esum = sum(ee)
    pe = [v / esum for v in ee]
    v1 = _list_max(pe)
    i1 = _first_index(pe, v1)
    pe2 = [jnp.where(i1 == float(e), -1.0, pe[e]) for e in range(EXPERTS_PER_GROUP)]
    v2 = _list_max(pe2)
    i2 = _first_index(pe2, v2)
    tot = v1 + v2
    w = [jnp.where(i1 == float(e), pg * v1 / tot, jnp.where(i2 == float(e), pg * v2 / tot, 0.0))
         for e in range(EXPERTS_PER_GROUP)]

    zero_row = jnp.zeros_like(gmax)
    g8 = jnp.concatenate([jnp.where(hot[g], 1.0, 0.0) for g in range(N_GROUPS)] + [zero_row] * 4, axis=0)
    cum = jnp.dot(g8.astype(BF16), tri_ref[...], preferred_element_type=F32)
    off = jnp.zeros((1, 1), F32)
    dest = zero_row
    counts = []
    for g in range(N_GROUPS):
        cnt = cum[g:g + 1, T - 1:T]
        padded = jnp.floor((cnt + (MOE_ROW_ALIGN - 1.0)) * (1.0 / MOE_ROW_ALIGN)) * MOE_ROW_ALIGN
        dest = jnp.where(hot[g], off + cum[g:g + 1, :] - 1.0, dest)
        off = off + padded
        counts.append(padded)
    rows = lax.broadcasted_iota(jnp.int32, (MOE_SORTED_ROWS, T), 0).astype(F32)
    perm = jnp.where(rows == dest, 1.0, 0.0).astype(BF16)
    hs_ref[...] = jnp.dot(perm, hh, preferred_element_type=F32).astype(BF16)
    gates = jnp.concatenate(w + [jnp.zeros((LANES - EXPERTS_PER_GROUP, T), F32)], axis=0)
    ghi = gates.astype(BF16)
    glo = (gates - ghi.astype(F32)).astype(BF16)
    gs_ref[...] = (lax.dot_general(perm, ghi, nt, preferred_element_type=F32)
                   + lax.dot_general(perm, glo, nt, preferred_element_type=F32))
    r_i = lax.broadcasted_iota(jnp.int32, (T, T), 0)
    c_i = lax.broadcasted_iota(jnp.int32, (T, T), 1)
    dcol = jnp.sum(jnp.where(r_i == c_i, jnp.broadcast_to(dest, (T, T)), 0.0), axis=1, keepdims=True)
    dcol_ref[...] = jnp.broadcast_to(dcol, (T, LANES))
    meta_ref[...] = jnp.concatenate([jnp.broadcast_to(c, (1, LANES)) for c in counts]
                                    + [jnp.zeros((SUBLANES - N_GROUPS, LANES), F32)], axis=0)


def _moe_sort(xf, sc, sh, lw, tiles_per_mod):
    N = xf.shape[0]
    T = MOE_SORT_TILE
    n_tiles = N // T
    sub = MOE_SORT_SUBTILES
    per_b = sc.shape[0] > 1
    mod = pl.BlockSpec((None, 1, D_MODEL), lambda t: (t * sub // tiles_per_mod if per_b else 0, 0, 0))
    sds = jax.ShapeDtypeStruct
    return pl.pallas_call(
        _moe_sort_kernel,
        grid=(n_tiles // sub,),
        in_specs=[pl.BlockSpec((sub * T, D_MODEL), lambda t: (t, 0)), mod, mod, _full((1, D_MODEL)),
                  _full((LANES, D_MODEL)), _full((LANES, D_MODEL)), _full((LANES, 1)), _full((T, T))],
        out_specs=[pl.BlockSpec((sub * MOE_SORTED_ROWS, D_MODEL), lambda t: (t, 0)),
                   pl.BlockSpec((sub * MOE_SORTED_ROWS, LANES), lambda t: (t, 0)),
                   pl.BlockSpec((sub * T, LANES), lambda t: (t, 0)),
                   pl.BlockSpec((sub, SUBLANES, LANES), lambda t: (t, 0, 0))],
        out_shape=[sds((n_tiles * MOE_SORTED_ROWS, D_MODEL), BF16), sds((n_tiles * MOE_SORTED_ROWS, LANES), F32),
                   sds((N, LANES), F32), sds((n_tiles, SUBLANES, LANES), F32)],
        compiler_params=_params(("parallel",)),
        name="moe_sort",
    )(xf, sc, sh, lw["g2"], lw["wrt_hi"], lw["wrt_lo"], lw["brt"], lw["tri"])


def _moe_expert_kernel(cnt_ref, hs_ref, gs_ref, wga_ref, wup_ref, wdn_ref, ys_ref, ch_ref, cg_ref, cy_ref):
    j = pl.program_id(0)
    g = pl.program_id(1)
    piece = MOE_ROW_ALIGN

    @pl.when((j == 0) & (g == 0))
    def _():
        ch_ref[...] = jnp.zeros_like(ch_ref)
        cg_ref[...] = jnp.zeros_like(cg_ref)

    @pl.when(g == 0)
    def _():
        ys_ref[...] = jnp.zeros_like(ys_ref)

    def for_each_piece(move):
        packed = jnp.int32(0)
        for t in range(MOE_BLOCK_TILES):
            base = (j * MOE_BLOCK_TILES + t) * N_GROUPS
            start = jnp.int32(t * MOE_SORTED_ROWS)
            for g2 in range(N_GROUPS - 1):
                start = start + jnp.where(g2 < g, cnt_ref[base + g2], 0)
            n = cnt_ref[base + g]

            def body(k, carry, start=start, packed=packed):
                move(pl.multiple_of(start + k * piece, piece), pl.multiple_of(packed + k * piece, piece))
                return carry

            lax.fori_loop(0, lax.shift_right_logical(n, 4), body, 0)
            packed = packed + n
        return packed

    def pack(src, dst):
        ch_ref[pl.ds(dst, piece), :] = hs_ref[pl.ds(src, piece), :]
        cg_ref[pl.ds(dst, piece), :] = gs_ref[pl.ds(src, piece), :]

    rows = for_each_piece(pack)

    def chunk(c, carry):
        r = pl.multiple_of(c * MOE_CHUNK, MOE_CHUNK)
        h = ch_ref[pl.ds(r, MOE_CHUNK), :]
        acc = jnp.zeros((MOE_CHUNK, D_MODEL), F32)
        for e in range(EXPERTS_PER_GROUP):
            a = jnp.dot(h, wga_ref[e], preferred_element_type=F32)
            u = jnp.dot(h, wup_ref[e], preferred_element_type=F32)
            act = (a * jax.nn.sigmoid(a)) * u * cg_ref[pl.ds(r, MOE_CHUNK), e:e + 1]
            acc = acc + jnp.dot(act.astype(BF16), wdn_ref[e], preferred_element_type=F32)
        cy_ref[pl.ds(r, MOE_CHUNK), :] = acc.astype(BF16)
        return carry

    lax.fori_loop(0, lax.shift_right_logical(rows + (MOE_CHUNK - 1), MOE_CHUNK.bit_length() - 1), chunk, 0)

    def unpack(src, dst):
        ys_ref[pl.ds(src, piece), :] = cy_ref[pl.ds(dst, piece), :]

    for_each_piece(unpack)


def _moe_experts(cnt, hs, gs, lw):
    rows = MOE_BLOCK_TILES * MOE_SORTED_ROWS
    cap = MOE_BLOCK_TILES * MOE_SORT_TILE
    blk = lambda c: pl.BlockSpec((rows, c), lambda j, g, cnt: (j, 0))
    wspec = lambda r, c: pl.BlockSpec((EXPERTS_PER_GROUP, r, c), lambda j, g, cnt: (g, 0, 0))
    return pl.pallas_call(
        _moe_expert_kernel,
        grid_spec=pltpu.PrefetchScalarGridSpec(
            num_scalar_prefetch=1, grid=(hs.shape[0] // rows, N_GROUPS),
            in_specs=[blk(D_MODEL), blk(LANES), wspec(D_MODEL, EXPERT_FF), wspec(D_MODEL, EXPERT_FF),
                      wspec(EXPERT_FF, D_MODEL)],
            out_specs=blk(D_MODEL),
            scratch_shapes=[pltpu.VMEM((cap, D_MODEL), BF16), pltpu.VMEM((cap, LANES), F32),
                            pltpu.VMEM((cap, D_MODEL), BF16)]),
        out_shape=jax.ShapeDtypeStruct(hs.shape, BF16),
        compiler_params=_params(("arbitrary", "arbitrary")),
        name="moe_experts",
    )(cnt, hs, gs, lw["wga"], lw["wup"], lw["wdn"])


def _moe_unsort_kernel(x_ref, gt_ref, ys_ref, dcol_ref, o_ref):
    cols = lax.broadcasted_iota(jnp.int32, (MOE_SORT_TILE, MOE_SORTED_ROWS), 1).astype(F32)
    perm_t = jnp.where(cols == dcol_ref[:, 0:1], 1.0, 0.0).astype(BF16)
    y = jnp.dot(perm_t, ys_ref[...], preferred_element_type=F32)
    o_ref[...] = x_ref[...] + gt_ref[...] * y


def _moe_unsort(xf, gt, ys, dcol, tiles_per_mod):
    N = xf.shape[0]
    T = MOE_SORT_TILE
    per_b = gt.shape[0] > 1
    mod = pl.BlockSpec((None, 1, D_MODEL), lambda t: (t // tiles_per_mod if per_b else 0, 0, 0))
    tok = pl.BlockSpec((T, D_MODEL), lambda t: (t, 0))
    return pl.pallas_call(
        _moe_unsort_kernel,
        grid=(N // T,),
        in_specs=[tok, mod, pl.BlockSpec((MOE_SORTED_ROWS, D_MODEL), lambda t: (t, 0)),
                  pl.BlockSpec((T, LANES), lambda t: (t, 0))],
        out_specs=tok,
        out_shape=jax.ShapeDtypeStruct((N, D_MODEL), F32),
        compiler_params=_params(("parallel",)),
        name="moe_unsort",
    )(xf, gt, ys, dcol)


def _moe(x, sc, sh, gt, lw):
    B, L, _ = x.shape
    N = B * L
    T = MOE_SORT_TILE
    xf = x.reshape(N, D_MODEL)
    hs, gs, dcol, meta = _moe_sort(xf, sc, sh, lw, L // T)

    cnt = meta[:, :N_GROUPS, 0].astype(jnp.int32).reshape(-1)
    ys = _moe_experts(cnt, hs, gs, lw)
    out = _moe_unsort(xf, gt, ys, dcol, L // T)
    return out.reshape(B, L, D_MODEL)


def _block_diag(blocks):
    n, r, c = blocks.shape
    eye = jnp.eye(n, dtype=blocks.dtype)
    return jnp.einsum("nrc,nm->nrmc", blocks, eye).reshape(n * r, n * c)


def _rope_tables(seq_len, dim, width, offset):
    rows = seq_len // GRID_W
    r, col = jnp.meshgrid(jnp.arange(rows), jnp.arange(GRID_W), indexing="ij")
    r = r.reshape(-1).astype(F32)
    col = col.reshape(-1).astype(F32)
    quarter = dim // 4
    freqs = ROPE_BASE ** (-jnp.arange(quarter, dtype=F32) / quarter)
    ang_r = r[:, None] * freqs
    ang_c = col[:, None] * freqs
    zero = jnp.zeros_like(ang_r)
    cos = jnp.cos(jnp.concatenate([ang_r, ang_r, ang_c, ang_c], axis=-1))
    sin_next = jnp.concatenate([-jnp.sin(ang_r), zero, -jnp.sin(ang_c), zero], axis=-1)
    sin_prev = jnp.concatenate([zero, jnp.sin(ang_r), zero, jnp.sin(ang_c)], axis=-1)

    def place(t, fill):
        return jnp.pad(t, ((0, 0), (offset, width - offset - dim)), constant_values=fill)

    return place(cos, 1.0), place(sin_next, 0.0), place(sin_prev, 0.0)


def _layer_weights(l, w):
    lw = {}
    w_in = w["w_in"][l]
    q_cols = w_in[:, 0:256].reshape(D_MODEL, A_HEADS, A_HEAD_DIM)[:, HEAD_ORDER].reshape(D_MODEL, 256)
    kr_cols = jnp.pad(w_in[:, 1664:1696], ((0, 0), (MLA_NOPE, LANES - MLA_QK)))
    lw["wz"] = jnp.concatenate([q_cols, w_in[:, 256:1664], kr_cols], axis=1).astype(BF16)
    lw["wg"] = w_in[:, 1696:].astype(BF16)
    lw["g1"] = w["norm1_g"][l][None]
    lw["g2"] = w["norm2_g"][l][None]
    lw["gq"] = jnp.tile(w["a_qnorm_g"][l], A_HEADS)[None] * (A_HEAD_DIM ** -0.5)
    lw["gk"] = jnp.tile(w["a_knorm_g"][l], A_KV_HEADS)[None]
    lw["sink"] = w["a_sink"][l]
    lw["qlg"] = w["mla_q_lat_norm"][l][None]
    wuq = w["mla_w_uq"][l].reshape(MLA_Q_LORA, MLA_HEADS, MLA_QK)
    lw["wuq"] = jnp.pad(wuq, ((0, 0), (0, 0), (0, LANES - MLA_QK))).reshape(MLA_Q_LORA, MLA_HEADS * LANES).astype(BF16)
    pad_g = lambda g: jnp.tile(jnp.pad(g, (0, LANES - MLA_QK)), MLA_HEADS)[None]
    lw["gmq"] = pad_g(w["mla_qnorm_g"][l]) * (MLA_QK ** -0.5 * math.log2(math.e))
    lw["gmk"] = pad_g(w["mla_knorm_g"][l])
    lw["kvg"] = w["mla_kv_norm"][l][None]
    wukv = w["mla_w_ukv"][l].reshape(MLA_KV_LORA, MLA_HEADS, MLA_NOPE + MLA_V)
    wk = jnp.pad(wukv[:, :, :MLA_NOPE], ((0, 0), (0, 0), (0, LANES - MLA_NOPE))).reshape(MLA_KV_LORA, MLA_HEADS * LANES)
    wv = wukv[:, :, MLA_NOPE:].reshape(MLA_KV_LORA, MLA_HEADS * MLA_V)
    lw["wukv"] = jnp.concatenate([wk, wv], axis=1).astype(BF16)

    lre = w["s5_lambda_re"][l]
    lim = w["s5_lambda_im"][l]
    dt = jnp.exp(w["s5_log_step"][l])[:, :, None]
    mag = jnp.exp(lre * dt)
    ar, ai = mag * jnp.cos(lim * dt), mag * jnp.sin(lim * dt)
    den = lre * lre + lim * lim
    fr = ((ar - 1.0) * lre + ai * lim) / den
    fi = (ai * lre - (ar - 1.0) * lim) / den
    br, bi = w["s5_b_re"][l], w["s5_b_im"][l]
    bbr = fr[..., None] * br - fi[..., None] * bi
    bbi = fr[..., None] * bi + fi[..., None] * br
    in_map = lambda d: jnp.concatenate(
        [_block_diag(jnp.swapaxes(bbr[d], 1, 2)), _block_diag(jnp.swapaxes(bbi[d], 1, 2))], axis=1).astype(BF16)
    out_map = lambda d: jnp.concatenate(
        [_block_diag(jnp.swapaxes(w["s5_c_re"][l][d], 1, 2)),
         -_block_diag(jnp.swapaxes(w["s5_c_im"][l][d], 1, 2))], axis=0).astype(BF16)
    lw["s5_bf"], lw["s5_bb"] = in_map(0), in_map(1)
    lw["s5_cf"], lw["s5_cb"] = out_map(0), out_map(1)
    coef = jnp.stack([ar[0].reshape(-1), ai[0].reshape(-1), ar[1].reshape(-1), ai[1].reshape(-1)])
    lw["s5_a"] = jnp.broadcast_to(coef[:, None, :], (4, SUBLANES, S5_WIDTH))
    lw["s5_d"] = w["s5_d"][l][None]
    lw["wglu"] = w["s5_w_glu"][l].astype(BF16)

    lw["lru_cw"] = w["lru_conv_w"][l]
    lw["lru_cb"] = w["lru_conv_b"][l][None]
    gate_w = lambda d: jnp.concatenate(
        [_block_diag(w["lru_w_a"][l][d]), _block_diag(w["lru_w_x"][l][d])], axis=1).astype(BF16)
    gate_b = lambda d: jnp.concatenate([w["lru_b_a"][l][d], w["lru_b_x"][l][d]])[None]
    lw["lru_wf"], lw["lru_wb"] = gate_w(0), gate_w(1)
    lw["lru_bf"], lw["lru_bb"] = gate_b(0), gate_b(1)
    sp = jax.nn.softplus(-w["lru_lambda"][l])
    lw["lru_spf"], lw["lru_spb"] = sp[0][None], sp[1][None]

    wb = w["w_branch"][l]
    wb0 = wb[0].reshape(A_HEADS, A_HEAD_DIM, D_MODEL)[HEAD_ORDER].reshape(BRANCH_W, D_MODEL)
    lw["wb"] = jnp.concatenate([wb0[None], wb[1:]], axis=0).astype(BF16)
    lw["wo"] = w["w_out"][l].astype(BF16)

    wr = jnp.pad(jnp.concatenate([w["moe_w_group"][l], w["moe_w_expert"][l]], axis=1),
                 ((0, 0), (0, LANES - N_GROUPS - N_EXPERTS)))
    lw["wrt_hi"] = wr.T.astype(BF16)
    lw["wrt_lo"] = (wr.T - lw["wrt_hi"].astype(F32)).astype(BF16)
    lw["brt"] = jnp.pad(jnp.concatenate([w["moe_b_group"][l], w["moe_b_expert"][l]]),
                        (0, LANES - N_GROUPS - N_EXPERTS))[:, None]
    lw["tri"] = jnp.triu(jnp.ones((MOE_SORT_TILE, MOE_SORT_TILE), BF16))
    lw["wga"] = w["moe_w_gate"][l].astype(BF16)
    lw["wup"] = w["moe_w_up"][l].astype(BF16)
    lw["wdn"] = w["moe_w_down"][l].astype(BF16)
    return lw


def _rows_to_state(fin, width):
    Bg, n = fin.shape[0], fin.shape[1]
    return jnp.swapaxes(fin, 1, 2).reshape(Bg * SUBLANES, n, width)


def _state_to_rows(state):
    B, n, width = state.shape
    return jnp.swapaxes(state.reshape(B // SUBLANES, SUBLANES, n, width), 1, 2)


def _mix(x, mods, lw, rope_tabs, cache, s5_h0, lru_h0, ctx):
    sh1, sc1, gt1 = mods
    B, L, _ = x.shape
    fold = max(1, TOK_TILE // L) if sc1.shape[0] == 1 else 1
    folded = lambda t: t.reshape(B // fold, L * fold, t.shape[-1])
    outs = [o.reshape(B, L, o.shape[-1]) for o in _in_proj(folded(x), sc1, sh1, lw, rope_tabs, ctx)]
    q, k, v, s5u, lx, lg, qm, km, vm = outs[:9]
    if ctx:
        o_a = _attn_a_ctx(q, k, v, lw["sink"])
        o_d = _mla_ctx(qm, km, vm)
    else:
        ck, cv, cckv, ckr = cache
        o_a = _attn_a_lat(q, k, v, ck, cv, lw["sink"])
        kc, vc = _mla_cache_kv(cckv, ckr, lw)
        o_d = _mla_lat(qm, km, vm, kc, vc)
    yf, yb, s5_fin = _s5_scan(s5u, lw, s5_h0)
    hf, hb, lru_fin = _lru_scan(lx, lw, lru_h0)
    x = _merge(folded(x), sc1, sh1, gt1, lw, *map(folded, (o_a, s5u, yf, yb, hf, hb, lg, o_d))).reshape(x.shape)
    return x, outs[9:], s5_fin, lru_fin


def kernel(x_prompt, x_sample, cache_attn_k, cache_attn_v, cache_mla_ckv, cache_mla_krope, state_ssm_re, state_ssm_im, state_lru, c, c_ctx, w_mod, b_mod, norm1_g, norm2_g, w_in, a_qnorm_g, a_knorm_g, a_sink, s5_lambda_re, s5_lambda_im, s5_log_step, s5_b_re, s5_b_im, s5_c_re, s5_c_im, s5_d, s5_w_glu, lru_conv_w, lru_conv_b, lru_w_a, lru_b_a, lru_w_x, lru_b_x, lru_lambda, mla_q_lat_norm, mla_w_uq, mla_kv_norm, mla_w_ukv, mla_qnorm_g, mla_knorm_g, w_branch, w_out, moe_w_group, moe_b_group, moe_w_expert, moe_b_expert, moe_w_gate, moe_w_up, moe_w_down):
    w = dict(norm1_g=norm1_g, norm2_g=norm2_g, w_in=w_in, a_qnorm_g=a_qnorm_g, a_knorm_g=a_knorm_g,
             a_sink=a_sink, s5_lambda_re=s5_lambda_re, s5_lambda_im=s5_lambda_im, s5_log_step=s5_log_step,
             s5_b_re=s5_b_re, s5_b_im=s5_b_im, s5_c_re=s5_c_re, s5_c_im=s5_c_im, s5_d=s5_d, s5_w_glu=s5_w_glu,
             lru_conv_w=lru_conv_w, lru_conv_b=lru_conv_b, lru_w_a=lru_w_a, lru_b_a=lru_b_a, lru_w_x=lru_w_x,
             lru_b_x=lru_b_x, lru_lambda=lru_lambda, mla_q_lat_norm=mla_q_lat_norm, mla_w_uq=mla_w_uq,
             mla_kv_norm=mla_kv_norm, mla_w_ukv=mla_w_ukv, mla_qnorm_g=mla_qnorm_g, mla_knorm_g=mla_knorm_g,
             w_branch=w_branch, w_out=w_out, moe_w_group=moe_w_group, moe_b_group=moe_b_group,
             moe_w_expert=moe_w_expert, moe_b_expert=moe_b_expert, moe_w_gate=moe_w_gate, moe_w_up=moe_w_up,
             moe_w_down=moe_w_down)
    B, L, _ = x_prompt.shape
    Bd, Ld, _ = x_sample.shape
    P = cache_attn_k.shape[2]

    cond = jnp.zeros((2 * SUBLANES, D_MODEL), F32).at[:Bd].set(c).at[Bd].set(c_ctx)
    mod = _modulation(cond, w_mod.astype(BF16), b_mod[:, None, :])
    mod = mod.reshape(DEPTH, 2 * SUBLANES, 6, D_MODEL)

    rope_a = _rope_tables(Ld, A_HEAD_DIM, A_HEAD_DIM, 0)
    rope_a = tuple(jnp.tile(t, (1, A_HEADS)) for t in rope_a)
    rope_m = _rope_tables(Ld, MLA_ROPE, LANES, MLA_NOPE)

    xp, xs = x_prompt, x_sample
    ak_l, av_l, ckv_l, kr_l, sr_l, si_l, lru_l = [], [], [], [], [], [], []
    for l in range(DEPTH):
        lw = _layer_weights(l, w)
        lat_mod = [mod[l, :Bd, n][:, None, :] for n in range(6)]
        ctx_mod = [mod[l, Bd:Bd + 1, n][:, None, :] for n in range(6)]

        zs5 = jnp.zeros((B // SUBLANES, 4, SUBLANES, S5_WIDTH), F32)
        zlru = jnp.zeros((B // SUBLANES, 2, SUBLANES, LRU_WIDTH), F32)
        xp, (k32, v32, ckv_n, krp), s5_fin, lru_fin = _mix(xp, ctx_mod[0:3], lw, None, None, zs5, zlru, True)
        xp = _moe(xp, ctx_mod[4], ctx_mod[3], ctx_mod[5], lw)
        ak_l.append(k32.reshape(B, L, A_KV_HEADS, A_HEAD_DIM))
        av_l.append(v32.reshape(B, L, A_KV_HEADS, A_HEAD_DIM))
        ckv_l.append(ckv_n)
        kr_l.append(krp[:, :, MLA_NOPE:MLA_QK])
        fin = _rows_to_state(s5_fin, S5_WIDTH)
        sr_l.append(fin[:, 0::2].reshape(B, 2, S5_GROUPS, S5_STATE))
        si_l.append(fin[:, 1::2].reshape(B, 2, S5_GROUPS, S5_STATE))
        lru_l.append(_rows_to_state(lru_fin, LRU_WIDTH))

        sre = state_ssm_re[:, l].reshape(Bd, 2, S5_WIDTH)
        sim = state_ssm_im[:, l].reshape(Bd, 2, S5_WIDTH)
        s5_h0 = _state_to_rows(jnp.stack([sre[:, 0], sim[:, 0], sre[:, 1], sim[:, 1]], axis=1))
        lru_h0 = _state_to_rows(state_lru[:, l])
        cache = (cache_attn_k[:, l].reshape(Bd, P, 128), cache_attn_v[:, l].reshape(Bd, P, 128),
                 cache_mla_ckv[:, l],
                 jnp.pad(cache_mla_krope[:, l], ((0, 0), (0, 0), (MLA_NOPE, LANES - MLA_QK))))
        xs, _, _, _ = _mix(xs, lat_mod[0:3], lw, rope_a + rope_m, cache, s5_h0, lru_h0, False)
        xs = _moe(xs, lat_mod[4], lat_mod[3], lat_mod[5], lw)

    stack = lambda ts: jnp.stack(ts, axis=1)
    return (xp, xs, stack(ak_l), stack(av_l), stack(ckv_l), stack(kr_l), stack(sr_l), stack(si_l), stack(lru_l))
```

```python
import functools
import math

import jax
import jax.numpy as jnp
import numpy as np
from jax import lax
from jax.experimental import pallas as pl
from jax.experimental.pallas import tpu as pltpu

F32 = jnp.float32
BF16 = jnp.bfloat16

D_MODEL = 1024
DEPTH = 2
GRID_W = 64
N_BRANCH = 4
BRANCH_W = 256
ROPE_BASE = 10000.0
EPS = 1e-6
NEG_INF = -1e30
LOG2_E = math.log2(math.e)
A_HEADS = 4
A_KV_HEADS = 2
A_HEAD_DIM = 64
WINDOW = 128
Q_BLOCK = 128
S5_GROUP = 16
S5_GROUPS = 16
S5_STATE = 64
S5_WIDTH = S5_GROUPS * S5_STATE
LRU_WIDTH = 256
LRU_BLOCKS = 4
LRU_CONV = 4
LRU_C = 8.0
MLA_HEADS = 4
MLA_Q_LORA = 256
MLA_KV_LORA = 128
MLA_NOPE = 64
MLA_ROPE = 32
MLA_V = 64
MLA_QK = MLA_NOPE + MLA_ROPE
N_GROUPS = 4
EXPERTS_PER_GROUP = 4
N_EXPERTS = 16
EXPERT_FF = 256

LANES = 128
SUBLANES = 8
VMEM_LIMIT = 56 * 1024 * 1024

TOK_TILE = 256
SCAN_STEPS = 64
MOE_SORT_TILE = 256
MOE_ROW_ALIGN = 16
MOE_SORTED_ROWS = MOE_SORT_TILE + LANES
MOE_SORT_SUBTILES = 4
MOE_BLOCK_TILES = 8
MOE_CHUNK = 256
MLA_Q_TILE = 256
Z_COLS = 1792
HEAD_ORDER = np.array((0, 2, 1, 3))


def _params(sem):
    return pltpu.CompilerParams(dimension_semantics=sem, vmem_limit_bytes=VMEM_LIMIT)


def _layer_spec(layer):
    def spec(shape):
        n = len(shape)
        return pl.BlockSpec((None,) + tuple(shape), lambda *_: (layer,) + (0,) * n)
    return spec


def _modnorm(x, g, sc, sh):
    ms = jnp.mean(x * x, axis=-1, keepdims=True)
    return (x * lax.rsqrt(ms + EPS)) * g * (1.0 + sc) + sh


def _rmsnorm(x, g):
    ms = jnp.mean(x * x, axis=-1, keepdims=True)
    return (x * lax.rsqrt(ms + EPS)) * g


def _half_rmsnorm(x):
    lane = lax.broadcasted_iota(jnp.int32, (1, LANES), 1)
    lo = lane < A_HEAD_DIM
    outs = []
    for t in range(x.shape[1] // LANES):
        xt = x[:, LANES * t:LANES * (t + 1)]
        sq = xt * xt
        s_lo = jnp.sum(jnp.where(lo, sq, 0.0), axis=-1, keepdims=True)
        s_hi = jnp.sum(jnp.where(lo, 0.0, sq), axis=-1, keepdims=True)
        inv = jnp.where(lo, lax.rsqrt(s_lo * (1.0 / A_HEAD_DIM) + EPS),
                        lax.rsqrt(s_hi * (1.0 / A_HEAD_DIM) + EPS))
        outs.append(xt * inv)
    return jnp.concatenate(outs, axis=1)


def _tile_rmsnorm(x, n_real):
    outs = []
    for t in range(x.shape[1] // LANES):
        xt = x[:, LANES * t:LANES * (t + 1)]
        ss = jnp.sum(xt * xt, axis=-1, keepdims=True)
        outs.append(xt * lax.rsqrt(ss * (1.0 / n_real) + EPS))
    return jnp.concatenate(outs, axis=1)


def _rope(x, cos, sin_next, sin_prev, quarter):
    width = x.shape[1]
    return (x * cos + pltpu.roll(x, width - quarter, 1) * sin_next
            + pltpu.roll(x, quarter, 1) * sin_prev)


def _mla_kv(ckv_n, kr_placed, wukv, gk, rope_tabs):
    kv = jnp.dot(ckv_n.astype(BF16), wukv, preferred_element_type=F32)
    kfull = kv[:, :MLA_HEADS * LANES] + jnp.concatenate([kr_placed] * MLA_HEADS, axis=1)
    kn = _tile_rmsnorm(kfull, MLA_QK) * gk
    if rope_tabs is not None:
        kn = _rope(kn, *rope_tabs, MLA_ROPE // 4)
    return kn, kv[:, MLA_HEADS * LANES:]


def _mod_kernel(c_ref, w_ref, b_ref, o_ref):
    c = c_ref[...]
    s = c * jax.nn.sigmoid(c)
    o_ref[...] = jnp.dot(s.astype(BF16), w_ref[...], preferred_element_type=F32) + b_ref[...]


def _modulation(cond, w_mod, b_mod):
    n_rows = cond.shape[0]
    n_out = w_mod.shape[-1]
    tn = 1024
    return pl.pallas_call(
        _mod_kernel,
        grid=(DEPTH, n_out // tn),
        in_specs=[pl.BlockSpec((n_rows, D_MODEL), lambda l, j: (0, 0)),
                  pl.BlockSpec((None, D_MODEL, tn), lambda l, j: (l, 0, j)),
                  pl.BlockSpec((None, 1, tn), lambda l, j: (l, 0, j))],
        out_specs=pl.BlockSpec((None, n_rows, tn), lambda l, j: (l, 0, j)),
        out_shape=jax.ShapeDtypeStruct((DEPTH, n_rows, n_out), F32),
        compiler_params=_params(("arbitrary", "arbitrary")),
        name="modulation",
    )(cond, w_mod, b_mod)


def _in_proj_kernel(rope, ctx, *refs):
    it = iter(refs)
    x_ref, sc_ref, sh_ref, g1_ref, wz_ref = (next(it) for _ in range(5))
    gq_ref, gk_ref, qlg_ref, wuq_ref, gmq_ref, kvg_ref, wukv_ref, gmk_ref = (next(it) for _ in range(8))
    if rope:
        ca_ref, sna_ref, spa_ref, cm_ref, snm_ref, spm_ref = (next(it) for _ in range(6))
    q_ref, k_ref, v_ref, s5u_ref, lx_ref, lg_ref, qm_ref, km_ref, vm_ref = (next(it) for _ in range(9))
    if ctx:
        k32_ref, v32_ref, ckv_ref, krp_ref = (next(it) for _ in range(4))

    h = _modnorm(x_ref[...], g1_ref[...], sc_ref[...], sh_ref[...])
    z = jnp.dot(h.astype(BF16), wz_ref[...], preferred_element_type=F32)

    q = _half_rmsnorm(z[:, 0:256]) * gq_ref[...]
    k = _half_rmsnorm(z[:, 256:384]) * gk_ref[...]
    v = z[:, 384:512]
    if ctx:
        k32_ref[...] = k
        v32_ref[...] = v
    if rope:
        quarter = A_HEAD_DIM // 4
        q = _rope(q, ca_ref[...], sna_ref[...], spa_ref[...], quarter)
        k = _rope(k, ca_ref[:, :LANES], sna_ref[:, :LANES], spa_ref[:, :LANES], quarter)
    q_ref[...] = q.astype(BF16)
    k_ref[...] = k.astype(BF16)
    v_ref[...] = v.astype(BF16)

    s5u_ref[...] = z[:, 512:768]
    lx_ref[...] = z[:, 768:1024]
    lg_ref[...] = z[:, 1024:1280]

    mtabs = None
    mtabs4 = None
    if rope:
        mtabs = (cm_ref[...], snm_ref[...], spm_ref[...])
        mtabs4 = tuple(jnp.concatenate([t] * MLA_HEADS, axis=1) for t in mtabs)
    ql = _rmsnorm(z[:, 1280:1536], qlg_ref[...])
    qm = jnp.dot(ql.astype(BF16), wuq_ref[...], preferred_element_type=F32)
    qm = _tile_rmsnorm(qm, MLA_QK) * gmq_ref[...]
    if rope:
        qm = _rope(qm, *mtabs4, MLA_ROPE // 4)
    qm_ref[...] = qm.astype(BF16)

    ckv_n = _rmsnorm(z[:, 1536:1664], kvg_ref[...])
    krp = z[:, 1664:1792]
    km, vm = _mla_kv(ckv_n, krp, wukv_ref[...], gmk_ref[...], mtabs4)
    km_ref[...] = km.astype(BF16)
    vm_ref[...] = vm.astype(BF16)
    if ctx:
        ckv_ref[...] = ckv_n
        krp_ref[...] = krp


def _in_proj(x, sc, sh, lw, rope_tabs, ctx):
    wspec = _layer_spec(lw["layer"])
    B, L, _ = x.shape
    Bg = B // SUBLANES
    TL = TOK_TILE
    rope = rope_tabs is not None
    per_b = sc.shape[0] > 1

    def mod_map(b, i):
        return (b if per_b else 0, 0, 0)

    tok = lambda c: pl.BlockSpec((None, TL, c), lambda b, i: (b, i, 0))
    tview = pl.BlockSpec((None, TL, 256), lambda b, i: (b // SUBLANES, i, b % SUBLANES))
    in_specs = [tok(D_MODEL), pl.BlockSpec((None, 1, D_MODEL), mod_map),
                pl.BlockSpec((None, 1, D_MODEL), mod_map), wspec((1, D_MODEL)), wspec((D_MODEL, Z_COLS)),
                wspec((1, 256)), wspec((1, 128)), wspec((1, 256)), wspec((256, 512)), wspec((1, 512)),
                wspec((1, 128)), wspec((128, 768)), wspec((1, 512))]
    args = [x, sc, sh, lw["g1"], lw["wz"], lw["gq"], lw["gk"], lw["qlg"], lw["wuq"], lw["gmq"],
            lw["kvg"], lw["wukv"], lw["gmk"]]
    if rope:
        in_specs += [pl.BlockSpec((TL, 256), lambda b, i: (i, 0))] * 3
        in_specs += [pl.BlockSpec((TL, 128), lambda b, i: (i, 0))] * 3
        args += list(rope_tabs)
    out_specs = [tok(256), tok(128), tok(128), tview, tview, tok(256), tok(512), tok(512), tok(256)]
    sds = jax.ShapeDtypeStruct
    out_shape = [sds((B, L, 256), BF16), sds((B, L, 128), BF16), sds((B, L, 128), BF16),
                 sds((Bg, L, SUBLANES * 256), F32), sds((Bg, L, SUBLANES * 256), F32),
                 sds((B, L, 256), F32), sds((B, L, 512), BF16), sds((B, L, 512), BF16),
                 sds((B, L, 256), BF16)]
    if ctx:
        out_specs += [tok(128), tok(128), tok(128), tok(128)]
        out_shape += [sds((B, L, 128), F32)] * 4
    return pl.pallas_call(
        functools.partial(_in_proj_kernel, rope, ctx),
        grid=(B, L // TL),
        in_specs=in_specs, out_specs=out_specs, out_shape=out_shape,
        compiler_params=_params(("parallel", "parallel")),
        name="in_proj_ctx" if ctx else "in_proj_lat",
    )(*args)


def _mla_cache_kernel(ckv_ref, krp_ref, wukv_ref, gmk_ref, km_ref, vm_ref):
    km, vm = _mla_kv(ckv_ref[...], krp_ref[...], wukv_ref[...], gmk_ref[...], None)
    km_ref[...] = km.astype(BF16)
    vm_ref[...] = vm.astype(BF16)


def _mla_cache_kv(cckv, ckr_placed, lw):
    wspec = _layer_spec(lw["layer"])
    B, P, _ = cckv.shape
    tok = lambda c: pl.BlockSpec((None, P, c), lambda b: (b, 0, 0))
    return pl.pallas_call(
        _mla_cache_kernel,
        grid=(B,),
        in_specs=[tok(128), tok(128), wspec((128, 768)), wspec((1, 512))],
        out_specs=[tok(512), tok(256)],
        out_shape=[jax.ShapeDtypeStruct((B, P, 512), BF16), jax.ShapeDtypeStruct((B, P, 256), BF16)],
        compiler_params=_params(("parallel",)),
        name="mla_cache_kv",
    )(cckv, ckr_placed, lw["wukv"], lw["gmk"])


def _sink_softmax(s, sink):
    sink = sink * LOG2_E
    m = jnp.maximum(jnp.max(s, axis=-1, keepdims=True), sink)
    e = jnp.exp2(s - m)
    den = jnp.sum(e, axis=-1, keepdims=True) + jnp.exp2(sink - m)
    return e.astype(BF16), 1.0 / den


def _gqa_tile(qt, keys, vst, sink_lo, sink_hi, mask):
    lane = lax.broadcasted_iota(jnp.int32, (1, LANES), 1)
    lo = lane < A_HEAD_DIM
    zero = jnp.zeros_like(qt)
    ps, invs = [], []
    for qh, sink in ((jnp.where(lo, qt, zero), sink_lo), (jnp.where(lo, zero, qt), sink_hi)):
        s = lax.dot_general(qh, keys, (((1,), (1,)), ((), ())), preferred_element_type=F32)
        if mask is not None:
            s = jnp.where(mask, s, NEG_INF)
        p, inv = _sink_softmax(s, sink)
        ps.append(p)
        invs.append(inv)
    o = jnp.dot(jnp.concatenate(ps, axis=1), vst, preferred_element_type=F32)
    return o * jnp.where(lo, invs[0], invs[1])


def _stack_kv_halves(v):
    lane = lax.broadcasted_iota(jnp.int32, (1, LANES), 1)
    lo = lane < A_HEAD_DIM
    zero = jnp.zeros_like(v)
    return jnp.concatenate([jnp.where(lo, v, zero), jnp.where(lo, zero, v)], axis=0)


def _attn_a_ctx_kernel(sink_ref, q_ref, k_ref, v_ref, o_ref):
    keys = k_ref[...]
    vst = _stack_kv_halves(v_ref[...])
    for t in range(2):
        o = _gqa_tile(q_ref[:, LANES * t:LANES * (t + 1)], keys, vst, sink_ref[t], sink_ref[2 + t], None)
        o_ref[:, LANES * t:LANES * (t + 1)] = o.astype(o_ref.dtype)


def _attn_a_ctx(q, k, v, sink):
    B, L, _ = q.shape
    tok = lambda c: pl.BlockSpec((None, L, c), lambda b: (b, 0, 0))
    return pl.pallas_call(
        _attn_a_ctx_kernel,
        grid=(B,),
        in_specs=[pl.BlockSpec(memory_space=pltpu.SMEM), tok(256), tok(128), tok(128)],
        out_specs=tok(256),
        out_shape=jax.ShapeDtypeStruct((B, L, 256), BF16),
        compiler_params=_params(("parallel",)),
        name="attn_a_ctx",
    )(sink, q, k, v)


def _attn_a_lat_kernel(sink_ref, q_ref, kc_ref, vc_ref, k_ref, v_ref, o_ref):
    i = pl.program_id(1)
    L = k_ref.shape[0]
    n_ctx = kc_ref.shape[0]
    span = 3 * Q_BLOCK
    start = pl.multiple_of(jnp.clip((i - 1) * Q_BLOCK, 0, L - span), Q_BLOCK)
    keys = jnp.concatenate([kc_ref[...].astype(BF16), k_ref[pl.ds(start, span), :]], axis=0)
    vals = jnp.concatenate([vc_ref[...].astype(BF16), v_ref[pl.ds(start, span), :]], axis=0)
    vst = _stack_kv_halves(vals)
    col = lax.broadcasted_iota(jnp.int32, (Q_BLOCK, n_ctx + span), 1)
    row = lax.broadcasted_iota(jnp.int32, (Q_BLOCK, n_ctx + span), 0)
    rel = (col - n_ctx + start) - (row + i * Q_BLOCK)
    mask = (col < n_ctx) | (jnp.abs(rel) <= WINDOW)
    for t in range(2):
        o = _gqa_tile(q_ref[:, LANES * t:LANES * (t + 1)], keys, vst, sink_ref[t], sink_ref[2 + t], mask)
        o_ref[:, LANES * t:LANES * (t + 1)] = o.astype(o_ref.dtype)


def _attn_a_lat(q, k, v, kc, vc, sink):
    B, L, _ = q.shape
    P = kc.shape[1]
    blk = lambda c: pl.BlockSpec((None, Q_BLOCK, c), lambda b, i: (b, i, 0))
    whole = lambda n, c: pl.BlockSpec((None, n, c), lambda b, i: (b, 0, 0))
    return pl.pallas_call(
        _attn_a_lat_kernel,
        grid=(B, L // Q_BLOCK),
        in_specs=[pl.BlockSpec(memory_space=pltpu.SMEM), blk(256), whole(P, 128), whole(P, 128),
                  whole(L, 128), whole(L, 128)],
        out_specs=blk(256),
        out_shape=jax.ShapeDtypeStruct((B, L, 256), BF16),
        compiler_params=_params(("parallel", "parallel")),
        name="attn_a_lat",
    )(sink, q, kc, vc, k, v)


def _mla_attend(q, key_parts, val_parts):
    vals = val_parts[0] if len(val_parts) == 1 else jnp.concatenate(val_parts, axis=0)
    lane = lax.broadcasted_iota(jnp.int32, (1, MLA_HEADS * MLA_V), 1)
    zero = jnp.zeros_like(vals)
    vst = jnp.concatenate([jnp.where((lane >= MLA_V * h) & (lane < MLA_V * (h + 1)), vals, zero)
                           for h in range(MLA_HEADS)], axis=0)
    ps = []
    inv = jnp.zeros((q.shape[0], MLA_HEADS * MLA_V), F32)
    for h in range(MLA_HEADS):
        qh = q[:, LANES * h:LANES * (h + 1)]
        ss = [lax.dot_general(qh, kp[:, LANES * h:LANES * (h + 1)], (((1,), (1,)), ((), ())),
                              preferred_element_type=F32) for kp in key_parts]
        m = functools.reduce(jnp.maximum, [jnp.max(s, axis=-1, keepdims=True) for s in ss])
        es = [jnp.exp2(s - m) for s in ss]
        den = functools.reduce(lambda a, b: a + b, [jnp.sum(e, axis=-1, keepdims=True) for e in es])
        ps += [e.astype(BF16) for e in es]
        inv = jnp.where((lane >= MLA_V * h) & (lane < MLA_V * (h + 1)), 1.0 / den, inv)
    return jnp.dot(jnp.concatenate(ps, axis=1), vst, preferred_element_type=F32) * inv


def _mla_ctx_kernel(q_ref, k_ref, v_ref, o_ref):
    o_ref[...] = _mla_attend(q_ref[...], [k_ref[...]], [v_ref[...]]).astype(o_ref.dtype)


def _mla_ctx(q, k, v):
    B, L, _ = q.shape
    tok = lambda c: pl.BlockSpec((None, L, c), lambda b: (b, 0, 0))
    return pl.pallas_call(
        _mla_ctx_kernel,
        grid=(B,),
        in_specs=[tok(512), tok(512), tok(256)],
        out_specs=tok(256),
        out_shape=jax.ShapeDtypeStruct((B, L, 256), BF16),
        compiler_params=_params(("parallel",)),
        name="mla_ctx",
    )(q, k, v)


def _mla_lat_kernel(q_ref, kc_ref, vc_ref, k_ref, v_ref, o_ref):
    o = _mla_attend(q_ref[...], [kc_ref[...], k_ref[...]], [vc_ref[...], v_ref[...]])
    o_ref[...] = o.astype(o_ref.dtype)


def _mla_lat(q, k, v, kc, vc):
    B, L, _ = q.shape
    P = kc.shape[1]
    TQ = MLA_Q_TILE
    blk = lambda c: pl.BlockSpec((None, TQ, c), lambda b, i: (b, i, 0))
    whole = lambda n, c: pl.BlockSpec((None, n, c), lambda b, i: (b, 0, 0))
    return pl.pallas_call(
        _mla_lat_kernel,
        grid=(B, L // TQ),
        in_specs=[blk(512), whole(P, 512), whole(P, 256), whole(L, 512), whole(L, 256)],
        out_specs=blk(256),
        out_shape=jax.ShapeDtypeStruct((B, L, 256), BF16),
        compiler_params=_params(("parallel", "parallel")),
        name="mla_lat",
    )(q, kc, vc, k, v)


def _s5_kernel(uf_ref, ub_ref, bf_ref, bb_ref, a_ref, h0_ref, cf_ref, cb_ref,
               yf_ref, yb_ref, fin_ref, s_ref, st_ref):
    i = pl.program_id(1)
    steps = SCAN_STEPS

    @pl.when(i == 0)
    def _():
        st_ref[...] = h0_ref[...]

    uf = uf_ref[...].astype(BF16)
    ub = ub_ref[...].astype(BF16)
    s_ref[0] = jnp.dot(uf, bf_ref[:, :S5_WIDTH], preferred_element_type=F32)
    s_ref[1] = jnp.dot(uf, bf_ref[:, S5_WIDTH:], preferred_element_type=F32)
    s_ref[2] = jnp.dot(ub, bb_ref[:, :S5_WIDTH], preferred_element_type=F32)
    s_ref[3] = jnp.dot(ub, bb_ref[:, S5_WIDTH:], preferred_element_type=F32)

    lane_chunk = 2 * LANES
    for c in range(S5_WIDTH // lane_chunk):
        sl = slice(lane_chunk * c, lane_chunk * (c + 1))
        arf, aif, arb, aib = (a_ref[n, :, sl] for n in range(4))

        def body(j, carry):
            hrf, hif, hrb, hib = carry
            rf = pl.multiple_of(j * SUBLANES, SUBLANES)
            rb = pl.multiple_of((steps - 1 - j) * SUBLANES, SUBLANES)
            nrf = arf * hrf - aif * hif + s_ref[0, pl.ds(rf, SUBLANES), sl]
            nif = arf * hif + aif * hrf + s_ref[1, pl.ds(rf, SUBLANES), sl]
            nrb = arb * hrb - aib * hib + s_ref[2, pl.ds(rb, SUBLANES), sl]
            nib = arb * hib + aib * hrb + s_ref[3, pl.ds(rb, SUBLANES), sl]
            s_ref[0, pl.ds(rf, SUBLANES), sl] = nrf
            s_ref[1, pl.ds(rf, SUBLANES), sl] = nif
            s_ref[2, pl.ds(rb, SUBLANES), sl] = nrb
            s_ref[3, pl.ds(rb, SUBLANES), sl] = nib
            return nrf, nif, nrb, nib

        fin = lax.fori_loop(0, steps, body, tuple(st_ref[n, :, sl] for n in range(4)), unroll=4)
        for n in range(4):
            st_ref[n, :, sl] = fin[n]

    hf = jnp.concatenate([s_ref[0].astype(BF16), s_ref[1].astype(BF16)], axis=1)
    yf_ref[...] = jnp.dot(hf, cf_ref[...], preferred_element_type=F32)
    hb = jnp.concatenate([s_ref[2].astype(BF16), s_ref[3].astype(BF16)], axis=1)
    yb_ref[...] = jnp.dot(hb, cb_ref[...], preferred_element_type=F32)
    fin_ref[...] = st_ref[...]


def _s5_scan(u_rows, lw, h0):
    wspec = _layer_spec(lw["layer"])
    Bg, rows, _ = u_rows.shape
    R = SCAN_STEPS * SUBLANES
    n = rows // R
    fwd = pl.BlockSpec((None, R, 256), lambda g, i: (g, i, 0))
    bwd = pl.BlockSpec((None, R, 256), lambda g, i: (g, n - 1 - i, 0))
    st = pl.BlockSpec((None, 4, SUBLANES, S5_WIDTH), lambda g, i: (g, 0, 0, 0))
    return pl.pallas_call(
        _s5_kernel,
        grid=(Bg, n),
        in_specs=[fwd, bwd, wspec((256, 2 * S5_WIDTH)), wspec((256, 2 * S5_WIDTH)),
                  wspec((4, SUBLANES, S5_WIDTH)), st, wspec((2 * S5_WIDTH, 256)), wspec((2 * S5_WIDTH, 256))],
        out_specs=[fwd, bwd, st],
        out_shape=[jax.ShapeDtypeStruct((Bg, rows, 256), F32), jax.ShapeDtypeStruct((Bg, rows, 256), F32),
                   jax.ShapeDtypeStruct((Bg, 4, SUBLANES, S5_WIDTH), F32)],
        scratch_shapes=[pltpu.VMEM((4, R, S5_WIDTH), F32), pltpu.VMEM((4, SUBLANES, S5_WIDTH), F32)],
        compiler_params=_params(("parallel", "arbitrary")),
        name="s5_scan",
    )(u_rows, u_rows, lw["s5_bf"], lw["s5_bb"], lw["s5_a"], h0, lw["s5_cf"], lw["s5_cb"])


def _lru_gates(x_ref, pre_ref, post_ref, has_pre, has_post, cw_ref, cb_ref, w_ref, b_ref, sp_ref, a_ref, h_ref):
    R = x_ref.shape[0]
    pre = jnp.where(has_pre, pre_ref[...], 0.0)
    post = jnp.where(has_post, post_ref[...], 0.0)
    xp = jnp.concatenate([pre, x_ref[...], post], axis=0)
    xc = cb_ref[...]
    for t in range(LRU_CONV):
        xc = xc + xp[SUBLANES * t:SUBLANES * t + R] * cw_ref[t:t + 1, :]
    g = jnp.dot(xc.astype(BF16), w_ref[...], preferred_element_type=F32) + b_ref[...]
    r = jax.nn.sigmoid(g[:, :LRU_WIDTH])
    ig = jax.nn.sigmoid(g[:, LRU_WIDTH:])
    log_a = (-LRU_C) * r * sp_ref[...]
    a = jnp.exp(log_a)
    a_ref[...] = a
    h_ref[...] = jnp.sqrt(1.0 - a * a) * (ig * xc)


def _lru_kernel(xf_ref, xfp_ref, xfn_ref, xb_ref, xbp_ref, xbn_ref, cw_ref, cb_ref, wf_ref, wb_ref,
                bf_ref, bb_ref, spf_ref, spb_ref, h0_ref, hf_ref, hb_ref, fin_ref, af_ref, ab_ref, st_ref):
    i = pl.program_id(1)
    n = pl.num_programs(1)
    steps = SCAN_STEPS

    @pl.when(i == 0)
    def _():
        st_ref[...] = h0_ref[...]

    _lru_gates(xf_ref, xfp_ref, xfn_ref, i > 0, i < n - 1, cw_ref, cb_ref, wf_ref, bf_ref, spf_ref, af_ref, hf_ref)
    _lru_gates(xb_ref, xbp_ref, xbn_ref, i < n - 1, i > 0, cw_ref, cb_ref, wb_ref, bb_ref, spb_ref, ab_ref, hb_ref)

    def body(j, carry):
        hf, hb = carry
        rf = pl.multiple_of(j * SUBLANES, SUBLANES)
        rb = pl.multiple_of((steps - 1 - j) * SUBLANES, SUBLANES)
        nf = af_ref[pl.ds(rf, SUBLANES), :] * hf + hf_ref[pl.ds(rf, SUBLANES), :]
        nb = ab_ref[pl.ds(rb, SUBLANES), :] * hb + hb_ref[pl.ds(rb, SUBLANES), :]
        hf_ref[pl.ds(rf, SUBLANES), :] = nf
        hb_ref[pl.ds(rb, SUBLANES), :] = nb
        return nf, nb

    ff, fb = lax.fori_loop(0, steps, body, (st_ref[0], st_ref[1]), unroll=8)
    st_ref[0] = ff
    st_ref[1] = fb
    fin_ref[...] = st_ref[...]


def _lru_scan(x_rows, lw, h0):
    wspec = _layer_spec(lw["layer"])
    Bg, rows, _ = x_rows.shape
    R = SCAN_STEPS * SUBLANES
    n = rows // R
    pre_rows = 2 * SUBLANES
    fwd = lambda g, i: (g, i, 0)
    bwd = lambda g, i: (g, n - 1 - i, 0)
    blk = lambda m: pl.BlockSpec((None, R, 256), m)
    pre = lambda m: pl.BlockSpec(
        (None, pre_rows, 256), lambda g, i: (g, jnp.maximum(m(g, i)[1] * (R // pre_rows) - 1, 0), 0))
    post = lambda m: pl.BlockSpec(
        (None, SUBLANES, 256),
        lambda g, i: (g, jnp.minimum((m(g, i)[1] + 1) * (R // SUBLANES), rows // SUBLANES - 1), 0))
    st = pl.BlockSpec((None, 2, SUBLANES, 256), lambda g, i: (g, 0, 0, 0))
    return pl.pallas_call(
        _lru_kernel,
        grid=(Bg, n),
        in_specs=[blk(fwd), pre(fwd), post(fwd), blk(bwd), pre(bwd), post(bwd),
                  wspec((LRU_CONV, 256)), wspec((1, 256)), wspec((256, 512)), wspec((256, 512)),
                  wspec((1, 512)), wspec((1, 512)), wspec((1, 256)), wspec((1, 256)), st],
        out_specs=[blk(fwd), blk(bwd), st],
        out_shape=[jax.ShapeDtypeStruct((Bg, rows, 256), F32), jax.ShapeDtypeStruct((Bg, rows, 256), F32),
                   jax.ShapeDtypeStruct((Bg, 2, SUBLANES, 256), F32)],
        scratch_shapes=[pltpu.VMEM((R, 256), F32), pltpu.VMEM((R, 256), F32),
                        pltpu.VMEM((2, SUBLANES, 256), F32)],
        compiler_params=_params(("parallel", "arbitrary")),
        name="lru_scan",
    )(x_rows, x_rows, x_rows, x_rows, x_rows, x_rows, lw["lru_cw"], lw["lru_cb"], lw["lru_wf"], lw["lru_wb"],
      lw["lru_bf"], lw["lru_bb"], lw["lru_spf"], lw["lru_spb"], h0)


def _merge_kernel(x_ref, sc_ref, sh_ref, gt_ref, g1_ref, wg_ref, oa_ref, u_ref, yf_ref, yb_ref, d_ref,
                  wglu_ref, hf_ref, hb_ref, lg_ref, od_ref, wb_ref, wo_ref, o_ref):
    x = x_ref[...]
    h = _modnorm(x, g1_ref[...], sc_ref[...], sh_ref[...]).astype(BF16)

    yb5 = jax.nn.gelu(d_ref[...] * u_ref[...] + yf_ref[...] + yb_ref[...])
    gv = jnp.dot(yb5.astype(BF16), wglu_ref[...], preferred_element_type=F32)
    o_b = gv[:, :BRANCH_W] * jax.nn.sigmoid(gv[:, BRANCH_W:])
    o_c = (hf_ref[...] + hb_ref[...]) * jax.nn.gelu(lg_ref[...])
    branches = (oa_ref[...], o_b.astype(BF16), o_c.astype(BF16), od_ref[...])

    acc = jnp.zeros(x.shape, F32)
    for n in range(N_BRANCH):
        gate = jnp.dot(h, wg_ref[:, D_MODEL * n:D_MODEL * (n + 1)], preferred_element_type=F32)
        proj = jnp.dot(branches[n], wb_ref[n], preferred_element_type=F32)
        acc = acc + jax.nn.sigmoid(gate) * proj
    out = jnp.dot(acc.astype(BF16), wo_ref[...], preferred_element_type=F32)
    o_ref[...] = x + gt_ref[...] * out


def _merge(x, sc, sh, gt, lw, o_a, s5u_t, yf_t, yb_t, hf_t, hb_t, lg, o_d):
    wspec = _layer_spec(lw["layer"])
    B, L, _ = x.shape
    TL = TOK_TILE
    per_b = sc.shape[0] > 1

    def mod_map(b, i):
        return (b if per_b else 0, 0, 0)

    tok = lambda c: pl.BlockSpec((None, TL, c), lambda b, i: (b, i, 0))
    tview = pl.BlockSpec((None, TL, 256), lambda b, i: (b // SUBLANES, i, b % SUBLANES))
    mod = pl.BlockSpec((None, 1, D_MODEL), mod_map)
    return pl.pallas_call(
        _merge_kernel,
        grid=(B, L // TL),
        in_specs=[tok(D_MODEL), mod, mod, mod, wspec((1, D_MODEL)), wspec((D_MODEL, N_BRANCH * D_MODEL)),
                  tok(256), tview, tview, tview, wspec((1, 256)), wspec((256, 512)),
                  tview, tview, tok(256), tok(256), wspec((N_BRANCH, BRANCH_W, D_MODEL)),
                  wspec((D_MODEL, D_MODEL))],
        out_specs=tok(D_MODEL),
        out_shape=jax.ShapeDtypeStruct((B, L, D_MODEL), F32),
        compiler_params=_params(("parallel", "parallel")),
        name="merge",
    )(x, sc, sh, gt, lw["g1"], lw["wg"], o_a, s5u_t, yf_t, yb_t, lw["s5_d"], lw["wglu"],
      hf_t, hb_t, lg, o_d, lw["wb"], lw["wo"])


def _first_index(values, target):
    idx = jnp.full_like(target, float(len(values) - 1))
    for n in range(len(values) - 2, -1, -1):
        idx = jnp.where(values[n] == target, float(n), idx)
    return idx


def _list_max(values):
    return functools.reduce(jnp.maximum, values)


def _moe_sort_kernel(x_ref, sc_ref, sh_ref, g2_ref, wrh_ref, wrl_ref, br_ref, tri_ref,
                     hs_ref, gs_ref, dcol_ref, meta_ref):
    for s in range(x_ref.shape[0] // MOE_SORT_TILE):
        tok = pl.ds(s * MOE_SORT_TILE, MOE_SORT_TILE)
        srt = pl.ds(s * MOE_SORTED_ROWS, MOE_SORTED_ROWS)
        _moe_sort_tile(x_ref.at[tok], sc_ref, sh_ref, g2_ref, wrh_ref, wrl_ref, br_ref, tri_ref,
                       hs_ref.at[srt], gs_ref.at[srt], dcol_ref.at[tok], meta_ref.at[s])


def _moe_sort_tile(x_ref, sc_ref, sh_ref, g2_ref, wrh_ref, wrl_ref, br_ref, tri_ref,
                   hs_ref, gs_ref, dcol_ref, meta_ref):
    T = MOE_SORT_TILE
    h = _modnorm(x_ref[...], g2_ref[...], sc_ref[...], sh_ref[...])
    hh = h.astype(BF16)
    hl = (h - hh.astype(F32)).astype(BF16)
    nt = (((1,), (1,)), ((), ()))
    logits = (jnp.dot(hh, wrh_ref[...], preferred_element_type=F32)
              + jnp.dot(hl, wrh_ref[...], preferred_element_type=F32)
              + jnp.dot(hh, wrl_ref[...], preferred_element_type=F32)) + br_ref[...]
    lt = logits.T
    gl = [lt[g:g + 1, :] for g in range(N_GROUPS)]
    gmax = _list_max(gl)
    g_idx = _first_index(gl, gmax)
    pg = 1.0 / sum(jnp.exp(v - gmax) for v in gl)
    hot = [g_idx == float(g) for g in range(N_GROUPS)]
    el = []
    for e in range(EXPERTS_PER_GROUP):
        v = jnp.zeros_like(gmax)
        for g in range(N_GROUPS):
            r = N_GROUPS + EXPERTS_PER_GROUP * g + e
            v = jnp.where(hot[g], lt[r:r + 1, :], v)
        el.append(v)
    emax = _list_max(el)
    ee = [jnp.exp(v - emax) for v in el]
    esum = sum(ee)
    pe = [v / esum for v in ee]
    v1 = _list_max(pe)
    i1 = _first_index(pe, v1)
    pe2 = [jnp.where(i1 == float(e), -1.0, pe[e]) for e in range(EXPERTS_PER_GROUP)]
    v2 = _list_max(pe2)
    i2 = _first_index(pe2, v2)
    tot = v1 + v2
    w = [jnp.where(i1 == float(e), pg * v1 / tot, jnp.where(i2 == float(e), pg * v2 / tot, 0.0))
         for e in range(EXPERTS_PER_GROUP)]

    zero_row = jnp.zeros_like(gmax)
    g8 = jnp.concatenate([jnp.where(hot[g], 1.0, 0.0) for g in range(N_GROUPS)] + [zero_row] * 4, axis=0)
    cum = jnp.dot(g8.astype(BF16), tri_ref[...], preferred_element_type=F32)
    off = jnp.zeros((1, 1), F32)
    dest = zero_row
    counts = []
    for g in range(N_GROUPS):
        cnt = cum[g:g + 1, T - 1:T]
        padded = jnp.floor((cnt + (MOE_ROW_ALIGN - 1.0)) * (1.0 / MOE_ROW_ALIGN)) * MOE_ROW_ALIGN
        dest = jnp.where(hot[g], off + cum[g:g + 1, :] - 1.0, dest)
        off = off + padded
        counts.append(padded)
    rows = lax.broadcasted_iota(jnp.int32, (MOE_SORTED_ROWS, T), 0).astype(F32)
    perm = jnp.where(rows == dest, 1.0, 0.0).astype(BF16)
    hs_ref[...] = jnp.dot(perm, hh, preferred_element_type=F32).astype(BF16)
    gates = jnp.concatenate(w + [jnp.zeros((LANES - EXPERTS_PER_GROUP, T), F32)], axis=0)
    ghi = gates.astype(BF16)
    glo = (gates - ghi.astype(F32)).astype(BF16)
    gs_ref[...] = (lax.dot_general(perm, ghi, nt, preferred_element_type=F32)
                   + lax.dot_general(perm, glo, nt, preferred_element_type=F32))
    r_i = lax.broadcasted_iota(jnp.int32, (T, T), 0)
    c_i = lax.broadcasted_iota(jnp.int32, (T, T), 1)
    dcol = jnp.sum(jnp.where(r_i == c_i, jnp.broadcast_to(dest, (T, T)), 0.0), axis=1, keepdims=True)
    dcol_ref[...] = jnp.broadcast_to(dcol, (T, LANES))
    meta_ref[...] = jnp.concatenate([jnp.broadcast_to(c, (1, LANES)) for c in counts]
                                    + [jnp.zeros((SUBLANES - N_GROUPS, LANES), F32)], axis=0)


def _moe_sort(xf, sc, sh, lw, tiles_per_mod):
    wspec = _layer_spec(lw["layer"])
    N = xf.shape[0]
    T = MOE_SORT_TILE
    n_tiles = N // T
    sub = MOE_SORT_SUBTILES
    per_b = sc.shape[0] > 1
    mod = pl.BlockSpec((None, 1, D_MODEL), lambda t: (t * sub // tiles_per_mod if per_b else 0, 0, 0))
    sds = jax.ShapeDtypeStruct
    return pl.pallas_call(
        _moe_sort_kernel,
        grid=(n_tiles // sub,),
        in_specs=[pl.BlockSpec((sub * T, D_MODEL), lambda t: (t, 0)), mod, mod, wspec((1, D_MODEL)),
                  wspec((D_MODEL, LANES)), wspec((D_MODEL, LANES)), wspec((1, LANES)), wspec((T, T))],
        out_specs=[pl.BlockSpec((sub * MOE_SORTED_ROWS, D_MODEL), lambda t: (t, 0)),
                   pl.BlockSpec((sub * MOE_SORTED_ROWS, LANES), lambda t: (t, 0)),
                   pl.BlockSpec((sub * T, LANES), lambda t: (t, 0)),
                   pl.BlockSpec((sub, SUBLANES, LANES), lambda t: (t, 0, 0))],
        out_shape=[sds((n_tiles * MOE_SORTED_ROWS, D_MODEL), BF16), sds((n_tiles * MOE_SORTED_ROWS, LANES), F32),
                   sds((N, LANES), F32), sds((n_tiles, SUBLANES, LANES), F32)],
        compiler_params=_params(("parallel",)),
        name="moe_sort",
    )(xf, sc, sh, lw["g2"], lw["wr_hi"], lw["wr_lo"], lw["br"], lw["tri"])


def _moe_expert_kernel(cnt_ref, hs_ref, gs_ref, wga_ref, wup_ref, wdn_ref, ys_ref, ch_ref, cg_ref, cy_ref):
    j = pl.program_id(0)
    g = pl.program_id(1)
    piece = MOE_ROW_ALIGN

    @pl.when((j == 0) & (g == 0))
    def _():
        ch_ref[...] = jnp.zeros_like(ch_ref)
        cg_ref[...] = jnp.zeros_like(cg_ref)

    @pl.when(g == 0)
    def _():
        ys_ref[...] = jnp.zeros_like(ys_ref)

    def for_each_piece(move):
        packed = jnp.int32(0)
        for t in range(MOE_BLOCK_TILES):
            base = (j * MOE_BLOCK_TILES + t) * N_GROUPS
            start = jnp.int32(t * MOE_SORTED_ROWS)
            for g2 in range(N_GROUPS - 1):
                start = start + jnp.where(g2 < g, cnt_ref[base + g2], 0)
            n = cnt_ref[base + g]

            def body(k, carry, start=start, packed=packed):
                move(pl.multiple_of(start + k * piece, piece), pl.multiple_of(packed + k * piece, piece))
                return carry

            lax.fori_loop(0, lax.shift_right_logical(n, 4), body, 0)
            packed = packed + n
        return packed

    def pack(src, dst):
        ch_ref[pl.ds(dst, piece), :] = hs_ref[pl.ds(src, piece), :]
        cg_ref[pl.ds(dst, piece), :] = gs_ref[pl.ds(src, piece), :]

    rows = for_each_piece(pack)

    def chunk(c, carry):
        r = pl.multiple_of(c * MOE_CHUNK, MOE_CHUNK)
        h = ch_ref[pl.ds(r, MOE_CHUNK), :]
        acc = jnp.zeros((MOE_CHUNK, D_MODEL), F32)
        for e in range(EXPERTS_PER_GROUP):
            a = jnp.dot(h, wga_ref[e], preferred_element_type=F32)
            u = jnp.dot(h, wup_ref[e], preferred_element_type=F32)
            act = (a * jax.nn.sigmoid(a)) * u * cg_ref[pl.ds(r, MOE_CHUNK), e:e + 1]
            acc = acc + jnp.dot(act.astype(BF16), wdn_ref[e], preferred_element_type=F32)
        cy_ref[pl.ds(r, MOE_CHUNK), :] = acc.astype(BF16)
        return carry

    lax.fori_loop(0, lax.shift_right_logical(rows + (MOE_CHUNK - 1), MOE_CHUNK.bit_length() - 1), chunk, 0)

    def unpack(src, dst):
        ys_ref[pl.ds(src, piece), :] = cy_ref[pl.ds(dst, piece), :]

    for_each_piece(unpack)


def _moe_experts(cnt, hs, gs, lw):
    rows = MOE_BLOCK_TILES * MOE_SORTED_ROWS
    cap = MOE_BLOCK_TILES * MOE_SORT_TILE
    blk = lambda c: pl.BlockSpec((rows, c), lambda j, g, cnt: (j, 0))
    layer = lw["layer"]
    wspec = lambda r, c: pl.BlockSpec((None, EXPERTS_PER_GROUP, r, c), lambda j, g, cnt: (layer, g, 0, 0))
    return pl.pallas_call(
        _moe_expert_kernel,
        grid_spec=pltpu.PrefetchScalarGridSpec(
            num_scalar_prefetch=1, grid=(hs.shape[0] // rows, N_GROUPS),
            in_specs=[blk(D_MODEL), blk(LANES), wspec(D_MODEL, EXPERT_FF), wspec(D_MODEL, EXPERT_FF),
                      wspec(EXPERT_FF, D_MODEL)],
            out_specs=blk(D_MODEL),
            scratch_shapes=[pltpu.VMEM((cap, D_MODEL), BF16), pltpu.VMEM((cap, LANES), F32),
                            pltpu.VMEM((cap, D_MODEL), BF16)]),
        out_shape=jax.ShapeDtypeStruct(hs.shape, BF16),
        compiler_params=_params(("arbitrary", "arbitrary")),
        name="moe_experts",
    )(cnt, hs, gs, lw["wga"], lw["wup"], lw["wdn"])


def _moe_unsort_kernel(x_ref, gt_ref, ys_ref, dcol_ref, o_ref):
    cols = lax.broadcasted_iota(jnp.int32, (MOE_SORT_TILE, MOE_SORTED_ROWS), 1).astype(F32)
    perm_t = jnp.where(cols == dcol_ref[:, 0:1], 1.0, 0.0).astype(BF16)
    y = jnp.dot(perm_t, ys_ref[...], preferred_element_type=F32)
    o_ref[...] = x_ref[...] + gt_ref[...] * y


def _moe_unsort(xf, gt, ys, dcol, tiles_per_mod):
    N = xf.shape[0]
    T = MOE_SORT_TILE
    per_b = gt.shape[0] > 1
    mod = pl.BlockSpec((None, 1, D_MODEL), lambda t: (t // tiles_per_mod if per_b else 0, 0, 0))
    tok = pl.BlockSpec((T, D_MODEL), lambda t: (t, 0))
    return pl.pallas_call(
        _moe_unsort_kernel,
        grid=(N // T,),
        in_specs=[tok, mod, pl.BlockSpec((MOE_SORTED_ROWS, D_MODEL), lambda t: (t, 0)),
                  pl.BlockSpec((T, LANES), lambda t: (t, 0))],
        out_specs=tok,
        out_shape=jax.ShapeDtypeStruct((N, D_MODEL), F32),
        compiler_params=_params(("parallel",)),
        name="moe_unsort",
    )(xf, gt, ys, dcol)


def _moe(x, sc, sh, gt, lw):
    B, L, _ = x.shape
    N = B * L
    T = MOE_SORT_TILE
    xf = x.reshape(N, D_MODEL)
    hs, gs, dcol, meta = _moe_sort(xf, sc, sh, lw, L // T)

    cnt = meta[:, :N_GROUPS, 0].astype(jnp.int32).reshape(-1)
    ys = _moe_experts(cnt, hs, gs, lw)
    out = _moe_unsort(xf, gt, ys, dcol, L // T)
    return out.reshape(B, L, D_MODEL)


def _block_diag(blocks):
    n, r, c = blocks.shape
    eye = jnp.eye(n, dtype=blocks.dtype)
    return jnp.einsum("nrc,nm->nrmc", blocks, eye).reshape(n * r, n * c)


def _rope_tables(seq_len, dim, width, offset):
    rows = seq_len // GRID_W
    r, col = jnp.meshgrid(jnp.arange(rows), jnp.arange(GRID_W), indexing="ij")
    r = r.reshape(-1).astype(F32)
    col = col.reshape(-1).astype(F32)
    quarter = dim // 4
    freqs = ROPE_BASE ** (-jnp.arange(quarter, dtype=F32) / quarter)
    ang_r = r[:, None] * freqs
    ang_c = col[:, None] * freqs
    zero = jnp.zeros_like(ang_r)
    cos = jnp.cos(jnp.concatenate([ang_r, ang_r, ang_c, ang_c], axis=-1))
    sin_next = jnp.concatenate([-jnp.sin(ang_r), zero, -jnp.sin(ang_c), zero], axis=-1)
    sin_prev = jnp.concatenate([zero, jnp.sin(ang_r), zero, jnp.sin(ang_c)], axis=-1)

    def place(t, fill):
        return jnp.pad(t, ((0, 0), (offset, width - offset - dim)), constant_values=fill)

    return place(cos, 1.0), place(sin_next, 0.0), place(sin_prev, 0.0)


def _layer_weights(l, w):
    lw = {}
    w_in = w["w_in"][l]
    q_cols = w_in[:, 0:256].reshape(D_MODEL, A_HEADS, A_HEAD_DIM)[:, HEAD_ORDER].reshape(D_MODEL, 256)
    kr_cols = jnp.pad(w_in[:, 1664:1696], ((0, 0), (MLA_NOPE, LANES - MLA_QK)))
    lw["wz"] = jnp.concatenate([q_cols, w_in[:, 256:1664], kr_cols], axis=1).astype(BF16)
    lw["wg"] = w_in[:, 1696:].astype(BF16)
    lw["g1"] = w["norm1_g"][l][None]
    lw["g2"] = w["norm2_g"][l][None]
    lw["gq"] = jnp.tile(w["a_qnorm_g"][l], A_HEADS)[None] * (A_HEAD_DIM ** -0.5 * LOG2_E)
    lw["gk"] = jnp.tile(w["a_knorm_g"][l], A_KV_HEADS)[None]
    lw["sink"] = w["a_sink"][l]
    lw["qlg"] = w["mla_q_lat_norm"][l][None]
    wuq = w["mla_w_uq"][l].reshape(MLA_Q_LORA, MLA_HEADS, MLA_QK)
    lw["wuq"] = jnp.pad(wuq, ((0, 0), (0, 0), (0, LANES - MLA_QK))).reshape(MLA_Q_LORA, MLA_HEADS * LANES).astype(BF16)
    pad_g = lambda g: jnp.tile(jnp.pad(g, (0, LANES - MLA_QK)), MLA_HEADS)[None]
    lw["gmq"] = pad_g(w["mla_qnorm_g"][l]) * (MLA_QK ** -0.5 * LOG2_E)
    lw["gmk"] = pad_g(w["mla_knorm_g"][l])
    lw["kvg"] = w["mla_kv_norm"][l][None]
    wukv = w["mla_w_ukv"][l].reshape(MLA_KV_LORA, MLA_HEADS, MLA_NOPE + MLA_V)
    wk = jnp.pad(wukv[:, :, :MLA_NOPE], ((0, 0), (0, 0), (0, LANES - MLA_NOPE))).reshape(MLA_KV_LORA, MLA_HEADS * LANES)
    wv = wukv[:, :, MLA_NOPE:].reshape(MLA_KV_LORA, MLA_HEADS * MLA_V)
    lw["wukv"] = jnp.concatenate([wk, wv], axis=1).astype(BF16)

    lre = w["s5_lambda_re"][l]
    lim = w["s5_lambda_im"][l]
    dt = jnp.exp(w["s5_log_step"][l])[:, :, None]
    mag = jnp.exp(lre * dt)
    ar, ai = mag * jnp.cos(lim * dt), mag * jnp.sin(lim * dt)
    den = lre * lre + lim * lim
    fr = ((ar - 1.0) * lre + ai * lim) / den
    fi = (ai * lre - (ar - 1.0) * lim) / den
    br, bi = w["s5_b_re"][l], w["s5_b_im"][l]
    bbr = fr[..., None] * br - fi[..., None] * bi
    bbi = fr[..., None] * bi + fi[..., None] * br
    in_map = lambda d: jnp.concatenate(
        [_block_diag(jnp.swapaxes(bbr[d], 1, 2)), _block_diag(jnp.swapaxes(bbi[d], 1, 2))], axis=1).astype(BF16)
    out_map = lambda d: jnp.concatenate(
        [_block_diag(jnp.swapaxes(w["s5_c_re"][l][d], 1, 2)),
         -_block_diag(jnp.swapaxes(w["s5_c_im"][l][d], 1, 2))], axis=0).astype(BF16)
    lw["s5_bf"], lw["s5_bb"] = in_map(0), in_map(1)
    lw["s5_cf"], lw["s5_cb"] = out_map(0), out_map(1)
    coef = jnp.stack([ar[0].reshape(-1), ai[0].reshape(-1), ar[1].reshape(-1), ai[1].reshape(-1)])
    lw["s5_a"] = jnp.broadcast_to(coef[:, None, :], (4, SUBLANES, S5_WIDTH))
    lw["s5_d"] = w["s5_d"][l][None]
    lw["wglu"] = w["s5_w_glu"][l].astype(BF16)

    lw["lru_cw"] = w["lru_conv_w"][l]
    lw["lru_cb"] = w["lru_conv_b"][l][None]
    gate_w = lambda d: jnp.concatenate(
        [_block_diag(w["lru_w_a"][l][d]), _block_diag(w["lru_w_x"][l][d])], axis=1).astype(BF16)
    gate_b = lambda d: jnp.concatenate([w["lru_b_a"][l][d], w["lru_b_x"][l][d]])[None]
    lw["lru_wf"], lw["lru_wb"] = gate_w(0), gate_w(1)
    lw["lru_bf"], lw["lru_bb"] = gate_b(0), gate_b(1)
    sp = jax.nn.softplus(-w["lru_lambda"][l])
    lw["lru_spf"], lw["lru_spb"] = sp[0][None], sp[1][None]

    wb = w["w_branch"][l]
    wb0 = wb[0].reshape(A_HEADS, A_HEAD_DIM, D_MODEL)[HEAD_ORDER].reshape(BRANCH_W, D_MODEL)
    lw["wb"] = jnp.concatenate([wb0[None], wb[1:]], axis=0).astype(BF16)
    lw["wo"] = w["w_out"][l].astype(BF16)

    wr = jnp.pad(jnp.concatenate([w["moe_w_group"][l], w["moe_w_expert"][l]], axis=1),
                 ((0, 0), (0, LANES - N_GROUPS - N_EXPERTS)))
    lw["wr_hi"] = wr.astype(BF16)
    lw["wr_lo"] = (wr - lw["wr_hi"].astype(F32)).astype(BF16)
    lw["br"] = jnp.pad(jnp.concatenate([w["moe_b_group"][l], w["moe_b_expert"][l]]),
                       (0, LANES - N_GROUPS - N_EXPERTS))[None]
    lw["tri"] = jnp.triu(jnp.ones((MOE_SORT_TILE, MOE_SORT_TILE), BF16))
    lw["wga"] = w["moe_w_gate"][l].astype(BF16)
    lw["wup"] = w["moe_w_up"][l].astype(BF16)
    lw["wdn"] = w["moe_w_down"][l].astype(BF16)
    return lw


def _rows_to_state(fin, width):
    Bg, n = fin.shape[0], fin.shape[1]
    return jnp.swapaxes(fin, 1, 2).reshape(Bg * SUBLANES, n, width)


def _state_to_rows(state):
    B, n, width = state.shape
    return jnp.swapaxes(state.reshape(B // SUBLANES, SUBLANES, n, width), 1, 2)


def _mix(x, mods, lw, rope_tabs, cache, s5_h0, lru_h0, ctx):
    sh1, sc1, gt1 = mods
    B, L, _ = x.shape
    Bg = B // SUBLANES
    outs = _in_proj(x, sc1, sh1, lw, rope_tabs, ctx)
    q, k, v, s5u_t, lx_t, lg, qm, km, vm = outs[:9]
    if ctx:
        o_a = _attn_a_ctx(q, k, v, lw["sink"][lw["layer"]])
        o_d = _mla_ctx(qm, km, vm)
    else:
        ck, cv, cckv, ckr = cache
        o_a = _attn_a_lat(q, k, v, ck, cv, lw["sink"][lw["layer"]])
        kc, vc = _mla_cache_kv(cckv, ckr, lw)
        o_d = _mla_lat(qm, km, vm, kc, vc)
    rows = lambda t: t.reshape(Bg, L * SUBLANES, 256)
    view = lambda t: t.reshape(Bg, L, SUBLANES * 256)
    yf, yb, s5_fin = _s5_scan(rows(s5u_t), lw, s5_h0)
    hf, hb, lru_fin = _lru_scan(rows(lx_t), lw, lru_h0)
    x = _merge(x, sc1, sh1, gt1, lw, o_a, s5u_t, view(yf), view(yb), view(hf), view(hb), lg, o_d)
    return x, outs[9:], s5_fin, lru_fin


def kernel(x_prompt, x_sample, cache_attn_k, cache_attn_v, cache_mla_ckv, cache_mla_krope, state_ssm_re, state_ssm_im, state_lru, c, c_ctx, w_mod, b_mod, norm1_g, norm2_g, w_in, a_qnorm_g, a_knorm_g, a_sink, s5_lambda_re, s5_lambda_im, s5_log_step, s5_b_re, s5_b_im, s5_c_re, s5_c_im, s5_d, s5_w_glu, lru_conv_w, lru_conv_b, lru_w_a, lru_b_a, lru_w_x, lru_b_x, lru_lambda, mla_q_lat_norm, mla_w_uq, mla_kv_norm, mla_w_ukv, mla_qnorm_g, mla_knorm_g, w_branch, w_out, moe_w_group, moe_b_group, moe_w_expert, moe_b_expert, moe_w_gate, moe_w_up, moe_w_down):
    w = dict(norm1_g=norm1_g, norm2_g=norm2_g, w_in=w_in, a_qnorm_g=a_qnorm_g, a_knorm_g=a_knorm_g,
             a_sink=a_sink, s5_lambda_re=s5_lambda_re, s5_lambda_im=s5_lambda_im, s5_log_step=s5_log_step,
             s5_b_re=s5_b_re, s5_b_im=s5_b_im, s5_c_re=s5_c_re, s5_c_im=s5_c_im, s5_d=s5_d, s5_w_glu=s5_w_glu,
             lru_conv_w=lru_conv_w, lru_conv_b=lru_conv_b, lru_w_a=lru_w_a, lru_b_a=lru_b_a, lru_w_x=lru_w_x,
             lru_b_x=lru_b_x, lru_lambda=lru_lambda, mla_q_lat_norm=mla_q_lat_norm, mla_w_uq=mla_w_uq,
             mla_kv_norm=mla_kv_norm, mla_w_ukv=mla_w_ukv, mla_qnorm_g=mla_qnorm_g, mla_knorm_g=mla_knorm_g,
             w_branch=w_branch, w_out=w_out, moe_w_group=moe_w_group, moe_b_group=moe_b_group,
             moe_w_expert=moe_w_expert, moe_b_expert=moe_b_expert, moe_w_gate=moe_w_gate, moe_w_up=moe_w_up,
             moe_w_down=moe_w_down)
    B, L, _ = x_prompt.shape
    Bd, Ld, _ = x_sample.shape
    P = cache_attn_k.shape[2]

    cond = jnp.zeros((2 * SUBLANES, D_MODEL), F32).at[:Bd].set(c).at[Bd].set(c_ctx)
    mod = _modulation(cond, w_mod.astype(BF16), b_mod[:, None, :])
    mod = mod.reshape(DEPTH, 2 * SUBLANES, 6, D_MODEL)

    rope_a = _rope_tables(Ld, A_HEAD_DIM, A_HEAD_DIM, 0)
    rope_a = tuple(jnp.tile(t, (1, A_HEADS)) for t in rope_a)
    rope_m = _rope_tables(Ld, MLA_ROPE, LANES, MLA_NOPE)

    lw_all = jax.vmap(lambda w1: _layer_weights(0, {name: v[None] for name, v in w1.items()}))(w)

    xp, xs = x_prompt, x_sample
    ak_l, av_l, ckv_l, kr_l, sr_l, si_l, lru_l = [], [], [], [], [], [], []
    for l in range(DEPTH):
        lw = dict(lw_all, layer=l)
        lat_mod = [mod[l, :Bd, n][:, None, :] for n in range(6)]
        ctx_mod = [mod[l, Bd:Bd + 1, n][:, None, :] for n in range(6)]

        zs5 = jnp.zeros((B // SUBLANES, 4, SUBLANES, S5_WIDTH), F32)
        zlru = jnp.zeros((B // SUBLANES, 2, SUBLANES, LRU_WIDTH), F32)
        xp, (k32, v32, ckv_n, krp), s5_fin, lru_fin = _mix(xp, ctx_mod[0:3], lw, None, None, zs5, zlru, True)
        xp = _moe(xp, ctx_mod[4], ctx_mod[3], ctx_mod[5], lw)
        ak_l.append(k32.reshape(B, L, A_KV_HEADS, A_HEAD_DIM))
        av_l.append(v32.reshape(B, L, A_KV_HEADS, A_HEAD_DIM))
        ckv_l.append(ckv_n)
        kr_l.append(krp[:, :, MLA_NOPE:MLA_QK])
        fin = _rows_to_state(s5_fin, S5_WIDTH)
        sr_l.append(fin[:, 0::2].reshape(B, 2, S5_GROUPS, S5_STATE))
        si_l.append(fin[:, 1::2].reshape(B, 2, S5_GROUPS, S5_STATE))
        lru_l.append(_rows_to_state(lru_fin, LRU_WIDTH))

        sre = state_ssm_re[:, l].reshape(Bd, 2, S5_WIDTH)
        sim = state_ssm_im[:, l].reshape(Bd, 2, S5_WIDTH)
        s5_h0 = _state_to_rows(jnp.stack([sre[:, 0], sim[:, 0], sre[:, 1], sim[:, 1]], axis=1))
        lru_h0 = _state_to_rows(state_lru[:, l])
        cache = (cache_attn_k[:, l].reshape(Bd, P, 128), cache_attn_v[:, l].reshape(Bd, P, 128),
                 cache_mla_ckv[:, l],
                 jnp.pad(cache_mla_krope[:, l], ((0, 0), (0, 0), (MLA_NOPE, LANES - MLA_QK))))
        xs, _, _, _ = _mix(xs, lat_mod[0:3], lw, rope_a + rope_m, cache, s5_h0, lru_h0, False)
        xs = _moe(xs, lat_mod[4], lat_mod[3], lat_mod[5], lw)

    stack = lambda ts: jnp.stack(ts, axis=1)
    return (xp, xs, stack(ak_l), stack(av_l), stack(ckv_l), stack(kr_l), stack(sr_l), stack(si_l), stack(lru_l))
```

```python
import functools
import math

import jax
import jax.numpy as jnp
import numpy as np
from jax import lax
from jax.experimental import pallas as pl
from jax.experimental.pallas import tpu as pltpu

F32 = jnp.float32
BF16 = jnp.bfloat16

D_MODEL = 1024
DEPTH = 2
GRID_W = 64
N_BRANCH = 4
BRANCH_W = 256
ROPE_BASE = 10000.0
EPS = 1e-6
NEG_INF = -1e30
LOG2_E = math.log2(math.e)
A_HEADS = 4
A_KV_HEADS = 2
A_HEAD_DIM = 64
WINDOW = 128
Q_BLOCK = 128
S5_GROUP = 16
S5_GROUPS = 16
S5_STATE = 64
S5_WIDTH = S5_GROUPS * S5_STATE
LRU_WIDTH = 256
LRU_BLOCKS = 4
LRU_CONV = 4
LRU_C = 8.0
MLA_HEADS = 4
MLA_Q_LORA = 256
MLA_KV_LORA = 128
MLA_NOPE = 64
MLA_ROPE = 32
MLA_V = 64
MLA_QK = MLA_NOPE + MLA_ROPE
N_GROUPS = 4
EXPERTS_PER_GROUP = 4
N_EXPERTS = 16
EXPERT_FF = 256

LANES = 128
SUBLANES = 8
VMEM_LIMIT = 56 * 1024 * 1024

TOK_TILE = 256
TOK_STEPS = TOK_TILE // SUBLANES
SCAN_STEPS = 64
MOE_SORT_TILE = 256
MOE_ROW_ALIGN = 16
MOE_SORTED_ROWS = MOE_SORT_TILE + LANES
MOE_SORT_SUBTILES = 4
MOE_BLOCK_TILES = 8
MOE_CHUNK = 256
MLA_Q_TILE = 256
Z_COLS = 1792
HEAD_ORDER = np.array((0, 2, 1, 3))


def _params(sem):
    return pltpu.CompilerParams(dimension_semantics=sem, vmem_limit_bytes=VMEM_LIMIT)


def _layer_spec(layer):
    def spec(shape):
        n = len(shape)
        return pl.BlockSpec((None,) + tuple(shape), lambda *_: (layer,) + (0,) * n)
    return spec


def _modnorm(x, g, sc, sh):
    ms = jnp.mean(x * x, axis=-1, keepdims=True)
    return (x * lax.rsqrt(ms + EPS)) * g * (1.0 + sc) + sh


def _rmsnorm(x, g):
    ms = jnp.mean(x * x, axis=-1, keepdims=True)
    return (x * lax.rsqrt(ms + EPS)) * g


def _half_rmsnorm(x):
    lane = lax.broadcasted_iota(jnp.int32, (1, LANES), 1)
    lo = lane < A_HEAD_DIM
    outs = []
    for t in range(x.shape[1] // LANES):
        xt = x[:, LANES * t:LANES * (t + 1)]
        sq = xt * xt
        s_lo = jnp.sum(jnp.where(lo, sq, 0.0), axis=-1, keepdims=True)
        s_hi = jnp.sum(jnp.where(lo, 0.0, sq), axis=-1, keepdims=True)
        inv = jnp.where(lo, lax.rsqrt(s_lo * (1.0 / A_HEAD_DIM) + EPS),
                        lax.rsqrt(s_hi * (1.0 / A_HEAD_DIM) + EPS))
        outs.append(xt * inv)
    return jnp.concatenate(outs, axis=1)


def _tile_rmsnorm(x, n_real):
    outs = []
    for t in range(x.shape[1] // LANES):
        xt = x[:, LANES * t:LANES * (t + 1)]
        ss = jnp.sum(xt * xt, axis=-1, keepdims=True)
        outs.append(xt * lax.rsqrt(ss * (1.0 / n_real) + EPS))
    return jnp.concatenate(outs, axis=1)


def _rope(x, cos, sin_next, sin_prev, quarter):
    width = x.shape[1]
    return (x * cos + pltpu.roll(x, width - quarter, 1) * sin_next
            + pltpu.roll(x, quarter, 1) * sin_prev)


def _mla_kv(ckv_n, kr_placed, wukv, gk, rope_tabs):
    kv = jnp.dot(ckv_n.astype(BF16), wukv, preferred_element_type=F32)
    kfull = kv[:, :MLA_HEADS * LANES] + jnp.concatenate([kr_placed] * MLA_HEADS, axis=1)
    kn = _tile_rmsnorm(kfull, MLA_QK) * gk
    if rope_tabs is not None:
        kn = _rope(kn, *rope_tabs, MLA_ROPE // 4)
    return kn, kv[:, MLA_HEADS * LANES:]


def _mod_kernel(c_ref, w_ref, b_ref, o_ref):
    c = c_ref[...]
    s = c * jax.nn.sigmoid(c)
    o_ref[...] = jnp.dot(s.astype(BF16), w_ref[...], preferred_element_type=F32) + b_ref[...]


def _modulation(cond, w_mod, b_mod):
    n_rows = cond.shape[0]
    n_out = w_mod.shape[-1]
    tn = 1024
    return pl.pallas_call(
        _mod_kernel,
        grid=(DEPTH, n_out // tn),
        in_specs=[pl.BlockSpec((n_rows, D_MODEL), lambda l, j: (0, 0)),
                  pl.BlockSpec((None, D_MODEL, tn), lambda l, j: (l, 0, j)),
                  pl.BlockSpec((None, 1, tn), lambda l, j: (l, 0, j))],
        out_specs=pl.BlockSpec((None, n_rows, tn), lambda l, j: (l, 0, j)),
        out_shape=jax.ShapeDtypeStruct((DEPTH, n_rows, n_out), F32),
        compiler_params=_params(("arbitrary", "arbitrary")),
        name="modulation",
    )(cond, w_mod, b_mod)


def _group_rows(ref):
    v = ref[...]
    return v.reshape(v.shape[0] * v.shape[1], v.shape[2])


def _mod_rows(ref, steps):
    v = ref[...]
    if v.shape[0] == 1:
        return v[0]
    return jnp.broadcast_to(v, (v.shape[0], steps, v.shape[2])).reshape(v.shape[0] * steps, v.shape[2])


def _store_group(ref, val):
    ref[...] = val.reshape(ref.shape).astype(ref.dtype)


def _store_time_major(ref, val):
    steps = ref.shape[1] // SUBLANES
    for b in range(SUBLANES):
        for half in range(2):
            ref[half, pl.ds(b, steps, stride=SUBLANES), :] = (
                val[b * steps:(b + 1) * steps, LANES * half:LANES * (half + 1)])


def _load_time_major(ref):
    steps = ref.shape[1] // SUBLANES
    return jnp.concatenate(
        [jnp.concatenate([ref[half, pl.ds(b, steps, stride=SUBLANES), :] for half in range(2)], axis=1)
         for b in range(SUBLANES)], axis=0)


def _in_proj_kernel(rope, ctx, *refs):
    it = iter(refs)
    x_ref, sc_ref, sh_ref, g1_ref, wz_ref = (next(it) for _ in range(5))
    gq_ref, gk_ref, qlg_ref, wuq_ref, gmq_ref, kvg_ref, wukv_ref, gmk_ref = (next(it) for _ in range(8))
    if rope:
        ca_ref, sna_ref, spa_ref, cm_ref, snm_ref, spm_ref = (next(it) for _ in range(6))
    q_ref, k_ref, v_ref, s5u_ref, lx_ref, lg_ref, qm_ref, km_ref, vm_ref = (next(it) for _ in range(9))
    if ctx:
        k32_ref, v32_ref, ckv_ref, krp_ref = (next(it) for _ in range(4))

    steps = x_ref.shape[1]
    h = _modnorm(_group_rows(x_ref), g1_ref[...], _mod_rows(sc_ref, steps), _mod_rows(sh_ref, steps))
    z = jnp.dot(h.astype(BF16), wz_ref[...], preferred_element_type=F32)
    per_seq = lambda ref: jnp.concatenate([ref[...]] * SUBLANES, axis=0)

    q = _half_rmsnorm(z[:, 0:256]) * gq_ref[...]
    k = _half_rmsnorm(z[:, 256:384]) * gk_ref[...]
    v = z[:, 384:512]
    if ctx:
        _store_group(k32_ref, k)
        _store_group(v32_ref, v)
    if rope:
        quarter = A_HEAD_DIM // 4
        atabs = (per_seq(ca_ref), per_seq(sna_ref), per_seq(spa_ref))
        q = _rope(q, *atabs, quarter)
        k = _rope(k, *(t[:, :LANES] for t in atabs), quarter)
    _store_group(q_ref, q)
    _store_group(k_ref, k)
    _store_group(v_ref, v)

    _store_time_major(s5u_ref, z[:, 512:768])
    _store_time_major(lx_ref, z[:, 768:1024])
    _store_group(lg_ref, z[:, 1024:1280])

    mtabs4 = None
    if rope:
        mtabs4 = tuple(jnp.concatenate([per_seq(r)] * MLA_HEADS, axis=1) for r in (cm_ref, snm_ref, spm_ref))
    ql = _rmsnorm(z[:, 1280:1536], qlg_ref[...])
    qm = jnp.dot(ql.astype(BF16), wuq_ref[...], preferred_element_type=F32)
    qm = _tile_rmsnorm(qm, MLA_QK) * gmq_ref[...]
    if rope:
        qm = _rope(qm, *mtabs4, MLA_ROPE // 4)
    _store_group(qm_ref, qm)

    ckv_n = _rmsnorm(z[:, 1536:1664], kvg_ref[...])
    krp = z[:, 1664:1792]
    km, vm = _mla_kv(ckv_n, krp, wukv_ref[...], gmk_ref[...], mtabs4)
    _store_group(km_ref, km)
    _store_group(vm_ref, vm)
    if ctx:
        _store_group(ckv_ref, ckv_n)
        _store_group(krp_ref, krp)


def _group_specs(per_seq_mod):
    TS = TOK_STEPS
    tok = lambda c: pl.BlockSpec((SUBLANES, TS, c), lambda g, i: (g, i, 0))
    scan = pl.BlockSpec((None, 2, TS * SUBLANES, LANES), lambda g, i: (g, 0, i, 0))
    if per_seq_mod:
        mod = pl.BlockSpec((SUBLANES, 1, D_MODEL), lambda g, i: (g, 0, 0))
    else:
        mod = pl.BlockSpec((1, 1, D_MODEL), lambda g, i: (0, 0, 0))
    return tok, scan, mod


def _in_proj(x, sc, sh, lw, rope_tabs, ctx):
    wspec = _layer_spec(lw["layer"])
    B, L, _ = x.shape
    Bg = B // SUBLANES
    TS = TOK_STEPS
    rope = rope_tabs is not None
    tok, scan, mod = _group_specs(sc.shape[0] > 1)
    in_specs = [tok(D_MODEL), mod, mod, wspec((1, D_MODEL)), wspec((D_MODEL, Z_COLS)),
                wspec((1, 256)), wspec((1, 128)), wspec((1, 256)), wspec((256, 512)), wspec((1, 512)),
                wspec((1, 128)), wspec((128, 768)), wspec((1, 512))]
    args = [x, sc, sh, lw["g1"], lw["wz"], lw["gq"], lw["gk"], lw["qlg"], lw["wuq"], lw["gmq"],
            lw["kvg"], lw["wukv"], lw["gmk"]]
    if rope:
        in_specs += [pl.BlockSpec((TS, 256), lambda g, i: (i, 0))] * 3
        in_specs += [pl.BlockSpec((TS, 128), lambda g, i: (i, 0))] * 3
        args += list(rope_tabs)
    out_specs = [tok(256), tok(128), tok(128), scan, scan, tok(256), tok(512), tok(512), tok(256)]
    sds = jax.ShapeDtypeStruct
    scan_shape = sds((Bg, 2, L * SUBLANES, LANES), F32)
    out_shape = [sds((B, L, 256), BF16), sds((B, L, 128), BF16), sds((B, L, 128), BF16),
                 scan_shape, scan_shape,
                 sds((B, L, 256), F32), sds((B, L, 512), BF16), sds((B, L, 512), BF16),
                 sds((B, L, 256), BF16)]
    if ctx:
        out_specs += [tok(128), tok(128), tok(128), tok(128)]
        out_shape += [sds((B, L, 128), F32)] * 4
    return pl.pallas_call(
        functools.partial(_in_proj_kernel, rope, ctx),
        grid=(Bg, L // TS),
        in_specs=in_specs, out_specs=out_specs, out_shape=out_shape,
        compiler_params=_params(("parallel", "parallel")),
        name="in_proj_ctx" if ctx else "in_proj_lat",
    )(*args)


def _mla_cache_kernel(ckv_ref, krp_ref, wukv_ref, gmk_ref, km_ref, vm_ref):
    km, vm = _mla_kv(ckv_ref[...], krp_ref[...], wukv_ref[...], gmk_ref[...], None)
    km_ref[...] = km.astype(BF16)
    vm_ref[...] = vm.astype(BF16)


def _mla_cache_kv(cckv, ckr_placed, lw):
    wspec = _layer_spec(lw["layer"])
    B, P, _ = cckv.shape
    tok = lambda c: pl.BlockSpec((None, P, c), lambda b: (b, 0, 0))
    return pl.pallas_call(
        _mla_cache_kernel,
        grid=(B,),
        in_specs=[tok(128), tok(128), wspec((128, 768)), wspec((1, 512))],
        out_specs=[tok(512), tok(256)],
        out_shape=[jax.ShapeDtypeStruct((B, P, 512), BF16), jax.ShapeDtypeStruct((B, P, 256), BF16)],
        compiler_params=_params(("parallel",)),
        name="mla_cache_kv",
    )(cckv, ckr_placed, lw["wukv"], lw["gmk"])


def _sink_softmax(s, sink):
    sink = sink * LOG2_E
    m = jnp.maximum(jnp.max(s, axis=-1, keepdims=True), sink)
    e = jnp.exp2(s - m)
    den = jnp.sum(e, axis=-1, keepdims=True) + jnp.exp2(sink - m)
    return e.astype(BF16), 1.0 / den


def _gqa_tile(qt, keys, vst, sink_lo, sink_hi, mask):
    lane = lax.broadcasted_iota(jnp.int32, (1, LANES), 1)
    lo = lane < A_HEAD_DIM
    zero = jnp.zeros_like(qt)
    ps, invs = [], []
    for qh, sink in ((jnp.where(lo, qt, zero), sink_lo), (jnp.where(lo, zero, qt), sink_hi)):
        s = lax.dot_general(qh, keys, (((1,), (1,)), ((), ())), preferred_element_type=F32)
        if mask is not None:
            s = jnp.where(mask, s, NEG_INF)
        p, inv = _sink_softmax(s, sink)
        ps.append(p)
        invs.append(inv)
    o = jnp.dot(jnp.concatenate(ps, axis=1), vst, preferred_element_type=F32)
    return o * jnp.where(lo, invs[0], invs[1])


def _stack_kv_halves(v):
    lane = lax.broadcasted_iota(jnp.int32, (1, LANES), 1)
    lo = lane < A_HEAD_DIM
    zero = jnp.zeros_like(v)
    return jnp.concatenate([jnp.where(lo, v, zero), jnp.where(lo, zero, v)], axis=0)


def _attn_a_ctx_kernel(sink_ref, q_ref, k_ref, v_ref, o_ref):
    keys = k_ref[...]
    vst = _stack_kv_halves(v_ref[...])
    for t in range(2):
        o = _gqa_tile(q_ref[:, LANES * t:LANES * (t + 1)], keys, vst, sink_ref[t], sink_ref[2 + t], None)
        o_ref[:, LANES * t:LANES * (t + 1)] = o.astype(o_ref.dtype)


def _attn_a_ctx(q, k, v, sink):
    B, L, _ = q.shape
    tok = lambda c: pl.BlockSpec((None, L, c), lambda b: (b, 0, 0))
    return pl.pallas_call(
        _attn_a_ctx_kernel,
        grid=(B,),
        in_specs=[pl.BlockSpec(memory_space=pltpu.SMEM), tok(256), tok(128), tok(128)],
        out_specs=tok(256),
        out_shape=jax.ShapeDtypeStruct((B, L, 256), BF16),
        compiler_params=_params(("parallel",)),
        name="attn_a_ctx",
    )(sink, q, k, v)


def _attn_a_lat_kernel(sink_ref, q_ref, kc_ref, vc_ref, k_ref, v_ref, o_ref):
    i = pl.program_id(1)
    L = k_ref.shape[0]
    n_ctx = kc_ref.shape[0]
    span = 3 * Q_BLOCK
    start = pl.multiple_of(jnp.clip((i - 1) * Q_BLOCK, 0, L - span), Q_BLOCK)
    keys = jnp.concatenate([kc_ref[...].astype(BF16), k_ref[pl.ds(start, span), :]], axis=0)
    vals = jnp.concatenate([vc_ref[...].astype(BF16), v_ref[pl.ds(start, span), :]], axis=0)
    vst = _stack_kv_halves(vals)
    col = lax.broadcasted_iota(jnp.int32, (Q_BLOCK, n_ctx + span), 1)
    row = lax.broadcasted_iota(jnp.int32, (Q_BLOCK, n_ctx + span), 0)
    rel = (col - n_ctx + start) - (row + i * Q_BLOCK)
    mask = (col < n_ctx) | (jnp.abs(rel) <= WINDOW)
    for t in range(2):
        o = _gqa_tile(q_ref[:, LANES * t:LANES * (t + 1)], keys, vst, sink_ref[t], sink_ref[2 + t], mask)
        o_ref[:, LANES * t:LANES * (t + 1)] = o.astype(o_ref.dtype)


def _attn_a_lat(q, k, v, kc, vc, sink):
    B, L, _ = q.shape
    P = kc.shape[1]
    blk = lambda c: pl.BlockSpec((None, Q_BLOCK, c), lambda b, i: (b, i, 0))
    whole = lambda n, c: pl.BlockSpec((None, n, c), lambda b, i: (b, 0, 0))
    return pl.pallas_call(
        _attn_a_lat_kernel,
        grid=(B, L // Q_BLOCK),
        in_specs=[pl.BlockSpec(memory_space=pltpu.SMEM), blk(256), whole(P, 128), whole(P, 128),
                  whole(L, 128), whole(L, 128)],
        out_specs=blk(256),
        out_shape=jax.ShapeDtypeStruct((B, L, 256), BF16),
        compiler_params=_params(("parallel", "parallel")),
        name="attn_a_lat",
    )(sink, q, kc, vc, k, v)


def _mla_attend(q, key_parts, val_parts):
    vals = val_parts[0] if len(val_parts) == 1 else jnp.concatenate(val_parts, axis=0)
    lane = lax.broadcasted_iota(jnp.int32, (1, MLA_HEADS * MLA_V), 1)
    zero = jnp.zeros_like(vals)
    vst = jnp.concatenate([jnp.where((lane >= MLA_V * h) & (lane < MLA_V * (h + 1)), vals, zero)
                           for h in range(MLA_HEADS)], axis=0)
    ps = []
    inv = jnp.zeros((q.shape[0], MLA_HEADS * MLA_V), F32)
    for h in range(MLA_HEADS):
        qh = q[:, LANES * h:LANES * (h + 1)]
        ss = [lax.dot_general(qh, kp[:, LANES * h:LANES * (h + 1)], (((1,), (1,)), ((), ())),
                              preferred_element_type=F32) for kp in key_parts]
        m = functools.reduce(jnp.maximum, [jnp.max(s, axis=-1, keepdims=True) for s in ss])
        es = [jnp.exp2(s - m) for s in ss]
        den = functools.reduce(lambda a, b: a + b, [jnp.sum(e, axis=-1, keepdims=True) for e in es])
        ps += [e.astype(BF16) for e in es]
        inv = jnp.where((lane >= MLA_V * h) & (lane < MLA_V * (h + 1)), 1.0 / den, inv)
    return jnp.dot(jnp.concatenate(ps, axis=1), vst, preferred_element_type=F32) * inv


def _mla_ctx_kernel(q_ref, k_ref, v_ref, o_ref):
    o_ref[...] = _mla_attend(q_ref[...], [k_ref[...]], [v_ref[...]]).astype(o_ref.dtype)


def _mla_ctx(q, k, v):
    B, L, _ = q.shape
    tok = lambda c: pl.BlockSpec((None, L, c), lambda b: (b, 0, 0))
    return pl.pallas_call(
        _mla_ctx_kernel,
        grid=(B,),
        in_specs=[tok(512), tok(512), tok(256)],
        out_specs=tok(256),
        out_shape=jax.ShapeDtypeStruct((B, L, 256), BF16),
        compiler_params=_params(("parallel",)),
        name="mla_ctx",
    )(q, k, v)


def _mla_lat_kernel(q_ref, kc_ref, vc_ref, k_ref, v_ref, o_ref):
    o = _mla_attend(q_ref[...], [kc_ref[...], k_ref[...]], [vc_ref[...], v_ref[...]])
    o_ref[...] = o.astype(o_ref.dtype)


def _mla_lat(q, k, v, kc, vc):
    B, L, _ = q.shape
    P = kc.shape[1]
    TQ = MLA_Q_TILE
    blk = lambda c: pl.BlockSpec((None, TQ, c), lambda b, i: (b, i, 0))
    whole = lambda n, c: pl.BlockSpec((None, n, c), lambda b, i: (b, 0, 0))
    return pl.pallas_call(
        _mla_lat_kernel,
        grid=(B, L // TQ),
        in_specs=[blk(512), whole(P, 512), whole(P, 256), whole(L, 512), whole(L, 256)],
        out_specs=blk(256),
        out_shape=jax.ShapeDtypeStruct((B, L, 256), BF16),
        compiler_params=_params(("parallel", "parallel")),
        name="mla_lat",
    )(q, kc, vc, k, v)


def _join_halves(ref):
    return jnp.concatenate([ref[0], ref[1]], axis=1)


def _split_halves(ref, val):
    ref[0] = val[:, :LANES]
    ref[1] = val[:, LANES:]


def _s5_kernel(uf_ref, ub_ref, bf_ref, bb_ref, a_ref, h0_ref, cf_ref, cb_ref,
               yf_ref, yb_ref, fin_ref, s_ref, st_ref):
    i = pl.program_id(1)
    steps = SCAN_STEPS

    @pl.when(i == 0)
    def _():
        st_ref[...] = h0_ref[...]

    uf = _join_halves(uf_ref).astype(BF16)
    ub = _join_halves(ub_ref).astype(BF16)
    s_ref[0] = jnp.dot(uf, bf_ref[:, :S5_WIDTH], preferred_element_type=F32)
    s_ref[1] = jnp.dot(uf, bf_ref[:, S5_WIDTH:], preferred_element_type=F32)
    s_ref[2] = jnp.dot(ub, bb_ref[:, :S5_WIDTH], preferred_element_type=F32)
    s_ref[3] = jnp.dot(ub, bb_ref[:, S5_WIDTH:], preferred_element_type=F32)

    lane_chunk = 2 * LANES
    for c in range(S5_WIDTH // lane_chunk):
        sl = slice(lane_chunk * c, lane_chunk * (c + 1))
        arf, aif, arb, aib = (a_ref[n, :, sl] for n in range(4))

        def body(j, carry):
            hrf, hif, hrb, hib = carry
            rf = pl.multiple_of(j * SUBLANES, SUBLANES)
            rb = pl.multiple_of((steps - 1 - j) * SUBLANES, SUBLANES)
            nrf = arf * hrf - aif * hif + s_ref[0, pl.ds(rf, SUBLANES), sl]
            nif = arf * hif + aif * hrf + s_ref[1, pl.ds(rf, SUBLANES), sl]
            nrb = arb * hrb - aib * hib + s_ref[2, pl.ds(rb, SUBLANES), sl]
            nib = arb * hib + aib * hrb + s_ref[3, pl.ds(rb, SUBLANES), sl]
            s_ref[0, pl.ds(rf, SUBLANES), sl] = nrf
            s_ref[1, pl.ds(rf, SUBLANES), sl] = nif
            s_ref[2, pl.ds(rb, SUBLANES), sl] = nrb
            s_ref[3, pl.ds(rb, SUBLANES), sl] = nib
            return nrf, nif, nrb, nib

        fin = lax.fori_loop(0, steps, body, tuple(st_ref[n, :, sl] for n in range(4)), unroll=4)
        for n in range(4):
            st_ref[n, :, sl] = fin[n]

    hf = jnp.concatenate([s_ref[0].astype(BF16), s_ref[1].astype(BF16)], axis=1)
    _split_halves(yf_ref, jnp.dot(hf, cf_ref[...], preferred_element_type=F32))
    hb = jnp.concatenate([s_ref[2].astype(BF16), s_ref[3].astype(BF16)], axis=1)
    _split_halves(yb_ref, jnp.dot(hb, cb_ref[...], preferred_element_type=F32))
    fin_ref[...] = st_ref[...]


def _s5_scan(u_rows, lw, h0):
    wspec = _layer_spec(lw["layer"])
    Bg, _, rows, _ = u_rows.shape
    R = SCAN_STEPS * SUBLANES
    n = rows // R
    fwd = pl.BlockSpec((None, 2, R, LANES), lambda g, i: (g, 0, i, 0))
    bwd = pl.BlockSpec((None, 2, R, LANES), lambda g, i: (g, 0, n - 1 - i, 0))
    st = pl.BlockSpec((None, 4, SUBLANES, S5_WIDTH), lambda g, i: (g, 0, 0, 0))
    return pl.pallas_call(
        _s5_kernel,
        grid=(Bg, n),
        in_specs=[fwd, bwd, wspec((256, 2 * S5_WIDTH)), wspec((256, 2 * S5_WIDTH)),
                  wspec((4, SUBLANES, S5_WIDTH)), st, wspec((2 * S5_WIDTH, 256)), wspec((2 * S5_WIDTH, 256))],
        out_specs=[fwd, bwd, st],
        out_shape=[jax.ShapeDtypeStruct(u_rows.shape, F32), jax.ShapeDtypeStruct(u_rows.shape, F32),
                   jax.ShapeDtypeStruct((Bg, 4, SUBLANES, S5_WIDTH), F32)],
        scratch_shapes=[pltpu.VMEM((4, R, S5_WIDTH), F32), pltpu.VMEM((4, SUBLANES, S5_WIDTH), F32)],
        compiler_params=_params(("parallel", "arbitrary")),
        name="s5_scan",
    )(u_rows, u_rows, lw["s5_bf"], lw["s5_bb"], lw["s5_a"], h0, lw["s5_cf"], lw["s5_cb"])


def _lru_gates(x_ref, pre_ref, post_ref, has_pre, has_post, cw_ref, cb_ref, w_ref, b_ref, sp_ref, a_ref, h_ref):
    R = x_ref.shape[1]
    pre = jnp.where(has_pre, _join_halves(pre_ref), 0.0)
    post = jnp.where(has_post, _join_halves(post_ref), 0.0)
    xp = jnp.concatenate([pre, _join_halves(x_ref), post], axis=0)
    xc = cb_ref[...]
    for t in range(LRU_CONV):
        xc = xc + xp[SUBLANES * t:SUBLANES * t + R] * cw_ref[t:t + 1, :]
    g = jnp.dot(xc.astype(BF16), w_ref[...], preferred_element_type=F32) + b_ref[...]
    r = jax.nn.sigmoid(g[:, :LRU_WIDTH])
    ig = jax.nn.sigmoid(g[:, LRU_WIDTH:])
    log_a = (-LRU_C) * r * sp_ref[...]
    a = jnp.exp(log_a)
    a_ref[...] = a
    h_ref[...] = jnp.sqrt(1.0 - a * a) * (ig * xc)


def _lru_kernel(xf_ref, xfp_ref, xfn_ref, xb_ref, xbp_ref, xbn_ref, cw_ref, cb_ref, wf_ref, wb_ref,
                bf_ref, bb_ref, spf_ref, spb_ref, h0_ref, of_ref, ob_ref, fin_ref,
                af_ref, ab_ref, hf_ref, hb_ref, st_ref):
    i = pl.program_id(1)
    n = pl.num_programs(1)
    steps = SCAN_STEPS

    @pl.when(i == 0)
    def _():
        st_ref[...] = h0_ref[...]

    _lru_gates(xf_ref, xfp_ref, xfn_ref, i > 0, i < n - 1, cw_ref, cb_ref, wf_ref, bf_ref, spf_ref, af_ref, hf_ref)
    _lru_gates(xb_ref, xbp_ref, xbn_ref, i < n - 1, i > 0, cw_ref, cb_ref, wb_ref, bb_ref, spb_ref, ab_ref, hb_ref)

    def body(j, carry):
        hf, hb = carry
        rf = pl.multiple_of(j * SUBLANES, SUBLANES)
        rb = pl.multiple_of((steps - 1 - j) * SUBLANES, SUBLANES)
        nf = af_ref[pl.ds(rf, SUBLANES), :] * hf + hf_ref[pl.ds(rf, SUBLANES), :]
        nb = ab_ref[pl.ds(rb, SUBLANES), :] * hb + hb_ref[pl.ds(rb, SUBLANES), :]
        hf_ref[pl.ds(rf, SUBLANES), :] = nf
        hb_ref[pl.ds(rb, SUBLANES), :] = nb
        return nf, nb

    ff, fb = lax.fori_loop(0, steps, body, (st_ref[0], st_ref[1]), unroll=8)
    st_ref[0] = ff
    st_ref[1] = fb
    fin_ref[...] = st_ref[...]
    _split_halves(of_ref, hf_ref[...])
    _split_halves(ob_ref, hb_ref[...])


def _lru_scan(x_rows, lw, h0):
    wspec = _layer_spec(lw["layer"])
    Bg, _, rows, _ = x_rows.shape
    R = SCAN_STEPS * SUBLANES
    n = rows // R
    pre_rows = 2 * SUBLANES
    fwd = lambda g, i: i
    bwd = lambda g, i: n - 1 - i
    blk = lambda m: pl.BlockSpec((None, 2, R, LANES), lambda g, i: (g, 0, m(g, i), 0))
    pre = lambda m: pl.BlockSpec(
        (None, 2, pre_rows, LANES), lambda g, i: (g, 0, jnp.maximum(m(g, i) * (R // pre_rows) - 1, 0), 0))
    post = lambda m: pl.BlockSpec(
        (None, 2, SUBLANES, LANES),
        lambda g, i: (g, 0, jnp.minimum((m(g, i) + 1) * (R // SUBLANES), rows // SUBLANES - 1), 0))
    st = pl.BlockSpec((None, 2, SUBLANES, 256), lambda g, i: (g, 0, 0, 0))
    return pl.pallas_call(
        _lru_kernel,
        grid=(Bg, n),
        in_specs=[blk(fwd), pre(fwd), post(fwd), blk(bwd), pre(bwd), post(bwd),
                  wspec((LRU_CONV, 256)), wspec((1, 256)), wspec((256, 512)), wspec((256, 512)),
                  wspec((1, 512)), wspec((1, 512)), wspec((1, 256)), wspec((1, 256)), st],
        out_specs=[blk(fwd), blk(bwd), st],
        out_shape=[jax.ShapeDtypeStruct(x_rows.shape, F32), jax.ShapeDtypeStruct(x_rows.shape, F32),
                   jax.ShapeDtypeStruct((Bg, 2, SUBLANES, 256), F32)],
        scratch_shapes=[pltpu.VMEM((R, 256), F32), pltpu.VMEM((R, 256), F32), pltpu.VMEM((R, 256), F32),
                        pltpu.VMEM((R, 256), F32), pltpu.VMEM((2, SUBLANES, 256), F32)],
        compiler_params=_params(("parallel", "arbitrary")),
        name="lru_scan",
    )(x_rows, x_rows, x_rows, x_rows, x_rows, x_rows, lw["lru_cw"], lw["lru_cb"], lw["lru_wf"], lw["lru_wb"],
      lw["lru_bf"], lw["lru_bb"], lw["lru_spf"], lw["lru_spb"], h0)


def _merge_kernel(x_ref, sc_ref, sh_ref, gt_ref, g1_ref, wg_ref, oa_ref, u_ref, yf_ref, yb_ref, d_ref,
                  wglu_ref, hf_ref, hb_ref, lg_ref, od_ref, wb_ref, wo_ref, o_ref):
    steps = x_ref.shape[1]
    x = _group_rows(x_ref)
    h = _modnorm(x, g1_ref[...], _mod_rows(sc_ref, steps), _mod_rows(sh_ref, steps)).astype(BF16)

    yb5 = jax.nn.gelu(d_ref[...] * _load_time_major(u_ref) + _load_time_major(yf_ref) + _load_time_major(yb_ref))
    gv = jnp.dot(yb5.astype(BF16), wglu_ref[...], preferred_element_type=F32)
    o_b = gv[:, :BRANCH_W] * jax.nn.sigmoid(gv[:, BRANCH_W:])
    o_c = (_load_time_major(hf_ref) + _load_time_major(hb_ref)) * jax.nn.gelu(_group_rows(lg_ref))
    branches = (_group_rows(oa_ref), o_b.astype(BF16), o_c.astype(BF16), _group_rows(od_ref))

    acc = jnp.zeros(x.shape, F32)
    for n in range(N_BRANCH):
        gate = jnp.dot(h, wg_ref[:, D_MODEL * n:D_MODEL * (n + 1)], preferred_element_type=F32)
        proj = jnp.dot(branches[n], wb_ref[n], preferred_element_type=F32)
        acc = acc + jax.nn.sigmoid(gate) * proj
    out = jnp.dot(acc.astype(BF16), wo_ref[...], preferred_element_type=F32)
    _store_group(o_ref, x + _mod_rows(gt_ref, steps) * out)


def _merge(x, sc, sh, gt, lw, o_a, s5u_t, yf_t, yb_t, hf_t, hb_t, lg, o_d):
    wspec = _layer_spec(lw["layer"])
    B, L, _ = x.shape
    tok, scan, mod = _group_specs(sc.shape[0] > 1)
    return pl.pallas_call(
        _merge_kernel,
        grid=(B // SUBLANES, L // TOK_STEPS),
        in_specs=[tok(D_MODEL), mod, mod, mod, wspec((1, D_MODEL)), wspec((D_MODEL, N_BRANCH * D_MODEL)),
                  tok(256), scan, scan, scan, wspec((1, 256)), wspec((256, 512)),
                  scan, scan, tok(256), tok(256), wspec((N_BRANCH, BRANCH_W, D_MODEL)),
                  wspec((D_MODEL, D_MODEL))],
        out_specs=tok(D_MODEL),
        out_shape=jax.ShapeDtypeStruct((B, L, D_MODEL), F32),
        compiler_params=_params(("parallel", "parallel")),
        name="merge",
    )(x, sc, sh, gt, lw["g1"], lw["wg"], o_a, s5u_t, yf_t, yb_t, lw["s5_d"], lw["wglu"],
      hf_t, hb_t, lg, o_d, lw["wb"], lw["wo"])


def _first_index(values, target):
    idx = jnp.full_like(target, float(len(values) - 1))
    for n in range(len(values) - 2, -1, -1):
        idx = jnp.where(values[n] == target, float(n), idx)
    return idx


def _list_max(values):
    return functools.reduce(jnp.maximum, values)


def _moe_sort_kernel(x_ref, sc_ref, sh_ref, g2_ref, wrh_ref, wrl_ref, br_ref, tri_ref,
                     hs_ref, gs_ref, dcol_ref, meta_ref):
    for s in range(x_ref.shape[0] // MOE_SORT_TILE):
        tok = pl.ds(s * MOE_SORT_TILE, MOE_SORT_TILE)
        srt = pl.ds(s * MOE_SORTED_ROWS, MOE_SORTED_ROWS)
        _moe_sort_tile(x_ref.at[tok], sc_ref, sh_ref, g2_ref, wrh_ref, wrl_ref, br_ref, tri_ref,
                       hs_ref.at[srt], gs_ref.at[srt], dcol_ref.at[tok], meta_ref.at[s])


def _moe_sort_tile(x_ref, sc_ref, sh_ref, g2_ref, wrh_ref, wrl_ref, br_ref, tri_ref,
                   hs_ref, gs_ref, dcol_ref, meta_ref):
    T = MOE_SORT_TILE
    h = _modnorm(x_ref[...], g2_ref[...], sc_ref[...], sh_ref[...])
    hh = h.astype(BF16)
    hl = (h - hh.astype(F32)).astype(BF16)
    nt = (((1,), (1,)), ((), ()))
    logits = (jnp.dot(hh, wrh_ref[...], preferred_element_type=F32)
              + jnp.dot(hl, wrh_ref[...], preferred_element_type=F32)
              + jnp.dot(hh, wrl_ref[...], preferred_element_type=F32)) + br_ref[...]
    lt = logits.T
    gl = [lt[g:g + 1, :] for g in range(N_GROUPS)]
    gmax = _list_max(gl)
    g_idx = _first_index(gl, gmax)
    pg = 1.0 / sum(jnp.exp(v - gmax) for v in gl)
    hot = [g_idx == float(g) for g in range(N_GROUPS)]
    el = []
    for e in range(EXPERTS_PER_GROUP):
        v = jnp.zeros_like(gmax)
        for g in range(N_GROUPS):
            r = N_GROUPS + EXPERTS_PER_GROUP * g + e
            v = jnp.where(hot[g], lt[r:r + 1, :], v)
        el.append(v)
    emax = _list_max(el)
    ee = [jnp.exp(v - emax) for v in el]
    esum = sum(ee)
    pe = [v / esum for v in ee]
    v1 = _list_max(pe)
    i1 = _first_index(pe, v1)
    pe2 = [jnp.where(i1 == float(e), -1.0, pe[e]) for e in range(EXPERTS_PER_GROUP)]
    v2 = _list_max(pe2)
    i2 = _first_index(pe2, v2)
    tot = v1 + v2
    w = [jnp.where(i1 == float(e), pg * v1 / tot, jnp.where(i2 == float(e), pg * v2 / tot, 0.0))
         for e in range(EXPERTS_PER_GROUP)]

    zero_row = jnp.zeros_like(gmax)
    g8 = jnp.concatenate([jnp.where(hot[g], 1.0, 0.0) for g in range(N_GROUPS)] + [zero_row] * 4, axis=0)
    cum = jnp.dot(g8.astype(BF16), tri_ref[...], preferred_element_type=F32)
    off = jnp.zeros((1, 1), F32)
    dest = zero_row
    counts = []
    for g in range(N_GROUPS):
        cnt = cum[g:g + 1, T - 1:T]
        padded = jnp.floor((cnt + (MOE_ROW_ALIGN - 1.0)) * (1.0 / MOE_ROW_ALIGN)) * MOE_ROW_ALIGN
        dest = jnp.where(hot[g], off + cum[g:g + 1, :] - 1.0, dest)
        off = off + padded
        counts.append(padded)
    rows = lax.broadcasted_iota(jnp.int32, (MOE_SORTED_ROWS, T), 0).astype(F32)
    perm = jnp.where(rows == dest, 1.0, 0.0).astype(BF16)
    hs_ref[...] = jnp.dot(perm, hh, preferred_element_type=F32).astype(BF16)
    gates = jnp.concatenate(w + [jnp.zeros((LANES - EXPERTS_PER_GROUP, T), F32)], axis=0)
    ghi = gates.astype(BF16)
    glo = (gates - ghi.astype(F32)).astype(BF16)
    gs_ref[...] = (lax.dot_general(perm, ghi, nt, preferred_element_type=F32)
                   + lax.dot_general(perm, glo, nt, preferred_element_type=F32))
    r_i = lax.broadcasted_iota(jnp.int32, (T, T), 0)
    c_i = lax.broadcasted_iota(jnp.int32, (T, T), 1)
    dcol = jnp.sum(jnp.where(r_i == c_i, jnp.broadcast_to(dest, (T, T)), 0.0), axis=1, keepdims=True)
    dcol_ref[...] = jnp.broadcast_to(dcol, (T, LANES))
    meta_ref[...] = jnp.concatenate([jnp.broadcast_to(c, (1, LANES)) for c in counts]
                                    + [jnp.zeros((SUBLANES - N_GROUPS, LANES), F32)], axis=0)


def _moe_sort(xf, sc, sh, lw, tiles_per_mod):
    wspec = _layer_spec(lw["layer"])
    N = xf.shape[0]
    T = MOE_SORT_TILE
    n_tiles = N // T
    sub = MOE_SORT_SUBTILES
    per_b = sc.shape[0] > 1
    mod = pl.BlockSpec((None, 1, D_MODEL), lambda t: (t * sub // tiles_per_mod if per_b else 0, 0, 0))
    sds = jax.ShapeDtypeStruct
    return pl.pallas_call(
        _moe_sort_kernel,
        grid=(n_tiles // sub,),
        in_specs=[pl.BlockSpec((sub * T, D_MODEL), lambda t: (t, 0)), mod, mod, wspec((1, D_MODEL)),
                  wspec((D_MODEL, LANES)), wspec((D_MODEL, LANES)), wspec((1, LANES)), wspec((T, T))],
        out_specs=[pl.BlockSpec((sub * MOE_SORTED_ROWS, D_MODEL), lambda t: (t, 0)),
                   pl.BlockSpec((sub * MOE_SORTED_ROWS, LANES), lambda t: (t, 0)),
                   pl.BlockSpec((sub * T, LANES), lambda t: (t, 0)),
                   pl.BlockSpec((sub, SUBLANES, LANES), lambda t: (t, 0, 0))],
        out_shape=[sds((n_tiles * MOE_SORTED_ROWS, D_MODEL), BF16), sds((n_tiles * MOE_SORTED_ROWS, LANES), F32),
                   sds((N, LANES), F32), sds((n_tiles, SUBLANES, LANES), F32)],
        compiler_params=_params(("parallel",)),
        name="moe_sort",
    )(xf, sc, sh, lw["g2"], lw["wr_hi"], lw["wr_lo"], lw["br"], lw["tri"])


def _moe_expert_kernel(cnt_ref, hs_ref, gs_ref, wga_ref, wup_ref, wdn_ref, ys_ref, ch_ref, cg_ref, cy_ref):
    j = pl.program_id(0)
    g = pl.program_id(1)
    piece = MOE_ROW_ALIGN

    @pl.when((j == 0) & (g == 0))
    def _():
        ch_ref[...] = jnp.zeros_like(ch_ref)
        cg_ref[...] = jnp.zeros_like(cg_ref)

    @pl.when(g == 0)
    def _():
        ys_ref[...] = jnp.zeros_like(ys_ref)

    def for_each_piece(move):
        packed = jnp.int32(0)
        for t in range(MOE_BLOCK_TILES):
            base = (j * MOE_BLOCK_TILES + t) * N_GROUPS
            start = jnp.int32(t * MOE_SORTED_ROWS)
            for g2 in range(N_GROUPS - 1):
                start = start + jnp.where(g2 < g, cnt_ref[base + g2], 0)
            n = cnt_ref[base + g]

            def body(k, carry, start=start, packed=packed):
                move(pl.multiple_of(start + k * piece, piece), pl.multiple_of(packed + k * piece, piece))
                return carry

            lax.fori_loop(0, lax.shift_right_logical(n, 4), body, 0)
            packed = packed + n
        return packed

    def pack(src, dst):
        ch_ref[pl.ds(dst, piece), :] = hs_ref[pl.ds(src, piece), :]
        cg_ref[pl.ds(dst, piece), :] = gs_ref[pl.ds(src, piece), :]

    rows = for_each_piece(pack)

    def chunk(c, carry):
        r = pl.multiple_of(c * MOE_CHUNK, MOE_CHUNK)
        h = ch_ref[pl.ds(r, MOE_CHUNK), :]
        acc = jnp.zeros((MOE_CHUNK, D_MODEL), F32)
        for e in range(EXPERTS_PER_GROUP):
            a = jnp.dot(h, wga_ref[e], preferred_element_type=F32)
            u = jnp.dot(h, wup_ref[e], preferred_element_type=F32)
            act = (a * jax.nn.sigmoid(a)) * u * cg_ref[pl.ds(r, MOE_CHUNK), e:e + 1]
            acc = acc + jnp.dot(act.astype(BF16), wdn_ref[e], preferred_element_type=F32)
        cy_ref[pl.ds(r, MOE_CHUNK), :] = acc.astype(BF16)
        return carry

    lax.fori_loop(0, lax.shift_right_logical(rows + (MOE_CHUNK - 1), MOE_CHUNK.bit_length() - 1), chunk, 0)

    def unpack(src, dst):
        ys_ref[pl.ds(src, piece), :] = cy_ref[pl.ds(dst, piece), :]

    for_each_piece(unpack)


def _moe_experts(cnt, hs, gs, lw):
    rows = MOE_BLOCK_TILES * MOE_SORTED_ROWS
    cap = MOE_BLOCK_TILES * MOE_SORT_TILE
    blk = lambda c: pl.BlockSpec((rows, c), lambda j, g, cnt: (j, 0))
    layer = lw["layer"]
    wspec = lambda r, c: pl.BlockSpec((None, EXPERTS_PER_GROUP, r, c), lambda j, g, cnt: (layer, g, 0, 0))
    return pl.pallas_call(
        _moe_expert_kernel,
        grid_spec=pltpu.PrefetchScalarGridSpec(
            num_scalar_prefetch=1, grid=(hs.shape[0] // rows, N_GROUPS),
            in_specs=[blk(D_MODEL), blk(LANES), wspec(D_MODEL, EXPERT_FF), wspec(D_MODEL, EXPERT_FF),
                      wspec(EXPERT_FF, D_MODEL)],
            out_specs=blk(D_MODEL),
            scratch_shapes=[pltpu.VMEM((cap, D_MODEL), BF16), pltpu.VMEM((cap, LANES), F32),
                            pltpu.VMEM((cap, D_MODEL), BF16)]),
        out_shape=jax.ShapeDtypeStruct(hs.shape, BF16),
        compiler_params=_params(("arbitrary", "arbitrary")),
        name="moe_experts",
    )(cnt, hs, gs, lw["wga"], lw["wup"], lw["wdn"])


def _moe_unsort_kernel(x_ref, gt_ref, ys_ref, dcol_ref, o_ref):
    cols = lax.broadcasted_iota(jnp.int32, (MOE_SORT_TILE, MOE_SORTED_ROWS), 1).astype(F32)
    perm_t = jnp.where(cols == dcol_ref[:, 0:1], 1.0, 0.0).astype(BF16)
    y = jnp.dot(perm_t, ys_ref[...], preferred_element_type=F32)
    o_ref[...] = x_ref[...] + gt_ref[...] * y


def _moe_unsort(xf, gt, ys, dcol, tiles_per_mod):
    N = xf.shape[0]
    T = MOE_SORT_TILE
    per_b = gt.shape[0] > 1
    mod = pl.BlockSpec((None, 1, D_MODEL), lambda t: (t // tiles_per_mod if per_b else 0, 0, 0))
    tok = pl.BlockSpec((T, D_MODEL), lambda t: (t, 0))
    return pl.pallas_call(
        _moe_unsort_kernel,
        grid=(N // T,),
        in_specs=[tok, mod, pl.BlockSpec((MOE_SORTED_ROWS, D_MODEL), lambda t: (t, 0)),
                  pl.BlockSpec((T, LANES), lambda t: (t, 0))],
        out_specs=tok,
        out_shape=jax.ShapeDtypeStruct((N, D_MODEL), F32),
        compiler_params=_params(("parallel",)),
        name="moe_unsort",
    )(xf, gt, ys, dcol)


def _moe(x, sc, sh, gt, lw):
    B, L, _ = x.shape
    N = B * L
    T = MOE_SORT_TILE
    xf = x.reshape(N, D_MODEL)
    hs, gs, dcol, meta = _moe_sort(xf, sc, sh, lw, L // T)

    cnt = meta[:, :N_GROUPS, 0].astype(jnp.int32).reshape(-1)
    ys = _moe_experts(cnt, hs, gs, lw)
    out = _moe_unsort(xf, gt, ys, dcol, L // T)
    return out.reshape(B, L, D_MODEL)


def _block_diag(blocks):
    n, r, c = blocks.shape
    eye = jnp.eye(n, dtype=blocks.dtype)
    return jnp.einsum("nrc,nm->nrmc", blocks, eye).reshape(n * r, n * c)


def _rope_tables(seq_len, dim, width, offset):
    rows = seq_len // GRID_W
    r, col = jnp.meshgrid(jnp.arange(rows), jnp.arange(GRID_W), indexing="ij")
    r = r.reshape(-1).astype(F32)
    col = col.reshape(-1).astype(F32)
    quarter = dim // 4
    freqs = ROPE_BASE ** (-jnp.arange(quarter, dtype=F32) / quarter)
    ang_r = r[:, None] * freqs
    ang_c = col[:, None] * freqs
    zero = jnp.zeros_like(ang_r)
    cos = jnp.cos(jnp.concatenate([ang_r, ang_r, ang_c, ang_c], axis=-1))
    sin_next = jnp.concatenate([-jnp.sin(ang_r), zero, -jnp.sin(ang_c), zero], axis=-1)
    sin_prev = jnp.concatenate([zero, jnp.sin(ang_r), zero, jnp.sin(ang_c)], axis=-1)

    def place(t, fill):
        return jnp.pad(t, ((0, 0), (offset, width - offset - dim)), constant_values=fill)

    return place(cos, 1.0), place(sin_next, 0.0), place(sin_prev, 0.0)


def _layer_weights(l, w):
    lw = {}
    w_in = w["w_in"][l]
    q_cols = w_in[:, 0:256].reshape(D_MODEL, A_HEADS, A_HEAD_DIM)[:, HEAD_ORDER].reshape(D_MODEL, 256)
    kr_cols = jnp.pad(w_in[:, 1664:1696], ((0, 0), (MLA_NOPE, LANES - MLA_QK)))
    lw["wz"] = jnp.concatenate([q_cols, w_in[:, 256:1664], kr_cols], axis=1).astype(BF16)
    lw["wg"] = w_in[:, 1696:].astype(BF16)
    lw["g1"] = w["norm1_g"][l][None]
    lw["g2"] = w["norm2_g"][l][None]
    lw["gq"] = jnp.tile(w["a_qnorm_g"][l], A_HEADS)[None] * (A_HEAD_DIM ** -0.5 * LOG2_E)
    lw["gk"] = jnp.tile(w["a_knorm_g"][l], A_KV_HEADS)[None]
    lw["sink"] = w["a_sink"][l]
    lw["qlg"] = w["mla_q_lat_norm"][l][None]
    wuq = w["mla_w_uq"][l].reshape(MLA_Q_LORA, MLA_HEADS, MLA_QK)
    lw["wuq"] = jnp.pad(wuq, ((0, 0), (0, 0), (0, LANES - MLA_QK))).reshape(MLA_Q_LORA, MLA_HEADS * LANES).astype(BF16)
    pad_g = lambda g: jnp.tile(jnp.pad(g, (0, LANES - MLA_QK)), MLA_HEADS)[None]
    lw["gmq"] = pad_g(w["mla_qnorm_g"][l]) * (MLA_QK ** -0.5 * LOG2_E)
    lw["gmk"] = pad_g(w["mla_knorm_g"][l])
    lw["kvg"] = w["mla_kv_norm"][l][None]
    wukv = w["mla_w_ukv"][l].reshape(MLA_KV_LORA, MLA_HEADS, MLA_NOPE + MLA_V)
    wk = jnp.pad(wukv[:, :, :MLA_NOPE], ((0, 0), (0, 0), (0, LANES - MLA_NOPE))).reshape(MLA_KV_LORA, MLA_HEADS * LANES)
    wv = wukv[:, :, MLA_NOPE:].reshape(MLA_KV_LORA, MLA_HEADS * MLA_V)
    lw["wukv"] = jnp.concatenate([wk, wv], axis=1).astype(BF16)

    lre = w["s5_lambda_re"][l]
    lim = w["s5_lambda_im"][l]
    dt = jnp.exp(w["s5_log_step"][l])[:, :, None]
    mag = jnp.exp(lre * dt)
    ar, ai = mag * jnp.cos(lim * dt), mag * jnp.sin(lim * dt)
    den = lre * lre + lim * lim
    fr = ((ar - 1.0) * lre + ai * lim) / den
    fi = (ai * lre - (ar - 1.0) * lim) / den
    br, bi = w["s5_b_re"][l], w["s5_b_im"][l]
    bbr = fr[..., None] * br - fi[..., None] * bi
    bbi = fr[..., None] * bi + fi[..., None] * br
    in_map = lambda d: jnp.concatenate(
        [_block_diag(jnp.swapaxes(bbr[d], 1, 2)), _block_diag(jnp.swapaxes(bbi[d], 1, 2))], axis=1).astype(BF16)
    out_map = lambda d: jnp.concatenate(
        [_block_diag(jnp.swapaxes(w["s5_c_re"][l][d], 1, 2)),
         -_block_diag(jnp.swapaxes(w["s5_c_im"][l][d], 1, 2))], axis=0).astype(BF16)
    lw["s5_bf"], lw["s5_bb"] = in_map(0), in_map(1)
    lw["s5_cf"], lw["s5_cb"] = out_map(0), out_map(1)
    coef = jnp.stack([ar[0].reshape(-1), ai[0].reshape(-1), ar[1].reshape(-1), ai[1].reshape(-1)])
    lw["s5_a"] = jnp.broadcast_to(coef[:, None, :], (4, SUBLANES, S5_WIDTH))
    lw["s5_d"] = w["s5_d"][l][None]
    lw["wglu"] = w["s5_w_glu"][l].astype(BF16)

    lw["lru_cw"] = w["lru_conv_w"][l]
    lw["lru_cb"] = w["lru_conv_b"][l][None]
    gate_w = lambda d: jnp.concatenate(
        [_block_diag(w["lru_w_a"][l][d]), _block_diag(w["lru_w_x"][l][d])], axis=1).astype(BF16)
    gate_b = lambda d: jnp.concatenate([w["lru_b_a"][l][d], w["lru_b_x"][l][d]])[None]
    lw["lru_wf"], lw["lru_wb"] = gate_w(0), gate_w(1)
    lw["lru_bf"], lw["lru_bb"] = gate_b(0), gate_b(1)
    sp = jax.nn.softplus(-w["lru_lambda"][l])
    lw["lru_spf"], lw["lru_spb"] = sp[0][None], sp[1][None]

    wb = w["w_branch"][l]
    wb0 = wb[0].reshape(A_HEADS, A_HEAD_DIM, D_MODEL)[HEAD_ORDER].reshape(BRANCH_W, D_MODEL)
    lw["wb"] = jnp.concatenate([wb0[None], wb[1:]], axis=0).astype(BF16)
    lw["wo"] = w["w_out"][l].astype(BF16)

    wr = jnp.pad(jnp.concatenate([w["moe_w_group"][l], w["moe_w_expert"][l]], axis=1),
                 ((0, 0), (0, LANES - N_GROUPS - N_EXPERTS)))
    lw["wr_hi"] = wr.astype(BF16)
    lw["wr_lo"] = (wr - lw["wr_hi"].astype(F32)).astype(BF16)
    lw["br"] = jnp.pad(jnp.concatenate([w["moe_b_group"][l], w["moe_b_expert"][l]]),
                       (0, LANES - N_GROUPS - N_EXPERTS))[None]
    lw["tri"] = jnp.triu(jnp.ones((MOE_SORT_TILE, MOE_SORT_TILE), BF16))
    lw["wga"] = w["moe_w_gate"][l].astype(BF16)
    lw["wup"] = w["moe_w_up"][l].astype(BF16)
    lw["wdn"] = w["moe_w_down"][l].astype(BF16)
    return lw


def _rows_to_state(fin, width):
    Bg, n = fin.shape[0], fin.shape[1]
    return jnp.swapaxes(fin, 1, 2).reshape(Bg * SUBLANES, n, width)


def _state_to_rows(state):
    B, n, width = state.shape
    return jnp.swapaxes(state.reshape(B // SUBLANES, SUBLANES, n, width), 1, 2)


def _mix(x, mods, lw, rope_tabs, cache, s5_h0, lru_h0, ctx):
    sh1, sc1, gt1 = mods
    outs = _in_proj(x, sc1, sh1, lw, rope_tabs, ctx)
    q, k, v, s5u_t, lx_t, lg, qm, km, vm = outs[:9]
    if ctx:
        o_a = _attn_a_ctx(q, k, v, lw["sink"][lw["layer"]])
        o_d = _mla_ctx(qm, km, vm)
    else:
        ck, cv, cckv, ckr = cache
        o_a = _attn_a_lat(q, k, v, ck, cv, lw["sink"][lw["layer"]])
        kc, vc = _mla_cache_kv(cckv, ckr, lw)
        o_d = _mla_lat(qm, km, vm, kc, vc)
    yf, yb, s5_fin = _s5_scan(s5u_t, lw, s5_h0)
    hf, hb, lru_fin = _lru_scan(lx_t, lw, lru_h0)
    x = _merge(x, sc1, sh1, gt1, lw, o_a, s5u_t, yf, yb, hf, hb, lg, o_d)
    return x, outs[9:], s5_fin, lru_fin


def kernel(x_prompt, x_sample, cache_attn_k, cache_attn_v, cache_mla_ckv, cache_mla_krope, state_ssm_re, state_ssm_im, state_lru, c, c_ctx, w_mod, b_mod, norm1_g, norm2_g, w_in, a_qnorm_g, a_knorm_g, a_sink, s5_lambda_re, s5_lambda_im, s5_log_step, s5_b_re, s5_b_im, s5_c_re, s5_c_im, s5_d, s5_w_glu, lru_conv_w, lru_conv_b, lru_w_a, lru_b_a, lru_w_x, lru_b_x, lru_lambda, mla_q_lat_norm, mla_w_uq, mla_kv_norm, mla_w_ukv, mla_qnorm_g, mla_knorm_g, w_branch, w_out, moe_w_group, moe_b_group, moe_w_expert, moe_b_expert, moe_w_gate, moe_w_up, moe_w_down):
    w = dict(norm1_g=norm1_g, norm2_g=norm2_g, w_in=w_in, a_qnorm_g=a_qnorm_g, a_knorm_g=a_knorm_g,
             a_sink=a_sink, s5_lambda_re=s5_lambda_re, s5_lambda_im=s5_lambda_im, s5_log_step=s5_log_step,
             s5_b_re=s5_b_re, s5_b_im=s5_b_im, s5_c_re=s5_c_re, s5_c_im=s5_c_im, s5_d=s5_d, s5_w_glu=s5_w_glu,
             lru_conv_w=lru_conv_w, lru_conv_b=lru_conv_b, lru_w_a=lru_w_a, lru_b_a=lru_b_a, lru_w_x=lru_w_x,
             lru_b_x=lru_b_x, lru_lambda=lru_lambda, mla_q_lat_norm=mla_q_lat_norm, mla_w_uq=mla_w_uq,
             mla_kv_norm=mla_kv_norm, mla_w_ukv=mla_w_ukv, mla_qnorm_g=mla_qnorm_g, mla_knorm_g=mla_knorm_g,
             w_branch=w_branch, w_out=w_out, moe_w_group=moe_w_group, moe_b_group=moe_b_group,
             moe_w_expert=moe_w_expert, moe_b_expert=moe_b_expert, moe_w_gate=moe_w_gate, moe_w_up=moe_w_up,
             moe_w_down=moe_w_down)
    B, L, _ = x_prompt.shape
    Bd, Ld, _ = x_sample.shape
    P = cache_attn_k.shape[2]

    cond = jnp.zeros((2 * SUBLANES, D_MODEL), F32).at[:Bd].set(c).at[Bd].set(c_ctx)
    mod = _modulation(cond, w_mod.astype(BF16), b_mod[:, None, :])
    mod = mod.reshape(DEPTH, 2 * SUBLANES, 6, D_MODEL)

    rope_a = _rope_tables(Ld, A_HEAD_DIM, A_HEAD_DIM, 0)
    rope_a = tuple(jnp.tile(t, (1, A_HEADS)) for t in rope_a)
    rope_m = _rope_tables(Ld, MLA_ROPE, LANES, MLA_NOPE)

    lw_all = jax.vmap(lambda w1: _layer_weights(0, {name: v[None] for name, v in w1.items()}))(w)

    xp, xs = x_prompt, x_sample
    ak_l, av_l, ckv_l, kr_l, sr_l, si_l, lru_l = [], [], [], [], [], [], []
    for l in range(DEPTH):
        lw = dict(lw_all, layer=l)
        lat_mod = [mod[l, :Bd, n][:, None, :] for n in range(6)]
        ctx_mod = [mod[l, Bd:Bd + 1, n][:, None, :] for n in range(6)]

        zs5 = jnp.zeros((B // SUBLANES, 4, SUBLANES, S5_WIDTH), F32)
        zlru = jnp.zeros((B // SUBLANES, 2, SUBLANES, LRU_WIDTH), F32)
        xp, (k32, v32, ckv_n, krp), s5_fin, lru_fin = _mix(xp, ctx_mod[0:3], lw, None, None, zs5, zlru, True)
        xp = _moe(xp, ctx_mod[4], ctx_mod[3], ctx_mod[5], lw)
        ak_l.append(k32.reshape(B, L, A_KV_HEADS, A_HEAD_DIM))
        av_l.append(v32.reshape(B, L, A_KV_HEADS, A_HEAD_DIM))
        ckv_l.append(ckv_n)
        kr_l.append(krp[:, :, MLA_NOPE:MLA_QK])
        fin = _rows_to_state(s5_fin, S5_WIDTH)
        sr_l.append(fin[:, 0::2].reshape(B, 2, S5_GROUPS, S5_STATE))
        si_l.append(fin[:, 1::2].reshape(B, 2, S5_GROUPS, S5_STATE))
        lru_l.append(_rows_to_state(lru_fin, LRU_WIDTH))

        sre = state_ssm_re[:, l].reshape(Bd, 2, S5_WIDTH)
        sim = state_ssm_im[:, l].reshape(Bd, 2, S5_WIDTH)
        s5_h0 = _state_to_rows(jnp.stack([sre[:, 0], sim[:, 0], sre[:, 1], sim[:, 1]], axis=1))
        lru_h0 = _state_to_rows(state_lru[:, l])
        cache = (cache_attn_k[:, l].reshape(Bd, P, 128), cache_attn_v[:, l].reshape(Bd, P, 128),
                 cache_mla_ckv[:, l],
                 jnp.pad(cache_mla_krope[:, l], ((0, 0), (0, 0), (MLA_NOPE, LANES - MLA_QK))))
        xs, _, _, _ = _mix(xs, lat_mod[0:3], lw, rope_a + rope_m, cache, s5_h0, lru_h0, False)
        xs = _moe(xs, lat_mod[4], lat_mod[3], lat_mod[5], lw)

    stack = lambda ts: jnp.stack(ts, axis=1)
    return (xp, xs, stack(ak_l), stack(av_l), stack(ckv_l), stack(kr_l), stack(sr_l), stack(si_l), stack(lru_l))
```

```python
import functools
import math

import jax
import jax.numpy as jnp
import numpy as np
from jax import lax
from jax.experimental import pallas as pl
from jax.experimental.pallas import tpu as pltpu

F32 = jnp.float32
BF16 = jnp.bfloat16

D_MODEL = 1024
DEPTH = 2
GRID_W = 64
N_BRANCH = 4
BRANCH_W = 256
ROPE_BASE = 10000.0
EPS = 1e-6
NEG_INF = -1e30
LOG2_E = math.log2(math.e)
A_HEADS = 4
A_KV_HEADS = 2
A_HEAD_DIM = 64
WINDOW = 128
Q_BLOCK = 128
S5_GROUP = 16
S5_GROUPS = 16
S5_STATE = 64
S5_WIDTH = S5_GROUPS * S5_STATE
LRU_WIDTH = 256
LRU_BLOCKS = 4
LRU_CONV = 4
LRU_C = 8.0
MLA_HEADS = 4
MLA_Q_LORA = 256
MLA_KV_LORA = 128
MLA_NOPE = 64
MLA_ROPE = 32
MLA_V = 64
MLA_QK = MLA_NOPE + MLA_ROPE
N_GROUPS = 4
EXPERTS_PER_GROUP = 4
N_EXPERTS = 16
EXPERT_FF = 256

LANES = 128
SUBLANES = 8
VMEM_LIMIT = 56 * 1024 * 1024

TOK_TILE = 256
TOK_STEPS = TOK_TILE // SUBLANES
SCAN_STEPS = 128
MOE_SORT_TILE = 256
MOE_ROW_ALIGN = 16
MOE_SORTED_ROWS = MOE_SORT_TILE + LANES
MOE_SORT_SUBTILES = 4
MOE_BLOCK_TILES = 8
MOE_CHUNK = 512
MLA_Q_TILE = 256
Z_COLS = 1792
HEAD_ORDER = np.array((0, 2, 1, 3))


def _params(sem):
    return pltpu.CompilerParams(dimension_semantics=sem, vmem_limit_bytes=VMEM_LIMIT)


def _layer_spec(layer):
    def spec(shape):
        n = len(shape)
        return pl.BlockSpec((None,) + tuple(shape), lambda *_: (layer,) + (0,) * n)
    return spec


def _modnorm(x, g, sc, sh):
    ms = jnp.mean(x * x, axis=-1, keepdims=True)
    return (x * lax.rsqrt(ms + EPS)) * g * (1.0 + sc) + sh


def _rmsnorm(x, g):
    ms = jnp.mean(x * x, axis=-1, keepdims=True)
    return (x * lax.rsqrt(ms + EPS)) * g


def _half_rmsnorm(x):
    lane = lax.broadcasted_iota(jnp.int32, (1, LANES), 1)
    lo = lane < A_HEAD_DIM
    outs = []
    for t in range(x.shape[1] // LANES):
        xt = x[:, LANES * t:LANES * (t + 1)]
        sq = xt * xt
        s_lo = jnp.sum(jnp.where(lo, sq, 0.0), axis=-1, keepdims=True)
        s_hi = jnp.sum(jnp.where(lo, 0.0, sq), axis=-1, keepdims=True)
        inv = jnp.where(lo, lax.rsqrt(s_lo * (1.0 / A_HEAD_DIM) + EPS),
                        lax.rsqrt(s_hi * (1.0 / A_HEAD_DIM) + EPS))
        outs.append(xt * inv)
    return jnp.concatenate(outs, axis=1)


def _tile_rmsnorm(x, n_real):
    outs = []
    for t in range(x.shape[1] // LANES):
        xt = x[:, LANES * t:LANES * (t + 1)]
        ss = jnp.sum(xt * xt, axis=-1, keepdims=True)
        outs.append(xt * lax.rsqrt(ss * (1.0 / n_real) + EPS))
    return jnp.concatenate(outs, axis=1)


def _rope(x, cos, sin_next, sin_prev, quarter):
    width = x.shape[1]
    return (x * cos + pltpu.roll(x, width - quarter, 1) * sin_next
            + pltpu.roll(x, quarter, 1) * sin_prev)


def _mla_kv(ckv_n, kr_placed, wukv, gk, rope_tabs):
    kv = jnp.dot(ckv_n.astype(BF16), wukv, preferred_element_type=F32)
    gain = gk[:, :LANES]
    rotated = None
    if rope_tabs is not None:
        cos, sin_next, sin_prev = rope_tabs
        quarter = MLA_ROPE // 4
        base = kr_placed * gain
        rotated = pltpu.roll(base, LANES - quarter, 1) * sin_next + pltpu.roll(base, quarter, 1) * sin_prev
    heads = []
    for h in range(MLA_HEADS):
        kf = kv[:, LANES * h:LANES * (h + 1)] + kr_placed
        inv = lax.rsqrt(jnp.sum(kf * kf, axis=-1, keepdims=True) * (1.0 / MLA_QK) + EPS)
        kh = kf * gain
        if rotated is not None:
            kh = kh * cos + rotated
        heads.append(kh * inv)
    return jnp.concatenate(heads, axis=1), kv[:, MLA_HEADS * LANES:]


def _mod_kernel(c_ref, w_ref, b_ref, o_ref):
    c = c_ref[...]
    s = c * jax.nn.sigmoid(c)
    o_ref[...] = jnp.dot(s.astype(BF16), w_ref[...].astype(BF16), preferred_element_type=F32) + b_ref[...]


def _modulation(cond, w_mod, b_mod):
    n_rows = cond.shape[0]
    n_out = w_mod.shape[-1]
    tn = 1024
    return pl.pallas_call(
        _mod_kernel,
        grid=(DEPTH, n_out // tn),
        in_specs=[pl.BlockSpec((n_rows, D_MODEL), lambda l, j: (0, 0)),
                  pl.BlockSpec((None, D_MODEL, tn), lambda l, j: (l, 0, j)),
                  pl.BlockSpec((None, 1, tn), lambda l, j: (l, 0, j))],
        out_specs=pl.BlockSpec((None, n_rows, tn), lambda l, j: (l, 0, j)),
        out_shape=jax.ShapeDtypeStruct((DEPTH, n_rows, n_out), F32),
        compiler_params=_params(("arbitrary", "arbitrary")),
        name="modulation",
    )(cond, w_mod, b_mod)


def _group_rows(ref):
    v = ref[...]
    return v.reshape(v.shape[0] * v.shape[1], v.shape[2])


def _mod_rows(ref, steps):
    v = ref[...]
    if v.shape[0] == 1:
        return v[0]
    return jnp.broadcast_to(v, (v.shape[0], steps, v.shape[2])).reshape(v.shape[0] * steps, v.shape[2])


def _store_group(ref, val):
    ref[...] = val.reshape(ref.shape).astype(ref.dtype)


def _store_time_major(ref, val):
    steps = ref.shape[1] // SUBLANES
    for b in range(SUBLANES):
        for half in range(2):
            ref[half, pl.ds(b, steps, stride=SUBLANES), :] = (
                val[b * steps:(b + 1) * steps, LANES * half:LANES * (half + 1)])


def _load_time_major(ref):
    steps = ref.shape[1] // SUBLANES
    return jnp.concatenate(
        [jnp.concatenate([ref[half, pl.ds(b, steps, stride=SUBLANES), :] for half in range(2)], axis=1)
         for b in range(SUBLANES)], axis=0)


def _in_proj_kernel(rope, ctx, *refs):
    it = iter(refs)
    x_ref, sc_ref, sh_ref, g1_ref, wz_ref = (next(it) for _ in range(5))
    gq_ref, gk_ref, qlg_ref, wuq_ref, gmq_ref, kvg_ref, wukv_ref, gmk_ref = (next(it) for _ in range(8))
    if rope:
        ca_ref, sna_ref, spa_ref, cm_ref, snm_ref, spm_ref = (next(it) for _ in range(6))
    q_ref, k_ref, v_ref, s5u_ref, lx_ref, lg_ref, qm_ref, km_ref, vm_ref = (next(it) for _ in range(9))
    if ctx:
        k32_ref, v32_ref, ckv_ref, krp_ref = (next(it) for _ in range(4))

    steps = x_ref.shape[1]
    h = _modnorm(_group_rows(x_ref), g1_ref[...], _mod_rows(sc_ref, steps), _mod_rows(sh_ref, steps))
    z = jnp.dot(h.astype(BF16), wz_ref[...], preferred_element_type=F32)
    per_seq = lambda ref: jnp.concatenate([ref[...]] * SUBLANES, axis=0)

    q = _half_rmsnorm(z[:, 0:256]) * gq_ref[...]
    k = _half_rmsnorm(z[:, 256:384]) * gk_ref[...]
    v = z[:, 384:512]
    if ctx:
        _store_group(k32_ref, k)
        _store_group(v32_ref, v)
    if rope:
        quarter = A_HEAD_DIM // 4
        atabs = (per_seq(ca_ref), per_seq(sna_ref), per_seq(spa_ref))
        q = _rope(q, *atabs, quarter)
        k = _rope(k, *(t[:, :LANES] for t in atabs), quarter)
    _store_group(q_ref, q)
    _store_group(k_ref, k)
    _store_group(v_ref, v)

    _store_time_major(s5u_ref, z[:, 512:768])
    _store_time_major(lx_ref, z[:, 768:1024])
    _store_group(lg_ref, z[:, 1024:1280])

    mtabs = mtabs4 = None
    if rope:
        mtabs = tuple(per_seq(r) for r in (cm_ref, snm_ref, spm_ref))
        mtabs4 = tuple(jnp.concatenate([t] * MLA_HEADS, axis=1) for t in mtabs)
    ql = _rmsnorm(z[:, 1280:1536], qlg_ref[...])
    qm = jnp.dot(ql.astype(BF16), wuq_ref[...], preferred_element_type=F32)
    qm = _tile_rmsnorm(qm, MLA_QK) * gmq_ref[...]
    if rope:
        qm = _rope(qm, *mtabs4, MLA_ROPE // 4)
    _store_group(qm_ref, qm)

    ckv_n = _rmsnorm(z[:, 1536:1664], kvg_ref[...])
    krp = z[:, 1664:1792]
    km, vm = _mla_kv(ckv_n, krp, wukv_ref[...], gmk_ref[...], mtabs)
    _store_group(km_ref, km)
    _store_group(vm_ref, vm)
    if ctx:
        _store_group(ckv_ref, ckv_n)
        _store_group(krp_ref, krp)


def _group_specs(per_seq_mod):
    TS = TOK_STEPS
    tok = lambda c: pl.BlockSpec((SUBLANES, TS, c), lambda g, i: (g, i, 0))
    scan = pl.BlockSpec((None, 2, TS * SUBLANES, LANES), lambda g, i: (g, 0, i, 0))
    if per_seq_mod:
        mod = pl.BlockSpec((SUBLANES, 1, D_MODEL), lambda g, i: (g, 0, 0))
    else:
        mod = pl.BlockSpec((1, 1, D_MODEL), lambda g, i: (0, 0, 0))
    return tok, scan, mod


def _in_proj(x, sc, sh, lw, rope_tabs, ctx):
    wspec = _layer_spec(lw["layer"])
    B, L, _ = x.shape
    Bg = B // SUBLANES
    TS = TOK_STEPS
    rope = rope_tabs is not None
    tok, scan, mod = _group_specs(sc.shape[0] > 1)
    in_specs = [tok(D_MODEL), mod, mod, wspec((1, D_MODEL)), wspec((D_MODEL, Z_COLS)),
                wspec((1, 256)), wspec((1, 128)), wspec((1, 256)), wspec((256, 512)), wspec((1, 512)),
                wspec((1, 128)), wspec((128, 768)), wspec((1, 512))]
    args = [x, sc, sh, lw["g1"], lw["wz"], lw["gq"], lw["gk"], lw["qlg"], lw["wuq"], lw["gmq"],
            lw["kvg"], lw["wukv"], lw["gmk"]]
    if rope:
        in_specs += [pl.BlockSpec((TS, 256), lambda g, i: (i, 0))] * 3
        in_specs += [pl.BlockSpec((TS, 128), lambda g, i: (i, 0))] * 3
        args += list(rope_tabs)
    out_specs = [tok(256), tok(128), tok(128), scan, scan, tok(256), tok(512), tok(512), tok(256)]
    sds = jax.ShapeDtypeStruct
    scan_shape = sds((Bg, 2, L * SUBLANES, LANES), F32)
    out_shape = [sds((B, L, 256), BF16), sds((B, L, 128), BF16), sds((B, L, 128), BF16),
                 scan_shape, scan_shape,
                 sds((B, L, 256), F32), sds((B, L, 512), BF16), sds((B, L, 512), BF16),
                 sds((B, L, 256), BF16)]
    if ctx:
        out_specs += [tok(128), tok(128), tok(128), tok(128)]
        out_shape += [sds((B, L, 128), F32)] * 4
    return pl.pallas_call(
        functools.partial(_in_proj_kernel, rope, ctx),
        grid=(Bg, L // TS),
        in_specs=in_specs, out_specs=out_specs, out_shape=out_shape,
        compiler_params=_params(("parallel", "parallel")),
        name="in_proj_ctx" if ctx else "in_proj_lat",
    )(*args)


def _mla_cache_kernel(ckv_ref, krp_ref, wukv_ref, gmk_ref, km_ref, vm_ref):
    km, vm = _mla_kv(ckv_ref[...], krp_ref[...], wukv_ref[...], gmk_ref[...], None)
    km_ref[...] = km.astype(BF16)
    vm_ref[...] = vm.astype(BF16)


def _mla_cache_kv(cckv, ckr_placed, lw):
    wspec = _layer_spec(lw["layer"])
    B, P, _ = cckv.shape
    tok = lambda c: pl.BlockSpec((None, P, c), lambda b: (b, 0, 0))
    return pl.pallas_call(
        _mla_cache_kernel,
        grid=(B,),
        in_specs=[tok(128), tok(128), wspec((128, 768)), wspec((1, 512))],
        out_specs=[tok(512), tok(256)],
        out_shape=[jax.ShapeDtypeStruct((B, P, 512), BF16), jax.ShapeDtypeStruct((B, P, 256), BF16)],
        compiler_params=_params(("parallel",)),
        name="mla_cache_kv",
    )(cckv, ckr_placed, lw["wukv"], lw["gmk"])


def _sink_softmax(s, sink):
    sink = sink * LOG2_E
    m = jnp.maximum(jnp.max(s, axis=-1, keepdims=True), sink)
    e = jnp.exp2(s - m)
    den = jnp.sum(e, axis=-1, keepdims=True) + jnp.exp2(sink - m)
    return e.astype(BF16), 1.0 / den


def _gqa_tile(qt, keys, vst, sink_lo, sink_hi, mask):
    lane = lax.broadcasted_iota(jnp.int32, (1, LANES), 1)
    lo = lane < A_HEAD_DIM
    zero = jnp.zeros_like(qt)
    ps, invs = [], []
    for qh, sink in ((jnp.where(lo, qt, zero), sink_lo), (jnp.where(lo, zero, qt), sink_hi)):
        s = lax.dot_general(qh, keys, (((1,), (1,)), ((), ())), preferred_element_type=F32)
        if mask is not None:
            s = jnp.where(mask, s, NEG_INF)
        p, inv = _sink_softmax(s, sink)
        ps.append(p)
        invs.append(inv)
    o = jnp.dot(jnp.concatenate(ps, axis=1), vst, preferred_element_type=F32)
    return o * jnp.where(lo, invs[0], invs[1])


def _stack_kv_halves(v):
    lane = lax.broadcasted_iota(jnp.int32, (1, LANES), 1)
    lo = lane < A_HEAD_DIM
    zero = jnp.zeros_like(v)
    return jnp.concatenate([jnp.where(lo, v, zero), jnp.where(lo, zero, v)], axis=0)


def _attn_a_ctx_kernel(sink_ref, q_ref, k_ref, v_ref, o_ref):
    keys = k_ref[...]
    vst = _stack_kv_halves(v_ref[...])
    for t in range(2):
        o = _gqa_tile(q_ref[:, LANES * t:LANES * (t + 1)], keys, vst, sink_ref[t], sink_ref[2 + t], None)
        o_ref[:, LANES * t:LANES * (t + 1)] = o.astype(o_ref.dtype)


def _attn_a_ctx(q, k, v, sink):
    B, L, _ = q.shape
    tok = lambda c: pl.BlockSpec((None, L, c), lambda b: (b, 0, 0))
    return pl.pallas_call(
        _attn_a_ctx_kernel,
        grid=(B,),
        in_specs=[pl.BlockSpec(memory_space=pltpu.SMEM), tok(256), tok(128), tok(128)],
        out_specs=tok(256),
        out_shape=jax.ShapeDtypeStruct((B, L, 256), BF16),
        compiler_params=_params(("parallel",)),
        name="attn_a_ctx",
    )(sink, q, k, v)


def _attn_a_lat_kernel(sink_ref, q_ref, kc_ref, vc_ref, k_ref, v_ref, o_ref):
    i = pl.program_id(1)
    L = k_ref.shape[0]
    n_ctx = kc_ref.shape[0]
    span = 3 * Q_BLOCK
    start = pl.multiple_of(jnp.clip((i - 1) * Q_BLOCK, 0, L - span), Q_BLOCK)
    keys = jnp.concatenate([kc_ref[...].astype(BF16), k_ref[pl.ds(start, span), :]], axis=0)
    vals = jnp.concatenate([vc_ref[...].astype(BF16), v_ref[pl.ds(start, span), :]], axis=0)
    vst = _stack_kv_halves(vals)
    col = lax.broadcasted_iota(jnp.int32, (Q_BLOCK, n_ctx + span), 1)
    row = lax.broadcasted_iota(jnp.int32, (Q_BLOCK, n_ctx + span), 0)
    rel = (col - n_ctx + start) - (row + i * Q_BLOCK)
    mask = (col < n_ctx) | (jnp.abs(rel) <= WINDOW)
    for t in range(2):
        o = _gqa_tile(q_ref[:, LANES * t:LANES * (t + 1)], keys, vst, sink_ref[t], sink_ref[2 + t], mask)
        o_ref[:, LANES * t:LANES * (t + 1)] = o.astype(o_ref.dtype)


def _attn_a_lat(q, k, v, kc, vc, sink):
    B, L, _ = q.shape
    P = kc.shape[1]
    blk = lambda c: pl.BlockSpec((None, Q_BLOCK, c), lambda b, i: (b, i, 0))
    whole = lambda n, c: pl.BlockSpec((None, n, c), lambda b, i: (b, 0, 0))
    return pl.pallas_call(
        _attn_a_lat_kernel,
        grid=(B, L // Q_BLOCK),
        in_specs=[pl.BlockSpec(memory_space=pltpu.SMEM), blk(256), whole(P, 128), whole(P, 128),
                  whole(L, 128), whole(L, 128)],
        out_specs=blk(256),
        out_shape=jax.ShapeDtypeStruct((B, L, 256), BF16),
        compiler_params=_params(("parallel", "parallel")),
        name="attn_a_lat",
    )(sink, q, kc, vc, k, v)


def _mla_attend(q, key_parts, val_parts):
    vals = val_parts[0] if len(val_parts) == 1 else jnp.concatenate(val_parts, axis=0)
    lane = lax.broadcasted_iota(jnp.int32, (1, MLA_HEADS * MLA_V), 1)
    zero = jnp.zeros_like(vals)
    vst = jnp.concatenate([jnp.where((lane >= MLA_V * h) & (lane < MLA_V * (h + 1)), vals, zero)
                           for h in range(MLA_HEADS)], axis=0)
    ps = []
    inv = jnp.zeros((q.shape[0], MLA_HEADS * MLA_V), F32)
    for h in range(MLA_HEADS):
        qh = q[:, LANES * h:LANES * (h + 1)]
        ss = [lax.dot_general(qh, kp[:, LANES * h:LANES * (h + 1)], (((1,), (1,)), ((), ())),
                              preferred_element_type=F32) for kp in key_parts]
        m = functools.reduce(jnp.maximum, [jnp.max(s, axis=-1, keepdims=True) for s in ss])
        es = [jnp.exp2(s - m) for s in ss]
        den = functools.reduce(lambda a, b: a + b, [jnp.sum(e, axis=-1, keepdims=True) for e in es])
        ps += [e.astype(BF16) for e in es]
        inv = jnp.where((lane >= MLA_V * h) & (lane < MLA_V * (h + 1)), 1.0 / den, inv)
    return jnp.dot(jnp.concatenate(ps, axis=1), vst, preferred_element_type=F32) * inv


def _mla_ctx_kernel(q_ref, k_ref, v_ref, o_ref):
    o_ref[...] = _mla_attend(q_ref[...], [k_ref[...]], [v_ref[...]]).astype(o_ref.dtype)


def _mla_ctx(q, k, v):
    B, L, _ = q.shape
    tok = lambda c: pl.BlockSpec((None, L, c), lambda b: (b, 0, 0))
    return pl.pallas_call(
        _mla_ctx_kernel,
        grid=(B,),
        in_specs=[tok(512), tok(512), tok(256)],
        out_specs=tok(256),
        out_shape=jax.ShapeDtypeStruct((B, L, 256), BF16),
        compiler_params=_params(("parallel",)),
        name="mla_ctx",
    )(q, k, v)


def _mla_lat_kernel(q_ref, kc_ref, vc_ref, k_ref, v_ref, o_ref):
    o = _mla_attend(q_ref[...], [kc_ref[...], k_ref[...]], [vc_ref[...], v_ref[...]])
    o_ref[...] = o.astype(o_ref.dtype)


def _mla_lat(q, k, v, kc, vc):
    B, L, _ = q.shape
    P = kc.shape[1]
    TQ = MLA_Q_TILE
    blk = lambda c: pl.BlockSpec((None, TQ, c), lambda b, i: (b, i, 0))
    whole = lambda n, c: pl.BlockSpec((None, n, c), lambda b, i: (b, 0, 0))
    return pl.pallas_call(
        _mla_lat_kernel,
        grid=(B, L // TQ),
        in_specs=[blk(512), whole(P, 512), whole(P, 256), whole(L, 512), whole(L, 256)],
        out_specs=blk(256),
        out_shape=jax.ShapeDtypeStruct((B, L, 256), BF16),
        compiler_params=_params(("parallel", "parallel")),
        name="mla_lat",
    )(q, kc, vc, k, v)


def _join_halves(ref):
    return jnp.concatenate([ref[0], ref[1]], axis=1)


def _split_halves(ref, val):
    ref[0] = val[:, :LANES]
    ref[1] = val[:, LANES:]


def _s5_kernel(uf_ref, ub_ref, bf_ref, bb_ref, a_ref, h0_ref, cf_ref, cb_ref,
               yf_ref, yb_ref, fin_ref, s_ref, st_ref):
    i = pl.program_id(1)
    steps = SCAN_STEPS

    @pl.when(i == 0)
    def _():
        st_ref[...] = h0_ref[...]

    uf = _join_halves(uf_ref).astype(BF16)
    ub = _join_halves(ub_ref).astype(BF16)
    s_ref[0] = jnp.dot(uf, bf_ref[:, :S5_WIDTH], preferred_element_type=F32)
    s_ref[1] = jnp.dot(uf, bf_ref[:, S5_WIDTH:], preferred_element_type=F32)
    s_ref[2] = jnp.dot(ub, bb_ref[:, :S5_WIDTH], preferred_element_type=F32)
    s_ref[3] = jnp.dot(ub, bb_ref[:, S5_WIDTH:], preferred_element_type=F32)

    lane_chunk = 2 * LANES
    for c in range(S5_WIDTH // lane_chunk):
        sl = slice(lane_chunk * c, lane_chunk * (c + 1))
        arf, aif, arb, aib = (a_ref[n, :, sl] for n in range(4))

        def body(j, carry):
            hrf, hif, hrb, hib = carry
            rf = pl.multiple_of(j * SUBLANES, SUBLANES)
            rb = pl.multiple_of((steps - 1 - j) * SUBLANES, SUBLANES)
            nrf = arf * hrf - aif * hif + s_ref[0, pl.ds(rf, SUBLANES), sl]
            nif = arf * hif + aif * hrf + s_ref[1, pl.ds(rf, SUBLANES), sl]
            nrb = arb * hrb - aib * hib + s_ref[2, pl.ds(rb, SUBLANES), sl]
            nib = arb * hib + aib * hrb + s_ref[3, pl.ds(rb, SUBLANES), sl]
            s_ref[0, pl.ds(rf, SUBLANES), sl] = nrf
            s_ref[1, pl.ds(rf, SUBLANES), sl] = nif
            s_ref[2, pl.ds(rb, SUBLANES), sl] = nrb
            s_ref[3, pl.ds(rb, SUBLANES), sl] = nib
            return nrf, nif, nrb, nib

        fin = lax.fori_loop(0, steps, body, tuple(st_ref[n, :, sl] for n in range(4)), unroll=4)
        for n in range(4):
            st_ref[n, :, sl] = fin[n]

    hf = jnp.concatenate([s_ref[0].astype(BF16), s_ref[1].astype(BF16)], axis=1)
    _split_halves(yf_ref, jnp.dot(hf, cf_ref[...], preferred_element_type=F32))
    hb = jnp.concatenate([s_ref[2].astype(BF16), s_ref[3].astype(BF16)], axis=1)
    _split_halves(yb_ref, jnp.dot(hb, cb_ref[...], preferred_element_type=F32))
    fin_ref[...] = st_ref[...]


def _s5_scan(u_rows, lw, h0):
    wspec = _layer_spec(lw["layer"])
    Bg, _, rows, _ = u_rows.shape
    R = SCAN_STEPS * SUBLANES
    n = rows // R
    fwd = pl.BlockSpec((None, 2, R, LANES), lambda g, i: (g, 0, i, 0))
    bwd = pl.BlockSpec((None, 2, R, LANES), lambda g, i: (g, 0, n - 1 - i, 0))
    st = pl.BlockSpec((None, 4, SUBLANES, S5_WIDTH), lambda g, i: (g, 0, 0, 0))
    return pl.pallas_call(
        _s5_kernel,
        grid=(Bg, n),
        in_specs=[fwd, bwd, wspec((256, 2 * S5_WIDTH)), wspec((256, 2 * S5_WIDTH)),
                  wspec((4, SUBLANES, S5_WIDTH)), st, wspec((2 * S5_WIDTH, 256)), wspec((2 * S5_WIDTH, 256))],
        out_specs=[fwd, bwd, st],
        out_shape=[jax.ShapeDtypeStruct(u_rows.shape, F32), jax.ShapeDtypeStruct(u_rows.shape, F32),
                   jax.ShapeDtypeStruct((Bg, 4, SUBLANES, S5_WIDTH), F32)],
        scratch_shapes=[pltpu.VMEM((4, R, S5_WIDTH), F32), pltpu.VMEM((4, SUBLANES, S5_WIDTH), F32)],
        compiler_params=_params(("parallel", "arbitrary")),
        name="s5_scan",
    )(u_rows, u_rows, lw["s5_bf"], lw["s5_bb"], lw["s5_a"], h0, lw["s5_cf"], lw["s5_cb"])


def _lru_gates(x_ref, pre_ref, post_ref, has_pre, has_post, cw_ref, cb_ref, w_ref, b_ref, sp_ref, a_ref, h_ref):
    R = x_ref.shape[1]
    pre = jnp.where(has_pre, _join_halves(pre_ref), 0.0)
    post = jnp.where(has_post, _join_halves(post_ref), 0.0)
    xp = jnp.concatenate([pre, _join_halves(x_ref), post], axis=0)
    xc = cb_ref[...]
    for t in range(LRU_CONV):
        xc = xc + xp[SUBLANES * t:SUBLANES * t + R] * cw_ref[t:t + 1, :]
    g = jnp.dot(xc.astype(BF16), w_ref[...], preferred_element_type=F32) + b_ref[...]
    r = jax.nn.sigmoid(g[:, :LRU_WIDTH])
    ig = jax.nn.sigmoid(g[:, LRU_WIDTH:])
    log_a = (-LRU_C) * r * sp_ref[...]
    a = jnp.exp(log_a)
    a_ref[...] = a
    h_ref[...] = jnp.sqrt(1.0 - a * a) * (ig * xc)


def _lru_kernel(xf_ref, xfp_ref, xfn_ref, xb_ref, xbp_ref, xbn_ref, cw_ref, cb_ref, wf_ref, wb_ref,
                bf_ref, bb_ref, spf_ref, spb_ref, h0_ref, of_ref, ob_ref, fin_ref,
                af_ref, ab_ref, hf_ref, hb_ref, st_ref):
    i = pl.program_id(1)
    n = pl.num_programs(1)
    steps = SCAN_STEPS

    @pl.when(i == 0)
    def _():
        st_ref[...] = h0_ref[...]

    _lru_gates(xf_ref, xfp_ref, xfn_ref, i > 0, i < n - 1, cw_ref, cb_ref, wf_ref, bf_ref, spf_ref, af_ref, hf_ref)
    _lru_gates(xb_ref, xbp_ref, xbn_ref, i < n - 1, i > 0, cw_ref, cb_ref, wb_ref, bb_ref, spb_ref, ab_ref, hb_ref)

    def body(j, carry):
        hf, hb = carry
        rf = pl.multiple_of(j * SUBLANES, SUBLANES)
        rb = pl.multiple_of((steps - 1 - j) * SUBLANES, SUBLANES)
        nf = af_ref[pl.ds(rf, SUBLANES), :] * hf + hf_ref[pl.ds(rf, SUBLANES), :]
        nb = ab_ref[pl.ds(rb, SUBLANES), :] * hb + hb_ref[pl.ds(rb, SUBLANES), :]
        hf_ref[pl.ds(rf, SUBLANES), :] = nf
        hb_ref[pl.ds(rb, SUBLANES), :] = nb
        return nf, nb

    ff, fb = lax.fori_loop(0, steps, body, (st_ref[0], st_ref[1]), unroll=8)
    st_ref[0] = ff
    st_ref[1] = fb
    fin_ref[...] = st_ref[...]
    _split_halves(of_ref, hf_ref[...])
    _split_halves(ob_ref, hb_ref[...])


def _lru_scan(x_rows, lw, h0):
    wspec = _layer_spec(lw["layer"])
    Bg, _, rows, _ = x_rows.shape
    R = SCAN_STEPS * SUBLANES
    n = rows // R
    pre_rows = 2 * SUBLANES
    fwd = lambda g, i: i
    bwd = lambda g, i: n - 1 - i
    blk = lambda m: pl.BlockSpec((None, 2, R, LANES), lambda g, i: (g, 0, m(g, i), 0))
    pre = lambda m: pl.BlockSpec(
        (None, 2, pre_rows, LANES), lambda g, i: (g, 0, jnp.maximum(m(g, i) * (R // pre_rows) - 1, 0), 0))
    post = lambda m: pl.BlockSpec(
        (None, 2, SUBLANES, LANES),
        lambda g, i: (g, 0, jnp.minimum((m(g, i) + 1) * (R // SUBLANES), rows // SUBLANES - 1), 0))
    st = pl.BlockSpec((None, 2, SUBLANES, 256), lambda g, i: (g, 0, 0, 0))
    return pl.pallas_call(
        _lru_kernel,
        grid=(Bg, n),
        in_specs=[blk(fwd), pre(fwd), post(fwd), blk(bwd), pre(bwd), post(bwd),
                  wspec((LRU_CONV, 256)), wspec((1, 256)), wspec((256, 512)), wspec((256, 512)),
                  wspec((1, 512)), wspec((1, 512)), wspec((1, 256)), wspec((1, 256)), st],
        out_specs=[blk(fwd), blk(bwd), st],
        out_shape=[jax.ShapeDtypeStruct(x_rows.shape, F32), jax.ShapeDtypeStruct(x_rows.shape, F32),
                   jax.ShapeDtypeStruct((Bg, 2, SUBLANES, 256), F32)],
        scratch_shapes=[pltpu.VMEM((R, 256), F32), pltpu.VMEM((R, 256), F32), pltpu.VMEM((R, 256), F32),
                        pltpu.VMEM((R, 256), F32), pltpu.VMEM((2, SUBLANES, 256), F32)],
        compiler_params=_params(("parallel", "arbitrary")),
        name="lru_scan",
    )(x_rows, x_rows, x_rows, x_rows, x_rows, x_rows, lw["lru_cw"], lw["lru_cb"], lw["lru_wf"], lw["lru_wb"],
      lw["lru_bf"], lw["lru_bb"], lw["lru_spf"], lw["lru_spb"], h0)


def _merge_kernel(x_ref, sc_ref, sh_ref, gt_ref, g1_ref, wg_ref, oa_ref, u_ref, yf_ref, yb_ref, d_ref,
                  wglu_ref, hf_ref, hb_ref, lg_ref, od_ref, wb_ref, wo_ref, o_ref):
    steps = x_ref.shape[1]
    x = _group_rows(x_ref)
    h = _modnorm(x, g1_ref[...], _mod_rows(sc_ref, steps), _mod_rows(sh_ref, steps)).astype(BF16)

    yb5 = jax.nn.gelu(d_ref[...] * _load_time_major(u_ref) + _load_time_major(yf_ref) + _load_time_major(yb_ref))
    gv = jnp.dot(yb5.astype(BF16), wglu_ref[...], preferred_element_type=F32)
    o_b = gv[:, :BRANCH_W] * jax.nn.sigmoid(gv[:, BRANCH_W:])
    o_c = (_load_time_major(hf_ref) + _load_time_major(hb_ref)) * jax.nn.gelu(_group_rows(lg_ref))
    branches = (_group_rows(oa_ref), o_b.astype(BF16), o_c.astype(BF16), _group_rows(od_ref))

    acc = jnp.zeros(x.shape, F32)
    for n in range(N_BRANCH):
        gate = jnp.dot(h, wg_ref[:, D_MODEL * n:D_MODEL * (n + 1)], preferred_element_type=F32)
        proj = jnp.dot(branches[n], wb_ref[n], preferred_element_type=F32)
        acc = acc + jax.nn.sigmoid(gate) * proj
    out = jnp.dot(acc.astype(BF16), wo_ref[...], preferred_element_type=F32)
    _store_group(o_ref, x + _mod_rows(gt_ref, steps) * out)


def _merge(x, sc, sh, gt, lw, o_a, s5u_t, yf_t, yb_t, hf_t, hb_t, lg, o_d):
    wspec = _layer_spec(lw["layer"])
    B, L, _ = x.shape
    tok, scan, mod = _group_specs(sc.shape[0] > 1)
    return pl.pallas_call(
        _merge_kernel,
        grid=(B // SUBLANES, L // TOK_STEPS),
        in_specs=[tok(D_MODEL), mod, mod, mod, wspec((1, D_MODEL)), wspec((D_MODEL, N_BRANCH * D_MODEL)),
                  tok(256), scan, scan, scan, wspec((1, 256)), wspec((256, 512)),
                  scan, scan, tok(256), tok(256), wspec((N_BRANCH, BRANCH_W, D_MODEL)),
                  wspec((D_MODEL, D_MODEL))],
        out_specs=tok(D_MODEL),
        out_shape=jax.ShapeDtypeStruct((B, L, D_MODEL), F32),
        compiler_params=_params(("parallel", "parallel")),
        name="merge",
    )(x, sc, sh, gt, lw["g1"], lw["wg"], o_a, s5u_t, yf_t, yb_t, lw["s5_d"], lw["wglu"],
      hf_t, hb_t, lg, o_d, lw["wb"], lw["wo"])


def _first_index(values, target):
    idx = jnp.full_like(target, float(len(values) - 1))
    for n in range(len(values) - 2, -1, -1):
        idx = jnp.where(values[n] == target, float(n), idx)
    return idx


def _list_max(values):
    return functools.reduce(jnp.maximum, values)


def _moe_sort_kernel(x_ref, sc_ref, sh_ref, g2_ref, wrh_ref, wrl_ref, br_ref, tri_ref,
                     hs_ref, gs_ref, dcol_ref, meta_ref):
    for s in range(x_ref.shape[0] // MOE_SORT_TILE):
        tok = pl.ds(s * MOE_SORT_TILE, MOE_SORT_TILE)
        srt = pl.ds(s * MOE_SORTED_ROWS, MOE_SORTED_ROWS)
        _moe_sort_tile(x_ref.at[tok], sc_ref, sh_ref, g2_ref, wrh_ref, wrl_ref, br_ref, tri_ref,
                       hs_ref.at[srt], gs_ref.at[srt], dcol_ref.at[tok], meta_ref.at[s])


def _moe_sort_tile(x_ref, sc_ref, sh_ref, g2_ref, wrh_ref, wrl_ref, br_ref, tri_ref,
                   hs_ref, gs_ref, dcol_ref, meta_ref):
    T = MOE_SORT_TILE
    h = _modnorm(x_ref[...], g2_ref[...], sc_ref[...], sh_ref[...])
    hh = h.astype(BF16)
    hl = (h - hh.astype(F32)).astype(BF16)
    nt = (((1,), (1,)), ((), ()))
    logits = (jnp.dot(hh, wrh_ref[...], preferred_element_type=F32)
              + jnp.dot(hl, wrh_ref[...], preferred_element_type=F32)
              + jnp.dot(hh, wrl_ref[...], preferred_element_type=F32)) + br_ref[...]
    lt = logits.T
    gl = [lt[g:g + 1, :] for g in range(N_GROUPS)]
    gmax = _list_max(gl)
    g_idx = _first_index(gl, gmax)
    pg = 1.0 / sum(jnp.exp(v - gmax) for v in gl)
    hot = [g_idx == float(g) for g in range(N_GROUPS)]
    el = []
    for e in range(EXPERTS_PER_GROUP):
        v = jnp.zeros_like(gmax)
        for g in range(N_GROUPS):
            r = N_GROUPS + EXPERTS_PER_GROUP * g + e
            v = jnp.where(hot[g], lt[r:r + 1, :], v)
        el.append(v)
    emax = _list_max(el)
    ee = [jnp.exp(v - emax) for v in el]
    esum = sum(ee)
    pe = [v / esum for v in ee]
    v1 = _list_max(pe)
    i1 = _first_index(pe, v1)
    pe2 = [jnp.where(i1 == float(e), -1.0, pe[e]) for e in range(EXPERTS_PER_GROUP)]
    v2 = _list_max(pe2)
    i2 = _first_index(pe2, v2)
    tot = v1 + v2
    w = [jnp.where(i1 == float(e), pg * v1 / tot, jnp.where(i2 == float(e), pg * v2 / tot, 0.0))
         for e in range(EXPERTS_PER_GROUP)]

    zero_row = jnp.zeros_like(gmax)
    g8 = jnp.concatenate([jnp.where(hot[g], 1.0, 0.0) for g in range(N_GROUPS)] + [zero_row] * 4, axis=0)
    cum = jnp.dot(g8.astype(BF16), tri_ref[...], preferred_element_type=F32)
    off = jnp.zeros((1, 1), F32)
    dest = zero_row
    counts = []
    for g in range(N_GROUPS):
        cnt = cum[g:g + 1, T - 1:T]
        padded = jnp.floor((cnt + (MOE_ROW_ALIGN - 1.0)) * (1.0 / MOE_ROW_ALIGN)) * MOE_ROW_ALIGN
        dest = jnp.where(hot[g], off + cum[g:g + 1, :] - 1.0, dest)
        off = off + padded
        counts.append(padded)
    rows = lax.broadcasted_iota(jnp.int32, (MOE_SORTED_ROWS, T), 0).astype(F32)
    perm = jnp.where(rows == dest, 1.0, 0.0).astype(BF16)
    hs_ref[...] = jnp.dot(perm, hh, preferred_element_type=F32).astype(BF16)
    gates = jnp.concatenate(w + [jnp.zeros((LANES - EXPERTS_PER_GROUP, T), F32)], axis=0)
    ghi = gates.astype(BF16)
    glo = (gates - ghi.astype(F32)).astype(BF16)
    gs_ref[...] = (lax.dot_general(perm, ghi, nt, preferred_element_type=F32)
                   + lax.dot_general(perm, glo, nt, preferred_element_type=F32))
    r_i = lax.broadcasted_iota(jnp.int32, (T, T), 0)
    c_i = lax.broadcasted_iota(jnp.int32, (T, T), 1)
    dcol = jnp.sum(jnp.where(r_i == c_i, jnp.broadcast_to(dest, (T, T)), 0.0), axis=1, keepdims=True)
    dcol_ref[...] = jnp.broadcast_to(dcol, (T, LANES))
    meta_ref[...] = jnp.concatenate([jnp.broadcast_to(c, (1, LANES)) for c in counts]
                                    + [jnp.zeros((SUBLANES - N_GROUPS, LANES), F32)], axis=0)


def _moe_sort(xf, sc, sh, lw, tiles_per_mod):
    wspec = _layer_spec(lw["layer"])
    N = xf.shape[0]
    T = MOE_SORT_TILE
    n_tiles = N // T
    sub = MOE_SORT_SUBTILES
    per_b = sc.shape[0] > 1
    mod = pl.BlockSpec((None, 1, D_MODEL), lambda t: (t * sub // tiles_per_mod if per_b else 0, 0, 0))
    sds = jax.ShapeDtypeStruct
    return pl.pallas_call(
        _moe_sort_kernel,
        grid=(n_tiles // sub,),
        in_specs=[pl.BlockSpec((sub * T, D_MODEL), lambda t: (t, 0)), mod, mod, wspec((1, D_MODEL)),
                  wspec((D_MODEL, LANES)), wspec((D_MODEL, LANES)), wspec((1, LANES)), wspec((T, T))],
        out_specs=[pl.BlockSpec((sub * MOE_SORTED_ROWS, D_MODEL), lambda t: (t, 0)),
                   pl.BlockSpec((sub * MOE_SORTED_ROWS, LANES), lambda t: (t, 0)),
                   pl.BlockSpec((sub * T, LANES), lambda t: (t, 0)),
                   pl.BlockSpec((sub, SUBLANES, LANES), lambda t: (t, 0, 0))],
        out_shape=[sds((n_tiles * MOE_SORTED_ROWS, D_MODEL), BF16), sds((n_tiles * MOE_SORTED_ROWS, LANES), F32),
                   sds((N, LANES), F32), sds((n_tiles, SUBLANES, LANES), F32)],
        compiler_params=_params(("parallel",)),
        name="moe_sort",
    )(xf, sc, sh, lw["g2"], lw["wr_hi"], lw["wr_lo"], lw["br"], lw["tri"])


def _moe_expert_kernel(cnt_ref, hs_ref, gs_ref, wga_ref, wup_ref, wdn_ref, ys_ref, ch_ref, cg_ref, cy_ref):
    j = pl.program_id(0)
    g = pl.program_id(1)
    piece = MOE_ROW_ALIGN

    @pl.when((j == 0) & (g == 0))
    def _():
        ch_ref[...] = jnp.zeros_like(ch_ref)
        cg_ref[...] = jnp.zeros_like(cg_ref)

    @pl.when(g == 0)
    def _():
        ys_ref[...] = jnp.zeros_like(ys_ref)

    def for_each_piece(move):
        packed = jnp.int32(0)
        for t in range(MOE_BLOCK_TILES):
            base = (j * MOE_BLOCK_TILES + t) * N_GROUPS
            start = jnp.int32(t * MOE_SORTED_ROWS)
            for g2 in range(N_GROUPS - 1):
                start = start + jnp.where(g2 < g, cnt_ref[base + g2], 0)
            n = cnt_ref[base + g]

            def body(k, carry, start=start, packed=packed):
                move(pl.multiple_of(start + k * piece, piece), pl.multiple_of(packed + k * piece, piece))
                return carry

            lax.fori_loop(0, lax.shift_right_logical(n, 4), body, 0)
            packed = packed + n
        return packed

    def pack(src, dst):
        ch_ref[pl.ds(dst, piece), :] = hs_ref[pl.ds(src, piece), :]
        cg_ref[pl.ds(dst, piece), :] = gs_ref[pl.ds(src, piece), :]

    rows = for_each_piece(pack)

    def experts(r, size):
        h = ch_ref[pl.ds(r, size), :]
        acts = []
        for e in range(EXPERTS_PER_GROUP):
            a = jnp.dot(h, wga_ref[e], preferred_element_type=F32)
            u = jnp.dot(h, wup_ref[e], preferred_element_type=F32)
            acts.append(((a * jax.nn.sigmoid(a)) * u * cg_ref[pl.ds(r, size), e:e + 1]).astype(BF16))
        y = jnp.dot(jnp.concatenate(acts, axis=1), wdn_ref[...], preferred_element_type=F32)
        cy_ref[pl.ds(r, size), :] = y.astype(BF16)

    big, mid, small = MOE_CHUNK, MOE_CHUNK // 2, MOE_CHUNK // 4
    n_big = lax.shift_right_logical(rows, big.bit_length() - 1)

    def chunk(c, carry):
        experts(pl.multiple_of(c * big, big), big)
        return carry

    lax.fori_loop(0, n_big, chunk, 0)
    tail = pl.multiple_of(n_big * big, big)
    rem = rows - tail
    pl.when(rem > mid + small)(lambda: experts(tail, big))
    pl.when((rem > small) & (rem <= mid + small))(lambda: experts(tail, mid))
    pl.when((rem > mid) & (rem <= mid + small))(lambda: experts(pl.multiple_of(tail + mid, small), small))
    pl.when((rem > 0) & (rem <= small))(lambda: experts(tail, small))

    def unpack(src, dst):
        ys_ref[pl.ds(src, piece), :] = cy_ref[pl.ds(dst, piece), :]

    for_each_piece(unpack)


def _moe_experts(cnt, hs, gs, lw):
    rows = MOE_BLOCK_TILES * MOE_SORTED_ROWS
    cap = MOE_BLOCK_TILES * MOE_SORT_TILE
    blk = lambda c: pl.BlockSpec((rows, c), lambda j, g, cnt: (j, 0))
    layer = lw["layer"]
    wspec = lambda r, c: pl.BlockSpec((None, EXPERTS_PER_GROUP, r, c), lambda j, g, cnt: (layer, g, 0, 0))
    return pl.pallas_call(
        _moe_expert_kernel,
        grid_spec=pltpu.PrefetchScalarGridSpec(
            num_scalar_prefetch=1, grid=(hs.shape[0] // rows, N_GROUPS),
            in_specs=[blk(D_MODEL), blk(LANES), wspec(D_MODEL, EXPERT_FF), wspec(D_MODEL, EXPERT_FF),
                      pl.BlockSpec((None, EXPERTS_PER_GROUP * EXPERT_FF, D_MODEL),
                                   lambda j, g, cnt: (layer, g, 0))],
            out_specs=blk(D_MODEL),
            scratch_shapes=[pltpu.VMEM((cap, D_MODEL), BF16), pltpu.VMEM((cap, LANES), F32),
                            pltpu.VMEM((cap, D_MODEL), BF16)]),
        out_shape=jax.ShapeDtypeStruct(hs.shape, BF16),
        compiler_params=_params(("arbitrary", "arbitrary")),
        name="moe_experts",
    )(cnt, hs, gs, lw["wga"], lw["wup"], lw["wdn"])


def _moe_unsort_kernel(x_ref, gt_ref, ys_ref, dcol_ref, o_ref):
    cols = lax.broadcasted_iota(jnp.int32, (MOE_SORT_TILE, MOE_SORTED_ROWS), 1).astype(F32)
    perm_t = jnp.where(cols == dcol_ref[:, 0:1], 1.0, 0.0).astype(BF16)
    y = jnp.dot(perm_t, ys_ref[...], preferred_element_type=F32)
    o_ref[...] = x_ref[...] + gt_ref[...] * y


def _moe_unsort(xf, gt, ys, dcol, tiles_per_mod):
    N = xf.shape[0]
    T = MOE_SORT_TILE
    per_b = gt.shape[0] > 1
    mod = pl.BlockSpec((None, 1, D_MODEL), lambda t: (t // tiles_per_mod if per_b else 0, 0, 0))
    tok = pl.BlockSpec((T, D_MODEL), lambda t: (t, 0))
    return pl.pallas_call(
        _moe_unsort_kernel,
        grid=(N // T,),
        in_specs=[tok, mod, pl.BlockSpec((MOE_SORTED_ROWS, D_MODEL), lambda t: (t, 0)),
                  pl.BlockSpec((T, LANES), lambda t: (t, 0))],
        out_specs=tok,
        out_shape=jax.ShapeDtypeStruct((N, D_MODEL), F32),
        compiler_params=_params(("parallel",)),
        name="moe_unsort",
    )(xf, gt, ys, dcol)


def _moe(x, sc, sh, gt, lw):
    B, L, _ = x.shape
    N = B * L
    T = MOE_SORT_TILE
    xf = x.reshape(N, D_MODEL)
    hs, gs, dcol, meta = _moe_sort(xf, sc, sh, lw, L // T)

    cnt = meta[:, :N_GROUPS, 0].astype(jnp.int32).reshape(-1)
    ys = _moe_experts(cnt, hs, gs, lw)
    out = _moe_unsort(xf, gt, ys, dcol, L // T)
    return out.reshape(B, L, D_MODEL)


def _block_diag(blocks):
    n, r, c = blocks.shape
    eye = jnp.eye(n, dtype=blocks.dtype)
    return jnp.einsum("nrc,nm->nrmc", blocks, eye).reshape(n * r, n * c)


def _rope_tables(seq_len, dim, width, offset):
    rows = seq_len // GRID_W
    r, col = jnp.meshgrid(jnp.arange(rows), jnp.arange(GRID_W), indexing="ij")
    r = r.reshape(-1).astype(F32)
    col = col.reshape(-1).astype(F32)
    quarter = dim // 4
    freqs = ROPE_BASE ** (-jnp.arange(quarter, dtype=F32) / quarter)
    ang_r = r[:, None] * freqs
    ang_c = col[:, None] * freqs
    zero = jnp.zeros_like(ang_r)
    cos = jnp.cos(jnp.concatenate([ang_r, ang_r, ang_c, ang_c], axis=-1))
    sin_next = jnp.concatenate([-jnp.sin(ang_r), zero, -jnp.sin(ang_c), zero], axis=-1)
    sin_prev = jnp.concatenate([zero, jnp.sin(ang_r), zero, jnp.sin(ang_c)], axis=-1)

    def place(t, fill):
        return jnp.pad(t, ((0, 0), (offset, width - offset - dim)), constant_values=fill)

    return place(cos, 1.0), place(sin_next, 0.0), place(sin_prev, 0.0)


def _layer_weights(l, w):
    lw = {}
    w_in = w["w_in"][l]
    q_cols = w_in[:, 0:256].reshape(D_MODEL, A_HEADS, A_HEAD_DIM)[:, HEAD_ORDER].reshape(D_MODEL, 256)
    kr_cols = jnp.pad(w_in[:, 1664:1696], ((0, 0), (MLA_NOPE, LANES - MLA_QK)))
    lw["wz"] = jnp.concatenate([q_cols, w_in[:, 256:1664], kr_cols], axis=1).astype(BF16)
    lw["wg"] = w_in[:, 1696:].astype(BF16)
    lw["g1"] = w["norm1_g"][l][None]
    lw["g2"] = w["norm2_g"][l][None]
    lw["gq"] = jnp.tile(w["a_qnorm_g"][l], A_HEADS)[None] * (A_HEAD_DIM ** -0.5 * LOG2_E)
    lw["gk"] = jnp.tile(w["a_knorm_g"][l], A_KV_HEADS)[None]
    lw["sink"] = w["a_sink"][l]
    lw["qlg"] = w["mla_q_lat_norm"][l][None]
    wuq = w["mla_w_uq"][l].reshape(MLA_Q_LORA, MLA_HEADS, MLA_QK)
    lw["wuq"] = jnp.pad(wuq, ((0, 0), (0, 0), (0, LANES - MLA_QK))).reshape(MLA_Q_LORA, MLA_HEADS * LANES).astype(BF16)
    pad_g = lambda g: jnp.tile(jnp.pad(g, (0, LANES - MLA_QK)), MLA_HEADS)[None]
    lw["gmq"] = pad_g(w["mla_qnorm_g"][l]) * (MLA_QK ** -0.5 * LOG2_E)
    lw["gmk"] = pad_g(w["mla_knorm_g"][l])
    lw["kvg"] = w["mla_kv_norm"][l][None]
    wukv = w["mla_w_ukv"][l].reshape(MLA_KV_LORA, MLA_HEADS, MLA_NOPE + MLA_V)
    wk = jnp.pad(wukv[:, :, :MLA_NOPE], ((0, 0), (0, 0), (0, LANES - MLA_NOPE))).reshape(MLA_KV_LORA, MLA_HEADS * LANES)
    wv = wukv[:, :, MLA_NOPE:].reshape(MLA_KV_LORA, MLA_HEADS * MLA_V)
    lw["wukv"] = jnp.concatenate([wk, wv], axis=1).astype(BF16)

    lre = w["s5_lambda_re"][l]
    lim = w["s5_lambda_im"][l]
    dt = jnp.exp(w["s5_log_step"][l])[:, :, None]
    mag = jnp.exp(lre * dt)
    ar, ai = mag * jnp.cos(lim * dt), mag * jnp.sin(lim * dt)
    den = lre * lre + lim * lim
    fr = ((ar - 1.0) * lre + ai * lim) / den
    fi = (ai * lre - (ar - 1.0) * lim) / den
    br, bi = w["s5_b_re"][l], w["s5_b_im"][l]
    bbr = fr[..., None] * br - fi[..., None] * bi
    bbi = fr[..., None] * bi + fi[..., None] * br
    in_map = lambda d: jnp.concatenate(
        [_block_diag(jnp.swapaxes(bbr[d], 1, 2)), _block_diag(jnp.swapaxes(bbi[d], 1, 2))], axis=1).astype(BF16)
    out_map = lambda d: jnp.concatenate(
        [_block_diag(jnp.swapaxes(w["s5_c_re"][l][d], 1, 2)),
         -_block_diag(jnp.swapaxes(w["s5_c_im"][l][d], 1, 2))], axis=0).astype(BF16)
    lw["s5_bf"], lw["s5_bb"] = in_map(0), in_map(1)
    lw["s5_cf"], lw["s5_cb"] = out_map(0), out_map(1)
    coef = jnp.stack([ar[0].reshape(-1), ai[0].reshape(-1), ar[1].reshape(-1), ai[1].reshape(-1)])
    lw["s5_a"] = jnp.broadcast_to(coef[:, None, :], (4, SUBLANES, S5_WIDTH))
    lw["s5_d"] = w["s5_d"][l][None]
    lw["wglu"] = w["s5_w_glu"][l].astype(BF16)

    lw["lru_cw"] = w["lru_conv_w"][l]
    lw["lru_cb"] = w["lru_conv_b"][l][None]
    gate_w = lambda d: jnp.concatenate(
        [_block_diag(w["lru_w_a"][l][d]), _block_diag(w["lru_w_x"][l][d])], axis=1).astype(BF16)
    gate_b = lambda d: jnp.concatenate([w["lru_b_a"][l][d], w["lru_b_x"][l][d]])[None]
    lw["lru_wf"], lw["lru_wb"] = gate_w(0), gate_w(1)
    lw["lru_bf"], lw["lru_bb"] = gate_b(0), gate_b(1)
    sp = jax.nn.softplus(-w["lru_lambda"][l])
    lw["lru_spf"], lw["lru_spb"] = sp[0][None], sp[1][None]

    wb = w["w_branch"][l]
    wb0 = wb[0].reshape(A_HEADS, A_HEAD_DIM, D_MODEL)[HEAD_ORDER].reshape(BRANCH_W, D_MODEL)
    lw["wb"] = jnp.concatenate([wb0[None], wb[1:]], axis=0).astype(BF16)
    lw["wo"] = w["w_out"][l].astype(BF16)

    wr = jnp.pad(jnp.concatenate([w["moe_w_group"][l], w["moe_w_expert"][l]], axis=1),
                 ((0, 0), (0, LANES - N_GROUPS - N_EXPERTS)))
    lw["wr_hi"] = wr.astype(BF16)
    lw["wr_lo"] = (wr - lw["wr_hi"].astype(F32)).astype(BF16)
    lw["br"] = jnp.pad(jnp.concatenate([w["moe_b_group"][l], w["moe_b_expert"][l]]),
                       (0, LANES - N_GROUPS - N_EXPERTS))[None]
    lw["tri"] = jnp.triu(jnp.ones((MOE_SORT_TILE, MOE_SORT_TILE), BF16))
    lw["wga"] = w["moe_w_gate"][l].astype(BF16)
    lw["wup"] = w["moe_w_up"][l].astype(BF16)
    lw["wdn"] = w["moe_w_down"][l].astype(BF16).reshape(N_EXPERTS * EXPERT_FF, D_MODEL)
    return lw


def _rows_to_state(fin, width):
    Bg, n = fin.shape[0], fin.shape[1]
    return jnp.swapaxes(fin, 1, 2).reshape(Bg * SUBLANES, n, width)


def _state_to_rows(state):
    B, n, width = state.shape
    return jnp.swapaxes(state.reshape(B // SUBLANES, SUBLANES, n, width), 1, 2)


def _mix(x, mods, lw, rope_tabs, cache, s5_h0, lru_h0, ctx):
    sh1, sc1, gt1 = mods
    outs = _in_proj(x, sc1, sh1, lw, rope_tabs, ctx)
    q, k, v, s5u_t, lx_t, lg, qm, km, vm = outs[:9]
    if ctx:
        o_a = _attn_a_ctx(q, k, v, lw["sink"][lw["layer"]])
        o_d = _mla_ctx(qm, km, vm)
    else:
        ck, cv, cckv, ckr = cache
        o_a = _attn_a_lat(q, k, v, ck, cv, lw["sink"][lw["layer"]])
        kc, vc = _mla_cache_kv(cckv, ckr, lw)
        o_d = _mla_lat(qm, km, vm, kc, vc)
    yf, yb, s5_fin = _s5_scan(s5u_t, lw, s5_h0)
    hf, hb, lru_fin = _lru_scan(lx_t, lw, lru_h0)
    x = _merge(x, sc1, sh1, gt1, lw, o_a, s5u_t, yf, yb, hf, hb, lg, o_d)
    return x, outs[9:], s5_fin, lru_fin


def kernel(x_prompt, x_sample, cache_attn_k, cache_attn_v, cache_mla_ckv, cache_mla_krope, state_ssm_re, state_ssm_im, state_lru, c, c_ctx, w_mod, b_mod, norm1_g, norm2_g, w_in, a_qnorm_g, a_knorm_g, a_sink, s5_lambda_re, s5_lambda_im, s5_log_step, s5_b_re, s5_b_im, s5_c_re, s5_c_im, s5_d, s5_w_glu, lru_conv_w, lru_conv_b, lru_w_a, lru_b_a, lru_w_x, lru_b_x, lru_lambda, mla_q_lat_norm, mla_w_uq, mla_kv_norm, mla_w_ukv, mla_qnorm_g, mla_knorm_g, w_branch, w_out, moe_w_group, moe_b_group, moe_w_expert, moe_b_expert, moe_w_gate, moe_w_up, moe_w_down):
    w = dict(norm1_g=norm1_g, norm2_g=norm2_g, w_in=w_in, a_qnorm_g=a_qnorm_g, a_knorm_g=a_knorm_g,
             a_sink=a_sink, s5_lambda_re=s5_lambda_re, s5_lambda_im=s5_lambda_im, s5_log_step=s5_log_step,
             s5_b_re=s5_b_re, s5_b_im=s5_b_im, s5_c_re=s5_c_re, s5_c_im=s5_c_im, s5_d=s5_d, s5_w_glu=s5_w_glu,
             lru_conv_w=lru_conv_w, lru_conv_b=lru_conv_b, lru_w_a=lru_w_a, lru_b_a=lru_b_a, lru_w_x=lru_w_x,
             lru_b_x=lru_b_x, lru_lambda=lru_lambda, mla_q_lat_norm=mla_q_lat_norm, mla_w_uq=mla_w_uq,
             mla_kv_norm=mla_kv_norm, mla_w_ukv=mla_w_ukv, mla_qnorm_g=mla_qnorm_g, mla_knorm_g=mla_knorm_g,
             w_branch=w_branch, w_out=w_out, moe_w_group=moe_w_group, moe_b_group=moe_b_group,
             moe_w_expert=moe_w_expert, moe_b_expert=moe_b_expert, moe_w_gate=moe_w_gate, moe_w_up=moe_w_up,
             moe_w_down=moe_w_down)
    B, L, _ = x_prompt.shape
    Bd, Ld, _ = x_sample.shape
    P = cache_attn_k.shape[2]

    cond = jnp.zeros((2 * SUBLANES, D_MODEL), F32).at[:Bd].set(c).at[Bd].set(c_ctx)
    mod = _modulation(cond, w_mod, b_mod[:, None, :])
    mod = mod.reshape(DEPTH, 2 * SUBLANES, 6, D_MODEL)

    rope_a = _rope_tables(Ld, A_HEAD_DIM, A_HEAD_DIM, 0)
    rope_a = tuple(jnp.tile(t, (1, A_HEADS)) for t in rope_a)
    rope_m = _rope_tables(Ld, MLA_ROPE, LANES, MLA_NOPE)

    lw_all = jax.vmap(lambda w1: _layer_weights(0, {name: v[None] for name, v in w1.items()}))(w)

    xp, xs = x_prompt, x_sample
    ak_l, av_l, ckv_l, kr_l, sr_l, si_l, lru_l = [], [], [], [], [], [], []
    for l in range(DEPTH):
        lw = dict(lw_all, layer=l)
        lat_mod = [mod[l, :Bd, n][:, None, :] for n in range(6)]
        ctx_mod = [mod[l, Bd:Bd + 1, n][:, None, :] for n in range(6)]

        zs5 = jnp.zeros((B // SUBLANES, 4, SUBLANES, S5_WIDTH), F32)
        zlru = jnp.zeros((B // SUBLANES, 2, SUBLANES, LRU_WIDTH), F32)
        xp, (k32, v32, ckv_n, krp), s5_fin, lru_fin = _mix(xp, ctx_mod[0:3], lw, None, None, zs5, zlru, True)
        xp = _moe(xp, ctx_mod[4], ctx_mod[3], ctx_mod[5], lw)
        ak_l.append(k32.reshape(B, L, A_KV_HEADS, A_HEAD_DIM))
        av_l.append(v32.reshape(B, L, A_KV_HEADS, A_HEAD_DIM))
        ckv_l.append(ckv_n)
        kr_l.append(krp[:, :, MLA_NOPE:MLA_QK])
        fin = _rows_to_state(s5_fin, S5_WIDTH)
        sr_l.append(fin[:, 0::2].reshape(B, 2, S5_GROUPS, S5_STATE))
        si_l.append(fin[:, 1::2].reshape(B, 2, S5_GROUPS, S5_STATE))
        lru_l.append(_rows_to_state(lru_fin, LRU_WIDTH))

        sre = state_ssm_re[:, l].reshape(Bd, 2, S5_WIDTH)
        sim = state_ssm_im[:, l].reshape(Bd, 2, S5_WIDTH)
        s5_h0 = _state_to_rows(jnp.stack([sre[:, 0], sim[:, 0], sre[:, 1], sim[:, 1]], axis=1))
        lru_h0 = _state_to_rows(state_lru[:, l])
        cache = (cache_attn_k[:, l].reshape(Bd, P, 128), cache_attn_v[:, l].reshape(Bd, P, 128),
                 cache_mla_ckv[:, l],
                 jnp.pad(cache_mla_krope[:, l], ((0, 0), (0, 0), (MLA_NOPE, LANES - MLA_QK))))
        xs, _, _, _ = _mix(xs, lat_mod[0:3], lw, rope_a + rope_m, cache, s5_h0, lru_h0, False)
        xs = _moe(xs, lat_mod[4], lat_mod[3], lat_mod[5], lw)

    stack = lambda ts: jnp.stack(ts, axis=1)
    return (xp, xs, stack(ak_l), stack(av_l), stack(ckv_l), stack(kr_l), stack(sr_l), stack(si_l), stack(lru_l))
```

```python
import functools
import math

import jax
import jax.numpy as jnp
import numpy as np
from jax import lax
from jax.experimental import pallas as pl
from jax.experimental.pallas import tpu as pltpu

F32 = jnp.float32
BF16 = jnp.bfloat16

D_MODEL = 1024
DEPTH = 2
GRID_W = 64
N_BRANCH = 4
BRANCH_W = 256
ROPE_BASE = 10000.0
EPS = 1e-6
NEG_INF = -1e30
LOG2_E = math.log2(math.e)
A_HEADS = 4
A_KV_HEADS = 2
A_HEAD_DIM = 64
WINDOW = 128
Q_BLOCK = 128
S5_GROUP = 16
S5_GROUPS = 16
S5_STATE = 64
S5_WIDTH = S5_GROUPS * S5_STATE
LRU_WIDTH = 256
LRU_BLOCKS = 4
LRU_CONV = 4
LRU_C = 8.0
MLA_HEADS = 4
MLA_Q_LORA = 256
MLA_KV_LORA = 128
MLA_NOPE = 64
MLA_ROPE = 32
MLA_V = 64
MLA_QK = MLA_NOPE + MLA_ROPE
N_GROUPS = 4
EXPERTS_PER_GROUP = 4
N_EXPERTS = 16
EXPERT_FF = 256

LANES = 128
SUBLANES = 8
VMEM_LIMIT = 56 * 1024 * 1024

TOK_TILE = 256
TOK_STEPS = TOK_TILE // SUBLANES
SCAN_STEPS = 128
MOE_SORT_TILE = 256
MOE_ROW_ALIGN = 16
MOE_SORTED_ROWS = MOE_SORT_TILE + LANES
MOE_SORT_SUBTILES = 4
MOE_BLOCK_TILES = 8
MOE_CHUNK = 512
MLA_Q_TILE = 256
ATTN_Q_TILE = 256
Z_COLS = 1792
HEAD_ORDER = np.array((0, 2, 1, 3))


def _params(sem):
    return pltpu.CompilerParams(dimension_semantics=sem, vmem_limit_bytes=VMEM_LIMIT)


def _layer_spec(layer):
    def spec(shape):
        n = len(shape)
        return pl.BlockSpec((None,) + tuple(shape), lambda *_: (layer,) + (0,) * n)
    return spec


def _modnorm(x, g, sc, sh):
    ms = jnp.mean(x * x, axis=-1, keepdims=True)
    return (x * lax.rsqrt(ms + EPS)) * g * (1.0 + sc) + sh


def _rmsnorm(x, g):
    ms = jnp.mean(x * x, axis=-1, keepdims=True)
    return (x * lax.rsqrt(ms + EPS)) * g


def _half_rmsnorm(x):
    lane = lax.broadcasted_iota(jnp.int32, (1, LANES), 1)
    lo = lane < A_HEAD_DIM
    outs = []
    for t in range(x.shape[1] // LANES):
        xt = x[:, LANES * t:LANES * (t + 1)]
        sq = xt * xt
        s_lo = jnp.sum(jnp.where(lo, sq, 0.0), axis=-1, keepdims=True)
        s_hi = jnp.sum(jnp.where(lo, 0.0, sq), axis=-1, keepdims=True)
        inv = jnp.where(lo, lax.rsqrt(s_lo * (1.0 / A_HEAD_DIM) + EPS),
                        lax.rsqrt(s_hi * (1.0 / A_HEAD_DIM) + EPS))
        outs.append(xt * inv)
    return jnp.concatenate(outs, axis=1)


def _tile_rmsnorm(x, n_real):
    outs = []
    for t in range(x.shape[1] // LANES):
        xt = x[:, LANES * t:LANES * (t + 1)]
        ss = jnp.sum(xt * xt, axis=-1, keepdims=True)
        outs.append(xt * lax.rsqrt(ss * (1.0 / n_real) + EPS))
    return jnp.concatenate(outs, axis=1)


def _rope(x, cos, sin_next, sin_prev, quarter):
    width = x.shape[1]
    return (x * cos + pltpu.roll(x, width - quarter, 1) * sin_next
            + pltpu.roll(x, quarter, 1) * sin_prev)


def _mla_kv(ckv_n, kr_placed, wukv, gk, rope_tabs):
    kv = jnp.dot(ckv_n.astype(BF16), wukv, preferred_element_type=F32)
    gain = gk[:, :LANES]
    rotated = None
    if rope_tabs is not None:
        cos, sin_next, sin_prev = rope_tabs
        quarter = MLA_ROPE // 4
        base = kr_placed * gain
        rotated = pltpu.roll(base, LANES - quarter, 1) * sin_next + pltpu.roll(base, quarter, 1) * sin_prev
    heads = []
    for h in range(MLA_HEADS):
        kf = kv[:, LANES * h:LANES * (h + 1)] + kr_placed
        inv = lax.rsqrt(jnp.sum(kf * kf, axis=-1, keepdims=True) * (1.0 / MLA_QK) + EPS)
        kh = kf * gain
        if rotated is not None:
            kh = kh * cos + rotated
        heads.append(kh * inv)
    return jnp.concatenate(heads, axis=1), kv[:, MLA_HEADS * LANES:]


def _mod_kernel(c_ref, w_ref, b_ref, o_ref):
    c = c_ref[...]
    s = c * jax.nn.sigmoid(c)
    o_ref[...] = jnp.dot(s.astype(BF16), w_ref[...].astype(BF16), preferred_element_type=F32) + b_ref[...]


def _modulation(cond, w_mod, b_mod):
    n_rows = cond.shape[0]
    n_out = w_mod.shape[-1]
    tn = 1024
    return pl.pallas_call(
        _mod_kernel,
        grid=(DEPTH, n_out // tn),
        in_specs=[pl.BlockSpec((n_rows, D_MODEL), lambda l, j: (0, 0)),
                  pl.BlockSpec((None, D_MODEL, tn), lambda l, j: (l, 0, j)),
                  pl.BlockSpec((None, 1, tn), lambda l, j: (l, 0, j))],
        out_specs=pl.BlockSpec((None, n_rows, tn), lambda l, j: (l, 0, j)),
        out_shape=jax.ShapeDtypeStruct((DEPTH, n_rows, n_out), F32),
        compiler_params=_params(("arbitrary", "arbitrary")),
        name="modulation",
    )(cond, w_mod, b_mod)


def _group_rows(ref):
    v = ref[...]
    return v.reshape(v.shape[0] * v.shape[1], v.shape[2])


def _mod_rows(ref, steps):
    v = ref[...]
    if v.shape[0] == 1:
        return v[0]
    return jnp.broadcast_to(v, (v.shape[0], steps, v.shape[2])).reshape(v.shape[0] * steps, v.shape[2])


def _store_group(ref, val):
    ref[...] = val.reshape(ref.shape).astype(ref.dtype)


def _store_time_major(ref, val):
    steps = ref.shape[1] // SUBLANES
    for b in range(SUBLANES):
        for half in range(2):
            ref[half, pl.ds(b, steps, stride=SUBLANES), :] = (
                val[b * steps:(b + 1) * steps, LANES * half:LANES * (half + 1)])


def _load_time_major(ref):
    steps = ref.shape[1] // SUBLANES
    return jnp.concatenate(
        [jnp.concatenate([ref[half, pl.ds(b, steps, stride=SUBLANES), :] for half in range(2)], axis=1)
         for b in range(SUBLANES)], axis=0)


def _in_proj_kernel(rope, ctx, *refs):
    it = iter(refs)
    x_ref, sc_ref, sh_ref, g1_ref, wz_ref = (next(it) for _ in range(5))
    gq_ref, gk_ref, qlg_ref, wuq_ref, gmq_ref, kvg_ref, wukv_ref, gmk_ref = (next(it) for _ in range(8))
    if rope:
        ca_ref, sna_ref, spa_ref, cm_ref, snm_ref, spm_ref = (next(it) for _ in range(6))
    q_ref, k_ref, v_ref, s5u_ref, lx_ref, lg_ref, qm_ref, km_ref, vm_ref = (next(it) for _ in range(9))
    if ctx:
        k32_ref, v32_ref, ckv_ref, krp_ref = (next(it) for _ in range(4))

    steps = x_ref.shape[1]
    h = _modnorm(_group_rows(x_ref), g1_ref[...], _mod_rows(sc_ref, steps), _mod_rows(sh_ref, steps))
    z = jnp.dot(h.astype(BF16), wz_ref[...], preferred_element_type=F32)
    per_seq = lambda ref: jnp.concatenate([ref[...]] * SUBLANES, axis=0)

    q = _half_rmsnorm(z[:, 0:256]) * gq_ref[...]
    k = _half_rmsnorm(z[:, 256:384]) * gk_ref[...]
    v = z[:, 384:512]
    if ctx:
        _store_group(k32_ref, k)
        _store_group(v32_ref, v)
    if rope:
        quarter = A_HEAD_DIM // 4
        atabs = (per_seq(ca_ref), per_seq(sna_ref), per_seq(spa_ref))
        q = _rope(q, *atabs, quarter)
        k = _rope(k, *(t[:, :LANES] for t in atabs), quarter)
    _store_group(q_ref, q)
    _store_group(k_ref, k)
    _store_group(v_ref, v)

    _store_time_major(s5u_ref, z[:, 512:768])
    _store_time_major(lx_ref, z[:, 768:1024])
    _store_group(lg_ref, z[:, 1024:1280])

    mtabs = mtabs4 = None
    if rope:
        mtabs = tuple(per_seq(r) for r in (cm_ref, snm_ref, spm_ref))
        mtabs4 = tuple(jnp.concatenate([t] * MLA_HEADS, axis=1) for t in mtabs)
    ql = _rmsnorm(z[:, 1280:1536], qlg_ref[...])
    qm = jnp.dot(ql.astype(BF16), wuq_ref[...], preferred_element_type=F32)
    qm = _tile_rmsnorm(qm, MLA_QK) * gmq_ref[...]
    if rope:
        qm = _rope(qm, *mtabs4, MLA_ROPE // 4)
    _store_group(qm_ref, qm)

    ckv_n = _rmsnorm(z[:, 1536:1664], kvg_ref[...])
    krp = z[:, 1664:1792]
    km, vm = _mla_kv(ckv_n, krp, wukv_ref[...], gmk_ref[...], mtabs)
    _store_group(km_ref, km)
    _store_group(vm_ref, vm)
    if ctx:
        _store_group(ckv_ref, ckv_n)
        _store_group(krp_ref, krp)


def _group_specs(per_seq_mod):
    TS = TOK_STEPS
    tok = lambda c: pl.BlockSpec((SUBLANES, TS, c), lambda g, i: (g, i, 0))
    scan = pl.BlockSpec((None, 2, TS * SUBLANES, LANES), lambda g, i: (g, 0, i, 0))
    if per_seq_mod:
        mod = pl.BlockSpec((SUBLANES, 1, D_MODEL), lambda g, i: (g, 0, 0))
    else:
        mod = pl.BlockSpec((1, 1, D_MODEL), lambda g, i: (0, 0, 0))
    return tok, scan, mod


def _in_proj(x, sc, sh, lw, rope_tabs, ctx):
    wspec = _layer_spec(lw["layer"])
    B, L, _ = x.shape
    Bg = B // SUBLANES
    TS = TOK_STEPS
    rope = rope_tabs is not None
    tok, scan, mod = _group_specs(sc.shape[0] > 1)
    in_specs = [tok(D_MODEL), mod, mod, wspec((1, D_MODEL)), wspec((D_MODEL, Z_COLS)),
                wspec((1, 256)), wspec((1, 128)), wspec((1, 256)), wspec((256, 512)), wspec((1, 512)),
                wspec((1, 128)), wspec((128, 768)), wspec((1, 512))]
    args = [x, sc, sh, lw["g1"], lw["wz"], lw["gq"], lw["gk"], lw["qlg"], lw["wuq"], lw["gmq"],
            lw["kvg"], lw["wukv"], lw["gmk"]]
    if rope:
        in_specs += [pl.BlockSpec((TS, 256), lambda g, i: (i, 0))] * 3
        in_specs += [pl.BlockSpec((TS, 128), lambda g, i: (i, 0))] * 3
        args += list(rope_tabs)
    out_specs = [tok(256), tok(128), tok(128), scan, scan, tok(256), tok(512), tok(512), tok(256)]
    sds = jax.ShapeDtypeStruct
    scan_shape = sds((Bg, 2, L * SUBLANES, LANES), F32)
    out_shape = [sds((B, L, 256), BF16), sds((B, L, 128), BF16), sds((B, L, 128), BF16),
                 scan_shape, scan_shape,
                 sds((B, L, 256), F32), sds((B, L, 512), BF16), sds((B, L, 512), BF16),
                 sds((B, L, 256), BF16)]
    if ctx:
        out_specs += [tok(128), tok(128), tok(128), tok(128)]
        out_shape += [sds((B, L, 128), F32)] * 4
    return pl.pallas_call(
        functools.partial(_in_proj_kernel, rope, ctx),
        grid=(Bg, L // TS),
        in_specs=in_specs, out_specs=out_specs, out_shape=out_shape,
        compiler_params=_params(("parallel", "parallel")),
        name="in_proj_ctx" if ctx else "in_proj_lat",
    )(*args)


def _mla_cache_kernel(ckv_ref, krp_ref, wukv_ref, gmk_ref, km_ref, vm_ref):
    km, vm = _mla_kv(ckv_ref[...], krp_ref[...], wukv_ref[...], gmk_ref[...], None)
    km_ref[...] = km.astype(BF16)
    vm_ref[...] = vm.astype(BF16)


def _mla_cache_kv(cckv, ckr_placed, lw):
    wspec = _layer_spec(lw["layer"])
    B, P, _ = cckv.shape
    tok = lambda c: pl.BlockSpec((None, P, c), lambda b: (b, 0, 0))
    return pl.pallas_call(
        _mla_cache_kernel,
        grid=(B,),
        in_specs=[tok(128), tok(128), wspec((128, 768)), wspec((1, 512))],
        out_specs=[tok(512), tok(256)],
        out_shape=[jax.ShapeDtypeStruct((B, P, 512), BF16), jax.ShapeDtypeStruct((B, P, 256), BF16)],
        compiler_params=_params(("parallel",)),
        name="mla_cache_kv",
    )(cckv, ckr_placed, lw["wukv"], lw["gmk"])


def _sink_softmax(s, sink):
    sink = sink * LOG2_E
    m = jnp.maximum(jnp.max(s, axis=-1, keepdims=True), sink)
    e = jnp.exp2(s - m)
    den = jnp.sum(e, axis=-1, keepdims=True) + jnp.exp2(sink - m)
    return e.astype(BF16), 1.0 / den


def _gqa_tile(qt, keys, vst, sink_lo, sink_hi, mask):
    lane = lax.broadcasted_iota(jnp.int32, (1, LANES), 1)
    lo = lane < A_HEAD_DIM
    zero = jnp.zeros_like(qt)
    ps, invs = [], []
    for qh, sink in ((jnp.where(lo, qt, zero), sink_lo), (jnp.where(lo, zero, qt), sink_hi)):
        s = lax.dot_general(qh, keys, (((1,), (1,)), ((), ())), preferred_element_type=F32)
        if mask is not None:
            s = jnp.where(mask, s, NEG_INF)
        p, inv = _sink_softmax(s, sink)
        ps.append(p)
        invs.append(inv)
    o = jnp.dot(jnp.concatenate(ps, axis=1), vst, preferred_element_type=F32)
    return o * jnp.where(lo, invs[0], invs[1])


def _stack_kv_halves(v):
    lane = lax.broadcasted_iota(jnp.int32, (1, LANES), 1)
    lo = lane < A_HEAD_DIM
    zero = jnp.zeros_like(v)
    return jnp.concatenate([jnp.where(lo, v, zero), jnp.where(lo, zero, v)], axis=0)


def _attn_a_ctx_kernel(sink_ref, q_ref, k_ref, v_ref, o_ref):
    keys = k_ref[...]
    vst = _stack_kv_halves(v_ref[...])
    for t in range(2):
        o = _gqa_tile(q_ref[:, LANES * t:LANES * (t + 1)], keys, vst, sink_ref[t], sink_ref[2 + t], None)
        o_ref[:, LANES * t:LANES * (t + 1)] = o.astype(o_ref.dtype)


def _attn_a_ctx(q, k, v, sink):
    B, L, _ = q.shape
    tok = lambda c: pl.BlockSpec((None, L, c), lambda b: (b, 0, 0))
    return pl.pallas_call(
        _attn_a_ctx_kernel,
        grid=(B,),
        in_specs=[pl.BlockSpec(memory_space=pltpu.SMEM), tok(256), tok(128), tok(128)],
        out_specs=tok(256),
        out_shape=jax.ShapeDtypeStruct((B, L, 256), BF16),
        compiler_params=_params(("parallel",)),
        name="attn_a_ctx",
    )(sink, q, k, v)


def _attn_a_lat_kernel(sink_ref, q_ref, kc_ref, vc_ref, k_ref, v_ref, o_ref):
    i = pl.program_id(1)
    L = k_ref.shape[0]
    n_ctx = kc_ref.shape[0]
    rows = q_ref.shape[0]
    span = rows + 2 * WINDOW
    start = pl.multiple_of(jnp.clip(i * rows - WINDOW, 0, L - span), WINDOW)
    keys = jnp.concatenate([kc_ref[...], k_ref[pl.ds(start, span), :]], axis=0)
    vals = jnp.concatenate([vc_ref[...], v_ref[pl.ds(start, span), :]], axis=0)
    vst = _stack_kv_halves(vals)
    col = lax.broadcasted_iota(jnp.int32, (rows, n_ctx + span), 1)
    row = lax.broadcasted_iota(jnp.int32, (rows, n_ctx + span), 0)
    rel = (col - n_ctx + start) - (row + i * rows)
    mask = (col < n_ctx) | (jnp.abs(rel) <= WINDOW)
    for t in range(2):
        o = _gqa_tile(q_ref[:, LANES * t:LANES * (t + 1)], keys, vst, sink_ref[t], sink_ref[2 + t], mask)
        o_ref[:, LANES * t:LANES * (t + 1)] = o.astype(o_ref.dtype)


def _attn_a_lat(q, k, v, kc, vc, sink):
    B, L, _ = q.shape
    P = kc.shape[1]
    blk = lambda c: pl.BlockSpec((None, ATTN_Q_TILE, c), lambda b, i: (b, i, 0))
    whole = lambda n, c: pl.BlockSpec((None, n, c), lambda b, i: (b, 0, 0))
    return pl.pallas_call(
        _attn_a_lat_kernel,
        grid=(B, L // ATTN_Q_TILE),
        in_specs=[pl.BlockSpec(memory_space=pltpu.SMEM), blk(256), whole(P, 128), whole(P, 128),
                  whole(L, 128), whole(L, 128)],
        out_specs=blk(256),
        out_shape=jax.ShapeDtypeStruct((B, L, 256), BF16),
        compiler_params=_params(("parallel", "parallel")),
        name="attn_a_lat",
    )(sink, q, kc, vc, k, v)


def _mla_stack_values(vals):
    lane = lax.broadcasted_iota(jnp.int32, (1, MLA_HEADS * MLA_V), 1)
    zero = jnp.zeros_like(vals)
    return jnp.concatenate([jnp.where((lane >= MLA_V * h) & (lane < MLA_V * (h + 1)), vals, zero)
                            for h in range(MLA_HEADS)], axis=0)


def _mla_attend(q, key_parts, vst):
    lane = lax.broadcasted_iota(jnp.int32, (1, MLA_HEADS * MLA_V), 1)
    ps = []
    inv = jnp.zeros((q.shape[0], MLA_HEADS * MLA_V), F32)
    for h in range(MLA_HEADS):
        qh = q[:, LANES * h:LANES * (h + 1)]
        ss = [lax.dot_general(qh, kp[:, LANES * h:LANES * (h + 1)], (((1,), (1,)), ((), ())),
                              preferred_element_type=F32) for kp in key_parts]
        m = functools.reduce(jnp.maximum, [jnp.max(s, axis=-1, keepdims=True) for s in ss])
        es = [jnp.exp2(s - m) for s in ss]
        den = functools.reduce(lambda a, b: a + b, [jnp.sum(e, axis=-1, keepdims=True) for e in es])
        ps += [e.astype(BF16) for e in es]
        inv = jnp.where((lane >= MLA_V * h) & (lane < MLA_V * (h + 1)), 1.0 / den, inv)
    return jnp.dot(jnp.concatenate(ps, axis=1), vst, preferred_element_type=F32) * inv


def _mla_ctx_kernel(q_ref, k_ref, v_ref, o_ref):
    o_ref[...] = _mla_attend(q_ref[...], [k_ref[...]], _mla_stack_values(v_ref[...])).astype(o_ref.dtype)


def _mla_ctx(q, k, v):
    B, L, _ = q.shape
    tok = lambda c: pl.BlockSpec((None, L, c), lambda b: (b, 0, 0))
    return pl.pallas_call(
        _mla_ctx_kernel,
        grid=(B,),
        in_specs=[tok(512), tok(512), tok(256)],
        out_specs=tok(256),
        out_shape=jax.ShapeDtypeStruct((B, L, 256), BF16),
        compiler_params=_params(("parallel",)),
        name="mla_ctx",
    )(q, k, v)


def _mla_lat_kernel(q_ref, kc_ref, vc_ref, k_ref, v_ref, o_ref, vst_ref):
    @pl.when(pl.program_id(1) == 0)
    def _():
        vst_ref[...] = _mla_stack_values(jnp.concatenate([vc_ref[...], v_ref[...]], axis=0))

    o = _mla_attend(q_ref[...], [kc_ref[...], k_ref[...]], vst_ref[...])
    o_ref[...] = o.astype(o_ref.dtype)


def _mla_lat(q, k, v, kc, vc):
    B, L, _ = q.shape
    P = kc.shape[1]
    TQ = MLA_Q_TILE
    blk = lambda c: pl.BlockSpec((None, TQ, c), lambda b, i: (b, i, 0))
    whole = lambda n, c: pl.BlockSpec((None, n, c), lambda b, i: (b, 0, 0))
    return pl.pallas_call(
        _mla_lat_kernel,
        grid=(B, L // TQ),
        in_specs=[blk(512), whole(P, 512), whole(P, 256), whole(L, 512), whole(L, 256)],
        out_specs=blk(256),
        out_shape=jax.ShapeDtypeStruct((B, L, 256), BF16),
        scratch_shapes=[pltpu.VMEM((MLA_HEADS * (P + L), MLA_HEADS * MLA_V), BF16)],
        compiler_params=_params(("parallel", "arbitrary")),
        name="mla_lat",
    )(q, kc, vc, k, v)


def _join_halves(ref):
    return jnp.concatenate([ref[0], ref[1]], axis=1)


def _split_halves(ref, val):
    ref[0] = val[:, :LANES]
    ref[1] = val[:, LANES:]


def _s5_kernel(uf_ref, ub_ref, bf_ref, bb_ref, a_ref, h0_ref, cf_ref, cb_ref,
               yf_ref, yb_ref, fin_ref, s_ref, st_ref):
    i = pl.program_id(1)
    steps = SCAN_STEPS

    @pl.when(i == 0)
    def _():
        st_ref[...] = h0_ref[...]

    uf = _join_halves(uf_ref).astype(BF16)
    ub = _join_halves(ub_ref).astype(BF16)
    s_ref[0] = jnp.dot(uf, bf_ref[:, :S5_WIDTH], preferred_element_type=F32)
    s_ref[1] = jnp.dot(uf, bf_ref[:, S5_WIDTH:], preferred_element_type=F32)
    s_ref[2] = jnp.dot(ub, bb_ref[:, :S5_WIDTH], preferred_element_type=F32)
    s_ref[3] = jnp.dot(ub, bb_ref[:, S5_WIDTH:], preferred_element_type=F32)

    lane_chunk = 2 * LANES
    for c in range(S5_WIDTH // lane_chunk):
        sl = slice(lane_chunk * c, lane_chunk * (c + 1))
        arf, aif, arb, aib = (a_ref[n, :, sl] for n in range(4))

        def body(j, carry):
            hrf, hif, hrb, hib = carry
            rf = pl.multiple_of(j * SUBLANES, SUBLANES)
            rb = pl.multiple_of((steps - 1 - j) * SUBLANES, SUBLANES)
            nrf = arf * hrf - aif * hif + s_ref[0, pl.ds(rf, SUBLANES), sl]
            nif = arf * hif + aif * hrf + s_ref[1, pl.ds(rf, SUBLANES), sl]
            nrb = arb * hrb - aib * hib + s_ref[2, pl.ds(rb, SUBLANES), sl]
            nib = arb * hib + aib * hrb + s_ref[3, pl.ds(rb, SUBLANES), sl]
            s_ref[0, pl.ds(rf, SUBLANES), sl] = nrf
            s_ref[1, pl.ds(rf, SUBLANES), sl] = nif
            s_ref[2, pl.ds(rb, SUBLANES), sl] = nrb
            s_ref[3, pl.ds(rb, SUBLANES), sl] = nib
            return nrf, nif, nrb, nib

        fin = lax.fori_loop(0, steps, body, tuple(st_ref[n, :, sl] for n in range(4)), unroll=4)
        for n in range(4):
            st_ref[n, :, sl] = fin[n]

    hf = jnp.concatenate([s_ref[0].astype(BF16), s_ref[1].astype(BF16)], axis=1)
    _split_halves(yf_ref, jnp.dot(hf, cf_ref[...], preferred_element_type=F32))
    hb = jnp.concatenate([s_ref[2].astype(BF16), s_ref[3].astype(BF16)], axis=1)
    _split_halves(yb_ref, jnp.dot(hb, cb_ref[...], preferred_element_type=F32))
    fin_ref[...] = st_ref[...]


def _s5_scan(u_rows, lw, h0):
    wspec = _layer_spec(lw["layer"])
    Bg, _, rows, _ = u_rows.shape
    R = SCAN_STEPS * SUBLANES
    n = rows // R
    fwd = pl.BlockSpec((None, 2, R, LANES), lambda g, i: (g, 0, i, 0))
    bwd = pl.BlockSpec((None, 2, R, LANES), lambda g, i: (g, 0, n - 1 - i, 0))
    st = pl.BlockSpec((None, 4, SUBLANES, S5_WIDTH), lambda g, i: (g, 0, 0, 0))
    return pl.pallas_call(
        _s5_kernel,
        grid=(Bg, n),
        in_specs=[fwd, bwd, wspec((256, 2 * S5_WIDTH)), wspec((256, 2 * S5_WIDTH)),
                  wspec((4, SUBLANES, S5_WIDTH)), st, wspec((2 * S5_WIDTH, 256)), wspec((2 * S5_WIDTH, 256))],
        out_specs=[fwd, bwd, st],
        out_shape=[jax.ShapeDtypeStruct(u_rows.shape, F32), jax.ShapeDtypeStruct(u_rows.shape, F32),
                   jax.ShapeDtypeStruct((Bg, 4, SUBLANES, S5_WIDTH), F32)],
        scratch_shapes=[pltpu.VMEM((4, R, S5_WIDTH), F32), pltpu.VMEM((4, SUBLANES, S5_WIDTH), F32)],
        compiler_params=_params(("parallel", "arbitrary")),
        name="s5_scan",
    )(u_rows, u_rows, lw["s5_bf"], lw["s5_bb"], lw["s5_a"], h0, lw["s5_cf"], lw["s5_cb"])


def _lru_gates(x_ref, pre_ref, post_ref, has_pre, has_post, cw_ref, cb_ref, w_ref, b_ref, sp_ref, a_ref, h_ref):
    R = x_ref.shape[1]
    pre = jnp.where(has_pre, _join_halves(pre_ref), 0.0)
    post = jnp.where(has_post, _join_halves(post_ref), 0.0)
    xp = jnp.concatenate([pre, _join_halves(x_ref), post], axis=0)
    xc = cb_ref[...]
    for t in range(LRU_CONV):
        xc = xc + xp[SUBLANES * t:SUBLANES * t + R] * cw_ref[t:t + 1, :]
    g = jnp.dot(xc.astype(BF16), w_ref[...], preferred_element_type=F32) + b_ref[...]
    r = jax.nn.sigmoid(g[:, :LRU_WIDTH])
    ig = jax.nn.sigmoid(g[:, LRU_WIDTH:])
    log_a = (-LRU_C) * r * sp_ref[...]
    a = jnp.exp(log_a)
    a_ref[...] = a
    h_ref[...] = jnp.sqrt(1.0 - a * a) * (ig * xc)


def _lru_kernel(xf_ref, xfp_ref, xfn_ref, xb_ref, xbp_ref, xbn_ref, cw_ref, cb_ref, wf_ref, wb_ref,
                bf_ref, bb_ref, spf_ref, spb_ref, h0_ref, of_ref, ob_ref, fin_ref,
                af_ref, ab_ref, hf_ref, hb_ref, st_ref):
    i = pl.program_id(1)
    n = pl.num_programs(1)
    steps = SCAN_STEPS

    @pl.when(i == 0)
    def _():
        st_ref[...] = h0_ref[...]

    _lru_gates(xf_ref, xfp_ref, xfn_ref, i > 0, i < n - 1, cw_ref, cb_ref, wf_ref, bf_ref, spf_ref, af_ref, hf_ref)
    _lru_gates(xb_ref, xbp_ref, xbn_ref, i < n - 1, i > 0, cw_ref, cb_ref, wb_ref, bb_ref, spb_ref, ab_ref, hb_ref)

    def body(j, carry):
        hf, hb = carry
        rf = pl.multiple_of(j * SUBLANES, SUBLANES)
        rb = pl.multiple_of((steps - 1 - j) * SUBLANES, SUBLANES)
        nf = af_ref[pl.ds(rf, SUBLANES), :] * hf + hf_ref[pl.ds(rf, SUBLANES), :]
        nb = ab_ref[pl.ds(rb, SUBLANES), :] * hb + hb_ref[pl.ds(rb, SUBLANES), :]
        hf_ref[pl.ds(rf, SUBLANES), :] = nf
        hb_ref[pl.ds(rb, SUBLANES), :] = nb
        return nf, nb

    ff, fb = lax.fori_loop(0, steps, body, (st_ref[0], st_ref[1]), unroll=8)
    st_ref[0] = ff
    st_ref[1] = fb
    fin_ref[...] = st_ref[...]
    _split_halves(of_ref, hf_ref[...])
    _split_halves(ob_ref, hb_ref[...])


def _lru_scan(x_rows, lw, h0):
    wspec = _layer_spec(lw["layer"])
    Bg, _, rows, _ = x_rows.shape
    R = SCAN_STEPS * SUBLANES
    n = rows // R
    pre_rows = 2 * SUBLANES
    fwd = lambda g, i: i
    bwd = lambda g, i: n - 1 - i
    blk = lambda m: pl.BlockSpec((None, 2, R, LANES), lambda g, i: (g, 0, m(g, i), 0))
    pre = lambda m: pl.BlockSpec(
        (None, 2, pre_rows, LANES), lambda g, i: (g, 0, jnp.maximum(m(g, i) * (R // pre_rows) - 1, 0), 0))
    post = lambda m: pl.BlockSpec(
        (None, 2, SUBLANES, LANES),
        lambda g, i: (g, 0, jnp.minimum((m(g, i) + 1) * (R // SUBLANES), rows // SUBLANES - 1), 0))
    st = pl.BlockSpec((None, 2, SUBLANES, 256), lambda g, i: (g, 0, 0, 0))
    return pl.pallas_call(
        _lru_kernel,
        grid=(Bg, n),
        in_specs=[blk(fwd), pre(fwd), post(fwd), blk(bwd), pre(bwd), post(bwd),
                  wspec((LRU_CONV, 256)), wspec((1, 256)), wspec((256, 512)), wspec((256, 512)),
                  wspec((1, 512)), wspec((1, 512)), wspec((1, 256)), wspec((1, 256)), st],
        out_specs=[blk(fwd), blk(bwd), st],
        out_shape=[jax.ShapeDtypeStruct(x_rows.shape, F32), jax.ShapeDtypeStruct(x_rows.shape, F32),
                   jax.ShapeDtypeStruct((Bg, 2, SUBLANES, 256), F32)],
        scratch_shapes=[pltpu.VMEM((R, 256), F32), pltpu.VMEM((R, 256), F32), pltpu.VMEM((R, 256), F32),
                        pltpu.VMEM((R, 256), F32), pltpu.VMEM((2, SUBLANES, 256), F32)],
        compiler_params=_params(("parallel", "arbitrary")),
        name="lru_scan",
    )(x_rows, x_rows, x_rows, x_rows, x_rows, x_rows, lw["lru_cw"], lw["lru_cb"], lw["lru_wf"], lw["lru_wb"],
      lw["lru_bf"], lw["lru_bb"], lw["lru_spf"], lw["lru_spb"], h0)


def _merge_kernel(x_ref, sc_ref, sh_ref, gt_ref, g1_ref, wg_ref, oa_ref, u_ref, yf_ref, yb_ref, d_ref,
                  wglu_ref, hf_ref, hb_ref, lg_ref, od_ref, wb_ref, wo_ref, o_ref):
    steps = x_ref.shape[1]
    x = _group_rows(x_ref)
    h = _modnorm(x, g1_ref[...], _mod_rows(sc_ref, steps), _mod_rows(sh_ref, steps)).astype(BF16)

    yb5 = jax.nn.gelu(d_ref[...] * _load_time_major(u_ref) + _load_time_major(yf_ref) + _load_time_major(yb_ref))
    gv = jnp.dot(yb5.astype(BF16), wglu_ref[...], preferred_element_type=F32)
    o_b = gv[:, :BRANCH_W] * jax.nn.sigmoid(gv[:, BRANCH_W:])
    o_c = (_load_time_major(hf_ref) + _load_time_major(hb_ref)) * jax.nn.gelu(_group_rows(lg_ref))
    branches = (_group_rows(oa_ref), o_b.astype(BF16), o_c.astype(BF16), _group_rows(od_ref))

    acc = jnp.zeros(x.shape, F32)
    for n in range(N_BRANCH):
        gate = jnp.dot(h, wg_ref[:, D_MODEL * n:D_MODEL * (n + 1)], preferred_element_type=F32)
        proj = jnp.dot(branches[n], wb_ref[n], preferred_element_type=F32)
        acc = acc + jax.nn.sigmoid(gate) * proj
    out = jnp.dot(acc.astype(BF16), wo_ref[...], preferred_element_type=F32)
    _store_group(o_ref, x + _mod_rows(gt_ref, steps) * out)


def _merge(x, sc, sh, gt, lw, o_a, s5u_t, yf_t, yb_t, hf_t, hb_t, lg, o_d):
    wspec = _layer_spec(lw["layer"])
    B, L, _ = x.shape
    tok, scan, mod = _group_specs(sc.shape[0] > 1)
    return pl.pallas_call(
        _merge_kernel,
        grid=(B // SUBLANES, L // TOK_STEPS),
        in_specs=[tok(D_MODEL), mod, mod, mod, wspec((1, D_MODEL)), wspec((D_MODEL, N_BRANCH * D_MODEL)),
                  tok(256), scan, scan, scan, wspec((1, 256)), wspec((256, 512)),
                  scan, scan, tok(256), tok(256), wspec((N_BRANCH, BRANCH_W, D_MODEL)),
                  wspec((D_MODEL, D_MODEL))],
        out_specs=tok(D_MODEL),
        out_shape=jax.ShapeDtypeStruct((B, L, D_MODEL), F32),
        compiler_params=_params(("parallel", "parallel")),
        name="merge",
    )(x, sc, sh, gt, lw["g1"], lw["wg"], o_a, s5u_t, yf_t, yb_t, lw["s5_d"], lw["wglu"],
      hf_t, hb_t, lg, o_d, lw["wb"], lw["wo"])


def _first_index(values, target):
    idx = jnp.full_like(target, float(len(values) - 1))
    for n in range(len(values) - 2, -1, -1):
        idx = jnp.where(values[n] == target, float(n), idx)
    return idx


def _list_max(values):
    return functools.reduce(jnp.maximum, values)


def _moe_sort_kernel(x_ref, sc_ref, sh_ref, g2_ref, wrh_ref, wrl_ref, br_ref, tri_ref,
                     hs_ref, gs_ref, dcol_ref, meta_ref):
    for s in range(x_ref.shape[0] // MOE_SORT_TILE):
        tok = pl.ds(s * MOE_SORT_TILE, MOE_SORT_TILE)
        srt = pl.ds(s * MOE_SORTED_ROWS, MOE_SORTED_ROWS)
        _moe_sort_tile(x_ref.at[tok], sc_ref, sh_ref, g2_ref, wrh_ref, wrl_ref, br_ref, tri_ref,
                       hs_ref.at[srt], gs_ref.at[srt], dcol_ref.at[tok], meta_ref.at[s])


def _moe_sort_tile(x_ref, sc_ref, sh_ref, g2_ref, wrh_ref, wrl_ref, br_ref, tri_ref,
                   hs_ref, gs_ref, dcol_ref, meta_ref):
    T = MOE_SORT_TILE
    h = _modnorm(x_ref[...], g2_ref[...], sc_ref[...], sh_ref[...])
    hh = h.astype(BF16)
    hl = (h - hh.astype(F32)).astype(BF16)
    nt = (((1,), (1,)), ((), ()))
    logits = (jnp.dot(hh, wrh_ref[...], preferred_element_type=F32)
              + jnp.dot(hl, wrh_ref[...], preferred_element_type=F32)
              + jnp.dot(hh, wrl_ref[...], preferred_element_type=F32)) + br_ref[...]
    lt = logits.T
    gl = [lt[g:g + 1, :] for g in range(N_GROUPS)]
    gmax = _list_max(gl)
    g_idx = _first_index(gl, gmax)
    pg = 1.0 / sum(jnp.exp(v - gmax) for v in gl)
    hot = [g_idx == float(g) for g in range(N_GROUPS)]
    el = []
    for e in range(EXPERTS_PER_GROUP):
        v = jnp.zeros_like(gmax)
        for g in range(N_GROUPS):
            r = N_GROUPS + EXPERTS_PER_GROUP * g + e
            v = jnp.where(hot[g], lt[r:r + 1, :], v)
        el.append(v)
    emax = _list_max(el)
    ee = [jnp.exp(v - emax) for v in el]
    esum = sum(ee)
    pe = [v / esum for v in ee]
    v1 = _list_max(pe)
    i1 = _first_index(pe, v1)
    pe2 = [jnp.where(i1 == float(e), -1.0, pe[e]) for e in range(EXPERTS_PER_GROUP)]
    v2 = _list_max(pe2)
    i2 = _first_index(pe2, v2)
    tot = v1 + v2
    w = [jnp.where(i1 == float(e), pg * v1 / tot, jnp.where(i2 == float(e), pg * v2 / tot, 0.0))
         for e in range(EXPERTS_PER_GROUP)]

    zero_row = jnp.zeros_like(gmax)
    g8 = jnp.concatenate([jnp.where(hot[g], 1.0, 0.0) for g in range(N_GROUPS)] + [zero_row] * 4, axis=0)
    cum = jnp.dot(g8.astype(BF16), tri_ref[...], preferred_element_type=F32)
    off = jnp.zeros((1, 1), F32)
    dest = zero_row
    counts = []
    for g in range(N_GROUPS):
        cnt = cum[g:g + 1, T - 1:T]
        padded = jnp.floor((cnt + (MOE_ROW_ALIGN - 1.0)) * (1.0 / MOE_ROW_ALIGN)) * MOE_ROW_ALIGN
        dest = jnp.where(hot[g], off + cum[g:g + 1, :] - 1.0, dest)
        off = off + padded
        counts.append(padded)
    rows = lax.broadcasted_iota(jnp.int32, (MOE_SORTED_ROWS, T), 0).astype(F32)
    perm = jnp.where(rows == dest, 1.0, 0.0).astype(BF16)
    hs_ref[...] = jnp.dot(perm, hh, preferred_element_type=F32).astype(BF16)
    gates = jnp.concatenate(w + [jnp.zeros((LANES - EXPERTS_PER_GROUP, T), F32)], axis=0)
    ghi = gates.astype(BF16)
    glo = (gates - ghi.astype(F32)).astype(BF16)
    gs_ref[...] = (lax.dot_general(perm, ghi, nt, preferred_element_type=F32)
                   + lax.dot_general(perm, glo, nt, preferred_element_type=F32))
    r_i = lax.broadcasted_iota(jnp.int32, (T, T), 0)
    c_i = lax.broadcasted_iota(jnp.int32, (T, T), 1)
    dcol = jnp.sum(jnp.where(r_i == c_i, jnp.broadcast_to(dest, (T, T)), 0.0), axis=1, keepdims=True)
    dcol_ref[...] = jnp.broadcast_to(dcol, (T, LANES))
    meta_ref[...] = jnp.concatenate([jnp.broadcast_to(c, (1, LANES)) for c in counts]
                                    + [jnp.zeros((SUBLANES - N_GROUPS, LANES), F32)], axis=0)


def _moe_sort(xf, sc, sh, lw, tiles_per_mod):
    wspec = _layer_spec(lw["layer"])
    N = xf.shape[0]
    T = MOE_SORT_TILE
    n_tiles = N // T
    sub = MOE_SORT_SUBTILES
    per_b = sc.shape[0] > 1
    mod = pl.BlockSpec((None, 1, D_MODEL), lambda t: (t * sub // tiles_per_mod if per_b else 0, 0, 0))
    sds = jax.ShapeDtypeStruct
    return pl.pallas_call(
        _moe_sort_kernel,
        grid=(n_tiles // sub,),
        in_specs=[pl.BlockSpec((sub * T, D_MODEL), lambda t: (t, 0)), mod, mod, wspec((1, D_MODEL)),
                  wspec((D_MODEL, LANES)), wspec((D_MODEL, LANES)), wspec((1, LANES)), wspec((T, T))],
        out_specs=[pl.BlockSpec((sub * MOE_SORTED_ROWS, D_MODEL), lambda t: (t, 0)),
                   pl.BlockSpec((sub * MOE_SORTED_ROWS, LANES), lambda t: (t, 0)),
                   pl.BlockSpec((sub * T, LANES), lambda t: (t, 0)),
                   pl.BlockSpec((sub, SUBLANES, LANES), lambda t: (t, 0, 0))],
        out_shape=[sds((n_tiles * MOE_SORTED_ROWS, D_MODEL), BF16), sds((n_tiles * MOE_SORTED_ROWS, LANES), F32),
                   sds((N, LANES), F32), sds((n_tiles, SUBLANES, LANES), F32)],
        compiler_params=_params(("parallel",)),
        name="moe_sort",
    )(xf, sc, sh, lw["g2"], lw["wr_hi"], lw["wr_lo"], lw["br"], lw["tri"])


def _moe_expert_kernel(cnt_ref, hs_ref, gs_ref, wga_ref, wup_ref, wdn_ref, ys_ref, ch_ref, cg_ref, cy_ref):
    j = pl.program_id(0)
    g = pl.program_id(1)
    piece = MOE_ROW_ALIGN

    @pl.when((j == 0) & (g == 0))
    def _():
        ch_ref[...] = jnp.zeros_like(ch_ref)
        cg_ref[...] = jnp.zeros_like(cg_ref)

    @pl.when(g == 0)
    def _():
        ys_ref[...] = jnp.zeros_like(ys_ref)

    def for_each_piece(move):
        packed = jnp.int32(0)
        for t in range(MOE_BLOCK_TILES):
            base = (j * MOE_BLOCK_TILES + t) * N_GROUPS
            start = jnp.int32(t * MOE_SORTED_ROWS)
            for g2 in range(N_GROUPS - 1):
                start = start + jnp.where(g2 < g, cnt_ref[base + g2], 0)
            n = cnt_ref[base + g]

            def body(k, carry, start=start, packed=packed):
                move(pl.multiple_of(start + k * piece, piece), pl.multiple_of(packed + k * piece, piece))
                return carry

            lax.fori_loop(0, lax.shift_right_logical(n, 4), body, 0)
            packed = packed + n
        return packed

    def pack(src, dst):
        ch_ref[pl.ds(dst, piece), :] = hs_ref[pl.ds(src, piece), :]
        cg_ref[pl.ds(dst, piece), :] = gs_ref[pl.ds(src, piece), :]

    rows = for_each_piece(pack)

    def experts(r, size):
        h = ch_ref[pl.ds(r, size), :]
        acts = []
        for e in range(EXPERTS_PER_GROUP):
            a = jnp.dot(h, wga_ref[e], preferred_element_type=F32)
            u = jnp.dot(h, wup_ref[e], preferred_element_type=F32)
            acts.append(((a * jax.nn.sigmoid(a)) * u * cg_ref[pl.ds(r, size), e:e + 1]).astype(BF16))
        y = jnp.dot(jnp.concatenate(acts, axis=1), wdn_ref[...], preferred_element_type=F32)
        cy_ref[pl.ds(r, size), :] = y.astype(BF16)

    big, mid, small = MOE_CHUNK, MOE_CHUNK // 2, MOE_CHUNK // 4
    n_big = lax.shift_right_logical(rows, big.bit_length() - 1)

    def chunk(c, carry):
        experts(pl.multiple_of(c * big, big), big)
        return carry

    lax.fori_loop(0, n_big, chunk, 0)
    tail = pl.multiple_of(n_big * big, big)
    rem = rows - tail
    pl.when(rem > mid + small)(lambda: experts(tail, big))
    pl.when((rem > small) & (rem <= mid + small))(lambda: experts(tail, mid))
    pl.when((rem > mid) & (rem <= mid + small))(lambda: experts(pl.multiple_of(tail + mid, small), small))
    pl.when((rem > 0) & (rem <= small))(lambda: experts(tail, small))

    def unpack(src, dst):
        ys_ref[pl.ds(src, piece), :] = cy_ref[pl.ds(dst, piece), :]

    for_each_piece(unpack)


def _moe_experts(cnt, hs, gs, lw):
    rows = MOE_BLOCK_TILES * MOE_SORTED_ROWS
    cap = MOE_BLOCK_TILES * MOE_SORT_TILE
    blk = lambda c: pl.BlockSpec((rows, c), lambda j, g, cnt: (j, 0))
    layer = lw["layer"]
    wspec = lambda r, c: pl.BlockSpec((None, EXPERTS_PER_GROUP, r, c), lambda j, g, cnt: (layer, g, 0, 0))
    return pl.pallas_call(
        _moe_expert_kernel,
        grid_spec=pltpu.PrefetchScalarGridSpec(
            num_scalar_prefetch=1, grid=(hs.shape[0] // rows, N_GROUPS),
            in_specs=[blk(D_MODEL), blk(LANES), wspec(D_MODEL, EXPERT_FF), wspec(D_MODEL, EXPERT_FF),
                      pl.BlockSpec((None, EXPERTS_PER_GROUP * EXPERT_FF, D_MODEL),
                                   lambda j, g, cnt: (layer, g, 0))],
            out_specs=blk(D_MODEL),
            scratch_shapes=[pltpu.VMEM((cap, D_MODEL), BF16), pltpu.VMEM((cap, LANES), F32),
                            pltpu.VMEM((cap, D_MODEL), BF16)]),
        out_shape=jax.ShapeDtypeStruct(hs.shape, BF16),
        compiler_params=_params(("arbitrary", "arbitrary")),
        name="moe_experts",
    )(cnt, hs, gs, lw["wga"], lw["wup"], lw["wdn"])


def _moe_unsort_kernel(x_ref, gt_ref, ys_ref, dcol_ref, o_ref):
    cols = lax.broadcasted_iota(jnp.int32, (MOE_SORT_TILE, MOE_SORTED_ROWS), 1).astype(F32)
    perm_t = jnp.where(cols == dcol_ref[:, 0:1], 1.0, 0.0).astype(BF16)
    y = jnp.dot(perm_t, ys_ref[...], preferred_element_type=F32)
    o_ref[...] = x_ref[...] + gt_ref[...] * y


def _moe_unsort(xf, gt, ys, dcol, tiles_per_mod):
    N = xf.shape[0]
    T = MOE_SORT_TILE
    per_b = gt.shape[0] > 1
    mod = pl.BlockSpec((None, 1, D_MODEL), lambda t: (t // tiles_per_mod if per_b else 0, 0, 0))
    tok = pl.BlockSpec((T, D_MODEL), lambda t: (t, 0))
    return pl.pallas_call(
        _moe_unsort_kernel,
        grid=(N // T,),
        in_specs=[tok, mod, pl.BlockSpec((MOE_SORTED_ROWS, D_MODEL), lambda t: (t, 0)),
                  pl.BlockSpec((T, LANES), lambda t: (t, 0))],
        out_specs=tok,
        out_shape=jax.ShapeDtypeStruct((N, D_MODEL), F32),
        compiler_params=_params(("parallel",)),
        name="moe_unsort",
    )(xf, gt, ys, dcol)


def _moe(x, sc, sh, gt, lw):
    B, L, _ = x.shape
    N = B * L
    T = MOE_SORT_TILE
    xf = x.reshape(N, D_MODEL)
    hs, gs, dcol, meta = _moe_sort(xf, sc, sh, lw, L // T)

    cnt = meta[:, :N_GROUPS, 0].astype(jnp.int32).reshape(-1)
    ys = _moe_experts(cnt, hs, gs, lw)
    out = _moe_unsort(xf, gt, ys, dcol, L // T)
    return out.reshape(B, L, D_MODEL)


def _block_diag(blocks):
    n, r, c = blocks.shape
    eye = jnp.eye(n, dtype=blocks.dtype)
    return jnp.einsum("nrc,nm->nrmc", blocks, eye).reshape(n * r, n * c)


def _rope_tables(seq_len, dim, width, offset):
    rows = seq_len // GRID_W
    r, col = np.meshgrid(np.arange(rows), np.arange(GRID_W), indexing="ij")
    r = r.reshape(-1).astype(np.float64)
    col = col.reshape(-1).astype(np.float64)
    quarter = dim // 4
    freqs = ROPE_BASE ** (-np.arange(quarter, dtype=np.float64) / quarter)
    ang_r = r[:, None] * freqs
    ang_c = col[:, None] * freqs
    zero = np.zeros_like(ang_r)
    cos = np.cos(np.concatenate([ang_r, ang_r, ang_c, ang_c], axis=-1))
    sin_next = np.concatenate([-np.sin(ang_r), zero, -np.sin(ang_c), zero], axis=-1)
    sin_prev = np.concatenate([zero, np.sin(ang_r), zero, np.sin(ang_c)], axis=-1)

    def place(t, fill):
        return np.pad(t, ((0, 0), (offset, width - offset - dim)), constant_values=fill).astype(np.float32)

    return place(cos, 1.0), place(sin_next, 0.0), place(sin_prev, 0.0)


def _layer_weights(l, w):
    lw = {}
    w_in = w["w_in"][l]
    q_cols = w_in[:, 0:256].reshape(D_MODEL, A_HEADS, A_HEAD_DIM)[:, HEAD_ORDER].reshape(D_MODEL, 256)
    kr_cols = jnp.pad(w_in[:, 1664:1696], ((0, 0), (MLA_NOPE, LANES - MLA_QK)))
    lw["wz"] = jnp.concatenate([q_cols, w_in[:, 256:1664], kr_cols], axis=1).astype(BF16)
    lw["wg"] = w_in[:, 1696:].astype(BF16)
    lw["g1"] = w["norm1_g"][l][None]
    lw["g2"] = w["norm2_g"][l][None]
    lw["gq"] = jnp.tile(w["a_qnorm_g"][l], A_HEADS)[None] * (A_HEAD_DIM ** -0.5 * LOG2_E)
    lw["gk"] = jnp.tile(w["a_knorm_g"][l], A_KV_HEADS)[None]
    lw["sink"] = w["a_sink"][l]
    lw["qlg"] = w["mla_q_lat_norm"][l][None]
    wuq = w["mla_w_uq"][l].reshape(MLA_Q_LORA, MLA_HEADS, MLA_QK)
    lw["wuq"] = jnp.pad(wuq, ((0, 0), (0, 0), (0, LANES - MLA_QK))).reshape(MLA_Q_LORA, MLA_HEADS * LANES).astype(BF16)
    pad_g = lambda g: jnp.tile(jnp.pad(g, (0, LANES - MLA_QK)), MLA_HEADS)[None]
    lw["gmq"] = pad_g(w["mla_qnorm_g"][l]) * (MLA_QK ** -0.5 * LOG2_E)
    lw["gmk"] = pad_g(w["mla_knorm_g"][l])
    lw["kvg"] = w["mla_kv_norm"][l][None]
    wukv = w["mla_w_ukv"][l].reshape(MLA_KV_LORA, MLA_HEADS, MLA_NOPE + MLA_V)
    wk = jnp.pad(wukv[:, :, :MLA_NOPE], ((0, 0), (0, 0), (0, LANES - MLA_NOPE))).reshape(MLA_KV_LORA, MLA_HEADS * LANES)
    wv = wukv[:, :, MLA_NOPE:].reshape(MLA_KV_LORA, MLA_HEADS * MLA_V)
    lw["wukv"] = jnp.concatenate([wk, wv], axis=1).astype(BF16)

    lre = w["s5_lambda_re"][l]
    lim = w["s5_lambda_im"][l]
    dt = jnp.exp(w["s5_log_step"][l])[:, :, None]
    mag = jnp.exp(lre * dt)
    ar, ai = mag * jnp.cos(lim * dt), mag * jnp.sin(lim * dt)
    den = lre * lre + lim * lim
    fr = ((ar - 1.0) * lre + ai * lim) / den
    fi = (ai * lre - (ar - 1.0) * lim) / den
    br, bi = w["s5_b_re"][l], w["s5_b_im"][l]
    bbr = fr[..., None] * br - fi[..., None] * bi
    bbi = fr[..., None] * bi + fi[..., None] * br
    in_map = lambda d: jnp.concatenate(
        [_block_diag(jnp.swapaxes(bbr[d], 1, 2)), _block_diag(jnp.swapaxes(bbi[d], 1, 2))], axis=1).astype(BF16)
    out_map = lambda d: jnp.concatenate(
        [_block_diag(jnp.swapaxes(w["s5_c_re"][l][d], 1, 2)),
         -_block_diag(jnp.swapaxes(w["s5_c_im"][l][d], 1, 2))], axis=0).astype(BF16)
    lw["s5_bf"], lw["s5_bb"] = in_map(0), in_map(1)
    lw["s5_cf"], lw["s5_cb"] = out_map(0), out_map(1)
    coef = jnp.stack([ar[0].reshape(-1), ai[0].reshape(-1), ar[1].reshape(-1), ai[1].reshape(-1)])
    lw["s5_a"] = jnp.broadcast_to(coef[:, None, :], (4, SUBLANES, S5_WIDTH))
    lw["s5_d"] = w["s5_d"][l][None]
    lw["wglu"] = w["s5_w_glu"][l].astype(BF16)

    lw["lru_cw"] = w["lru_conv_w"][l]
    lw["lru_cb"] = w["lru_conv_b"][l][None]
    gate_w = lambda d: jnp.concatenate(
        [_block_diag(w["lru_w_a"][l][d]), _block_diag(w["lru_w_x"][l][d])], axis=1).astype(BF16)
    gate_b = lambda d: jnp.concatenate([w["lru_b_a"][l][d], w["lru_b_x"][l][d]])[None]
    lw["lru_wf"], lw["lru_wb"] = gate_w(0), gate_w(1)
    lw["lru_bf"], lw["lru_bb"] = gate_b(0), gate_b(1)
    sp = jax.nn.softplus(-w["lru_lambda"][l])
    lw["lru_spf"], lw["lru_spb"] = sp[0][None], sp[1][None]

    wb = w["w_branch"][l]
    wb0 = wb[0].reshape(A_HEADS, A_HEAD_DIM, D_MODEL)[HEAD_ORDER].reshape(BRANCH_W, D_MODEL)
    lw["wb"] = jnp.concatenate([wb0[None], wb[1:]], axis=0).astype(BF16)
    lw["wo"] = w["w_out"][l].astype(BF16)

    wr = jnp.pad(jnp.concatenate([w["moe_w_group"][l], w["moe_w_expert"][l]], axis=1),
                 ((0, 0), (0, LANES - N_GROUPS - N_EXPERTS)))
    lw["wr_hi"] = wr.astype(BF16)
    lw["wr_lo"] = (wr - lw["wr_hi"].astype(F32)).astype(BF16)
    lw["br"] = jnp.pad(jnp.concatenate([w["moe_b_group"][l], w["moe_b_expert"][l]]),
                       (0, LANES - N_GROUPS - N_EXPERTS))[None]
    lw["tri"] = jnp.triu(jnp.ones((MOE_SORT_TILE, MOE_SORT_TILE), BF16))
    lw["wga"] = w["moe_w_gate"][l].astype(BF16)
    lw["wup"] = w["moe_w_up"][l].astype(BF16)
    lw["wdn"] = w["moe_w_down"][l].astype(BF16).reshape(N_EXPERTS * EXPERT_FF, D_MODEL)
    return lw


def _rows_to_state(fin, width):
    Bg, n = fin.shape[0], fin.shape[1]
    return jnp.swapaxes(fin, 1, 2).reshape(Bg * SUBLANES, n, width)


def _state_to_rows(state):
    B, n, width = state.shape
    return jnp.swapaxes(state.reshape(B // SUBLANES, SUBLANES, n, width), 1, 2)


def _mix(x, mods, lw, rope_tabs, cache, s5_h0, lru_h0, ctx):
    sh1, sc1, gt1 = mods
    outs = _in_proj(x, sc1, sh1, lw, rope_tabs, ctx)
    q, k, v, s5u_t, lx_t, lg, qm, km, vm = outs[:9]
    if ctx:
        o_a = _attn_a_ctx(q, k, v, lw["sink"][lw["layer"]])
        o_d = _mla_ctx(qm, km, vm)
    else:
        ck, cv, cckv, ckr = cache
        o_a = _attn_a_lat(q, k, v, ck, cv, lw["sink"][lw["layer"]])
        kc, vc = _mla_cache_kv(cckv, ckr, lw)
        o_d = _mla_lat(qm, km, vm, kc, vc)
    yf, yb, s5_fin = _s5_scan(s5u_t, lw, s5_h0)
    hf, hb, lru_fin = _lru_scan(lx_t, lw, lru_h0)
    x = _merge(x, sc1, sh1, gt1, lw, o_a, s5u_t, yf, yb, hf, hb, lg, o_d)
    return x, outs[9:], s5_fin, lru_fin


def kernel(x_prompt, x_sample, cache_attn_k, cache_attn_v, cache_mla_ckv, cache_mla_krope, state_ssm_re, state_ssm_im, state_lru, c, c_ctx, w_mod, b_mod, norm1_g, norm2_g, w_in, a_qnorm_g, a_knorm_g, a_sink, s5_lambda_re, s5_lambda_im, s5_log_step, s5_b_re, s5_b_im, s5_c_re, s5_c_im, s5_d, s5_w_glu, lru_conv_w, lru_conv_b, lru_w_a, lru_b_a, lru_w_x, lru_b_x, lru_lambda, mla_q_lat_norm, mla_w_uq, mla_kv_norm, mla_w_ukv, mla_qnorm_g, mla_knorm_g, w_branch, w_out, moe_w_group, moe_b_group, moe_w_expert, moe_b_expert, moe_w_gate, moe_w_up, moe_w_down):
    w = dict(norm1_g=norm1_g, norm2_g=norm2_g, w_in=w_in, a_qnorm_g=a_qnorm_g, a_knorm_g=a_knorm_g,
             a_sink=a_sink, s5_lambda_re=s5_lambda_re, s5_lambda_im=s5_lambda_im, s5_log_step=s5_log_step,
             s5_b_re=s5_b_re, s5_b_im=s5_b_im, s5_c_re=s5_c_re, s5_c_im=s5_c_im, s5_d=s5_d, s5_w_glu=s5_w_glu,
             lru_conv_w=lru_conv_w, lru_conv_b=lru_conv_b, lru_w_a=lru_w_a, lru_b_a=lru_b_a, lru_w_x=lru_w_x,
             lru_b_x=lru_b_x, lru_lambda=lru_lambda, mla_q_lat_norm=mla_q_lat_norm, mla_w_uq=mla_w_uq,
             mla_kv_norm=mla_kv_norm, mla_w_ukv=mla_w_ukv, mla_qnorm_g=mla_qnorm_g, mla_knorm_g=mla_knorm_g,
             w_branch=w_branch, w_out=w_out, moe_w_group=moe_w_group, moe_b_group=moe_b_group,
             moe_w_expert=moe_w_expert, moe_b_expert=moe_b_expert, moe_w_gate=moe_w_gate, moe_w_up=moe_w_up,
             moe_w_down=moe_w_down)
    B, L, _ = x_prompt.shape
    Bd, Ld, _ = x_sample.shape
    P = cache_attn_k.shape[2]

    cond = jnp.zeros((2 * SUBLANES, D_MODEL), F32).at[:Bd].set(c).at[Bd].set(c_ctx)
    mod = _modulation(cond, w_mod, b_mod[:, None, :])
    mod = mod.reshape(DEPTH, 2 * SUBLANES, 6, D_MODEL)

    rope_a = _rope_tables(Ld, A_HEAD_DIM, A_HEAD_DIM, 0)
    rope_a = tuple(jnp.asarray(np.tile(t, (1, A_HEADS))) for t in rope_a)
    rope_m = tuple(jnp.asarray(t) for t in _rope_tables(Ld, MLA_ROPE, LANES, MLA_NOPE))

    lw_all = jax.vmap(lambda w1: _layer_weights(0, {name: v[None] for name, v in w1.items()}))(w)

    xp, xs = x_prompt, x_sample
    ak_l, av_l, ckv_l, kr_l, sr_l, si_l, lru_l = [], [], [], [], [], [], []
    for l in range(DEPTH):
        lw = dict(lw_all, layer=l)
        lat_mod = [mod[l, :Bd, n][:, None, :] for n in range(6)]
        ctx_mod = [mod[l, Bd:Bd + 1, n][:, None, :] for n in range(6)]

        zs5 = jnp.zeros((B // SUBLANES, 4, SUBLANES, S5_WIDTH), F32)
        zlru = jnp.zeros((B // SUBLANES, 2, SUBLANES, LRU_WIDTH), F32)
        xp, (k32, v32, ckv_n, krp), s5_fin, lru_fin = _mix(xp, ctx_mod[0:3], lw, None, None, zs5, zlru, True)
        xp = _moe(xp, ctx_mod[4], ctx_mod[3], ctx_mod[5], lw)
        ak_l.append(k32.reshape(B, L, A_KV_HEADS, A_HEAD_DIM))
        av_l.append(v32.reshape(B, L, A_KV_HEADS, A_HEAD_DIM))
        ckv_l.append(ckv_n)
        kr_l.append(krp[:, :, MLA_NOPE:MLA_QK])
        fin = _rows_to_state(s5_fin, S5_WIDTH)
        sr_l.append(fin[:, 0::2].reshape(B, 2, S5_GROUPS, S5_STATE))
        si_l.append(fin[:, 1::2].reshape(B, 2, S5_GROUPS, S5_STATE))
        lru_l.append(_rows_to_state(lru_fin, LRU_WIDTH))

        sre = state_ssm_re[:, l].reshape(Bd, 2, S5_WIDTH)
        sim = state_ssm_im[:, l].reshape(Bd, 2, S5_WIDTH)
        s5_h0 = _state_to_rows(jnp.stack([sre[:, 0], sim[:, 0], sre[:, 1], sim[:, 1]], axis=1))
        lru_h0 = _state_to_rows(state_lru[:, l])
        cache = (cache_attn_k[:, l].reshape(Bd, P, 128).astype(BF16),
                 cache_attn_v[:, l].reshape(Bd, P, 128).astype(BF16),
                 cache_mla_ckv[:, l],
                 jnp.pad(cache_mla_krope[:, l], ((0, 0), (0, 0), (MLA_NOPE, LANES - MLA_QK))))
        xs, _, _, _ = _mix(xs, lat_mod[0:3], lw, rope_a + rope_m, cache, s5_h0, lru_h0, False)
        xs = _moe(xs, lat_mod[4], lat_mod[3], lat_mod[5], lw)

    stack = lambda ts: jnp.stack(ts, axis=1)
    return (xp, xs, stack(ak_l), stack(av_l), stack(ckv_l), stack(kr_l), stack(sr_l), stack(si_l), stack(lru_l))
```

```python
import functools
import math

import jax
import jax.numpy as jnp
import numpy as np
from jax import lax
from jax.experimental import pallas as pl
from jax.experimental.pallas import tpu as pltpu

F32 = jnp.float32
BF16 = jnp.bfloat16

D_MODEL = 1024
DEPTH = 2
GRID_W = 64
N_BRANCH = 4
BRANCH_W = 256
ROPE_BASE = 10000.0
EPS = 1e-6
NEG_INF = -1e30
LOG2_E = math.log2(math.e)
A_HEADS = 4
A_KV_HEADS = 2
A_HEAD_DIM = 64
WINDOW = 128
Q_BLOCK = 128
S5_GROUP = 16
S5_GROUPS = 16
S5_STATE = 64
S5_WIDTH = S5_GROUPS * S5_STATE
LRU_WIDTH = 256
LRU_BLOCKS = 4
LRU_CONV = 4
LRU_C = 8.0
MLA_HEADS = 4
MLA_Q_LORA = 256
MLA_KV_LORA = 128
MLA_NOPE = 64
MLA_ROPE = 32
MLA_V = 64
MLA_QK = MLA_NOPE + MLA_ROPE
N_GROUPS = 4
EXPERTS_PER_GROUP = 4
N_EXPERTS = 16
EXPERT_FF = 256

LANES = 128
SUBLANES = 8
VMEM_LIMIT = 56 * 1024 * 1024

TOK_TILE = 256
TOK_STEPS = TOK_TILE // SUBLANES
SCAN_STEPS = 128
MOE_SORT_TILE = 256
MOE_ROW_ALIGN = 16
MOE_SORTED_ROWS = MOE_SORT_TILE + LANES
MOE_SORT_SUBTILES = 8
MOE_BLOCK_TILES = 8
MOE_CHUNK = 512
MLA_Q_TILE = 512
ATTN_Q_TILE = 256
Z_COLS = 1792
HEAD_ORDER = np.array((0, 2, 1, 3))


def _params(sem):
    return pltpu.CompilerParams(dimension_semantics=sem, vmem_limit_bytes=VMEM_LIMIT)


def _layer_spec(layer):
    def spec(shape):
        n = len(shape)
        return pl.BlockSpec((None,) + tuple(shape), lambda *_: (layer,) + (0,) * n)
    return spec


def _modnorm(x, g, sc, sh):
    ms = jnp.mean(x * x, axis=-1, keepdims=True)
    return (x * lax.rsqrt(ms + EPS)) * g * (1.0 + sc) + sh


def _rmsnorm(x, g):
    ms = jnp.mean(x * x, axis=-1, keepdims=True)
    return (x * lax.rsqrt(ms + EPS)) * g


def _half_rmsnorm(x):
    lane = lax.broadcasted_iota(jnp.int32, (1, LANES), 1)
    lo = lane < A_HEAD_DIM
    outs = []
    for t in range(x.shape[1] // LANES):
        xt = x[:, LANES * t:LANES * (t + 1)]
        sq = xt * xt
        s_lo = jnp.sum(jnp.where(lo, sq, 0.0), axis=-1, keepdims=True)
        s_hi = jnp.sum(jnp.where(lo, 0.0, sq), axis=-1, keepdims=True)
        inv = jnp.where(lo, lax.rsqrt(s_lo * (1.0 / A_HEAD_DIM) + EPS),
                        lax.rsqrt(s_hi * (1.0 / A_HEAD_DIM) + EPS))
        outs.append(xt * inv)
    return jnp.concatenate(outs, axis=1)


def _tile_rmsnorm(x, n_real):
    outs = []
    for t in range(x.shape[1] // LANES):
        xt = x[:, LANES * t:LANES * (t + 1)]
        ss = jnp.sum(xt * xt, axis=-1, keepdims=True)
        outs.append(xt * lax.rsqrt(ss * (1.0 / n_real) + EPS))
    return jnp.concatenate(outs, axis=1)


def _rope(x, cos, sin_next, sin_prev, quarter):
    width = x.shape[1]
    return (x * cos + pltpu.roll(x, width - quarter, 1) * sin_next
            + pltpu.roll(x, quarter, 1) * sin_prev)


def _mla_kv(ckv_n, kr_placed, wukv, gk, rope_tabs):
    kv = jnp.dot(ckv_n.astype(BF16), wukv, preferred_element_type=F32)
    gain = gk[:, :LANES]
    rotated = None
    if rope_tabs is not None:
        cos, sin_next, sin_prev = rope_tabs
        quarter = MLA_ROPE // 4
        base = kr_placed * gain
        rotated = pltpu.roll(base, LANES - quarter, 1) * sin_next + pltpu.roll(base, quarter, 1) * sin_prev
    heads = []
    for h in range(MLA_HEADS):
        kf = kv[:, LANES * h:LANES * (h + 1)] + kr_placed
        inv = lax.rsqrt(jnp.sum(kf * kf, axis=-1, keepdims=True) * (1.0 / MLA_QK) + EPS)
        kh = kf * gain
        if rotated is not None:
            kh = kh * cos + rotated
        heads.append(kh * inv)
    return jnp.concatenate(heads, axis=1), kv[:, MLA_HEADS * LANES:]


def _mod_kernel(c_ref, w_ref, b_ref, o_ref):
    c = c_ref[...]
    s = c * jax.nn.sigmoid(c)
    o_ref[...] = jnp.dot(s.astype(BF16), w_ref[...].astype(BF16), preferred_element_type=F32) + b_ref[...]


def _modulation(cond, w_mod, b_mod):
    n_rows = cond.shape[0]
    n_out = w_mod.shape[-1]
    tn = 1024
    return pl.pallas_call(
        _mod_kernel,
        grid=(DEPTH, n_out // tn),
        in_specs=[pl.BlockSpec((n_rows, D_MODEL), lambda l, j: (0, 0)),
                  pl.BlockSpec((None, D_MODEL, tn), lambda l, j: (l, 0, j)),
                  pl.BlockSpec((None, 1, tn), lambda l, j: (l, 0, j))],
        out_specs=pl.BlockSpec((None, n_rows, tn), lambda l, j: (l, 0, j)),
        out_shape=jax.ShapeDtypeStruct((DEPTH, n_rows, n_out), F32),
        compiler_params=_params(("arbitrary", "arbitrary")),
        name="modulation",
    )(cond, w_mod, b_mod)


def _group_rows(ref):
    v = ref[...]
    return v.reshape(v.shape[0] * v.shape[1], v.shape[2])


def _mod_rows(ref, steps):
    v = ref[...]
    if v.shape[0] == 1:
        return v[0]
    return jnp.broadcast_to(v, (v.shape[0], steps, v.shape[2])).reshape(v.shape[0] * steps, v.shape[2])


def _store_group(ref, val):
    ref[...] = val.reshape(ref.shape).astype(ref.dtype)


def _store_time_major(ref, val):
    steps = ref.shape[1] // SUBLANES
    for b in range(SUBLANES):
        for half in range(2):
            ref[half, pl.ds(b, steps, stride=SUBLANES), :] = (
                val[b * steps:(b + 1) * steps, LANES * half:LANES * (half + 1)])


def _load_time_major(ref):
    steps = ref.shape[1] // SUBLANES
    return jnp.concatenate(
        [jnp.concatenate([ref[half, pl.ds(b, steps, stride=SUBLANES), :] for half in range(2)], axis=1)
         for b in range(SUBLANES)], axis=0)


def _in_proj_kernel(rope, ctx, *refs):
    it = iter(refs)
    x_ref, sc_ref, sh_ref, g1_ref, wz_ref = (next(it) for _ in range(5))
    gq_ref, gk_ref, qlg_ref, wuq_ref, gmq_ref, kvg_ref, wukv_ref, gmk_ref = (next(it) for _ in range(8))
    if rope:
        ca_ref, sna_ref, spa_ref, cm_ref, snm_ref, spm_ref = (next(it) for _ in range(6))
    q_ref, k_ref, v_ref, s5u_ref, lx_ref, lg_ref, qm_ref, km_ref, vm_ref = (next(it) for _ in range(9))
    if ctx:
        k32_ref, v32_ref, ckv_ref, krp_ref = (next(it) for _ in range(4))

    steps = x_ref.shape[1]
    h = _modnorm(_group_rows(x_ref), g1_ref[...], _mod_rows(sc_ref, steps), _mod_rows(sh_ref, steps))
    z = jnp.dot(h.astype(BF16), wz_ref[...], preferred_element_type=F32)
    per_seq = lambda ref: jnp.concatenate([ref[...]] * SUBLANES, axis=0)

    q = _half_rmsnorm(z[:, 0:256]) * gq_ref[...]
    k = _half_rmsnorm(z[:, 256:384]) * gk_ref[...]
    v = z[:, 384:512]
    if ctx:
        _store_group(k32_ref, k)
        _store_group(v32_ref, v)
    if rope:
        quarter = A_HEAD_DIM // 4
        atabs = (per_seq(ca_ref), per_seq(sna_ref), per_seq(spa_ref))
        q = _rope(q, *atabs, quarter)
        k = _rope(k, *(t[:, :LANES] for t in atabs), quarter)
    _store_group(q_ref, q)
    _store_group(k_ref, k)
    _store_group(v_ref, v)

    _store_time_major(s5u_ref, z[:, 512:768])
    _store_time_major(lx_ref, z[:, 768:1024])
    _store_group(lg_ref, z[:, 1024:1280])

    mtabs = mtabs4 = None
    if rope:
        mtabs = tuple(per_seq(r) for r in (cm_ref, snm_ref, spm_ref))
        mtabs4 = tuple(jnp.concatenate([t] * MLA_HEADS, axis=1) for t in mtabs)
    ql = _rmsnorm(z[:, 1280:1536], qlg_ref[...])
    qm = jnp.dot(ql.astype(BF16), wuq_ref[...], preferred_element_type=F32)
    qm = _tile_rmsnorm(qm, MLA_QK) * gmq_ref[...]
    if rope:
        qm = _rope(qm, *mtabs4, MLA_ROPE // 4)
    _store_group(qm_ref, qm)

    ckv_n = _rmsnorm(z[:, 1536:1664], kvg_ref[...])
    krp = z[:, 1664:1792]
    km, vm = _mla_kv(ckv_n, krp, wukv_ref[...], gmk_ref[...], mtabs)
    _store_group(km_ref, km)
    _store_group(vm_ref, vm)
    if ctx:
        _store_group(ckv_ref, ckv_n)
        _store_group(krp_ref, krp)


def _group_specs(per_seq_mod):
    TS = TOK_STEPS
    tok = lambda c: pl.BlockSpec((SUBLANES, TS, c), lambda g, i: (g, i, 0))
    scan = pl.BlockSpec((None, 2, TS * SUBLANES, LANES), lambda g, i: (g, 0, i, 0))
    if per_seq_mod:
        mod = pl.BlockSpec((SUBLANES, 1, D_MODEL), lambda g, i: (g, 0, 0))
    else:
        mod = pl.BlockSpec((1, 1, D_MODEL), lambda g, i: (0, 0, 0))
    return tok, scan, mod


def _in_proj(x, sc, sh, lw, rope_tabs, ctx):
    wspec = _layer_spec(lw["layer"])
    B, L, _ = x.shape
    Bg = B // SUBLANES
    TS = TOK_STEPS
    rope = rope_tabs is not None
    tok, scan, mod = _group_specs(sc.shape[0] > 1)
    in_specs = [tok(D_MODEL), mod, mod, wspec((1, D_MODEL)), wspec((D_MODEL, Z_COLS)),
                wspec((1, 256)), wspec((1, 128)), wspec((1, 256)), wspec((256, 512)), wspec((1, 512)),
                wspec((1, 128)), wspec((128, 768)), wspec((1, 512))]
    args = [x, sc, sh, lw["g1"], lw["wz"], lw["gq"], lw["gk"], lw["qlg"], lw["wuq"], lw["gmq"],
            lw["kvg"], lw["wukv"], lw["gmk"]]
    if rope:
        in_specs += [pl.BlockSpec((TS, 256), lambda g, i: (i, 0))] * 3
        in_specs += [pl.BlockSpec((TS, 128), lambda g, i: (i, 0))] * 3
        args += list(rope_tabs)
    out_specs = [tok(256), tok(128), tok(128), scan, scan, tok(256), tok(512), tok(512), tok(256)]
    sds = jax.ShapeDtypeStruct
    scan_shape = sds((Bg, 2, L * SUBLANES, LANES), F32)
    out_shape = [sds((B, L, 256), BF16), sds((B, L, 128), BF16), sds((B, L, 128), BF16),
                 scan_shape, scan_shape,
                 sds((B, L, 256), F32), sds((B, L, 512), BF16), sds((B, L, 512), BF16),
                 sds((B, L, 256), BF16)]
    if ctx:
        out_specs += [tok(128), tok(128), tok(128), tok(128)]
        out_shape += [sds((B, L, 128), F32)] * 4
    return pl.pallas_call(
        functools.partial(_in_proj_kernel, rope, ctx),
        grid=(Bg, L // TS),
        in_specs=in_specs, out_specs=out_specs, out_shape=out_shape,
        compiler_params=_params(("parallel", "parallel")),
        name="in_proj_ctx" if ctx else "in_proj_lat",
    )(*args)


def _mla_cache_kernel(ckv_ref, krp_ref, wukv_ref, gmk_ref, km_ref, vm_ref):
    km, vm = _mla_kv(ckv_ref[...], krp_ref[...], wukv_ref[...], gmk_ref[...], None)
    km_ref[...] = km.astype(BF16)
    vm_ref[...] = vm.astype(BF16)


def _mla_cache_kv(cckv, ckr_placed, lw):
    wspec = _layer_spec(lw["layer"])
    B, P, _ = cckv.shape
    tok = lambda c: pl.BlockSpec((None, P, c), lambda b: (b, 0, 0))
    return pl.pallas_call(
        _mla_cache_kernel,
        grid=(B,),
        in_specs=[tok(128), tok(128), wspec((128, 768)), wspec((1, 512))],
        out_specs=[tok(512), tok(256)],
        out_shape=[jax.ShapeDtypeStruct((B, P, 512), BF16), jax.ShapeDtypeStruct((B, P, 256), BF16)],
        compiler_params=_params(("parallel",)),
        name="mla_cache_kv",
    )(cckv, ckr_placed, lw["wukv"], lw["gmk"])


def _sink_softmax(s, sink):
    sink = sink * LOG2_E
    m = jnp.maximum(jnp.max(s, axis=-1, keepdims=True), sink)
    e = jnp.exp2(s - m)
    den = jnp.sum(e, axis=-1, keepdims=True) + jnp.exp2(sink - m)
    return e.astype(BF16), 1.0 / den


def _gqa_tile(qt, keys, vst, sink_lo, sink_hi, mask):
    lane = lax.broadcasted_iota(jnp.int32, (1, LANES), 1)
    lo = lane < A_HEAD_DIM
    zero = jnp.zeros_like(qt)
    ps, invs = [], []
    for qh, sink in ((jnp.where(lo, qt, zero), sink_lo), (jnp.where(lo, zero, qt), sink_hi)):
        s = lax.dot_general(qh, keys, (((1,), (1,)), ((), ())), preferred_element_type=F32)
        if mask is not None:
            s = jnp.where(mask, s, NEG_INF)
        p, inv = _sink_softmax(s, sink)
        ps.append(p)
        invs.append(inv)
    o = jnp.dot(jnp.concatenate(ps, axis=1), vst, preferred_element_type=F32)
    return o * jnp.where(lo, invs[0], invs[1])


def _stack_kv_halves(v):
    lane = lax.broadcasted_iota(jnp.int32, (1, LANES), 1)
    lo = lane < A_HEAD_DIM
    zero = jnp.zeros_like(v)
    return jnp.concatenate([jnp.where(lo, v, zero), jnp.where(lo, zero, v)], axis=0)


def _attn_a_ctx_kernel(sink_ref, q_ref, k_ref, v_ref, o_ref):
    keys = k_ref[...]
    vst = _stack_kv_halves(v_ref[...])
    for t in range(2):
        o = _gqa_tile(q_ref[:, LANES * t:LANES * (t + 1)], keys, vst, sink_ref[t], sink_ref[2 + t], None)
        o_ref[:, LANES * t:LANES * (t + 1)] = o.astype(o_ref.dtype)


def _attn_a_ctx(q, k, v, sink):
    B, L, _ = q.shape
    tok = lambda c: pl.BlockSpec((None, L, c), lambda b: (b, 0, 0))
    return pl.pallas_call(
        _attn_a_ctx_kernel,
        grid=(B,),
        in_specs=[pl.BlockSpec(memory_space=pltpu.SMEM), tok(256), tok(128), tok(128)],
        out_specs=tok(256),
        out_shape=jax.ShapeDtypeStruct((B, L, 256), BF16),
        compiler_params=_params(("parallel",)),
        name="attn_a_ctx",
    )(sink, q, k, v)


def _attn_a_lat_kernel(sink_ref, q_ref, kc_ref, vc_ref, k_ref, v_ref, o_ref):
    i = pl.program_id(1)
    L = k_ref.shape[0]
    n_ctx = kc_ref.shape[0]
    rows = q_ref.shape[0]
    span = rows + 2 * WINDOW
    start = pl.multiple_of(jnp.clip(i * rows - WINDOW, 0, L - span), WINDOW)
    keys = jnp.concatenate([kc_ref[...], k_ref[pl.ds(start, span), :]], axis=0)
    vals = jnp.concatenate([vc_ref[...], v_ref[pl.ds(start, span), :]], axis=0)
    vst = _stack_kv_halves(vals)
    col = lax.broadcasted_iota(jnp.int32, (rows, n_ctx + span), 1)
    row = lax.broadcasted_iota(jnp.int32, (rows, n_ctx + span), 0)
    rel = (col - n_ctx + start) - (row + i * rows)
    mask = (col < n_ctx) | (jnp.abs(rel) <= WINDOW)
    for t in range(2):
        o = _gqa_tile(q_ref[:, LANES * t:LANES * (t + 1)], keys, vst, sink_ref[t], sink_ref[2 + t], mask)
        o_ref[:, LANES * t:LANES * (t + 1)] = o.astype(o_ref.dtype)


def _attn_a_lat(q, k, v, kc, vc, sink):
    B, L, _ = q.shape
    P = kc.shape[1]
    blk = lambda c: pl.BlockSpec((None, ATTN_Q_TILE, c), lambda b, i: (b, i, 0))
    whole = lambda n, c: pl.BlockSpec((None, n, c), lambda b, i: (b, 0, 0))
    return pl.pallas_call(
        _attn_a_lat_kernel,
        grid=(B, L // ATTN_Q_TILE),
        in_specs=[pl.BlockSpec(memory_space=pltpu.SMEM), blk(256), whole(P, 128), whole(P, 128),
                  whole(L, 128), whole(L, 128)],
        out_specs=blk(256),
        out_shape=jax.ShapeDtypeStruct((B, L, 256), BF16),
        compiler_params=_params(("parallel", "parallel")),
        name="attn_a_lat",
    )(sink, q, kc, vc, k, v)


def _mla_stack_values(vals):
    lane = lax.broadcasted_iota(jnp.int32, (1, MLA_HEADS * MLA_V), 1)
    zero = jnp.zeros_like(vals)
    return jnp.concatenate([jnp.where((lane >= MLA_V * h) & (lane < MLA_V * (h + 1)), vals, zero)
                            for h in range(MLA_HEADS)], axis=0)


def _mla_attend(q, key_parts, vst):
    lane = lax.broadcasted_iota(jnp.int32, (1, MLA_HEADS * MLA_V), 1)
    n_keys = vst.shape[0] // MLA_HEADS
    out = jnp.zeros((q.shape[0], MLA_HEADS * MLA_V), F32)
    inv = jnp.zeros((q.shape[0], MLA_HEADS * MLA_V), F32)
    for h in range(MLA_HEADS):
        qh = q[:, LANES * h:LANES * (h + 1)]
        ss = [lax.dot_general(qh, kp[:, LANES * h:LANES * (h + 1)], (((1,), (1,)), ((), ())),
                              preferred_element_type=F32) for kp in key_parts]
        m = functools.reduce(jnp.maximum, [jnp.max(s, axis=-1, keepdims=True) for s in ss])
        es = [jnp.exp2(s - m) for s in ss]
        den = functools.reduce(lambda a, b: a + b, [jnp.sum(e, axis=-1, keepdims=True) for e in es])
        p = jnp.concatenate([e.astype(BF16) for e in es], axis=1) if len(es) > 1 else es[0].astype(BF16)
        out = out + jnp.dot(p, vst[n_keys * h:n_keys * (h + 1)], preferred_element_type=F32)
        inv = jnp.where((lane >= MLA_V * h) & (lane < MLA_V * (h + 1)), 1.0 / den, inv)
    return out * inv


def _mla_ctx_kernel(q_ref, k_ref, v_ref, o_ref):
    o_ref[...] = _mla_attend(q_ref[...], [k_ref[...]], _mla_stack_values(v_ref[...])).astype(o_ref.dtype)


def _mla_ctx(q, k, v):
    B, L, _ = q.shape
    tok = lambda c: pl.BlockSpec((None, L, c), lambda b: (b, 0, 0))
    return pl.pallas_call(
        _mla_ctx_kernel,
        grid=(B,),
        in_specs=[tok(512), tok(512), tok(256)],
        out_specs=tok(256),
        out_shape=jax.ShapeDtypeStruct((B, L, 256), BF16),
        compiler_params=_params(("parallel",)),
        name="mla_ctx",
    )(q, k, v)


def _mla_lat_kernel(q_ref, kc_ref, vc_ref, k_ref, v_ref, o_ref, vst_ref):
    @pl.when(pl.program_id(1) == 0)
    def _():
        vst_ref[...] = _mla_stack_values(jnp.concatenate([vc_ref[...], v_ref[...]], axis=0))

    o = _mla_attend(q_ref[...], [kc_ref[...], k_ref[...]], vst_ref)
    o_ref[...] = o.astype(o_ref.dtype)


def _mla_lat(q, k, v, kc, vc):
    B, L, _ = q.shape
    P = kc.shape[1]
    TQ = MLA_Q_TILE
    blk = lambda c: pl.BlockSpec((None, TQ, c), lambda b, i: (b, i, 0))
    whole = lambda n, c: pl.BlockSpec((None, n, c), lambda b, i: (b, 0, 0))
    return pl.pallas_call(
        _mla_lat_kernel,
        grid=(B, L // TQ),
        in_specs=[blk(512), whole(P, 512), whole(P, 256), whole(L, 512), whole(L, 256)],
        out_specs=blk(256),
        out_shape=jax.ShapeDtypeStruct((B, L, 256), BF16),
        scratch_shapes=[pltpu.VMEM((MLA_HEADS * (P + L), MLA_HEADS * MLA_V), BF16)],
        compiler_params=_params(("parallel", "arbitrary")),
        name="mla_lat",
    )(q, kc, vc, k, v)


def _join_halves(ref):
    return jnp.concatenate([ref[0], ref[1]], axis=1)


def _split_halves(ref, val):
    ref[0] = val[:, :LANES]
    ref[1] = val[:, LANES:]


def _s5_kernel(uf_ref, ub_ref, bf_ref, bb_ref, a_ref, h0_ref, cf_ref, cb_ref,
               yf_ref, yb_ref, fin_ref, s_ref, st_ref):
    i = pl.program_id(1)
    steps = SCAN_STEPS

    @pl.when(i == 0)
    def _():
        st_ref[...] = h0_ref[...]

    uf = _join_halves(uf_ref).astype(BF16)
    ub = _join_halves(ub_ref).astype(BF16)
    s_ref[0] = jnp.dot(uf, bf_ref[:, :S5_WIDTH], preferred_element_type=F32)
    s_ref[1] = jnp.dot(uf, bf_ref[:, S5_WIDTH:], preferred_element_type=F32)
    s_ref[2] = jnp.dot(ub, bb_ref[:, :S5_WIDTH], preferred_element_type=F32)
    s_ref[3] = jnp.dot(ub, bb_ref[:, S5_WIDTH:], preferred_element_type=F32)

    lane_chunk = 2 * LANES
    for c in range(S5_WIDTH // lane_chunk):
        sl = slice(lane_chunk * c, lane_chunk * (c + 1))
        arf, aif, arb, aib = (a_ref[n, :, sl] for n in range(4))

        def body(j, carry):
            hrf, hif, hrb, hib = carry
            rf = pl.multiple_of(j * SUBLANES, SUBLANES)
            rb = pl.multiple_of((steps - 1 - j) * SUBLANES, SUBLANES)
            nrf = arf * hrf - aif * hif + s_ref[0, pl.ds(rf, SUBLANES), sl]
            nif = arf * hif + aif * hrf + s_ref[1, pl.ds(rf, SUBLANES), sl]
            nrb = arb * hrb - aib * hib + s_ref[2, pl.ds(rb, SUBLANES), sl]
            nib = arb * hib + aib * hrb + s_ref[3, pl.ds(rb, SUBLANES), sl]
            s_ref[0, pl.ds(rf, SUBLANES), sl] = nrf
            s_ref[1, pl.ds(rf, SUBLANES), sl] = nif
            s_ref[2, pl.ds(rb, SUBLANES), sl] = nrb
            s_ref[3, pl.ds(rb, SUBLANES), sl] = nib
            return nrf, nif, nrb, nib

        fin = lax.fori_loop(0, steps, body, tuple(st_ref[n, :, sl] for n in range(4)), unroll=4)
        for n in range(4):
            st_ref[n, :, sl] = fin[n]

    hf = jnp.concatenate([s_ref[0].astype(BF16), s_ref[1].astype(BF16)], axis=1)
    _split_halves(yf_ref, jnp.dot(hf, cf_ref[...], preferred_element_type=F32))
    hb = jnp.concatenate([s_ref[2].astype(BF16), s_ref[3].astype(BF16)], axis=1)
    _split_halves(yb_ref, jnp.dot(hb, cb_ref[...], preferred_element_type=F32))
    fin_ref[...] = st_ref[...]


def _s5_scan(u_rows, lw, h0):
    wspec = _layer_spec(lw["layer"])
    Bg, _, rows, _ = u_rows.shape
    R = SCAN_STEPS * SUBLANES
    n = rows // R
    fwd = pl.BlockSpec((None, 2, R, LANES), lambda g, i: (g, 0, i, 0))
    bwd = pl.BlockSpec((None, 2, R, LANES), lambda g, i: (g, 0, n - 1 - i, 0))
    st = pl.BlockSpec((None, 4, SUBLANES, S5_WIDTH), lambda g, i: (g, 0, 0, 0))
    return pl.pallas_call(
        _s5_kernel,
        grid=(Bg, n),
        in_specs=[fwd, bwd, wspec((256, 2 * S5_WIDTH)), wspec((256, 2 * S5_WIDTH)),
                  wspec((4, SUBLANES, S5_WIDTH)), st, wspec((2 * S5_WIDTH, 256)), wspec((2 * S5_WIDTH, 256))],
        out_specs=[fwd, bwd, st],
        out_shape=[jax.ShapeDtypeStruct(u_rows.shape, F32), jax.ShapeDtypeStruct(u_rows.shape, F32),
                   jax.ShapeDtypeStruct((Bg, 4, SUBLANES, S5_WIDTH), F32)],
        scratch_shapes=[pltpu.VMEM((4, R, S5_WIDTH), F32), pltpu.VMEM((4, SUBLANES, S5_WIDTH), F32)],
        compiler_params=_params(("parallel", "arbitrary")),
        name="s5_scan",
    )(u_rows, u_rows, lw["s5_bf"], lw["s5_bb"], lw["s5_a"], h0, lw["s5_cf"], lw["s5_cb"])


def _lru_gates(x_ref, pre_ref, post_ref, has_pre, has_post, cw_ref, cb_ref, w_ref, b_ref, sp_ref, a_ref, h_ref):
    R = x_ref.shape[1]
    pre = jnp.where(has_pre, _join_halves(pre_ref), 0.0)
    post = jnp.where(has_post, _join_halves(post_ref), 0.0)
    xp = jnp.concatenate([pre, _join_halves(x_ref), post], axis=0)
    xc = cb_ref[...]
    for t in range(LRU_CONV):
        xc = xc + xp[SUBLANES * t:SUBLANES * t + R] * cw_ref[t:t + 1, :]
    g = jnp.dot(xc.astype(BF16), w_ref[...], preferred_element_type=F32) + b_ref[...]
    r = jax.nn.sigmoid(g[:, :LRU_WIDTH])
    ig = jax.nn.sigmoid(g[:, LRU_WIDTH:])
    log_a = (-LRU_C) * r * sp_ref[...]
    a = jnp.exp(log_a)
    a_ref[...] = a
    h_ref[...] = jnp.sqrt(1.0 - a * a) * (ig * xc)


def _lru_kernel(xf_ref, xfp_ref, xfn_ref, xb_ref, xbp_ref, xbn_ref, cw_ref, cb_ref, wf_ref, wb_ref,
                bf_ref, bb_ref, spf_ref, spb_ref, h0_ref, of_ref, ob_ref, fin_ref,
                af_ref, ab_ref, hf_ref, hb_ref, st_ref):
    i = pl.program_id(1)
    n = pl.num_programs(1)
    steps = SCAN_STEPS

    @pl.when(i == 0)
    def _():
        st_ref[...] = h0_ref[...]

    _lru_gates(xf_ref, xfp_ref, xfn_ref, i > 0, i < n - 1, cw_ref, cb_ref, wf_ref, bf_ref, spf_ref, af_ref, hf_ref)
    _lru_gates(xb_ref, xbp_ref, xbn_ref, i < n - 1, i > 0, cw_ref, cb_ref, wb_ref, bb_ref, spb_ref, ab_ref, hb_ref)

    def body(j, carry):
        hf, hb = carry
        rf = pl.multiple_of(j * SUBLANES, SUBLANES)
        rb = pl.multiple_of((steps - 1 - j) * SUBLANES, SUBLANES)
        nf = af_ref[pl.ds(rf, SUBLANES), :] * hf + hf_ref[pl.ds(rf, SUBLANES), :]
        nb = ab_ref[pl.ds(rb, SUBLANES), :] * hb + hb_ref[pl.ds(rb, SUBLANES), :]
        hf_ref[pl.ds(rf, SUBLANES), :] = nf
        hb_ref[pl.ds(rb, SUBLANES), :] = nb
        return nf, nb

    ff, fb = lax.fori_loop(0, steps, body, (st_ref[0], st_ref[1]), unroll=8)
    st_ref[0] = ff
    st_ref[1] = fb
    fin_ref[...] = st_ref[...]
    _split_halves(of_ref, hf_ref[...])
    _split_halves(ob_ref, hb_ref[...])


def _lru_scan(x_rows, lw, h0):
    wspec = _layer_spec(lw["layer"])
    Bg, _, rows, _ = x_rows.shape
    R = SCAN_STEPS * SUBLANES
    n = rows // R
    pre_rows = 2 * SUBLANES
    fwd = lambda g, i: i
    bwd = lambda g, i: n - 1 - i
    blk = lambda m: pl.BlockSpec((None, 2, R, LANES), lambda g, i: (g, 0, m(g, i), 0))
    pre = lambda m: pl.BlockSpec(
        (None, 2, pre_rows, LANES), lambda g, i: (g, 0, jnp.maximum(m(g, i) * (R // pre_rows) - 1, 0), 0))
    post = lambda m: pl.BlockSpec(
        (None, 2, SUBLANES, LANES),
        lambda g, i: (g, 0, jnp.minimum((m(g, i) + 1) * (R // SUBLANES), rows // SUBLANES - 1), 0))
    st = pl.BlockSpec((None, 2, SUBLANES, 256), lambda g, i: (g, 0, 0, 0))
    return pl.pallas_call(
        _lru_kernel,
        grid=(Bg, n),
        in_specs=[blk(fwd), pre(fwd), post(fwd), blk(bwd), pre(bwd), post(bwd),
                  wspec((LRU_CONV, 256)), wspec((1, 256)), wspec((256, 512)), wspec((256, 512)),
                  wspec((1, 512)), wspec((1, 512)), wspec((1, 256)), wspec((1, 256)), st],
        out_specs=[blk(fwd), blk(bwd), st],
        out_shape=[jax.ShapeDtypeStruct(x_rows.shape, F32), jax.ShapeDtypeStruct(x_rows.shape, F32),
                   jax.ShapeDtypeStruct((Bg, 2, SUBLANES, 256), F32)],
        scratch_shapes=[pltpu.VMEM((R, 256), F32), pltpu.VMEM((R, 256), F32), pltpu.VMEM((R, 256), F32),
                        pltpu.VMEM((R, 256), F32), pltpu.VMEM((2, SUBLANES, 256), F32)],
        compiler_params=_params(("parallel", "arbitrary")),
        name="lru_scan",
    )(x_rows, x_rows, x_rows, x_rows, x_rows, x_rows, lw["lru_cw"], lw["lru_cb"], lw["lru_wf"], lw["lru_wb"],
      lw["lru_bf"], lw["lru_bb"], lw["lru_spf"], lw["lru_spb"], h0)


def _merge_kernel(x_ref, sc_ref, sh_ref, gt_ref, g1_ref, wg_ref, oa_ref, u_ref, yf_ref, yb_ref, d_ref,
                  wglu_ref, hf_ref, hb_ref, lg_ref, od_ref, wb_ref, wo_ref, o_ref):
    steps = x_ref.shape[1]
    x = _group_rows(x_ref)
    h = _modnorm(x, g1_ref[...], _mod_rows(sc_ref, steps), _mod_rows(sh_ref, steps)).astype(BF16)

    yb5 = jax.nn.gelu(d_ref[...] * _load_time_major(u_ref) + _load_time_major(yf_ref) + _load_time_major(yb_ref))
    gv = jnp.dot(yb5.astype(BF16), wglu_ref[...], preferred_element_type=F32)
    o_b = gv[:, :BRANCH_W] * jax.nn.sigmoid(gv[:, BRANCH_W:])
    o_c = (_load_time_major(hf_ref) + _load_time_major(hb_ref)) * jax.nn.gelu(_group_rows(lg_ref))
    branches = (_group_rows(oa_ref), o_b.astype(BF16), o_c.astype(BF16), _group_rows(od_ref))

    acc = jnp.zeros(x.shape, F32)
    for n in range(N_BRANCH):
        gate = jnp.dot(h, wg_ref[:, D_MODEL * n:D_MODEL * (n + 1)], preferred_element_type=F32)
        proj = jnp.dot(branches[n], wb_ref[n], preferred_element_type=F32)
        acc = acc + jax.nn.sigmoid(gate) * proj
    out = jnp.dot(acc.astype(BF16), wo_ref[...], preferred_element_type=F32)
    _store_group(o_ref, x + _mod_rows(gt_ref, steps) * out)


def _merge(x, sc, sh, gt, lw, o_a, s5u_t, yf_t, yb_t, hf_t, hb_t, lg, o_d):
    wspec = _layer_spec(lw["layer"])
    B, L, _ = x.shape
    tok, scan, mod = _group_specs(sc.shape[0] > 1)
    return pl.pallas_call(
        _merge_kernel,
        grid=(B // SUBLANES, L // TOK_STEPS),
        in_specs=[tok(D_MODEL), mod, mod, mod, wspec((1, D_MODEL)), wspec((D_MODEL, N_BRANCH * D_MODEL)),
                  tok(256), scan, scan, scan, wspec((1, 256)), wspec((256, 512)),
                  scan, scan, tok(256), tok(256), wspec((N_BRANCH, BRANCH_W, D_MODEL)),
                  wspec((D_MODEL, D_MODEL))],
        out_specs=tok(D_MODEL),
        out_shape=jax.ShapeDtypeStruct((B, L, D_MODEL), F32),
        compiler_params=_params(("parallel", "parallel")),
        name="merge",
    )(x, sc, sh, gt, lw["g1"], lw["wg"], o_a, s5u_t, yf_t, yb_t, lw["s5_d"], lw["wglu"],
      hf_t, hb_t, lg, o_d, lw["wb"], lw["wo"])


def _first_index(values, target):
    idx = jnp.full_like(target, float(len(values) - 1))
    for n in range(len(values) - 2, -1, -1):
        idx = jnp.where(values[n] == target, float(n), idx)
    return idx


def _list_max(values):
    return functools.reduce(jnp.maximum, values)


def _moe_sort_kernel(x_ref, sc_ref, sh_ref, g2_ref, wrh_ref, wrl_ref, br_ref, tri_ref,
                     hs_ref, gs_ref, dcol_ref, meta_ref):
    for s in range(x_ref.shape[0] // MOE_SORT_TILE):
        tok = pl.ds(s * MOE_SORT_TILE, MOE_SORT_TILE)
        srt = pl.ds(s * MOE_SORTED_ROWS, MOE_SORTED_ROWS)
        _moe_sort_tile(x_ref.at[tok], sc_ref, sh_ref, g2_ref, wrh_ref, wrl_ref, br_ref, tri_ref,
                       hs_ref.at[srt], gs_ref.at[srt], dcol_ref.at[tok], meta_ref.at[s])


def _moe_sort_tile(x_ref, sc_ref, sh_ref, g2_ref, wrh_ref, wrl_ref, br_ref, tri_ref,
                   hs_ref, gs_ref, dcol_ref, meta_ref):
    T = MOE_SORT_TILE
    h = _modnorm(x_ref[...], g2_ref[...], sc_ref[...], sh_ref[...])
    hh = h.astype(BF16)
    hl = (h - hh.astype(F32)).astype(BF16)
    nt = (((1,), (1,)), ((), ()))
    logits = (jnp.dot(hh, wrh_ref[...], preferred_element_type=F32)
              + jnp.dot(hl, wrh_ref[...], preferred_element_type=F32)
              + jnp.dot(hh, wrl_ref[...], preferred_element_type=F32)) + br_ref[...]
    lt = logits.T
    gl = [lt[g:g + 1, :] for g in range(N_GROUPS)]
    gmax = _list_max(gl)
    g_idx = _first_index(gl, gmax)
    pg = 1.0 / sum(jnp.exp(v - gmax) for v in gl)
    hot = [g_idx == float(g) for g in range(N_GROUPS)]
    el = []
    for e in range(EXPERTS_PER_GROUP):
        v = jnp.zeros_like(gmax)
        for g in range(N_GROUPS):
            r = N_GROUPS + EXPERTS_PER_GROUP * g + e
            v = jnp.where(hot[g], lt[r:r + 1, :], v)
        el.append(v)
    emax = _list_max(el)
    ee = [jnp.exp(v - emax) for v in el]
    esum = sum(ee)
    pe = [v / esum for v in ee]
    v1 = _list_max(pe)
    i1 = _first_index(pe, v1)
    pe2 = [jnp.where(i1 == float(e), -1.0, pe[e]) for e in range(EXPERTS_PER_GROUP)]
    v2 = _list_max(pe2)
    i2 = _first_index(pe2, v2)
    tot = v1 + v2
    w = [jnp.where(i1 == float(e), pg * v1 / tot, jnp.where(i2 == float(e), pg * v2 / tot, 0.0))
         for e in range(EXPERTS_PER_GROUP)]

    zero_row = jnp.zeros_like(gmax)
    g8 = jnp.concatenate([jnp.where(hot[g], 1.0, 0.0) for g in range(N_GROUPS)] + [zero_row] * 4, axis=0)
    cum = jnp.dot(g8.astype(BF16), tri_ref[...], preferred_element_type=F32)
    off = jnp.zeros((1, 1), F32)
    dest = zero_row
    counts = []
    for g in range(N_GROUPS):
        cnt = cum[g:g + 1, T - 1:T]
        padded = jnp.floor((cnt + (MOE_ROW_ALIGN - 1.0)) * (1.0 / MOE_ROW_ALIGN)) * MOE_ROW_ALIGN
        dest = jnp.where(hot[g], off + cum[g:g + 1, :] - 1.0, dest)
        off = off + padded
        counts.append(padded)
    rows = lax.broadcasted_iota(jnp.int32, (MOE_SORTED_ROWS, T), 0).astype(F32)
    perm = jnp.where(rows == dest, 1.0, 0.0).astype(BF16)
    hs_ref[...] = jnp.dot(perm, hh, preferred_element_type=F32).astype(BF16)
    gates = jnp.concatenate(w + [jnp.zeros((LANES - EXPERTS_PER_GROUP, T), F32)], axis=0)
    ghi = gates.astype(BF16)
    glo = (gates - ghi.astype(F32)).astype(BF16)
    gs_ref[...] = (lax.dot_general(perm, ghi, nt, preferred_element_type=F32)
                   + lax.dot_general(perm, glo, nt, preferred_element_type=F32))
    r_i = lax.broadcasted_iota(jnp.int32, (T, T), 0)
    c_i = lax.broadcasted_iota(jnp.int32, (T, T), 1)
    dcol = jnp.sum(jnp.where(r_i == c_i, jnp.broadcast_to(dest, (T, T)), 0.0), axis=1, keepdims=True)
    dcol_ref[...] = jnp.broadcast_to(dcol, (T, LANES))
    meta_ref[...] = jnp.concatenate([jnp.broadcast_to(c, (1, LANES)) for c in counts]
                                    + [jnp.zeros((SUBLANES - N_GROUPS, LANES), F32)], axis=0)


def _moe_sort(xf, sc, sh, lw, tiles_per_mod):
    wspec = _layer_spec(lw["layer"])
    N = xf.shape[0]
    T = MOE_SORT_TILE
    n_tiles = N // T
    sub = MOE_SORT_SUBTILES
    per_b = sc.shape[0] > 1
    mod = pl.BlockSpec((None, 1, D_MODEL), lambda t: (t * sub // tiles_per_mod if per_b else 0, 0, 0))
    sds = jax.ShapeDtypeStruct
    return pl.pallas_call(
        _moe_sort_kernel,
        grid=(n_tiles // sub,),
        in_specs=[pl.BlockSpec((sub * T, D_MODEL), lambda t: (t, 0)), mod, mod, wspec((1, D_MODEL)),
                  wspec((D_MODEL, LANES)), wspec((D_MODEL, LANES)), wspec((1, LANES)), wspec((T, T))],
        out_specs=[pl.BlockSpec((sub * MOE_SORTED_ROWS, D_MODEL), lambda t: (t, 0)),
                   pl.BlockSpec((sub * MOE_SORTED_ROWS, LANES), lambda t: (t, 0)),
                   pl.BlockSpec((sub * T, LANES), lambda t: (t, 0)),
                   pl.BlockSpec((sub, SUBLANES, LANES), lambda t: (t, 0, 0))],
        out_shape=[sds((n_tiles * MOE_SORTED_ROWS, D_MODEL), BF16), sds((n_tiles * MOE_SORTED_ROWS, LANES), F32),
                   sds((N, LANES), F32), sds((n_tiles, SUBLANES, LANES), F32)],
        compiler_params=_params(("parallel",)),
        name="moe_sort",
    )(xf, sc, sh, lw["g2"], lw["wr_hi"], lw["wr_lo"], lw["br"], lw["tri"])


def _moe_expert_kernel(cnt_ref, hs_ref, gs_ref, wga_ref, wup_ref, wdn_ref, ys_ref, ch_ref, cg_ref, cy_ref):
    j = pl.program_id(0)
    g = pl.program_id(1)
    piece = MOE_ROW_ALIGN

    @pl.when((j == 0) & (g == 0))
    def _():
        ch_ref[...] = jnp.zeros_like(ch_ref)
        cg_ref[...] = jnp.zeros_like(cg_ref)

    @pl.when(g == 0)
    def _():
        ys_ref[...] = jnp.zeros_like(ys_ref)

    def for_each_piece(move):
        packed = jnp.int32(0)
        for t in range(MOE_BLOCK_TILES):
            base = (j * MOE_BLOCK_TILES + t) * N_GROUPS
            start = jnp.int32(t * MOE_SORTED_ROWS)
            for g2 in range(N_GROUPS - 1):
                start = start + jnp.where(g2 < g, cnt_ref[base + g2], 0)
            n = cnt_ref[base + g]

            def body(k, carry, start=start, packed=packed):
                move(pl.multiple_of(start + k * piece, piece), pl.multiple_of(packed + k * piece, piece))
                return carry

            lax.fori_loop(0, lax.shift_right_logical(n, 4), body, 0)
            packed = packed + n
        return packed

    def pack(src, dst):
        ch_ref[pl.ds(dst, piece), :] = hs_ref[pl.ds(src, piece), :]
        cg_ref[pl.ds(dst, piece), :] = gs_ref[pl.ds(src, piece), :]

    rows = for_each_piece(pack)

    def experts(r, size):
        h = ch_ref[pl.ds(r, size), :]
        acts = []
        for e in range(EXPERTS_PER_GROUP):
            a = jnp.dot(h, wga_ref[e], preferred_element_type=F32)
            u = jnp.dot(h, wup_ref[e], preferred_element_type=F32)
            acts.append(((a * jax.nn.sigmoid(a)) * u * cg_ref[pl.ds(r, size), e:e + 1]).astype(BF16))
        y = jnp.dot(jnp.concatenate(acts, axis=1), wdn_ref[...], preferred_element_type=F32)
        cy_ref[pl.ds(r, size), :] = y.astype(BF16)

    big, mid, small = MOE_CHUNK, MOE_CHUNK // 2, MOE_CHUNK // 4
    n_big = lax.shift_right_logical(rows, big.bit_length() - 1)

    def chunk(c, carry):
        experts(pl.multiple_of(c * big, big), big)
        return carry

    lax.fori_loop(0, n_big, chunk, 0)
    tail = pl.multiple_of(n_big * big, big)
    rem = rows - tail
    pl.when(rem > mid + small)(lambda: experts(tail, big))
    pl.when((rem > small) & (rem <= mid + small))(lambda: experts(tail, mid))
    pl.when((rem > mid) & (rem <= mid + small))(lambda: experts(pl.multiple_of(tail + mid, small), small))
    pl.when((rem > 0) & (rem <= small))(lambda: experts(tail, small))

    def unpack(src, dst):
        ys_ref[pl.ds(src, piece), :] = cy_ref[pl.ds(dst, piece), :]

    for_each_piece(unpack)


def _moe_experts(cnt, hs, gs, lw):
    rows = MOE_BLOCK_TILES * MOE_SORTED_ROWS
    cap = MOE_BLOCK_TILES * MOE_SORT_TILE
    blk = lambda c: pl.BlockSpec((rows, c), lambda j, g, cnt: (j, 0))
    layer = lw["layer"]
    wspec = lambda r, c: pl.BlockSpec((None, EXPERTS_PER_GROUP, r, c), lambda j, g, cnt: (layer, g, 0, 0))
    return pl.pallas_call(
        _moe_expert_kernel,
        grid_spec=pltpu.PrefetchScalarGridSpec(
            num_scalar_prefetch=1, grid=(hs.shape[0] // rows, N_GROUPS),
            in_specs=[blk(D_MODEL), blk(LANES), wspec(D_MODEL, EXPERT_FF), wspec(D_MODEL, EXPERT_FF),
                      pl.BlockSpec((None, EXPERTS_PER_GROUP * EXPERT_FF, D_MODEL),
                                   lambda j, g, cnt: (layer, g, 0))],
            out_specs=blk(D_MODEL),
            scratch_shapes=[pltpu.VMEM((cap, D_MODEL), BF16), pltpu.VMEM((cap, LANES), F32),
                            pltpu.VMEM((cap, D_MODEL), BF16)]),
        out_shape=jax.ShapeDtypeStruct(hs.shape, BF16),
        compiler_params=_params(("arbitrary", "arbitrary")),
        name="moe_experts",
    )(cnt, hs, gs, lw["wga"], lw["wup"], lw["wdn"])


def _moe_unsort_kernel(x_ref, gt_ref, ys_ref, dcol_ref, o_ref):
    cols = lax.broadcasted_iota(jnp.int32, (MOE_SORT_TILE, MOE_SORTED_ROWS), 1).astype(F32)
    perm_t = jnp.where(cols == dcol_ref[:, 0:1], 1.0, 0.0).astype(BF16)
    y = jnp.dot(perm_t, ys_ref[...], preferred_element_type=F32)
    o_ref[...] = x_ref[...] + gt_ref[...] * y


def _moe_unsort(xf, gt, ys, dcol, tiles_per_mod):
    N = xf.shape[0]
    T = MOE_SORT_TILE
    per_b = gt.shape[0] > 1
    mod = pl.BlockSpec((None, 1, D_MODEL), lambda t: (t // tiles_per_mod if per_b else 0, 0, 0))
    tok = pl.BlockSpec((T, D_MODEL), lambda t: (t, 0))
    return pl.pallas_call(
        _moe_unsort_kernel,
        grid=(N // T,),
        in_specs=[tok, mod, pl.BlockSpec((MOE_SORTED_ROWS, D_MODEL), lambda t: (t, 0)),
                  pl.BlockSpec((T, LANES), lambda t: (t, 0))],
        out_specs=tok,
        out_shape=jax.ShapeDtypeStruct((N, D_MODEL), F32),
        compiler_params=_params(("parallel",)),
        name="moe_unsort",
    )(xf, gt, ys, dcol)


def _moe(x, sc, sh, gt, lw):
    B, L, _ = x.shape
    N = B * L
    T = MOE_SORT_TILE
    xf = x.reshape(N, D_MODEL)
    hs, gs, dcol, meta = _moe_sort(xf, sc, sh, lw, L // T)

    cnt = meta[:, :N_GROUPS, 0].astype(jnp.int32).reshape(-1)
    ys = _moe_experts(cnt, hs, gs, lw)
    out = _moe_unsort(xf, gt, ys, dcol, L // T)
    return out.reshape(B, L, D_MODEL)


def _block_diag(blocks):
    n, r, c = blocks.shape
    eye = jnp.eye(n, dtype=blocks.dtype)
    return jnp.einsum("nrc,nm->nrmc", blocks, eye).reshape(n * r, n * c)


def _rope_tables(seq_len, dim, width, offset):
    rows = seq_len // GRID_W
    r, col = np.meshgrid(np.arange(rows), np.arange(GRID_W), indexing="ij")
    r = r.reshape(-1).astype(np.float64)
    col = col.reshape(-1).astype(np.float64)
    quarter = dim // 4
    freqs = ROPE_BASE ** (-np.arange(quarter, dtype=np.float64) / quarter)
    ang_r = r[:, None] * freqs
    ang_c = col[:, None] * freqs
    zero = np.zeros_like(ang_r)
    cos = np.cos(np.concatenate([ang_r, ang_r, ang_c, ang_c], axis=-1))
    sin_next = np.concatenate([-np.sin(ang_r), zero, -np.sin(ang_c), zero], axis=-1)
    sin_prev = np.concatenate([zero, np.sin(ang_r), zero, np.sin(ang_c)], axis=-1)

    def place(t, fill):
        return np.pad(t, ((0, 0), (offset, width - offset - dim)), constant_values=fill).astype(np.float32)

    return place(cos, 1.0), place(sin_next, 0.0), place(sin_prev, 0.0)


def _layer_weights(l, w):
    lw = {}
    w_in = w["w_in"][l]
    q_cols = w_in[:, 0:256].reshape(D_MODEL, A_HEADS, A_HEAD_DIM)[:, HEAD_ORDER].reshape(D_MODEL, 256)
    kr_cols = jnp.pad(w_in[:, 1664:1696], ((0, 0), (MLA_NOPE, LANES - MLA_QK)))
    lw["wz"] = jnp.concatenate([q_cols, w_in[:, 256:1664], kr_cols], axis=1).astype(BF16)
    lw["wg"] = w_in[:, 1696:].astype(BF16)
    lw["g1"] = w["norm1_g"][l][None]
    lw["g2"] = w["norm2_g"][l][None]
    lw["gq"] = jnp.tile(w["a_qnorm_g"][l], A_HEADS)[None] * (A_HEAD_DIM ** -0.5 * LOG2_E)
    lw["gk"] = jnp.tile(w["a_knorm_g"][l], A_KV_HEADS)[None]
    lw["sink"] = w["a_sink"][l]
    lw["qlg"] = w["mla_q_lat_norm"][l][None]
    wuq = w["mla_w_uq"][l].reshape(MLA_Q_LORA, MLA_HEADS, MLA_QK)
    lw["wuq"] = jnp.pad(wuq, ((0, 0), (0, 0), (0, LANES - MLA_QK))).reshape(MLA_Q_LORA, MLA_HEADS * LANES).astype(BF16)
    pad_g = lambda g: jnp.tile(jnp.pad(g, (0, LANES - MLA_QK)), MLA_HEADS)[None]
    lw["gmq"] = pad_g(w["mla_qnorm_g"][l]) * (MLA_QK ** -0.5 * LOG2_E)
    lw["gmk"] = pad_g(w["mla_knorm_g"][l])
    lw["kvg"] = w["mla_kv_norm"][l][None]
    wukv = w["mla_w_ukv"][l].reshape(MLA_KV_LORA, MLA_HEADS, MLA_NOPE + MLA_V)
    wk = jnp.pad(wukv[:, :, :MLA_NOPE], ((0, 0), (0, 0), (0, LANES - MLA_NOPE))).reshape(MLA_KV_LORA, MLA_HEADS * LANES)
    wv = wukv[:, :, MLA_NOPE:].reshape(MLA_KV_LORA, MLA_HEADS * MLA_V)
    lw["wukv"] = jnp.concatenate([wk, wv], axis=1).astype(BF16)

    lre = w["s5_lambda_re"][l]
    lim = w["s5_lambda_im"][l]
    dt = jnp.exp(w["s5_log_step"][l])[:, :, None]
    mag = jnp.exp(lre * dt)
    ar, ai = mag * jnp.cos(lim * dt), mag * jnp.sin(lim * dt)
    den = lre * lre + lim * lim
    fr = ((ar - 1.0) * lre + ai * lim) / den
    fi = (ai * lre - (ar - 1.0) * lim) / den
    br, bi = w["s5_b_re"][l], w["s5_b_im"][l]
    bbr = fr[..., None] * br - fi[..., None] * bi
    bbi = fr[..., None] * bi + fi[..., None] * br
    in_map = lambda d: jnp.concatenate(
        [_block_diag(jnp.swapaxes(bbr[d], 1, 2)), _block_diag(jnp.swapaxes(bbi[d], 1, 2))], axis=1).astype(BF16)
    out_map = lambda d: jnp.concatenate(
        [_block_diag(jnp.swapaxes(w["s5_c_re"][l][d], 1, 2)),
         -_block_diag(jnp.swapaxes(w["s5_c_im"][l][d], 1, 2))], axis=0).astype(BF16)
    lw["s5_bf"], lw["s5_bb"] = in_map(0), in_map(1)
    lw["s5_cf"], lw["s5_cb"] = out_map(0), out_map(1)
    coef = jnp.stack([ar[0].reshape(-1), ai[0].reshape(-1), ar[1].reshape(-1), ai[1].reshape(-1)])
    lw["s5_a"] = jnp.broadcast_to(coef[:, None, :], (4, SUBLANES, S5_WIDTH))
    lw["s5_d"] = w["s5_d"][l][None]
    lw["wglu"] = w["s5_w_glu"][l].astype(BF16)

    lw["lru_cw"] = w["lru_conv_w"][l]
    lw["lru_cb"] = w["lru_conv_b"][l][None]
    gate_w = lambda d: jnp.concatenate(
        [_block_diag(w["lru_w_a"][l][d]), _block_diag(w["lru_w_x"][l][d])], axis=1).astype(BF16)
    gate_b = lambda d: jnp.concatenate([w["lru_b_a"][l][d], w["lru_b_x"][l][d]])[None]
    lw["lru_wf"], lw["lru_wb"] = gate_w(0), gate_w(1)
    lw["lru_bf"], lw["lru_bb"] = gate_b(0), gate_b(1)
    sp = jax.nn.softplus(-w["lru_lambda"][l])
    lw["lru_spf"], lw["lru_spb"] = sp[0][None], sp[1][None]

    wb = w["w_branch"][l]
    wb0 = wb[0].reshape(A_HEADS, A_HEAD_DIM, D_MODEL)[HEAD_ORDER].reshape(BRANCH_W, D_MODEL)
    lw["wb"] = jnp.concatenate([wb0[None], wb[1:]], axis=0).astype(BF16)
    lw["wo"] = w["w_out"][l].astype(BF16)

    wr = jnp.pad(jnp.concatenate([w["moe_w_group"][l], w["moe_w_expert"][l]], axis=1),
                 ((0, 0), (0, LANES - N_GROUPS - N_EXPERTS)))
    lw["wr_hi"] = wr.astype(BF16)
    lw["wr_lo"] = (wr - lw["wr_hi"].astype(F32)).astype(BF16)
    lw["br"] = jnp.pad(jnp.concatenate([w["moe_b_group"][l], w["moe_b_expert"][l]]),
                       (0, LANES - N_GROUPS - N_EXPERTS))[None]
    lw["tri"] = jnp.triu(jnp.ones((MOE_SORT_TILE, MOE_SORT_TILE), BF16))
    lw["wga"] = w["moe_w_gate"][l].astype(BF16)
    lw["wup"] = w["moe_w_up"][l].astype(BF16)
    lw["wdn"] = w["moe_w_down"][l].astype(BF16).reshape(N_EXPERTS * EXPERT_FF, D_MODEL)
    return lw


def _rows_to_state(fin, width):
    Bg, n = fin.shape[0], fin.shape[1]
    return jnp.swapaxes(fin, 1, 2).reshape(Bg * SUBLANES, n, width)


def _state_to_rows(state):
    B, n, width = state.shape
    return jnp.swapaxes(state.reshape(B // SUBLANES, SUBLANES, n, width), 1, 2)


def _mix(x, mods, lw, rope_tabs, cache, s5_h0, lru_h0, ctx):
    sh1, sc1, gt1 = mods
    outs = _in_proj(x, sc1, sh1, lw, rope_tabs, ctx)
    q, k, v, s5u_t, lx_t, lg, qm, km, vm = outs[:9]
    if ctx:
        o_a = _attn_a_ctx(q, k, v, lw["sink"][lw["layer"]])
        o_d = _mla_ctx(qm, km, vm)
    else:
        ck, cv, cckv, ckr = cache
        o_a = _attn_a_lat(q, k, v, ck, cv, lw["sink"][lw["layer"]])
        kc, vc = _mla_cache_kv(cckv, ckr, lw)
        o_d = _mla_lat(qm, km, vm, kc, vc)
    yf, yb, s5_fin = _s5_scan(s5u_t, lw, s5_h0)
    hf, hb, lru_fin = _lru_scan(lx_t, lw, lru_h0)
    x = _merge(x, sc1, sh1, gt1, lw, o_a, s5u_t, yf, yb, hf, hb, lg, o_d)
    return x, outs[9:], s5_fin, lru_fin


def kernel(x_prompt, x_sample, cache_attn_k, cache_attn_v, cache_mla_ckv, cache_mla_krope, state_ssm_re, state_ssm_im, state_lru, c, c_ctx, w_mod, b_mod, norm1_g, norm2_g, w_in, a_qnorm_g, a_knorm_g, a_sink, s5_lambda_re, s5_lambda_im, s5_log_step, s5_b_re, s5_b_im, s5_c_re, s5_c_im, s5_d, s5_w_glu, lru_conv_w, lru_conv_b, lru_w_a, lru_b_a, lru_w_x, lru_b_x, lru_lambda, mla_q_lat_norm, mla_w_uq, mla_kv_norm, mla_w_ukv, mla_qnorm_g, mla_knorm_g, w_branch, w_out, moe_w_group, moe_b_group, moe_w_expert, moe_b_expert, moe_w_gate, moe_w_up, moe_w_down):
    w = dict(norm1_g=norm1_g, norm2_g=norm2_g, w_in=w_in, a_qnorm_g=a_qnorm_g, a_knorm_g=a_knorm_g,
             a_sink=a_sink, s5_lambda_re=s5_lambda_re, s5_lambda_im=s5_lambda_im, s5_log_step=s5_log_step,
             s5_b_re=s5_b_re, s5_b_im=s5_b_im, s5_c_re=s5_c_re, s5_c_im=s5_c_im, s5_d=s5_d, s5_w_glu=s5_w_glu,
             lru_conv_w=lru_conv_w, lru_conv_b=lru_conv_b, lru_w_a=lru_w_a, lru_b_a=lru_b_a, lru_w_x=lru_w_x,
             lru_b_x=lru_b_x, lru_lambda=lru_lambda, mla_q_lat_norm=mla_q_lat_norm, mla_w_uq=mla_w_uq,
             mla_kv_norm=mla_kv_norm, mla_w_ukv=mla_w_ukv, mla_qnorm_g=mla_qnorm_g, mla_knorm_g=mla_knorm_g,
             w_branch=w_branch, w_out=w_out, moe_w_group=moe_w_group, moe_b_group=moe_b_group,
             moe_w_expert=moe_w_expert, moe_b_expert=moe_b_expert, moe_w_gate=moe_w_gate, moe_w_up=moe_w_up,
             moe_w_down=moe_w_down)
    B, L, _ = x_prompt.shape
    Bd, Ld, _ = x_sample.shape
    P = cache_attn_k.shape[2]

    cond = jnp.zeros((2 * SUBLANES, D_MODEL), F32).at[:Bd].set(c).at[Bd].set(c_ctx)
    mod = _modulation(cond, w_mod, b_mod[:, None, :])
    mod = mod.reshape(DEPTH, 2 * SUBLANES, 6, D_MODEL)

    rope_a = _rope_tables(Ld, A_HEAD_DIM, A_HEAD_DIM, 0)
    rope_a = tuple(jnp.asarray(np.tile(t, (1, A_HEADS))) for t in rope_a)
    rope_m = tuple(jnp.asarray(t) for t in _rope_tables(Ld, MLA_ROPE, LANES, MLA_NOPE))

    lw_all = jax.vmap(lambda w1: _layer_weights(0, {name: v[None] for name, v in w1.items()}))(w)

    xp, xs = x_prompt, x_sample
    ak_l, av_l, ckv_l, kr_l, sr_l, si_l, lru_l = [], [], [], [], [], [], []
    for l in range(DEPTH):
        lw = dict(lw_all, layer=l)
        lat_mod = [mod[l, :Bd, n][:, None, :] for n in range(6)]
        ctx_mod = [mod[l, Bd:Bd + 1, n][:, None, :] for n in range(6)]

        zs5 = jnp.zeros((B // SUBLANES, 4, SUBLANES, S5_WIDTH), F32)
        zlru = jnp.zeros((B // SUBLANES, 2, SUBLANES, LRU_WIDTH), F32)
        xp, (k32, v32, ckv_n, krp), s5_fin, lru_fin = _mix(xp, ctx_mod[0:3], lw, None, None, zs5, zlru, True)
        xp = _moe(xp, ctx_mod[4], ctx_mod[3], ctx_mod[5], lw)
        ak_l.append(k32.reshape(B, L, A_KV_HEADS, A_HEAD_DIM))
        av_l.append(v32.reshape(B, L, A_KV_HEADS, A_HEAD_DIM))
        ckv_l.append(ckv_n)
        kr_l.append(krp[:, :, MLA_NOPE:MLA_QK])
        fin = _rows_to_state(s5_fin, S5_WIDTH)
        sr_l.append(fin[:, 0::2].reshape(B, 2, S5_GROUPS, S5_STATE))
        si_l.append(fin[:, 1::2].reshape(B, 2, S5_GROUPS, S5_STATE))
        lru_l.append(_rows_to_state(lru_fin, LRU_WIDTH))

        sre = state_ssm_re[:, l].reshape(Bd, 2, S5_WIDTH)
        sim = state_ssm_im[:, l].reshape(Bd, 2, S5_WIDTH)
        s5_h0 = _state_to_rows(jnp.stack([sre[:, 0], sim[:, 0], sre[:, 1], sim[:, 1]], axis=1))
        lru_h0 = _state_to_rows(state_lru[:, l])
        cache = (cache_attn_k[:, l].reshape(Bd, P, 128).astype(BF16),
                 cache_attn_v[:, l].reshape(Bd, P, 128).astype(BF16),
                 cache_mla_ckv[:, l],
                 jnp.pad(cache_mla_krope[:, l], ((0, 0), (0, 0), (MLA_NOPE, LANES - MLA_QK))))
        xs, _, _, _ = _mix(xs, lat_mod[0:3], lw, rope_a + rope_m, cache, s5_h0, lru_h0, False)
        xs = _moe(xs, lat_mod[4], lat_mod[3], lat_mod[5], lw)

    stack = lambda ts: jnp.stack(ts, axis=1)
    return (xp, xs, stack(ak_l), stack(av_l), stack(ckv_l), stack(kr_l), stack(sr_l), stack(si_l), stack(lru_l))
```

```python
import functools
import math

import jax
import jax.numpy as jnp
import numpy as np
from jax import lax
from jax.experimental import pallas as pl
from jax.experimental.pallas import tpu as pltpu

F32 = jnp.float32
BF16 = jnp.bfloat16

D_MODEL = 1024
DEPTH = 2
GRID_W = 64
N_BRANCH = 4
BRANCH_W = 256
ROPE_BASE = 10000.0
EPS = 1e-6
NEG_INF = -1e30
LOG2_E = math.log2(math.e)
A_HEADS = 4
A_KV_HEADS = 2
A_HEAD_DIM = 64
WINDOW = 128
Q_BLOCK = 128
S5_GROUP = 16
S5_GROUPS = 16
S5_STATE = 64
S5_WIDTH = S5_GROUPS * S5_STATE
LRU_WIDTH = 256
LRU_BLOCKS = 4
LRU_CONV = 4
LRU_C = 8.0
MLA_HEADS = 4
MLA_Q_LORA = 256
MLA_KV_LORA = 128
MLA_NOPE = 64
MLA_ROPE = 32
MLA_V = 64
MLA_QK = MLA_NOPE + MLA_ROPE
N_GROUPS = 4
EXPERTS_PER_GROUP = 4
N_EXPERTS = 16
EXPERT_FF = 256

LANES = 128
SUBLANES = 8
VMEM_LIMIT = 56 * 1024 * 1024

TOK_TILE = 256
TOK_STEPS = TOK_TILE // SUBLANES
SCAN_STEPS = 128
MOE_SORT_TILE = 256
MOE_ROW_ALIGN = 16
MOE_SORTED_ROWS = MOE_SORT_TILE + LANES
MOE_SORT_SUBTILES = 8
MOE_BLOCK_TILES = 8
MOE_CHUNK = 512
MLA_Q_TILE = 256
ATTN_Q_TILE = 256
Z_COLS = 1792
HEAD_ORDER = np.array((0, 2, 1, 3))


def _params(sem):
    return pltpu.CompilerParams(dimension_semantics=sem, vmem_limit_bytes=VMEM_LIMIT)


def _layer_spec(layer):
    def spec(shape):
        n = len(shape)
        return pl.BlockSpec((None,) + tuple(shape), lambda *_: (layer,) + (0,) * n)
    return spec


def _modnorm(x, g, sc, sh):
    ms = jnp.mean(x * x, axis=-1, keepdims=True)
    return (x * lax.rsqrt(ms + EPS)) * g * (1.0 + sc) + sh


def _rmsnorm(x, g):
    ms = jnp.mean(x * x, axis=-1, keepdims=True)
    return (x * lax.rsqrt(ms + EPS)) * g


def _segment_ones(width, segment):
    lane = np.arange(width) // segment
    return jnp.asarray(lane[:, None] == lane[None, :], BF16)


def _segment_rsqrt(x, ones, n_real):
    sq = x * x
    hi = sq.astype(BF16)
    lo = (sq - hi.astype(F32)).astype(BF16)
    ss = jnp.dot(hi, ones, preferred_element_type=F32) + jnp.dot(lo, ones, preferred_element_type=F32)
    return lax.rsqrt(ss * (1.0 / n_real) + EPS)


def _rope(x, cos, sin_next, sin_prev, quarter):
    width = x.shape[1]
    return (x * cos + pltpu.roll(x, width - quarter, 1) * sin_next
            + pltpu.roll(x, quarter, 1) * sin_prev)


def _mla_kv(ckv_n, kr_placed, wukv, gk, ones_tile, rope_tabs):
    kv = jnp.dot(ckv_n.astype(BF16), wukv, preferred_element_type=F32)
    gain = gk[:, :LANES]
    kfull = kv[:, :MLA_HEADS * LANES] + jnp.concatenate([kr_placed] * MLA_HEADS, axis=1)
    inv_all = _segment_rsqrt(kfull, ones_tile, MLA_QK)
    rotated = None
    if rope_tabs is not None:
        cos, sin_next, sin_prev = rope_tabs
        quarter = MLA_ROPE // 4
        base = kr_placed * gain
        rotated = pltpu.roll(base, LANES - quarter, 1) * sin_next + pltpu.roll(base, quarter, 1) * sin_prev
    heads = []
    for h in range(MLA_HEADS):
        kf = kfull[:, LANES * h:LANES * (h + 1)]
        inv = inv_all[:, LANES * h:LANES * (h + 1)]
        kh = kf * gain
        if rotated is not None:
            kh = kh * cos + rotated
        heads.append(kh * inv)
    return jnp.concatenate(heads, axis=1), kv[:, MLA_HEADS * LANES:]


def _mod_kernel(c_ref, w_ref, b_ref, o_ref):
    c = c_ref[...]
    s = c * jax.nn.sigmoid(c)
    o_ref[...] = jnp.dot(s.astype(BF16), w_ref[...].astype(BF16), preferred_element_type=F32) + b_ref[...]


def _modulation(cond, w_mod, b_mod):
    n_rows = cond.shape[0]
    n_out = w_mod.shape[-1]
    tn = 1024
    return pl.pallas_call(
        _mod_kernel,
        grid=(DEPTH, n_out // tn),
        in_specs=[pl.BlockSpec((n_rows, D_MODEL), lambda l, j: (0, 0)),
                  pl.BlockSpec((None, D_MODEL, tn), lambda l, j: (l, 0, j)),
                  pl.BlockSpec((None, 1, tn), lambda l, j: (l, 0, j))],
        out_specs=pl.BlockSpec((None, n_rows, tn), lambda l, j: (l, 0, j)),
        out_shape=jax.ShapeDtypeStruct((DEPTH, n_rows, n_out), F32),
        compiler_params=_params(("arbitrary", "arbitrary")),
        name="modulation",
    )(cond, w_mod, b_mod)


def _group_rows(ref):
    v = ref[...]
    return v.reshape(v.shape[0] * v.shape[1], v.shape[2])


def _mod_rows(ref, steps):
    v = ref[...]
    if v.shape[0] == 1:
        return v[0]
    return jnp.broadcast_to(v, (v.shape[0], steps, v.shape[2])).reshape(v.shape[0] * steps, v.shape[2])


def _store_group(ref, val):
    ref[...] = val.reshape(ref.shape).astype(ref.dtype)


def _store_time_major(ref, val):
    steps = ref.shape[1] // SUBLANES
    for b in range(SUBLANES):
        for half in range(2):
            ref[half, pl.ds(b, steps, stride=SUBLANES), :] = (
                val[b * steps:(b + 1) * steps, LANES * half:LANES * (half + 1)])


def _load_time_major(ref):
    steps = ref.shape[1] // SUBLANES
    return jnp.concatenate(
        [jnp.concatenate([ref[half, pl.ds(b, steps, stride=SUBLANES), :] for half in range(2)], axis=1)
         for b in range(SUBLANES)], axis=0)


def _in_proj_kernel(rope, ctx, *refs):
    it = iter(refs)
    x_ref, sc_ref, sh_ref, g1_ref, wz_ref = (next(it) for _ in range(5))
    gq_ref, gk_ref, qlg_ref, wuq_ref, gmq_ref, kvg_ref, wukv_ref, gmk_ref = (next(it) for _ in range(8))
    oh_ref, ot_ref = next(it), next(it)
    if rope:
        ca_ref, sna_ref, spa_ref, cm_ref, snm_ref, spm_ref = (next(it) for _ in range(6))
    q_ref, k_ref, v_ref, s5u_ref, lx_ref, lg_ref, qm_ref, km_ref, vm_ref = (next(it) for _ in range(9))
    if ctx:
        k32_ref, v32_ref, ckv_ref, krp_ref = (next(it) for _ in range(4))

    steps = x_ref.shape[1]
    h = _modnorm(_group_rows(x_ref), g1_ref[...], _mod_rows(sc_ref, steps), _mod_rows(sh_ref, steps))
    z = jnp.dot(h.astype(BF16), wz_ref[...], preferred_element_type=F32)
    per_seq = lambda ref: jnp.concatenate([ref[...]] * SUBLANES, axis=0)

    qk = z[:, 0:384]
    qk = qk * _segment_rsqrt(qk, oh_ref[...], A_HEAD_DIM)
    q = qk[:, 0:256] * gq_ref[...]
    k = qk[:, 256:384] * gk_ref[...]
    v = z[:, 384:512]
    if ctx:
        _store_group(k32_ref, k)
        _store_group(v32_ref, v)
    if rope:
        quarter = A_HEAD_DIM // 4
        atabs = (per_seq(ca_ref), per_seq(sna_ref), per_seq(spa_ref))
        q = _rope(q, *atabs, quarter)
        k = _rope(k, *(t[:, :LANES] for t in atabs), quarter)
    _store_group(q_ref, q)
    _store_group(k_ref, k)
    _store_group(v_ref, v)

    _store_time_major(s5u_ref, z[:, 512:768])
    _store_time_major(lx_ref, z[:, 768:1024])
    _store_group(lg_ref, z[:, 1024:1280])

    mtabs = mtabs4 = None
    if rope:
        mtabs = tuple(per_seq(r) for r in (cm_ref, snm_ref, spm_ref))
        mtabs4 = tuple(jnp.concatenate([t] * MLA_HEADS, axis=1) for t in mtabs)
    ql = _rmsnorm(z[:, 1280:1536], qlg_ref[...])
    qm = jnp.dot(ql.astype(BF16), wuq_ref[...], preferred_element_type=F32)
    qm = qm * _segment_rsqrt(qm, ot_ref[...], MLA_QK) * gmq_ref[...]
    if rope:
        qm = _rope(qm, *mtabs4, MLA_ROPE // 4)
    _store_group(qm_ref, qm)

    ckv_n = _rmsnorm(z[:, 1536:1664], kvg_ref[...])
    krp = z[:, 1664:1792]
    km, vm = _mla_kv(ckv_n, krp, wukv_ref[...], gmk_ref[...], ot_ref[...], mtabs)
    _store_group(km_ref, km)
    _store_group(vm_ref, vm)
    if ctx:
        _store_group(ckv_ref, ckv_n)
        _store_group(krp_ref, krp)


def _group_specs(per_seq_mod):
    TS = TOK_STEPS
    tok = lambda c: pl.BlockSpec((SUBLANES, TS, c), lambda g, i: (g, i, 0))
    scan = pl.BlockSpec((None, 2, TS * SUBLANES, LANES), lambda g, i: (g, 0, i, 0))
    if per_seq_mod:
        mod = pl.BlockSpec((SUBLANES, 1, D_MODEL), lambda g, i: (g, 0, 0))
    else:
        mod = pl.BlockSpec((1, 1, D_MODEL), lambda g, i: (0, 0, 0))
    return tok, scan, mod


def _in_proj(x, sc, sh, lw, rope_tabs, ctx):
    wspec = _layer_spec(lw["layer"])
    B, L, _ = x.shape
    Bg = B // SUBLANES
    TS = TOK_STEPS
    rope = rope_tabs is not None
    tok, scan, mod = _group_specs(sc.shape[0] > 1)
    in_specs = [tok(D_MODEL), mod, mod, wspec((1, D_MODEL)), wspec((D_MODEL, Z_COLS)),
                wspec((1, 256)), wspec((1, 128)), wspec((1, 256)), wspec((256, 512)), wspec((1, 512)),
                wspec((1, 128)), wspec((128, 768)), wspec((1, 512)),
                pl.BlockSpec((384, 384), lambda g, i: (0, 0)), pl.BlockSpec((512, 512), lambda g, i: (0, 0))]
    args = [x, sc, sh, lw["g1"], lw["wz"], lw["gq"], lw["gk"], lw["qlg"], lw["wuq"], lw["gmq"],
            lw["kvg"], lw["wukv"], lw["gmk"], _segment_ones(384, A_HEAD_DIM), _segment_ones(MLA_HEADS * LANES, LANES)]
    if rope:
        in_specs += [pl.BlockSpec((TS, 256), lambda g, i: (i, 0))] * 3
        in_specs += [pl.BlockSpec((TS, 128), lambda g, i: (i, 0))] * 3
        args += list(rope_tabs)
    out_specs = [tok(256), tok(128), tok(128), scan, scan, tok(256), tok(512), tok(512), tok(256)]
    sds = jax.ShapeDtypeStruct
    scan_shape = sds((Bg, 2, L * SUBLANES, LANES), F32)
    out_shape = [sds((B, L, 256), BF16), sds((B, L, 128), BF16), sds((B, L, 128), BF16),
                 scan_shape, scan_shape,
                 sds((B, L, 256), F32), sds((B, L, 512), BF16), sds((B, L, 512), BF16),
                 sds((B, L, 256), BF16)]
    if ctx:
        out_specs += [tok(128), tok(128), tok(128), tok(128)]
        out_shape += [sds((B, L, 128), F32)] * 4
    return pl.pallas_call(
        functools.partial(_in_proj_kernel, rope, ctx),
        grid=(Bg, L // TS),
        in_specs=in_specs, out_specs=out_specs, out_shape=out_shape,
        compiler_params=_params(("parallel", "parallel")),
        name="in_proj_ctx" if ctx else "in_proj_lat",
    )(*args)


def _mla_cache_kernel(ckv_ref, krp_ref, wukv_ref, gmk_ref, ot_ref, km_ref, vm_ref):
    km, vm = _mla_kv(ckv_ref[...], krp_ref[...], wukv_ref[...], gmk_ref[...], ot_ref[...], None)
    km_ref[...] = km.astype(BF16)
    vm_ref[...] = vm.astype(BF16)


def _mla_cache_kv(cckv, ckr_placed, lw):
    wspec = _layer_spec(lw["layer"])
    B, P, _ = cckv.shape
    tok = lambda c: pl.BlockSpec((None, P, c), lambda b: (b, 0, 0))
    return pl.pallas_call(
        _mla_cache_kernel,
        grid=(B,),
        in_specs=[tok(128), tok(128), wspec((128, 768)), wspec((1, 512)),
                  pl.BlockSpec((512, 512), lambda b: (0, 0))],
        out_specs=[tok(512), tok(256)],
        out_shape=[jax.ShapeDtypeStruct((B, P, 512), BF16), jax.ShapeDtypeStruct((B, P, 256), BF16)],
        compiler_params=_params(("parallel",)),
        name="mla_cache_kv",
    )(cckv, ckr_placed, lw["wukv"], lw["gmk"], _segment_ones(MLA_HEADS * LANES, LANES))


def _sink_softmax(s, sink):
    sink = sink * LOG2_E
    m = jnp.maximum(jnp.max(s, axis=-1, keepdims=True), sink)
    e = jnp.exp2(s - m)
    den = jnp.sum(e, axis=-1, keepdims=True) + jnp.exp2(sink - m)
    return e.astype(BF16), 1.0 / den


def _gqa_tile(qt, keys, vst, sink_lo, sink_hi, mask):
    lane = lax.broadcasted_iota(jnp.int32, (1, LANES), 1)
    lo = lane < A_HEAD_DIM
    zero = jnp.zeros_like(qt)
    ps, invs = [], []
    for qh, sink in ((jnp.where(lo, qt, zero), sink_lo), (jnp.where(lo, zero, qt), sink_hi)):
        s = lax.dot_general(qh, keys, (((1,), (1,)), ((), ())), preferred_element_type=F32)
        if mask is not None:
            s = jnp.where(mask, s, NEG_INF)
        p, inv = _sink_softmax(s, sink)
        ps.append(p)
        invs.append(inv)
    o = jnp.dot(jnp.concatenate(ps, axis=1), vst, preferred_element_type=F32)
    return o * jnp.where(lo, invs[0], invs[1])


def _stack_kv_halves(v):
    lane = lax.broadcasted_iota(jnp.int32, (1, LANES), 1)
    lo = lane < A_HEAD_DIM
    zero = jnp.zeros_like(v)
    return jnp.concatenate([jnp.where(lo, v, zero), jnp.where(lo, zero, v)], axis=0)


def _attn_a_ctx_kernel(sink_ref, q_ref, k_ref, v_ref, o_ref):
    keys = k_ref[...]
    vst = _stack_kv_halves(v_ref[...])
    for t in range(2):
        o = _gqa_tile(q_ref[:, LANES * t:LANES * (t + 1)], keys, vst, sink_ref[t], sink_ref[2 + t], None)
        o_ref[:, LANES * t:LANES * (t + 1)] = o.astype(o_ref.dtype)


def _attn_a_ctx(q, k, v, sink):
    B, L, _ = q.shape
    tok = lambda c: pl.BlockSpec((None, L, c), lambda b: (b, 0, 0))
    return pl.pallas_call(
        _attn_a_ctx_kernel,
        grid=(B,),
        in_specs=[pl.BlockSpec(memory_space=pltpu.SMEM), tok(256), tok(128), tok(128)],
        out_specs=tok(256),
        out_shape=jax.ShapeDtypeStruct((B, L, 256), BF16),
        compiler_params=_params(("parallel",)),
        name="attn_a_ctx",
    )(sink, q, k, v)


def _attn_a_lat_kernel(sink_ref, q_ref, kc_ref, vc_ref, k_ref, v_ref, o_ref):
    i = pl.program_id(1)
    L = k_ref.shape[0]
    n_ctx = kc_ref.shape[0]
    rows = q_ref.shape[0]
    span = rows + 2 * WINDOW
    start = pl.multiple_of(jnp.clip(i * rows - WINDOW, 0, L - span), WINDOW)
    keys = jnp.concatenate([kc_ref[...], k_ref[pl.ds(start, span), :]], axis=0)
    vals = jnp.concatenate([vc_ref[...], v_ref[pl.ds(start, span), :]], axis=0)
    vst = _stack_kv_halves(vals)
    col = lax.broadcasted_iota(jnp.int32, (rows, n_ctx + span), 1)
    row = lax.broadcasted_iota(jnp.int32, (rows, n_ctx + span), 0)
    rel = (col - n_ctx + start) - (row + i * rows)
    mask = (col < n_ctx) | (jnp.abs(rel) <= WINDOW)
    for t in range(2):
        o = _gqa_tile(q_ref[:, LANES * t:LANES * (t + 1)], keys, vst, sink_ref[t], sink_ref[2 + t], mask)
        o_ref[:, LANES * t:LANES * (t + 1)] = o.astype(o_ref.dtype)


def _attn_a_lat(q, k, v, kc, vc, sink):
    B, L, _ = q.shape
    P = kc.shape[1]
    blk = lambda c: pl.BlockSpec((None, ATTN_Q_TILE, c), lambda b, i: (b, i, 0))
    whole = lambda n, c: pl.BlockSpec((None, n, c), lambda b, i: (b, 0, 0))
    return pl.pallas_call(
        _attn_a_lat_kernel,
        grid=(B, L // ATTN_Q_TILE),
        in_specs=[pl.BlockSpec(memory_space=pltpu.SMEM), blk(256), whole(P, 128), whole(P, 128),
                  whole(L, 128), whole(L, 128)],
        out_specs=blk(256),
        out_shape=jax.ShapeDtypeStruct((B, L, 256), BF16),
        compiler_params=_params(("parallel", "parallel")),
        name="attn_a_lat",
    )(sink, q, kc, vc, k, v)


def _mla_stack_values(vals):
    lane = lax.broadcasted_iota(jnp.int32, (1, MLA_HEADS * MLA_V), 1)
    zero = jnp.zeros_like(vals)
    return jnp.concatenate([jnp.where((lane >= MLA_V * h) & (lane < MLA_V * (h + 1)), vals, zero)
                            for h in range(MLA_HEADS)], axis=0)


def _mla_attend(q, key_parts, vst):
    lane = lax.broadcasted_iota(jnp.int32, (1, MLA_HEADS * MLA_V), 1)
    ps = []
    inv = jnp.zeros((q.shape[0], MLA_HEADS * MLA_V), F32)
    for h in range(MLA_HEADS):
        qh = q[:, LANES * h:LANES * (h + 1)]
        ss = [lax.dot_general(qh, kp[:, LANES * h:LANES * (h + 1)], (((1,), (1,)), ((), ())),
                              preferred_element_type=F32) for kp in key_parts]
        m = functools.reduce(jnp.maximum, [jnp.max(s, axis=-1, keepdims=True) for s in ss])
        es = [jnp.exp2(s - m) for s in ss]
        den = functools.reduce(lambda a, b: a + b, [jnp.sum(e, axis=-1, keepdims=True) for e in es])
        ps += [e.astype(BF16) for e in es]
        inv = jnp.where((lane >= MLA_V * h) & (lane < MLA_V * (h + 1)), 1.0 / den, inv)
    return jnp.dot(jnp.concatenate(ps, axis=1), vst, preferred_element_type=F32) * inv


def _mla_ctx_kernel(q_ref, k_ref, v_ref, o_ref):
    o_ref[...] = _mla_attend(q_ref[...], [k_ref[...]], _mla_stack_values(v_ref[...])).astype(o_ref.dtype)


def _mla_ctx(q, k, v):
    B, L, _ = q.shape
    tok = lambda c: pl.BlockSpec((None, L, c), lambda b: (b, 0, 0))
    return pl.pallas_call(
        _mla_ctx_kernel,
        grid=(B,),
        in_specs=[tok(512), tok(512), tok(256)],
        out_specs=tok(256),
        out_shape=jax.ShapeDtypeStruct((B, L, 256), BF16),
        compiler_params=_params(("parallel",)),
        name="mla_ctx",
    )(q, k, v)


def _mla_lat_kernel(q_ref, kc_ref, vc_ref, k_ref, v_ref, o_ref):
    vst = _mla_stack_values(jnp.concatenate([vc_ref[...], v_ref[...]], axis=0))
    o = _mla_attend(q_ref[...], [kc_ref[...], k_ref[...]], vst)
    o_ref[...] = o.astype(o_ref.dtype)


def _mla_lat(q, k, v, kc, vc):
    B, L, _ = q.shape
    P = kc.shape[1]
    TQ = MLA_Q_TILE
    blk = lambda c: pl.BlockSpec((None, TQ, c), lambda b, i: (b, i, 0))
    whole = lambda n, c: pl.BlockSpec((None, n, c), lambda b, i: (b, 0, 0))
    return pl.pallas_call(
        _mla_lat_kernel,
        grid=(B, L // TQ),
        in_specs=[blk(512), whole(P, 512), whole(P, 256), whole(L, 512), whole(L, 256)],
        out_specs=blk(256),
        out_shape=jax.ShapeDtypeStruct((B, L, 256), BF16),
        compiler_params=_params(("parallel", "parallel")),
        name="mla_lat",
    )(q, kc, vc, k, v)


def _join_halves(ref):
    return jnp.concatenate([ref[0], ref[1]], axis=1)


def _split_halves(ref, val):
    ref[0] = val[:, :LANES]
    ref[1] = val[:, LANES:]


def _s5_kernel(uf_ref, ub_ref, bf_ref, bb_ref, a_ref, h0_ref, cf_ref, cb_ref,
               yf_ref, yb_ref, fin_ref, s_ref, st_ref):
    i = pl.program_id(1)
    steps = SCAN_STEPS

    @pl.when(i == 0)
    def _():
        st_ref[...] = h0_ref[...]

    uf = _join_halves(uf_ref).astype(BF16)
    ub = _join_halves(ub_ref).astype(BF16)
    s_ref[0] = jnp.dot(uf, bf_ref[:, :S5_WIDTH], preferred_element_type=F32)
    s_ref[1] = jnp.dot(uf, bf_ref[:, S5_WIDTH:], preferred_element_type=F32)
    s_ref[2] = jnp.dot(ub, bb_ref[:, :S5_WIDTH], preferred_element_type=F32)
    s_ref[3] = jnp.dot(ub, bb_ref[:, S5_WIDTH:], preferred_element_type=F32)

    lane_chunk = 2 * LANES
    for c in range(S5_WIDTH // lane_chunk):
        sl = slice(lane_chunk * c, lane_chunk * (c + 1))
        arf, aif, arb, aib = (a_ref[n, :, sl] for n in range(4))

        def body(j, carry):
            hrf, hif, hrb, hib = carry
            rf = pl.multiple_of(j * SUBLANES, SUBLANES)
            rb = pl.multiple_of((steps - 1 - j) * SUBLANES, SUBLANES)
            nrf = arf * hrf - aif * hif + s_ref[0, pl.ds(rf, SUBLANES), sl]
            nif = arf * hif + aif * hrf + s_ref[1, pl.ds(rf, SUBLANES), sl]
            nrb = arb * hrb - aib * hib + s_ref[2, pl.ds(rb, SUBLANES), sl]
            nib = arb * hib + aib * hrb + s_ref[3, pl.ds(rb, SUBLANES), sl]
            s_ref[0, pl.ds(rf, SUBLANES), sl] = nrf
            s_ref[1, pl.ds(rf, SUBLANES), sl] = nif
            s_ref[2, pl.ds(rb, SUBLANES), sl] = nrb
            s_ref[3, pl.ds(rb, SUBLANES), sl] = nib
            return nrf, nif, nrb, nib

        fin = lax.fori_loop(0, steps, body, tuple(st_ref[n, :, sl] for n in range(4)), unroll=4)
        for n in range(4):
            st_ref[n, :, sl] = fin[n]

    hf = jnp.concatenate([s_ref[0].astype(BF16), s_ref[1].astype(BF16)], axis=1)
    _split_halves(yf_ref, jnp.dot(hf, cf_ref[...], preferred_element_type=F32))
    hb = jnp.concatenate([s_ref[2].astype(BF16), s_ref[3].astype(BF16)], axis=1)
    _split_halves(yb_ref, jnp.dot(hb, cb_ref[...], preferred_element_type=F32))
    fin_ref[...] = st_ref[...]


def _s5_scan(u_rows, lw, h0):
    wspec = _layer_spec(lw["layer"])
    Bg, _, rows, _ = u_rows.shape
    R = SCAN_STEPS * SUBLANES
    n = rows // R
    fwd = pl.BlockSpec((None, 2, R, LANES), lambda g, i: (g, 0, i, 0))
    bwd = pl.BlockSpec((None, 2, R, LANES), lambda g, i: (g, 0, n - 1 - i, 0))
    st = pl.BlockSpec((None, 4, SUBLANES, S5_WIDTH), lambda g, i: (g, 0, 0, 0))
    return pl.pallas_call(
        _s5_kernel,
        grid=(Bg, n),
        in_specs=[fwd, bwd, wspec((256, 2 * S5_WIDTH)), wspec((256, 2 * S5_WIDTH)),
                  wspec((4, SUBLANES, S5_WIDTH)), st, wspec((2 * S5_WIDTH, 256)), wspec((2 * S5_WIDTH, 256))],
        out_specs=[fwd, bwd, st],
        out_shape=[jax.ShapeDtypeStruct(u_rows.shape, F32), jax.ShapeDtypeStruct(u_rows.shape, F32),
                   jax.ShapeDtypeStruct((Bg, 4, SUBLANES, S5_WIDTH), F32)],
        scratch_shapes=[pltpu.VMEM((4, R, S5_WIDTH), F32), pltpu.VMEM((4, SUBLANES, S5_WIDTH), F32)],
        compiler_params=_params(("parallel", "arbitrary")),
        name="s5_scan",
    )(u_rows, u_rows, lw["s5_bf"], lw["s5_bb"], lw["s5_a"], h0, lw["s5_cf"], lw["s5_cb"])


def _lru_gates(x_ref, pre_ref, post_ref, has_pre, has_post, cw_ref, cb_ref, w_ref, b_ref, sp_ref, a_ref, h_ref):
    R = x_ref.shape[1]
    pre = jnp.where(has_pre, _join_halves(pre_ref), 0.0)
    post = jnp.where(has_post, _join_halves(post_ref), 0.0)
    xp = jnp.concatenate([pre, _join_halves(x_ref), post], axis=0)
    xc = cb_ref[...]
    for t in range(LRU_CONV):
        xc = xc + xp[SUBLANES * t:SUBLANES * t + R] * cw_ref[t:t + 1, :]
    g = jnp.dot(xc.astype(BF16), w_ref[...], preferred_element_type=F32) + b_ref[...]
    r = jax.nn.sigmoid(g[:, :LRU_WIDTH])
    ig = jax.nn.sigmoid(g[:, LRU_WIDTH:])
    log_a = (-LRU_C) * r * sp_ref[...]
    a = jnp.exp(log_a)
    a_ref[...] = a
    h_ref[...] = jnp.sqrt(1.0 - a * a) * (ig * xc)


def _lru_kernel(xf_ref, xfp_ref, xfn_ref, xb_ref, xbp_ref, xbn_ref, cw_ref, cb_ref, wf_ref, wb_ref,
                bf_ref, bb_ref, spf_ref, spb_ref, h0_ref, of_ref, ob_ref, fin_ref,
                af_ref, ab_ref, hf_ref, hb_ref, st_ref):
    i = pl.program_id(1)
    n = pl.num_programs(1)
    steps = SCAN_STEPS

    @pl.when(i == 0)
    def _():
        st_ref[...] = h0_ref[...]

    _lru_gates(xf_ref, xfp_ref, xfn_ref, i > 0, i < n - 1, cw_ref, cb_ref, wf_ref, bf_ref, spf_ref, af_ref, hf_ref)
    _lru_gates(xb_ref, xbp_ref, xbn_ref, i < n - 1, i > 0, cw_ref, cb_ref, wb_ref, bb_ref, spb_ref, ab_ref, hb_ref)

    def body(j, carry):
        hf, hb = carry
        rf = pl.multiple_of(j * SUBLANES, SUBLANES)
        rb = pl.multiple_of((steps - 1 - j) * SUBLANES, SUBLANES)
        nf = af_ref[pl.ds(rf, SUBLANES), :] * hf + hf_ref[pl.ds(rf, SUBLANES), :]
        nb = ab_ref[pl.ds(rb, SUBLANES), :] * hb + hb_ref[pl.ds(rb, SUBLANES), :]
        hf_ref[pl.ds(rf, SUBLANES), :] = nf
        hb_ref[pl.ds(rb, SUBLANES), :] = nb
        return nf, nb

    ff, fb = lax.fori_loop(0, steps, body, (st_ref[0], st_ref[1]), unroll=8)
    st_ref[0] = ff
    st_ref[1] = fb
    fin_ref[...] = st_ref[...]
    _split_halves(of_ref, hf_ref[...])
    _split_halves(ob_ref, hb_ref[...])


def _lru_scan(x_rows, lw, h0):
    wspec = _layer_spec(lw["layer"])
    Bg, _, rows, _ = x_rows.shape
    R = SCAN_STEPS * SUBLANES
    n = rows // R
    pre_rows = 2 * SUBLANES
    fwd = lambda g, i: i
    bwd = lambda g, i: n - 1 - i
    blk = lambda m: pl.BlockSpec((None, 2, R, LANES), lambda g, i: (g, 0, m(g, i), 0))
    pre = lambda m: pl.BlockSpec(
        (None, 2, pre_rows, LANES), lambda g, i: (g, 0, jnp.maximum(m(g, i) * (R // pre_rows) - 1, 0), 0))
    post = lambda m: pl.BlockSpec(
        (None, 2, SUBLANES, LANES),
        lambda g, i: (g, 0, jnp.minimum((m(g, i) + 1) * (R // SUBLANES), rows // SUBLANES - 1), 0))
    st = pl.BlockSpec((None, 2, SUBLANES, 256), lambda g, i: (g, 0, 0, 0))
    return pl.pallas_call(
        _lru_kernel,
        grid=(Bg, n),
        in_specs=[blk(fwd), pre(fwd), post(fwd), blk(bwd), pre(bwd), post(bwd),
                  wspec((LRU_CONV, 256)), wspec((1, 256)), wspec((256, 512)), wspec((256, 512)),
                  wspec((1, 512)), wspec((1, 512)), wspec((1, 256)), wspec((1, 256)), st],
        out_specs=[blk(fwd), blk(bwd), st],
        out_shape=[jax.ShapeDtypeStruct(x_rows.shape, F32), jax.ShapeDtypeStruct(x_rows.shape, F32),
                   jax.ShapeDtypeStruct((Bg, 2, SUBLANES, 256), F32)],
        scratch_shapes=[pltpu.VMEM((R, 256), F32), pltpu.VMEM((R, 256), F32), pltpu.VMEM((R, 256), F32),
                        pltpu.VMEM((R, 256), F32), pltpu.VMEM((2, SUBLANES, 256), F32)],
        compiler_params=_params(("parallel", "arbitrary")),
        name="lru_scan",
    )(x_rows, x_rows, x_rows, x_rows, x_rows, x_rows, lw["lru_cw"], lw["lru_cb"], lw["lru_wf"], lw["lru_wb"],
      lw["lru_bf"], lw["lru_bb"], lw["lru_spf"], lw["lru_spb"], h0)


def _merge_kernel(x_ref, sc_ref, sh_ref, gt_ref, g1_ref, wg_ref, oa_ref, u_ref, yf_ref, yb_ref, d_ref,
                  wglu_ref, hf_ref, hb_ref, lg_ref, od_ref, wb_ref, wo_ref, o_ref):
    steps = x_ref.shape[1]
    x = _group_rows(x_ref)
    h = _modnorm(x, g1_ref[...], _mod_rows(sc_ref, steps), _mod_rows(sh_ref, steps)).astype(BF16)

    yb5 = jax.nn.gelu(d_ref[...] * _load_time_major(u_ref) + _load_time_major(yf_ref) + _load_time_major(yb_ref))
    gv = jnp.dot(yb5.astype(BF16), wglu_ref[...], preferred_element_type=F32)
    o_b = gv[:, :BRANCH_W] * jax.nn.sigmoid(gv[:, BRANCH_W:])
    o_c = (_load_time_major(hf_ref) + _load_time_major(hb_ref)) * jax.nn.gelu(_group_rows(lg_ref))
    branches = (_group_rows(oa_ref), o_b.astype(BF16), o_c.astype(BF16), _group_rows(od_ref))

    acc = jnp.zeros(x.shape, F32)
    for n in range(N_BRANCH):
        gate = jnp.dot(h, wg_ref[:, D_MODEL * n:D_MODEL * (n + 1)], preferred_element_type=F32)
        proj = jnp.dot(branches[n], wb_ref[n], preferred_element_type=F32)
        acc = acc + jax.nn.sigmoid(gate) * proj
    out = jnp.dot(acc.astype(BF16), wo_ref[...], preferred_element_type=F32)
    _store_group(o_ref, x + _mod_rows(gt_ref, steps) * out)


def _merge(x, sc, sh, gt, lw, o_a, s5u_t, yf_t, yb_t, hf_t, hb_t, lg, o_d):
    wspec = _layer_spec(lw["layer"])
    B, L, _ = x.shape
    tok, scan, mod = _group_specs(sc.shape[0] > 1)
    return pl.pallas_call(
        _merge_kernel,
        grid=(B // SUBLANES, L // TOK_STEPS),
        in_specs=[tok(D_MODEL), mod, mod, mod, wspec((1, D_MODEL)), wspec((D_MODEL, N_BRANCH * D_MODEL)),
                  tok(256), scan, scan, scan, wspec((1, 256)), wspec((256, 512)),
                  scan, scan, tok(256), tok(256), wspec((N_BRANCH, BRANCH_W, D_MODEL)),
                  wspec((D_MODEL, D_MODEL))],
        out_specs=tok(D_MODEL),
        out_shape=jax.ShapeDtypeStruct((B, L, D_MODEL), F32),
        compiler_params=_params(("parallel", "parallel")),
        name="merge",
    )(x, sc, sh, gt, lw["g1"], lw["wg"], o_a, s5u_t, yf_t, yb_t, lw["s5_d"], lw["wglu"],
      hf_t, hb_t, lg, o_d, lw["wb"], lw["wo"])


def _first_index(values, target):
    idx = jnp.full_like(target, float(len(values) - 1))
    for n in range(len(values) - 2, -1, -1):
        idx = jnp.where(values[n] == target, float(n), idx)
    return idx


def _list_max(values):
    return functools.reduce(jnp.maximum, values)


def _moe_sort_kernel(x_ref, sc_ref, sh_ref, g2_ref, wrh_ref, wrl_ref, br_ref, tri_ref,
                     hs_ref, gs_ref, dcol_ref, meta_ref):
    for s in range(x_ref.shape[0] // MOE_SORT_TILE):
        tok = pl.ds(s * MOE_SORT_TILE, MOE_SORT_TILE)
        srt = pl.ds(s * MOE_SORTED_ROWS, MOE_SORTED_ROWS)
        _moe_sort_tile(x_ref.at[tok], sc_ref, sh_ref, g2_ref, wrh_ref, wrl_ref, br_ref, tri_ref,
                       hs_ref.at[srt], gs_ref.at[srt], dcol_ref.at[tok], meta_ref.at[s])


def _moe_sort_tile(x_ref, sc_ref, sh_ref, g2_ref, wrh_ref, wrl_ref, br_ref, tri_ref,
                   hs_ref, gs_ref, dcol_ref, meta_ref):
    T = MOE_SORT_TILE
    h = _modnorm(x_ref[...], g2_ref[...], sc_ref[...], sh_ref[...])
    hh = h.astype(BF16)
    hl = (h - hh.astype(F32)).astype(BF16)
    nt = (((1,), (1,)), ((), ()))
    logits = (jnp.dot(hh, wrh_ref[...], preferred_element_type=F32)
              + jnp.dot(hl, wrh_ref[...], preferred_element_type=F32)
              + jnp.dot(hh, wrl_ref[...], preferred_element_type=F32)) + br_ref[...]
    lt = logits.T
    gl = [lt[g:g + 1, :] for g in range(N_GROUPS)]
    gmax = _list_max(gl)
    g_idx = _first_index(gl, gmax)
    pg = 1.0 / sum(jnp.exp(v - gmax) for v in gl)
    hot = [g_idx == float(g) for g in range(N_GROUPS)]
    el = []
    for e in range(EXPERTS_PER_GROUP):
        v = jnp.zeros_like(gmax)
        for g in range(N_GROUPS):
            r = N_GROUPS + EXPERTS_PER_GROUP * g + e
            v = jnp.where(hot[g], lt[r:r + 1, :], v)
        el.append(v)
    emax = _list_max(el)
    ee = [jnp.exp(v - emax) for v in el]
    esum = sum(ee)
    pe = [v / esum for v in ee]
    v1 = _list_max(pe)
    i1 = _first_index(pe, v1)
    pe2 = [jnp.where(i1 == float(e), -1.0, pe[e]) for e in range(EXPERTS_PER_GROUP)]
    v2 = _list_max(pe2)
    i2 = _first_index(pe2, v2)
    tot = v1 + v2
    w = [jnp.where(i1 == float(e), pg * v1 / tot, jnp.where(i2 == float(e), pg * v2 / tot, 0.0))
         for e in range(EXPERTS_PER_GROUP)]

    zero_row = jnp.zeros_like(gmax)
    g8 = jnp.concatenate([jnp.where(hot[g], 1.0, 0.0) for g in range(N_GROUPS)] + [zero_row] * 4, axis=0)
    cum = jnp.dot(g8.astype(BF16), tri_ref[...], preferred_element_type=F32)
    off = jnp.zeros((1, 1), F32)
    dest = zero_row
    counts = []
    for g in range(N_GROUPS):
        cnt = cum[g:g + 1, T - 1:T]
        padded = jnp.floor((cnt + (MOE_ROW_ALIGN - 1.0)) * (1.0 / MOE_ROW_ALIGN)) * MOE_ROW_ALIGN
        dest = jnp.where(hot[g], off + cum[g:g + 1, :] - 1.0, dest)
        off = off + padded
        counts.append(padded)
    rows = lax.broadcasted_iota(jnp.int32, (MOE_SORTED_ROWS, T), 0).astype(F32)
    perm = jnp.where(rows == dest, 1.0, 0.0).astype(BF16)
    hs_ref[...] = jnp.dot(perm, hh, preferred_element_type=F32).astype(BF16)
    gates = jnp.concatenate(w + [jnp.zeros((LANES - EXPERTS_PER_GROUP, T), F32)], axis=0)
    ghi = gates.astype(BF16)
    glo = (gates - ghi.astype(F32)).astype(BF16)
    gs_ref[...] = (lax.dot_general(perm, ghi, nt, preferred_element_type=F32)
                   + lax.dot_general(perm, glo, nt, preferred_element_type=F32))
    r_i = lax.broadcasted_iota(jnp.int32, (T, T), 0)
    c_i = lax.broadcasted_iota(jnp.int32, (T, T), 1)
    dcol = jnp.sum(jnp.where(r_i == c_i, jnp.broadcast_to(dest, (T, T)), 0.0), axis=1, keepdims=True)
    dcol_ref[...] = jnp.broadcast_to(dcol, (T, LANES))
    meta_ref[...] = jnp.concatenate([jnp.broadcast_to(c, (1, LANES)) for c in counts]
                                    + [jnp.zeros((SUBLANES - N_GROUPS, LANES), F32)], axis=0)


def _moe_sort(xf, sc, sh, lw, tiles_per_mod):
    wspec = _layer_spec(lw["layer"])
    N = xf.shape[0]
    T = MOE_SORT_TILE
    n_tiles = N // T
    sub = MOE_SORT_SUBTILES
    per_b = sc.shape[0] > 1
    mod = pl.BlockSpec((None, 1, D_MODEL), lambda t: (t * sub // tiles_per_mod if per_b else 0, 0, 0))
    sds = jax.ShapeDtypeStruct
    return pl.pallas_call(
        _moe_sort_kernel,
        grid=(n_tiles // sub,),
        in_specs=[pl.BlockSpec((sub * T, D_MODEL), lambda t: (t, 0)), mod, mod, wspec((1, D_MODEL)),
                  wspec((D_MODEL, LANES)), wspec((D_MODEL, LANES)), wspec((1, LANES)), wspec((T, T))],
        out_specs=[pl.BlockSpec((sub * MOE_SORTED_ROWS, D_MODEL), lambda t: (t, 0)),
                   pl.BlockSpec((sub * MOE_SORTED_ROWS, LANES), lambda t: (t, 0)),
                   pl.BlockSpec((sub * T, LANES), lambda t: (t, 0)),
                   pl.BlockSpec((sub, SUBLANES, LANES), lambda t: (t, 0, 0))],
        out_shape=[sds((n_tiles * MOE_SORTED_ROWS, D_MODEL), BF16), sds((n_tiles * MOE_SORTED_ROWS, LANES), F32),
                   sds((N, LANES), F32), sds((n_tiles, SUBLANES, LANES), F32)],
        compiler_params=_params(("parallel",)),
        name="moe_sort",
    )(xf, sc, sh, lw["g2"], lw["wr_hi"], lw["wr_lo"], lw["br"], lw["tri"])


def _moe_expert_kernel(cnt_ref, hs_ref, gs_ref, wga_ref, wup_ref, wdn_ref, ys_ref, ch_ref, cg_ref, cy_ref):
    j = pl.program_id(0)
    g = pl.program_id(1)
    piece = MOE_ROW_ALIGN

    @pl.when((j == 0) & (g == 0))
    def _():
        ch_ref[...] = jnp.zeros_like(ch_ref)
        cg_ref[...] = jnp.zeros_like(cg_ref)

    @pl.when(g == 0)
    def _():
        ys_ref[...] = jnp.zeros_like(ys_ref)

    def for_each_piece(move):
        packed = jnp.int32(0)
        for t in range(MOE_BLOCK_TILES):
            base = (j * MOE_BLOCK_TILES + t) * N_GROUPS
            start = jnp.int32(t * MOE_SORTED_ROWS)
            for g2 in range(N_GROUPS - 1):
                start = start + jnp.where(g2 < g, cnt_ref[base + g2], 0)
            n = cnt_ref[base + g]

            def body(k, carry, start=start, packed=packed):
                move(pl.multiple_of(start + k * piece, piece), pl.multiple_of(packed + k * piece, piece))
                return carry

            lax.fori_loop(0, lax.shift_right_logical(n, 4), body, 0)
            packed = packed + n
        return packed

    def pack(src, dst):
        ch_ref[pl.ds(dst, piece), :] = hs_ref[pl.ds(src, piece), :]
        cg_ref[pl.ds(dst, piece), :] = gs_ref[pl.ds(src, piece), :]

    rows = for_each_piece(pack)

    def experts(r, size):
        h = ch_ref[pl.ds(r, size), :]
        acts = []
        for e in range(EXPERTS_PER_GROUP):
            a = jnp.dot(h, wga_ref[e], preferred_element_type=F32)
            u = jnp.dot(h, wup_ref[e], preferred_element_type=F32)
            acts.append(((a * jax.nn.sigmoid(a)) * u * cg_ref[pl.ds(r, size), e:e + 1]).astype(BF16))
        y = jnp.dot(jnp.concatenate(acts, axis=1), wdn_ref[...], preferred_element_type=F32)
        cy_ref[pl.ds(r, size), :] = y.astype(BF16)

    big, mid, small = MOE_CHUNK, MOE_CHUNK // 2, MOE_CHUNK // 4
    n_big = lax.shift_right_logical(rows, big.bit_length() - 1)

    def chunk(c, carry):
        experts(pl.multiple_of(c * big, big), big)
        return carry

    lax.fori_loop(0, n_big, chunk, 0)
    tail = pl.multiple_of(n_big * big, big)
    rem = rows - tail
    pl.when(rem > mid + small)(lambda: experts(tail, big))
    pl.when((rem > small) & (rem <= mid + small))(lambda: experts(tail, mid))
    pl.when((rem > mid) & (rem <= mid + small))(lambda: experts(pl.multiple_of(tail + mid, small), small))
    pl.when((rem > 0) & (rem <= small))(lambda: experts(tail, small))

    def unpack(src, dst):
        ys_ref[pl.ds(src, piece), :] = cy_ref[pl.ds(dst, piece), :]

    for_each_piece(unpack)


def _moe_experts(cnt, hs, gs, lw):
    rows = MOE_BLOCK_TILES * MOE_SORTED_ROWS
    cap = MOE_BLOCK_TILES * MOE_SORT_TILE
    blk = lambda c: pl.BlockSpec((rows, c), lambda j, g, cnt: (j, 0))
    layer = lw["layer"]
    wspec = lambda r, c: pl.BlockSpec((None, EXPERTS_PER_GROUP, r, c), lambda j, g, cnt: (layer, g, 0, 0))
    return pl.pallas_call(
        _moe_expert_kernel,
        grid_spec=pltpu.PrefetchScalarGridSpec(
            num_scalar_prefetch=1, grid=(hs.shape[0] // rows, N_GROUPS),
            in_specs=[blk(D_MODEL), blk(LANES), wspec(D_MODEL, EXPERT_FF), wspec(D_MODEL, EXPERT_FF),
                      pl.BlockSpec((None, EXPERTS_PER_GROUP * EXPERT_FF, D_MODEL),
                                   lambda j, g, cnt: (layer, g, 0))],
            out_specs=blk(D_MODEL),
            scratch_shapes=[pltpu.VMEM((cap, D_MODEL), BF16), pltpu.VMEM((cap, LANES), F32),
                            pltpu.VMEM((cap, D_MODEL), BF16)]),
        out_shape=jax.ShapeDtypeStruct(hs.shape, BF16),
        compiler_params=_params(("arbitrary", "arbitrary")),
        name="moe_experts",
    )(cnt, hs, gs, lw["wga"], lw["wup"], lw["wdn"])


def _moe_unsort_kernel(x_ref, gt_ref, ys_ref, dcol_ref, o_ref):
    cols = lax.broadcasted_iota(jnp.int32, (MOE_SORT_TILE, MOE_SORTED_ROWS), 1).astype(F32)
    perm_t = jnp.where(cols == dcol_ref[:, 0:1], 1.0, 0.0).astype(BF16)
    y = jnp.dot(perm_t, ys_ref[...], preferred_element_type=F32)
    o_ref[...] = x_ref[...] + gt_ref[...] * y


def _moe_unsort(xf, gt, ys, dcol, tiles_per_mod):
    N = xf.shape[0]
    T = MOE_SORT_TILE
    per_b = gt.shape[0] > 1
    mod = pl.BlockSpec((None, 1, D_MODEL), lambda t: (t // tiles_per_mod if per_b else 0, 0, 0))
    tok = pl.BlockSpec((T, D_MODEL), lambda t: (t, 0))
    return pl.pallas_call(
        _moe_unsort_kernel,
        grid=(N // T,),
        in_specs=[tok, mod, pl.BlockSpec((MOE_SORTED_ROWS, D_MODEL), lambda t: (t, 0)),
                  pl.BlockSpec((T, LANES), lambda t: (t, 0))],
        out_specs=tok,
        out_shape=jax.ShapeDtypeStruct((N, D_MODEL), F32),
        compiler_params=_params(("parallel",)),
        name="moe_unsort",
    )(xf, gt, ys, dcol)


def _moe(x, sc, sh, gt, lw):
    B, L, _ = x.shape
    N = B * L
    T = MOE_SORT_TILE
    xf = x.reshape(N, D_MODEL)
    hs, gs, dcol, meta = _moe_sort(xf, sc, sh, lw, L // T)

    cnt = meta[:, :N_GROUPS, 0].astype(jnp.int32).reshape(-1)
    ys = _moe_experts(cnt, hs, gs, lw)
    out = _moe_unsort(xf, gt, ys, dcol, L // T)
    return out.reshape(B, L, D_MODEL)


def _block_diag(blocks):
    n, r, c = blocks.shape
    eye = jnp.eye(n, dtype=blocks.dtype)
    return jnp.einsum("nrc,nm->nrmc", blocks, eye).reshape(n * r, n * c)


def _rope_tables(seq_len, dim, width, offset):
    rows = seq_len // GRID_W
    r, col = np.meshgrid(np.arange(rows), np.arange(GRID_W), indexing="ij")
    r = r.reshape(-1).astype(np.float64)
    col = col.reshape(-1).astype(np.float64)
    quarter = dim // 4
    freqs = ROPE_BASE ** (-np.arange(quarter, dtype=np.float64) / quarter)
    ang_r = r[:, None] * freqs
    ang_c = col[:, None] * freqs
    zero = np.zeros_like(ang_r)
    cos = np.cos(np.concatenate([ang_r, ang_r, ang_c, ang_c], axis=-1))
    sin_next = np.concatenate([-np.sin(ang_r), zero, -np.sin(ang_c), zero], axis=-1)
    sin_prev = np.concatenate([zero, np.sin(ang_r), zero, np.sin(ang_c)], axis=-1)

    def place(t, fill):
        return np.pad(t, ((0, 0), (offset, width - offset - dim)), constant_values=fill).astype(np.float32)

    return place(cos, 1.0), place(sin_next, 0.0), place(sin_prev, 0.0)


def _layer_weights(l, w):
    lw = {}
    w_in = w["w_in"][l]
    q_cols = w_in[:, 0:256].reshape(D_MODEL, A_HEADS, A_HEAD_DIM)[:, HEAD_ORDER].reshape(D_MODEL, 256)
    kr_cols = jnp.pad(w_in[:, 1664:1696], ((0, 0), (MLA_NOPE, LANES - MLA_QK)))
    lw["wz"] = jnp.concatenate([q_cols, w_in[:, 256:1664], kr_cols], axis=1).astype(BF16)
    lw["wg"] = w_in[:, 1696:].astype(BF16)
    lw["g1"] = w["norm1_g"][l][None]
    lw["g2"] = w["norm2_g"][l][None]
    lw["gq"] = jnp.tile(w["a_qnorm_g"][l], A_HEADS)[None] * (A_HEAD_DIM ** -0.5 * LOG2_E)
    lw["gk"] = jnp.tile(w["a_knorm_g"][l], A_KV_HEADS)[None]
    lw["sink"] = w["a_sink"][l]
    lw["qlg"] = w["mla_q_lat_norm"][l][None]
    wuq = w["mla_w_uq"][l].reshape(MLA_Q_LORA, MLA_HEADS, MLA_QK)
    lw["wuq"] = jnp.pad(wuq, ((0, 0), (0, 0), (0, LANES - MLA_QK))).reshape(MLA_Q_LORA, MLA_HEADS * LANES).astype(BF16)
    pad_g = lambda g: jnp.tile(jnp.pad(g, (0, LANES - MLA_QK)), MLA_HEADS)[None]
    lw["gmq"] = pad_g(w["mla_qnorm_g"][l]) * (MLA_QK ** -0.5 * LOG2_E)
    lw["gmk"] = pad_g(w["mla_knorm_g"][l])
    lw["kvg"] = w["mla_kv_norm"][l][None]
    wukv = w["mla_w_ukv"][l].reshape(MLA_KV_LORA, MLA_HEADS, MLA_NOPE + MLA_V)
    wk = jnp.pad(wukv[:, :, :MLA_NOPE], ((0, 0), (0, 0), (0, LANES - MLA_NOPE))).reshape(MLA_KV_LORA, MLA_HEADS * LANES)
    wv = wukv[:, :, MLA_NOPE:].reshape(MLA_KV_LORA, MLA_HEADS * MLA_V)
    lw["wukv"] = jnp.concatenate([wk, wv], axis=1).astype(BF16)

    lre = w["s5_lambda_re"][l]
    lim = w["s5_lambda_im"][l]
    dt = jnp.exp(w["s5_log_step"][l])[:, :, None]
    mag = jnp.exp(lre * dt)
    ar, ai = mag * jnp.cos(lim * dt), mag * jnp.sin(lim * dt)
    den = lre * lre + lim * lim
    fr = ((ar - 1.0) * lre + ai * lim) / den
    fi = (ai * lre - (ar - 1.0) * lim) / den
    br, bi = w["s5_b_re"][l], w["s5_b_im"][l]
    bbr = fr[..., None] * br - fi[..., None] * bi
    bbi = fr[..., None] * bi + fi[..., None] * br
    in_map = lambda d: jnp.concatenate(
        [_block_diag(jnp.swapaxes(bbr[d], 1, 2)), _block_diag(jnp.swapaxes(bbi[d], 1, 2))], axis=1).astype(BF16)
    out_map = lambda d: jnp.concatenate(
        [_block_diag(jnp.swapaxes(w["s5_c_re"][l][d], 1, 2)),
         -_block_diag(jnp.swapaxes(w["s5_c_im"][l][d], 1, 2))], axis=0).astype(BF16)
    lw["s5_bf"], lw["s5_bb"] = in_map(0), in_map(1)
    lw["s5_cf"], lw["s5_cb"] = out_map(0), out_map(1)
    coef = jnp.stack([ar[0].reshape(-1), ai[0].reshape(-1), ar[1].reshape(-1), ai[1].reshape(-1)])
    lw["s5_a"] = jnp.broadcast_to(coef[:, None, :], (4, SUBLANES, S5_WIDTH))
    lw["s5_d"] = w["s5_d"][l][None]
    lw["wglu"] = w["s5_w_glu"][l].astype(BF16)

    lw["lru_cw"] = w["lru_conv_w"][l]
    lw["lru_cb"] = w["lru_conv_b"][l][None]
    gate_w = lambda d: jnp.concatenate(
        [_block_diag(w["lru_w_a"][l][d]), _block_diag(w["lru_w_x"][l][d])], axis=1).astype(BF16)
    gate_b = lambda d: jnp.concatenate([w["lru_b_a"][l][d], w["lru_b_x"][l][d]])[None]
    lw["lru_wf"], lw["lru_wb"] = gate_w(0), gate_w(1)
    lw["lru_bf"], lw["lru_bb"] = gate_b(0), gate_b(1)
    sp = jax.nn.softplus(-w["lru_lambda"][l])
    lw["lru_spf"], lw["lru_spb"] = sp[0][None], sp[1][None]

    wb = w["w_branch"][l]
    wb0 = wb[0].reshape(A_HEADS, A_HEAD_DIM, D_MODEL)[HEAD_ORDER].reshape(BRANCH_W, D_MODEL)
    lw["wb"] = jnp.concatenate([wb0[None], wb[1:]], axis=0).astype(BF16)
    lw["wo"] = w["w_out"][l].astype(BF16)

    wr = jnp.pad(jnp.concatenate([w["moe_w_group"][l], w["moe_w_expert"][l]], axis=1),
                 ((0, 0), (0, LANES - N_GROUPS - N_EXPERTS)))
    lw["wr_hi"] = wr.astype(BF16)
    lw["wr_lo"] = (wr - lw["wr_hi"].astype(F32)).astype(BF16)
    lw["br"] = jnp.pad(jnp.concatenate([w["moe_b_group"][l], w["moe_b_expert"][l]]),
                       (0, LANES - N_GROUPS - N_EXPERTS))[None]
    lw["tri"] = jnp.triu(jnp.ones((MOE_SORT_TILE, MOE_SORT_TILE), BF16))
    lw["wga"] = w["moe_w_gate"][l].astype(BF16)
    lw["wup"] = w["moe_w_up"][l].astype(BF16)
    lw["wdn"] = w["moe_w_down"][l].astype(BF16).reshape(N_EXPERTS * EXPERT_FF, D_MODEL)
    return lw


def _rows_to_state(fin, width):
    Bg, n = fin.shape[0], fin.shape[1]
    return jnp.swapaxes(fin, 1, 2).reshape(Bg * SUBLANES, n, width)


def _state_to_rows(state):
    B, n, width = state.shape
    return jnp.swapaxes(state.reshape(B // SUBLANES, SUBLANES, n, width), 1, 2)


def _mix(x, mods, lw, rope_tabs, cache, s5_h0, lru_h0, ctx):
    sh1, sc1, gt1 = mods
    outs = _in_proj(x, sc1, sh1, lw, rope_tabs, ctx)
    q, k, v, s5u_t, lx_t, lg, qm, km, vm = outs[:9]
    if ctx:
        o_a = _attn_a_ctx(q, k, v, lw["sink"][lw["layer"]])
        o_d = _mla_ctx(qm, km, vm)
    else:
        ck, cv, cckv, ckr = cache
        o_a = _attn_a_lat(q, k, v, ck, cv, lw["sink"][lw["layer"]])
        kc, vc = _mla_cache_kv(cckv, ckr, lw)
        o_d = _mla_lat(qm, km, vm, kc, vc)
    yf, yb, s5_fin = _s5_scan(s5u_t, lw, s5_h0)
    hf, hb, lru_fin = _lru_scan(lx_t, lw, lru_h0)
    x = _merge(x, sc1, sh1, gt1, lw, o_a, s5u_t, yf, yb, hf, hb, lg, o_d)
    return x, outs[9:], s5_fin, lru_fin


def kernel(x_prompt, x_sample, cache_attn_k, cache_attn_v, cache_mla_ckv, cache_mla_krope, state_ssm_re, state_ssm_im, state_lru, c, c_ctx, w_mod, b_mod, norm1_g, norm2_g, w_in, a_qnorm_g, a_knorm_g, a_sink, s5_lambda_re, s5_lambda_im, s5_log_step, s5_b_re, s5_b_im, s5_c_re, s5_c_im, s5_d, s5_w_glu, lru_conv_w, lru_conv_b, lru_w_a, lru_b_a, lru_w_x, lru_b_x, lru_lambda, mla_q_lat_norm, mla_w_uq, mla_kv_norm, mla_w_ukv, mla_qnorm_g, mla_knorm_g, w_branch, w_out, moe_w_group, moe_b_group, moe_w_expert, moe_b_expert, moe_w_gate, moe_w_up, moe_w_down):
    w = dict(norm1_g=norm1_g, norm2_g=norm2_g, w_in=w_in, a_qnorm_g=a_qnorm_g, a_knorm_g=a_knorm_g,
             a_sink=a_sink, s5_lambda_re=s5_lambda_re, s5_lambda_im=s5_lambda_im, s5_log_step=s5_log_step,
             s5_b_re=s5_b_re, s5_b_im=s5_b_im, s5_c_re=s5_c_re, s5_c_im=s5_c_im, s5_d=s5_d, s5_w_glu=s5_w_glu,
             lru_conv_w=lru_conv_w, lru_conv_b=lru_conv_b, lru_w_a=lru_w_a, lru_b_a=lru_b_a, lru_w_x=lru_w_x,
             lru_b_x=lru_b_x, lru_lambda=lru_lambda, mla_q_lat_norm=mla_q_lat_norm, mla_w_uq=mla_w_uq,
             mla_kv_norm=mla_kv_norm, mla_w_ukv=mla_w_ukv, mla_qnorm_g=mla_qnorm_g, mla_knorm_g=mla_knorm_g,
             w_branch=w_branch, w_out=w_out, moe_w_group=moe_w_group, moe_b_group=moe_b_group,
             moe_w_expert=moe_w_expert, moe_b_expert=moe_b_expert, moe_w_gate=moe_w_gate, moe_w_up=moe_w_up,
             moe_w_down=moe_w_down)
    B, L, _ = x_prompt.shape
    Bd, Ld, _ = x_sample.shape
    P = cache_attn_k.shape[2]

    cond = jnp.zeros((2 * SUBLANES, D_MODEL), F32).at[:Bd].set(c).at[Bd].set(c_ctx)
    mod = _modulation(cond, w_mod, b_mod[:, None, :])
    mod = mod.reshape(DEPTH, 2 * SUBLANES, 6, D_MODEL)

    rope_a = _rope_tables(Ld, A_HEAD_DIM, A_HEAD_DIM, 0)
    rope_a = tuple(jnp.asarray(np.tile(t, (1, A_HEADS))) for t in rope_a)
    rope_m = tuple(jnp.asarray(t) for t in _rope_tables(Ld, MLA_ROPE, LANES, MLA_NOPE))

    lw_all = jax.vmap(lambda w1: _layer_weights(0, {name: v[None] for name, v in w1.items()}))(w)

    xp, xs = x_prompt, x_sample
    ak_l, av_l, ckv_l, kr_l, sr_l, si_l, lru_l = [], [], [], [], [], [], []
    for l in range(DEPTH):
        lw = dict(lw_all, layer=l)
        lat_mod = [mod[l, :Bd, n][:, None, :] for n in range(6)]
        ctx_mod = [mod[l, Bd:Bd + 1, n][:, None, :] for n in range(6)]

        zs5 = jnp.zeros((B // SUBLANES, 4, SUBLANES, S5_WIDTH), F32)
        zlru = jnp.zeros((B // SUBLANES, 2, SUBLANES, LRU_WIDTH), F32)
        xp, (k32, v32, ckv_n, krp), s5_fin, lru_fin = _mix(xp, ctx_mod[0:3], lw, None, None, zs5, zlru, True)
        xp = _moe(xp, ctx_mod[4], ctx_mod[3], ctx_mod[5], lw)
        ak_l.append(k32.reshape(B, L, A_KV_HEADS, A_HEAD_DIM))
        av_l.append(v32.reshape(B, L, A_KV_HEADS, A_HEAD_DIM))
        ckv_l.append(ckv_n)
        kr_l.append(krp[:, :, MLA_NOPE:MLA_QK])
        fin = _rows_to_state(s5_fin, S5_WIDTH)
        sr_l.append(fin[:, 0::2].reshape(B, 2, S5_GROUPS, S5_STATE))
        si_l.append(fin[:, 1::2].reshape(B, 2, S5_GROUPS, S5_STATE))
        lru_l.append(_rows_to_state(lru_fin, LRU_WIDTH))

        sre = state_ssm_re[:, l].reshape(Bd, 2, S5_WIDTH)
        sim = state_ssm_im[:, l].reshape(Bd, 2, S5_WIDTH)
        s5_h0 = _state_to_rows(jnp.stack([sre[:, 0], sim[:, 0], sre[:, 1], sim[:, 1]], axis=1))
        lru_h0 = _state_to_rows(state_lru[:, l])
        cache = (cache_attn_k[:, l].reshape(Bd, P, 128).astype(BF16),
                 cache_attn_v[:, l].reshape(Bd, P, 128).astype(BF16),
                 cache_mla_ckv[:, l],
                 jnp.pad(cache_mla_krope[:, l], ((0, 0), (0, 0), (MLA_NOPE, LANES - MLA_QK))))
        xs, _, _, _ = _mix(xs, lat_mod[0:3], lw, rope_a + rope_m, cache, s5_h0, lru_h0, False)
        xs = _moe(xs, lat_mod[4], lat_mod[3], lat_mod[5], lw)

    stack = lambda ts: jnp.stack(ts, axis=1)
    return (xp, xs, stack(ak_l), stack(av_l), stack(ckv_l), stack(kr_l), stack(sr_l), stack(si_l), stack(lru_l))
```

```python
import functools
import math

import jax
import jax.numpy as jnp
import numpy as np
from jax import lax
from jax.experimental import pallas as pl
from jax.experimental.pallas import tpu as pltpu

F32 = jnp.float32
BF16 = jnp.bfloat16

D_MODEL = 1024
DEPTH = 2
GRID_W = 64
N_BRANCH = 4
BRANCH_W = 256
ROPE_BASE = 10000.0
EPS = 1e-6
NEG_INF = -1e30
LOG2_E = math.log2(math.e)
A_HEADS = 4
A_KV_HEADS = 2
A_HEAD_DIM = 64
WINDOW = 128
Q_BLOCK = 128
S5_GROUP = 16
S5_GROUPS = 16
S5_STATE = 64
S5_WIDTH = S5_GROUPS * S5_STATE
LRU_WIDTH = 256
LRU_BLOCKS = 4
LRU_CONV = 4
LRU_C = 8.0
MLA_HEADS = 4
MLA_Q_LORA = 256
MLA_KV_LORA = 128
MLA_NOPE = 64
MLA_ROPE = 32
MLA_V = 64
MLA_QK = MLA_NOPE + MLA_ROPE
N_GROUPS = 4
EXPERTS_PER_GROUP = 4
N_EXPERTS = 16
EXPERT_FF = 256

LANES = 128
SUBLANES = 8
VMEM_LIMIT = 56 * 1024 * 1024

TOK_TILE = 256
TOK_STEPS = TOK_TILE // SUBLANES
SCAN_STEPS = 128
MOE_SORT_TILE = 256
MOE_ROW_ALIGN = 16
MOE_SORTED_ROWS = MOE_SORT_TILE + LANES
MOE_SORT_SUBTILES = 8
MOE_BLOCK_TILES = 8
MOE_CHUNK = 512
MLA_Q_TILE = 256
ATTN_Q_TILE = 256
Z_COLS = 1792
HEAD_ORDER = np.array((0, 2, 1, 3))


def _params(sem):
    return pltpu.CompilerParams(dimension_semantics=sem, vmem_limit_bytes=VMEM_LIMIT)


def _layer_spec(layer):
    def spec(shape):
        n = len(shape)
        return pl.BlockSpec((None,) + tuple(shape), lambda *_: (layer,) + (0,) * n)
    return spec


def _modnorm(x, g, sc, sh):
    ms = jnp.mean(x * x, axis=-1, keepdims=True)
    return (x * lax.rsqrt(ms + EPS)) * g * (1.0 + sc) + sh


def _rmsnorm(x, g):
    ms = jnp.mean(x * x, axis=-1, keepdims=True)
    return (x * lax.rsqrt(ms + EPS)) * g


def _segment_ones(width, segment):
    lane = np.arange(width) // segment
    return jnp.asarray(lane[:, None] == lane[None, :], BF16)


def _segment_rsqrt(x, ones, n_real):
    if isinstance(ones, int):
        segment = ones
        lane = lax.broadcasted_iota(jnp.int32, (1, LANES), 1)
        outs = []
        for t in range(x.shape[1] // LANES):
            xt = x[:, LANES * t:LANES * (t + 1)]
            sq = xt * xt
            inv = None
            for s in range(LANES // segment):
                mask = (lane >= s * segment) & (lane < (s + 1) * segment)
                part = sq if segment == LANES else jnp.where(mask, sq, 0.0)
                r = lax.rsqrt(jnp.sum(part, axis=-1, keepdims=True) * (1.0 / n_real) + EPS)
                inv = r if inv is None else jnp.where(mask, r, inv)
            outs.append(jnp.broadcast_to(inv, sq.shape))
        return jnp.concatenate(outs, axis=1)
    sq = x * x
    hi = sq.astype(BF16)
    lo = (sq - hi.astype(F32)).astype(BF16)
    ss = jnp.dot(hi, ones, preferred_element_type=F32) + jnp.dot(lo, ones, preferred_element_type=F32)
    return lax.rsqrt(ss * (1.0 / n_real) + EPS)


def _rope(x, cos, sin_next, sin_prev, quarter):
    width = x.shape[1]
    return (x * cos + pltpu.roll(x, width - quarter, 1) * sin_next
            + pltpu.roll(x, quarter, 1) * sin_prev)


def _mla_kv(ckv_n, kr_placed, wukv, gk, ones_tile, rope_tabs):
    kv = jnp.dot(ckv_n.astype(BF16), wukv, preferred_element_type=F32)
    gain = gk[:, :LANES]
    kfull = kv[:, :MLA_HEADS * LANES] + jnp.concatenate([kr_placed] * MLA_HEADS, axis=1)
    inv_all = _segment_rsqrt(kfull, ones_tile, MLA_QK)
    rotated = None
    if rope_tabs is not None:
        cos, sin_next, sin_prev = rope_tabs
        quarter = MLA_ROPE // 4
        base = kr_placed * gain
        rotated = pltpu.roll(base, LANES - quarter, 1) * sin_next + pltpu.roll(base, quarter, 1) * sin_prev
    heads = []
    for h in range(MLA_HEADS):
        kf = kfull[:, LANES * h:LANES * (h + 1)]
        inv = inv_all[:, LANES * h:LANES * (h + 1)]
        kh = kf * gain
        if rotated is not None:
            kh = kh * cos + rotated
        heads.append(kh * inv)
    return jnp.concatenate(heads, axis=1), kv[:, MLA_HEADS * LANES:]


def _mod_kernel(c_ref, w_ref, b_ref, o_ref):
    c = c_ref[...]
    s = c * jax.nn.sigmoid(c)
    o_ref[...] = jnp.dot(s.astype(BF16), w_ref[...].astype(BF16), preferred_element_type=F32) + b_ref[...]


def _modulation(cond, w_mod, b_mod):
    n_rows = cond.shape[0]
    n_out = w_mod.shape[-1]
    tn = 1024
    return pl.pallas_call(
        _mod_kernel,
        grid=(DEPTH, n_out // tn),
        in_specs=[pl.BlockSpec((n_rows, D_MODEL), lambda l, j: (0, 0)),
                  pl.BlockSpec((None, D_MODEL, tn), lambda l, j: (l, 0, j)),
                  pl.BlockSpec((None, 1, tn), lambda l, j: (l, 0, j))],
        out_specs=pl.BlockSpec((None, n_rows, tn), lambda l, j: (l, 0, j)),
        out_shape=jax.ShapeDtypeStruct((DEPTH, n_rows, n_out), F32),
        compiler_params=_params(("arbitrary", "arbitrary")),
        name="modulation",
    )(cond, w_mod, b_mod)


def _group_rows(ref):
    v = ref[...]
    return v.reshape(v.shape[0] * v.shape[1], v.shape[2])


def _mod_rows(ref, steps):
    v = ref[...]
    if v.shape[0] == 1:
        return v[0]
    return jnp.broadcast_to(v, (v.shape[0], steps, v.shape[2])).reshape(v.shape[0] * steps, v.shape[2])


def _store_group(ref, val):
    ref[...] = val.reshape(ref.shape).astype(ref.dtype)


def _store_time_major(ref, val):
    steps = ref.shape[1] // SUBLANES
    for b in range(SUBLANES):
        for half in range(2):
            ref[half, pl.ds(b, steps, stride=SUBLANES), :] = (
                val[b * steps:(b + 1) * steps, LANES * half:LANES * (half + 1)])


def _load_time_major(ref):
    steps = ref.shape[1] // SUBLANES
    return jnp.concatenate(
        [jnp.concatenate([ref[half, pl.ds(b, steps, stride=SUBLANES), :] for half in range(2)], axis=1)
         for b in range(SUBLANES)], axis=0)


def _in_proj_kernel(rope, ctx, *refs):
    it = iter(refs)
    x_ref, sc_ref, sh_ref, g1_ref, wz_ref = (next(it) for _ in range(5))
    gq_ref, gk_ref, qlg_ref, wuq_ref, gmq_ref, kvg_ref, wukv_ref, gmk_ref = (next(it) for _ in range(8))
    oh_ref, ot_ref = next(it), next(it)
    seg_heads = oh_ref[...] if rope else A_HEAD_DIM
    seg_tiles = ot_ref[...] if rope else LANES
    if rope:
        ca_ref, sna_ref, spa_ref, cm_ref, snm_ref, spm_ref = (next(it) for _ in range(6))
    q_ref, k_ref, v_ref, s5u_ref, lx_ref, lg_ref, qm_ref, km_ref, vm_ref = (next(it) for _ in range(9))
    if ctx:
        k32_ref, v32_ref, ckv_ref, krp_ref = (next(it) for _ in range(4))

    steps = x_ref.shape[1]
    h = _modnorm(_group_rows(x_ref), g1_ref[...], _mod_rows(sc_ref, steps), _mod_rows(sh_ref, steps))
    z = jnp.dot(h.astype(BF16), wz_ref[...], preferred_element_type=F32)
    per_seq = lambda ref: jnp.concatenate([ref[...]] * SUBLANES, axis=0)

    qk = z[:, 0:384]
    qk = qk * _segment_rsqrt(qk, seg_heads, A_HEAD_DIM)
    q = qk[:, 0:256] * gq_ref[...]
    k = qk[:, 256:384] * gk_ref[...]
    v = z[:, 384:512]
    if ctx:
        _store_group(k32_ref, k)
        _store_group(v32_ref, v)
    if rope:
        quarter = A_HEAD_DIM // 4
        atabs = (per_seq(ca_ref), per_seq(sna_ref), per_seq(spa_ref))
        q = _rope(q, *atabs, quarter)
        k = _rope(k, *(t[:, :LANES] for t in atabs), quarter)
    _store_group(q_ref, q)
    _store_group(k_ref, k)
    _store_group(v_ref, v)

    _store_time_major(s5u_ref, z[:, 512:768])
    _store_time_major(lx_ref, z[:, 768:1024])
    _store_group(lg_ref, z[:, 1024:1280])

    mtabs = mtabs4 = None
    if rope:
        mtabs = tuple(per_seq(r) for r in (cm_ref, snm_ref, spm_ref))
        mtabs4 = tuple(jnp.concatenate([t] * MLA_HEADS, axis=1) for t in mtabs)
    ql = _rmsnorm(z[:, 1280:1536], qlg_ref[...])
    qm = jnp.dot(ql.astype(BF16), wuq_ref[...], preferred_element_type=F32)
    qm = qm * _segment_rsqrt(qm, seg_tiles, MLA_QK) * gmq_ref[...]
    if rope:
        qm = _rope(qm, *mtabs4, MLA_ROPE // 4)
    _store_group(qm_ref, qm)

    ckv_n = _rmsnorm(z[:, 1536:1664], kvg_ref[...])
    krp = z[:, 1664:1792]
    km, vm = _mla_kv(ckv_n, krp, wukv_ref[...], gmk_ref[...], seg_tiles, mtabs)
    _store_group(km_ref, km)
    _store_group(vm_ref, vm)
    if ctx:
        _store_group(ckv_ref, ckv_n)
        _store_group(krp_ref, krp)


def _group_specs(per_seq_mod):
    TS = TOK_STEPS
    tok = lambda c: pl.BlockSpec((SUBLANES, TS, c), lambda g, i: (g, i, 0))
    scan = pl.BlockSpec((None, 2, TS * SUBLANES, LANES), lambda g, i: (g, 0, i, 0))
    if per_seq_mod:
        mod = pl.BlockSpec((SUBLANES, 1, D_MODEL), lambda g, i: (g, 0, 0))
    else:
        mod = pl.BlockSpec((1, 1, D_MODEL), lambda g, i: (0, 0, 0))
    return tok, scan, mod


def _in_proj(x, sc, sh, lw, rope_tabs, ctx):
    wspec = _layer_spec(lw["layer"])
    B, L, _ = x.shape
    Bg = B // SUBLANES
    TS = TOK_STEPS
    rope = rope_tabs is not None
    tok, scan, mod = _group_specs(sc.shape[0] > 1)
    in_specs = [tok(D_MODEL), mod, mod, wspec((1, D_MODEL)), wspec((D_MODEL, Z_COLS)),
                wspec((1, 256)), wspec((1, 128)), wspec((1, 256)), wspec((256, 512)), wspec((1, 512)),
                wspec((1, 128)), wspec((128, 768)), wspec((1, 512)),
                pl.BlockSpec((384, 384), lambda g, i: (0, 0)), pl.BlockSpec((512, 512), lambda g, i: (0, 0))]
    args = [x, sc, sh, lw["g1"], lw["wz"], lw["gq"], lw["gk"], lw["qlg"], lw["wuq"], lw["gmq"],
            lw["kvg"], lw["wukv"], lw["gmk"], _segment_ones(384, A_HEAD_DIM), _segment_ones(MLA_HEADS * LANES, LANES)]
    if rope:
        in_specs += [pl.BlockSpec((TS, 256), lambda g, i: (i, 0))] * 3
        in_specs += [pl.BlockSpec((TS, 128), lambda g, i: (i, 0))] * 3
        args += list(rope_tabs)
    out_specs = [tok(256), tok(128), tok(128), scan, scan, tok(256), tok(512), tok(512), tok(256)]
    sds = jax.ShapeDtypeStruct
    scan_shape = sds((Bg, 2, L * SUBLANES, LANES), F32)
    out_shape = [sds((B, L, 256), BF16), sds((B, L, 128), BF16), sds((B, L, 128), BF16),
                 scan_shape, scan_shape,
                 sds((B, L, 256), F32), sds((B, L, 512), BF16), sds((B, L, 512), BF16),
                 sds((B, L, 256), BF16)]
    if ctx:
        out_specs += [tok(128), tok(128), tok(128), tok(128)]
        out_shape += [sds((B, L, 128), F32)] * 4
    return pl.pallas_call(
        functools.partial(_in_proj_kernel, rope, ctx),
        grid=(Bg, L // TS),
        in_specs=in_specs, out_specs=out_specs, out_shape=out_shape,
        compiler_params=_params(("parallel", "parallel")),
        name="in_proj_ctx" if ctx else "in_proj_lat",
    )(*args)


def _mla_cache_kernel(ckv_ref, krp_ref, wukv_ref, gmk_ref, ot_ref, km_ref, vm_ref):
    km, vm = _mla_kv(ckv_ref[...], krp_ref[...], wukv_ref[...], gmk_ref[...], ot_ref[...], None)
    km_ref[...] = km.astype(BF16)
    vm_ref[...] = vm.astype(BF16)


def _mla_cache_kv(cckv, ckr_placed, lw):
    wspec = _layer_spec(lw["layer"])
    B, P, _ = cckv.shape
    tok = lambda c: pl.BlockSpec((None, P, c), lambda b: (b, 0, 0))
    return pl.pallas_call(
        _mla_cache_kernel,
        grid=(B,),
        in_specs=[tok(128), tok(128), wspec((128, 768)), wspec((1, 512)),
                  pl.BlockSpec((512, 512), lambda b: (0, 0))],
        out_specs=[tok(512), tok(256)],
        out_shape=[jax.ShapeDtypeStruct((B, P, 512), BF16), jax.ShapeDtypeStruct((B, P, 256), BF16)],
        compiler_params=_params(("parallel",)),
        name="mla_cache_kv",
    )(cckv, ckr_placed, lw["wukv"], lw["gmk"], _segment_ones(MLA_HEADS * LANES, LANES))


def _sink_softmax(s, sink):
    sink = sink * LOG2_E
    m = jnp.maximum(jnp.max(s, axis=-1, keepdims=True), sink)
    e = jnp.exp2(s - m)
    den = jnp.sum(e, axis=-1, keepdims=True) + jnp.exp2(sink - m)
    return e.astype(BF16), 1.0 / den


def _gqa_tile(qt, keys, vst, sink_lo, sink_hi, mask):
    lane = lax.broadcasted_iota(jnp.int32, (1, LANES), 1)
    lo = lane < A_HEAD_DIM
    zero = jnp.zeros_like(qt)
    ps, invs = [], []
    for qh, sink in ((jnp.where(lo, qt, zero), sink_lo), (jnp.where(lo, zero, qt), sink_hi)):
        s = lax.dot_general(qh, keys, (((1,), (1,)), ((), ())), preferred_element_type=F32)
        if mask is not None:
            s = jnp.where(mask, s, NEG_INF)
        p, inv = _sink_softmax(s, sink)
        ps.append(p)
        invs.append(inv)
    o = jnp.dot(jnp.concatenate(ps, axis=1), vst, preferred_element_type=F32)
    return o * jnp.where(lo, invs[0], invs[1])


def _stack_kv_halves(v):
    lane = lax.broadcasted_iota(jnp.int32, (1, LANES), 1)
    lo = lane < A_HEAD_DIM
    zero = jnp.zeros_like(v)
    return jnp.concatenate([jnp.where(lo, v, zero), jnp.where(lo, zero, v)], axis=0)


def _attn_a_ctx_kernel(sink_ref, q_ref, k_ref, v_ref, o_ref):
    keys = k_ref[...]
    vst = _stack_kv_halves(v_ref[...])
    for t in range(2):
        o = _gqa_tile(q_ref[:, LANES * t:LANES * (t + 1)], keys, vst, sink_ref[t], sink_ref[2 + t], None)
        o_ref[:, LANES * t:LANES * (t + 1)] = o.astype(o_ref.dtype)


def _attn_a_ctx(q, k, v, sink):
    B, L, _ = q.shape
    tok = lambda c: pl.BlockSpec((None, L, c), lambda b: (b, 0, 0))
    return pl.pallas_call(
        _attn_a_ctx_kernel,
        grid=(B,),
        in_specs=[pl.BlockSpec(memory_space=pltpu.SMEM), tok(256), tok(128), tok(128)],
        out_specs=tok(256),
        out_shape=jax.ShapeDtypeStruct((B, L, 256), BF16),
        compiler_params=_params(("parallel",)),
        name="attn_a_ctx",
    )(sink, q, k, v)


def _attn_a_lat_kernel(sink_ref, q_ref, kc_ref, vc_ref, k_ref, v_ref, o_ref):
    i = pl.program_id(1)
    L = k_ref.shape[0]
    n_ctx = kc_ref.shape[0]
    rows = q_ref.shape[0]
    span = rows + 2 * WINDOW
    start = pl.multiple_of(jnp.clip(i * rows - WINDOW, 0, L - span), WINDOW)
    keys = jnp.concatenate([kc_ref[...], k_ref[pl.ds(start, span), :]], axis=0)
    vals = jnp.concatenate([vc_ref[...], v_ref[pl.ds(start, span), :]], axis=0)
    vst = _stack_kv_halves(vals)
    col = lax.broadcasted_iota(jnp.int32, (rows, n_ctx + span), 1)
    row = lax.broadcasted_iota(jnp.int32, (rows, n_ctx + span), 0)
    rel = (col - n_ctx + start) - (row + i * rows)
    mask = (col < n_ctx) | (jnp.abs(rel) <= WINDOW)
    for t in range(2):
        o = _gqa_tile(q_ref[:, LANES * t:LANES * (t + 1)], keys, vst, sink_ref[t], sink_ref[2 + t], mask)
        o_ref[:, LANES * t:LANES * (t + 1)] = o.astype(o_ref.dtype)


def _attn_a_lat(q, k, v, kc, vc, sink):
    B, L, _ = q.shape
    P = kc.shape[1]
    blk = lambda c: pl.BlockSpec((None, ATTN_Q_TILE, c), lambda b, i: (b, i, 0))
    whole = lambda n, c: pl.BlockSpec((None, n, c), lambda b, i: (b, 0, 0))
    return pl.pallas_call(
        _attn_a_lat_kernel,
        grid=(B, L // ATTN_Q_TILE),
        in_specs=[pl.BlockSpec(memory_space=pltpu.SMEM), blk(256), whole(P, 128), whole(P, 128),
                  whole(L, 128), whole(L, 128)],
        out_specs=blk(256),
        out_shape=jax.ShapeDtypeStruct((B, L, 256), BF16),
        compiler_params=_params(("parallel", "parallel")),
        name="attn_a_lat",
    )(sink, q, kc, vc, k, v)


def _mla_stack_values(vals):
    lane = lax.broadcasted_iota(jnp.int32, (1, MLA_HEADS * MLA_V), 1)
    zero = jnp.zeros_like(vals)
    return jnp.concatenate([jnp.where((lane >= MLA_V * h) & (lane < MLA_V * (h + 1)), vals, zero)
                            for h in range(MLA_HEADS)], axis=0)


def _mla_attend(q, key_parts, vst):
    lane = lax.broadcasted_iota(jnp.int32, (1, MLA_HEADS * MLA_V), 1)
    ps = []
    inv = jnp.zeros((q.shape[0], MLA_HEADS * MLA_V), F32)
    for h in range(MLA_HEADS):
        qh = q[:, LANES * h:LANES * (h + 1)]
        ss = [lax.dot_general(qh, kp[:, LANES * h:LANES * (h + 1)], (((1,), (1,)), ((), ())),
                              preferred_element_type=F32) for kp in key_parts]
        m = functools.reduce(jnp.maximum, [jnp.max(s, axis=-1, keepdims=True) for s in ss])
        es = [jnp.exp2(s - m) for s in ss]
        den = functools.reduce(lambda a, b: a + b, [jnp.sum(e, axis=-1, keepdims=True) for e in es])
        ps += [e.astype(BF16) for e in es]
        inv = jnp.where((lane >= MLA_V * h) & (lane < MLA_V * (h + 1)), 1.0 / den, inv)
    return jnp.dot(jnp.concatenate(ps, axis=1), vst, preferred_element_type=F32) * inv


def _mla_ctx_kernel(q_ref, k_ref, v_ref, o_ref):
    o_ref[...] = _mla_attend(q_ref[...], [k_ref[...]], _mla_stack_values(v_ref[...])).astype(o_ref.dtype)


def _mla_ctx(q, k, v):
    B, L, _ = q.shape
    tok = lambda c: pl.BlockSpec((None, L, c), lambda b: (b, 0, 0))
    return pl.pallas_call(
        _mla_ctx_kernel,
        grid=(B,),
        in_specs=[tok(512), tok(512), tok(256)],
        out_specs=tok(256),
        out_shape=jax.ShapeDtypeStruct((B, L, 256), BF16),
        compiler_params=_params(("parallel",)),
        name="mla_ctx",
    )(q, k, v)


def _mla_lat_kernel(q_ref, kc_ref, vc_ref, k_ref, v_ref, o_ref):
    vst = _mla_stack_values(jnp.concatenate([vc_ref[...], v_ref[...]], axis=0))
    o = _mla_attend(q_ref[...], [kc_ref[...], k_ref[...]], vst)
    o_ref[...] = o.astype(o_ref.dtype)


def _mla_lat(q, k, v, kc, vc):
    B, L, _ = q.shape
    P = kc.shape[1]
    TQ = MLA_Q_TILE
    blk = lambda c: pl.BlockSpec((None, TQ, c), lambda b, i: (b, i, 0))
    whole = lambda n, c: pl.BlockSpec((None, n, c), lambda b, i: (b, 0, 0))
    return pl.pallas_call(
        _mla_lat_kernel,
        grid=(B, L // TQ),
        in_specs=[blk(512), whole(P, 512), whole(P, 256), whole(L, 512), whole(L, 256)],
        out_specs=blk(256),
        out_shape=jax.ShapeDtypeStruct((B, L, 256), BF16),
        compiler_params=_params(("parallel", "parallel")),
        name="mla_lat",
    )(q, kc, vc, k, v)


def _join_halves(ref):
    return jnp.concatenate([ref[0], ref[1]], axis=1)


def _split_halves(ref, val):
    ref[0] = val[:, :LANES]
    ref[1] = val[:, LANES:]


def _s5_kernel(uf_ref, ub_ref, bf_ref, bb_ref, a_ref, h0_ref, cf_ref, cb_ref,
               yf_ref, yb_ref, fin_ref, s_ref, st_ref):
    i = pl.program_id(1)
    steps = SCAN_STEPS

    @pl.when(i == 0)
    def _():
        st_ref[...] = h0_ref[...]

    uf = _join_halves(uf_ref).astype(BF16)
    ub = _join_halves(ub_ref).astype(BF16)
    s_ref[0] = jnp.dot(uf, bf_ref[:, :S5_WIDTH], preferred_element_type=F32)
    s_ref[1] = jnp.dot(uf, bf_ref[:, S5_WIDTH:], preferred_element_type=F32)
    s_ref[2] = jnp.dot(ub, bb_ref[:, :S5_WIDTH], preferred_element_type=F32)
    s_ref[3] = jnp.dot(ub, bb_ref[:, S5_WIDTH:], preferred_element_type=F32)

    lane_chunk = 2 * LANES
    for c in range(S5_WIDTH // lane_chunk):
        sl = slice(lane_chunk * c, lane_chunk * (c + 1))
        arf, aif, arb, aib = (a_ref[n, :, sl] for n in range(4))

        def body(j, carry):
            hrf, hif, hrb, hib = carry
            rf = pl.multiple_of(j * SUBLANES, SUBLANES)
            rb = pl.multiple_of((steps - 1 - j) * SUBLANES, SUBLANES)
            nrf = arf * hrf - aif * hif + s_ref[0, pl.ds(rf, SUBLANES), sl]
            nif = arf * hif + aif * hrf + s_ref[1, pl.ds(rf, SUBLANES), sl]
            nrb = arb * hrb - aib * hib + s_ref[2, pl.ds(rb, SUBLANES), sl]
            nib = arb * hib + aib * hrb + s_ref[3, pl.ds(rb, SUBLANES), sl]
            s_ref[0, pl.ds(rf, SUBLANES), sl] = nrf
            s_ref[1, pl.ds(rf, SUBLANES), sl] = nif
            s_ref[2, pl.ds(rb, SUBLANES), sl] = nrb
            s_ref[3, pl.ds(rb, SUBLANES), sl] = nib
            return nrf, nif, nrb, nib

        fin = lax.fori_loop(0, steps, body, tuple(st_ref[n, :, sl] for n in range(4)), unroll=8)
        for n in range(4):
            st_ref[n, :, sl] = fin[n]

    hf = jnp.concatenate([s_ref[0].astype(BF16), s_ref[1].astype(BF16)], axis=1)
    _split_halves(yf_ref, jnp.dot(hf, cf_ref[...], preferred_element_type=F32))
    hb = jnp.concatenate([s_ref[2].astype(BF16), s_ref[3].astype(BF16)], axis=1)
    _split_halves(yb_ref, jnp.dot(hb, cb_ref[...], preferred_element_type=F32))
    fin_ref[...] = st_ref[...]


def _s5_scan(u_rows, lw, h0):
    wspec = _layer_spec(lw["layer"])
    Bg, _, rows, _ = u_rows.shape
    R = SCAN_STEPS * SUBLANES
    n = rows // R
    fwd = pl.BlockSpec((None, 2, R, LANES), lambda g, i: (g, 0, i, 0))
    bwd = pl.BlockSpec((None, 2, R, LANES), lambda g, i: (g, 0, n - 1 - i, 0))
    st = pl.BlockSpec((None, 4, SUBLANES, S5_WIDTH), lambda g, i: (g, 0, 0, 0))
    return pl.pallas_call(
        _s5_kernel,
        grid=(Bg, n),
        in_specs=[fwd, bwd, wspec((256, 2 * S5_WIDTH)), wspec((256, 2 * S5_WIDTH)),
                  wspec((4, SUBLANES, S5_WIDTH)), st, wspec((2 * S5_WIDTH, 256)), wspec((2 * S5_WIDTH, 256))],
        out_specs=[fwd, bwd, st],
        out_shape=[jax.ShapeDtypeStruct(u_rows.shape, F32), jax.ShapeDtypeStruct(u_rows.shape, F32),
                   jax.ShapeDtypeStruct((Bg, 4, SUBLANES, S5_WIDTH), F32)],
        scratch_shapes=[pltpu.VMEM((4, R, S5_WIDTH), F32), pltpu.VMEM((4, SUBLANES, S5_WIDTH), F32)],
        compiler_params=_params(("parallel", "arbitrary")),
        name="s5_scan",
    )(u_rows, u_rows, lw["s5_bf"], lw["s5_bb"], lw["s5_a"], h0, lw["s5_cf"], lw["s5_cb"])


def _lru_gates(x_ref, pre_ref, post_ref, has_pre, has_post, cw_ref, cb_ref, w_ref, b_ref, sp_ref, a_ref, h_ref):
    R = x_ref.shape[1]
    pre = jnp.where(has_pre, _join_halves(pre_ref), 0.0)
    post = jnp.where(has_post, _join_halves(post_ref), 0.0)
    xp = jnp.concatenate([pre, _join_halves(x_ref), post], axis=0)
    xc = cb_ref[...]
    for t in range(LRU_CONV):
        xc = xc + xp[SUBLANES * t:SUBLANES * t + R] * cw_ref[t:t + 1, :]
    g = jnp.dot(xc.astype(BF16), w_ref[...], preferred_element_type=F32) + b_ref[...]
    r = jax.nn.sigmoid(g[:, :LRU_WIDTH])
    ig = jax.nn.sigmoid(g[:, LRU_WIDTH:])
    log_a = (-LRU_C) * r * sp_ref[...]
    a = jnp.exp(log_a)
    a_ref[...] = a
    h_ref[...] = jnp.sqrt(1.0 - a * a) * (ig * xc)


def _lru_kernel(xf_ref, xfp_ref, xfn_ref, xb_ref, xbp_ref, xbn_ref, cw_ref, cb_ref, wf_ref, wb_ref,
                bf_ref, bb_ref, spf_ref, spb_ref, h0_ref, of_ref, ob_ref, fin_ref,
                af_ref, ab_ref, hf_ref, hb_ref, st_ref):
    i = pl.program_id(1)
    n = pl.num_programs(1)
    steps = SCAN_STEPS

    @pl.when(i == 0)
    def _():
        st_ref[...] = h0_ref[...]

    _lru_gates(xf_ref, xfp_ref, xfn_ref, i > 0, i < n - 1, cw_ref, cb_ref, wf_ref, bf_ref, spf_ref, af_ref, hf_ref)
    _lru_gates(xb_ref, xbp_ref, xbn_ref, i < n - 1, i > 0, cw_ref, cb_ref, wb_ref, bb_ref, spb_ref, ab_ref, hb_ref)

    def body(j, carry):
        hf, hb = carry
        rf = pl.multiple_of(j * SUBLANES, SUBLANES)
        rb = pl.multiple_of((steps - 1 - j) * SUBLANES, SUBLANES)
        nf = af_ref[pl.ds(rf, SUBLANES), :] * hf + hf_ref[pl.ds(rf, SUBLANES), :]
        nb = ab_ref[pl.ds(rb, SUBLANES), :] * hb + hb_ref[pl.ds(rb, SUBLANES), :]
        hf_ref[pl.ds(rf, SUBLANES), :] = nf
        hb_ref[pl.ds(rb, SUBLANES), :] = nb
        return nf, nb

    ff, fb = lax.fori_loop(0, steps, body, (st_ref[0], st_ref[1]), unroll=8)
    st_ref[0] = ff
    st_ref[1] = fb
    fin_ref[...] = st_ref[...]
    _split_halves(of_ref, hf_ref[...])
    _split_halves(ob_ref, hb_ref[...])


def _lru_scan(x_rows, lw, h0):
    wspec = _layer_spec(lw["layer"])
    Bg, _, rows, _ = x_rows.shape
    R = SCAN_STEPS * SUBLANES
    n = rows // R
    pre_rows = 2 * SUBLANES
    fwd = lambda g, i: i
    bwd = lambda g, i: n - 1 - i
    blk = lambda m: pl.BlockSpec((None, 2, R, LANES), lambda g, i: (g, 0, m(g, i), 0))
    pre = lambda m: pl.BlockSpec(
        (None, 2, pre_rows, LANES), lambda g, i: (g, 0, jnp.maximum(m(g, i) * (R // pre_rows) - 1, 0), 0))
    post = lambda m: pl.BlockSpec(
        (None, 2, SUBLANES, LANES),
        lambda g, i: (g, 0, jnp.minimum((m(g, i) + 1) * (R // SUBLANES), rows // SUBLANES - 1), 0))
    st = pl.BlockSpec((None, 2, SUBLANES, 256), lambda g, i: (g, 0, 0, 0))
    return pl.pallas_call(
        _lru_kernel,
        grid=(Bg, n),
        in_specs=[blk(fwd), pre(fwd), post(fwd), blk(bwd), pre(bwd), post(bwd),
                  wspec((LRU_CONV, 256)), wspec((1, 256)), wspec((256, 512)), wspec((256, 512)),
                  wspec((1, 512)), wspec((1, 512)), wspec((1, 256)), wspec((1, 256)), st],
        out_specs=[blk(fwd), blk(bwd), st],
        out_shape=[jax.ShapeDtypeStruct(x_rows.shape, F32), jax.ShapeDtypeStruct(x_rows.shape, F32),
                   jax.ShapeDtypeStruct((Bg, 2, SUBLANES, 256), F32)],
        scratch_shapes=[pltpu.VMEM((R, 256), F32), pltpu.VMEM((R, 256), F32), pltpu.VMEM((R, 256), F32),
                        pltpu.VMEM((R, 256), F32), pltpu.VMEM((2, SUBLANES, 256), F32)],
        compiler_params=_params(("parallel", "arbitrary")),
        name="lru_scan",
    )(x_rows, x_rows, x_rows, x_rows, x_rows, x_rows, lw["lru_cw"], lw["lru_cb"], lw["lru_wf"], lw["lru_wb"],
      lw["lru_bf"], lw["lru_bb"], lw["lru_spf"], lw["lru_spb"], h0)


def _merge_kernel(x_ref, sc_ref, sh_ref, gt_ref, g1_ref, wg_ref, oa_ref, u_ref, yf_ref, yb_ref, d_ref,
                  wglu_ref, hf_ref, hb_ref, lg_ref, od_ref, wb_ref, wo_ref, o_ref):
    steps = x_ref.shape[1]
    x = _group_rows(x_ref)
    h = _modnorm(x, g1_ref[...], _mod_rows(sc_ref, steps), _mod_rows(sh_ref, steps)).astype(BF16)

    yb5 = jax.nn.gelu(d_ref[...] * _load_time_major(u_ref) + _load_time_major(yf_ref) + _load_time_major(yb_ref))
    gv = jnp.dot(yb5.astype(BF16), wglu_ref[...], preferred_element_type=F32)
    o_b = gv[:, :BRANCH_W] * jax.nn.sigmoid(gv[:, BRANCH_W:])
    o_c = (_load_time_major(hf_ref) + _load_time_major(hb_ref)) * jax.nn.gelu(_group_rows(lg_ref))
    branches = (_group_rows(oa_ref), o_b.astype(BF16), o_c.astype(BF16), _group_rows(od_ref))

    acc = jnp.zeros(x.shape, F32)
    for n in range(N_BRANCH):
        gate = jnp.dot(h, wg_ref[:, D_MODEL * n:D_MODEL * (n + 1)], preferred_element_type=F32)
        proj = jnp.dot(branches[n], wb_ref[n], preferred_element_type=F32)
        acc = acc + jax.nn.sigmoid(gate) * proj
    out = jnp.dot(acc.astype(BF16), wo_ref[...], preferred_element_type=F32)
    _store_group(o_ref, x + _mod_rows(gt_ref, steps) * out)


def _merge(x, sc, sh, gt, lw, o_a, s5u_t, yf_t, yb_t, hf_t, hb_t, lg, o_d):
    wspec = _layer_spec(lw["layer"])
    B, L, _ = x.shape
    tok, scan, mod = _group_specs(sc.shape[0] > 1)
    return pl.pallas_call(
        _merge_kernel,
        grid=(B // SUBLANES, L // TOK_STEPS),
        in_specs=[tok(D_MODEL), mod, mod, mod, wspec((1, D_MODEL)), wspec((D_MODEL, N_BRANCH * D_MODEL)),
                  tok(256), scan, scan, scan, wspec((1, 256)), wspec((256, 512)),
                  scan, scan, tok(256), tok(256), wspec((N_BRANCH, BRANCH_W, D_MODEL)),
                  wspec((D_MODEL, D_MODEL))],
        out_specs=tok(D_MODEL),
        out_shape=jax.ShapeDtypeStruct((B, L, D_MODEL), F32),
        compiler_params=_params(("parallel", "parallel")),
        name="merge",
    )(x, sc, sh, gt, lw["g1"], lw["wg"], o_a, s5u_t, yf_t, yb_t, lw["s5_d"], lw["wglu"],
      hf_t, hb_t, lg, o_d, lw["wb"], lw["wo"])


def _first_index(values, target):
    idx = jnp.full_like(target, float(len(values) - 1))
    for n in range(len(values) - 2, -1, -1):
        idx = jnp.where(values[n] == target, float(n), idx)
    return idx


def _list_max(values):
    return functools.reduce(jnp.maximum, values)


def _moe_sort_kernel(x_ref, sc_ref, sh_ref, g2_ref, wrh_ref, wrl_ref, br_ref, tri_ref,
                     hs_ref, gs_ref, dcol_ref, meta_ref):
    for s in range(x_ref.shape[0] // MOE_SORT_TILE):
        tok = pl.ds(s * MOE_SORT_TILE, MOE_SORT_TILE)
        srt = pl.ds(s * MOE_SORTED_ROWS, MOE_SORTED_ROWS)
        _moe_sort_tile(x_ref.at[tok], sc_ref, sh_ref, g2_ref, wrh_ref, wrl_ref, br_ref, tri_ref,
                       hs_ref.at[srt], gs_ref.at[srt], dcol_ref.at[tok], meta_ref.at[s])


def _moe_sort_tile(x_ref, sc_ref, sh_ref, g2_ref, wrh_ref, wrl_ref, br_ref, tri_ref,
                   hs_ref, gs_ref, dcol_ref, meta_ref):
    T = MOE_SORT_TILE
    h = _modnorm(x_ref[...], g2_ref[...], sc_ref[...], sh_ref[...])
    hh = h.astype(BF16)
    hl = (h - hh.astype(F32)).astype(BF16)
    nt = (((1,), (1,)), ((), ()))
    logits = (jnp.dot(hh, wrh_ref[...], preferred_element_type=F32)
              + jnp.dot(hl, wrh_ref[...], preferred_element_type=F32)
              + jnp.dot(hh, wrl_ref[...], preferred_element_type=F32)) + br_ref[...]
    lt = logits.T
    gl = [lt[g:g + 1, :] for g in range(N_GROUPS)]
    gmax = _list_max(gl)
    g_idx = _first_index(gl, gmax)
    pg = 1.0 / sum(jnp.exp(v - gmax) for v in gl)
    hot = [g_idx == float(g) for g in range(N_GROUPS)]
    el = []
    for e in range(EXPERTS_PER_GROUP):
        v = jnp.zeros_like(gmax)
        for g in range(N_GROUPS):
            r = N_GROUPS + EXPERTS_PER_GROUP * g + e
            v = jnp.where(hot[g], lt[r:r + 1, :], v)
        el.append(v)
    emax = _list_max(el)
    ee = [jnp.exp(v - emax) for v in el]
    esum = sum(ee)
    pe = [v / esum for v in ee]
    v1 = _list_max(pe)
    i1 = _first_index(pe, v1)
    pe2 = [jnp.where(i1 == float(e), -1.0, pe[e]) for e in range(EXPERTS_PER_GROUP)]
    v2 = _list_max(pe2)
    i2 = _first_index(pe2, v2)
    tot = v1 + v2
    w = [jnp.where(i1 == float(e), pg * v1 / tot, jnp.where(i2 == float(e), pg * v2 / tot, 0.0))
         for e in range(EXPERTS_PER_GROUP)]

    zero_row = jnp.zeros_like(gmax)
    g8 = jnp.concatenate([jnp.where(hot[g], 1.0, 0.0) for g in range(N_GROUPS)] + [zero_row] * 4, axis=0)
    cum = jnp.dot(g8.astype(BF16), tri_ref[...], preferred_element_type=F32)
    off = jnp.zeros((1, 1), F32)
    dest = zero_row
    counts = []
    for g in range(N_GROUPS):
        cnt = cum[g:g + 1, T - 1:T]
        padded = jnp.floor((cnt + (MOE_ROW_ALIGN - 1.0)) * (1.0 / MOE_ROW_ALIGN)) * MOE_ROW_ALIGN
        dest = jnp.where(hot[g], off + cum[g:g + 1, :] - 1.0, dest)
        off = off + padded
        counts.append(padded)
    rows = lax.broadcasted_iota(jnp.int32, (MOE_SORTED_ROWS, T), 0).astype(F32)
    perm = jnp.where(rows == dest, 1.0, 0.0).astype(BF16)
    hs_ref[...] = jnp.dot(perm, hh, preferred_element_type=F32).astype(BF16)
    gates = jnp.concatenate(w + [jnp.zeros((LANES - EXPERTS_PER_GROUP, T), F32)], axis=0)
    ghi = gates.astype(BF16)
    glo = (gates - ghi.astype(F32)).astype(BF16)
    gs_ref[...] = (lax.dot_general(perm, ghi, nt, preferred_element_type=F32)
                   + lax.dot_general(perm, glo, nt, preferred_element_type=F32))
    r_i = lax.broadcasted_iota(jnp.int32, (T, T), 0)
    c_i = lax.broadcasted_iota(jnp.int32, (T, T), 1)
    dcol = jnp.sum(jnp.where(r_i == c_i, jnp.broadcast_to(dest, (T, T)), 0.0), axis=1, keepdims=True)
    dcol_ref[...] = jnp.broadcast_to(dcol, (T, LANES))
    meta_ref[...] = jnp.concatenate([jnp.broadcast_to(c, (1, LANES)) for c in counts]
                                    + [jnp.zeros((SUBLANES - N_GROUPS, LANES), F32)], axis=0)


def _moe_sort(xf, sc, sh, lw, tiles_per_mod):
    wspec = _layer_spec(lw["layer"])
    N = xf.shape[0]
    T = MOE_SORT_TILE
    n_tiles = N // T
    sub = MOE_SORT_SUBTILES
    per_b = sc.shape[0] > 1
    mod = pl.BlockSpec((None, 1, D_MODEL), lambda t: (t * sub // tiles_per_mod if per_b else 0, 0, 0))
    sds = jax.ShapeDtypeStruct
    return pl.pallas_call(
        _moe_sort_kernel,
        grid=(n_tiles // sub,),
        in_specs=[pl.BlockSpec((sub * T, D_MODEL), lambda t: (t, 0)), mod, mod, wspec((1, D_MODEL)),
                  wspec((D_MODEL, LANES)), wspec((D_MODEL, LANES)), wspec((1, LANES)), wspec((T, T))],
        out_specs=[pl.BlockSpec((sub * MOE_SORTED_ROWS, D_MODEL), lambda t: (t, 0)),
                   pl.BlockSpec((sub * MOE_SORTED_ROWS, LANES), lambda t: (t, 0)),
                   pl.BlockSpec((sub * T, LANES), lambda t: (t, 0)),
                   pl.BlockSpec((sub, SUBLANES, LANES), lambda t: (t, 0, 0))],
        out_shape=[sds((n_tiles * MOE_SORTED_ROWS, D_MODEL), BF16), sds((n_tiles * MOE_SORTED_ROWS, LANES), F32),
                   sds((N, LANES), F32), sds((n_tiles, SUBLANES, LANES), F32)],
        compiler_params=_params(("parallel",)),
        name="moe_sort",
    )(xf, sc, sh, lw["g2"], lw["wr_hi"], lw["wr_lo"], lw["br"], lw["tri"])


def _moe_expert_kernel(cnt_ref, hs_ref, gs_ref, wga_ref, wup_ref, wdn_ref, ys_ref, ch_ref, cg_ref, cy_ref):
    j = pl.program_id(0)
    g = pl.program_id(1)
    piece = MOE_ROW_ALIGN

    @pl.when((j == 0) & (g == 0))
    def _():
        ch_ref[...] = jnp.zeros_like(ch_ref)
        cg_ref[...] = jnp.zeros_like(cg_ref)

    @pl.when(g == 0)
    def _():
        ys_ref[...] = jnp.zeros_like(ys_ref)

    def for_each_piece(move):
        packed = jnp.int32(0)
        for t in range(MOE_BLOCK_TILES):
            base = (j * MOE_BLOCK_TILES + t) * N_GROUPS
            start = jnp.int32(t * MOE_SORTED_ROWS)
            for g2 in range(N_GROUPS - 1):
                start = start + jnp.where(g2 < g, cnt_ref[base + g2], 0)
            n = cnt_ref[base + g]

            def body(k, carry, start=start, packed=packed):
                move(pl.multiple_of(start + k * piece, piece), pl.multiple_of(packed + k * piece, piece))
                return carry

            lax.fori_loop(0, lax.shift_right_logical(n, 4), body, 0)
            packed = packed + n
        return packed

    def pack(src, dst):
        ch_ref[pl.ds(dst, piece), :] = hs_ref[pl.ds(src, piece), :]
        cg_ref[pl.ds(dst, piece), :] = gs_ref[pl.ds(src, piece), :]

    rows = for_each_piece(pack)

    def experts(r, size):
        h = ch_ref[pl.ds(r, size), :]
        acts = []
        for e in range(EXPERTS_PER_GROUP):
            a = jnp.dot(h, wga_ref[e], preferred_element_type=F32)
            u = jnp.dot(h, wup_ref[e], preferred_element_type=F32)
            acts.append(((a * jax.nn.sigmoid(a)) * u * cg_ref[pl.ds(r, size), e:e + 1]).astype(BF16))
        y = jnp.dot(jnp.concatenate(acts, axis=1), wdn_ref[...], preferred_element_type=F32)
        cy_ref[pl.ds(r, size), :] = y.astype(BF16)

    big, mid, small = MOE_CHUNK, MOE_CHUNK // 2, MOE_CHUNK // 4
    n_big = lax.shift_right_logical(rows, big.bit_length() - 1)

    def chunk(c, carry):
        experts(pl.multiple_of(c * big, big), big)
        return carry

    lax.fori_loop(0, n_big, chunk, 0)
    tail = pl.multiple_of(n_big * big, big)
    rem = rows - tail
    pl.when(rem > mid + small)(lambda: experts(tail, big))
    pl.when((rem > small) & (rem <= mid + small))(lambda: experts(tail, mid))
    pl.when((rem > mid) & (rem <= mid + small))(lambda: experts(pl.multiple_of(tail + mid, small), small))
    pl.when((rem > 0) & (rem <= small))(lambda: experts(tail, small))

    def unpack(src, dst):
        ys_ref[pl.ds(src, piece), :] = cy_ref[pl.ds(dst, piece), :]

    for_each_piece(unpack)


def _moe_experts(cnt, hs, gs, lw):
    rows = MOE_BLOCK_TILES * MOE_SORTED_ROWS
    cap = MOE_BLOCK_TILES * MOE_SORT_TILE
    blk = lambda c: pl.BlockSpec((rows, c), lambda j, g, cnt: (j, 0))
    layer = lw["layer"]
    wspec = lambda r, c: pl.BlockSpec((None, EXPERTS_PER_GROUP, r, c), lambda j, g, cnt: (layer, g, 0, 0))
    return pl.pallas_call(
        _moe_expert_kernel,
        grid_spec=pltpu.PrefetchScalarGridSpec(
            num_scalar_prefetch=1, grid=(hs.shape[0] // rows, N_GROUPS),
            in_specs=[blk(D_MODEL), blk(LANES), wspec(D_MODEL, EXPERT_FF), wspec(D_MODEL, EXPERT_FF),
                      pl.BlockSpec((None, EXPERTS_PER_GROUP * EXPERT_FF, D_MODEL),
                                   lambda j, g, cnt: (layer, g, 0))],
            out_specs=blk(D_MODEL),
            scratch_shapes=[pltpu.VMEM((cap, D_MODEL), BF16), pltpu.VMEM((cap, LANES), F32),
                            pltpu.VMEM((cap, D_MODEL), BF16)]),
        out_shape=jax.ShapeDtypeStruct(hs.shape, BF16),
        compiler_params=_params(("arbitrary", "arbitrary")),
        name="moe_experts",
    )(cnt, hs, gs, lw["wga"], lw["wup"], lw["wdn"])


def _moe_unsort_kernel(x_ref, gt_ref, ys_ref, dcol_ref, o_ref):
    cols = lax.broadcasted_iota(jnp.int32, (MOE_SORT_TILE, MOE_SORTED_ROWS), 1).astype(F32)
    perm_t = jnp.where(cols == dcol_ref[:, 0:1], 1.0, 0.0).astype(BF16)
    y = jnp.dot(perm_t, ys_ref[...], preferred_element_type=F32)
    o_ref[...] = x_ref[...] + gt_ref[...] * y


def _moe_unsort(xf, gt, ys, dcol, tiles_per_mod):
    N = xf.shape[0]
    T = MOE_SORT_TILE
    per_b = gt.shape[0] > 1
    mod = pl.BlockSpec((None, 1, D_MODEL), lambda t: (t // tiles_per_mod if per_b else 0, 0, 0))
    tok = pl.BlockSpec((T, D_MODEL), lambda t: (t, 0))
    return pl.pallas_call(
        _moe_unsort_kernel,
        grid=(N // T,),
        in_specs=[tok, mod, pl.BlockSpec((MOE_SORTED_ROWS, D_MODEL), lambda t: (t, 0)),
                  pl.BlockSpec((T, LANES), lambda t: (t, 0))],
        out_specs=tok,
        out_shape=jax.ShapeDtypeStruct((N, D_MODEL), F32),
        compiler_params=_params(("parallel",)),
        name="moe_unsort",
    )(xf, gt, ys, dcol)


def _moe(x, sc, sh, gt, lw):
    B, L, _ = x.shape
    N = B * L
    T = MOE_SORT_TILE
    xf = x.reshape(N, D_MODEL)
    hs, gs, dcol, meta = _moe_sort(xf, sc, sh, lw, L // T)

    cnt = meta[:, :N_GROUPS, 0].astype(jnp.int32).reshape(-1)
    ys = _moe_experts(cnt, hs, gs, lw)
    out = _moe_unsort(xf, gt, ys, dcol, L // T)
    return out.reshape(B, L, D_MODEL)


def _block_diag(blocks):
    n, r, c = blocks.shape
    eye = jnp.eye(n, dtype=blocks.dtype)
    return jnp.einsum("nrc,nm->nrmc", blocks, eye).reshape(n * r, n * c)


def _rope_tables(seq_len, dim, width, offset):
    rows = seq_len // GRID_W
    r, col = np.meshgrid(np.arange(rows), np.arange(GRID_W), indexing="ij")
    r = r.reshape(-1).astype(np.float64)
    col = col.reshape(-1).astype(np.float64)
    quarter = dim // 4
    freqs = ROPE_BASE ** (-np.arange(quarter, dtype=np.float64) / quarter)
    ang_r = r[:, None] * freqs
    ang_c = col[:, None] * freqs
    zero = np.zeros_like(ang_r)
    cos = np.cos(np.concatenate([ang_r, ang_r, ang_c, ang_c], axis=-1))
    sin_next = np.concatenate([-np.sin(ang_r), zero, -np.sin(ang_c), zero], axis=-1)
    sin_prev = np.concatenate([zero, np.sin(ang_r), zero, np.sin(ang_c)], axis=-1)

    def place(t, fill):
        return np.pad(t, ((0, 0), (offset, width - offset - dim)), constant_values=fill).astype(np.float32)

    return place(cos, 1.0), place(sin_next, 0.0), place(sin_prev, 0.0)


def _layer_weights(l, w):
    lw = {}
    w_in = w["w_in"][l]
    q_cols = w_in[:, 0:256].reshape(D_MODEL, A_HEADS, A_HEAD_DIM)[:, HEAD_ORDER].reshape(D_MODEL, 256)
    kr_cols = jnp.pad(w_in[:, 1664:1696], ((0, 0), (MLA_NOPE, LANES - MLA_QK)))
    lw["wz"] = jnp.concatenate([q_cols, w_in[:, 256:1664], kr_cols], axis=1).astype(BF16)
    lw["wg"] = w_in[:, 1696:].astype(BF16)
    lw["g1"] = w["norm1_g"][l][None]
    lw["g2"] = w["norm2_g"][l][None]
    lw["gq"] = jnp.tile(w["a_qnorm_g"][l], A_HEADS)[None] * (A_HEAD_DIM ** -0.5 * LOG2_E)
    lw["gk"] = jnp.tile(w["a_knorm_g"][l], A_KV_HEADS)[None]
    lw["sink"] = w["a_sink"][l]
    lw["qlg"] = w["mla_q_lat_norm"][l][None]
    wuq = w["mla_w_uq"][l].reshape(MLA_Q_LORA, MLA_HEADS, MLA_QK)
    lw["wuq"] = jnp.pad(wuq, ((0, 0), (0, 0), (0, LANES - MLA_QK))).reshape(MLA_Q_LORA, MLA_HEADS * LANES).astype(BF16)
    pad_g = lambda g: jnp.tile(jnp.pad(g, (0, LANES - MLA_QK)), MLA_HEADS)[None]
    lw["gmq"] = pad_g(w["mla_qnorm_g"][l]) * (MLA_QK ** -0.5 * LOG2_E)
    lw["gmk"] = pad_g(w["mla_knorm_g"][l])
    lw["kvg"] = w["mla_kv_norm"][l][None]
    wukv = w["mla_w_ukv"][l].reshape(MLA_KV_LORA, MLA_HEADS, MLA_NOPE + MLA_V)
    wk = jnp.pad(wukv[:, :, :MLA_NOPE], ((0, 0), (0, 0), (0, LANES - MLA_NOPE))).reshape(MLA_KV_LORA, MLA_HEADS * LANES)
    wv = wukv[:, :, MLA_NOPE:].reshape(MLA_KV_LORA, MLA_HEADS * MLA_V)
    lw["wukv"] = jnp.concatenate([wk, wv], axis=1).astype(BF16)

    lre = w["s5_lambda_re"][l]
    lim = w["s5_lambda_im"][l]
    dt = jnp.exp(w["s5_log_step"][l])[:, :, None]
    mag = jnp.exp(lre * dt)
    ar, ai = mag * jnp.cos(lim * dt), mag * jnp.sin(lim * dt)
    den = lre * lre + lim * lim
    fr = ((ar - 1.0) * lre + ai * lim) / den
    fi = (ai * lre - (ar - 1.0) * lim) / den
    br, bi = w["s5_b_re"][l], w["s5_b_im"][l]
    bbr = fr[..., None] * br - fi[..., None] * bi
    bbi = fr[..., None] * bi + fi[..., None] * br
    in_map = lambda d: jnp.concatenate(
        [_block_diag(jnp.swapaxes(bbr[d], 1, 2)), _block_diag(jnp.swapaxes(bbi[d], 1, 2))], axis=1).astype(BF16)
    out_map = lambda d: jnp.concatenate(
        [_block_diag(jnp.swapaxes(w["s5_c_re"][l][d], 1, 2)),
         -_block_diag(jnp.swapaxes(w["s5_c_im"][l][d], 1, 2))], axis=0).astype(BF16)
    lw["s5_bf"], lw["s5_bb"] = in_map(0), in_map(1)
    lw["s5_cf"], lw["s5_cb"] = out_map(0), out_map(1)
    coef = jnp.stack([ar[0].reshape(-1), ai[0].reshape(-1), ar[1].reshape(-1), ai[1].reshape(-1)])
    lw["s5_a"] = jnp.broadcast_to(coef[:, None, :], (4, SUBLANES, S5_WIDTH))
    lw["s5_d"] = w["s5_d"][l][None]
    lw["wglu"] = w["s5_w_glu"][l].astype(BF16)

    lw["lru_cw"] = w["lru_conv_w"][l]
    lw["lru_cb"] = w["lru_conv_b"][l][None]
    gate_w = lambda d: jnp.concatenate(
        [_block_diag(w["lru_w_a"][l][d]), _block_diag(w["lru_w_x"][l][d])], axis=1).astype(BF16)
    gate_b = lambda d: jnp.concatenate([w["lru_b_a"][l][d], w["lru_b_x"][l][d]])[None]
    lw["lru_wf"], lw["lru_wb"] = gate_w(0), gate_w(1)
    lw["lru_bf"], lw["lru_bb"] = gate_b(0), gate_b(1)
    sp = jax.nn.softplus(-w["lru_lambda"][l])
    lw["lru_spf"], lw["lru_spb"] = sp[0][None], sp[1][None]

    wb = w["w_branch"][l]
    wb0 = wb[0].reshape(A_HEADS, A_HEAD_DIM, D_MODEL)[HEAD_ORDER].reshape(BRANCH_W, D_MODEL)
    lw["wb"] = jnp.concatenate([wb0[None], wb[1:]], axis=0).astype(BF16)
    lw["wo"] = w["w_out"][l].astype(BF16)

    wr = jnp.pad(jnp.concatenate([w["moe_w_group"][l], w["moe_w_expert"][l]], axis=1),
                 ((0, 0), (0, LANES - N_GROUPS - N_EXPERTS)))
    lw["wr_hi"] = wr.astype(BF16)
    lw["wr_lo"] = (wr - lw["wr_hi"].astype(F32)).astype(BF16)
    lw["br"] = jnp.pad(jnp.concatenate([w["moe_b_group"][l], w["moe_b_expert"][l]]),
                       (0, LANES - N_GROUPS - N_EXPERTS))[None]
    lw["tri"] = jnp.triu(jnp.ones((MOE_SORT_TILE, MOE_SORT_TILE), BF16))
    lw["wga"] = w["moe_w_gate"][l].astype(BF16)
    lw["wup"] = w["moe_w_up"][l].astype(BF16)
    lw["wdn"] = w["moe_w_down"][l].astype(BF16).reshape(N_EXPERTS * EXPERT_FF, D_MODEL)
    return lw


def _rows_to_state(fin, width):
    Bg, n = fin.shape[0], fin.shape[1]
    return jnp.swapaxes(fin, 1, 2).reshape(Bg * SUBLANES, n, width)


def _state_to_rows(state):
    B, n, width = state.shape
    return jnp.swapaxes(state.reshape(B // SUBLANES, SUBLANES, n, width), 1, 2)


def _mix(x, mods, lw, rope_tabs, cache, s5_h0, lru_h0, ctx):
    sh1, sc1, gt1 = mods
    outs = _in_proj(x, sc1, sh1, lw, rope_tabs, ctx)
    q, k, v, s5u_t, lx_t, lg, qm, km, vm = outs[:9]
    if ctx:
        o_a = _attn_a_ctx(q, k, v, lw["sink"][lw["layer"]])
        o_d = _mla_ctx(qm, km, vm)
    else:
        ck, cv, cckv, ckr = cache
        o_a = _attn_a_lat(q, k, v, ck, cv, lw["sink"][lw["layer"]])
        kc, vc = _mla_cache_kv(cckv, ckr, lw)
        o_d = _mla_lat(qm, km, vm, kc, vc)
    yf, yb, s5_fin = _s5_scan(s5u_t, lw, s5_h0)
    hf, hb, lru_fin = _lru_scan(lx_t, lw, lru_h0)
    x = _merge(x, sc1, sh1, gt1, lw, o_a, s5u_t, yf, yb, hf, hb, lg, o_d)
    return x, outs[9:], s5_fin, lru_fin


def kernel(x_prompt, x_sample, cache_attn_k, cache_attn_v, cache_mla_ckv, cache_mla_krope, state_ssm_re, state_ssm_im, state_lru, c, c_ctx, w_mod, b_mod, norm1_g, norm2_g, w_in, a_qnorm_g, a_knorm_g, a_sink, s5_lambda_re, s5_lambda_im, s5_log_step, s5_b_re, s5_b_im, s5_c_re, s5_c_im, s5_d, s5_w_glu, lru_conv_w, lru_conv_b, lru_w_a, lru_b_a, lru_w_x, lru_b_x, lru_lambda, mla_q_lat_norm, mla_w_uq, mla_kv_norm, mla_w_ukv, mla_qnorm_g, mla_knorm_g, w_branch, w_out, moe_w_group, moe_b_group, moe_w_expert, moe_b_expert, moe_w_gate, moe_w_up, moe_w_down):
    w = dict(norm1_g=norm1_g, norm2_g=norm2_g, w_in=w_in, a_qnorm_g=a_qnorm_g, a_knorm_g=a_knorm_g,
             a_sink=a_sink, s5_lambda_re=s5_lambda_re, s5_lambda_im=s5_lambda_im, s5_log_step=s5_log_step,
             s5_b_re=s5_b_re, s5_b_im=s5_b_im, s5_c_re=s5_c_re, s5_c_im=s5_c_im, s5_d=s5_d, s5_w_glu=s5_w_glu,
             lru_conv_w=lru_conv_w, lru_conv_b=lru_conv_b, lru_w_a=lru_w_a, lru_b_a=lru_b_a, lru_w_x=lru_w_x,
             lru_b_x=lru_b_x, lru_lambda=lru_lambda, mla_q_lat_norm=mla_q_lat_norm, mla_w_uq=mla_w_uq,
             mla_kv_norm=mla_kv_norm, mla_w_ukv=mla_w_ukv, mla_qnorm_g=mla_qnorm_g, mla_knorm_g=mla_knorm_g,
             w_branch=w_branch, w_out=w_out, moe_w_group=moe_w_group, moe_b_group=moe_b_group,
             moe_w_expert=moe_w_expert, moe_b_expert=moe_b_expert, moe_w_gate=moe_w_gate, moe_w_up=moe_w_up,
             moe_w_down=moe_w_down)
    B, L, _ = x_prompt.shape
    Bd, Ld, _ = x_sample.shape
    P = cache_attn_k.shape[2]

    cond = jnp.zeros((2 * SUBLANES, D_MODEL), F32).at[:Bd].set(c).at[Bd].set(c_ctx)
    mod = _modulation(cond, w_mod, b_mod[:, None, :])
    mod = mod.reshape(DEPTH, 2 * SUBLANES, 6, D_MODEL)

    rope_a = _rope_tables(Ld, A_HEAD_DIM, A_HEAD_DIM, 0)
    rope_a = tuple(jnp.asarray(np.tile(t, (1, A_HEADS))) for t in rope_a)
    rope_m = tuple(jnp.asarray(t) for t in _rope_tables(Ld, MLA_ROPE, LANES, MLA_NOPE))

    lw_all = jax.vmap(lambda w1: _layer_weights(0, {name: v[None] for name, v in w1.items()}))(w)

    xp, xs = x_prompt, x_sample
    ak_l, av_l, ckv_l, kr_l, sr_l, si_l, lru_l = [], [], [], [], [], [], []
    for l in range(DEPTH):
        lw = dict(lw_all, layer=l)
        lat_mod = [mod[l, :Bd, n][:, None, :] for n in range(6)]
        ctx_mod = [mod[l, Bd:Bd + 1, n][:, None, :] for n in range(6)]

        zs5 = jnp.zeros((B // SUBLANES, 4, SUBLANES, S5_WIDTH), F32)
        zlru = jnp.zeros((B // SUBLANES, 2, SUBLANES, LRU_WIDTH), F32)
        xp, (k32, v32, ckv_n, krp), s5_fin, lru_fin = _mix(xp, ctx_mod[0:3], lw, None, None, zs5, zlru, True)
        xp = _moe(xp, ctx_mod[4], ctx_mod[3], ctx_mod[5], lw)
        ak_l.append(k32.reshape(B, L, A_KV_HEADS, A_HEAD_DIM))
        av_l.append(v32.reshape(B, L, A_KV_HEADS, A_HEAD_DIM))
        ckv_l.append(ckv_n)
        kr_l.append(krp[:, :, MLA_NOPE:MLA_QK])
        fin = _rows_to_state(s5_fin, S5_WIDTH)
        sr_l.append(fin[:, 0::2].reshape(B, 2, S5_GROUPS, S5_STATE))
        si_l.append(fin[:, 1::2].reshape(B, 2, S5_GROUPS, S5_STATE))
        lru_l.append(_rows_to_state(lru_fin, LRU_WIDTH))

        sre = state_ssm_re[:, l].reshape(Bd, 2, S5_WIDTH)
        sim = state_ssm_im[:, l].reshape(Bd, 2, S5_WIDTH)
        s5_h0 = _state_to_rows(jnp.stack([sre[:, 0], sim[:, 0], sre[:, 1], sim[:, 1]], axis=1))
        lru_h0 = _state_to_rows(state_lru[:, l])
        cache = (cache_attn_k[:, l].reshape(Bd, P, 128).astype(BF16),
                 cache_attn_v[:, l].reshape(Bd, P, 128).astype(BF16),
                 cache_mla_ckv[:, l],
                 jnp.pad(cache_mla_krope[:, l], ((0, 0), (0, 0), (MLA_NOPE, LANES - MLA_QK))))
        xs, _, _, _ = _mix(xs, lat_mod[0:3], lw, rope_a + rope_m, cache, s5_h0, lru_h0, False)
        xs = _moe(xs, lat_mod[4], lat_mod[3], lat_mod[5], lw)

    stack = lambda ts: jnp.stack(ts, axis=1)
    return (xp, xs, stack(ak_l), stack(av_l), stack(ckv_l), stack(kr_l), stack(sr_l), stack(si_l), stack(lru_l))
```

```python
import functools
import math

import jax
import jax.numpy as jnp
import numpy as np
from jax import lax
from jax.experimental import pallas as pl
from jax.experimental.pallas import tpu as pltpu

F32 = jnp.float32
BF16 = jnp.bfloat16

D_MODEL = 1024
DEPTH = 2
GRID_W = 64
N_BRANCH = 4
BRANCH_W = 256
ROPE_BASE = 10000.0
EPS = 1e-6
NEG_INF = -1e30
LOG2_E = math.log2(math.e)
A_HEADS = 4
A_KV_HEADS = 2
A_HEAD_DIM = 64
WINDOW = 128
Q_BLOCK = 128
S5_GROUP = 16
S5_GROUPS = 16
S5_STATE = 64
S5_WIDTH = S5_GROUPS * S5_STATE
LRU_WIDTH = 256
LRU_BLOCKS = 4
LRU_CONV = 4
LRU_C = 8.0
MLA_HEADS = 4
MLA_Q_LORA = 256
MLA_KV_LORA = 128
MLA_NOPE = 64
MLA_ROPE = 32
MLA_V = 64
MLA_QK = MLA_NOPE + MLA_ROPE
N_GROUPS = 4
EXPERTS_PER_GROUP = 4
N_EXPERTS = 16
EXPERT_FF = 256

LANES = 128
SUBLANES = 8
VMEM_LIMIT = 56 * 1024 * 1024

TOK_TILE = 256
TOK_STEPS = TOK_TILE // SUBLANES
SCAN_STEPS = 128
MOE_SORT_TILE = 256
MOE_ROW_ALIGN = 16
MOE_SORTED_ROWS = MOE_SORT_TILE + LANES
MOE_SORT_SUBTILES = 8
MOE_UNSORT_SUBTILES = 4
MOE_BLOCK_TILES = 8
MOE_CHUNK = 512
MLA_Q_TILE = 256
ATTN_Q_TILE = 256
Z_COLS = 1792
HEAD_ORDER = np.array((0, 2, 1, 3))


def _params(sem):
    return pltpu.CompilerParams(dimension_semantics=sem, vmem_limit_bytes=VMEM_LIMIT)


def _layer_spec(layer):
    def spec(shape):
        n = len(shape)
        return pl.BlockSpec((None,) + tuple(shape), lambda *_: (layer,) + (0,) * n)
    return spec


def _modnorm(x, g, sc, sh):
    ms = jnp.mean(x * x, axis=-1, keepdims=True)
    return (x * lax.rsqrt(ms + EPS)) * g * (1.0 + sc) + sh


def _rmsnorm(x, g):
    ms = jnp.mean(x * x, axis=-1, keepdims=True)
    return (x * lax.rsqrt(ms + EPS)) * g


def _segment_ones(width, segment):
    lane = np.arange(width) // segment
    return jnp.asarray(lane[:, None] == lane[None, :], BF16)


def _segment_rsqrt(x, ones, n_real):
    if isinstance(ones, int):
        segment = ones
        lane = lax.broadcasted_iota(jnp.int32, (1, LANES), 1)
        outs = []
        for t in range(x.shape[1] // LANES):
            xt = x[:, LANES * t:LANES * (t + 1)]
            sq = xt * xt
            inv = None
            for s in range(LANES // segment):
                mask = (lane >= s * segment) & (lane < (s + 1) * segment)
                part = sq if segment == LANES else jnp.where(mask, sq, 0.0)
                r = lax.rsqrt(jnp.sum(part, axis=-1, keepdims=True) * (1.0 / n_real) + EPS)
                inv = r if inv is None else jnp.where(mask, r, inv)
            outs.append(jnp.broadcast_to(inv, sq.shape))
        return jnp.concatenate(outs, axis=1)
    sq = x * x
    hi = sq.astype(BF16)
    lo = (sq - hi.astype(F32)).astype(BF16)
    ss = jnp.dot(hi, ones, preferred_element_type=F32) + jnp.dot(lo, ones, preferred_element_type=F32)
    return lax.rsqrt(ss * (1.0 / n_real) + EPS)


def _rope(x, cos, sin_next, sin_prev, quarter):
    width = x.shape[1]
    return (x * cos + pltpu.roll(x, width - quarter, 1) * sin_next
            + pltpu.roll(x, quarter, 1) * sin_prev)


def _mla_kv(ckv_n, kr_placed, wukv, gk, ones_tile, rope_tabs):
    kv = jnp.dot(ckv_n.astype(BF16), wukv, preferred_element_type=F32)
    gain = gk[:, :LANES]
    kfull = kv[:, :MLA_HEADS * LANES] + jnp.concatenate([kr_placed] * MLA_HEADS, axis=1)
    inv_all = _segment_rsqrt(kfull, ones_tile, MLA_QK)
    rotated = None
    if rope_tabs is not None:
        cos, sin_next, sin_prev = rope_tabs
        quarter = MLA_ROPE // 4
        base = kr_placed * gain
        rotated = pltpu.roll(base, LANES - quarter, 1) * sin_next + pltpu.roll(base, quarter, 1) * sin_prev
    heads = []
    for h in range(MLA_HEADS):
        kf = kfull[:, LANES * h:LANES * (h + 1)]
        inv = inv_all[:, LANES * h:LANES * (h + 1)]
        kh = kf * gain
        if rotated is not None:
            kh = kh * cos + rotated
        heads.append(kh * inv)
    return jnp.concatenate(heads, axis=1), kv[:, MLA_HEADS * LANES:]


def _mod_kernel(c_ref, w_ref, b_ref, o_ref):
    c = c_ref[...]
    s = c * jax.nn.sigmoid(c)
    o_ref[...] = jnp.dot(s.astype(BF16), w_ref[...].astype(BF16), preferred_element_type=F32) + b_ref[...]


def _modulation(cond, w_mod, b_mod):
    n_rows = cond.shape[0]
    n_out = w_mod.shape[-1]
    tn = 1024
    return pl.pallas_call(
        _mod_kernel,
        grid=(DEPTH, n_out // tn),
        in_specs=[pl.BlockSpec((n_rows, D_MODEL), lambda l, j: (0, 0)),
                  pl.BlockSpec((None, D_MODEL, tn), lambda l, j: (l, 0, j)),
                  pl.BlockSpec((None, 1, tn), lambda l, j: (l, 0, j))],
        out_specs=pl.BlockSpec((None, n_rows, tn), lambda l, j: (l, 0, j)),
        out_shape=jax.ShapeDtypeStruct((DEPTH, n_rows, n_out), F32),
        compiler_params=_params(("arbitrary", "arbitrary")),
        name="modulation",
    )(cond, w_mod, b_mod)


def _group_rows(ref):
    v = ref[...]
    return v.reshape(v.shape[0] * v.shape[1], v.shape[2])


def _mod_rows(ref, steps):
    v = ref[...]
    if v.shape[0] == 1:
        return v[0]
    return jnp.broadcast_to(v, (v.shape[0], steps, v.shape[2])).reshape(v.shape[0] * steps, v.shape[2])


def _store_group(ref, val):
    ref[...] = val.reshape(ref.shape).astype(ref.dtype)


def _store_time_major(ref, val):
    steps = ref.shape[1] // SUBLANES
    for b in range(SUBLANES):
        for half in range(2):
            ref[half, pl.ds(b, steps, stride=SUBLANES), :] = (
                val[b * steps:(b + 1) * steps, LANES * half:LANES * (half + 1)])


def _load_time_major(ref):
    steps = ref.shape[1] // SUBLANES
    return jnp.concatenate(
        [jnp.concatenate([ref[half, pl.ds(b, steps, stride=SUBLANES), :] for half in range(2)], axis=1)
         for b in range(SUBLANES)], axis=0)


def _in_proj_kernel(rope, ctx, *refs):
    it = iter(refs)
    x_ref, sc_ref, sh_ref, g1_ref, wz_ref = (next(it) for _ in range(5))
    gq_ref, gk_ref, qlg_ref, wuq_ref, gmq_ref, kvg_ref, wukv_ref, gmk_ref = (next(it) for _ in range(8))
    oh_ref, ot_ref = next(it), next(it)
    seg_heads = oh_ref[...] if rope else A_HEAD_DIM
    seg_tiles = ot_ref[...] if rope else LANES
    if rope:
        ca_ref, sna_ref, spa_ref, cm_ref, snm_ref, spm_ref = (next(it) for _ in range(6))
    q_ref, k_ref, v_ref, s5u_ref, lx_ref, lg_ref, qm_ref, km_ref, vm_ref = (next(it) for _ in range(9))
    if ctx:
        k32_ref, v32_ref, ckv_ref, krp_ref = (next(it) for _ in range(4))

    steps = x_ref.shape[1]
    h = _modnorm(_group_rows(x_ref), g1_ref[...], _mod_rows(sc_ref, steps), _mod_rows(sh_ref, steps))
    z = jnp.dot(h.astype(BF16), wz_ref[...], preferred_element_type=F32)
    per_seq = lambda ref: jnp.concatenate([ref[...]] * SUBLANES, axis=0)

    qk = z[:, 0:384]
    qk = qk * _segment_rsqrt(qk, seg_heads, A_HEAD_DIM)
    q = qk[:, 0:256] * gq_ref[...]
    k = qk[:, 256:384] * gk_ref[...]
    v = z[:, 384:512]
    if ctx:
        _store_group(k32_ref, k)
        _store_group(v32_ref, v)
    if rope:
        quarter = A_HEAD_DIM // 4
        atabs = (per_seq(ca_ref), per_seq(sna_ref), per_seq(spa_ref))
        q = _rope(q, *atabs, quarter)
        k = _rope(k, *(t[:, :LANES] for t in atabs), quarter)
    _store_group(q_ref, q)
    _store_group(k_ref, k)
    _store_group(v_ref, v)

    _store_time_major(s5u_ref, z[:, 512:768])
    _store_time_major(lx_ref, z[:, 768:1024])
    _store_group(lg_ref, z[:, 1024:1280])

    mtabs = mtabs4 = None
    if rope:
        mtabs = tuple(per_seq(r) for r in (cm_ref, snm_ref, spm_ref))
        mtabs4 = tuple(jnp.concatenate([t] * MLA_HEADS, axis=1) for t in mtabs)
    ql = _rmsnorm(z[:, 1280:1536], qlg_ref[...])
    qm = jnp.dot(ql.astype(BF16), wuq_ref[...], preferred_element_type=F32)
    qm = qm * _segment_rsqrt(qm, seg_tiles, MLA_QK) * gmq_ref[...]
    if rope:
        qm = _rope(qm, *mtabs4, MLA_ROPE // 4)
    _store_group(qm_ref, qm)

    ckv_n = _rmsnorm(z[:, 1536:1664], kvg_ref[...])
    krp = z[:, 1664:1792]
    km, vm = _mla_kv(ckv_n, krp, wukv_ref[...], gmk_ref[...], seg_tiles, mtabs)
    _store_group(km_ref, km)
    _store_group(vm_ref, vm)
    if ctx:
        _store_group(ckv_ref, ckv_n)
        _store_group(krp_ref, krp)


def _group_specs(per_seq_mod):
    TS = TOK_STEPS
    tok = lambda c: pl.BlockSpec((SUBLANES, TS, c), lambda g, i: (g, i, 0))
    scan = pl.BlockSpec((None, 2, TS * SUBLANES, LANES), lambda g, i: (g, 0, i, 0))
    if per_seq_mod:
        mod = pl.BlockSpec((SUBLANES, 1, D_MODEL), lambda g, i: (g, 0, 0))
    else:
        mod = pl.BlockSpec((1, 1, D_MODEL), lambda g, i: (0, 0, 0))
    return tok, scan, mod


def _in_proj(x, sc, sh, lw, rope_tabs, ctx):
    wspec = _layer_spec(lw["layer"])
    B, L, _ = x.shape
    Bg = B // SUBLANES
    TS = TOK_STEPS
    rope = rope_tabs is not None
    tok, scan, mod = _group_specs(sc.shape[0] > 1)
    in_specs = [tok(D_MODEL), mod, mod, wspec((1, D_MODEL)), wspec((D_MODEL, Z_COLS)),
                wspec((1, 256)), wspec((1, 128)), wspec((1, 256)), wspec((256, 512)), wspec((1, 512)),
                wspec((1, 128)), wspec((128, 768)), wspec((1, 512)),
                pl.BlockSpec((384, 384), lambda g, i: (0, 0)), pl.BlockSpec((512, 512), lambda g, i: (0, 0))]
    args = [x, sc, sh, lw["g1"], lw["wz"], lw["gq"], lw["gk"], lw["qlg"], lw["wuq"], lw["gmq"],
            lw["kvg"], lw["wukv"], lw["gmk"], _segment_ones(384, A_HEAD_DIM), _segment_ones(MLA_HEADS * LANES, LANES)]
    if rope:
        in_specs += [pl.BlockSpec((TS, 256), lambda g, i: (i, 0))] * 3
        in_specs += [pl.BlockSpec((TS, 128), lambda g, i: (i, 0))] * 3
        args += list(rope_tabs)
    out_specs = [tok(256), tok(128), tok(128), scan, scan, tok(256), tok(512), tok(512), tok(256)]
    sds = jax.ShapeDtypeStruct
    scan_shape = sds((Bg, 2, L * SUBLANES, LANES), F32)
    out_shape = [sds((B, L, 256), BF16), sds((B, L, 128), BF16), sds((B, L, 128), BF16),
                 scan_shape, scan_shape,
                 sds((B, L, 256), F32), sds((B, L, 512), BF16), sds((B, L, 512), BF16),
                 sds((B, L, 256), BF16)]
    if ctx:
        out_specs += [tok(128), tok(128), tok(128), tok(128)]
        out_shape += [sds((B, L, 128), F32)] * 4
    return pl.pallas_call(
        functools.partial(_in_proj_kernel, rope, ctx),
        grid=(Bg, L // TS),
        in_specs=in_specs, out_specs=out_specs, out_shape=out_shape,
        compiler_params=_params(("parallel", "parallel")),
        name="in_proj_ctx" if ctx else "in_proj_lat",
    )(*args)


def _mla_cache_kernel(ckv_ref, krp_ref, wukv_ref, gmk_ref, ot_ref, km_ref, vm_ref):
    km, vm = _mla_kv(ckv_ref[...], krp_ref[...], wukv_ref[...], gmk_ref[...], ot_ref[...], None)
    km_ref[...] = km.astype(BF16)
    vm_ref[...] = vm.astype(BF16)


def _mla_cache_kv(cckv, ckr_placed, lw):
    wspec = _layer_spec(lw["layer"])
    B, P, _ = cckv.shape
    tok = lambda c: pl.BlockSpec((None, P, c), lambda b: (b, 0, 0))
    return pl.pallas_call(
        _mla_cache_kernel,
        grid=(B,),
        in_specs=[tok(128), tok(128), wspec((128, 768)), wspec((1, 512)),
                  pl.BlockSpec((512, 512), lambda b: (0, 0))],
        out_specs=[tok(512), tok(256)],
        out_shape=[jax.ShapeDtypeStruct((B, P, 512), BF16), jax.ShapeDtypeStruct((B, P, 256), BF16)],
        compiler_params=_params(("parallel",)),
        name="mla_cache_kv",
    )(cckv, ckr_placed, lw["wukv"], lw["gmk"], _segment_ones(MLA_HEADS * LANES, LANES))


def _sink_softmax(s, sink):
    sink = sink * LOG2_E
    m = jnp.maximum(jnp.max(s, axis=-1, keepdims=True), sink)
    e = jnp.exp2(s - m)
    den = jnp.sum(e, axis=-1, keepdims=True) + jnp.exp2(sink - m)
    return e.astype(BF16), 1.0 / den


def _gqa_tile(qt, keys, vst, sink_lo, sink_hi, mask):
    lane = lax.broadcasted_iota(jnp.int32, (1, LANES), 1)
    lo = lane < A_HEAD_DIM
    zero = jnp.zeros_like(qt)
    ps, invs = [], []
    for qh, sink in ((jnp.where(lo, qt, zero), sink_lo), (jnp.where(lo, zero, qt), sink_hi)):
        s = lax.dot_general(qh, keys, (((1,), (1,)), ((), ())), preferred_element_type=F32)
        if mask is not None:
            s = jnp.where(mask, s, NEG_INF)
        p, inv = _sink_softmax(s, sink)
        ps.append(p)
        invs.append(inv)
    o = jnp.dot(jnp.concatenate(ps, axis=1), vst, preferred_element_type=F32)
    return o * jnp.where(lo, invs[0], invs[1])


def _stack_kv_halves(v):
    lane = lax.broadcasted_iota(jnp.int32, (1, LANES), 1)
    lo = lane < A_HEAD_DIM
    zero = jnp.zeros_like(v)
    return jnp.concatenate([jnp.where(lo, v, zero), jnp.where(lo, zero, v)], axis=0)


def _attn_a_ctx_kernel(sink_ref, q_ref, k_ref, v_ref, o_ref):
    keys = k_ref[...]
    vst = _stack_kv_halves(v_ref[...])
    for t in range(2):
        o = _gqa_tile(q_ref[:, LANES * t:LANES * (t + 1)], keys, vst, sink_ref[t], sink_ref[2 + t], None)
        o_ref[:, LANES * t:LANES * (t + 1)] = o.astype(o_ref.dtype)


def _attn_a_ctx(q, k, v, sink):
    B, L, _ = q.shape
    tok = lambda c: pl.BlockSpec((None, L, c), lambda b: (b, 0, 0))
    return pl.pallas_call(
        _attn_a_ctx_kernel,
        grid=(B,),
        in_specs=[pl.BlockSpec(memory_space=pltpu.SMEM), tok(256), tok(128), tok(128)],
        out_specs=tok(256),
        out_shape=jax.ShapeDtypeStruct((B, L, 256), BF16),
        compiler_params=_params(("parallel",)),
        name="attn_a_ctx",
    )(sink, q, k, v)


def _attn_a_lat_kernel(sink_ref, q_ref, kc_ref, vc_ref, k_ref, v_ref, o_ref):
    i = pl.program_id(1)
    L = k_ref.shape[0]
    n_ctx = kc_ref.shape[0]
    rows = q_ref.shape[0]
    span = rows + 2 * WINDOW
    start = pl.multiple_of(jnp.clip(i * rows - WINDOW, 0, L - span), WINDOW)
    keys = jnp.concatenate([kc_ref[...], k_ref[pl.ds(start, span), :]], axis=0)
    vals = jnp.concatenate([vc_ref[...], v_ref[pl.ds(start, span), :]], axis=0)
    vst = _stack_kv_halves(vals)
    col = lax.broadcasted_iota(jnp.int32, (rows, n_ctx + span), 1)
    row = lax.broadcasted_iota(jnp.int32, (rows, n_ctx + span), 0)
    rel = (col - n_ctx + start) - (row + i * rows)
    mask = (col < n_ctx) | (jnp.abs(rel) <= WINDOW)
    for t in range(2):
        o = _gqa_tile(q_ref[:, LANES * t:LANES * (t + 1)], keys, vst, sink_ref[t], sink_ref[2 + t], mask)
        o_ref[:, LANES * t:LANES * (t + 1)] = o.astype(o_ref.dtype)


def _attn_a_lat(q, k, v, kc, vc, sink):
    B, L, _ = q.shape
    P = kc.shape[1]
    blk = lambda c: pl.BlockSpec((None, ATTN_Q_TILE, c), lambda b, i: (b, i, 0))
    whole = lambda n, c: pl.BlockSpec((None, n, c), lambda b, i: (b, 0, 0))
    return pl.pallas_call(
        _attn_a_lat_kernel,
        grid=(B, L // ATTN_Q_TILE),
        in_specs=[pl.BlockSpec(memory_space=pltpu.SMEM), blk(256), whole(P, 128), whole(P, 128),
                  whole(L, 128), whole(L, 128)],
        out_specs=blk(256),
        out_shape=jax.ShapeDtypeStruct((B, L, 256), BF16),
        compiler_params=_params(("parallel", "parallel")),
        name="attn_a_lat",
    )(sink, q, kc, vc, k, v)


def _mla_stack_values(vals):
    lane = lax.broadcasted_iota(jnp.int32, (1, MLA_HEADS * MLA_V), 1)
    zero = jnp.zeros_like(vals)
    return jnp.concatenate([jnp.where((lane >= MLA_V * h) & (lane < MLA_V * (h + 1)), vals, zero)
                            for h in range(MLA_HEADS)], axis=0)


def _mla_attend(q, key_parts, vst):
    lane = lax.broadcasted_iota(jnp.int32, (1, MLA_HEADS * MLA_V), 1)
    ps = []
    inv = jnp.zeros((q.shape[0], MLA_HEADS * MLA_V), F32)
    for h in range(MLA_HEADS):
        qh = q[:, LANES * h:LANES * (h + 1)]
        ss = [lax.dot_general(qh, kp[:, LANES * h:LANES * (h + 1)], (((1,), (1,)), ((), ())),
                              preferred_element_type=F32) for kp in key_parts]
        m = functools.reduce(jnp.maximum, [jnp.max(s, axis=-1, keepdims=True) for s in ss])
        es = [jnp.exp2(s - m) for s in ss]
        den = functools.reduce(lambda a, b: a + b, [jnp.sum(e, axis=-1, keepdims=True) for e in es])
        ps += [e.astype(BF16) for e in es]
        inv = jnp.where((lane >= MLA_V * h) & (lane < MLA_V * (h + 1)), 1.0 / den, inv)
    return jnp.dot(jnp.concatenate(ps, axis=1), vst, preferred_element_type=F32) * inv


def _mla_ctx_kernel(q_ref, k_ref, v_ref, o_ref):
    o_ref[...] = _mla_attend(q_ref[...], [k_ref[...]], _mla_stack_values(v_ref[...])).astype(o_ref.dtype)


def _mla_ctx(q, k, v):
    B, L, _ = q.shape
    tok = lambda c: pl.BlockSpec((None, L, c), lambda b: (b, 0, 0))
    return pl.pallas_call(
        _mla_ctx_kernel,
        grid=(B,),
        in_specs=[tok(512), tok(512), tok(256)],
        out_specs=tok(256),
        out_shape=jax.ShapeDtypeStruct((B, L, 256), BF16),
        compiler_params=_params(("parallel",)),
        name="mla_ctx",
    )(q, k, v)


def _mla_lat_kernel(q_ref, kc_ref, vc_ref, k_ref, v_ref, o_ref):
    vst = _mla_stack_values(jnp.concatenate([vc_ref[...], v_ref[...]], axis=0))
    o = _mla_attend(q_ref[...], [kc_ref[...], k_ref[...]], vst)
    o_ref[...] = o.astype(o_ref.dtype)


def _mla_lat(q, k, v, kc, vc):
    B, L, _ = q.shape
    P = kc.shape[1]
    TQ = MLA_Q_TILE
    blk = lambda c: pl.BlockSpec((None, TQ, c), lambda b, i: (b, i, 0))
    whole = lambda n, c: pl.BlockSpec((None, n, c), lambda b, i: (b, 0, 0))
    return pl.pallas_call(
        _mla_lat_kernel,
        grid=(B, L // TQ),
        in_specs=[blk(512), whole(P, 512), whole(P, 256), whole(L, 512), whole(L, 256)],
        out_specs=blk(256),
        out_shape=jax.ShapeDtypeStruct((B, L, 256), BF16),
        compiler_params=_params(("parallel", "parallel")),
        name="mla_lat",
    )(q, kc, vc, k, v)


def _join_halves(ref):
    return jnp.concatenate([ref[0], ref[1]], axis=1)


def _split_halves(ref, val):
    ref[0] = val[:, :LANES]
    ref[1] = val[:, LANES:]


def _s5_kernel(uf_ref, ub_ref, bf_ref, bb_ref, a_ref, h0_ref, cf_ref, cb_ref,
               yf_ref, yb_ref, fin_ref, s_ref, st_ref):
    i = pl.program_id(1)
    steps = SCAN_STEPS

    @pl.when(i == 0)
    def _():
        st_ref[...] = h0_ref[...]

    uf = _join_halves(uf_ref).astype(BF16)
    ub = _join_halves(ub_ref).astype(BF16)
    s_ref[0] = jnp.dot(uf, bf_ref[:, :S5_WIDTH], preferred_element_type=F32)
    s_ref[1] = jnp.dot(uf, bf_ref[:, S5_WIDTH:], preferred_element_type=F32)
    s_ref[2] = jnp.dot(ub, bb_ref[:, :S5_WIDTH], preferred_element_type=F32)
    s_ref[3] = jnp.dot(ub, bb_ref[:, S5_WIDTH:], preferred_element_type=F32)

    lane_chunk = 2 * LANES
    for c in range(S5_WIDTH // lane_chunk):
        sl = slice(lane_chunk * c, lane_chunk * (c + 1))
        arf, aif, arb, aib = (a_ref[n, :, sl] for n in range(4))

        def body(j, carry):
            hrf, hif, hrb, hib = carry
            rf = pl.multiple_of(j * SUBLANES, SUBLANES)
            rb = pl.multiple_of((steps - 1 - j) * SUBLANES, SUBLANES)
            nrf = arf * hrf - aif * hif + s_ref[0, pl.ds(rf, SUBLANES), sl]
            nif = arf * hif + aif * hrf + s_ref[1, pl.ds(rf, SUBLANES), sl]
            nrb = arb * hrb - aib * hib + s_ref[2, pl.ds(rb, SUBLANES), sl]
            nib = arb * hib + aib * hrb + s_ref[3, pl.ds(rb, SUBLANES), sl]
            s_ref[0, pl.ds(rf, SUBLANES), sl] = nrf
            s_ref[1, pl.ds(rf, SUBLANES), sl] = nif
            s_ref[2, pl.ds(rb, SUBLANES), sl] = nrb
            s_ref[3, pl.ds(rb, SUBLANES), sl] = nib
            return nrf, nif, nrb, nib

        fin = lax.fori_loop(0, steps, body, tuple(st_ref[n, :, sl] for n in range(4)), unroll=8)
        for n in range(4):
            st_ref[n, :, sl] = fin[n]

    hf = jnp.concatenate([s_ref[0].astype(BF16), s_ref[1].astype(BF16)], axis=1)
    _split_halves(yf_ref, jnp.dot(hf, cf_ref[...], preferred_element_type=F32))
    hb = jnp.concatenate([s_ref[2].astype(BF16), s_ref[3].astype(BF16)], axis=1)
    _split_halves(yb_ref, jnp.dot(hb, cb_ref[...], preferred_element_type=F32))
    fin_ref[...] = st_ref[...]


def _s5_scan(u_rows, lw, h0):
    wspec = _layer_spec(lw["layer"])
    Bg, _, rows, _ = u_rows.shape
    R = SCAN_STEPS * SUBLANES
    n = rows // R
    fwd = pl.BlockSpec((None, 2, R, LANES), lambda g, i: (g, 0, i, 0))
    bwd = pl.BlockSpec((None, 2, R, LANES), lambda g, i: (g, 0, n - 1 - i, 0))
    st = pl.BlockSpec((None, 4, SUBLANES, S5_WIDTH), lambda g, i: (g, 0, 0, 0))
    return pl.pallas_call(
        _s5_kernel,
        grid=(Bg, n),
        in_specs=[fwd, bwd, wspec((256, 2 * S5_WIDTH)), wspec((256, 2 * S5_WIDTH)),
                  wspec((4, SUBLANES, S5_WIDTH)), st, wspec((2 * S5_WIDTH, 256)), wspec((2 * S5_WIDTH, 256))],
        out_specs=[fwd, bwd, st],
        out_shape=[jax.ShapeDtypeStruct(u_rows.shape, F32), jax.ShapeDtypeStruct(u_rows.shape, F32),
                   jax.ShapeDtypeStruct((Bg, 4, SUBLANES, S5_WIDTH), F32)],
        scratch_shapes=[pltpu.VMEM((4, R, S5_WIDTH), F32), pltpu.VMEM((4, SUBLANES, S5_WIDTH), F32)],
        compiler_params=_params(("parallel", "arbitrary")),
        name="s5_scan",
    )(u_rows, u_rows, lw["s5_bf"], lw["s5_bb"], lw["s5_a"], h0, lw["s5_cf"], lw["s5_cb"])


def _lru_gates(x_ref, pre_ref, post_ref, has_pre, has_post, cw_ref, cb_ref, w_ref, b_ref, sp_ref, a_ref, h_ref):
    R = x_ref.shape[1]
    pre = jnp.where(has_pre, _join_halves(pre_ref), 0.0)
    post = jnp.where(has_post, _join_halves(post_ref), 0.0)
    xp = jnp.concatenate([pre, _join_halves(x_ref), post], axis=0)
    xc = cb_ref[...]
    for t in range(LRU_CONV):
        xc = xc + xp[SUBLANES * t:SUBLANES * t + R] * cw_ref[t:t + 1, :]
    g = jnp.dot(xc.astype(BF16), w_ref[...], preferred_element_type=F32) + b_ref[...]
    r = jax.nn.sigmoid(g[:, :LRU_WIDTH])
    ig = jax.nn.sigmoid(g[:, LRU_WIDTH:])
    log_a = (-LRU_C) * r * sp_ref[...]
    a = jnp.exp(log_a)
    a_ref[...] = a
    h_ref[...] = jnp.sqrt(1.0 - a * a) * (ig * xc)


def _lru_kernel(xf_ref, xfp_ref, xfn_ref, xb_ref, xbp_ref, xbn_ref, cw_ref, cb_ref, wf_ref, wb_ref,
                bf_ref, bb_ref, spf_ref, spb_ref, h0_ref, of_ref, ob_ref, fin_ref,
                af_ref, ab_ref, hf_ref, hb_ref, st_ref):
    i = pl.program_id(1)
    n = pl.num_programs(1)
    steps = SCAN_STEPS

    @pl.when(i == 0)
    def _():
        st_ref[...] = h0_ref[...]

    _lru_gates(xf_ref, xfp_ref, xfn_ref, i > 0, i < n - 1, cw_ref, cb_ref, wf_ref, bf_ref, spf_ref, af_ref, hf_ref)
    _lru_gates(xb_ref, xbp_ref, xbn_ref, i < n - 1, i > 0, cw_ref, cb_ref, wb_ref, bb_ref, spb_ref, ab_ref, hb_ref)

    def body(j, carry):
        hf, hb = carry
        rf = pl.multiple_of(j * SUBLANES, SUBLANES)
        rb = pl.multiple_of((steps - 1 - j) * SUBLANES, SUBLANES)
        nf = af_ref[pl.ds(rf, SUBLANES), :] * hf + hf_ref[pl.ds(rf, SUBLANES), :]
        nb = ab_ref[pl.ds(rb, SUBLANES), :] * hb + hb_ref[pl.ds(rb, SUBLANES), :]
        hf_ref[pl.ds(rf, SUBLANES), :] = nf
        hb_ref[pl.ds(rb, SUBLANES), :] = nb
        return nf, nb

    ff, fb = lax.fori_loop(0, steps, body, (st_ref[0], st_ref[1]), unroll=8)
    st_ref[0] = ff
    st_ref[1] = fb
    fin_ref[...] = st_ref[...]
    _split_halves(of_ref, hf_ref[...])
    _split_halves(ob_ref, hb_ref[...])


def _lru_scan(x_rows, lw, h0):
    wspec = _layer_spec(lw["layer"])
    Bg, _, rows, _ = x_rows.shape
    R = SCAN_STEPS * SUBLANES
    n = rows // R
    pre_rows = 2 * SUBLANES
    fwd = lambda g, i: i
    bwd = lambda g, i: n - 1 - i
    blk = lambda m: pl.BlockSpec((None, 2, R, LANES), lambda g, i: (g, 0, m(g, i), 0))
    pre = lambda m: pl.BlockSpec(
        (None, 2, pre_rows, LANES), lambda g, i: (g, 0, jnp.maximum(m(g, i) * (R // pre_rows) - 1, 0), 0))
    post = lambda m: pl.BlockSpec(
        (None, 2, SUBLANES, LANES),
        lambda g, i: (g, 0, jnp.minimum((m(g, i) + 1) * (R // SUBLANES), rows // SUBLANES - 1), 0))
    st = pl.BlockSpec((None, 2, SUBLANES, 256), lambda g, i: (g, 0, 0, 0))
    return pl.pallas_call(
        _lru_kernel,
        grid=(Bg, n),
        in_specs=[blk(fwd), pre(fwd), post(fwd), blk(bwd), pre(bwd), post(bwd),
                  wspec((LRU_CONV, 256)), wspec((1, 256)), wspec((256, 512)), wspec((256, 512)),
                  wspec((1, 512)), wspec((1, 512)), wspec((1, 256)), wspec((1, 256)), st],
        out_specs=[blk(fwd), blk(bwd), st],
        out_shape=[jax.ShapeDtypeStruct(x_rows.shape, F32), jax.ShapeDtypeStruct(x_rows.shape, F32),
                   jax.ShapeDtypeStruct((Bg, 2, SUBLANES, 256), F32)],
        scratch_shapes=[pltpu.VMEM((R, 256), F32), pltpu.VMEM((R, 256), F32), pltpu.VMEM((R, 256), F32),
                        pltpu.VMEM((R, 256), F32), pltpu.VMEM((2, SUBLANES, 256), F32)],
        compiler_params=_params(("parallel", "arbitrary")),
        name="lru_scan",
    )(x_rows, x_rows, x_rows, x_rows, x_rows, x_rows, lw["lru_cw"], lw["lru_cb"], lw["lru_wf"], lw["lru_wb"],
      lw["lru_bf"], lw["lru_bb"], lw["lru_spf"], lw["lru_spb"], h0)


def _merge_kernel(x_ref, sc_ref, sh_ref, gt_ref, g1_ref, wg_ref, oa_ref, u_ref, yf_ref, yb_ref, d_ref,
                  wglu_ref, hf_ref, hb_ref, lg_ref, od_ref, wb_ref, wo_ref, o_ref):
    steps = x_ref.shape[1]
    x = _group_rows(x_ref)
    h = _modnorm(x, g1_ref[...], _mod_rows(sc_ref, steps), _mod_rows(sh_ref, steps)).astype(BF16)

    yb5 = jax.nn.gelu(d_ref[...] * _load_time_major(u_ref) + _load_time_major(yf_ref) + _load_time_major(yb_ref))
    gv = jnp.dot(yb5.astype(BF16), wglu_ref[...], preferred_element_type=F32)
    o_b = gv[:, :BRANCH_W] * jax.nn.sigmoid(gv[:, BRANCH_W:])
    o_c = (_load_time_major(hf_ref) + _load_time_major(hb_ref)) * jax.nn.gelu(_group_rows(lg_ref))
    branches = (_group_rows(oa_ref), o_b.astype(BF16), o_c.astype(BF16), _group_rows(od_ref))

    acc = jnp.zeros(x.shape, F32)
    for n in range(N_BRANCH):
        gate = jnp.dot(h, wg_ref[:, D_MODEL * n:D_MODEL * (n + 1)], preferred_element_type=F32)
        proj = jnp.dot(branches[n], wb_ref[n], preferred_element_type=F32)
        acc = acc + jax.nn.sigmoid(gate) * proj
    out = jnp.dot(acc.astype(BF16), wo_ref[...], preferred_element_type=F32)
    _store_group(o_ref, x + _mod_rows(gt_ref, steps) * out)


def _merge(x, sc, sh, gt, lw, o_a, s5u_t, yf_t, yb_t, hf_t, hb_t, lg, o_d):
    wspec = _layer_spec(lw["layer"])
    B, L, _ = x.shape
    tok, scan, mod = _group_specs(sc.shape[0] > 1)
    return pl.pallas_call(
        _merge_kernel,
        grid=(B // SUBLANES, L // TOK_STEPS),
        in_specs=[tok(D_MODEL), mod, mod, mod, wspec((1, D_MODEL)), wspec((D_MODEL, N_BRANCH * D_MODEL)),
                  tok(256), scan, scan, scan, wspec((1, 256)), wspec((256, 512)),
                  scan, scan, tok(256), tok(256), wspec((N_BRANCH, BRANCH_W, D_MODEL)),
                  wspec((D_MODEL, D_MODEL))],
        out_specs=tok(D_MODEL),
        out_shape=jax.ShapeDtypeStruct((B, L, D_MODEL), F32),
        compiler_params=_params(("parallel", "parallel")),
        name="merge",
    )(x, sc, sh, gt, lw["g1"], lw["wg"], o_a, s5u_t, yf_t, yb_t, lw["s5_d"], lw["wglu"],
      hf_t, hb_t, lg, o_d, lw["wb"], lw["wo"])


def _first_index(values, target):
    idx = jnp.full_like(target, float(len(values) - 1))
    for n in range(len(values) - 2, -1, -1):
        idx = jnp.where(values[n] == target, float(n), idx)
    return idx


def _list_max(values):
    return functools.reduce(jnp.maximum, values)


def _moe_sort_kernel(x_ref, sc_ref, sh_ref, g2_ref, wrh_ref, wrl_ref, br_ref, tri_ref,
                     hs_ref, gs_ref, dcol_ref, meta_ref):
    for s in range(x_ref.shape[0] // MOE_SORT_TILE):
        tok = pl.ds(s * MOE_SORT_TILE, MOE_SORT_TILE)
        srt = pl.ds(s * MOE_SORTED_ROWS, MOE_SORTED_ROWS)
        _moe_sort_tile(x_ref.at[tok], sc_ref, sh_ref, g2_ref, wrh_ref, wrl_ref, br_ref, tri_ref,
                       hs_ref.at[srt], gs_ref.at[srt], dcol_ref.at[tok], meta_ref.at[s])


def _moe_sort_tile(x_ref, sc_ref, sh_ref, g2_ref, wrh_ref, wrl_ref, br_ref, tri_ref,
                   hs_ref, gs_ref, dcol_ref, meta_ref):
    T = MOE_SORT_TILE
    h = _modnorm(x_ref[...], g2_ref[...], sc_ref[...], sh_ref[...])
    hh = h.astype(BF16)
    hl = (h - hh.astype(F32)).astype(BF16)
    nt = (((1,), (1,)), ((), ()))
    logits = (jnp.dot(hh, wrh_ref[...], preferred_element_type=F32)
              + jnp.dot(hl, wrh_ref[...], preferred_element_type=F32)
              + jnp.dot(hh, wrl_ref[...], preferred_element_type=F32)) + br_ref[...]
    lt = logits.T
    gl = [lt[g:g + 1, :] for g in range(N_GROUPS)]
    gmax = _list_max(gl)
    g_idx = _first_index(gl, gmax)
    pg = 1.0 / sum(jnp.exp(v - gmax) for v in gl)
    hot = [g_idx == float(g) for g in range(N_GROUPS)]
    el = []
    for e in range(EXPERTS_PER_GROUP):
        v = jnp.zeros_like(gmax)
        for g in range(N_GROUPS):
            r = N_GROUPS + EXPERTS_PER_GROUP * g + e
            v = jnp.where(hot[g], lt[r:r + 1, :], v)
        el.append(v)
    emax = _list_max(el)
    ee = [jnp.exp(v - emax) for v in el]
    esum = sum(ee)
    pe = [v / esum for v in ee]
    v1 = _list_max(pe)
    i1 = _first_index(pe, v1)
    pe2 = [jnp.where(i1 == float(e), -1.0, pe[e]) for e in range(EXPERTS_PER_GROUP)]
    v2 = _list_max(pe2)
    i2 = _first_index(pe2, v2)
    tot = v1 + v2
    w = [jnp.where(i1 == float(e), pg * v1 / tot, jnp.where(i2 == float(e), pg * v2 / tot, 0.0))
         for e in range(EXPERTS_PER_GROUP)]

    zero_row = jnp.zeros_like(gmax)
    g8 = jnp.concatenate([jnp.where(hot[g], 1.0, 0.0) for g in range(N_GROUPS)] + [zero_row] * 4, axis=0)
    cum = jnp.dot(g8.astype(BF16), tri_ref[...], preferred_element_type=F32)
    off = jnp.zeros((1, 1), F32)
    dest = zero_row
    counts = []
    for g in range(N_GROUPS):
        cnt = cum[g:g + 1, T - 1:T]
        padded = jnp.floor((cnt + (MOE_ROW_ALIGN - 1.0)) * (1.0 / MOE_ROW_ALIGN)) * MOE_ROW_ALIGN
        dest = jnp.where(hot[g], off + cum[g:g + 1, :] - 1.0, dest)
        off = off + padded
        counts.append(padded)
    rows = lax.broadcasted_iota(jnp.int32, (MOE_SORTED_ROWS, T), 0).astype(F32)
    perm = jnp.where(rows == dest, 1.0, 0.0).astype(BF16)
    hs_ref[...] = jnp.dot(perm, hh, preferred_element_type=F32).astype(BF16)
    gates = jnp.concatenate(w + [jnp.zeros((LANES - EXPERTS_PER_GROUP, T), F32)], axis=0)
    ghi = gates.astype(BF16)
    glo = (gates - ghi.astype(F32)).astype(BF16)
    gs_ref[...] = (lax.dot_general(perm, ghi, nt, preferred_element_type=F32)
                   + lax.dot_general(perm, glo, nt, preferred_element_type=F32))
    r_i = lax.broadcasted_iota(jnp.int32, (T, T), 0)
    c_i = lax.broadcasted_iota(jnp.int32, (T, T), 1)
    dcol = jnp.sum(jnp.where(r_i == c_i, jnp.broadcast_to(dest, (T, T)), 0.0), axis=1, keepdims=True)
    dcol_ref[...] = jnp.broadcast_to(dcol, (T, LANES))
    meta_ref[...] = jnp.concatenate([jnp.broadcast_to(c, (1, LANES)) for c in counts]
                                    + [jnp.zeros((SUBLANES - N_GROUPS, LANES), F32)], axis=0)


def _moe_sort(xf, sc, sh, lw, tiles_per_mod):
    wspec = _layer_spec(lw["layer"])
    N = xf.shape[0]
    T = MOE_SORT_TILE
    n_tiles = N // T
    sub = MOE_SORT_SUBTILES
    per_b = sc.shape[0] > 1
    mod = pl.BlockSpec((None, 1, D_MODEL), lambda t: (t * sub // tiles_per_mod if per_b else 0, 0, 0))
    sds = jax.ShapeDtypeStruct
    return pl.pallas_call(
        _moe_sort_kernel,
        grid=(n_tiles // sub,),
        in_specs=[pl.BlockSpec((sub * T, D_MODEL), lambda t: (t, 0)), mod, mod, wspec((1, D_MODEL)),
                  wspec((D_MODEL, LANES)), wspec((D_MODEL, LANES)), wspec((1, LANES)), wspec((T, T))],
        out_specs=[pl.BlockSpec((sub * MOE_SORTED_ROWS, D_MODEL), lambda t: (t, 0)),
                   pl.BlockSpec((sub * MOE_SORTED_ROWS, LANES), lambda t: (t, 0)),
                   pl.BlockSpec((sub * T, LANES), lambda t: (t, 0)),
                   pl.BlockSpec((sub, SUBLANES, LANES), lambda t: (t, 0, 0))],
        out_shape=[sds((n_tiles * MOE_SORTED_ROWS, D_MODEL), BF16), sds((n_tiles * MOE_SORTED_ROWS, LANES), F32),
                   sds((N, LANES), F32), sds((n_tiles, SUBLANES, LANES), F32)],
        compiler_params=_params(("parallel",)),
        name="moe_sort",
    )(xf, sc, sh, lw["g2"], lw["wr_hi"], lw["wr_lo"], lw["br"], lw["tri"])


def _moe_expert_kernel(cnt_ref, hs_ref, gs_ref, wga_ref, wup_ref, wdn_ref, ys_ref, ch_ref, cg_ref, cy_ref):
    j = pl.program_id(0)
    g = pl.program_id(1)
    piece = MOE_ROW_ALIGN

    @pl.when((j == 0) & (g == 0))
    def _():
        ch_ref[...] = jnp.zeros_like(ch_ref)
        cg_ref[...] = jnp.zeros_like(cg_ref)

    @pl.when(g == 0)
    def _():
        ys_ref[...] = jnp.zeros_like(ys_ref)

    def for_each_piece(move):
        packed = jnp.int32(0)
        for t in range(MOE_BLOCK_TILES):
            base = (j * MOE_BLOCK_TILES + t) * N_GROUPS
            start = jnp.int32(t * MOE_SORTED_ROWS)
            for g2 in range(N_GROUPS - 1):
                start = start + jnp.where(g2 < g, cnt_ref[base + g2], 0)
            n = cnt_ref[base + g]

            def body(k, carry, start=start, packed=packed):
                move(pl.multiple_of(start + k * piece, piece), pl.multiple_of(packed + k * piece, piece))
                return carry

            lax.fori_loop(0, lax.shift_right_logical(n, 4), body, 0)
            packed = packed + n
        return packed

    def pack(src, dst):
        ch_ref[pl.ds(dst, piece), :] = hs_ref[pl.ds(src, piece), :]
        cg_ref[pl.ds(dst, piece), :] = gs_ref[pl.ds(src, piece), :]

    rows = for_each_piece(pack)

    def experts(r, size):
        h = ch_ref[pl.ds(r, size), :]
        acts = []
        for e in range(EXPERTS_PER_GROUP):
            a = jnp.dot(h, wga_ref[e], preferred_element_type=F32)
            u = jnp.dot(h, wup_ref[e], preferred_element_type=F32)
            acts.append(((a * jax.nn.sigmoid(a)) * u * cg_ref[pl.ds(r, size), e:e + 1]).astype(BF16))
        y = jnp.dot(jnp.concatenate(acts, axis=1), wdn_ref[...], preferred_element_type=F32)
        cy_ref[pl.ds(r, size), :] = y.astype(BF16)

    big, mid, small = MOE_CHUNK, MOE_CHUNK // 2, MOE_CHUNK // 4
    n_big = lax.shift_right_logical(rows, big.bit_length() - 1)

    def chunk(c, carry):
        experts(pl.multiple_of(c * big, big), big)
        return carry

    lax.fori_loop(0, n_big, chunk, 0)
    tail = pl.multiple_of(n_big * big, big)
    rem = rows - tail
    pl.when(rem > mid + small)(lambda: experts(tail, big))
    pl.when((rem > small) & (rem <= mid + small))(lambda: experts(tail, mid))
    pl.when((rem > mid) & (rem <= mid + small))(lambda: experts(pl.multiple_of(tail + mid, small), small))
    pl.when((rem > 0) & (rem <= small))(lambda: experts(tail, small))

    def unpack(src, dst):
        ys_ref[pl.ds(src, piece), :] = cy_ref[pl.ds(dst, piece), :]

    for_each_piece(unpack)


def _moe_experts(cnt, hs, gs, lw):
    rows = MOE_BLOCK_TILES * MOE_SORTED_ROWS
    cap = MOE_BLOCK_TILES * MOE_SORT_TILE
    blk = lambda c: pl.BlockSpec((rows, c), lambda j, g, cnt: (j, 0))
    layer = lw["layer"]
    wspec = lambda r, c: pl.BlockSpec((None, EXPERTS_PER_GROUP, r, c), lambda j, g, cnt: (layer, g, 0, 0))
    return pl.pallas_call(
        _moe_expert_kernel,
        grid_spec=pltpu.PrefetchScalarGridSpec(
            num_scalar_prefetch=1, grid=(hs.shape[0] // rows, N_GROUPS),
            in_specs=[blk(D_MODEL), blk(LANES), wspec(D_MODEL, EXPERT_FF), wspec(D_MODEL, EXPERT_FF),
                      pl.BlockSpec((None, EXPERTS_PER_GROUP * EXPERT_FF, D_MODEL),
                                   lambda j, g, cnt: (layer, g, 0))],
            out_specs=blk(D_MODEL),
            scratch_shapes=[pltpu.VMEM((cap, D_MODEL), BF16), pltpu.VMEM((cap, LANES), F32),
                            pltpu.VMEM((cap, D_MODEL), BF16)]),
        out_shape=jax.ShapeDtypeStruct(hs.shape, BF16),
        compiler_params=_params(("arbitrary", "arbitrary")),
        name="moe_experts",
    )(cnt, hs, gs, lw["wga"], lw["wup"], lw["wdn"])


def _moe_unsort_kernel(x_ref, gt_ref, ys_ref, dcol_ref, o_ref):
    cols = lax.broadcasted_iota(jnp.int32, (MOE_SORT_TILE, MOE_SORTED_ROWS), 1).astype(F32)
    for s in range(x_ref.shape[0] // MOE_SORT_TILE):
        tok = pl.ds(s * MOE_SORT_TILE, MOE_SORT_TILE)
        perm_t = jnp.where(cols == dcol_ref[tok, 0:1], 1.0, 0.0).astype(BF16)
        y = jnp.dot(perm_t, ys_ref[pl.ds(s * MOE_SORTED_ROWS, MOE_SORTED_ROWS), :], preferred_element_type=F32)
        o_ref[tok, :] = x_ref[tok, :] + gt_ref[...] * y


def _moe_unsort(xf, gt, ys, dcol, tiles_per_mod):
    N = xf.shape[0]
    sub = MOE_UNSORT_SUBTILES
    T = MOE_SORT_TILE * sub
    per_b = gt.shape[0] > 1
    mod = pl.BlockSpec((None, 1, D_MODEL), lambda t: (t * sub // tiles_per_mod if per_b else 0, 0, 0))
    tok = pl.BlockSpec((T, D_MODEL), lambda t: (t, 0))
    return pl.pallas_call(
        _moe_unsort_kernel,
        grid=(N // T,),
        in_specs=[tok, mod, pl.BlockSpec((sub * MOE_SORTED_ROWS, D_MODEL), lambda t: (t, 0)),
                  pl.BlockSpec((T, LANES), lambda t: (t, 0))],
        out_specs=tok,
        out_shape=jax.ShapeDtypeStruct((N, D_MODEL), F32),
        compiler_params=_params(("parallel",)),
        name="moe_unsort",
    )(xf, gt, ys, dcol)


def _moe(x, sc, sh, gt, lw):
    B, L, _ = x.shape
    N = B * L
    T = MOE_SORT_TILE
    xf = x.reshape(N, D_MODEL)
    hs, gs, dcol, meta = _moe_sort(xf, sc, sh, lw, L // T)

    cnt = meta[:, :N_GROUPS, 0].astype(jnp.int32).reshape(-1)
    ys = _moe_experts(cnt, hs, gs, lw)
    out = _moe_unsort(xf, gt, ys, dcol, L // T)
    return out.reshape(B, L, D_MODEL)


def _block_diag(blocks):
    n, r, c = blocks.shape
    eye = jnp.eye(n, dtype=blocks.dtype)
    return jnp.einsum("nrc,nm->nrmc", blocks, eye).reshape(n * r, n * c)


def _rope_tables(seq_len, dim, width, offset):
    rows = seq_len // GRID_W
    r, col = np.meshgrid(np.arange(rows), np.arange(GRID_W), indexing="ij")
    r = r.reshape(-1).astype(np.float64)
    col = col.reshape(-1).astype(np.float64)
    quarter = dim // 4
    freqs = ROPE_BASE ** (-np.arange(quarter, dtype=np.float64) / quarter)
    ang_r = r[:, None] * freqs
    ang_c = col[:, None] * freqs
    zero = np.zeros_like(ang_r)
    cos = np.cos(np.concatenate([ang_r, ang_r, ang_c, ang_c], axis=-1))
    sin_next = np.concatenate([-np.sin(ang_r), zero, -np.sin(ang_c), zero], axis=-1)
    sin_prev = np.concatenate([zero, np.sin(ang_r), zero, np.sin(ang_c)], axis=-1)

    def place(t, fill):
        return np.pad(t, ((0, 0), (offset, width - offset - dim)), constant_values=fill).astype(np.float32)

    return place(cos, 1.0), place(sin_next, 0.0), place(sin_prev, 0.0)


def _layer_weights(l, w):
    lw = {}
    w_in = w["w_in"][l]
    q_cols = w_in[:, 0:256].reshape(D_MODEL, A_HEADS, A_HEAD_DIM)[:, HEAD_ORDER].reshape(D_MODEL, 256)
    kr_cols = jnp.pad(w_in[:, 1664:1696], ((0, 0), (MLA_NOPE, LANES - MLA_QK)))
    lw["wz"] = jnp.concatenate([q_cols, w_in[:, 256:1664], kr_cols], axis=1).astype(BF16)
    lw["wg"] = w_in[:, 1696:].astype(BF16)
    lw["g1"] = w["norm1_g"][l][None]
    lw["g2"] = w["norm2_g"][l][None]
    lw["gq"] = jnp.tile(w["a_qnorm_g"][l], A_HEADS)[None] * (A_HEAD_DIM ** -0.5 * LOG2_E)
    lw["gk"] = jnp.tile(w["a_knorm_g"][l], A_KV_HEADS)[None]
    lw["sink"] = w["a_sink"][l]
    lw["qlg"] = w["mla_q_lat_norm"][l][None]
    wuq = w["mla_w_uq"][l].reshape(MLA_Q_LORA, MLA_HEADS, MLA_QK)
    lw["wuq"] = jnp.pad(wuq, ((0, 0), (0, 0), (0, LANES - MLA_QK))).reshape(MLA_Q_LORA, MLA_HEADS * LANES).astype(BF16)
    pad_g = lambda g: jnp.tile(jnp.pad(g, (0, LANES - MLA_QK)), MLA_HEADS)[None]
    lw["gmq"] = pad_g(w["mla_qnorm_g"][l]) * (MLA_QK ** -0.5 * LOG2_E)
    lw["gmk"] = pad_g(w["mla_knorm_g"][l])
    lw["kvg"] = w["mla_kv_norm"][l][None]
    wukv = w["mla_w_ukv"][l].reshape(MLA_KV_LORA, MLA_HEADS, MLA_NOPE + MLA_V)
    wk = jnp.pad(wukv[:, :, :MLA_NOPE], ((0, 0), (0, 0), (0, LANES - MLA_NOPE))).reshape(MLA_KV_LORA, MLA_HEADS * LANES)
    wv = wukv[:, :, MLA_NOPE:].reshape(MLA_KV_LORA, MLA_HEADS * MLA_V)
    lw["wukv"] = jnp.concatenate([wk, wv], axis=1).astype(BF16)

    lre = w["s5_lambda_re"][l]
    lim = w["s5_lambda_im"][l]
    dt = jnp.exp(w["s5_log_step"][l])[:, :, None]
    mag = jnp.exp(lre * dt)
    ar, ai = mag * jnp.cos(lim * dt), mag * jnp.sin(lim * dt)
    den = lre * lre + lim * lim
    fr = ((ar - 1.0) * lre + ai * lim) / den
    fi = (ai * lre - (ar - 1.0) * lim) / den
    br, bi = w["s5_b_re"][l], w["s5_b_im"][l]
    bbr = fr[..., None] * br - fi[..., None] * bi
    bbi = fr[..., None] * bi + fi[..., None] * br
    in_map = lambda d: jnp.concatenate(
        [_block_diag(jnp.swapaxes(bbr[d], 1, 2)), _block_diag(jnp.swapaxes(bbi[d], 1, 2))], axis=1).astype(BF16)
    out_map = lambda d: jnp.concatenate(
        [_block_diag(jnp.swapaxes(w["s5_c_re"][l][d], 1, 2)),
         -_block_diag(jnp.swapaxes(w["s5_c_im"][l][d], 1, 2))], axis=0).astype(BF16)
    lw["s5_bf"], lw["s5_bb"] = in_map(0), in_map(1)
    lw["s5_cf"], lw["s5_cb"] = out_map(0), out_map(1)
    coef = jnp.stack([ar[0].reshape(-1), ai[0].reshape(-1), ar[1].reshape(-1), ai[1].reshape(-1)])
    lw["s5_a"] = jnp.broadcast_to(coef[:, None, :], (4, SUBLANES, S5_WIDTH))
    lw["s5_d"] = w["s5_d"][l][None]
    lw["wglu"] = w["s5_w_glu"][l].astype(BF16)

    lw["lru_cw"] = w["lru_conv_w"][l]
    lw["lru_cb"] = w["lru_conv_b"][l][None]
    gate_w = lambda d: jnp.concatenate(
        [_block_diag(w["lru_w_a"][l][d]), _block_diag(w["lru_w_x"][l][d])], axis=1).astype(BF16)
    gate_b = lambda d: jnp.concatenate([w["lru_b_a"][l][d], w["lru_b_x"][l][d]])[None]
    lw["lru_wf"], lw["lru_wb"] = gate_w(0), gate_w(1)
    lw["lru_bf"], lw["lru_bb"] = gate_b(0), gate_b(1)
    sp = jax.nn.softplus(-w["lru_lambda"][l])
    lw["lru_spf"], lw["lru_spb"] = sp[0][None], sp[1][None]

    wb = w["w_branch"][l]
    wb0 = wb[0].reshape(A_HEADS, A_HEAD_DIM, D_MODEL)[HEAD_ORDER].reshape(BRANCH_W, D_MODEL)
    lw["wb"] = jnp.concatenate([wb0[None], wb[1:]], axis=0).astype(BF16)
    lw["wo"] = w["w_out"][l].astype(BF16)

    wr = jnp.pad(jnp.concatenate([w["moe_w_group"][l], w["moe_w_expert"][l]], axis=1),
                 ((0, 0), (0, LANES - N_GROUPS - N_EXPERTS)))
    lw["wr_hi"] = wr.astype(BF16)
    lw["wr_lo"] = (wr - lw["wr_hi"].astype(F32)).astype(BF16)
    lw["br"] = jnp.pad(jnp.concatenate([w["moe_b_group"][l], w["moe_b_expert"][l]]),
                       (0, LANES - N_GROUPS - N_EXPERTS))[None]
    lw["tri"] = jnp.triu(jnp.ones((MOE_SORT_TILE, MOE_SORT_TILE), BF16))
    lw["wga"] = w["moe_w_gate"][l].astype(BF16)
    lw["wup"] = w["moe_w_up"][l].astype(BF16)
    lw["wdn"] = w["moe_w_down"][l].astype(BF16).reshape(N_EXPERTS * EXPERT_FF, D_MODEL)
    return lw


def _rows_to_state(fin, width):
    Bg, n = fin.shape[0], fin.shape[1]
    return jnp.swapaxes(fin, 1, 2).reshape(Bg * SUBLANES, n, width)


def _state_to_rows(state):
    B, n, width = state.shape
    return jnp.swapaxes(state.reshape(B // SUBLANES, SUBLANES, n, width), 1, 2)


def _mix(x, mods, lw, rope_tabs, cache, s5_h0, lru_h0, ctx):
    sh1, sc1, gt1 = mods
    outs = _in_proj(x, sc1, sh1, lw, rope_tabs, ctx)
    q, k, v, s5u_t, lx_t, lg, qm, km, vm = outs[:9]
    if ctx:
        o_a = _attn_a_ctx(q, k, v, lw["sink"][lw["layer"]])
        o_d = _mla_ctx(qm, km, vm)
    else:
        ck, cv, cckv, ckr = cache
        o_a = _attn_a_lat(q, k, v, ck, cv, lw["sink"][lw["layer"]])
        kc, vc = _mla_cache_kv(cckv, ckr, lw)
        o_d = _mla_lat(qm, km, vm, kc, vc)
    yf, yb, s5_fin = _s5_scan(s5u_t, lw, s5_h0)
    hf, hb, lru_fin = _lru_scan(lx_t, lw, lru_h0)
    x = _merge(x, sc1, sh1, gt1, lw, o_a, s5u_t, yf, yb, hf, hb, lg, o_d)
    return x, outs[9:], s5_fin, lru_fin


def kernel(x_prompt, x_sample, cache_attn_k, cache_attn_v, cache_mla_ckv, cache_mla_krope, state_ssm_re, state_ssm_im, state_lru, c, c_ctx, w_mod, b_mod, norm1_g, norm2_g, w_in, a_qnorm_g, a_knorm_g, a_sink, s5_lambda_re, s5_lambda_im, s5_log_step, s5_b_re, s5_b_im, s5_c_re, s5_c_im, s5_d, s5_w_glu, lru_conv_w, lru_conv_b, lru_w_a, lru_b_a, lru_w_x, lru_b_x, lru_lambda, mla_q_lat_norm, mla_w_uq, mla_kv_norm, mla_w_ukv, mla_qnorm_g, mla_knorm_g, w_branch, w_out, moe_w_group, moe_b_group, moe_w_expert, moe_b_expert, moe_w_gate, moe_w_up, moe_w_down):
    w = dict(norm1_g=norm1_g, norm2_g=norm2_g, w_in=w_in, a_qnorm_g=a_qnorm_g, a_knorm_g=a_knorm_g,
             a_sink=a_sink, s5_lambda_re=s5_lambda_re, s5_lambda_im=s5_lambda_im, s5_log_step=s5_log_step,
             s5_b_re=s5_b_re, s5_b_im=s5_b_im, s5_c_re=s5_c_re, s5_c_im=s5_c_im, s5_d=s5_d, s5_w_glu=s5_w_glu,
             lru_conv_w=lru_conv_w, lru_conv_b=lru_conv_b, lru_w_a=lru_w_a, lru_b_a=lru_b_a, lru_w_x=lru_w_x,
             lru_b_x=lru_b_x, lru_lambda=lru_lambda, mla_q_lat_norm=mla_q_lat_norm, mla_w_uq=mla_w_uq,
             mla_kv_norm=mla_kv_norm, mla_w_ukv=mla_w_ukv, mla_qnorm_g=mla_qnorm_g, mla_knorm_g=mla_knorm_g,
             w_branch=w_branch, w_out=w_out, moe_w_group=moe_w_group, moe_b_group=moe_b_group,
             moe_w_expert=moe_w_expert, moe_b_expert=moe_b_expert, moe_w_gate=moe_w_gate, moe_w_up=moe_w_up,
             moe_w_down=moe_w_down)
    B, L, _ = x_prompt.shape
    Bd, Ld, _ = x_sample.shape
    P = cache_attn_k.shape[2]

    cond = jnp.zeros((2 * SUBLANES, D_MODEL), F32).at[:Bd].set(c).at[Bd].set(c_ctx)
    mod = _modulation(cond, w_mod, b_mod[:, None, :])
    mod = mod.reshape(DEPTH, 2 * SUBLANES, 6, D_MODEL)

    rope_a = _rope_tables(Ld, A_HEAD_DIM, A_HEAD_DIM, 0)
    rope_a = tuple(jnp.asarray(np.tile(t, (1, A_HEADS))) for t in rope_a)
    rope_m = tuple(jnp.asarray(t) for t in _rope_tables(Ld, MLA_ROPE, LANES, MLA_NOPE))

    lw_all = jax.vmap(lambda w1: _layer_weights(0, {name: v[None] for name, v in w1.items()}))(w)

    xp, xs = x_prompt, x_sample
    ak_l, av_l, ckv_l, kr_l, sr_l, si_l, lru_l = [], [], [], [], [], [], []
    for l in range(DEPTH):
        lw = dict(lw_all, layer=l)
        lat_mod = [mod[l, :Bd, n][:, None, :] for n in range(6)]
        ctx_mod = [mod[l, Bd:Bd + 1, n][:, None, :] for n in range(6)]

        zs5 = jnp.zeros((B // SUBLANES, 4, SUBLANES, S5_WIDTH), F32)
        zlru = jnp.zeros((B // SUBLANES, 2, SUBLANES, LRU_WIDTH), F32)
        xp, (k32, v32, ckv_n, krp), s5_fin, lru_fin = _mix(xp, ctx_mod[0:3], lw, None, None, zs5, zlru, True)
        xp = _moe(xp, ctx_mod[4], ctx_mod[3], ctx_mod[5], lw)
        ak_l.append(k32.reshape(B, L, A_KV_HEADS, A_HEAD_DIM))
        av_l.append(v32.reshape(B, L, A_KV_HEADS, A_HEAD_DIM))
        ckv_l.append(ckv_n)
        kr_l.append(krp[:, :, MLA_NOPE:MLA_QK])
        fin = _rows_to_state(s5_fin, S5_WIDTH)
        sr_l.append(fin[:, 0::2].reshape(B, 2, S5_GROUPS, S5_STATE))
        si_l.append(fin[:, 1::2].reshape(B, 2, S5_GROUPS, S5_STATE))
        lru_l.append(_rows_to_state(lru_fin, LRU_WIDTH))

        sre = state_ssm_re[:, l].reshape(Bd, 2, S5_WIDTH)
        sim = state_ssm_im[:, l].reshape(Bd, 2, S5_WIDTH)
        s5_h0 = _state_to_rows(jnp.stack([sre[:, 0], sim[:, 0], sre[:, 1], sim[:, 1]], axis=1))
        lru_h0 = _state_to_rows(state_lru[:, l])
        cache = (cache_attn_k[:, l].reshape(Bd, P, 128).astype(BF16),
                 cache_attn_v[:, l].reshape(Bd, P, 128).astype(BF16),
                 cache_mla_ckv[:, l],
                 jnp.pad(cache_mla_krope[:, l], ((0, 0), (0, 0), (MLA_NOPE, LANES - MLA_QK))))
        xs, _, _, _ = _mix(xs, lat_mod[0:3], lw, rope_a + rope_m, cache, s5_h0, lru_h0, False)
        xs = _moe(xs, lat_mod[4], lat_mod[3], lat_mod[5], lw)

    stack = lambda ts: jnp.stack(ts, axis=1)
    return (xp, xs, stack(ak_l), stack(av_l), stack(ckv_l), stack(kr_l), stack(sr_l), stack(si_l), stack(lru_l))
```

```python
import functools
import math

import jax
import jax.numpy as jnp
import numpy as np
from jax import lax
from jax.experimental import pallas as pl
from jax.experimental.pallas import tpu as pltpu

F32 = jnp.float32
BF16 = jnp.bfloat16

D_MODEL = 1024
DEPTH = 2
GRID_W = 64
N_BRANCH = 4
BRANCH_W = 256
ROPE_BASE = 10000.0
EPS = 1e-6
NEG_INF = -1e30
LOG2_E = math.log2(math.e)
A_HEADS = 4
A_KV_HEADS = 2
A_HEAD_DIM = 64
WINDOW = 128
Q_BLOCK = 128
S5_GROUP = 16
S5_GROUPS = 16
S5_STATE = 64
S5_WIDTH = S5_GROUPS * S5_STATE
LRU_WIDTH = 256
LRU_BLOCKS = 4
LRU_CONV = 4
LRU_C = 8.0
MLA_HEADS = 4
MLA_Q_LORA = 256
MLA_KV_LORA = 128
MLA_NOPE = 64
MLA_ROPE = 32
MLA_V = 64
MLA_QK = MLA_NOPE + MLA_ROPE
N_GROUPS = 4
EXPERTS_PER_GROUP = 4
N_EXPERTS = 16
EXPERT_FF = 256

LANES = 128
SUBLANES = 8
VMEM_LIMIT = 56 * 1024 * 1024

TOK_TILE = 256
TOK_STEPS = TOK_TILE // SUBLANES
SCAN_STEPS = 128
MOE_SORT_TILE = 256
MOE_ROW_ALIGN = 16
MOE_SORTED_ROWS = MOE_SORT_TILE + LANES
MOE_SORT_SUBTILES = 8
MOE_UNSORT_SUBTILES = 4
MOE_BLOCK_TILES = 8
MOE_CHUNK = 512
MLA_Q_TILE = 256
ATTN_Q_TILE = 256
CTX_SEQS_PER_STEP = 4
Z_COLS = 1792
HEAD_ORDER = np.array((0, 2, 1, 3))


def _params(sem):
    return pltpu.CompilerParams(dimension_semantics=sem, vmem_limit_bytes=VMEM_LIMIT)


def _layer_spec(layer):
    def spec(shape):
        n = len(shape)
        return pl.BlockSpec((None,) + tuple(shape), lambda *_: (layer,) + (0,) * n)
    return spec


def _modnorm(x, g, sc, sh):
    ms = jnp.mean(x * x, axis=-1, keepdims=True)
    return (x * lax.rsqrt(ms + EPS)) * g * (1.0 + sc) + sh


def _rmsnorm(x, g):
    ms = jnp.mean(x * x, axis=-1, keepdims=True)
    return (x * lax.rsqrt(ms + EPS)) * g


def _segment_ones(width, segment):
    lane = np.arange(width) // segment
    return jnp.asarray(lane[:, None] == lane[None, :], BF16)


def _segment_rsqrt(x, ones, n_real):
    if isinstance(ones, int):
        segment = ones
        lane = lax.broadcasted_iota(jnp.int32, (1, LANES), 1)
        outs = []
        for t in range(x.shape[1] // LANES):
            xt = x[:, LANES * t:LANES * (t + 1)]
            sq = xt * xt
            inv = None
            for s in range(LANES // segment):
                mask = (lane >= s * segment) & (lane < (s + 1) * segment)
                part = sq if segment == LANES else jnp.where(mask, sq, 0.0)
                r = lax.rsqrt(jnp.sum(part, axis=-1, keepdims=True) * (1.0 / n_real) + EPS)
                inv = r if inv is None else jnp.where(mask, r, inv)
            outs.append(jnp.broadcast_to(inv, sq.shape))
        return jnp.concatenate(outs, axis=1)
    sq = x * x
    hi = sq.astype(BF16)
    lo = (sq - hi.astype(F32)).astype(BF16)
    ss = jnp.dot(hi, ones, preferred_element_type=F32) + jnp.dot(lo, ones, preferred_element_type=F32)
    return lax.rsqrt(ss * (1.0 / n_real) + EPS)


def _rope(x, cos, sin_next, sin_prev, quarter):
    width = x.shape[1]
    return (x * cos + pltpu.roll(x, width - quarter, 1) * sin_next
            + pltpu.roll(x, quarter, 1) * sin_prev)


def _mla_kv(ckv_n, kr_placed, wukv, gk, ones_tile, rope_tabs):
    kv = jnp.dot(ckv_n.astype(BF16), wukv, preferred_element_type=F32)
    gain = gk[:, :LANES]
    kfull = kv[:, :MLA_HEADS * LANES] + jnp.concatenate([kr_placed] * MLA_HEADS, axis=1)
    inv_all = _segment_rsqrt(kfull, ones_tile, MLA_QK)
    rotated = None
    if rope_tabs is not None:
        cos, sin_next, sin_prev = rope_tabs
        quarter = MLA_ROPE // 4
        base = kr_placed * gain
        rotated = pltpu.roll(base, LANES - quarter, 1) * sin_next + pltpu.roll(base, quarter, 1) * sin_prev
    heads = []
    for h in range(MLA_HEADS):
        kf = kfull[:, LANES * h:LANES * (h + 1)]
        inv = inv_all[:, LANES * h:LANES * (h + 1)]
        kh = kf * gain
        if rotated is not None:
            kh = kh * cos + rotated
        heads.append(kh * inv)
    return jnp.concatenate(heads, axis=1), kv[:, MLA_HEADS * LANES:]


def _mod_kernel(c_ref, w_ref, b_ref, o_ref):
    c = c_ref[...]
    s = c * jax.nn.sigmoid(c)
    o_ref[...] = jnp.dot(s.astype(BF16), w_ref[...].astype(BF16), preferred_element_type=F32) + b_ref[...]


def _modulation(cond, w_mod, b_mod):
    n_rows = cond.shape[0]
    n_out = w_mod.shape[-1]
    tn = 1024
    return pl.pallas_call(
        _mod_kernel,
        grid=(DEPTH, n_out // tn),
        in_specs=[pl.BlockSpec((n_rows, D_MODEL), lambda l, j: (0, 0)),
                  pl.BlockSpec((None, D_MODEL, tn), lambda l, j: (l, 0, j)),
                  pl.BlockSpec((None, 1, tn), lambda l, j: (l, 0, j))],
        out_specs=pl.BlockSpec((None, n_rows, tn), lambda l, j: (l, 0, j)),
        out_shape=jax.ShapeDtypeStruct((DEPTH, n_rows, n_out), F32),
        compiler_params=_params(("arbitrary", "arbitrary")),
        name="modulation",
    )(cond, w_mod, b_mod)


def _group_rows(ref):
    v = ref[...]
    return v.reshape(v.shape[0] * v.shape[1], v.shape[2])


def _mod_rows(ref, steps):
    v = ref[...]
    if v.shape[0] == 1:
        return v[0]
    return jnp.broadcast_to(v, (v.shape[0], steps, v.shape[2])).reshape(v.shape[0] * steps, v.shape[2])


def _store_group(ref, val):
    ref[...] = val.reshape(ref.shape).astype(ref.dtype)


def _store_time_major(ref, val):
    steps = ref.shape[1] // SUBLANES
    for b in range(SUBLANES):
        for half in range(2):
            ref[half, pl.ds(b, steps, stride=SUBLANES), :] = (
                val[b * steps:(b + 1) * steps, LANES * half:LANES * (half + 1)])


def _load_time_major(ref):
    steps = ref.shape[1] // SUBLANES
    return jnp.concatenate(
        [jnp.concatenate([ref[half, pl.ds(b, steps, stride=SUBLANES), :] for half in range(2)], axis=1)
         for b in range(SUBLANES)], axis=0)


def _in_proj_kernel(rope, ctx, *refs):
    it = iter(refs)
    x_ref, sc_ref, sh_ref, g1_ref, wz_ref = (next(it) for _ in range(5))
    gq_ref, gk_ref, qlg_ref, wuq_ref, gmq_ref, kvg_ref, wukv_ref, gmk_ref = (next(it) for _ in range(8))
    oh_ref, ot_ref = next(it), next(it)
    seg_heads = oh_ref[...] if rope else A_HEAD_DIM
    seg_tiles = ot_ref[...] if rope else LANES
    if rope:
        ca_ref, sna_ref, spa_ref, cm_ref, snm_ref, spm_ref = (next(it) for _ in range(6))
    q_ref, k_ref, v_ref, s5u_ref, lx_ref, lg_ref, qm_ref, km_ref, vm_ref = (next(it) for _ in range(9))
    if ctx:
        k32_ref, v32_ref, ckv_ref, krp_ref = (next(it) for _ in range(4))

    steps = x_ref.shape[1]
    h = _modnorm(_group_rows(x_ref), g1_ref[...], _mod_rows(sc_ref, steps), _mod_rows(sh_ref, steps))
    z = jnp.dot(h.astype(BF16), wz_ref[...], preferred_element_type=F32)
    per_seq = lambda ref: jnp.concatenate([ref[...]] * SUBLANES, axis=0)

    qk = z[:, 0:384]
    qk = qk * _segment_rsqrt(qk, seg_heads, A_HEAD_DIM)
    q = qk[:, 0:256] * gq_ref[...]
    k = qk[:, 256:384] * gk_ref[...]
    v = z[:, 384:512]
    if ctx:
        _store_group(k32_ref, k)
        _store_group(v32_ref, v)
    if rope:
        quarter = A_HEAD_DIM // 4
        atabs = (per_seq(ca_ref), per_seq(sna_ref), per_seq(spa_ref))
        q = _rope(q, *atabs, quarter)
        k = _rope(k, *(t[:, :LANES] for t in atabs), quarter)
    _store_group(q_ref, q)
    _store_group(k_ref, k)
    _store_group(v_ref, v)

    _store_time_major(s5u_ref, z[:, 512:768])
    _store_time_major(lx_ref, z[:, 768:1024])
    _store_group(lg_ref, z[:, 1024:1280])

    mtabs = mtabs4 = None
    if rope:
        mtabs = tuple(per_seq(r) for r in (cm_ref, snm_ref, spm_ref))
        mtabs4 = tuple(jnp.concatenate([t] * MLA_HEADS, axis=1) for t in mtabs)
    ql = _rmsnorm(z[:, 1280:1536], qlg_ref[...])
    qm = jnp.dot(ql.astype(BF16), wuq_ref[...], preferred_element_type=F32)
    qm = qm * _segment_rsqrt(qm, seg_tiles, MLA_QK) * gmq_ref[...]
    if rope:
        qm = _rope(qm, *mtabs4, MLA_ROPE // 4)
    _store_group(qm_ref, qm)

    ckv_n = _rmsnorm(z[:, 1536:1664], kvg_ref[...])
    krp = z[:, 1664:1792]
    km, vm = _mla_kv(ckv_n, krp, wukv_ref[...], gmk_ref[...], seg_tiles, mtabs)
    _store_group(km_ref, km)
    _store_group(vm_ref, vm)
    if ctx:
        _store_group(ckv_ref, ckv_n)
        _store_group(krp_ref, krp)


def _group_specs(per_seq_mod):
    TS = TOK_STEPS
    tok = lambda c: pl.BlockSpec((SUBLANES, TS, c), lambda g, i: (g, i, 0))
    scan = pl.BlockSpec((None, 2, TS * SUBLANES, LANES), lambda g, i: (g, 0, i, 0))
    if per_seq_mod:
        mod = pl.BlockSpec((SUBLANES, 1, D_MODEL), lambda g, i: (g, 0, 0))
    else:
        mod = pl.BlockSpec((1, 1, D_MODEL), lambda g, i: (0, 0, 0))
    return tok, scan, mod


def _in_proj(x, sc, sh, lw, rope_tabs, ctx):
    wspec = _layer_spec(lw["layer"])
    B, L, _ = x.shape
    Bg = B // SUBLANES
    TS = TOK_STEPS
    rope = rope_tabs is not None
    tok, scan, mod = _group_specs(sc.shape[0] > 1)
    in_specs = [tok(D_MODEL), mod, mod, wspec((1, D_MODEL)), wspec((D_MODEL, Z_COLS)),
                wspec((1, 256)), wspec((1, 128)), wspec((1, 256)), wspec((256, 512)), wspec((1, 512)),
                wspec((1, 128)), wspec((128, 768)), wspec((1, 512)),
                pl.BlockSpec((384, 384), lambda g, i: (0, 0)), pl.BlockSpec((512, 512), lambda g, i: (0, 0))]
    args = [x, sc, sh, lw["g1"], lw["wz"], lw["gq"], lw["gk"], lw["qlg"], lw["wuq"], lw["gmq"],
            lw["kvg"], lw["wukv"], lw["gmk"], _segment_ones(384, A_HEAD_DIM), _segment_ones(MLA_HEADS * LANES, LANES)]
    if rope:
        in_specs += [pl.BlockSpec((TS, 256), lambda g, i: (i, 0))] * 3
        in_specs += [pl.BlockSpec((TS, 128), lambda g, i: (i, 0))] * 3
        args += list(rope_tabs)
    out_specs = [tok(256), tok(128), tok(128), scan, scan, tok(256), tok(512), tok(512), tok(256)]
    sds = jax.ShapeDtypeStruct
    scan_shape = sds((Bg, 2, L * SUBLANES, LANES), F32)
    out_shape = [sds((B, L, 256), BF16), sds((B, L, 128), BF16), sds((B, L, 128), BF16),
                 scan_shape, scan_shape,
                 sds((B, L, 256), F32), sds((B, L, 512), BF16), sds((B, L, 512), BF16),
                 sds((B, L, 256), BF16)]
    if ctx:
        out_specs += [tok(128), tok(128), tok(128), tok(128)]
        out_shape += [sds((B, L, 128), F32)] * 4
    return pl.pallas_call(
        functools.partial(_in_proj_kernel, rope, ctx),
        grid=(Bg, L // TS),
        in_specs=in_specs, out_specs=out_specs, out_shape=out_shape,
        compiler_params=_params(("parallel", "parallel")),
        name="in_proj_ctx" if ctx else "in_proj_lat",
    )(*args)


def _mla_cache_kernel(ckv_ref, krp_ref, wukv_ref, gmk_ref, ot_ref, km_ref, vm_ref):
    km, vm = _mla_kv(ckv_ref[...], krp_ref[...], wukv_ref[...], gmk_ref[...], ot_ref[...], None)
    km_ref[...] = km.astype(BF16)
    vm_ref[...] = vm.astype(BF16)


def _mla_cache_kv(cckv, ckr_placed, lw):
    wspec = _layer_spec(lw["layer"])
    B, P, _ = cckv.shape
    tok = lambda c: pl.BlockSpec((None, P, c), lambda b: (b, 0, 0))
    return pl.pallas_call(
        _mla_cache_kernel,
        grid=(B,),
        in_specs=[tok(128), tok(128), wspec((128, 768)), wspec((1, 512)),
                  pl.BlockSpec((512, 512), lambda b: (0, 0))],
        out_specs=[tok(512), tok(256)],
        out_shape=[jax.ShapeDtypeStruct((B, P, 512), BF16), jax.ShapeDtypeStruct((B, P, 256), BF16)],
        compiler_params=_params(("parallel",)),
        name="mla_cache_kv",
    )(cckv, ckr_placed, lw["wukv"], lw["gmk"], _segment_ones(MLA_HEADS * LANES, LANES))


def _sink_softmax(s, sink):
    sink = sink * LOG2_E
    m = jnp.maximum(jnp.max(s, axis=-1, keepdims=True), sink)
    e = jnp.exp2(s - m)
    den = jnp.sum(e, axis=-1, keepdims=True) + jnp.exp2(sink - m)
    return e.astype(BF16), 1.0 / den


def _gqa_tile(qt, keys, vst, sink_lo, sink_hi, mask):
    lane = lax.broadcasted_iota(jnp.int32, (1, LANES), 1)
    lo = lane < A_HEAD_DIM
    zero = jnp.zeros_like(qt)
    ps, invs = [], []
    for qh, sink in ((jnp.where(lo, qt, zero), sink_lo), (jnp.where(lo, zero, qt), sink_hi)):
        s = lax.dot_general(qh, keys, (((1,), (1,)), ((), ())), preferred_element_type=F32)
        if mask is not None:
            s = jnp.where(mask, s, NEG_INF)
        p, inv = _sink_softmax(s, sink)
        ps.append(p)
        invs.append(inv)
    o = jnp.dot(jnp.concatenate(ps, axis=1), vst, preferred_element_type=F32)
    return o * jnp.where(lo, invs[0], invs[1])


def _stack_kv_halves(v):
    lane = lax.broadcasted_iota(jnp.int32, (1, LANES), 1)
    lo = lane < A_HEAD_DIM
    zero = jnp.zeros_like(v)
    return jnp.concatenate([jnp.where(lo, v, zero), jnp.where(lo, zero, v)], axis=0)


def _attn_a_ctx_kernel(sink_ref, q_ref, k_ref, v_ref, o_ref):
    for s in range(q_ref.shape[0]):
        keys = k_ref[s]
        vst = _stack_kv_halves(v_ref[s])
        for t in range(2):
            o = _gqa_tile(q_ref[s, :, LANES * t:LANES * (t + 1)], keys, vst, sink_ref[t], sink_ref[2 + t], None)
            o_ref[s, :, LANES * t:LANES * (t + 1)] = o.astype(o_ref.dtype)


def _attn_a_ctx(q, k, v, sink):
    B, L, _ = q.shape
    nb = CTX_SEQS_PER_STEP
    tok = lambda c: pl.BlockSpec((nb, L, c), lambda b: (b, 0, 0))
    return pl.pallas_call(
        _attn_a_ctx_kernel,
        grid=(B // nb,),
        in_specs=[pl.BlockSpec(memory_space=pltpu.SMEM), tok(256), tok(128), tok(128)],
        out_specs=tok(256),
        out_shape=jax.ShapeDtypeStruct((B, L, 256), BF16),
        compiler_params=_params(("parallel",)),
        name="attn_a_ctx",
    )(sink, q, k, v)


def _attn_a_lat_kernel(sink_ref, q_ref, kc_ref, vc_ref, k_ref, v_ref, o_ref):
    i = pl.program_id(1)
    L = k_ref.shape[0]
    n_ctx = kc_ref.shape[0]
    rows = q_ref.shape[0]
    span = rows + 2 * WINDOW
    start = pl.multiple_of(jnp.clip(i * rows - WINDOW, 0, L - span), WINDOW)
    keys = jnp.concatenate([kc_ref[...], k_ref[pl.ds(start, span), :]], axis=0)
    vals = jnp.concatenate([vc_ref[...], v_ref[pl.ds(start, span), :]], axis=0)
    vst = _stack_kv_halves(vals)
    col = lax.broadcasted_iota(jnp.int32, (rows, n_ctx + span), 1)
    row = lax.broadcasted_iota(jnp.int32, (rows, n_ctx + span), 0)
    rel = (col - n_ctx + start) - (row + i * rows)
    mask = (col < n_ctx) | (jnp.abs(rel) <= WINDOW)
    for t in range(2):
        o = _gqa_tile(q_ref[:, LANES * t:LANES * (t + 1)], keys, vst, sink_ref[t], sink_ref[2 + t], mask)
        o_ref[:, LANES * t:LANES * (t + 1)] = o.astype(o_ref.dtype)


def _attn_a_lat(q, k, v, kc, vc, sink):
    B, L, _ = q.shape
    P = kc.shape[1]
    blk = lambda c: pl.BlockSpec((None, ATTN_Q_TILE, c), lambda b, i: (b, i, 0))
    whole = lambda n, c: pl.BlockSpec((None, n, c), lambda b, i: (b, 0, 0))
    return pl.pallas_call(
        _attn_a_lat_kernel,
        grid=(B, L // ATTN_Q_TILE),
        in_specs=[pl.BlockSpec(memory_space=pltpu.SMEM), blk(256), whole(P, 128), whole(P, 128),
                  whole(L, 128), whole(L, 128)],
        out_specs=blk(256),
        out_shape=jax.ShapeDtypeStruct((B, L, 256), BF16),
        compiler_params=_params(("parallel", "parallel")),
        name="attn_a_lat",
    )(sink, q, kc, vc, k, v)


def _mla_stack_values(vals):
    lane = lax.broadcasted_iota(jnp.int32, (1, MLA_HEADS * MLA_V), 1)
    zero = jnp.zeros_like(vals)
    return jnp.concatenate([jnp.where((lane >= MLA_V * h) & (lane < MLA_V * (h + 1)), vals, zero)
                            for h in range(MLA_HEADS)], axis=0)


def _mla_attend(q, key_parts, vst):
    lane = lax.broadcasted_iota(jnp.int32, (1, MLA_HEADS * MLA_V), 1)
    ps = []
    inv = jnp.zeros((q.shape[0], MLA_HEADS * MLA_V), F32)
    for h in range(MLA_HEADS):
        qh = q[:, LANES * h:LANES * (h + 1)]
        ss = [lax.dot_general(qh, kp[:, LANES * h:LANES * (h + 1)], (((1,), (1,)), ((), ())),
                              preferred_element_type=F32) for kp in key_parts]
        m = functools.reduce(jnp.maximum, [jnp.max(s, axis=-1, keepdims=True) for s in ss])
        es = [jnp.exp2(s - m) for s in ss]
        den = functools.reduce(lambda a, b: a + b, [jnp.sum(e, axis=-1, keepdims=True) for e in es])
        ps += [e.astype(BF16) for e in es]
        inv = jnp.where((lane >= MLA_V * h) & (lane < MLA_V * (h + 1)), 1.0 / den, inv)
    return jnp.dot(jnp.concatenate(ps, axis=1), vst, preferred_element_type=F32) * inv


def _mla_ctx_kernel(q_ref, k_ref, v_ref, o_ref):
    for s in range(q_ref.shape[0]):
        o_ref[s] = _mla_attend(q_ref[s], [k_ref[s]], _mla_stack_values(v_ref[s])).astype(o_ref.dtype)


def _mla_ctx(q, k, v):
    B, L, _ = q.shape
    nb = CTX_SEQS_PER_STEP
    tok = lambda c: pl.BlockSpec((nb, L, c), lambda b: (b, 0, 0))
    return pl.pallas_call(
        _mla_ctx_kernel,
        grid=(B // nb,),
        in_specs=[tok(512), tok(512), tok(256)],
        out_specs=tok(256),
        out_shape=jax.ShapeDtypeStruct((B, L, 256), BF16),
        compiler_params=_params(("parallel",)),
        name="mla_ctx",
    )(q, k, v)


def _mla_lat_kernel(q_ref, kc_ref, vc_ref, k_ref, v_ref, o_ref):
    vst = _mla_stack_values(jnp.concatenate([vc_ref[...], v_ref[...]], axis=0))
    o = _mla_attend(q_ref[...], [kc_ref[...], k_ref[...]], vst)
    o_ref[...] = o.astype(o_ref.dtype)


def _mla_lat(q, k, v, kc, vc):
    B, L, _ = q.shape
    P = kc.shape[1]
    TQ = MLA_Q_TILE
    blk = lambda c: pl.BlockSpec((None, TQ, c), lambda b, i: (b, i, 0))
    whole = lambda n, c: pl.BlockSpec((None, n, c), lambda b, i: (b, 0, 0))
    return pl.pallas_call(
        _mla_lat_kernel,
        grid=(B, L // TQ),
        in_specs=[blk(512), whole(P, 512), whole(P, 256), whole(L, 512), whole(L, 256)],
        out_specs=blk(256),
        out_shape=jax.ShapeDtypeStruct((B, L, 256), BF16),
        compiler_params=_params(("parallel", "parallel")),
        name="mla_lat",
    )(q, kc, vc, k, v)


def _join_halves(ref):
    return jnp.concatenate([ref[0], ref[1]], axis=1)


def _split_halves(ref, val):
    ref[0] = val[:, :LANES]
    ref[1] = val[:, LANES:]


def _s5_kernel(uf_ref, ub_ref, bf_ref, bb_ref, a_ref, h0_ref, cf_ref, cb_ref,
               yf_ref, yb_ref, fin_ref, s_ref, st_ref):
    i = pl.program_id(1)
    steps = SCAN_STEPS

    @pl.when(i == 0)
    def _():
        st_ref[...] = h0_ref[...]

    uf = _join_halves(uf_ref).astype(BF16)
    ub = _join_halves(ub_ref).astype(BF16)
    s_ref[0] = jnp.dot(uf, bf_ref[:, :S5_WIDTH], preferred_element_type=F32)
    s_ref[1] = jnp.dot(uf, bf_ref[:, S5_WIDTH:], preferred_element_type=F32)
    s_ref[2] = jnp.dot(ub, bb_ref[:, :S5_WIDTH], preferred_element_type=F32)
    s_ref[3] = jnp.dot(ub, bb_ref[:, S5_WIDTH:], preferred_element_type=F32)

    lane_chunk = 2 * LANES
    for c in range(S5_WIDTH // lane_chunk):
        sl = slice(lane_chunk * c, lane_chunk * (c + 1))
        arf, aif, arb, aib = (a_ref[n, :, sl] for n in range(4))

        def body(j, carry):
            hrf, hif, hrb, hib = carry
            rf = pl.multiple_of(j * SUBLANES, SUBLANES)
            rb = pl.multiple_of((steps - 1 - j) * SUBLANES, SUBLANES)
            nrf = arf * hrf - aif * hif + s_ref[0, pl.ds(rf, SUBLANES), sl]
            nif = arf * hif + aif * hrf + s_ref[1, pl.ds(rf, SUBLANES), sl]
            nrb = arb * hrb - aib * hib + s_ref[2, pl.ds(rb, SUBLANES), sl]
            nib = arb * hib + aib * hrb + s_ref[3, pl.ds(rb, SUBLANES), sl]
            s_ref[0, pl.ds(rf, SUBLANES), sl] = nrf
            s_ref[1, pl.ds(rf, SUBLANES), sl] = nif
            s_ref[2, pl.ds(rb, SUBLANES), sl] = nrb
            s_ref[3, pl.ds(rb, SUBLANES), sl] = nib
            return nrf, nif, nrb, nib

        fin = lax.fori_loop(0, steps, body, tuple(st_ref[n, :, sl] for n in range(4)), unroll=8)
        for n in range(4):
            st_ref[n, :, sl] = fin[n]

    hf = jnp.concatenate([s_ref[0].astype(BF16), s_ref[1].astype(BF16)], axis=1)
    _split_halves(yf_ref, jnp.dot(hf, cf_ref[...], preferred_element_type=F32))
    hb = jnp.concatenate([s_ref[2].astype(BF16), s_ref[3].astype(BF16)], axis=1)
    _split_halves(yb_ref, jnp.dot(hb, cb_ref[...], preferred_element_type=F32))
    fin_ref[...] = st_ref[...]


def _s5_scan(u_rows, lw, h0):
    wspec = _layer_spec(lw["layer"])
    Bg, _, rows, _ = u_rows.shape
    R = SCAN_STEPS * SUBLANES
    n = rows // R
    fwd = pl.BlockSpec((None, 2, R, LANES), lambda g, i: (g, 0, i, 0))
    bwd = pl.BlockSpec((None, 2, R, LANES), lambda g, i: (g, 0, n - 1 - i, 0))
    st = pl.BlockSpec((None, 4, SUBLANES, S5_WIDTH), lambda g, i: (g, 0, 0, 0))
    return pl.pallas_call(
        _s5_kernel,
        grid=(Bg, n),
        in_specs=[fwd, bwd, wspec((256, 2 * S5_WIDTH)), wspec((256, 2 * S5_WIDTH)),
                  wspec((4, SUBLANES, S5_WIDTH)), st, wspec((2 * S5_WIDTH, 256)), wspec((2 * S5_WIDTH, 256))],
        out_specs=[fwd, bwd, st],
        out_shape=[jax.ShapeDtypeStruct(u_rows.shape, F32), jax.ShapeDtypeStruct(u_rows.shape, F32),
                   jax.ShapeDtypeStruct((Bg, 4, SUBLANES, S5_WIDTH), F32)],
        scratch_shapes=[pltpu.VMEM((4, R, S5_WIDTH), F32), pltpu.VMEM((4, SUBLANES, S5_WIDTH), F32)],
        compiler_params=_params(("parallel", "arbitrary")),
        name="s5_scan",
    )(u_rows, u_rows, lw["s5_bf"], lw["s5_bb"], lw["s5_a"], h0, lw["s5_cf"], lw["s5_cb"])


def _lru_gates(x_ref, pre_ref, post_ref, has_pre, has_post, cw_ref, cb_ref, w_ref, b_ref, sp_ref, a_ref, h_ref):
    R = x_ref.shape[1]
    pre = jnp.where(has_pre, _join_halves(pre_ref), 0.0)
    post = jnp.where(has_post, _join_halves(post_ref), 0.0)
    xp = jnp.concatenate([pre, _join_halves(x_ref), post], axis=0)
    xc = cb_ref[...]
    for t in range(LRU_CONV):
        xc = xc + xp[SUBLANES * t:SUBLANES * t + R] * cw_ref[t:t + 1, :]
    g = jnp.dot(xc.astype(BF16), w_ref[...], preferred_element_type=F32) + b_ref[...]
    r = jax.nn.sigmoid(g[:, :LRU_WIDTH])
    ig = jax.nn.sigmoid(g[:, LRU_WIDTH:])
    log_a = (-LRU_C) * r * sp_ref[...]
    a = jnp.exp(log_a)
    a_ref[...] = a
    h_ref[...] = jnp.sqrt(1.0 - a * a) * (ig * xc)


def _lru_kernel(xf_ref, xfp_ref, xfn_ref, xb_ref, xbp_ref, xbn_ref, cw_ref, cb_ref, wf_ref, wb_ref,
                bf_ref, bb_ref, spf_ref, spb_ref, h0_ref, of_ref, ob_ref, fin_ref,
                af_ref, ab_ref, hf_ref, hb_ref, st_ref):
    i = pl.program_id(1)
    n = pl.num_programs(1)
    steps = SCAN_STEPS

    @pl.when(i == 0)
    def _():
        st_ref[...] = h0_ref[...]

    _lru_gates(xf_ref, xfp_ref, xfn_ref, i > 0, i < n - 1, cw_ref, cb_ref, wf_ref, bf_ref, spf_ref, af_ref, hf_ref)
    _lru_gates(xb_ref, xbp_ref, xbn_ref, i < n - 1, i > 0, cw_ref, cb_ref, wb_ref, bb_ref, spb_ref, ab_ref, hb_ref)

    def body(j, carry):
        hf, hb = carry
        rf = pl.multiple_of(j * SUBLANES, SUBLANES)
        rb = pl.multiple_of((steps - 1 - j) * SUBLANES, SUBLANES)
        nf = af_ref[pl.ds(rf, SUBLANES), :] * hf + hf_ref[pl.ds(rf, SUBLANES), :]
        nb = ab_ref[pl.ds(rb, SUBLANES), :] * hb + hb_ref[pl.ds(rb, SUBLANES), :]
        hf_ref[pl.ds(rf, SUBLANES), :] = nf
        hb_ref[pl.ds(rb, SUBLANES), :] = nb
        return nf, nb

    ff, fb = lax.fori_loop(0, steps, body, (st_ref[0], st_ref[1]), unroll=8)
    st_ref[0] = ff
    st_ref[1] = fb
    fin_ref[...] = st_ref[...]
    _split_halves(of_ref, hf_ref[...])
    _split_halves(ob_ref, hb_ref[...])


def _lru_scan(x_rows, lw, h0):
    wspec = _layer_spec(lw["layer"])
    Bg, _, rows, _ = x_rows.shape
    R = SCAN_STEPS * SUBLANES
    n = rows // R
    pre_rows = 2 * SUBLANES
    fwd = lambda g, i: i
    bwd = lambda g, i: n - 1 - i
    blk = lambda m: pl.BlockSpec((None, 2, R, LANES), lambda g, i: (g, 0, m(g, i), 0))
    pre = lambda m: pl.BlockSpec(
        (None, 2, pre_rows, LANES), lambda g, i: (g, 0, jnp.maximum(m(g, i) * (R // pre_rows) - 1, 0), 0))
    post = lambda m: pl.BlockSpec(
        (None, 2, SUBLANES, LANES),
        lambda g, i: (g, 0, jnp.minimum((m(g, i) + 1) * (R // SUBLANES), rows // SUBLANES - 1), 0))
    st = pl.BlockSpec((None, 2, SUBLANES, 256), lambda g, i: (g, 0, 0, 0))
    return pl.pallas_call(
        _lru_kernel,
        grid=(Bg, n),
        in_specs=[blk(fwd), pre(fwd), post(fwd), blk(bwd), pre(bwd), post(bwd),
                  wspec((LRU_CONV, 256)), wspec((1, 256)), wspec((256, 512)), wspec((256, 512)),
                  wspec((1, 512)), wspec((1, 512)), wspec((1, 256)), wspec((1, 256)), st],
        out_specs=[blk(fwd), blk(bwd), st],
        out_shape=[jax.ShapeDtypeStruct(x_rows.shape, F32), jax.ShapeDtypeStruct(x_rows.shape, F32),
                   jax.ShapeDtypeStruct((Bg, 2, SUBLANES, 256), F32)],
        scratch_shapes=[pltpu.VMEM((R, 256), F32), pltpu.VMEM((R, 256), F32), pltpu.VMEM((R, 256), F32),
                        pltpu.VMEM((R, 256), F32), pltpu.VMEM((2, SUBLANES, 256), F32)],
        compiler_params=_params(("parallel", "arbitrary")),
        name="lru_scan",
    )(x_rows, x_rows, x_rows, x_rows, x_rows, x_rows, lw["lru_cw"], lw["lru_cb"], lw["lru_wf"], lw["lru_wb"],
      lw["lru_bf"], lw["lru_bb"], lw["lru_spf"], lw["lru_spb"], h0)


def _merge_kernel(x_ref, sc_ref, sh_ref, gt_ref, g1_ref, wg_ref, oa_ref, u_ref, yf_ref, yb_ref, d_ref,
                  wglu_ref, hf_ref, hb_ref, lg_ref, od_ref, wb_ref, wo_ref, o_ref):
    steps = x_ref.shape[1]
    x = _group_rows(x_ref)
    h = _modnorm(x, g1_ref[...], _mod_rows(sc_ref, steps), _mod_rows(sh_ref, steps)).astype(BF16)

    yb5 = jax.nn.gelu(d_ref[...] * _load_time_major(u_ref) + _load_time_major(yf_ref) + _load_time_major(yb_ref))
    gv = jnp.dot(yb5.astype(BF16), wglu_ref[...], preferred_element_type=F32)
    o_b = gv[:, :BRANCH_W] * jax.nn.sigmoid(gv[:, BRANCH_W:])
    o_c = (_load_time_major(hf_ref) + _load_time_major(hb_ref)) * jax.nn.gelu(_group_rows(lg_ref))
    branches = (_group_rows(oa_ref), o_b.astype(BF16), o_c.astype(BF16), _group_rows(od_ref))

    acc = jnp.zeros(x.shape, F32)
    for n in range(N_BRANCH):
        gate = jnp.dot(h, wg_ref[:, D_MODEL * n:D_MODEL * (n + 1)], preferred_element_type=F32)
        proj = jnp.dot(branches[n], wb_ref[n], preferred_element_type=F32)
        acc = acc + jax.nn.sigmoid(gate) * proj
    out = jnp.dot(acc.astype(BF16), wo_ref[...], preferred_element_type=F32)
    _store_group(o_ref, x + _mod_rows(gt_ref, steps) * out)


def _merge(x, sc, sh, gt, lw, o_a, s5u_t, yf_t, yb_t, hf_t, hb_t, lg, o_d):
    wspec = _layer_spec(lw["layer"])
    B, L, _ = x.shape
    tok, scan, mod = _group_specs(sc.shape[0] > 1)
    return pl.pallas_call(
        _merge_kernel,
        grid=(B // SUBLANES, L // TOK_STEPS),
        in_specs=[tok(D_MODEL), mod, mod, mod, wspec((1, D_MODEL)), wspec((D_MODEL, N_BRANCH * D_MODEL)),
                  tok(256), scan, scan, scan, wspec((1, 256)), wspec((256, 512)),
                  scan, scan, tok(256), tok(256), wspec((N_BRANCH, BRANCH_W, D_MODEL)),
                  wspec((D_MODEL, D_MODEL))],
        out_specs=tok(D_MODEL),
        out_shape=jax.ShapeDtypeStruct((B, L, D_MODEL), F32),
        compiler_params=_params(("parallel", "parallel")),
        name="merge",
    )(x, sc, sh, gt, lw["g1"], lw["wg"], o_a, s5u_t, yf_t, yb_t, lw["s5_d"], lw["wglu"],
      hf_t, hb_t, lg, o_d, lw["wb"], lw["wo"])


def _first_index(values, target):
    idx = jnp.full_like(target, float(len(values) - 1))
    for n in range(len(values) - 2, -1, -1):
        idx = jnp.where(values[n] == target, float(n), idx)
    return idx


def _list_max(values):
    return functools.reduce(jnp.maximum, values)


def _moe_sort_kernel(x_ref, sc_ref, sh_ref, g2_ref, wrh_ref, wrl_ref, br_ref, tri_ref,
                     hs_ref, gs_ref, dcol_ref, meta_ref):
    for s in range(x_ref.shape[0] // MOE_SORT_TILE):
        tok = pl.ds(s * MOE_SORT_TILE, MOE_SORT_TILE)
        srt = pl.ds(s * MOE_SORTED_ROWS, MOE_SORTED_ROWS)
        _moe_sort_tile(x_ref.at[tok], sc_ref, sh_ref, g2_ref, wrh_ref, wrl_ref, br_ref, tri_ref,
                       hs_ref.at[srt], gs_ref.at[srt], dcol_ref.at[tok], meta_ref.at[s])


def _moe_sort_tile(x_ref, sc_ref, sh_ref, g2_ref, wrh_ref, wrl_ref, br_ref, tri_ref,
                   hs_ref, gs_ref, dcol_ref, meta_ref):
    T = MOE_SORT_TILE
    h = _modnorm(x_ref[...], g2_ref[...], sc_ref[...], sh_ref[...])
    hh = h.astype(BF16)
    hl = (h - hh.astype(F32)).astype(BF16)
    nt = (((1,), (1,)), ((), ()))
    logits = (jnp.dot(hh, wrh_ref[...], preferred_element_type=F32)
              + jnp.dot(hl, wrh_ref[...], preferred_element_type=F32)
              + jnp.dot(hh, wrl_ref[...], preferred_element_type=F32)) + br_ref[...]
    lt = logits.T
    gl = [lt[g:g + 1, :] for g in range(N_GROUPS)]
    gmax = _list_max(gl)
    g_idx = _first_index(gl, gmax)
    pg = 1.0 / sum(jnp.exp(v - gmax) for v in gl)
    hot = [g_idx == float(g) for g in range(N_GROUPS)]
    el = []
    for e in range(EXPERTS_PER_GROUP):
        v = jnp.zeros_like(gmax)
        for g in range(N_GROUPS):
            r = N_GROUPS + EXPERTS_PER_GROUP * g + e
            v = jnp.where(hot[g], lt[r:r + 1, :], v)
        el.append(v)
    emax = _list_max(el)
    ee = [jnp.exp(v - emax) for v in el]
    esum = sum(ee)
    pe = [v / esum for v in ee]
    v1 = _list_max(pe)
    i1 = _first_index(pe, v1)
    pe2 = [jnp.where(i1 == float(e), -1.0, pe[e]) for e in range(EXPERTS_PER_GROUP)]
    v2 = _list_max(pe2)
    i2 = _first_index(pe2, v2)
    tot = v1 + v2
    w = [jnp.where(i1 == float(e), pg * v1 / tot, jnp.where(i2 == float(e), pg * v2 / tot, 0.0))
         for e in range(EXPERTS_PER_GROUP)]

    zero_row = jnp.zeros_like(gmax)
    g8 = jnp.concatenate([jnp.where(hot[g], 1.0, 0.0) for g in range(N_GROUPS)] + [zero_row] * 4, axis=0)
    cum = jnp.dot(g8.astype(BF16), tri_ref[...], preferred_element_type=F32)
    off = jnp.zeros((1, 1), F32)
    dest = zero_row
    counts = []
    for g in range(N_GROUPS):
        cnt = cum[g:g + 1, T - 1:T]
        padded = jnp.floor((cnt + (MOE_ROW_ALIGN - 1.0)) * (1.0 / MOE_ROW_ALIGN)) * MOE_ROW_ALIGN
        dest = jnp.where(hot[g], off + cum[g:g + 1, :] - 1.0, dest)
        off = off + padded
        counts.append(padded)
    rows = lax.broadcasted_iota(jnp.int32, (MOE_SORTED_ROWS, T), 0).astype(F32)
    perm = jnp.where(rows == dest, 1.0, 0.0).astype(BF16)
    hs_ref[...] = jnp.dot(perm, hh, preferred_element_type=F32).astype(BF16)
    gates = jnp.concatenate(w + [jnp.zeros((LANES - EXPERTS_PER_GROUP, T), F32)], axis=0)
    ghi = gates.astype(BF16)
    glo = (gates - ghi.astype(F32)).astype(BF16)
    gs_ref[...] = (lax.dot_general(perm, ghi, nt, preferred_element_type=F32)
                   + lax.dot_general(perm, glo, nt, preferred_element_type=F32))
    r_i = lax.broadcasted_iota(jnp.int32, (T, T), 0)
    c_i = lax.broadcasted_iota(jnp.int32, (T, T), 1)
    dcol = jnp.sum(jnp.where(r_i == c_i, jnp.broadcast_to(dest, (T, T)), 0.0), axis=1, keepdims=True)
    dcol_ref[...] = jnp.broadcast_to(dcol, (T, LANES))
    meta_ref[...] = jnp.concatenate([jnp.broadcast_to(c, (1, LANES)) for c in counts]
                                    + [jnp.zeros((SUBLANES - N_GROUPS, LANES), F32)], axis=0)


def _moe_sort(xf, sc, sh, lw, tiles_per_mod):
    wspec = _layer_spec(lw["layer"])
    N = xf.shape[0]
    T = MOE_SORT_TILE
    n_tiles = N // T
    sub = MOE_SORT_SUBTILES
    per_b = sc.shape[0] > 1
    mod = pl.BlockSpec((None, 1, D_MODEL), lambda t: (t * sub // tiles_per_mod if per_b else 0, 0, 0))
    sds = jax.ShapeDtypeStruct
    return pl.pallas_call(
        _moe_sort_kernel,
        grid=(n_tiles // sub,),
        in_specs=[pl.BlockSpec((sub * T, D_MODEL), lambda t: (t, 0)), mod, mod, wspec((1, D_MODEL)),
                  wspec((D_MODEL, LANES)), wspec((D_MODEL, LANES)), wspec((1, LANES)), wspec((T, T))],
        out_specs=[pl.BlockSpec((sub * MOE_SORTED_ROWS, D_MODEL), lambda t: (t, 0)),
                   pl.BlockSpec((sub * MOE_SORTED_ROWS, LANES), lambda t: (t, 0)),
                   pl.BlockSpec((sub * T, LANES), lambda t: (t, 0)),
                   pl.BlockSpec((sub, SUBLANES, LANES), lambda t: (t, 0, 0))],
        out_shape=[sds((n_tiles * MOE_SORTED_ROWS, D_MODEL), BF16), sds((n_tiles * MOE_SORTED_ROWS, LANES), F32),
                   sds((N, LANES), F32), sds((n_tiles, SUBLANES, LANES), F32)],
        compiler_params=_params(("parallel",)),
        name="moe_sort",
    )(xf, sc, sh, lw["g2"], lw["wr_hi"], lw["wr_lo"], lw["br"], lw["tri"])


def _moe_expert_kernel(cnt_ref, hs_ref, gs_ref, wga_ref, wup_ref, wdn_ref, ys_ref, ch_ref, cg_ref, cy_ref):
    j = pl.program_id(0)
    g = pl.program_id(1)
    piece = MOE_ROW_ALIGN

    @pl.when((j == 0) & (g == 0))
    def _():
        ch_ref[...] = jnp.zeros_like(ch_ref)
        cg_ref[...] = jnp.zeros_like(cg_ref)

    @pl.when(g == 0)
    def _():
        ys_ref[...] = jnp.zeros_like(ys_ref)

    def for_each_piece(move):
        packed = jnp.int32(0)
        for t in range(MOE_BLOCK_TILES):
            base = (j * MOE_BLOCK_TILES + t) * N_GROUPS
            start = jnp.int32(t * MOE_SORTED_ROWS)
            for g2 in range(N_GROUPS - 1):
                start = start + jnp.where(g2 < g, cnt_ref[base + g2], 0)
            n = cnt_ref[base + g]

            def body(k, carry, start=start, packed=packed):
                move(pl.multiple_of(start + k * piece, piece), pl.multiple_of(packed + k * piece, piece))
                return carry

            lax.fori_loop(0, lax.shift_right_logical(n, 4), body, 0)
            packed = packed + n
        return packed

    def pack(src, dst):
        ch_ref[pl.ds(dst, piece), :] = hs_ref[pl.ds(src, piece), :]
        cg_ref[pl.ds(dst, piece), :] = gs_ref[pl.ds(src, piece), :]

    rows = for_each_piece(pack)

    def experts(r, size):
        h = ch_ref[pl.ds(r, size), :]
        acts = []
        for e in range(EXPERTS_PER_GROUP):
            a = jnp.dot(h, wga_ref[e], preferred_element_type=F32)
            u = jnp.dot(h, wup_ref[e], preferred_element_type=F32)
            acts.append(((a * jax.nn.sigmoid(a)) * u * cg_ref[pl.ds(r, size), e:e + 1]).astype(BF16))
        y = jnp.dot(jnp.concatenate(acts, axis=1), wdn_ref[...], preferred_element_type=F32)
        cy_ref[pl.ds(r, size), :] = y.astype(BF16)

    big, mid, small = MOE_CHUNK, MOE_CHUNK // 2, MOE_CHUNK // 4
    n_big = lax.shift_right_logical(rows, big.bit_length() - 1)

    def chunk(c, carry):
        experts(pl.multiple_of(c * big, big), big)
        return carry

    lax.fori_loop(0, n_big, chunk, 0)
    tail = pl.multiple_of(n_big * big, big)
    rem = rows - tail
    pl.when(rem > mid + small)(lambda: experts(tail, big))
    pl.when((rem > small) & (rem <= mid + small))(lambda: experts(tail, mid))
    pl.when((rem > mid) & (rem <= mid + small))(lambda: experts(pl.multiple_of(tail + mid, small), small))
    pl.when((rem > 0) & (rem <= small))(lambda: experts(tail, small))

    def unpack(src, dst):
        ys_ref[pl.ds(src, piece), :] = cy_ref[pl.ds(dst, piece), :]

    for_each_piece(unpack)


def _moe_experts(cnt, hs, gs, lw):
    rows = MOE_BLOCK_TILES * MOE_SORTED_ROWS
    cap = MOE_BLOCK_TILES * MOE_SORT_TILE
    blk = lambda c: pl.BlockSpec((rows, c), lambda j, g, cnt: (j, 0))
    layer = lw["layer"]
    wspec = lambda r, c: pl.BlockSpec((None, EXPERTS_PER_GROUP, r, c), lambda j, g, cnt: (layer, g, 0, 0))
    return pl.pallas_call(
        _moe_expert_kernel,
        grid_spec=pltpu.PrefetchScalarGridSpec(
            num_scalar_prefetch=1, grid=(hs.shape[0] // rows, N_GROUPS),
            in_specs=[blk(D_MODEL), blk(LANES), wspec(D_MODEL, EXPERT_FF), wspec(D_MODEL, EXPERT_FF),
                      pl.BlockSpec((None, EXPERTS_PER_GROUP * EXPERT_FF, D_MODEL),
                                   lambda j, g, cnt: (layer, g, 0))],
            out_specs=blk(D_MODEL),
            scratch_shapes=[pltpu.VMEM((cap, D_MODEL), BF16), pltpu.VMEM((cap, LANES), F32),
                            pltpu.VMEM((cap, D_MODEL), BF16)]),
        out_shape=jax.ShapeDtypeStruct(hs.shape, BF16),
        compiler_params=_params(("arbitrary", "arbitrary")),
        name="moe_experts",
    )(cnt, hs, gs, lw["wga"], lw["wup"], lw["wdn"])


def _moe_unsort_kernel(x_ref, gt_ref, ys_ref, dcol_ref, o_ref):
    cols = lax.broadcasted_iota(jnp.int32, (MOE_SORT_TILE, MOE_SORTED_ROWS), 1).astype(F32)
    for s in range(x_ref.shape[0] // MOE_SORT_TILE):
        tok = pl.ds(s * MOE_SORT_TILE, MOE_SORT_TILE)
        perm_t = jnp.where(cols == dcol_ref[tok, 0:1], 1.0, 0.0).astype(BF16)
        y = jnp.dot(perm_t, ys_ref[pl.ds(s * MOE_SORTED_ROWS, MOE_SORTED_ROWS), :], preferred_element_type=F32)
        o_ref[tok, :] = x_ref[tok, :] + gt_ref[...] * y


def _moe_unsort(xf, gt, ys, dcol, tiles_per_mod):
    N = xf.shape[0]
    sub = MOE_UNSORT_SUBTILES
    T = MOE_SORT_TILE * sub
    per_b = gt.shape[0] > 1
    mod = pl.BlockSpec((None, 1, D_MODEL), lambda t: (t * sub // tiles_per_mod if per_b else 0, 0, 0))
    tok = pl.BlockSpec((T, D_MODEL), lambda t: (t, 0))
    return pl.pallas_call(
        _moe_unsort_kernel,
        grid=(N // T,),
        in_specs=[tok, mod, pl.BlockSpec((sub * MOE_SORTED_ROWS, D_MODEL), lambda t: (t, 0)),
                  pl.BlockSpec((T, LANES), lambda t: (t, 0))],
        out_specs=tok,
        out_shape=jax.ShapeDtypeStruct((N, D_MODEL), F32),
        compiler_params=_params(("parallel",)),
        name="moe_unsort",
    )(xf, gt, ys, dcol)


def _moe(x, sc, sh, gt, lw):
    B, L, _ = x.shape
    N = B * L
    T = MOE_SORT_TILE
    xf = x.reshape(N, D_MODEL)
    hs, gs, dcol, meta = _moe_sort(xf, sc, sh, lw, L // T)

    cnt = meta[:, :N_GROUPS, 0].astype(jnp.int32).reshape(-1)
    ys = _moe_experts(cnt, hs, gs, lw)
    out = _moe_unsort(xf, gt, ys, dcol, L // T)
    return out.reshape(B, L, D_MODEL)


def _block_diag(blocks):
    n, r, c = blocks.shape
    eye = jnp.eye(n, dtype=blocks.dtype)
    return jnp.einsum("nrc,nm->nrmc", blocks, eye).reshape(n * r, n * c)


def _rope_tables(seq_len, dim, width, offset):
    rows = seq_len // GRID_W
    r, col = np.meshgrid(np.arange(rows), np.arange(GRID_W), indexing="ij")
    r = r.reshape(-1).astype(np.float64)
    col = col.reshape(-1).astype(np.float64)
    quarter = dim // 4
    freqs = ROPE_BASE ** (-np.arange(quarter, dtype=np.float64) / quarter)
    ang_r = r[:, None] * freqs
    ang_c = col[:, None] * freqs
    zero = np.zeros_like(ang_r)
    cos = np.cos(np.concatenate([ang_r, ang_r, ang_c, ang_c], axis=-1))
    sin_next = np.concatenate([-np.sin(ang_r), zero, -np.sin(ang_c), zero], axis=-1)
    sin_prev = np.concatenate([zero, np.sin(ang_r), zero, np.sin(ang_c)], axis=-1)

    def place(t, fill):
        return np.pad(t, ((0, 0), (offset, width - offset - dim)), constant_values=fill).astype(np.float32)

    return place(cos, 1.0), place(sin_next, 0.0), place(sin_prev, 0.0)


def _layer_weights(l, w):
    lw = {}
    w_in = w["w_in"][l]
    q_cols = w_in[:, 0:256].reshape(D_MODEL, A_HEADS, A_HEAD_DIM)[:, HEAD_ORDER].reshape(D_MODEL, 256)
    kr_cols = jnp.pad(w_in[:, 1664:1696], ((0, 0), (MLA_NOPE, LANES - MLA_QK)))
    lw["wz"] = jnp.concatenate([q_cols, w_in[:, 256:1664], kr_cols], axis=1).astype(BF16)
    lw["wg"] = w_in[:, 1696:].astype(BF16)
    lw["g1"] = w["norm1_g"][l][None]
    lw["g2"] = w["norm2_g"][l][None]
    lw["gq"] = jnp.tile(w["a_qnorm_g"][l], A_HEADS)[None] * (A_HEAD_DIM ** -0.5 * LOG2_E)
    lw["gk"] = jnp.tile(w["a_knorm_g"][l], A_KV_HEADS)[None]
    lw["sink"] = w["a_sink"][l]
    lw["qlg"] = w["mla_q_lat_norm"][l][None]
    wuq = w["mla_w_uq"][l].reshape(MLA_Q_LORA, MLA_HEADS, MLA_QK)
    lw["wuq"] = jnp.pad(wuq, ((0, 0), (0, 0), (0, LANES - MLA_QK))).reshape(MLA_Q_LORA, MLA_HEADS * LANES).astype(BF16)
    pad_g = lambda g: jnp.tile(jnp.pad(g, (0, LANES - MLA_QK)), MLA_HEADS)[None]
    lw["gmq"] = pad_g(w["mla_qnorm_g"][l]) * (MLA_QK ** -0.5 * LOG2_E)
    lw["gmk"] = pad_g(w["mla_knorm_g"][l])
    lw["kvg"] = w["mla_kv_norm"][l][None]
    wukv = w["mla_w_ukv"][l].reshape(MLA_KV_LORA, MLA_HEADS, MLA_NOPE + MLA_V)
    wk = jnp.pad(wukv[:, :, :MLA_NOPE], ((0, 0), (0, 0), (0, LANES - MLA_NOPE))).reshape(MLA_KV_LORA, MLA_HEADS * LANES)
    wv = wukv[:, :, MLA_NOPE:].reshape(MLA_KV_LORA, MLA_HEADS * MLA_V)
    lw["wukv"] = jnp.concatenate([wk, wv], axis=1).astype(BF16)

    lre = w["s5_lambda_re"][l]
    lim = w["s5_lambda_im"][l]
    dt = jnp.exp(w["s5_log_step"][l])[:, :, None]
    mag = jnp.exp(lre * dt)
    ar, ai = mag * jnp.cos(lim * dt), mag * jnp.sin(lim * dt)
    den = lre * lre + lim * lim
    fr = ((ar - 1.0) * lre + ai * lim) / den
    fi = (ai * lre - (ar - 1.0) * lim) / den
    br, bi = w["s5_b_re"][l], w["s5_b_im"][l]
    bbr = fr[..., None] * br - fi[..., None] * bi
    bbi = fr[..., None] * bi + fi[..., None] * br
    in_map = lambda d: jnp.concatenate(
        [_block_diag(jnp.swapaxes(bbr[d], 1, 2)), _block_diag(jnp.swapaxes(bbi[d], 1, 2))], axis=1).astype(BF16)
    out_map = lambda d: jnp.concatenate(
        [_block_diag(jnp.swapaxes(w["s5_c_re"][l][d], 1, 2)),
         -_block_diag(jnp.swapaxes(w["s5_c_im"][l][d], 1, 2))], axis=0).astype(BF16)
    lw["s5_bf"], lw["s5_bb"] = in_map(0), in_map(1)
    lw["s5_cf"], lw["s5_cb"] = out_map(0), out_map(1)
    coef = jnp.stack([ar[0].reshape(-1), ai[0].reshape(-1), ar[1].reshape(-1), ai[1].reshape(-1)])
    lw["s5_a"] = jnp.broadcast_to(coef[:, None, :], (4, SUBLANES, S5_WIDTH))
    lw["s5_d"] = w["s5_d"][l][None]
    lw["wglu"] = w["s5_w_glu"][l].astype(BF16)

    lw["lru_cw"] = w["lru_conv_w"][l]
    lw["lru_cb"] = w["lru_conv_b"][l][None]
    gate_w = lambda d: jnp.concatenate(
        [_block_diag(w["lru_w_a"][l][d]), _block_diag(w["lru_w_x"][l][d])], axis=1).astype(BF16)
    gate_b = lambda d: jnp.concatenate([w["lru_b_a"][l][d], w["lru_b_x"][l][d]])[None]
    lw["lru_wf"], lw["lru_wb"] = gate_w(0), gate_w(1)
    lw["lru_bf"], lw["lru_bb"] = gate_b(0), gate_b(1)
    sp = jax.nn.softplus(-w["lru_lambda"][l])
    lw["lru_spf"], lw["lru_spb"] = sp[0][None], sp[1][None]

    wb = w["w_branch"][l]
    wb0 = wb[0].reshape(A_HEADS, A_HEAD_DIM, D_MODEL)[HEAD_ORDER].reshape(BRANCH_W, D_MODEL)
    lw["wb"] = jnp.concatenate([wb0[None], wb[1:]], axis=0).astype(BF16)
    lw["wo"] = w["w_out"][l].astype(BF16)

    wr = jnp.pad(jnp.concatenate([w["moe_w_group"][l], w["moe_w_expert"][l]], axis=1),
                 ((0, 0), (0, LANES - N_GROUPS - N_EXPERTS)))
    lw["wr_hi"] = wr.astype(BF16)
    lw["wr_lo"] = (wr - lw["wr_hi"].astype(F32)).astype(BF16)
    lw["br"] = jnp.pad(jnp.concatenate([w["moe_b_group"][l], w["moe_b_expert"][l]]),
                       (0, LANES - N_GROUPS - N_EXPERTS))[None]
    lw["tri"] = jnp.triu(jnp.ones((MOE_SORT_TILE, MOE_SORT_TILE), BF16))
    lw["wga"] = w["moe_w_gate"][l].astype(BF16)
    lw["wup"] = w["moe_w_up"][l].astype(BF16)
    lw["wdn"] = w["moe_w_down"][l].astype(BF16).reshape(N_EXPERTS * EXPERT_FF, D_MODEL)
    return lw


def _rows_to_state(fin, width):
    Bg, n = fin.shape[0], fin.shape[1]
    return jnp.swapaxes(fin, 1, 2).reshape(Bg * SUBLANES, n, width)


def _state_to_rows(state):
    B, n, width = state.shape
    return jnp.swapaxes(state.reshape(B // SUBLANES, SUBLANES, n, width), 1, 2)


def _mix(x, mods, lw, rope_tabs, cache, s5_h0, lru_h0, ctx):
    sh1, sc1, gt1 = mods
    outs = _in_proj(x, sc1, sh1, lw, rope_tabs, ctx)
    q, k, v, s5u_t, lx_t, lg, qm, km, vm = outs[:9]
    if ctx:
        o_a = _attn_a_ctx(q, k, v, lw["sink"][lw["layer"]])
        o_d = _mla_ctx(qm, km, vm)
    else:
        ck, cv, cckv, ckr = cache
        o_a = _attn_a_lat(q, k, v, ck, cv, lw["sink"][lw["layer"]])
        kc, vc = _mla_cache_kv(cckv, ckr, lw)
        o_d = _mla_lat(qm, km, vm, kc, vc)
    yf, yb, s5_fin = _s5_scan(s5u_t, lw, s5_h0)
    hf, hb, lru_fin = _lru_scan(lx_t, lw, lru_h0)
    x = _merge(x, sc1, sh1, gt1, lw, o_a, s5u_t, yf, yb, hf, hb, lg, o_d)
    return x, outs[9:], s5_fin, lru_fin


def kernel(x_prompt, x_sample, cache_attn_k, cache_attn_v, cache_mla_ckv, cache_mla_krope, state_ssm_re, state_ssm_im, state_lru, c, c_ctx, w_mod, b_mod, norm1_g, norm2_g, w_in, a_qnorm_g, a_knorm_g, a_sink, s5_lambda_re, s5_lambda_im, s5_log_step, s5_b_re, s5_b_im, s5_c_re, s5_c_im, s5_d, s5_w_glu, lru_conv_w, lru_conv_b, lru_w_a, lru_b_a, lru_w_x, lru_b_x, lru_lambda, mla_q_lat_norm, mla_w_uq, mla_kv_norm, mla_w_ukv, mla_qnorm_g, mla_knorm_g, w_branch, w_out, moe_w_group, moe_b_group, moe_w_expert, moe_b_expert, moe_w_gate, moe_w_up, moe_w_down):
    w = dict(norm1_g=norm1_g, norm2_g=norm2_g, w_in=w_in, a_qnorm_g=a_qnorm_g, a_knorm_g=a_knorm_g,
             a_sink=a_sink, s5_lambda_re=s5_lambda_re, s5_lambda_im=s5_lambda_im, s5_log_step=s5_log_step,
             s5_b_re=s5_b_re, s5_b_im=s5_b_im, s5_c_re=s5_c_re, s5_c_im=s5_c_im, s5_d=s5_d, s5_w_glu=s5_w_glu,
             lru_conv_w=lru_conv_w, lru_conv_b=lru_conv_b, lru_w_a=lru_w_a, lru_b_a=lru_b_a, lru_w_x=lru_w_x,
             lru_b_x=lru_b_x, lru_lambda=lru_lambda, mla_q_lat_norm=mla_q_lat_norm, mla_w_uq=mla_w_uq,
             mla_kv_norm=mla_kv_norm, mla_w_ukv=mla_w_ukv, mla_qnorm_g=mla_qnorm_g, mla_knorm_g=mla_knorm_g,
             w_branch=w_branch, w_out=w_out, moe_w_group=moe_w_group, moe_b_group=moe_b_group,
             moe_w_expert=moe_w_expert, moe_b_expert=moe_b_expert, moe_w_gate=moe_w_gate, moe_w_up=moe_w_up,
             moe_w_down=moe_w_down)
    B, L, _ = x_prompt.shape
    Bd, Ld, _ = x_sample.shape
    P = cache_attn_k.shape[2]

    cond = jnp.zeros((2 * SUBLANES, D_MODEL), F32).at[:Bd].set(c).at[Bd].set(c_ctx)
    mod = _modulation(cond, w_mod, b_mod[:, None, :])
    mod = mod.reshape(DEPTH, 2 * SUBLANES, 6, D_MODEL)

    rope_a = _rope_tables(Ld, A_HEAD_DIM, A_HEAD_DIM, 0)
    rope_a = tuple(jnp.asarray(np.tile(t, (1, A_HEADS))) for t in rope_a)
    rope_m = tuple(jnp.asarray(t) for t in _rope_tables(Ld, MLA_ROPE, LANES, MLA_NOPE))

    lw_all = jax.vmap(lambda w1: _layer_weights(0, {name: v[None] for name, v in w1.items()}))(w)

    xp, xs = x_prompt, x_sample
    ak_l, av_l, ckv_l, kr_l, sr_l, si_l, lru_l = [], [], [], [], [], [], []
    for l in range(DEPTH):
        lw = dict(lw_all, layer=l)
        lat_mod = [mod[l, :Bd, n][:, None, :] for n in range(6)]
        ctx_mod = [mod[l, Bd:Bd + 1, n][:, None, :] for n in range(6)]

        zs5 = jnp.zeros((B // SUBLANES, 4, SUBLANES, S5_WIDTH), F32)
        zlru = jnp.zeros((B // SUBLANES, 2, SUBLANES, LRU_WIDTH), F32)
        xp, (k32, v32, ckv_n, krp), s5_fin, lru_fin = _mix(xp, ctx_mod[0:3], lw, None, None, zs5, zlru, True)
        xp = _moe(xp, ctx_mod[4], ctx_mod[3], ctx_mod[5], lw)
        ak_l.append(k32.reshape(B, L, A_KV_HEADS, A_HEAD_DIM))
        av_l.append(v32.reshape(B, L, A_KV_HEADS, A_HEAD_DIM))
        ckv_l.append(ckv_n)
        kr_l.append(krp[:, :, MLA_NOPE:MLA_QK])
        fin = _rows_to_state(s5_fin, S5_WIDTH)
        sr_l.append(fin[:, 0::2].reshape(B, 2, S5_GROUPS, S5_STATE))
        si_l.append(fin[:, 1::2].reshape(B, 2, S5_GROUPS, S5_STATE))
        lru_l.append(_rows_to_state(lru_fin, LRU_WIDTH))

        sre = state_ssm_re[:, l].reshape(Bd, 2, S5_WIDTH)
        sim = state_ssm_im[:, l].reshape(Bd, 2, S5_WIDTH)
        s5_h0 = _state_to_rows(jnp.stack([sre[:, 0], sim[:, 0], sre[:, 1], sim[:, 1]], axis=1))
        lru_h0 = _state_to_rows(state_lru[:, l])
        cache = (cache_attn_k[:, l].reshape(Bd, P, 128).astype(BF16),
                 cache_attn_v[:, l].reshape(Bd, P, 128).astype(BF16),
                 cache_mla_ckv[:, l],
                 jnp.pad(cache_mla_krope[:, l], ((0, 0), (0, 0), (MLA_NOPE, LANES - MLA_QK))))
        xs, _, _, _ = _mix(xs, lat_mod[0:3], lw, rope_a + rope_m, cache, s5_h0, lru_h0, False)
        xs = _moe(xs, lat_mod[4], lat_mod[3], lat_mod[5], lw)

    stack = lambda ts: jnp.stack(ts, axis=1)
    return (xp, xs, stack(ak_l), stack(av_l), stack(ckv_l), stack(kr_l), stack(sr_l), stack(si_l), stack(lru_l))
```

```python
import functools
import math

import jax
import jax.numpy as jnp
import numpy as np
from jax import lax
from jax.experimental import pallas as pl
from jax.experimental.pallas import tpu as pltpu

F32 = jnp.float32
BF16 = jnp.bfloat16

D_MODEL = 1024
DEPTH = 2
GRID_W = 64
N_BRANCH = 4
BRANCH_W = 256
ROPE_BASE = 10000.0
EPS = 1e-6
NEG_INF = -1e30
LOG2_E = math.log2(math.e)
A_HEADS = 4
A_KV_HEADS = 2
A_HEAD_DIM = 64
WINDOW = 128
Q_BLOCK = 128
S5_GROUP = 16
S5_GROUPS = 16
S5_STATE = 64
S5_WIDTH = S5_GROUPS * S5_STATE
LRU_WIDTH = 256
LRU_BLOCKS = 4
LRU_CONV = 4
LRU_C = 8.0
MLA_HEADS = 4
MLA_Q_LORA = 256
MLA_KV_LORA = 128
MLA_NOPE = 64
MLA_ROPE = 32
MLA_V = 64
MLA_QK = MLA_NOPE + MLA_ROPE
N_GROUPS = 4
EXPERTS_PER_GROUP = 4
N_EXPERTS = 16
EXPERT_FF = 256

LANES = 128
SUBLANES = 8
VMEM_LIMIT = 56 * 1024 * 1024

TOK_TILE = 256
TOK_STEPS = TOK_TILE // SUBLANES
SCAN_STEPS = 128
MOE_SORT_TILE = 256
MOE_ROW_ALIGN = 16
MOE_SORTED_ROWS = MOE_SORT_TILE + LANES
MOE_SORT_SUBTILES = 8
MOE_UNSORT_SUBTILES = 4
MOE_BLOCK_TILES = 8
MOE_CHUNK = 512
MLA_Q_TILE = 256
ATTN_Q_TILE = 256
CTX_SEQS_PER_STEP = 8
Z_COLS = 1792
HEAD_ORDER = np.array((0, 2, 1, 3))


def _params(sem):
    return pltpu.CompilerParams(dimension_semantics=sem, vmem_limit_bytes=VMEM_LIMIT)


def _layer_spec(layer):
    def spec(shape):
        n = len(shape)
        return pl.BlockSpec((None,) + tuple(shape), lambda *_: (layer,) + (0,) * n)
    return spec


def _modnorm(x, g, sc, sh):
    ms = jnp.mean(x * x, axis=-1, keepdims=True)
    return (x * lax.rsqrt(ms + EPS)) * g * (1.0 + sc) + sh


def _rmsnorm(x, g):
    ms = jnp.mean(x * x, axis=-1, keepdims=True)
    return (x * lax.rsqrt(ms + EPS)) * g


def _segment_ones(width, segment):
    lane = np.arange(width) // segment
    return jnp.asarray(lane[:, None] == lane[None, :], BF16)


def _segment_rsqrt(x, ones, n_real):
    if isinstance(ones, int):
        segment = ones
        lane = lax.broadcasted_iota(jnp.int32, (1, LANES), 1)
        outs = []
        for t in range(x.shape[1] // LANES):
            xt = x[:, LANES * t:LANES * (t + 1)]
            sq = xt * xt
            inv = None
            for s in range(LANES // segment):
                mask = (lane >= s * segment) & (lane < (s + 1) * segment)
                part = sq if segment == LANES else jnp.where(mask, sq, 0.0)
                r = lax.rsqrt(jnp.sum(part, axis=-1, keepdims=True) * (1.0 / n_real) + EPS)
                inv = r if inv is None else jnp.where(mask, r, inv)
            outs.append(jnp.broadcast_to(inv, sq.shape))
        return jnp.concatenate(outs, axis=1)
    sq = x * x
    hi = sq.astype(BF16)
    lo = (sq - hi.astype(F32)).astype(BF16)
    ss = jnp.dot(hi, ones, preferred_element_type=F32) + jnp.dot(lo, ones, preferred_element_type=F32)
    return lax.rsqrt(ss * (1.0 / n_real) + EPS)


def _rope(x, cos, sin_next, sin_prev, quarter):
    width = x.shape[1]
    return (x * cos + pltpu.roll(x, width - quarter, 1) * sin_next
            + pltpu.roll(x, quarter, 1) * sin_prev)


def _mla_kv(ckv_n, kr_placed, wukv, gk, ones_tile, rope_tabs):
    kv = jnp.dot(ckv_n.astype(BF16), wukv, preferred_element_type=F32)
    gain = gk[:, :LANES]
    kfull = kv[:, :MLA_HEADS * LANES] + jnp.concatenate([kr_placed] * MLA_HEADS, axis=1)
    inv_all = _segment_rsqrt(kfull, ones_tile, MLA_QK)
    rotated = None
    if rope_tabs is not None:
        cos, sin_next, sin_prev = rope_tabs
        quarter = MLA_ROPE // 4
        base = kr_placed * gain
        rotated = pltpu.roll(base, LANES - quarter, 1) * sin_next + pltpu.roll(base, quarter, 1) * sin_prev
    heads = []
    for h in range(MLA_HEADS):
        kf = kfull[:, LANES * h:LANES * (h + 1)]
        inv = inv_all[:, LANES * h:LANES * (h + 1)]
        kh = kf * gain
        if rotated is not None:
            kh = kh * cos + rotated
        heads.append(kh * inv)
    return jnp.concatenate(heads, axis=1), kv[:, MLA_HEADS * LANES:]


def _mod_kernel(c_ref, w_ref, b_ref, o_ref):
    c = c_ref[...]
    s = c * jax.nn.sigmoid(c)
    o_ref[...] = jnp.dot(s.astype(BF16), w_ref[...].astype(BF16), preferred_element_type=F32) + b_ref[...]


def _modulation(cond, w_mod, b_mod):
    n_rows = cond.shape[0]
    n_out = w_mod.shape[-1]
    tn = 1024
    return pl.pallas_call(
        _mod_kernel,
        grid=(DEPTH, n_out // tn),
        in_specs=[pl.BlockSpec((n_rows, D_MODEL), lambda l, j: (0, 0)),
                  pl.BlockSpec((None, D_MODEL, tn), lambda l, j: (l, 0, j)),
                  pl.BlockSpec((None, 1, tn), lambda l, j: (l, 0, j))],
        out_specs=pl.BlockSpec((None, n_rows, tn), lambda l, j: (l, 0, j)),
        out_shape=jax.ShapeDtypeStruct((DEPTH, n_rows, n_out), F32),
        compiler_params=_params(("arbitrary", "arbitrary")),
        name="modulation",
    )(cond, w_mod, b_mod)


def _group_rows(ref):
    v = ref[...]
    return v.reshape(v.shape[0] * v.shape[1], v.shape[2])


def _mod_rows(ref, steps):
    v = ref[...]
    if v.shape[0] == 1:
        return v[0]
    return jnp.broadcast_to(v, (v.shape[0], steps, v.shape[2])).reshape(v.shape[0] * steps, v.shape[2])


def _store_group(ref, val):
    ref[...] = val.reshape(ref.shape).astype(ref.dtype)


def _store_time_major(ref, val):
    steps = ref.shape[1] // SUBLANES
    for b in range(SUBLANES):
        for half in range(2):
            ref[half, pl.ds(b, steps, stride=SUBLANES), :] = (
                val[b * steps:(b + 1) * steps, LANES * half:LANES * (half + 1)])


def _load_time_major(ref):
    steps = ref.shape[1] // SUBLANES
    return jnp.concatenate(
        [jnp.concatenate([ref[half, pl.ds(b, steps, stride=SUBLANES), :] for half in range(2)], axis=1)
         for b in range(SUBLANES)], axis=0)


def _in_proj_kernel(rope, ctx, *refs):
    it = iter(refs)
    x_ref, sc_ref, sh_ref, g1_ref, wz_ref = (next(it) for _ in range(5))
    gq_ref, gk_ref, qlg_ref, wuq_ref, gmq_ref, kvg_ref, wukv_ref, gmk_ref = (next(it) for _ in range(8))
    oh_ref, ot_ref = next(it), next(it)
    seg_heads = oh_ref[...] if rope else A_HEAD_DIM
    seg_tiles = ot_ref[...] if rope else LANES
    if rope:
        ca_ref, sna_ref, spa_ref, cm_ref, snm_ref, spm_ref = (next(it) for _ in range(6))
    q_ref, k_ref, v_ref, s5u_ref, lx_ref, lg_ref, qm_ref, km_ref, vm_ref = (next(it) for _ in range(9))
    if ctx:
        k32_ref, v32_ref, ckv_ref, krp_ref = (next(it) for _ in range(4))

    steps = x_ref.shape[1]
    h = _modnorm(_group_rows(x_ref), g1_ref[...], _mod_rows(sc_ref, steps), _mod_rows(sh_ref, steps))
    z = jnp.dot(h.astype(BF16), wz_ref[...], preferred_element_type=F32)
    per_seq = lambda ref: jnp.concatenate([ref[...]] * SUBLANES, axis=0)

    qk = z[:, 0:384]
    qk = qk * _segment_rsqrt(qk, seg_heads, A_HEAD_DIM)
    q = qk[:, 0:256] * gq_ref[...]
    k = qk[:, 256:384] * gk_ref[...]
    v = z[:, 384:512]
    if ctx:
        _store_group(k32_ref, k)
        _store_group(v32_ref, v)
    if rope:
        quarter = A_HEAD_DIM // 4
        atabs = (per_seq(ca_ref), per_seq(sna_ref), per_seq(spa_ref))
        q = _rope(q, *atabs, quarter)
        k = _rope(k, *(t[:, :LANES] for t in atabs), quarter)
    _store_group(q_ref, q)
    _store_group(k_ref, k)
    _store_group(v_ref, v)

    _store_time_major(s5u_ref, z[:, 512:768])
    _store_time_major(lx_ref, z[:, 768:1024])
    _store_group(lg_ref, z[:, 1024:1280])

    mtabs = mtabs4 = None
    if rope:
        mtabs = tuple(per_seq(r) for r in (cm_ref, snm_ref, spm_ref))
        mtabs4 = tuple(jnp.concatenate([t] * MLA_HEADS, axis=1) for t in mtabs)
    ql = _rmsnorm(z[:, 1280:1536], qlg_ref[...])
    qm = jnp.dot(ql.astype(BF16), wuq_ref[...], preferred_element_type=F32)
    qm = qm * _segment_rsqrt(qm, seg_tiles, MLA_QK) * gmq_ref[...]
    if rope:
        qm = _rope(qm, *mtabs4, MLA_ROPE // 4)
    _store_group(qm_ref, qm)

    ckv_n = _rmsnorm(z[:, 1536:1664], kvg_ref[...])
    krp = z[:, 1664:1792]
    km, vm = _mla_kv(ckv_n, krp, wukv_ref[...], gmk_ref[...], seg_tiles, mtabs)
    _store_group(km_ref, km)
    _store_group(vm_ref, vm)
    if ctx:
        _store_group(ckv_ref, ckv_n)
        _store_group(krp_ref, krp)


def _group_specs(per_seq_mod):
    TS = TOK_STEPS
    tok = lambda c: pl.BlockSpec((SUBLANES, TS, c), lambda g, i: (g, i, 0))
    scan = pl.BlockSpec((None, 2, TS * SUBLANES, LANES), lambda g, i: (g, 0, i, 0))
    if per_seq_mod:
        mod = pl.BlockSpec((SUBLANES, 1, D_MODEL), lambda g, i: (g, 0, 0))
    else:
        mod = pl.BlockSpec((1, 1, D_MODEL), lambda g, i: (0, 0, 0))
    return tok, scan, mod


def _in_proj(x, sc, sh, lw, rope_tabs, ctx):
    wspec = _layer_spec(lw["layer"])
    B, L, _ = x.shape
    Bg = B // SUBLANES
    TS = TOK_STEPS
    rope = rope_tabs is not None
    tok, scan, mod = _group_specs(sc.shape[0] > 1)
    in_specs = [tok(D_MODEL), mod, mod, wspec((1, D_MODEL)), wspec((D_MODEL, Z_COLS)),
                wspec((1, 256)), wspec((1, 128)), wspec((1, 256)), wspec((256, 512)), wspec((1, 512)),
                wspec((1, 128)), wspec((128, 768)), wspec((1, 512)),
                pl.BlockSpec((384, 384), lambda g, i: (0, 0)), pl.BlockSpec((512, 512), lambda g, i: (0, 0))]
    args = [x, sc, sh, lw["g1"], lw["wz"], lw["gq"], lw["gk"], lw["qlg"], lw["wuq"], lw["gmq"],
            lw["kvg"], lw["wukv"], lw["gmk"], _segment_ones(384, A_HEAD_DIM), _segment_ones(MLA_HEADS * LANES, LANES)]
    if rope:
        in_specs += [pl.BlockSpec((TS, 256), lambda g, i: (i, 0))] * 3
        in_specs += [pl.BlockSpec((TS, 128), lambda g, i: (i, 0))] * 3
        args += list(rope_tabs)
    out_specs = [tok(256), tok(128), tok(128), scan, scan, tok(256), tok(512), tok(512), tok(256)]
    sds = jax.ShapeDtypeStruct
    scan_shape = sds((Bg, 2, L * SUBLANES, LANES), F32)
    out_shape = [sds((B, L, 256), BF16), sds((B, L, 128), BF16), sds((B, L, 128), BF16),
                 scan_shape, scan_shape,
                 sds((B, L, 256), F32), sds((B, L, 512), BF16), sds((B, L, 512), BF16),
                 sds((B, L, 256), BF16)]
    if ctx:
        out_specs += [tok(128), tok(128), tok(128), tok(128)]
        out_shape += [sds((B, L, 128), F32)] * 4
    return pl.pallas_call(
        functools.partial(_in_proj_kernel, rope, ctx),
        grid=(Bg, L // TS),
        in_specs=in_specs, out_specs=out_specs, out_shape=out_shape,
        compiler_params=_params(("parallel", "parallel")),
        name="in_proj_ctx" if ctx else "in_proj_lat",
    )(*args)


def _mla_cache_kernel(ckv_ref, krp_ref, wukv_ref, gmk_ref, ot_ref, km_ref, vm_ref):
    km, vm = _mla_kv(ckv_ref[...], krp_ref[...], wukv_ref[...], gmk_ref[...], ot_ref[...], None)
    km_ref[...] = km.astype(BF16)
    vm_ref[...] = vm.astype(BF16)


def _mla_cache_kv(cckv, ckr_placed, lw):
    wspec = _layer_spec(lw["layer"])
    B, P, _ = cckv.shape
    tok = lambda c: pl.BlockSpec((None, P, c), lambda b: (b, 0, 0))
    return pl.pallas_call(
        _mla_cache_kernel,
        grid=(B,),
        in_specs=[tok(128), tok(128), wspec((128, 768)), wspec((1, 512)),
                  pl.BlockSpec((512, 512), lambda b: (0, 0))],
        out_specs=[tok(512), tok(256)],
        out_shape=[jax.ShapeDtypeStruct((B, P, 512), BF16), jax.ShapeDtypeStruct((B, P, 256), BF16)],
        compiler_params=_params(("parallel",)),
        name="mla_cache_kv",
    )(cckv, ckr_placed, lw["wukv"], lw["gmk"], _segment_ones(MLA_HEADS * LANES, LANES))


def _sink_softmax(s, sink):
    sink = sink * LOG2_E
    m = jnp.maximum(jnp.max(s, axis=-1, keepdims=True), sink)
    e = jnp.exp2(s - m)
    den = jnp.sum(e, axis=-1, keepdims=True) + jnp.exp2(sink - m)
    return e.astype(BF16), 1.0 / den


def _gqa_tile(qt, keys, vst, sink_lo, sink_hi, mask):
    lane = lax.broadcasted_iota(jnp.int32, (1, LANES), 1)
    lo = lane < A_HEAD_DIM
    zero = jnp.zeros_like(qt)
    ps, invs = [], []
    for qh, sink in ((jnp.where(lo, qt, zero), sink_lo), (jnp.where(lo, zero, qt), sink_hi)):
        s = lax.dot_general(qh, keys, (((1,), (1,)), ((), ())), preferred_element_type=F32)
        if mask is not None:
            s = jnp.where(mask, s, NEG_INF)
        p, inv = _sink_softmax(s, sink)
        ps.append(p)
        invs.append(inv)
    o = jnp.dot(jnp.concatenate(ps, axis=1), vst, preferred_element_type=F32)
    return o * jnp.where(lo, invs[0], invs[1])


def _stack_kv_halves(v):
    lane = lax.broadcasted_iota(jnp.int32, (1, LANES), 1)
    lo = lane < A_HEAD_DIM
    zero = jnp.zeros_like(v)
    return jnp.concatenate([jnp.where(lo, v, zero), jnp.where(lo, zero, v)], axis=0)


def _attn_a_ctx_kernel(sink_ref, q_ref, k_ref, v_ref, o_ref):
    for s in range(q_ref.shape[0]):
        keys = k_ref[s]
        vst = _stack_kv_halves(v_ref[s])
        for t in range(2):
            o = _gqa_tile(q_ref[s, :, LANES * t:LANES * (t + 1)], keys, vst, sink_ref[t], sink_ref[2 + t], None)
            o_ref[s, :, LANES * t:LANES * (t + 1)] = o.astype(o_ref.dtype)


def _attn_a_ctx(q, k, v, sink):
    B, L, _ = q.shape
    nb = CTX_SEQS_PER_STEP
    tok = lambda c: pl.BlockSpec((nb, L, c), lambda b: (b, 0, 0))
    return pl.pallas_call(
        _attn_a_ctx_kernel,
        grid=(B // nb,),
        in_specs=[pl.BlockSpec(memory_space=pltpu.SMEM), tok(256), tok(128), tok(128)],
        out_specs=tok(256),
        out_shape=jax.ShapeDtypeStruct((B, L, 256), BF16),
        compiler_params=_params(("parallel",)),
        name="attn_a_ctx",
    )(sink, q, k, v)


def _attn_a_lat_kernel(sink_ref, q_ref, kc_ref, vc_ref, k_ref, v_ref, o_ref):
    i = pl.program_id(1)
    L = k_ref.shape[0]
    n_ctx = kc_ref.shape[0]
    rows = q_ref.shape[0]
    span = rows + 2 * WINDOW
    start = pl.multiple_of(jnp.clip(i * rows - WINDOW, 0, L - span), WINDOW)
    keys = jnp.concatenate([kc_ref[...], k_ref[pl.ds(start, span), :]], axis=0)
    vals = jnp.concatenate([vc_ref[...], v_ref[pl.ds(start, span), :]], axis=0)
    vst = _stack_kv_halves(vals)
    col = lax.broadcasted_iota(jnp.int32, (rows, n_ctx + span), 1)
    row = lax.broadcasted_iota(jnp.int32, (rows, n_ctx + span), 0)
    rel = (col - n_ctx + start) - (row + i * rows)
    mask = (col < n_ctx) | (jnp.abs(rel) <= WINDOW)
    for t in range(2):
        o = _gqa_tile(q_ref[:, LANES * t:LANES * (t + 1)], keys, vst, sink_ref[t], sink_ref[2 + t], mask)
        o_ref[:, LANES * t:LANES * (t + 1)] = o.astype(o_ref.dtype)


def _attn_a_lat(q, k, v, kc, vc, sink):
    B, L, _ = q.shape
    P = kc.shape[1]
    blk = lambda c: pl.BlockSpec((None, ATTN_Q_TILE, c), lambda b, i: (b, i, 0))
    whole = lambda n, c: pl.BlockSpec((None, n, c), lambda b, i: (b, 0, 0))
    return pl.pallas_call(
        _attn_a_lat_kernel,
        grid=(B, L // ATTN_Q_TILE),
        in_specs=[pl.BlockSpec(memory_space=pltpu.SMEM), blk(256), whole(P, 128), whole(P, 128),
                  whole(L, 128), whole(L, 128)],
        out_specs=blk(256),
        out_shape=jax.ShapeDtypeStruct((B, L, 256), BF16),
        compiler_params=_params(("parallel", "parallel")),
        name="attn_a_lat",
    )(sink, q, kc, vc, k, v)


def _mla_stack_values(vals):
    lane = lax.broadcasted_iota(jnp.int32, (1, MLA_HEADS * MLA_V), 1)
    zero = jnp.zeros_like(vals)
    return jnp.concatenate([jnp.where((lane >= MLA_V * h) & (lane < MLA_V * (h + 1)), vals, zero)
                            for h in range(MLA_HEADS)], axis=0)


def _mla_attend(q, key_parts, vst):
    lane = lax.broadcasted_iota(jnp.int32, (1, MLA_HEADS * MLA_V), 1)
    ps = []
    inv = jnp.zeros((q.shape[0], MLA_HEADS * MLA_V), F32)
    for h in range(MLA_HEADS):
        qh = q[:, LANES * h:LANES * (h + 1)]
        ss = [lax.dot_general(qh, kp[:, LANES * h:LANES * (h + 1)], (((1,), (1,)), ((), ())),
                              preferred_element_type=F32) for kp in key_parts]
        m = functools.reduce(jnp.maximum, [jnp.max(s, axis=-1, keepdims=True) for s in ss])
        es = [jnp.exp2(s - m) for s in ss]
        den = functools.reduce(lambda a, b: a + b, [jnp.sum(e, axis=-1, keepdims=True) for e in es])
        ps += [e.astype(BF16) for e in es]
        inv = jnp.where((lane >= MLA_V * h) & (lane < MLA_V * (h + 1)), 1.0 / den, inv)
    return jnp.dot(jnp.concatenate(ps, axis=1), vst, preferred_element_type=F32) * inv


def _mla_ctx_kernel(q_ref, k_ref, v_ref, o_ref):
    for s in range(q_ref.shape[0]):
        o_ref[s] = _mla_attend(q_ref[s], [k_ref[s]], _mla_stack_values(v_ref[s])).astype(o_ref.dtype)


def _mla_ctx(q, k, v):
    B, L, _ = q.shape
    nb = CTX_SEQS_PER_STEP
    tok = lambda c: pl.BlockSpec((nb, L, c), lambda b: (b, 0, 0))
    return pl.pallas_call(
        _mla_ctx_kernel,
        grid=(B // nb,),
        in_specs=[tok(512), tok(512), tok(256)],
        out_specs=tok(256),
        out_shape=jax.ShapeDtypeStruct((B, L, 256), BF16),
        compiler_params=_params(("parallel",)),
        name="mla_ctx",
    )(q, k, v)


def _mla_lat_kernel(q_ref, kc_ref, vc_ref, k_ref, v_ref, o_ref):
    vst = _mla_stack_values(jnp.concatenate([vc_ref[...], v_ref[...]], axis=0))
    o = _mla_attend(q_ref[...], [kc_ref[...], k_ref[...]], vst)
    o_ref[...] = o.astype(o_ref.dtype)


def _mla_lat(q, k, v, kc, vc):
    B, L, _ = q.shape
    P = kc.shape[1]
    TQ = MLA_Q_TILE
    blk = lambda c: pl.BlockSpec((None, TQ, c), lambda b, i: (b, i, 0))
    whole = lambda n, c: pl.BlockSpec((None, n, c), lambda b, i: (b, 0, 0))
    return pl.pallas_call(
        _mla_lat_kernel,
        grid=(B, L // TQ),
        in_specs=[blk(512), whole(P, 512), whole(P, 256), whole(L, 512), whole(L, 256)],
        out_specs=blk(256),
        out_shape=jax.ShapeDtypeStruct((B, L, 256), BF16),
        compiler_params=_params(("parallel", "parallel")),
        name="mla_lat",
    )(q, kc, vc, k, v)


def _join_halves(ref):
    return jnp.concatenate([ref[0], ref[1]], axis=1)


def _split_halves(ref, val):
    ref[0] = val[:, :LANES]
    ref[1] = val[:, LANES:]


def _s5_kernel(uf_ref, ub_ref, bf_ref, bb_ref, a_ref, h0_ref, cf_ref, cb_ref,
               yf_ref, yb_ref, fin_ref, s_ref, st_ref):
    i = pl.program_id(1)
    steps = SCAN_STEPS

    @pl.when(i == 0)
    def _():
        st_ref[...] = h0_ref[...]

    uf = _join_halves(uf_ref).astype(BF16)
    ub = _join_halves(ub_ref).astype(BF16)
    s_ref[0] = jnp.dot(uf, bf_ref[:, :S5_WIDTH], preferred_element_type=F32)
    s_ref[1] = jnp.dot(uf, bf_ref[:, S5_WIDTH:], preferred_element_type=F32)
    s_ref[2] = jnp.dot(ub, bb_ref[:, :S5_WIDTH], preferred_element_type=F32)
    s_ref[3] = jnp.dot(ub, bb_ref[:, S5_WIDTH:], preferred_element_type=F32)

    lane_chunk = 2 * LANES
    for c in range(S5_WIDTH // lane_chunk):
        sl = slice(lane_chunk * c, lane_chunk * (c + 1))
        arf, aif, arb, aib = (a_ref[n, :, sl] for n in range(4))

        def body(j, carry):
            hrf, hif, hrb, hib = carry
            rf = pl.multiple_of(j * SUBLANES, SUBLANES)
            rb = pl.multiple_of((steps - 1 - j) * SUBLANES, SUBLANES)
            nrf = arf * hrf - aif * hif + s_ref[0, pl.ds(rf, SUBLANES), sl]
            nif = arf * hif + aif * hrf + s_ref[1, pl.ds(rf, SUBLANES), sl]
            nrb = arb * hrb - aib * hib + s_ref[2, pl.ds(rb, SUBLANES), sl]
            nib = arb * hib + aib * hrb + s_ref[3, pl.ds(rb, SUBLANES), sl]
            s_ref[0, pl.ds(rf, SUBLANES), sl] = nrf
            s_ref[1, pl.ds(rf, SUBLANES), sl] = nif
            s_ref[2, pl.ds(rb, SUBLANES), sl] = nrb
            s_ref[3, pl.ds(rb, SUBLANES), sl] = nib
            return nrf, nif, nrb, nib

        fin = lax.fori_loop(0, steps, body, tuple(st_ref[n, :, sl] for n in range(4)), unroll=8)
        for n in range(4):
            st_ref[n, :, sl] = fin[n]

    hf = jnp.concatenate([s_ref[0].astype(BF16), s_ref[1].astype(BF16)], axis=1)
    _split_halves(yf_ref, jnp.dot(hf, cf_ref[...], preferred_element_type=F32))
    hb = jnp.concatenate([s_ref[2].astype(BF16), s_ref[3].astype(BF16)], axis=1)
    _split_halves(yb_ref, jnp.dot(hb, cb_ref[...], preferred_element_type=F32))
    fin_ref[...] = st_ref[...]


def _s5_scan(u_rows, lw, h0):
    wspec = _layer_spec(lw["layer"])
    Bg, _, rows, _ = u_rows.shape
    R = SCAN_STEPS * SUBLANES
    n = rows // R
    fwd = pl.BlockSpec((None, 2, R, LANES), lambda g, i: (g, 0, i, 0))
    bwd = pl.BlockSpec((None, 2, R, LANES), lambda g, i: (g, 0, n - 1 - i, 0))
    st = pl.BlockSpec((None, 4, SUBLANES, S5_WIDTH), lambda g, i: (g, 0, 0, 0))
    return pl.pallas_call(
        _s5_kernel,
        grid=(Bg, n),
        in_specs=[fwd, bwd, wspec((256, 2 * S5_WIDTH)), wspec((256, 2 * S5_WIDTH)),
                  wspec((4, SUBLANES, S5_WIDTH)), st, wspec((2 * S5_WIDTH, 256)), wspec((2 * S5_WIDTH, 256))],
        out_specs=[fwd, bwd, st],
        out_shape=[jax.ShapeDtypeStruct(u_rows.shape, F32), jax.ShapeDtypeStruct(u_rows.shape, F32),
                   jax.ShapeDtypeStruct((Bg, 4, SUBLANES, S5_WIDTH), F32)],
        scratch_shapes=[pltpu.VMEM((4, R, S5_WIDTH), F32), pltpu.VMEM((4, SUBLANES, S5_WIDTH), F32)],
        compiler_params=_params(("parallel", "arbitrary")),
        name="s5_scan",
    )(u_rows, u_rows, lw["s5_bf"], lw["s5_bb"], lw["s5_a"], h0, lw["s5_cf"], lw["s5_cb"])


def _lru_gates(x_ref, pre_ref, post_ref, has_pre, has_post, cw_ref, cb_ref, w_ref, b_ref, sp_ref, a_ref, h_ref):
    R = x_ref.shape[1]
    pre = jnp.where(has_pre, _join_halves(pre_ref), 0.0)
    post = jnp.where(has_post, _join_halves(post_ref), 0.0)
    xp = jnp.concatenate([pre, _join_halves(x_ref), post], axis=0)
    xc = cb_ref[...]
    for t in range(LRU_CONV):
        xc = xc + xp[SUBLANES * t:SUBLANES * t + R] * cw_ref[t:t + 1, :]
    g = jnp.dot(xc.astype(BF16), w_ref[...], preferred_element_type=F32) + b_ref[...]
    r = jax.nn.sigmoid(g[:, :LRU_WIDTH])
    ig = jax.nn.sigmoid(g[:, LRU_WIDTH:])
    log_a = (-LRU_C) * r * sp_ref[...]
    a = jnp.exp(log_a)
    a_ref[...] = a
    h_ref[...] = jnp.sqrt(1.0 - a * a) * (ig * xc)


def _lru_kernel(xf_ref, xfp_ref, xfn_ref, xb_ref, xbp_ref, xbn_ref, cw_ref, cb_ref, wf_ref, wb_ref,
                bf_ref, bb_ref, spf_ref, spb_ref, h0_ref, of_ref, ob_ref, fin_ref,
                af_ref, ab_ref, hf_ref, hb_ref, st_ref):
    i = pl.program_id(1)
    n = pl.num_programs(1)
    steps = SCAN_STEPS

    @pl.when(i == 0)
    def _():
        st_ref[...] = h0_ref[...]

    _lru_gates(xf_ref, xfp_ref, xfn_ref, i > 0, i < n - 1, cw_ref, cb_ref, wf_ref, bf_ref, spf_ref, af_ref, hf_ref)
    _lru_gates(xb_ref, xbp_ref, xbn_ref, i < n - 1, i > 0, cw_ref, cb_ref, wb_ref, bb_ref, spb_ref, ab_ref, hb_ref)

    def body(j, carry):
        hf, hb = carry
        rf = pl.multiple_of(j * SUBLANES, SUBLANES)
        rb = pl.multiple_of((steps - 1 - j) * SUBLANES, SUBLANES)
        nf = af_ref[pl.ds(rf, SUBLANES), :] * hf + hf_ref[pl.ds(rf, SUBLANES), :]
        nb = ab_ref[pl.ds(rb, SUBLANES), :] * hb + hb_ref[pl.ds(rb, SUBLANES), :]
        hf_ref[pl.ds(rf, SUBLANES), :] = nf
        hb_ref[pl.ds(rb, SUBLANES), :] = nb
        return nf, nb

    ff, fb = lax.fori_loop(0, steps, body, (st_ref[0], st_ref[1]), unroll=8)
    st_ref[0] = ff
    st_ref[1] = fb
    fin_ref[...] = st_ref[...]
    _split_halves(of_ref, hf_ref[...])
    _split_halves(ob_ref, hb_ref[...])


def _lru_scan(x_rows, lw, h0):
    wspec = _layer_spec(lw["layer"])
    Bg, _, rows, _ = x_rows.shape
    R = SCAN_STEPS * SUBLANES
    n = rows // R
    pre_rows = 2 * SUBLANES
    fwd = lambda g, i: i
    bwd = lambda g, i: n - 1 - i
    blk = lambda m: pl.BlockSpec((None, 2, R, LANES), lambda g, i: (g, 0, m(g, i), 0))
    pre = lambda m: pl.BlockSpec(
        (None, 2, pre_rows, LANES), lambda g, i: (g, 0, jnp.maximum(m(g, i) * (R // pre_rows) - 1, 0), 0))
    post = lambda m: pl.BlockSpec(
        (None, 2, SUBLANES, LANES),
        lambda g, i: (g, 0, jnp.minimum((m(g, i) + 1) * (R // SUBLANES), rows // SUBLANES - 1), 0))
    st = pl.BlockSpec((None, 2, SUBLANES, 256), lambda g, i: (g, 0, 0, 0))
    return pl.pallas_call(
        _lru_kernel,
        grid=(Bg, n),
        in_specs=[blk(fwd), pre(fwd), post(fwd), blk(bwd), pre(bwd), post(bwd),
                  wspec((LRU_CONV, 256)), wspec((1, 256)), wspec((256, 512)), wspec((256, 512)),
                  wspec((1, 512)), wspec((1, 512)), wspec((1, 256)), wspec((1, 256)), st],
        out_specs=[blk(fwd), blk(bwd), st],
        out_shape=[jax.ShapeDtypeStruct(x_rows.shape, F32), jax.ShapeDtypeStruct(x_rows.shape, F32),
                   jax.ShapeDtypeStruct((Bg, 2, SUBLANES, 256), F32)],
        scratch_shapes=[pltpu.VMEM((R, 256), F32), pltpu.VMEM((R, 256), F32), pltpu.VMEM((R, 256), F32),
                        pltpu.VMEM((R, 256), F32), pltpu.VMEM((2, SUBLANES, 256), F32)],
        compiler_params=_params(("parallel", "arbitrary")),
        name="lru_scan",
    )(x_rows, x_rows, x_rows, x_rows, x_rows, x_rows, lw["lru_cw"], lw["lru_cb"], lw["lru_wf"], lw["lru_wb"],
      lw["lru_bf"], lw["lru_bb"], lw["lru_spf"], lw["lru_spb"], h0)


def _merge_kernel(x_ref, sc_ref, sh_ref, gt_ref, g1_ref, wg_ref, oa_ref, u_ref, yf_ref, yb_ref, d_ref,
                  wglu_ref, hf_ref, hb_ref, lg_ref, od_ref, wb_ref, wo_ref, o_ref):
    steps = x_ref.shape[1]
    x = _group_rows(x_ref)
    h = _modnorm(x, g1_ref[...], _mod_rows(sc_ref, steps), _mod_rows(sh_ref, steps)).astype(BF16)

    yb5 = jax.nn.gelu(d_ref[...] * _load_time_major(u_ref) + _load_time_major(yf_ref) + _load_time_major(yb_ref))
    gv = jnp.dot(yb5.astype(BF16), wglu_ref[...], preferred_element_type=F32)
    o_b = gv[:, :BRANCH_W] * jax.nn.sigmoid(gv[:, BRANCH_W:])
    o_c = (_load_time_major(hf_ref) + _load_time_major(hb_ref)) * jax.nn.gelu(_group_rows(lg_ref))
    branches = (_group_rows(oa_ref), o_b.astype(BF16), o_c.astype(BF16), _group_rows(od_ref))

    acc = jnp.zeros(x.shape, F32)
    for n in range(N_BRANCH):
        gate = jnp.dot(h, wg_ref[:, D_MODEL * n:D_MODEL * (n + 1)], preferred_element_type=F32)
        proj = jnp.dot(branches[n], wb_ref[n], preferred_element_type=F32)
        acc = acc + jax.nn.sigmoid(gate) * proj
    out = jnp.dot(acc.astype(BF16), wo_ref[...], preferred_element_type=F32)
    _store_group(o_ref, x + _mod_rows(gt_ref, steps) * out)


def _merge(x, sc, sh, gt, lw, o_a, s5u_t, yf_t, yb_t, hf_t, hb_t, lg, o_d):
    wspec = _layer_spec(lw["layer"])
    B, L, _ = x.shape
    tok, scan, mod = _group_specs(sc.shape[0] > 1)
    return pl.pallas_call(
        _merge_kernel,
        grid=(B // SUBLANES, L // TOK_STEPS),
        in_specs=[tok(D_MODEL), mod, mod, mod, wspec((1, D_MODEL)), wspec((D_MODEL, N_BRANCH * D_MODEL)),
                  tok(256), scan, scan, scan, wspec((1, 256)), wspec((256, 512)),
                  scan, scan, tok(256), tok(256), wspec((N_BRANCH, BRANCH_W, D_MODEL)),
                  wspec((D_MODEL, D_MODEL))],
        out_specs=tok(D_MODEL),
        out_shape=jax.ShapeDtypeStruct((B, L, D_MODEL), F32),
        compiler_params=_params(("parallel", "parallel")),
        name="merge",
    )(x, sc, sh, gt, lw["g1"], lw["wg"], o_a, s5u_t, yf_t, yb_t, lw["s5_d"], lw["wglu"],
      hf_t, hb_t, lg, o_d, lw["wb"], lw["wo"])


def _first_index(values, target):
    idx = jnp.full_like(target, float(len(values) - 1))
    for n in range(len(values) - 2, -1, -1):
        idx = jnp.where(values[n] == target, float(n), idx)
    return idx


def _list_max(values):
    return functools.reduce(jnp.maximum, values)


def _moe_sort_kernel(x_ref, sc_ref, sh_ref, g2_ref, wrh_ref, wrl_ref, br_ref, tri_ref,
                     hs_ref, gs_ref, dcol_ref, meta_ref):
    for s in range(x_ref.shape[0] // MOE_SORT_TILE):
        tok = pl.ds(s * MOE_SORT_TILE, MOE_SORT_TILE)
        srt = pl.ds(s * MOE_SORTED_ROWS, MOE_SORTED_ROWS)
        _moe_sort_tile(x_ref.at[tok], sc_ref, sh_ref, g2_ref, wrh_ref, wrl_ref, br_ref, tri_ref,
                       hs_ref.at[srt], gs_ref.at[srt], dcol_ref.at[tok], meta_ref.at[s])


def _moe_sort_tile(x_ref, sc_ref, sh_ref, g2_ref, wrh_ref, wrl_ref, br_ref, tri_ref,
                   hs_ref, gs_ref, dcol_ref, meta_ref):
    T = MOE_SORT_TILE
    h = _modnorm(x_ref[...], g2_ref[...], sc_ref[...], sh_ref[...])
    hh = h.astype(BF16)
    hl = (h - hh.astype(F32)).astype(BF16)
    nt = (((1,), (1,)), ((), ()))
    logits = (jnp.dot(hh, wrh_ref[...], preferred_element_type=F32)
              + jnp.dot(hl, wrh_ref[...], preferred_element_type=F32)
              + jnp.dot(hh, wrl_ref[...], preferred_element_type=F32)) + br_ref[...]
    lt = logits.T
    gl = [lt[g:g + 1, :] for g in range(N_GROUPS)]
    gmax = _list_max(gl)
    g_idx = _first_index(gl, gmax)
    pg = 1.0 / sum(jnp.exp(v - gmax) for v in gl)
    hot = [g_idx == float(g) for g in range(N_GROUPS)]
    el = []
    for e in range(EXPERTS_PER_GROUP):
        v = jnp.zeros_like(gmax)
        for g in range(N_GROUPS):
            r = N_GROUPS + EXPERTS_PER_GROUP * g + e
            v = jnp.where(hot[g], lt[r:r + 1, :], v)
        el.append(v)
    emax = _list_max(el)
    ee = [jnp.exp(v - emax) for v in el]
    esum = sum(ee)
    pe = [v / esum for v in ee]
    v1 = _list_max(pe)
    i1 = _first_index(pe, v1)
    pe2 = [jnp.where(i1 == float(e), -1.0, pe[e]) for e in range(EXPERTS_PER_GROUP)]
    v2 = _list_max(pe2)
    i2 = _first_index(pe2, v2)
    tot = v1 + v2
    w = [jnp.where(i1 == float(e), pg * v1 / tot, jnp.where(i2 == float(e), pg * v2 / tot, 0.0))
         for e in range(EXPERTS_PER_GROUP)]

    zero_row = jnp.zeros_like(gmax)
    g8 = jnp.concatenate([jnp.where(hot[g], 1.0, 0.0) for g in range(N_GROUPS)] + [zero_row] * 4, axis=0)
    cum = jnp.dot(g8.astype(BF16), tri_ref[...], preferred_element_type=F32)
    off = jnp.zeros((1, 1), F32)
    dest = zero_row
    counts = []
    for g in range(N_GROUPS):
        cnt = cum[g:g + 1, T - 1:T]
        padded = jnp.floor((cnt + (MOE_ROW_ALIGN - 1.0)) * (1.0 / MOE_ROW_ALIGN)) * MOE_ROW_ALIGN
        dest = jnp.where(hot[g], off + cum[g:g + 1, :] - 1.0, dest)
        off = off + padded
        counts.append(padded)
    rows = lax.broadcasted_iota(jnp.int32, (MOE_SORTED_ROWS, T), 0).astype(F32)
    perm = jnp.where(rows == dest, 1.0, 0.0).astype(BF16)
    hs_ref[...] = jnp.dot(perm, hh, preferred_element_type=F32).astype(BF16)
    gates = jnp.concatenate(w + [jnp.zeros((LANES - EXPERTS_PER_GROUP, T), F32)], axis=0)
    ghi = gates.astype(BF16)
    glo = (gates - ghi.astype(F32)).astype(BF16)
    gs_ref[...] = (lax.dot_general(perm, ghi, nt, preferred_element_type=F32)
                   + lax.dot_general(perm, glo, nt, preferred_element_type=F32))
    r_i = lax.broadcasted_iota(jnp.int32, (T, T), 0)
    c_i = lax.broadcasted_iota(jnp.int32, (T, T), 1)
    dcol = jnp.sum(jnp.where(r_i == c_i, jnp.broadcast_to(dest, (T, T)), 0.0), axis=1, keepdims=True)
    dcol_ref[...] = jnp.broadcast_to(dcol, (T, LANES))
    meta_ref[...] = jnp.concatenate([jnp.broadcast_to(c, (1, LANES)) for c in counts]
                                    + [jnp.zeros((SUBLANES - N_GROUPS, LANES), F32)], axis=0)


def _moe_sort(xf, sc, sh, lw, tiles_per_mod):
    wspec = _layer_spec(lw["layer"])
    N = xf.shape[0]
    T = MOE_SORT_TILE
    n_tiles = N // T
    sub = MOE_SORT_SUBTILES
    per_b = sc.shape[0] > 1
    mod = pl.BlockSpec((None, 1, D_MODEL), lambda t: (t * sub // tiles_per_mod if per_b else 0, 0, 0))
    sds = jax.ShapeDtypeStruct
    return pl.pallas_call(
        _moe_sort_kernel,
        grid=(n_tiles // sub,),
        in_specs=[pl.BlockSpec((sub * T, D_MODEL), lambda t: (t, 0)), mod, mod, wspec((1, D_MODEL)),
                  wspec((D_MODEL, LANES)), wspec((D_MODEL, LANES)), wspec((1, LANES)), wspec((T, T))],
        out_specs=[pl.BlockSpec((sub * MOE_SORTED_ROWS, D_MODEL), lambda t: (t, 0)),
                   pl.BlockSpec((sub * MOE_SORTED_ROWS, LANES), lambda t: (t, 0)),
                   pl.BlockSpec((sub * T, LANES), lambda t: (t, 0)),
                   pl.BlockSpec((sub, SUBLANES, LANES), lambda t: (t, 0, 0))],
        out_shape=[sds((n_tiles * MOE_SORTED_ROWS, D_MODEL), BF16), sds((n_tiles * MOE_SORTED_ROWS, LANES), F32),
                   sds((N, LANES), F32), sds((n_tiles, SUBLANES, LANES), F32)],
        compiler_params=_params(("parallel",)),
        name="moe_sort",
    )(xf, sc, sh, lw["g2"], lw["wr_hi"], lw["wr_lo"], lw["br"], lw["tri"])


def _moe_expert_kernel(cnt_ref, hs_ref, gs_ref, wga_ref, wup_ref, wdn_ref, ys_ref, ch_ref, cg_ref, cy_ref):
    j = pl.program_id(0)
    g = pl.program_id(1)
    piece = MOE_ROW_ALIGN

    @pl.when((j == 0) & (g == 0))
    def _():
        ch_ref[...] = jnp.zeros_like(ch_ref)
        cg_ref[...] = jnp.zeros_like(cg_ref)

    @pl.when(g == 0)
    def _():
        ys_ref[...] = jnp.zeros_like(ys_ref)

    def for_each_piece(move):
        packed = jnp.int32(0)
        for t in range(MOE_BLOCK_TILES):
            base = (j * MOE_BLOCK_TILES + t) * N_GROUPS
            start = jnp.int32(t * MOE_SORTED_ROWS)
            for g2 in range(N_GROUPS - 1):
                start = start + jnp.where(g2 < g, cnt_ref[base + g2], 0)
            n = cnt_ref[base + g]

            def body(k, carry, start=start, packed=packed):
                move(pl.multiple_of(start + k * piece, piece), pl.multiple_of(packed + k * piece, piece))
                return carry

            lax.fori_loop(0, lax.shift_right_logical(n, 4), body, 0)
            packed = packed + n
        return packed

    def pack(src, dst):
        ch_ref[pl.ds(dst, piece), :] = hs_ref[pl.ds(src, piece), :]
        cg_ref[pl.ds(dst, piece), :] = gs_ref[pl.ds(src, piece), :]

    rows = for_each_piece(pack)

    def experts(r, size):
        h = ch_ref[pl.ds(r, size), :]
        acts = []
        for e in range(EXPERTS_PER_GROUP):
            a = jnp.dot(h, wga_ref[e], preferred_element_type=F32)
            u = jnp.dot(h, wup_ref[e], preferred_element_type=F32)
            acts.append(((a * jax.nn.sigmoid(a)) * u * cg_ref[pl.ds(r, size), e:e + 1]).astype(BF16))
        y = jnp.dot(jnp.concatenate(acts, axis=1), wdn_ref[...], preferred_element_type=F32)
        cy_ref[pl.ds(r, size), :] = y.astype(BF16)

    big, mid, small = MOE_CHUNK, MOE_CHUNK // 2, MOE_CHUNK // 4
    n_big = lax.shift_right_logical(rows, big.bit_length() - 1)

    def chunk(c, carry):
        experts(pl.multiple_of(c * big, big), big)
        return carry

    lax.fori_loop(0, n_big, chunk, 0)
    tail = pl.multiple_of(n_big * big, big)
    rem = rows - tail
    pl.when(rem > mid + small)(lambda: experts(tail, big))
    pl.when((rem > small) & (rem <= mid + small))(lambda: experts(tail, mid))
    pl.when((rem > mid) & (rem <= mid + small))(lambda: experts(pl.multiple_of(tail + mid, small), small))
    pl.when((rem > 0) & (rem <= small))(lambda: experts(tail, small))

    def unpack(src, dst):
        ys_ref[pl.ds(src, piece), :] = cy_ref[pl.ds(dst, piece), :]

    for_each_piece(unpack)


def _moe_experts(cnt, hs, gs, lw):
    rows = MOE_BLOCK_TILES * MOE_SORTED_ROWS
    cap = MOE_BLOCK_TILES * MOE_SORT_TILE
    blk = lambda c: pl.BlockSpec((rows, c), lambda j, g, cnt: (j, 0))
    layer = lw["layer"]
    wspec = lambda r, c: pl.BlockSpec((None, EXPERTS_PER_GROUP, r, c), lambda j, g, cnt: (layer, g, 0, 0))
    return pl.pallas_call(
        _moe_expert_kernel,
        grid_spec=pltpu.PrefetchScalarGridSpec(
            num_scalar_prefetch=1, grid=(hs.shape[0] // rows, N_GROUPS),
            in_specs=[blk(D_MODEL), blk(LANES), wspec(D_MODEL, EXPERT_FF), wspec(D_MODEL, EXPERT_FF),
                      pl.BlockSpec((None, EXPERTS_PER_GROUP * EXPERT_FF, D_MODEL),
                                   lambda j, g, cnt: (layer, g, 0))],
            out_specs=blk(D_MODEL),
            scratch_shapes=[pltpu.VMEM((cap, D_MODEL), BF16), pltpu.VMEM((cap, LANES), F32),
                            pltpu.VMEM((cap, D_MODEL), BF16)]),
        out_shape=jax.ShapeDtypeStruct(hs.shape, BF16),
        compiler_params=_params(("arbitrary", "arbitrary")),
        name="moe_experts",
    )(cnt, hs, gs, lw["wga"], lw["wup"], lw["wdn"])


def _moe_unsort_kernel(x_ref, gt_ref, ys_ref, dcol_ref, o_ref):
    cols = lax.broadcasted_iota(jnp.int32, (MOE_SORT_TILE, MOE_SORTED_ROWS), 1).astype(F32)
    for s in range(x_ref.shape[0] // MOE_SORT_TILE):
        tok = pl.ds(s * MOE_SORT_TILE, MOE_SORT_TILE)
        perm_t = jnp.where(cols == dcol_ref[tok, 0:1], 1.0, 0.0).astype(BF16)
        y = jnp.dot(perm_t, ys_ref[pl.ds(s * MOE_SORTED_ROWS, MOE_SORTED_ROWS), :], preferred_element_type=F32)
        o_ref[tok, :] = x_ref[tok, :] + gt_ref[...] * y


def _moe_unsort(xf, gt, ys, dcol, tiles_per_mod):
    N = xf.shape[0]
    sub = MOE_UNSORT_SUBTILES
    T = MOE_SORT_TILE * sub
    per_b = gt.shape[0] > 1
    mod = pl.BlockSpec((None, 1, D_MODEL), lambda t: (t * sub // tiles_per_mod if per_b else 0, 0, 0))
    tok = pl.BlockSpec((T, D_MODEL), lambda t: (t, 0))
    return pl.pallas_call(
        _moe_unsort_kernel,
        grid=(N // T,),
        in_specs=[tok, mod, pl.BlockSpec((sub * MOE_SORTED_ROWS, D_MODEL), lambda t: (t, 0)),
                  pl.BlockSpec((T, LANES), lambda t: (t, 0))],
        out_specs=tok,
        out_shape=jax.ShapeDtypeStruct((N, D_MODEL), F32),
        compiler_params=_params(("parallel",)),
        name="moe_unsort",
    )(xf, gt, ys, dcol)


def _moe(x, sc, sh, gt, lw):
    B, L, _ = x.shape
    N = B * L
    T = MOE_SORT_TILE
    xf = x.reshape(N, D_MODEL)
    hs, gs, dcol, meta = _moe_sort(xf, sc, sh, lw, L // T)

    cnt = meta[:, :N_GROUPS, 0].astype(jnp.int32).reshape(-1)
    ys = _moe_experts(cnt, hs, gs, lw)
    out = _moe_unsort(xf, gt, ys, dcol, L // T)
    return out.reshape(B, L, D_MODEL)


def _block_diag(blocks):
    n, r, c = blocks.shape
    eye = jnp.eye(n, dtype=blocks.dtype)
    return jnp.einsum("nrc,nm->nrmc", blocks, eye).reshape(n * r, n * c)


def _rope_tables(seq_len, dim, width, offset):
    rows = seq_len // GRID_W
    r, col = np.meshgrid(np.arange(rows), np.arange(GRID_W), indexing="ij")
    r = r.reshape(-1).astype(np.float64)
    col = col.reshape(-1).astype(np.float64)
    quarter = dim // 4
    freqs = ROPE_BASE ** (-np.arange(quarter, dtype=np.float64) / quarter)
    ang_r = r[:, None] * freqs
    ang_c = col[:, None] * freqs
    zero = np.zeros_like(ang_r)
    cos = np.cos(np.concatenate([ang_r, ang_r, ang_c, ang_c], axis=-1))
    sin_next = np.concatenate([-np.sin(ang_r), zero, -np.sin(ang_c), zero], axis=-1)
    sin_prev = np.concatenate([zero, np.sin(ang_r), zero, np.sin(ang_c)], axis=-1)

    def place(t, fill):
        return np.pad(t, ((0, 0), (offset, width - offset - dim)), constant_values=fill).astype(np.float32)

    return place(cos, 1.0), place(sin_next, 0.0), place(sin_prev, 0.0)


def _layer_weights(l, w):
    lw = {}
    w_in = w["w_in"][l]
    q_cols = w_in[:, 0:256].reshape(D_MODEL, A_HEADS, A_HEAD_DIM)[:, HEAD_ORDER].reshape(D_MODEL, 256)
    kr_cols = jnp.pad(w_in[:, 1664:1696], ((0, 0), (MLA_NOPE, LANES - MLA_QK)))
    lw["wz"] = jnp.concatenate([q_cols, w_in[:, 256:1664], kr_cols], axis=1).astype(BF16)
    lw["wg"] = w_in[:, 1696:].astype(BF16)
    lw["g1"] = w["norm1_g"][l][None]
    lw["g2"] = w["norm2_g"][l][None]
    lw["gq"] = jnp.tile(w["a_qnorm_g"][l], A_HEADS)[None] * (A_HEAD_DIM ** -0.5 * LOG2_E)
    lw["gk"] = jnp.tile(w["a_knorm_g"][l], A_KV_HEADS)[None]
    lw["sink"] = w["a_sink"][l]
    lw["qlg"] = w["mla_q_lat_norm"][l][None]
    wuq = w["mla_w_uq"][l].reshape(MLA_Q_LORA, MLA_HEADS, MLA_QK)
    lw["wuq"] = jnp.pad(wuq, ((0, 0), (0, 0), (0, LANES - MLA_QK))).reshape(MLA_Q_LORA, MLA_HEADS * LANES).astype(BF16)
    pad_g = lambda g: jnp.tile(jnp.pad(g, (0, LANES - MLA_QK)), MLA_HEADS)[None]
    lw["gmq"] = pad_g(w["mla_qnorm_g"][l]) * (MLA_QK ** -0.5 * LOG2_E)
    lw["gmk"] = pad_g(w["mla_knorm_g"][l])
    lw["kvg"] = w["mla_kv_norm"][l][None]
    wukv = w["mla_w_ukv"][l].reshape(MLA_KV_LORA, MLA_HEADS, MLA_NOPE + MLA_V)
    wk = jnp.pad(wukv[:, :, :MLA_NOPE], ((0, 0), (0, 0), (0, LANES - MLA_NOPE))).reshape(MLA_KV_LORA, MLA_HEADS * LANES)
    wv = wukv[:, :, MLA_NOPE:].reshape(MLA_KV_LORA, MLA_HEADS * MLA_V)
    lw["wukv"] = jnp.concatenate([wk, wv], axis=1).astype(BF16)

    lre = w["s5_lambda_re"][l]
    lim = w["s5_lambda_im"][l]
    dt = jnp.exp(w["s5_log_step"][l])[:, :, None]
    mag = jnp.exp(lre * dt)
    ar, ai = mag * jnp.cos(lim * dt), mag * jnp.sin(lim * dt)
    den = lre * lre + lim * lim
    fr = ((ar - 1.0) * lre + ai * lim) / den
    fi = (ai * lre - (ar - 1.0) * lim) / den
    br, bi = w["s5_b_re"][l], w["s5_b_im"][l]
    bbr = fr[..., None] * br - fi[..., None] * bi
    bbi = fr[..., None] * bi + fi[..., None] * br
    in_map = lambda d: jnp.concatenate(
        [_block_diag(jnp.swapaxes(bbr[d], 1, 2)), _block_diag(jnp.swapaxes(bbi[d], 1, 2))], axis=1).astype(BF16)
    out_map = lambda d: jnp.concatenate(
        [_block_diag(jnp.swapaxes(w["s5_c_re"][l][d], 1, 2)),
         -_block_diag(jnp.swapaxes(w["s5_c_im"][l][d], 1, 2))], axis=0).astype(BF16)
    lw["s5_bf"], lw["s5_bb"] = in_map(0), in_map(1)
    lw["s5_cf"], lw["s5_cb"] = out_map(0), out_map(1)
    coef = jnp.stack([ar[0].reshape(-1), ai[0].reshape(-1), ar[1].reshape(-1), ai[1].reshape(-1)])
    lw["s5_a"] = jnp.broadcast_to(coef[:, None, :], (4, SUBLANES, S5_WIDTH))
    lw["s5_d"] = w["s5_d"][l][None]
    lw["wglu"] = w["s5_w_glu"][l].astype(BF16)

    lw["lru_cw"] = w["lru_conv_w"][l]
    lw["lru_cb"] = w["lru_conv_b"][l][None]
    gate_w = lambda d: jnp.concatenate(
        [_block_diag(w["lru_w_a"][l][d]), _block_diag(w["lru_w_x"][l][d])], axis=1).astype(BF16)
    gate_b = lambda d: jnp.concatenate([w["lru_b_a"][l][d], w["lru_b_x"][l][d]])[None]
    lw["lru_wf"], lw["lru_wb"] = gate_w(0), gate_w(1)
    lw["lru_bf"], lw["lru_bb"] = gate_b(0), gate_b(1)
    sp = jax.nn.softplus(-w["lru_lambda"][l])
    lw["lru_spf"], lw["lru_spb"] = sp[0][None], sp[1][None]

    wb = w["w_branch"][l]
    wb0 = wb[0].reshape(A_HEADS, A_HEAD_DIM, D_MODEL)[HEAD_ORDER].reshape(BRANCH_W, D_MODEL)
    lw["wb"] = jnp.concatenate([wb0[None], wb[1:]], axis=0).astype(BF16)
    lw["wo"] = w["w_out"][l].astype(BF16)

    wr = jnp.pad(jnp.concatenate([w["moe_w_group"][l], w["moe_w_expert"][l]], axis=1),
                 ((0, 0), (0, LANES - N_GROUPS - N_EXPERTS)))
    lw["wr_hi"] = wr.astype(BF16)
    lw["wr_lo"] = (wr - lw["wr_hi"].astype(F32)).astype(BF16)
    lw["br"] = jnp.pad(jnp.concatenate([w["moe_b_group"][l], w["moe_b_expert"][l]]),
                       (0, LANES - N_GROUPS - N_EXPERTS))[None]
    lw["tri"] = jnp.triu(jnp.ones((MOE_SORT_TILE, MOE_SORT_TILE), BF16))
    lw["wga"] = w["moe_w_gate"][l].astype(BF16)
    lw["wup"] = w["moe_w_up"][l].astype(BF16)
    lw["wdn"] = w["moe_w_down"][l].astype(BF16).reshape(N_EXPERTS * EXPERT_FF, D_MODEL)
    return lw


def _rows_to_state(fin, width):
    Bg, n = fin.shape[0], fin.shape[1]
    return jnp.swapaxes(fin, 1, 2).reshape(Bg * SUBLANES, n, width)


def _state_to_rows(state):
    B, n, width = state.shape
    return jnp.swapaxes(state.reshape(B // SUBLANES, SUBLANES, n, width), 1, 2)


def _mix(x, mods, lw, rope_tabs, cache, s5_h0, lru_h0, ctx):
    sh1, sc1, gt1 = mods
    outs = _in_proj(x, sc1, sh1, lw, rope_tabs, ctx)
    q, k, v, s5u_t, lx_t, lg, qm, km, vm = outs[:9]
    if ctx:
        o_a = _attn_a_ctx(q, k, v, lw["sink"][lw["layer"]])
        o_d = _mla_ctx(qm, km, vm)
    else:
        ck, cv, cckv, ckr = cache
        o_a = _attn_a_lat(q, k, v, ck, cv, lw["sink"][lw["layer"]])
        kc, vc = _mla_cache_kv(cckv, ckr, lw)
        o_d = _mla_lat(qm, km, vm, kc, vc)
    yf, yb, s5_fin = _s5_scan(s5u_t, lw, s5_h0)
    hf, hb, lru_fin = _lru_scan(lx_t, lw, lru_h0)
    x = _merge(x, sc1, sh1, gt1, lw, o_a, s5u_t, yf, yb, hf, hb, lg, o_d)
    return x, outs[9:], s5_fin, lru_fin


def kernel(x_prompt, x_sample, cache_attn_k, cache_attn_v, cache_mla_ckv, cache_mla_krope, state_ssm_re, state_ssm_im, state_lru, c, c_ctx, w_mod, b_mod, norm1_g, norm2_g, w_in, a_qnorm_g, a_knorm_g, a_sink, s5_lambda_re, s5_lambda_im, s5_log_step, s5_b_re, s5_b_im, s5_c_re, s5_c_im, s5_d, s5_w_glu, lru_conv_w, lru_conv_b, lru_w_a, lru_b_a, lru_w_x, lru_b_x, lru_lambda, mla_q_lat_norm, mla_w_uq, mla_kv_norm, mla_w_ukv, mla_qnorm_g, mla_knorm_g, w_branch, w_out, moe_w_group, moe_b_group, moe_w_expert, moe_b_expert, moe_w_gate, moe_w_up, moe_w_down):
    w = dict(norm1_g=norm1_g, norm2_g=norm2_g, w_in=w_in, a_qnorm_g=a_qnorm_g, a_knorm_g=a_knorm_g,
             a_sink=a_sink, s5_lambda_re=s5_lambda_re, s5_lambda_im=s5_lambda_im, s5_log_step=s5_log_step,
             s5_b_re=s5_b_re, s5_b_im=s5_b_im, s5_c_re=s5_c_re, s5_c_im=s5_c_im, s5_d=s5_d, s5_w_glu=s5_w_glu,
             lru_conv_w=lru_conv_w, lru_conv_b=lru_conv_b, lru_w_a=lru_w_a, lru_b_a=lru_b_a, lru_w_x=lru_w_x,
             lru_b_x=lru_b_x, lru_lambda=lru_lambda, mla_q_lat_norm=mla_q_lat_norm, mla_w_uq=mla_w_uq,
             mla_kv_norm=mla_kv_norm, mla_w_ukv=mla_w_ukv, mla_qnorm_g=mla_qnorm_g, mla_knorm_g=mla_knorm_g,
             w_branch=w_branch, w_out=w_out, moe_w_group=moe_w_group, moe_b_group=moe_b_group,
             moe_w_expert=moe_w_expert, moe_b_expert=moe_b_expert, moe_w_gate=moe_w_gate, moe_w_up=moe_w_up,
             moe_w_down=moe_w_down)
    B, L, _ = x_prompt.shape
    Bd, Ld, _ = x_sample.shape
    P = cache_attn_k.shape[2]

    cond = jnp.zeros((2 * SUBLANES, D_MODEL), F32).at[:Bd].set(c).at[Bd].set(c_ctx)
    mod = _modulation(cond, w_mod, b_mod[:, None, :])
    mod = mod.reshape(DEPTH, 2 * SUBLANES, 6, D_MODEL)

    rope_a = _rope_tables(Ld, A_HEAD_DIM, A_HEAD_DIM, 0)
    rope_a = tuple(jnp.asarray(np.tile(t, (1, A_HEADS))) for t in rope_a)
    rope_m = tuple(jnp.asarray(t) for t in _rope_tables(Ld, MLA_ROPE, LANES, MLA_NOPE))

    lw_all = jax.vmap(lambda w1: _layer_weights(0, {name: v[None] for name, v in w1.items()}))(w)

    xp, xs = x_prompt, x_sample
    ak_l, av_l, ckv_l, kr_l, sr_l, si_l, lru_l = [], [], [], [], [], [], []
    for l in range(DEPTH):
        lw = dict(lw_all, layer=l)
        lat_mod = [mod[l, :Bd, n][:, None, :] for n in range(6)]
        ctx_mod = [mod[l, Bd:Bd + 1, n][:, None, :] for n in range(6)]

        zs5 = jnp.zeros((B // SUBLANES, 4, SUBLANES, S5_WIDTH), F32)
        zlru = jnp.zeros((B // SUBLANES, 2, SUBLANES, LRU_WIDTH), F32)
        xp, (k32, v32, ckv_n, krp), s5_fin, lru_fin = _mix(xp, ctx_mod[0:3], lw, None, None, zs5, zlru, True)
        xp = _moe(xp, ctx_mod[4], ctx_mod[3], ctx_mod[5], lw)
        ak_l.append(k32.reshape(B, L, A_KV_HEADS, A_HEAD_DIM))
        av_l.append(v32.reshape(B, L, A_KV_HEADS, A_HEAD_DIM))
        ckv_l.append(ckv_n)
        kr_l.append(krp[:, :, MLA_NOPE:MLA_QK])
        fin = _rows_to_state(s5_fin, S5_WIDTH)
        sr_l.append(fin[:, 0::2].reshape(B, 2, S5_GROUPS, S5_STATE))
        si_l.append(fin[:, 1::2].reshape(B, 2, S5_GROUPS, S5_STATE))
        lru_l.append(_rows_to_state(lru_fin, LRU_WIDTH))

        sre = state_ssm_re[:, l].reshape(Bd, 2, S5_WIDTH)
        sim = state_ssm_im[:, l].reshape(Bd, 2, S5_WIDTH)
        s5_h0 = _state_to_rows(jnp.stack([sre[:, 0], sim[:, 0], sre[:, 1], sim[:, 1]], axis=1))
        lru_h0 = _state_to_rows(state_lru[:, l])
        cache = (cache_attn_k[:, l].reshape(Bd, P, 128).astype(BF16),
                 cache_attn_v[:, l].reshape(Bd, P, 128).astype(BF16),
                 cache_mla_ckv[:, l],
                 jnp.pad(cache_mla_krope[:, l], ((0, 0), (0, 0), (MLA_NOPE, LANES - MLA_QK))))
        xs, _, _, _ = _mix(xs, lat_mod[0:3], lw, rope_a + rope_m, cache, s5_h0, lru_h0, False)
        xs = _moe(xs, lat_mod[4], lat_mod[3], lat_mod[5], lw)

    stack = lambda ts: jnp.stack(ts, axis=1)
    return (xp, xs, stack(ak_l), stack(av_l), stack(ckv_l), stack(kr_l), stack(sr_l), stack(si_l), stack(lru_l))
```
